```python
import math
import jax, jax.numpy as jnp
from jax import lax
import numpy as np

D_MODEL = 1024
BATCH = 2
SEQ = 8192
DEPTH = 2

SSM_GROUP = 16
SSM_GROUPS = 40
SSM_WIDTH = SSM_GROUP * SSM_GROUPS
SSM_STATE = 64
ATT_HEADS = 6
ATT_HEAD_DIM = 64
ATT_WIDTH = ATT_HEADS * ATT_HEAD_DIM
MIX_IN_EVEN = SSM_WIDTH + 3 * ATT_WIDTH
MIX_OUT_EVEN = SSM_WIDTH + ATT_WIDTH
DILATED_PATTERNS = ((128, 1), (512, 4), (2048, 16))
ATT_BLK = 128
CONV_WIDTH = D_MODEL
CONV_TAPS = 3
N_GROUPS = 4
EXPERTS_PER_GROUP = 8
N_EXPERTS = N_GROUPS * EXPERTS_PER_GROUP
TOP_K_IN_GROUP = 2
D_EXPERT = 512
MOE_BLK = 256

RMS_EPS = 1e-6
NEG_INF = -1e30
N_EVEN = (DEPTH + 1) // 2
N_ODD = DEPTH // 2

kernel_name = 'hybrid_s5_dilated_attn_shortconv_hier_moe'


def rms_norm(x, g):
    x32 = x.astype(jnp.float32)
    y = x32 * lax.rsqrt(jnp.mean(x32 * x32, axis=-1, keepdims=True) + RMS_EPS)
    return (y * g.astype(jnp.float32)).astype(x.dtype)


def alibi_slopes(n_heads):
    return jnp.asarray([2.0 ** (-8.0 * (h + 1) / n_heads) for h in range(n_heads)], jnp.float32)


def _ssm_combine(e1, e2):
    a1, b1 = e1
    a2, b2 = e2
    return a1 * a2, a2 * b1 + b2


def s5_mixer(u, a_re, a_im, b_re, b_im, c_re, c_im, d_skip, log_step, w_glu, b_glu):
    bsz, s_len, _ = u.shape
    f32 = jnp.float32
    u32 = u.astype(f32).reshape(bsz, s_len, SSM_GROUPS, SSM_GROUP)
    lam = lax.complex(a_re.astype(f32), a_im.astype(f32))
    step = jnp.exp(log_step.astype(f32))[:, None]
    lam_bar = jnp.exp(lam * step)
    b_bar = ((lam_bar - 1.0) / lam)[..., None] * lax.complex(b_re.astype(f32), b_im.astype(f32))
    bu = jnp.einsum('bsgh,gph->bsgp', u32.astype(jnp.complex64), b_bar)
    a = jnp.broadcast_to(lam_bar, (1, s_len) + lam_bar.shape)
    _, states = lax.associative_scan(_ssm_combine, (a, bu), axis=1)
    c = lax.complex(c_re.astype(f32), c_im.astype(f32))
    y = jnp.einsum('bsgp,ghp->bsgh', states, c).real + d_skip.astype(f32).reshape(SSM_GROUPS, SSM_GROUP) * u32
    y = jax.nn.gelu(y.reshape(bsz, s_len, SSM_WIDTH))
    y = y * jax.nn.sigmoid(y @ w_glu.astype(f32) + b_glu.astype(f32))
    return y.astype(u.dtype)


def _to_sub(t, s_pad, dil):
    bsz, s_len, nh, e = t.shape
    t = jnp.pad(t, ((0, 0), (0, s_pad - s_len), (0, 0), (0, 0)))
    return t.reshape(bsz, s_pad // dil, dil, nh, e).transpose(0, 2, 3, 1, 4)


def _from_sub(t, s_len):
    bsz, dil, nh = t.shape[:3]
    rest = t.shape[5:]
    t = t.reshape((bsz, dil, nh, -1) + rest)
    perm = (0, 3, 1, 2) + tuple(range(4, 4 + len(rest)))
    t = t.transpose(perm)
    return t.reshape((bsz, -1, nh) + rest)[:, :s_len]


def dilated_branch(q, k, v, slopes, window, dil):
    bsz, s_len, nh, dh = q.shape
    steps = window // dil
    sub_len = -(-s_len // dil)
    sub_pad = -(-sub_len // ATT_BLK) * ATT_BLK
    s_pad = sub_pad * dil
    nb = sub_pad // ATT_BLK
    qb = _to_sub(q, s_pad, dil).reshape(bsz, dil, nh, nb, ATT_BLK, dh)

    def kv_blocks(t):
        t = jnp.pad(_to_sub(t, s_pad, dil), ((0, 0), (0, 0), (0, 0), (ATT_BLK, 0), (0, 0)))
        t = t.reshape(bsz, dil, nh, nb + 1, ATT_BLK, dh)
        return jnp.concatenate([t[:, :, :, :-1], t[:, :, :, 1:]], axis=4)

    kb, vb = kv_blocks(k), kv_blocks(v)
    s = jnp.einsum('bdhnqe,bdhnke->bdhnqk', qb, kb, preferred_element_type=jnp.float32)
    qi = jnp.arange(ATT_BLK)[:, None]
    kj = jnp.arange(2 * ATT_BLK)[None, :]
    diff = qi + ATT_BLK - kj
    blk_start = (jnp.arange(nb) * ATT_BLK)[:, None, None]
    valid = (diff >= 0) & (diff <= steps) & (blk_start + kj - ATT_BLK >= 0)
    dist = (diff * dil).astype(jnp.float32)
    bias = -slopes[:, None, None, None] * dist[None, None]
    s = jnp.where(valid, s + bias, NEG_INF)
    m = jnp.max(s, axis=-1, keepdims=True)
    p = jnp.where(valid, jnp.exp(s - m), 0.0)
    l = jnp.sum(p, axis=-1)
    o = jnp.einsum('bdhnqk,bdhnke->bdhnqe', p, vb.astype(jnp.float32)) / l[..., None]
    return _from_sub(o, s_len), _from_sub(m[..., 0], s_len), _from_sub(l, s_len)


def dilated_attention(q, k, v):
    slopes = alibi_slopes(ATT_HEADS)
    res = [dilated_branch(q, k, v, slopes, w, d) for (w, d) in DILATED_PATTERNS]
    o_all = jnp.stack([r[0] for r in res])
    m_all = jnp.stack([r[1] for r in res])
    l_all = jnp.stack([r[2] for r in res])
    wts = l_all * jnp.exp(m_all - jnp.max(m_all, axis=0, keepdims=True))
    out = jnp.sum(wts[..., None] * o_all, axis=0) / jnp.sum(wts, axis=0)[..., None]
    return out.astype(q.dtype)


def even_mixer(h, w_in, a_re, a_im, b_re, b_im, c_re, c_im, d_skip, log_step, w_glu, b_glu, q_norm, k_norm, w_out):
    bsz, s_len, _ = h.shape
    proj = h @ w_in
    u = proj[..., :SSM_WIDTH]
    q = proj[..., SSM_WIDTH:SSM_WIDTH + ATT_WIDTH].reshape(bsz, s_len, ATT_HEADS, ATT_HEAD_DIM)
    k = proj[..., SSM_WIDTH + ATT_WIDTH:SSM_WIDTH + 2 * ATT_WIDTH].reshape(bsz, s_len, ATT_HEADS, ATT_HEAD_DIM)
    v = proj[..., SSM_WIDTH + 2 * ATT_WIDTH:].reshape(bsz, s_len, ATT_HEADS, ATT_HEAD_DIM)
    q = rms_norm(q, q_norm) * (ATT_HEAD_DIM ** -0.5)
    k = rms_norm(k, k_norm)
    attn = dilated_attention(q, k, v).reshape(bsz, s_len, ATT_WIDTH)
    ssm = s5_mixer(u, a_re, a_im, b_re, b_im, c_re, c_im, d_skip, log_step, w_glu, b_glu)
    return jnp.concatenate([ssm, attn], axis=-1) @ w_out


def conv_mixer(h, w_in, conv_w, w_out):
    proj = h @ w_in
    b_gate = proj[..., :CONV_WIDTH]
    c_gate = proj[..., CONV_WIDTH:2 * CONV_WIDTH]
    z = proj[..., 2 * CONV_WIDTH:]
    zc = c_gate * z
    zc = lax.conv_general_dilated(zc, conv_w.astype(zc.dtype)[:, None, :], window_strides=(1,),
                                  padding=((CONV_TAPS - 1, 0),), dimension_numbers=('NWC', 'WIO', 'NWC'),
                                  feature_group_count=CONV_WIDTH)
    return (b_gate * zc) @ w_out


def hier_moe(h, w_rg, b_rg, w_re, b_re, w_g, w_u, w_d):
    bsz, s_len, d = h.shape
    n_tok = bsz * s_len
    f32 = jnp.float32
    x2 = h.reshape(n_tok, d)
    g_prob = jax.nn.softmax((x2 @ w_rg).astype(f32) + b_rg.astype(f32), axis=-1)
    g_w, grp = lax.top_k(g_prob, 1)
    grp = grp[:, 0]
    e_logits = ((x2 @ w_re).astype(f32) + b_re.astype(f32)).reshape(n_tok, N_GROUPS, EXPERTS_PER_GROUP)
    e_logits = jnp.take_along_axis(e_logits, grp[:, None, None], axis=1)[:, 0]
    top_p, top_i = lax.top_k(jax.nn.softmax(e_logits, axis=-1), TOP_K_IN_GROUP)
    gate = g_w * top_p / jnp.sum(top_p, axis=-1, keepdims=True)
    expert = grp[:, None] * EXPERTS_PER_GROUP + top_i

    n = n_tok * TOP_K_IN_GROUP
    e_flat = expert.reshape(n).astype(jnp.int32)
    t_flat = jnp.repeat(jnp.arange(n_tok, dtype=jnp.int32), TOP_K_IN_GROUP)
    w_flat = gate.reshape(n)
    order = jnp.argsort(e_flat)
    e_s, t_s, w_s = e_flat[order], t_flat[order], w_flat[order]
    counts = jnp.bincount(e_flat, length=N_EXPERTS)
    starts = jnp.cumsum(counts) - counts
    padded = (counts + MOE_BLK - 1) // MOE_BLK * MOE_BLK
    pends = jnp.cumsum(padded)
    pstarts = pends - padded
    dest = pstarts[e_s] + jnp.arange(n, dtype=jnp.int32) - starts[e_s]
    n_rows = (-(-n // MOE_BLK) + N_EXPERTS) * MOE_BLK
    n_blocks = n_rows // MOE_BLK
    row_tok = jnp.full((n_rows,), n_tok, jnp.int32).at[dest].set(t_s)
    row_w = jnp.zeros((n_rows,), f32).at[dest].set(w_s)
    blk_e = jnp.minimum(jnp.searchsorted(pends, jnp.arange(n_blocks, dtype=jnp.int32) * MOE_BLK, side='right'),
                        N_EXPERTS - 1)
    x_pad = jnp.concatenate([x2, jnp.zeros((1, d), x2.dtype)], axis=0)

    def expert_block(args):
        tok, e = args
        xb = x_pad[tok]
        hb = jax.nn.silu(xb @ w_g[e]) * (xb @ w_u[e])
        return hb @ w_d[e]

    y_rows = lax.map(expert_block, (row_tok.reshape(n_blocks, MOE_BLK), blk_e))
    y = jax.ops.segment_sum(y_rows.reshape(n_rows, d).astype(f32) * row_w[:, None], row_tok,
                            num_segments=n_tok + 1)[:n_tok]
    return y.astype(h.dtype).reshape(bsz, s_len, d)


def setup_inputs(seed: int = 0) -> dict:
    key = jax.random.key(seed)
    ks = jax.random.split(key, 32)
    f32 = jnp.float32
    nrm = lambda k, shape, scale: jax.random.normal(k, shape, f32) * scale
    d = D_MODEL
    a_im_base = jnp.pi * jnp.arange(SSM_STATE, dtype=f32)
    return {
        'x': nrm(ks[0], (BATCH, SEQ, d), 1.0),
        'norm_mix': 1.0 + nrm(ks[1], (DEPTH, d), 0.02),
        'norm_ffn': 1.0 + nrm(ks[2], (DEPTH, d), 0.02),
        'w_in_even': nrm(ks[3], (N_EVEN, d, MIX_IN_EVEN), d ** -0.5),
        'ssm_a_re': -0.5 + nrm(ks[4], (N_EVEN, SSM_GROUPS, SSM_STATE), 0.01),
        'ssm_a_im': a_im_base + nrm(ks[5], (N_EVEN, SSM_GROUPS, SSM_STATE), 0.01),
        'ssm_b_re': nrm(ks[6], (N_EVEN, SSM_GROUPS, SSM_STATE, SSM_GROUP), (2 * SSM_GROUP) ** -0.5),
        'ssm_b_im': nrm(ks[7], (N_EVEN, SSM_GROUPS, SSM_STATE, SSM_GROUP), (2 * SSM_GROUP) ** -0.5),
        'ssm_c_re': nrm(ks[8], (N_EVEN, SSM_GROUPS, SSM_GROUP, SSM_STATE), SSM_STATE ** -0.5),
        'ssm_c_im': nrm(ks[9], (N_EVEN, SSM_GROUPS, SSM_GROUP, SSM_STATE), SSM_STATE ** -0.5),
        'ssm_d': nrm(ks[10], (N_EVEN, SSM_WIDTH), 1.0),
        'ssm_log_step': jax.random.uniform(ks[11], (N_EVEN, SSM_GROUPS), f32, math.log(1e-3), math.log(1e-1)),
        'w_glu': nrm(ks[12], (N_EVEN, SSM_WIDTH, SSM_WIDTH), SSM_WIDTH ** -0.5),
        'b_glu': nrm(ks[13], (N_EVEN, SSM_WIDTH), 0.01),
        'q_norm': 1.0 + nrm(ks[14], (N_EVEN, ATT_HEAD_DIM), 0.02),
        'k_norm': 1.0 + nrm(ks[15], (N_EVEN, ATT_HEAD_DIM), 0.02),
        'w_out_even': nrm(ks[16], (N_EVEN, MIX_OUT_EVEN, d), MIX_OUT_EVEN ** -0.5),
        'w_in_conv': nrm(ks[17], (N_ODD, d, 3 * CONV_WIDTH), d ** -0.5),
        'conv_w': nrm(ks[18], (N_ODD, CONV_TAPS, CONV_WIDTH), CONV_TAPS ** -0.5),
        'w_out_conv': nrm(ks[19], (N_ODD, CONV_WIDTH, d), CONV_WIDTH ** -0.5),
        'w_router_group': nrm(ks[20], (DEPTH, d, N_GROUPS), d ** -0.5),
        'b_router_group': nrm(ks[21], (DEPTH, N_GROUPS), 0.01),
        'w_router_expert': nrm(ks[22], (DEPTH, d, N_EXPERTS), d ** -0.5),
        'b_router_expert': nrm(ks[23], (DEPTH, N_EXPERTS), 0.01),
        'w_expert_gate': nrm(ks[24], (DEPTH, N_EXPERTS, d, D_EXPERT), d ** -0.5),
        'w_expert_up': nrm(ks[25], (DEPTH, N_EXPERTS, d, D_EXPERT), d ** -0.5),
        'w_expert_down': nrm(ks[26], (DEPTH, N_EXPERTS, D_EXPERT, d), D_EXPERT ** -0.5),
    }


def reference(x, norm_mix, norm_ffn, w_in_even, ssm_a_re, ssm_a_im, ssm_b_re, ssm_b_im, ssm_c_re, ssm_c_im,
              ssm_d, ssm_log_step, w_glu, b_glu, q_norm, k_norm, w_out_even, w_in_conv, conv_w, w_out_conv,
              w_router_group, b_router_group, w_router_expert, b_router_expert, w_expert_gate, w_expert_up,
              w_expert_down):
    h = x
    for layer in range(DEPTH):
        hn = rms_norm(h, norm_mix[layer])
        i = layer // 2
        if layer % 2 == 0:
            mix = even_mixer(hn, w_in_even[i], ssm_a_re[i], ssm_a_im[i], ssm_b_re[i], ssm_b_im[i], ssm_c_re[i],
                             ssm_c_im[i], ssm_d[i], ssm_log_step[i], w_glu[i], b_glu[i], q_norm[i], k_norm[i],
                             w_out_even[i])
        else:
            mix = conv_mixer(hn, w_in_conv[i], conv_w[i], w_out_conv[i])
        h = h + mix
        h = h + hier_moe(rms_norm(h, norm_ffn[layer]), w_router_group[layer], b_router_group[layer],
                         w_router_expert[layer], b_router_expert[layer], w_expert_gate[layer],
                         w_expert_up[layer], w_expert_down[layer])
    return h
```

```python
import functools
import math

import jax
import jax.numpy as jnp
from jax import lax
from jax.experimental import pallas as pl
from jax.experimental.pallas import tpu as pltpu

F32 = jnp.float32
BF16 = jnp.bfloat16

D_MODEL = 1024
SSM_GROUP = 16
SSM_GROUPS = 40
SSM_WIDTH = SSM_GROUP * SSM_GROUPS
SSM_STATE = 64
ATT_HEADS = 6
ATT_HEAD_DIM = 64
ATT_WIDTH = ATT_HEADS * ATT_HEAD_DIM
DILATIONS = (1, 4, 16)
ATT_BLK = 128
CONV_TAPS = 3
N_GROUPS = 4
EXPERTS_PER_GROUP = 8
N_EXPERTS = N_GROUPS * EXPERTS_PER_GROUP
D_EXPERT = 512
MOE_BLK = 256
RMS_EPS = 1e-6
NEG_INF = -1e30

LANES = 128
SUBLANES = 8
VMEM_LIMIT = 56 * 1024 * 1024

TOK_TILE = 512
SSM_CHUNK = 16
SSM_GB = 8
ATT_SB = 2048
ROW_TILE = 256


def _cparams(sem):
    return pltpu.CompilerParams(dimension_semantics=sem, vmem_limit_bytes=VMEM_LIMIT)


def _rms(x, gain):
    return x * lax.rsqrt(jnp.mean(x * x, axis=-1, keepdims=True) + RMS_EPS) * gain


def _head_norm(t, gain, bd):
    tt = t * t
    hi = tt.astype(BF16)
    lo = (tt - hi.astype(F32)).astype(BF16)
    ss = jnp.dot(hi, bd, preferred_element_type=F32) + jnp.dot(lo, bd, preferred_element_type=F32)
    return t * lax.rsqrt(ss * (1.0 / ATT_HEAD_DIM) + RMS_EPS) * gain


def _inproj_even_kernel(x_ref, g_ref, w_ref, bd_ref, qn_ref, kn_ref, u_ref, q_ref, k_ref, v_ref):
    hn = _rms(x_ref[...], g_ref[...]).astype(BF16)
    proj = jnp.dot(hn, w_ref[...], preferred_element_type=F32)
    u_ref[...] = proj[:, :SSM_WIDTH].astype(BF16)
    bd = bd_ref[...]
    o = SSM_WIDTH
    q = _head_norm(proj[:, o:o + ATT_WIDTH], qn_ref[...], bd) * (ATT_HEAD_DIM ** -0.5)
    k = _head_norm(proj[:, o + ATT_WIDTH:o + 2 * ATT_WIDTH], kn_ref[...], bd)
    v = proj[:, o + 2 * ATT_WIDTH:o + 3 * ATT_WIDTH]
    for j in range(ATT_WIDTH // LANES):
        q_ref[j] = q[:, j * LANES:(j + 1) * LANES]
        k_ref[j] = k[:, j * LANES:(j + 1) * LANES]
        v_ref[j] = v[:, j * LANES:(j + 1) * LANES]


def _inproj_even(x2, gain, w_in, q_norm, k_norm):
    n_tok = x2.shape[0]
    n_slab = ATT_WIDTH // LANES
    head_of = jnp.arange(ATT_WIDTH) // ATT_HEAD_DIM
    bd = (head_of[:, None] == head_of[None, :]).astype(BF16)
    qn = jnp.tile(q_norm.astype(F32), ATT_HEADS)[None]
    kn = jnp.tile(k_norm.astype(F32), ATT_HEADS)[None]
    full = lambda shape: pl.BlockSpec(shape, lambda i: (0,) * len(shape))
    slab = pl.BlockSpec((n_slab, TOK_TILE, LANES), lambda i: (0, i, 0))
    slab_shape = jax.ShapeDtypeStruct((n_slab, n_tok, LANES), F32)
    return pl.pallas_call(
        _inproj_even_kernel,
        grid=(n_tok // TOK_TILE,),
        in_specs=[pl.BlockSpec((TOK_TILE, D_MODEL), lambda i: (i, 0)), full((1, D_MODEL)),
                  full(w_in.shape), full(bd.shape), full(qn.shape), full(kn.shape)],
        out_specs=[pl.BlockSpec((TOK_TILE, SSM_WIDTH), lambda i: (i, 0)), slab, slab, slab],
        out_shape=[jax.ShapeDtypeStruct((n_tok, SSM_WIDTH), BF16), slab_shape, slab_shape, slab_shape],
        compiler_params=_cparams(("parallel",)),
        name="inproj_even",
    )(x2, gain[None].astype(F32), w_in.astype(BF16), bd, qn, kn)


def _s5_tables(a_re, a_im, b_re, b_im, c_re, c_im, d_skip, log_step):
    f = lambda t: t.astype(F32)
    a_re, a_im, b_re, b_im, c_re, c_im = map(f, (a_re, a_im, b_re, b_im, c_re, c_im))
    L = SSM_CHUNK
    step = jnp.exp(f(log_step))[:, None]
    ks = jnp.arange(L + 1, dtype=F32)[:, None, None]
    mag = jnp.exp(ks * (a_re * step)[None])
    ang = ks * (a_im * step)[None]
    pw_re, pw_im = mag * jnp.cos(ang), mag * jnp.sin(ang)
    nr, ni = pw_re[1] - 1.0, pw_im[1]
    den = a_re * a_re + a_im * a_im
    z_re, z_im = (nr * a_re + ni * a_im) / den, (ni * a_re - nr * a_im) / den
    bb_re = z_re[..., None] * b_re - z_im[..., None] * b_im
    bb_im = z_re[..., None] * b_im + z_im[..., None] * b_re
    lb_re = pw_re[..., None] * bb_re[None] - pw_im[..., None] * bb_im[None]
    lb_im = pw_re[..., None] * bb_im[None] + pw_im[..., None] * bb_re[None]
    kk = jnp.einsum('gop,kgpi->gkio', c_re, lb_re[:L]) - jnp.einsum('gop,kgpi->gkio', c_im, lb_im[:L])
    ti = jnp.arange(L)
    lag = ti[None, :] - ti[:, None]
    m = jnp.where((lag >= 0)[None, :, :, None, None], kk[:, jnp.maximum(lag, 0)], 0.0)
    m = m.transpose(0, 1, 3, 2, 4).reshape(SSM_GROUPS, L * SSM_GROUP, L * SSM_GROUP)
    e_re = lb_re[:L][::-1].transpose(1, 0, 3, 2).reshape(SSM_GROUPS, L * SSM_GROUP, SSM_STATE)
    e_im = lb_im[:L][::-1].transpose(1, 0, 3, 2).reshape(SSM_GROUPS, L * SSM_GROUP, SSM_STATE)
    cl_re = c_re[:, None] * pw_re[1:, :, None, :].transpose(1, 0, 2, 3) - c_im[:, None] * pw_im[1:, :, None, :].transpose(1, 0, 2, 3)
    cl_im = c_re[:, None] * pw_im[1:, :, None, :].transpose(1, 0, 2, 3) + c_im[:, None] * pw_re[1:, :, None, :].transpose(1, 0, 2, 3)
    f_re = cl_re.transpose(0, 3, 1, 2).reshape(SSM_GROUPS, SSM_STATE, L * SSM_GROUP)
    f_im = -cl_im.transpose(0, 3, 1, 2).reshape(SSM_GROUPS, SSM_STATE, L * SSM_GROUP)
    sel = (jnp.arange(SSM_GROUPS)[:, None] % 2 == jnp.arange(2)[None, :]).astype(F32)
    widen_cols = lambda t: (t[:, :, None, :] * sel[:, None, :, None]).reshape(SSM_GROUPS, t.shape[1], 2 * SSM_STATE)
    e_w = jnp.concatenate([widen_cols(e_re), widen_cols(e_im)], axis=-1)
    widen_rows = lambda t: (t[:, None, :, :] * sel[:, :, None, None]).reshape(SSM_GROUPS, 2 * SSM_STATE, t.shape[2])
    f_w = jnp.concatenate([widen_rows(f_re), widen_rows(f_im)], axis=1)
    pair = lambda t: jnp.repeat(t.reshape(SSM_GROUPS // 2, 2 * SSM_STATE), 2, axis=0)
    n_steps = SSM_GROUPS // SSM_GB
    a_r = pair(pw_re[L]).reshape(n_steps, SSM_GB, 2 * SSM_STATE)
    a_i = pair(pw_im[L]).reshape(n_steps, SSM_GB, 2 * SSM_STATE)
    d_vec = jnp.tile(f(d_skip).reshape(SSM_GROUPS, 1, SSM_GROUP), (1, 1, L))
    return m.astype(BF16), e_w.astype(BF16), f_w.astype(BF16), a_r, a_i, d_vec


def _s5_kernel(u_ref, m_ref, e_ref, f_ref, ar_ref, ai_ref, d_ref, y_ref, sr_ref, si_ref, *, n_chunk, pitch):
    half = 2 * SSM_STATE
    for q in range(SSM_GB // 2):
        sl = (jnp.dot(u_ref[2 * q], e_ref[2 * q], preferred_element_type=F32)
              + jnp.dot(u_ref[2 * q + 1], e_ref[2 * q + 1], preferred_element_type=F32))
        for b in range(2):
            base = (2 * q + b) * pitch
            sr_ref[base:base + n_chunk, :] = sl[b * n_chunk:(b + 1) * n_chunk, :half]
            si_ref[base:base + n_chunk, :] = sl[b * n_chunk:(b + 1) * n_chunk, half:]
    ar, ai = ar_ref[0], ai_ref[0]

    def scan_step(c, carry):
        s_re, s_im = carry
        rows = pl.ds(c, SUBLANES, stride=pitch)
        x_re, x_im = sr_ref[rows, :], si_ref[rows, :]
        sr_ref[rows, :] = s_re
        si_ref[rows, :] = s_im
        return ar * s_re - ai * s_im + x_re, ar * s_im + ai * s_re + x_im

    zero = jnp.zeros((SUBLANES, half), F32)
    lax.fori_loop(0, n_chunk, scan_step, (zero, zero), unroll=8)

    for q in range(SSM_GB // 2):
        for b in range(2):
            base = (2 * q + b) * pitch
            sp = jnp.concatenate([sr_ref[base:base + n_chunk, :], si_ref[base:base + n_chunk, :]], axis=1).astype(BF16)
            for s in range(2):
                g = 2 * q + s
                ub = u_ref[g, b * n_chunk:(b + 1) * n_chunk, :]
                y = (jnp.dot(ub, m_ref[g], preferred_element_type=F32)
                     + jnp.dot(sp, f_ref[g], preferred_element_type=F32)
                     + d_ref[g] * ub.astype(F32))
                y_ref[g, b * n_chunk:(b + 1) * n_chunk, :] = y


def _s5_core(u, tables, bsz, s_len):
    assert bsz == 2, "the scan packs (lane pair, batch) into the 8 sublanes of one vreg"
    m, e_w, f_w, a_r, a_i, d_vec = tables
    n_chunk = s_len // SSM_CHUNK
    pitch = n_chunk + SUBLANES
    cw = SSM_CHUNK * SSM_GROUP
    ug = u.reshape(bsz, n_chunk, SSM_CHUNK, SSM_GROUPS, SSM_GROUP).transpose(3, 0, 1, 2, 4)
    ug = ug.reshape(SSM_GROUPS, bsz * n_chunk, cw)
    gblk = lambda last2: pl.BlockSpec((SSM_GB,) + last2, lambda i: (i, 0, 0))
    y = pl.pallas_call(
        functools.partial(_s5_kernel, n_chunk=n_chunk, pitch=pitch),
        grid=(SSM_GROUPS // SSM_GB,),
        in_specs=[gblk((bsz * n_chunk, cw)), gblk((cw, cw)), gblk((cw, cw)), gblk((cw, cw)),
                  pl.BlockSpec((1, SSM_GB, 2 * SSM_STATE), lambda i: (i, 0, 0)),
                  pl.BlockSpec((1, SSM_GB, 2 * SSM_STATE), lambda i: (i, 0, 0)),
                  gblk((1, cw))],
        out_specs=gblk((bsz * n_chunk, cw)),
        out_shape=jax.ShapeDtypeStruct((SSM_GROUPS, bsz * n_chunk, cw), F32),
        scratch_shapes=[pltpu.VMEM((SSM_GB * pitch, 2 * SSM_STATE), F32),
                        pltpu.VMEM((SSM_GB * pitch, 2 * SSM_STATE), F32)],
        compiler_params=_cparams(("parallel",)),
        name="s5_core",
    )(ug, m, e_w, f_w, a_r, a_i, d_vec)
    y = y.reshape(SSM_GROUPS, bsz, n_chunk, SSM_CHUNK, SSM_GROUP).transpose(1, 2, 3, 0, 4)
    return y.reshape(bsz * s_len, SSM_WIDTH)


def _attn_kernel(slope_ref, q_ref, kp_ref, kc_ref, vp_ref, vc_ref, o_ref, kk_ref, vv_ref, m_ref, l_ref, acc_ref):
    slab = pl.program_id(1)
    sb = pl.program_id(2)
    kk_ref[0:ATT_SB, :] = kp_ref[...]
    kk_ref[ATT_SB:2 * ATT_SB, :] = kc_ref[...]
    vv_ref[0:ATT_SB, :] = vp_ref[...]
    vv_ref[ATT_SB:2 * ATT_SB, :] = vc_ref[...]

    lane = lax.broadcasted_iota(jnp.int32, (ATT_BLK, LANES), 1)
    head0 = lane < ATT_HEAD_DIM
    qi = lax.broadcasted_iota(jnp.int32, (ATT_BLK, 2 * ATT_BLK), 0)
    kj = lax.broadcasted_iota(jnp.int32, (ATT_BLK, 2 * ATT_BLK), 1)
    back = qi + ATT_BLK - kj
    band = (back >= 0) & (back <= ATT_BLK)
    neg_steps = -back.astype(F32)
    slopes = (slope_ref[2 * slab], slope_ref[2 * slab + 1])

    for pat, dil in enumerate(DILATIONS):
        span = ATT_BLK * dil

        def tile(idx, carry, dil=dil, span=span, pat=pat):
            start = (idx // dil) * span + idx % dil
            seq_ok = jnp.logical_or(sb > 0, idx >= dil)
            valid = band & ((kj >= ATT_BLK) | seq_ok)
            if dil == 1:
                start = pl.multiple_of(start, ATT_BLK)
                rows_of = lambda first, n: pl.ds(first, n)
            else:
                rows_of = lambda first, n: pl.ds(first, n, stride=dil)
            rows = rows_of(start, ATT_BLK)
            qt = q_ref[rows, :]
            kt = kk_ref[rows_of(ATT_SB + start - span, 2 * ATT_BLK), :].astype(BF16)
            vt = vv_ref[rows_of(ATT_SB + start - span, 2 * ATT_BLK), :].astype(BF16)
            q0 = jnp.where(head0, qt, 0.0)
            parts = []
            for hh, qh in enumerate((q0, qt - q0)):
                s = lax.dot_general(qh.astype(BF16), kt, (((1,), (1,)), ((), ())), preferred_element_type=F32)
                s = jnp.where(valid, s + (slopes[hh] * float(dil)) * neg_steps, NEG_INF)
                m = jnp.max(s, axis=-1, keepdims=True)
                p = jnp.exp(s - m)
                l = jnp.sum(p, axis=-1, keepdims=True)
                o = jnp.dot(p.astype(BF16), vt, preferred_element_type=F32)
                parts.append((m, l, o))
            (m0, l0, o0), (m1, l1, o1) = parts
            m_t = jnp.where(head0, m0, m1)
            l_t = jnp.where(head0, l0, l1)
            o_t = jnp.where(head0, o0, o1)
            if pat == 0:
                m_ref[rows, :] = m_t
                l_ref[rows, :] = l_t
                acc_ref[rows, :] = o_t
            else:
                m_old = m_ref[rows, :]
                m_new = jnp.maximum(m_old, m_t)
                a = jnp.exp(m_old - m_new)
                b = jnp.exp(m_t - m_new)
                m_ref[rows, :] = m_new
                l_ref[rows, :] = a * l_ref[rows, :] + b * l_t
                acc_ref[rows, :] = a * acc_ref[rows, :] + b * o_t
            return carry

        lax.fori_loop(0, ATT_SB // ATT_BLK, tile, 0)

    o_ref[...] = (acc_ref[...] / l_ref[...]).astype(o_ref.dtype)


def _dilated_attention(q, k, v, bsz, s_len):
    n_slab = q.shape[0]
    shape4 = (n_slab, bsz, s_len, LANES)
    q, k, v = (t.reshape(shape4) for t in (q, k, v))
    slopes = jnp.asarray([2.0 ** (-8.0 * (h + 1) / ATT_HEADS) for h in range(ATT_HEADS)], F32)
    blk = (None, None, ATT_SB, LANES)
    cur = pl.BlockSpec(blk, lambda b, j, i, s: (j, b, i, 0))
    prev = pl.BlockSpec(blk, lambda b, j, i, s: (j, b, jnp.maximum(i - 1, 0), 0))
    out = pl.pallas_call(
        _attn_kernel,
        grid_spec=pltpu.PrefetchScalarGridSpec(
            num_scalar_prefetch=1,
            grid=(bsz, n_slab, s_len // ATT_SB),
            in_specs=[cur, prev, cur, prev, cur],
            out_specs=cur,
            scratch_shapes=[pltpu.VMEM((2 * ATT_SB, LANES), F32), pltpu.VMEM((2 * ATT_SB, LANES), F32),
                            pltpu.VMEM((ATT_SB, LANES), F32), pltpu.VMEM((ATT_SB, LANES), F32),
                            pltpu.VMEM((ATT_SB, LANES), F32)]),
        out_shape=jax.ShapeDtypeStruct(shape4, BF16),
        compiler_params=_cparams(("parallel", "parallel", "parallel")),
        name="dilated_attn",
    )(slopes, q, k, k, v, v)
    return out.reshape(n_slab, bsz * s_len, LANES)


def _route_epilogue(h, gain_ref, wr_ref, br_ref, tri_ref, cnt_ref, h_ref, hn_ref, ids_ref, gate_ref, cnt_out_ref):
    tm = h.shape[0]
    h_ref[...] = h
    hn = _rms(h, gain_ref[...])
    hn_ref[...] = hn
    logits = jnp.dot(hn, wr_ref[...], preferred_element_type=F32, precision=lax.Precision.HIGHEST) + br_ref[...]
    lane = lax.broadcasted_iota(jnp.int32, (tm, LANES), 1)
    big = jnp.int32(LANES)
    rmax = lambda t: jnp.max(t, axis=-1, keepdims=True)
    rmin = lambda t: jnp.min(t, axis=-1, keepdims=True)
    rsum = lambda t: jnp.sum(t, axis=-1, keepdims=True)
    gmask = lane < N_GROUPS
    gl = jnp.where(gmask, logits, -jnp.inf)
    gmax = rmax(gl)
    ge = jnp.where(gmask, jnp.exp(gl - gmax), 0.0)
    gprob = ge / rsum(ge)
    g_w = rmax(gprob)
    grp = rmin(jnp.where(gmask & (gprob == g_w), lane, big))
    group_of_lane = (lane - N_GROUPS) >> int(math.log2(EXPERTS_PER_GROUP))
    emask = (lane >= N_GROUPS) & (lane < N_GROUPS + N_EXPERTS) & (group_of_lane == grp)
    el = jnp.where(emask, logits, -jnp.inf)
    ee = jnp.where(emask, jnp.exp(el - rmax(el)), 0.0)
    ep = jnp.where(emask, ee / rsum(ee), -1.0)
    p1 = rmax(ep)
    i1 = rmin(jnp.where(ep == p1, lane, big))
    ep2 = jnp.where(lane == i1, -1.0, ep)
    p2 = rmax(ep2)
    i2 = rmin(jnp.where(ep2 == p2, lane, big))
    e1, e2 = i1 - N_GROUPS, i2 - N_GROUPS
    psum = p1 + p2
    gate1, gate2 = g_w * p1 / psum, g_w * p2 / psum
    oh1, oh2 = lane == e1, lane == e2
    member = (oh1 | oh2).astype(BF16)
    before = jnp.dot(tri_ref[...], member, preferred_element_type=F32) + cnt_ref[...]
    r1 = rsum(jnp.where(oh1, before, 0.0)).astype(jnp.int32)
    r2 = rsum(jnp.where(oh2, before, 0.0)).astype(jnp.int32)
    cnt_ref[...] = cnt_ref[...] + jnp.sum(member.astype(F32), axis=0, keepdims=True)
    ids_ref[...] = jnp.where(lane == 0, e1, jnp.where(lane == 1, e2, jnp.where(lane == 2, r1, jnp.where(lane == 3, r2, 0))))
    gate_ref[...] = jnp.where(lane == 0, gate1, jnp.where(lane == 1, gate2, 0.0))
    cnt_out_ref[...] = jnp.broadcast_to(cnt_ref[...], cnt_out_ref.shape)


def _router_operands(norm_gain, w_rg, b_rg, w_re, b_re):
    pad = LANES - N_GROUPS - N_EXPERTS
    wr = jnp.pad(jnp.concatenate([w_rg, w_re], axis=1).astype(F32), ((0, 0), (0, pad)))
    br = jnp.pad(jnp.concatenate([b_rg, b_re]).astype(F32), (0, pad))[None]
    r = jnp.arange(TOK_TILE)
    tri = (r[None, :] < r[:, None]).astype(BF16)
    return norm_gain[None].astype(F32), wr, br, tri


def _route_specs(n_tok):
    full = lambda shape: pl.BlockSpec(shape, lambda i: (0,) * len(shape))
    in_specs = [full((1, D_MODEL)), full((D_MODEL, LANES)), full((1, LANES)), full((TOK_TILE, TOK_TILE))]
    tok = lambda w: pl.BlockSpec((TOK_TILE, w), lambda i: (i, 0))
    out_specs = [tok(D_MODEL), tok(D_MODEL), tok(LANES), tok(LANES), full((SUBLANES, LANES))]
    out_shape = [jax.ShapeDtypeStruct((n_tok, D_MODEL), F32), jax.ShapeDtypeStruct((n_tok, D_MODEL), F32),
                 jax.ShapeDtypeStruct((n_tok, LANES), jnp.int32), jax.ShapeDtypeStruct((n_tok, LANES), F32),
                 jax.ShapeDtypeStruct((SUBLANES, LANES), F32)]
    return in_specs, out_specs, out_shape


def _gelu_tanh(x):
    return 0.5 * x * (1.0 + jnp.tanh(math.sqrt(2.0 / math.pi) * (x + 0.044715 * (x * x * x))))


def _outproj_even_kernel(x_ref, y_ref, a_ref, wglu_ref, bglu_ref, wout_ref, gain_ref, wr_ref, br_ref, tri_ref,
                         h_ref, hn_ref, ids_ref, gate_ref, cnt_out_ref, cnt_ref):
    @pl.when(pl.program_id(0) == 0)
    def _():
        cnt_ref[...] = jnp.zeros_like(cnt_ref)

    y = _gelu_tanh(y_ref[...])
    y = y * jax.nn.sigmoid(jnp.dot(y.astype(BF16), wglu_ref[...], preferred_element_type=F32) + bglu_ref[...])
    mix = jnp.dot(y.astype(BF16), wout_ref[0:SSM_WIDTH, :], preferred_element_type=F32)
    for j in range(ATT_WIDTH // LANES):
        rows = slice(SSM_WIDTH + j * LANES, SSM_WIDTH + (j + 1) * LANES)
        mix = mix + jnp.dot(a_ref[j], wout_ref[rows, :], preferred_element_type=F32)
    _route_epilogue(x_ref[...] + mix, gain_ref, wr_ref, br_ref, tri_ref, cnt_ref,
                    h_ref, hn_ref, ids_ref, gate_ref, cnt_out_ref)


def _outproj_even(x2, y_pre, attn, w_glu, b_glu, w_out, route_ops):
    n_tok = x2.shape[0]
    n_slab = attn.shape[0]
    r_in, r_out, r_shape = _route_specs(n_tok)
    full = lambda shape: pl.BlockSpec(shape, lambda i: (0,) * len(shape))
    return pl.pallas_call(
        _outproj_even_kernel,
        grid=(n_tok // TOK_TILE,),
        in_specs=[pl.BlockSpec((TOK_TILE, D_MODEL), lambda i: (i, 0)),
                  pl.BlockSpec((TOK_TILE, SSM_WIDTH), lambda i: (i, 0)),
                  pl.BlockSpec((n_slab, TOK_TILE, LANES), lambda i: (0, i, 0)),
                  full(w_glu.shape), full((1, SSM_WIDTH)), full(w_out.shape)] + r_in,
        out_specs=r_out, out_shape=r_shape,
        scratch_shapes=[pltpu.VMEM((1, LANES), F32)],
        compiler_params=_cparams(("arbitrary",)),
        name="outproj_even",
    )(x2, y_pre, attn, w_glu.astype(BF16), b_glu[None].astype(F32), w_out.astype(BF16), *route_ops)


def _conv_layer_kernel(h_ref, gmix_ref, win_ref, cw_ref, wout_ref, gain_ref, wr_ref, br_ref, tri_ref,
                       ho_ref, hn_ref, ids_ref, gate_ref, cnt_out_ref, cnt_ref, zc_ref, *, tiles_per_seq):
    i = pl.program_id(0)

    @pl.when(i == 0)
    def _():
        cnt_ref[...] = jnp.zeros_like(cnt_ref)

    @pl.when(i % tiles_per_seq == 0)
    def _():
        zc_ref[0:SUBLANES, :] = jnp.zeros((SUBLANES, D_MODEL), F32)

    h = h_ref[...]
    tm = h.shape[0]
    hn = _rms(h, gmix_ref[...]).astype(BF16)
    c = D_MODEL
    b_gate = jnp.dot(hn, win_ref[:, 0:c], preferred_element_type=F32)
    zc = jnp.dot(hn, win_ref[:, c:2 * c], preferred_element_type=F32) * jnp.dot(hn, win_ref[:, 2 * c:3 * c], preferred_element_type=F32)
    zc_ref[SUBLANES:SUBLANES + tm, :] = zc
    conv = cw_ref[CONV_TAPS - 1:CONV_TAPS, :] * zc
    for back in range(1, CONV_TAPS):
        tap = CONV_TAPS - 1 - back
        conv = conv + cw_ref[tap:tap + 1, :] * zc_ref[SUBLANES - back:SUBLANES - back + tm, :]
    zc_ref[0:SUBLANES, :] = zc_ref[tm:tm + SUBLANES, :]
    mix = jnp.dot((b_gate * conv).astype(BF16), wout_ref[...], preferred_element_type=F32)
    _route_epilogue(h + mix, gain_ref, wr_ref, br_ref, tri_ref, cnt_ref,
                    ho_ref, hn_ref, ids_ref, gate_ref, cnt_out_ref)


def _conv_layer(h2, gain_mix, w_in, conv_w, w_out, route_ops, s_len):
    n_tok = h2.shape[0]
    r_in, r_out, r_shape = _route_specs(n_tok)
    full = lambda shape: pl.BlockSpec(shape, lambda i: (0,) * len(shape))
    return pl.pallas_call(
        functools.partial(_conv_layer_kernel, tiles_per_seq=s_len // TOK_TILE),
        grid=(n_tok // TOK_TILE,),
        in_specs=[pl.BlockSpec((TOK_TILE, D_MODEL), lambda i: (i, 0)), full((1, D_MODEL)),
                  full(w_in.shape), full(conv_w.shape), full(w_out.shape)] + r_in,
        out_specs=r_out, out_shape=r_shape,
        scratch_shapes=[pltpu.VMEM((1, LANES), F32), pltpu.VMEM((TOK_TILE + SUBLANES, D_MODEL), F32)],
        compiler_params=_cparams(("arbitrary",)),
        name="conv_layer",
    )(h2, gain_mix[None].astype(F32), w_in.astype(BF16), conv_w.astype(F32), w_out.astype(BF16), *route_ops)


def _dispatch_kernel(dest_ref, hn_ref, zero_ref, xs_ref, sem):
    del zero_ref
    base = pl.program_id(0) * ROW_TILE

    def issue(r, carry):
        for slot in range(2):
            d = dest_ref[2 * (base + r) + slot]
            pltpu.make_async_copy(hn_ref.at[pl.ds(r, 1)], xs_ref.at[pl.ds(d, 1)], sem).start()
        return carry

    lax.fori_loop(0, ROW_TILE, issue, 0)
    for _ in range(2):
        pltpu.make_async_copy(hn_ref, hn_ref, sem).wait()


def _dispatch(hn, dest_flat, n_rows):
    n_tok = hn.shape[0]
    xs0 = jnp.zeros((n_rows, D_MODEL), F32)
    return pl.pallas_call(
        _dispatch_kernel,
        grid_spec=pltpu.PrefetchScalarGridSpec(
            num_scalar_prefetch=1,
            grid=(n_tok // ROW_TILE,),
            in_specs=[pl.BlockSpec((ROW_TILE, D_MODEL), lambda i, d: (i, 0)),
                      pl.BlockSpec(memory_space=pltpu.HBM)],
            out_specs=pl.BlockSpec(memory_space=pltpu.HBM),
            scratch_shapes=[pltpu.SemaphoreType.DMA]),
        out_shape=jax.ShapeDtypeStruct((n_rows, D_MODEL), F32),
        input_output_aliases={2: 0},
        compiler_params=_cparams(("arbitrary",)),
        name="moe_dispatch",
    )(dest_flat, hn, xs0)


def _expert_kernel(blk_e_ref, n_used_ref, xs_ref, wg_ref, wu_ref, wd_ref, y_ref, wg_s, wu_s, wd_s):
    b = pl.program_id(0)

    @pl.when(b < n_used_ref[0])
    def _():
        new_expert = jnp.logical_or(b == 0, blk_e_ref[b] != blk_e_ref[jnp.maximum(b - 1, 0)])

        @pl.when(new_expert)
        def _():
            wg_s[...] = wg_ref[0].astype(BF16)
            wu_s[...] = wu_ref[0].astype(BF16)
            wd_s[...] = wd_ref[0].astype(BF16)

        x = xs_ref[...].astype(BF16)
        g = jnp.dot(x, wg_s[...], preferred_element_type=F32)
        u = jnp.dot(x, wu_s[...], preferred_element_type=F32)
        hb = (g * jax.nn.sigmoid(g) * u).astype(BF16)
        y_ref[...] = jnp.dot(hb, wd_s[...], preferred_element_type=F32)

    @pl.when(b >= n_used_ref[0])
    def _():
        y_ref[...] = jnp.zeros_like(y_ref)


def _expert_ffn(xs, blk_e, n_used, w_g, w_u, w_d):
    n_rows = xs.shape[0]
    last = lambda b, n: jnp.minimum(b, n[0] - 1)
    return pl.pallas_call(
        _expert_kernel,
        grid_spec=pltpu.PrefetchScalarGridSpec(
            num_scalar_prefetch=2,
            grid=(n_rows // MOE_BLK,),
            in_specs=[pl.BlockSpec((MOE_BLK, D_MODEL), lambda b, e, n: (last(b, n), 0)),
                      pl.BlockSpec((1, D_MODEL, D_EXPERT), lambda b, e, n: (e[last(b, n)], 0, 0)),
                      pl.BlockSpec((1, D_MODEL, D_EXPERT), lambda b, e, n: (e[last(b, n)], 0, 0)),
                      pl.BlockSpec((1, D_EXPERT, D_MODEL), lambda b, e, n: (e[last(b, n)], 0, 0))],
            out_specs=pl.BlockSpec((MOE_BLK, D_MODEL), lambda b, e, n: (b, 0)),
            scratch_shapes=[pltpu.VMEM((D_MODEL, D_EXPERT), BF16), pltpu.VMEM((D_MODEL, D_EXPERT), BF16),
                            pltpu.VMEM((D_EXPERT, D_MODEL), BF16)]),
        out_shape=jax.ShapeDtypeStruct((n_rows, D_MODEL), F32),
        compiler_params=_cparams(("arbitrary",)),
        name="moe_experts",
    )(blk_e, n_used, xs, w_g, w_u, w_d)


def _combine_kernel(dest_ref, h_ref, gate_ref, y_ref, o_ref, buf_ref, sem):
    base = pl.program_id(0) * ROW_TILE

    def issue(r, carry):
        for slot in range(2):
            d = dest_ref[2 * (base + r) + slot]
            pltpu.make_async_copy(y_ref.at[pl.ds(d, 1)], buf_ref.at[slot, pl.ds(r, 1)], sem).start()
        return carry

    lax.fori_loop(0, ROW_TILE, issue, 0)
    for slot in range(2):
        pltpu.make_async_copy(buf_ref.at[slot], buf_ref.at[slot], sem).wait()
    gate = gate_ref[...]
    o_ref[...] = h_ref[...] + gate[:, 0:1] * buf_ref[0] + gate[:, 1:2] * buf_ref[1]


def _combine(h, gate, y_rows, dest_flat):
    n_tok = h.shape[0]
    return pl.pallas_call(
        _combine_kernel,
        grid_spec=pltpu.PrefetchScalarGridSpec(
            num_scalar_prefetch=1,
            grid=(n_tok // ROW_TILE,),
            in_specs=[pl.BlockSpec((ROW_TILE, D_MODEL), lambda i, d: (i, 0)),
                      pl.BlockSpec((ROW_TILE, LANES), lambda i, d: (i, 0)),
                      pl.BlockSpec(memory_space=pltpu.HBM)],
            out_specs=pl.BlockSpec((ROW_TILE, D_MODEL), lambda i, d: (i, 0)),
            scratch_shapes=[pltpu.VMEM((2, ROW_TILE, D_MODEL), F32), pltpu.SemaphoreType.DMA]),
        out_shape=jax.ShapeDtypeStruct((n_tok, D_MODEL), F32),
        compiler_params=_cparams(("arbitrary",)),
        name="moe_combine",
    )(dest_flat, h, gate, y_rows)


def _moe(h, hn, ids, gate, counts, w_g, w_u, w_d):
    n_tok = h.shape[0]
    n_assign = 2 * n_tok
    n_blocks = n_assign // MOE_BLK + N_EXPERTS
    n_rows = n_blocks * MOE_BLK
    cnt = counts[0, :N_EXPERTS].astype(jnp.int32)
    padded = (cnt + MOE_BLK - 1) // MOE_BLK * MOE_BLK
    pends = jnp.cumsum(padded)
    pstarts = pends - padded
    dest = (pstarts[ids[:, 0:2]] + ids[:, 2:4]).reshape(n_assign)
    blk_e = jnp.minimum(jnp.searchsorted(pends, jnp.arange(n_blocks, dtype=jnp.int32) * MOE_BLK, side='right'),
                        N_EXPERTS - 1).astype(jnp.int32)
    n_used = (pends[-1:] // MOE_BLK).astype(jnp.int32)
    xs = _dispatch(hn, dest, n_rows)
    y_rows = _expert_ffn(xs, blk_e, n_used, w_g, w_u, w_d)
    return _combine(h, gate, y_rows, dest)


def kernel(x, norm_mix, norm_ffn, w_in_even, ssm_a_re, ssm_a_im, ssm_b_re, ssm_b_im, ssm_c_re, ssm_c_im, ssm_d,
           ssm_log_step, w_glu, b_glu, q_norm, k_norm, w_out_even, w_in_conv, conv_w, w_out_conv, w_router_group,
           b_router_group, w_router_expert, b_router_expert, w_expert_gate, w_expert_up, w_expert_down):
    bsz, s_len, d = x.shape
    x2 = x.reshape(bsz * s_len, d)
    route = lambda layer: _router_operands(norm_ffn[layer], w_router_group[layer], b_router_group[layer],
                                           w_router_expert[layer], b_router_expert[layer])
    experts = lambda layer: (w_expert_gate[layer], w_expert_up[layer], w_expert_down[layer])

    u, q, k, v = _inproj_even(x2, norm_mix[0], w_in_even[0], q_norm[0], k_norm[0])
    tables = _s5_tables(ssm_a_re[0], ssm_a_im[0], ssm_b_re[0], ssm_b_im[0], ssm_c_re[0], ssm_c_im[0], ssm_d[0],
                        ssm_log_step[0])
    y_pre = _s5_core(u, tables, bsz, s_len)
    attn = _dilated_attention(q, k, v, bsz, s_len)
    h, hn, ids, gate, counts = _outproj_even(x2, y_pre, attn, w_glu[0], b_glu[0], w_out_even[0], route(0))
    h = _moe(h, hn, ids, gate, counts, *experts(0))

    h, hn, ids, gate, counts = _conv_layer(h, norm_mix[1], w_in_conv[0], conv_w[0], w_out_conv[0], route(1), s_len)
    h = _moe(h, hn, ids, gate, counts, *experts(1))
    return h.reshape(bsz, s_len, d)
```

```python
import functools
import math

import jax
import jax.numpy as jnp
from jax import lax
from jax.experimental import pallas as pl
from jax.experimental.pallas import tpu as pltpu

F32 = jnp.float32
BF16 = jnp.bfloat16

D_MODEL = 1024
SSM_GROUP = 16
SSM_GROUPS = 40
SSM_WIDTH = SSM_GROUP * SSM_GROUPS
SSM_STATE = 64
ATT_HEADS = 6
ATT_HEAD_DIM = 64
ATT_WIDTH = ATT_HEADS * ATT_HEAD_DIM
DILATIONS = (1, 4, 16)
ATT_BLK = 128
CONV_TAPS = 3
N_GROUPS = 4
EXPERTS_PER_GROUP = 8
N_EXPERTS = N_GROUPS * EXPERTS_PER_GROUP
D_EXPERT = 512
MOE_BLK = 256
RMS_EPS = 1e-6
NEG_INF = -1e30

LANES = 128
SUBLANES = 8
VMEM_LIMIT = 56 * 1024 * 1024

TOK_TILE = 512
SSM_CHUNK = 8
SSM_SLAB_GROUPS = LANES // SSM_GROUP
SSM_SLABS = SSM_WIDTH // LANES
S5_ROW_TILE = 256
ATT_SB = 2048
ROW_TILE = 256


def _cparams(sem):
    return pltpu.CompilerParams(dimension_semantics=sem, vmem_limit_bytes=VMEM_LIMIT)


def _rms(x, gain):
    return x * lax.rsqrt(jnp.mean(x * x, axis=-1, keepdims=True) + RMS_EPS) * gain


def _head_norm(t, gain, bd):
    tt = t * t
    hi = tt.astype(BF16)
    lo = (tt - hi.astype(F32)).astype(BF16)
    ss = jnp.dot(hi, bd, preferred_element_type=F32) + jnp.dot(lo, bd, preferred_element_type=F32)
    return t * lax.rsqrt(ss * (1.0 / ATT_HEAD_DIM) + RMS_EPS) * gain


def _inproj_even_kernel(x_ref, g_ref, w_ref, bd_ref, qn_ref, kn_ref, u_ref, q_ref, k_ref, v_ref):
    hn = _rms(x_ref[...], g_ref[...]).astype(BF16)
    proj = jnp.dot(hn, w_ref[...], preferred_element_type=F32)
    for j in range(SSM_SLABS):
        u_ref[j] = proj[:, j * LANES:(j + 1) * LANES]
    bd = bd_ref[...]
    o = SSM_WIDTH
    q = _head_norm(proj[:, o:o + ATT_WIDTH], qn_ref[...], bd) * (ATT_HEAD_DIM ** -0.5)
    k = _head_norm(proj[:, o + ATT_WIDTH:o + 2 * ATT_WIDTH], kn_ref[...], bd)
    v = proj[:, o + 2 * ATT_WIDTH:o + 3 * ATT_WIDTH]
    for j in range(ATT_WIDTH // LANES):
        q_ref[j] = q[:, j * LANES:(j + 1) * LANES]
        k_ref[j] = k[:, j * LANES:(j + 1) * LANES]
        v_ref[j] = v[:, j * LANES:(j + 1) * LANES]


def _inproj_even(x2, gain, w_in, q_norm, k_norm):
    n_tok = x2.shape[0]
    n_slab = ATT_WIDTH // LANES
    head_of = jnp.arange(ATT_WIDTH) // ATT_HEAD_DIM
    bd = (head_of[:, None] == head_of[None, :]).astype(BF16)
    qn = jnp.tile(q_norm.astype(F32), ATT_HEADS)[None]
    kn = jnp.tile(k_norm.astype(F32), ATT_HEADS)[None]
    full = lambda shape: pl.BlockSpec(shape, lambda i: (0,) * len(shape))
    slab = pl.BlockSpec((n_slab, TOK_TILE, LANES), lambda i: (0, i, 0))
    slab_shape = jax.ShapeDtypeStruct((n_slab, n_tok, LANES), F32)
    return pl.pallas_call(
        _inproj_even_kernel,
        grid=(n_tok // TOK_TILE,),
        in_specs=[pl.BlockSpec((TOK_TILE, D_MODEL), lambda i: (i, 0)), full((1, D_MODEL)),
                  full(w_in.shape), full(bd.shape), full(qn.shape), full(kn.shape)],
        out_specs=[pl.BlockSpec((SSM_SLABS, TOK_TILE, LANES), lambda i: (0, i, 0)), slab, slab, slab],
        out_shape=[jax.ShapeDtypeStruct((SSM_SLABS, n_tok, LANES), F32), slab_shape, slab_shape, slab_shape],
        compiler_params=_cparams(("parallel",)),
        name="inproj_even",
    )(x2, gain[None].astype(F32), w_in.astype(BF16), bd, qn, kn)


def _s5_tables(a_re, a_im, b_re, b_im, c_re, c_im, d_skip, log_step):
    f = lambda t: t.astype(F32)
    a_re, a_im, b_re, b_im, c_re, c_im = map(f, (a_re, a_im, b_re, b_im, c_re, c_im))
    L = SSM_CHUNK
    step = jnp.exp(f(log_step))[:, None]
    ks = jnp.arange(L + 1, dtype=F32)[:, None, None]
    mag = jnp.exp(ks * (a_re * step)[None])
    ang = ks * (a_im * step)[None]
    pw_re, pw_im = mag * jnp.cos(ang), mag * jnp.sin(ang)
    nr, ni = pw_re[1] - 1.0, pw_im[1]
    den = a_re * a_re + a_im * a_im
    z_re, z_im = (nr * a_re + ni * a_im) / den, (ni * a_re - nr * a_im) / den
    bb_re = z_re[..., None] * b_re - z_im[..., None] * b_im
    bb_im = z_re[..., None] * b_im + z_im[..., None] * b_re
    lb_re = pw_re[..., None] * bb_re[None] - pw_im[..., None] * bb_im[None]
    lb_im = pw_re[..., None] * bb_im[None] + pw_im[..., None] * bb_re[None]
    kk = jnp.einsum('gop,kgpi->gkio', c_re, lb_re[:L]) - jnp.einsum('gop,kgpi->gkio', c_im, lb_im[:L])
    ti = jnp.arange(L)
    lag = ti[None, :] - ti[:, None]
    m = jnp.where((lag >= 0)[None, :, :, None, None], kk[:, jnp.maximum(lag, 0)], 0.0)
    ns, gs = SSM_SLABS, SSM_SLAB_GROUPS
    eye = jnp.eye(gs, dtype=F32)
    lw = L * LANES
    m = jnp.einsum('sgabio,gh->sagibho', m.reshape(ns, gs, L, L, SSM_GROUP, SSM_GROUP), eye).reshape(ns, lw, lw)
    fold_e = lambda t: jnp.einsum('tsgpi,gh->stgihp', t[:L][::-1].reshape(L, ns, gs, SSM_STATE, SSM_GROUP),
                                  eye).reshape(ns, lw, gs * SSM_STATE)
    e = jnp.concatenate([fold_e(lb_re), fold_e(lb_im)], axis=-1)
    pw1_re, pw1_im = (t[1:].transpose(1, 0, 2)[:, :, None, :] for t in (pw_re, pw_im))
    cl_re = c_re[:, None] * pw1_re - c_im[:, None] * pw1_im
    cl_im = c_re[:, None] * pw1_im + c_im[:, None] * pw1_re
    fold_f = lambda t: jnp.einsum('sgtop,gh->shptgo', t.reshape(ns, gs, L, SSM_GROUP, SSM_STATE),
                                  eye).reshape(ns, gs * SSM_STATE, lw)
    fm = jnp.concatenate([fold_f(cl_re), fold_f(-cl_im)], axis=1)
    per_chain = lambda t: jnp.tile(t.reshape(ns, gs * SSM_STATE // LANES, LANES), (1, 2, 1))
    d_vec = jnp.tile(f(d_skip).reshape(ns, 1, LANES), (1, 1, L))
    return m.astype(BF16), e.astype(BF16), fm.astype(BF16), per_chain(pw_re[L]), per_chain(pw_im[L]), d_vec


def _s5_kernel(u_ref, m_ref, e_ref, f_ref, ar_ref, ai_ref, d_ref, y_ref, x_ref, sr_ref, si_ref, *, n_chunk, pitch):
    L = SSM_CHUNK
    n_blk = SSM_SLAB_GROUPS * SSM_STATE // LANES
    n_re = n_blk * LANES
    tiles = [(b, c0) for b in range(2) for c0 in range(0, n_chunk, S5_ROW_TILE)]
    for b, c0 in tiles:
        r0 = b * n_chunk + c0
        for t in range(L):
            x_ref[r0:r0 + S5_ROW_TILE, t * LANES:(t + 1) * LANES] = (
                u_ref[pl.ds(r0 * L + t, S5_ROW_TILE, stride=L), :].astype(BF16))
        sl = jnp.dot(x_ref[r0:r0 + S5_ROW_TILE, :], e_ref[...], preferred_element_type=F32)
        for j in range(n_blk):
            base = (b * n_blk + j) * pitch + c0
            sr_ref[base:base + S5_ROW_TILE, :] = sl[:, j * LANES:(j + 1) * LANES]
            si_ref[base:base + S5_ROW_TILE, :] = sl[:, n_re + j * LANES:n_re + (j + 1) * LANES]
    ar, ai = ar_ref[...], ai_ref[...]
    half = LANES

    def scan_step(c, carry):
        s_re, s_im = carry
        rows = pl.ds(c, SUBLANES, stride=pitch)
        x_re, x_im = sr_ref[rows, :], si_ref[rows, :]
        sr_ref[rows, :] = s_re
        si_ref[rows, :] = s_im
        return ar * s_re - ai * s_im + x_re, ar * s_im + ai * s_re + x_im

    zero = jnp.zeros((SUBLANES, half), F32)
    lax.fori_loop(0, n_chunk, scan_step, (zero, zero), unroll=8)

    for b, c0 in tiles:
        r0 = b * n_chunk + c0
        chain = lambda ref, j: ref[(b * n_blk + j) * pitch + c0:(b * n_blk + j) * pitch + c0 + S5_ROW_TILE, :]
        sp = jnp.concatenate([chain(sr_ref, j) for j in range(n_blk)] + [chain(si_ref, j) for j in range(n_blk)],
                             axis=1).astype(BF16)
        xt = x_ref[r0:r0 + S5_ROW_TILE, :]
        y = (jnp.dot(xt, m_ref[...], preferred_element_type=F32)
             + jnp.dot(sp, f_ref[...], preferred_element_type=F32)
             + d_ref[...] * xt.astype(F32))
        for t in range(L):
            y_ref[pl.ds(r0 * L + t, S5_ROW_TILE, stride=L), :] = y[:, t * LANES:(t + 1) * LANES]


def _s5_core(u, tables, bsz, s_len):
    assert bsz == 2, "the scan packs (batch, lane block) into the 8 sublanes of one vreg"
    m, e, fm, a_r, a_i, d_vec = tables
    n_tok = bsz * s_len
    n_chunk = s_len // SSM_CHUNK
    pitch = n_chunk + SUBLANES
    lw = SSM_CHUNK * LANES
    slab = lambda shape, **kw: pl.BlockSpec((None,) + shape, lambda i: (i,) + (0,) * len(shape), **kw)
    once = dict(pipeline_mode=pl.Buffered(1))
    return pl.pallas_call(
        functools.partial(_s5_kernel, n_chunk=n_chunk, pitch=pitch),
        grid=(SSM_SLABS,),
        in_specs=[slab((n_tok, LANES), **once), slab((lw, lw)), slab(e.shape[1:]), slab(fm.shape[1:]),
                  slab((SUBLANES, LANES)), slab((SUBLANES, LANES)), slab((1, lw))],
        out_specs=slab((n_tok, LANES), **once),
        out_shape=jax.ShapeDtypeStruct((SSM_SLABS, n_tok, LANES), F32),
        scratch_shapes=[pltpu.VMEM((bsz * n_chunk, lw), BF16),
                        pltpu.VMEM((SUBLANES * pitch, LANES), F32),
                        pltpu.VMEM((SUBLANES * pitch, LANES), F32)],
        compiler_params=_cparams(("parallel",)),
        name="s5_core",
    )(u, m, e, fm, a_r, a_i, d_vec)


def _attn_kernel(slope_ref, q_ref, kp_ref, kc_ref, vp_ref, vc_ref, o_ref, kk_ref, vv_ref, m_ref, l_ref, acc_ref):
    slab = pl.program_id(1)
    sb = pl.program_id(2)
    kk_ref[0:ATT_SB, :] = kp_ref[...]
    kk_ref[ATT_SB:2 * ATT_SB, :] = kc_ref[...]
    vv_ref[0:ATT_SB, :] = vp_ref[...]
    vv_ref[ATT_SB:2 * ATT_SB, :] = vc_ref[...]

    lane = lax.broadcasted_iota(jnp.int32, (ATT_BLK, LANES), 1)
    head0 = lane < ATT_HEAD_DIM
    qi = lax.broadcasted_iota(jnp.int32, (ATT_BLK, 2 * ATT_BLK), 0)
    kj = lax.broadcasted_iota(jnp.int32, (ATT_BLK, 2 * ATT_BLK), 1)
    back = qi + ATT_BLK - kj
    band = (back >= 0) & (back <= ATT_BLK)
    neg_steps = -back.astype(F32)
    slopes = (slope_ref[2 * slab], slope_ref[2 * slab + 1])

    for pat, dil in enumerate(DILATIONS):
        span = ATT_BLK * dil

        def tile(idx, carry, dil=dil, span=span, pat=pat):
            start = (idx // dil) * span + idx % dil
            seq_ok = jnp.logical_or(sb > 0, idx >= dil)
            valid = band & ((kj >= ATT_BLK) | seq_ok)
            if dil == 1:
                start = pl.multiple_of(start, ATT_BLK)
                rows_of = lambda first, n: pl.ds(first, n)
            else:
                rows_of = lambda first, n: pl.ds(first, n, stride=dil)
            rows = rows_of(start, ATT_BLK)
            qt = q_ref[rows, :]
            kt = kk_ref[rows_of(ATT_SB + start - span, 2 * ATT_BLK), :].astype(BF16)
            vt = vv_ref[rows_of(ATT_SB + start - span, 2 * ATT_BLK), :].astype(BF16)
            q0 = jnp.where(head0, qt, 0.0)
            parts = []
            for hh, qh in enumerate((q0, qt - q0)):
                s = lax.dot_general(qh.astype(BF16), kt, (((1,), (1,)), ((), ())), preferred_element_type=F32)
                s = jnp.where(valid, s + (slopes[hh] * float(dil)) * neg_steps, NEG_INF)
                m = jnp.max(s, axis=-1, keepdims=True)
                p = jnp.exp(s - m)
                l = jnp.sum(p, axis=-1, keepdims=True)
                o = jnp.dot(p.astype(BF16), vt, preferred_element_type=F32)
                parts.append((m, l, o))
            (m0, l0, o0), (m1, l1, o1) = parts
            m_t = jnp.where(head0, m0, m1)
            l_t = jnp.where(head0, l0, l1)
            o_t = jnp.where(head0, o0, o1)
            if pat == 0:
                m_ref[rows, :] = m_t
                l_ref[rows, :] = l_t
                acc_ref[rows, :] = o_t
            else:
                m_old = m_ref[rows, :]
                m_new = jnp.maximum(m_old, m_t)
                a = jnp.exp(m_old - m_new)
                b = jnp.exp(m_t - m_new)
                m_ref[rows, :] = m_new
                l_ref[rows, :] = a * l_ref[rows, :] + b * l_t
                acc_ref[rows, :] = a * acc_ref[rows, :] + b * o_t
            return carry

        lax.fori_loop(0, ATT_SB // ATT_BLK, tile, 0, unroll=4)

    o_ref[...] = (acc_ref[...] / l_ref[...]).astype(o_ref.dtype)


def _dilated_attention(q, k, v, bsz, s_len):
    n_slab = q.shape[0]
    shape4 = (n_slab, bsz, s_len, LANES)
    q, k, v = (t.reshape(shape4) for t in (q, k, v))
    slopes = jnp.asarray([2.0 ** (-8.0 * (h + 1) / ATT_HEADS) for h in range(ATT_HEADS)], F32)
    blk = (None, None, ATT_SB, LANES)
    cur = pl.BlockSpec(blk, lambda b, j, i, s: (j, b, i, 0))
    prev = pl.BlockSpec(blk, lambda b, j, i, s: (j, b, jnp.maximum(i - 1, 0), 0))
    out = pl.pallas_call(
        _attn_kernel,
        grid_spec=pltpu.PrefetchScalarGridSpec(
            num_scalar_prefetch=1,
            grid=(bsz, n_slab, s_len // ATT_SB),
            in_specs=[cur, prev, cur, prev, cur],
            out_specs=cur,
            scratch_shapes=[pltpu.VMEM((2 * ATT_SB, LANES), F32), pltpu.VMEM((2 * ATT_SB, LANES), F32),
                            pltpu.VMEM((ATT_SB, LANES), F32), pltpu.VMEM((ATT_SB, LANES), F32),
                            pltpu.VMEM((ATT_SB, LANES), F32)]),
        out_shape=jax.ShapeDtypeStruct(shape4, BF16),
        compiler_params=_cparams(("parallel", "parallel", "parallel")),
        name="dilated_attn",
    )(slopes, q, k, k, v, v)
    return out.reshape(n_slab, bsz * s_len, LANES)


def _route_epilogue(h, gain_ref, wr_ref, br_ref, tri_ref, cnt_ref, h_ref, hn_ref, ids_ref, gate_ref, cnt_out_ref):
    tm = h.shape[0]
    h_ref[...] = h
    hn = _rms(h, gain_ref[...])
    hn_ref[...] = hn
    hn_hi = hn.astype(BF16)
    hn_lo = (hn - hn_hi.astype(F32)).astype(BF16)
    logits = (jnp.dot(hn_hi, wr_ref[0], preferred_element_type=F32)
              + (jnp.dot(hn_hi, wr_ref[1], preferred_element_type=F32)
                 + jnp.dot(hn_lo, wr_ref[0], preferred_element_type=F32))) + br_ref[...]
    lane = lax.broadcasted_iota(jnp.int32, (tm, LANES), 1)
    big = jnp.int32(LANES)
    rmax = lambda t: jnp.max(t, axis=-1, keepdims=True)
    rmin = lambda t: jnp.min(t, axis=-1, keepdims=True)
    rsum = lambda t: jnp.sum(t, axis=-1, keepdims=True)
    gmask = lane < N_GROUPS
    gl = jnp.where(gmask, logits, -jnp.inf)
    gmax = rmax(gl)
    ge = jnp.where(gmask, jnp.exp(gl - gmax), 0.0)
    gprob = ge / rsum(ge)
    g_w = rmax(gprob)
    grp = rmin(jnp.where(gmask & (gprob == g_w), lane, big))
    group_of_lane = (lane - N_GROUPS) >> int(math.log2(EXPERTS_PER_GROUP))
    emask = (lane >= N_GROUPS) & (lane < N_GROUPS + N_EXPERTS) & (group_of_lane == grp)
    el = jnp.where(emask, logits, -jnp.inf)
    ee = jnp.where(emask, jnp.exp(el - rmax(el)), 0.0)
    ep = jnp.where(emask, ee / rsum(ee), -1.0)
    p1 = rmax(ep)
    i1 = rmin(jnp.where(ep == p1, lane, big))
    ep2 = jnp.where(lane == i1, -1.0, ep)
    p2 = rmax(ep2)
    i2 = rmin(jnp.where(ep2 == p2, lane, big))
    e1, e2 = i1 - N_GROUPS, i2 - N_GROUPS
    psum = p1 + p2
    gate1, gate2 = g_w * p1 / psum, g_w * p2 / psum
    oh1, oh2 = lane == e1, lane == e2
    member = (oh1 | oh2).astype(BF16)
    before = jnp.dot(tri_ref[...], member, preferred_element_type=F32) + cnt_ref[...]
    r1 = rsum(jnp.where(oh1, before, 0.0)).astype(jnp.int32)
    r2 = rsum(jnp.where(oh2, before, 0.0)).astype(jnp.int32)
    cnt_ref[...] = cnt_ref[...] + jnp.sum(member.astype(F32), axis=0, keepdims=True)
    ids_ref[...] = jnp.where(lane == 0, e1, jnp.where(lane == 1, e2, jnp.where(lane == 2, r1, jnp.where(lane == 3, r2, 0))))
    gate_ref[...] = jnp.where(lane == 0, gate1, jnp.where(lane == 1, gate2, 0.0))
    cnt_out_ref[...] = jnp.broadcast_to(cnt_ref[...], cnt_out_ref.shape)


def _router_operands(norm_gain, w_rg, b_rg, w_re, b_re):
    pad = LANES - N_GROUPS - N_EXPERTS
    wr = jnp.pad(jnp.concatenate([w_rg, w_re], axis=1).astype(F32), ((0, 0), (0, pad)))
    br = jnp.pad(jnp.concatenate([b_rg, b_re]).astype(F32), (0, pad))[None]
    r = jnp.arange(TOK_TILE)
    tri = (r[None, :] < r[:, None]).astype(BF16)
    wr_hi = wr.astype(BF16)
    wr_lo = (wr - wr_hi.astype(F32)).astype(BF16)
    return norm_gain[None].astype(F32), jnp.stack([wr_hi, wr_lo]), br, tri


def _route_specs(n_tok):
    full = lambda shape: pl.BlockSpec(shape, lambda i: (0,) * len(shape))
    in_specs = [full((1, D_MODEL)), full((2, D_MODEL, LANES)), full((1, LANES)), full((TOK_TILE, TOK_TILE))]
    tok = lambda w: pl.BlockSpec((TOK_TILE, w), lambda i: (i, 0))
    out_specs = [tok(D_MODEL), tok(D_MODEL), tok(LANES), tok(LANES), full((SUBLANES, LANES))]
    out_shape = [jax.ShapeDtypeStruct((n_tok, D_MODEL), F32), jax.ShapeDtypeStruct((n_tok, D_MODEL), F32),
                 jax.ShapeDtypeStruct((n_tok, LANES), jnp.int32), jax.ShapeDtypeStruct((n_tok, LANES), F32),
                 jax.ShapeDtypeStruct((SUBLANES, LANES), F32)]
    return in_specs, out_specs, out_shape


def _gelu_tanh(x):
    return 0.5 * x * (1.0 + jnp.tanh(math.sqrt(2.0 / math.pi) * (x + 0.044715 * (x * x * x))))


def _outproj_even_kernel(x_ref, y_ref, a_ref, wglu_ref, bglu_ref, wout_ref, gain_ref, wr_ref, br_ref, tri_ref,
                         h_ref, hn_ref, ids_ref, gate_ref, cnt_out_ref, cnt_ref):
    @pl.when(pl.program_id(0) == 0)
    def _():
        cnt_ref[...] = jnp.zeros_like(cnt_ref)

    y = _gelu_tanh(jnp.concatenate([y_ref[j] for j in range(SSM_SLABS)], axis=1))
    y = y * jax.nn.sigmoid(jnp.dot(y.astype(BF16), wglu_ref[...], preferred_element_type=F32) + bglu_ref[...])
    mix = jnp.dot(y.astype(BF16), wout_ref[0:SSM_WIDTH, :], preferred_element_type=F32)
    for j in range(ATT_WIDTH // LANES):
        rows = slice(SSM_WIDTH + j * LANES, SSM_WIDTH + (j + 1) * LANES)
        mix = mix + jnp.dot(a_ref[j], wout_ref[rows, :], preferred_element_type=F32)
    _route_epilogue(x_ref[...] + mix, gain_ref, wr_ref, br_ref, tri_ref, cnt_ref,
                    h_ref, hn_ref, ids_ref, gate_ref, cnt_out_ref)


def _outproj_even(x2, y_pre, attn, w_glu, b_glu, w_out, route_ops):
    n_tok = x2.shape[0]
    n_slab = attn.shape[0]
    r_in, r_out, r_shape = _route_specs(n_tok)
    full = lambda shape: pl.BlockSpec(shape, lambda i: (0,) * len(shape))
    return pl.pallas_call(
        _outproj_even_kernel,
        grid=(n_tok // TOK_TILE,),
        in_specs=[pl.BlockSpec((TOK_TILE, D_MODEL), lambda i: (i, 0)),
                  pl.BlockSpec((SSM_SLABS, TOK_TILE, LANES), lambda i: (0, i, 0)),
                  pl.BlockSpec((n_slab, TOK_TILE, LANES), lambda i: (0, i, 0)),
                  full(w_glu.shape), full((1, SSM_WIDTH)), full(w_out.shape)] + r_in,
        out_specs=r_out, out_shape=r_shape,
        scratch_shapes=[pltpu.VMEM((1, LANES), F32)],
        compiler_params=_cparams(("arbitrary",)),
        name="outproj_even",
    )(x2, y_pre, attn, w_glu.astype(BF16), b_glu[None].astype(F32), w_out.astype(BF16), *route_ops)


def _conv_layer_kernel(h_ref, gmix_ref, win_ref, cw_ref, wout_ref, gain_ref, wr_ref, br_ref, tri_ref,
                       ho_ref, hn_ref, ids_ref, gate_ref, cnt_out_ref, cnt_ref, zc_ref, *, tiles_per_seq):
    i = pl.program_id(0)

    @pl.when(i == 0)
    def _():
        cnt_ref[...] = jnp.zeros_like(cnt_ref)

    @pl.when(i % tiles_per_seq == 0)
    def _():
        zc_ref[0:SUBLANES, :] = jnp.zeros((SUBLANES, D_MODEL), F32)

    h = h_ref[...]
    tm = h.shape[0]
    hn = _rms(h, gmix_ref[...]).astype(BF16)
    c = D_MODEL
    b_gate = jnp.dot(hn, win_ref[:, 0:c], preferred_element_type=F32)
    zc = jnp.dot(hn, win_ref[:, c:2 * c], preferred_element_type=F32) * jnp.dot(hn, win_ref[:, 2 * c:3 * c], preferred_element_type=F32)
    zc_ref[SUBLANES:SUBLANES + tm, :] = zc
    conv = cw_ref[CONV_TAPS - 1:CONV_TAPS, :] * zc
    for back in range(1, CONV_TAPS):
        tap = CONV_TAPS - 1 - back
        conv = conv + cw_ref[tap:tap + 1, :] * zc_ref[SUBLANES - back:SUBLANES - back + tm, :]
    zc_ref[0:SUBLANES, :] = zc_ref[tm:tm + SUBLANES, :]
    mix = jnp.dot((b_gate * conv).astype(BF16), wout_ref[...], preferred_element_type=F32)
    _route_epilogue(h + mix, gain_ref, wr_ref, br_ref, tri_ref, cnt_ref,
                    ho_ref, hn_ref, ids_ref, gate_ref, cnt_out_ref)


def _conv_layer(h2, gain_mix, w_in, conv_w, w_out, route_ops, s_len):
    n_tok = h2.shape[0]
    r_in, r_out, r_shape = _route_specs(n_tok)
    full = lambda shape: pl.BlockSpec(shape, lambda i: (0,) * len(shape))
    return pl.pallas_call(
        functools.partial(_conv_layer_kernel, tiles_per_seq=s_len // TOK_TILE),
        grid=(n_tok // TOK_TILE,),
        in_specs=[pl.BlockSpec((TOK_TILE, D_MODEL), lambda i: (i, 0)), full((1, D_MODEL)),
                  full(w_in.shape), full(conv_w.shape), full(w_out.shape)] + r_in,
        out_specs=r_out, out_shape=r_shape,
        scratch_shapes=[pltpu.VMEM((1, LANES), F32), pltpu.VMEM((TOK_TILE + SUBLANES, D_MODEL), F32)],
        compiler_params=_cparams(("arbitrary",)),
        name="conv_layer",
    )(h2, gain_mix[None].astype(F32), w_in.astype(BF16), conv_w.astype(F32), w_out.astype(BF16), *route_ops)


def _dispatch_kernel(dest_ref, hn_ref, zero_ref, xs_ref, sem):
    del zero_ref
    base = pl.program_id(0) * ROW_TILE

    def issue(r, carry):
        for slot in range(2):
            d = dest_ref[2 * (base + r) + slot]
            pltpu.make_async_copy(hn_ref.at[pl.ds(r, 1)], xs_ref.at[pl.ds(d, 1)], sem).start()
        return carry

    lax.fori_loop(0, ROW_TILE, issue, 0)
    for _ in range(2):
        pltpu.make_async_copy(hn_ref, hn_ref, sem).wait()


def _dispatch(hn, dest_flat, n_rows):
    n_tok = hn.shape[0]
    xs0 = jnp.zeros((n_rows, D_MODEL), F32)
    return pl.pallas_call(
        _dispatch_kernel,
        grid_spec=pltpu.PrefetchScalarGridSpec(
            num_scalar_prefetch=1,
            grid=(n_tok // ROW_TILE,),
            in_specs=[pl.BlockSpec((ROW_TILE, D_MODEL), lambda i, d: (i, 0)),
                      pl.BlockSpec(memory_space=pltpu.HBM)],
            out_specs=pl.BlockSpec(memory_space=pltpu.HBM),
            scratch_shapes=[pltpu.SemaphoreType.DMA]),
        out_shape=jax.ShapeDtypeStruct((n_rows, D_MODEL), F32),
        input_output_aliases={2: 0},
        compiler_params=_cparams(("arbitrary",)),
        name="moe_dispatch",
    )(dest_flat, hn, xs0)


def _expert_kernel(blk_e_ref, n_used_ref, xs_ref, wg_ref, wu_ref, wd_ref, y_ref, wg_s, wu_s, wd_s):
    b = pl.program_id(0)

    @pl.when(b < n_used_ref[0])
    def _():
        new_expert = jnp.logical_or(b == 0, blk_e_ref[b] != blk_e_ref[jnp.maximum(b - 1, 0)])

        @pl.when(new_expert)
        def _():
            wg_s[...] = wg_ref[0].astype(BF16)
            wu_s[...] = wu_ref[0].astype(BF16)
            wd_s[...] = wd_ref[0].astype(BF16)

        x = xs_ref[...].astype(BF16)
        g = jnp.dot(x, wg_s[...], preferred_element_type=F32)
        u = jnp.dot(x, wu_s[...], preferred_element_type=F32)
        hb = (g * jax.nn.sigmoid(g) * u).astype(BF16)
        y_ref[...] = jnp.dot(hb, wd_s[...], preferred_element_type=F32)

    @pl.when(b >= n_used_ref[0])
    def _():
        y_ref[...] = jnp.zeros_like(y_ref)


def _expert_ffn(xs, blk_e, n_used, w_g, w_u, w_d):
    n_rows = xs.shape[0]
    last = lambda b, n: jnp.minimum(b, n[0] - 1)
    return pl.pallas_call(
        _expert_kernel,
        grid_spec=pltpu.PrefetchScalarGridSpec(
            num_scalar_prefetch=2,
            grid=(n_rows // MOE_BLK,),
            in_specs=[pl.BlockSpec((MOE_BLK, D_MODEL), lambda b, e, n: (last(b, n), 0)),
                      pl.BlockSpec((1, D_MODEL, D_EXPERT), lambda b, e, n: (e[last(b, n)], 0, 0)),
                      pl.BlockSpec((1, D_MODEL, D_EXPERT), lambda b, e, n: (e[last(b, n)], 0, 0)),
                      pl.BlockSpec((1, D_EXPERT, D_MODEL), lambda b, e, n: (e[last(b, n)], 0, 0))],
            out_specs=pl.BlockSpec((MOE_BLK, D_MODEL), lambda b, e, n: (b, 0)),
            scratch_shapes=[pltpu.VMEM((D_MODEL, D_EXPERT), BF16), pltpu.VMEM((D_MODEL, D_EXPERT), BF16),
                            pltpu.VMEM((D_EXPERT, D_MODEL), BF16)]),
        out_shape=jax.ShapeDtypeStruct((n_rows, D_MODEL), F32),
        compiler_params=_cparams(("arbitrary",)),
        name="moe_experts",
    )(blk_e, n_used, xs, w_g, w_u, w_d)


def _combine_kernel(dest_ref, h_ref, gate_ref, y_ref, o_ref, buf_ref, sem):
    base = pl.program_id(0) * ROW_TILE

    def issue(r, carry):
        for slot in range(2):
            d = dest_ref[2 * (base + r) + slot]
            pltpu.make_async_copy(y_ref.at[pl.ds(d, 1)], buf_ref.at[slot, pl.ds(r, 1)], sem).start()
        return carry

    lax.fori_loop(0, ROW_TILE, issue, 0)
    for slot in range(2):
        pltpu.make_async_copy(buf_ref.at[slot], buf_ref.at[slot], sem).wait()
    gate = gate_ref[...]
    o_ref[...] = h_ref[...] + gate[:, 0:1] * buf_ref[0] + gate[:, 1:2] * buf_ref[1]


def _combine(h, gate, y_rows, dest_flat):
    n_tok = h.shape[0]
    return pl.pallas_call(
        _combine_kernel,
        grid_spec=pltpu.PrefetchScalarGridSpec(
            num_scalar_prefetch=1,
            grid=(n_tok // ROW_TILE,),
            in_specs=[pl.BlockSpec((ROW_TILE, D_MODEL), lambda i, d: (i, 0)),
                      pl.BlockSpec((ROW_TILE, LANES), lambda i, d: (i, 0)),
                      pl.BlockSpec(memory_space=pltpu.HBM)],
            out_specs=pl.BlockSpec((ROW_TILE, D_MODEL), lambda i, d: (i, 0)),
            scratch_shapes=[pltpu.VMEM((2, ROW_TILE, D_MODEL), F32), pltpu.SemaphoreType.DMA]),
        out_shape=jax.ShapeDtypeStruct((n_tok, D_MODEL), F32),
        compiler_params=_cparams(("arbitrary",)),
        name="moe_combine",
    )(dest_flat, h, gate, y_rows)


def _moe(h, hn, ids, gate, counts, w_g, w_u, w_d):
    n_tok = h.shape[0]
    n_assign = 2 * n_tok
    n_blocks = n_assign // MOE_BLK + N_EXPERTS
    n_rows = n_blocks * MOE_BLK
    cnt = counts[0, :N_EXPERTS].astype(jnp.int32)
    padded = (cnt + MOE_BLK - 1) // MOE_BLK * MOE_BLK
    pends = jnp.cumsum(padded)
    pstarts = pends - padded
    dest = (pstarts[ids[:, 0:2]] + ids[:, 2:4]).reshape(n_assign)
    blk_start = jnp.arange(n_blocks, dtype=jnp.int32) * MOE_BLK
    blk_e = jnp.minimum(jnp.sum(pends[None, :] <= blk_start[:, None], axis=1), N_EXPERTS - 1).astype(jnp.int32)
    n_used = (pends[-1:] // MOE_BLK).astype(jnp.int32)
    xs = _dispatch(hn, dest, n_rows)
    y_rows = _expert_ffn(xs, blk_e, n_used, w_g, w_u, w_d)
    return _combine(h, gate, y_rows, dest)


def kernel(x, norm_mix, norm_ffn, w_in_even, ssm_a_re, ssm_a_im, ssm_b_re, ssm_b_im, ssm_c_re, ssm_c_im, ssm_d,
           ssm_log_step, w_glu, b_glu, q_norm, k_norm, w_out_even, w_in_conv, conv_w, w_out_conv, w_router_group,
           b_router_group, w_router_expert, b_router_expert, w_expert_gate, w_expert_up, w_expert_down):
    bsz, s_len, d = x.shape
    x2 = x.reshape(bsz * s_len, d)
    route = lambda layer: _router_operands(norm_ffn[layer], w_router_group[layer], b_router_group[layer],
                                           w_router_expert[layer], b_router_expert[layer])
    experts = lambda layer: (w_expert_gate[layer], w_expert_up[layer], w_expert_down[layer])

    u, q, k, v = _inproj_even(x2, norm_mix[0], w_in_even[0], q_norm[0], k_norm[0])
    tables = _s5_tables(ssm_a_re[0], ssm_a_im[0], ssm_b_re[0], ssm_b_im[0], ssm_c_re[0], ssm_c_im[0], ssm_d[0],
                        ssm_log_step[0])
    y_pre = _s5_core(u, tables, bsz, s_len)
    attn = _dilated_attention(q, k, v, bsz, s_len)
    h, hn, ids, gate, counts = _outproj_even(x2, y_pre, attn, w_glu[0], b_glu[0], w_out_even[0], route(0))
    h = _moe(h, hn, ids, gate, counts, *experts(0))

    h, hn, ids, gate, counts = _conv_layer(h, norm_mix[1], w_in_conv[0], conv_w[0], w_out_conv[0], route(1), s_len)
    h = _moe(h, hn, ids, gate, counts, *experts(1))
    return h.reshape(bsz, s_len, d)
```

```python
import functools
import math

import jax
import jax.numpy as jnp
from jax import lax
from jax.experimental import pallas as pl
from jax.experimental.pallas import tpu as pltpu

F32 = jnp.float32
BF16 = jnp.bfloat16

D_MODEL = 1024
SSM_GROUP = 16
SSM_GROUPS = 40
SSM_WIDTH = SSM_GROUP * SSM_GROUPS
SSM_STATE = 64
ATT_HEADS = 6
ATT_HEAD_DIM = 64
ATT_WIDTH = ATT_HEADS * ATT_HEAD_DIM
DILATIONS = (1, 4, 16)
ATT_BLK = 128
CONV_TAPS = 3
N_GROUPS = 4
EXPERTS_PER_GROUP = 8
N_EXPERTS = N_GROUPS * EXPERTS_PER_GROUP
D_EXPERT = 512
MOE_BLK = 256
RMS_EPS = 1e-6
NEG_INF = -1e30

LANES = 128
SUBLANES = 8
VMEM_LIMIT = 56 * 1024 * 1024

TOK_TILE = 512
SSM_CHUNK = 8
SSM_SLAB_GROUPS = LANES // SSM_GROUP
SSM_SLABS = SSM_WIDTH // LANES
S5_ROW_TILE = 256
ATT_SB = 2048
ROW_TILE = 256


def _cparams(sem):
    return pltpu.CompilerParams(dimension_semantics=sem, vmem_limit_bytes=VMEM_LIMIT)


def _rms(x, gain):
    return x * lax.rsqrt(jnp.mean(x * x, axis=-1, keepdims=True) + RMS_EPS) * gain


def _head_norm(t, gain, bd):
    tt = t * t
    hi = tt.astype(BF16)
    lo = (tt - hi.astype(F32)).astype(BF16)
    ss = jnp.dot(hi, bd, preferred_element_type=F32) + jnp.dot(lo, bd, preferred_element_type=F32)
    return t * lax.rsqrt(ss * (1.0 / ATT_HEAD_DIM) + RMS_EPS) * gain


def _inproj_even_kernel(x_ref, g_ref, w_ref, bd_ref, qn_ref, kn_ref, u_ref, q_ref, k_ref, v_ref):
    hn = _rms(x_ref[...], g_ref[...]).astype(BF16)
    proj = jnp.dot(hn, w_ref[...], preferred_element_type=F32)
    for j in range(SSM_SLABS):
        u_ref[j] = proj[:, j * LANES:(j + 1) * LANES]
    bd = bd_ref[...]
    o = SSM_WIDTH
    q = _head_norm(proj[:, o:o + ATT_WIDTH], qn_ref[...], bd) * (ATT_HEAD_DIM ** -0.5)
    k = _head_norm(proj[:, o + ATT_WIDTH:o + 2 * ATT_WIDTH], kn_ref[...], bd)
    v = proj[:, o + 2 * ATT_WIDTH:o + 3 * ATT_WIDTH]
    for j in range(ATT_WIDTH // LANES):
        q_ref[j] = q[:, j * LANES:(j + 1) * LANES]
        k_ref[j] = k[:, j * LANES:(j + 1) * LANES]
        v_ref[j] = v[:, j * LANES:(j + 1) * LANES]


def _inproj_even(x2, gain, w_in, q_norm, k_norm):
    n_tok = x2.shape[0]
    n_slab = ATT_WIDTH // LANES
    head_of = jnp.arange(ATT_WIDTH) // ATT_HEAD_DIM
    bd = (head_of[:, None] == head_of[None, :]).astype(BF16)
    qn = jnp.tile(q_norm.astype(F32), ATT_HEADS)[None]
    kn = jnp.tile(k_norm.astype(F32), ATT_HEADS)[None]
    full = lambda shape: pl.BlockSpec(shape, lambda i: (0,) * len(shape))
    slab = pl.BlockSpec((n_slab, TOK_TILE, LANES), lambda i: (0, i, 0))
    slab_shape = jax.ShapeDtypeStruct((n_slab, n_tok, LANES), F32)
    return pl.pallas_call(
        _inproj_even_kernel,
        grid=(n_tok // TOK_TILE,),
        in_specs=[pl.BlockSpec((TOK_TILE, D_MODEL), lambda i: (i, 0)), full((1, D_MODEL)),
                  full(w_in.shape), full(bd.shape), full(qn.shape), full(kn.shape)],
        out_specs=[pl.BlockSpec((SSM_SLABS, TOK_TILE, LANES), lambda i: (0, i, 0)), slab, slab, slab],
        out_shape=[jax.ShapeDtypeStruct((SSM_SLABS, n_tok, LANES), F32), slab_shape, slab_shape, slab_shape],
        compiler_params=_cparams(("parallel",)),
        name="inproj_even",
    )(x2, gain[None].astype(F32), w_in.astype(BF16), bd, qn, kn)


def _s5_tables(a_re, a_im, b_re, b_im, c_re, c_im, d_skip, log_step):
    f = lambda t: t.astype(F32)
    a_re, a_im, b_re, b_im, c_re, c_im = map(f, (a_re, a_im, b_re, b_im, c_re, c_im))
    L = SSM_CHUNK
    step = jnp.exp(f(log_step))[:, None]
    ks = jnp.arange(L + 1, dtype=F32)[:, None, None]
    mag = jnp.exp(ks * (a_re * step)[None])
    ang = ks * (a_im * step)[None]
    pw_re, pw_im = mag * jnp.cos(ang), mag * jnp.sin(ang)
    nr, ni = pw_re[1] - 1.0, pw_im[1]
    den = a_re * a_re + a_im * a_im
    z_re, z_im = (nr * a_re + ni * a_im) / den, (ni * a_re - nr * a_im) / den
    bb_re = z_re[..., None] * b_re - z_im[..., None] * b_im
    bb_im = z_re[..., None] * b_im + z_im[..., None] * b_re
    lb_re = pw_re[..., None] * bb_re[None] - pw_im[..., None] * bb_im[None]
    lb_im = pw_re[..., None] * bb_im[None] + pw_im[..., None] * bb_re[None]
    kk = jnp.einsum('gop,kgpi->gkio', c_re, lb_re[:L]) - jnp.einsum('gop,kgpi->gkio', c_im, lb_im[:L])
    ti = jnp.arange(L)
    lag = ti[None, :] - ti[:, None]
    m = jnp.where((lag >= 0)[None, :, :, None, None], kk[:, jnp.maximum(lag, 0)], 0.0)
    ns, gs = SSM_SLABS, SSM_SLAB_GROUPS
    lw = L * LANES
    pm = m.reshape(ns, gs, L, L, SSM_GROUP, SSM_GROUP).transpose(0, 2, 1, 4, 3, 5).reshape(ns, lw, L * SSM_GROUP)
    fold_e = lambda t: t[:L][::-1].reshape(L, ns, gs, SSM_STATE, SSM_GROUP).transpose(1, 0, 2, 4, 3).reshape(ns, lw, SSM_STATE)
    pe = jnp.concatenate([fold_e(lb_re), fold_e(lb_im)], axis=-1)
    pw1_re, pw1_im = (t[1:].transpose(1, 0, 2)[:, :, None, :] for t in (pw_re, pw_im))
    cl_re = c_re[:, None] * pw1_re - c_im[:, None] * pw1_im
    cl_im = c_re[:, None] * pw1_im + c_im[:, None] * pw1_re
    fold_f = lambda t: t.reshape(ns, gs, L, SSM_GROUP, SSM_STATE).transpose(0, 4, 2, 1, 3).reshape(ns, SSM_STATE, lw)
    pf = jnp.concatenate([fold_f(cl_re), fold_f(-cl_im)], axis=1)
    per_chain = lambda t: jnp.tile(t.reshape(ns, gs * SSM_STATE // LANES, LANES), (1, 2, 1))
    d_vec = jnp.tile(f(d_skip).reshape(ns, 1, LANES), (1, 1, L))
    return pm.astype(BF16), pe.astype(BF16), pf.astype(BF16), per_chain(pw_re[L]), per_chain(pw_im[L]), d_vec


def _iota2(shape):
    return lax.broadcasted_iota(jnp.int32, shape, 0), lax.broadcasted_iota(jnp.int32, shape, 1)


def _widen(compact, group_major_cols, shape, r_shift, c_shift, sel_rows):
    gmask = SSM_SLAB_GROUPS - 1
    if sel_rows:
        r, c = _iota2((shape[0], compact.shape[0]))
        sel = ((r >> 9) == (c >> 6)) & ((r & (SSM_STATE - 1)) == (c & (SSM_STATE - 1)))
        wide = jnp.dot(sel.astype(BF16), compact, preferred_element_type=F32)
    else:
        r, c = _iota2((compact.shape[1], shape[1]))
        if group_major_cols:
            sel = ((c >> 9) == (r >> 6)) & ((c & (SSM_STATE - 1)) == (r & (SSM_STATE - 1)))
        else:
            sel = ((c >> 7) == (r >> 4)) & ((c & (SSM_GROUP - 1)) == (r & (SSM_GROUP - 1)))
        wide = jnp.dot(compact, sel.astype(BF16), preferred_element_type=F32)
    r, c = _iota2(shape)
    keep = ((r >> r_shift) & gmask) == ((c >> c_shift) & gmask)
    return jnp.where(keep, wide, 0.0).astype(BF16)


def _s5_kernel(u_ref, pm_ref, pe_ref, pf_ref, ar_ref, ai_ref, d_ref, y_ref, m_ref, e_ref, f_ref, x_ref, sr_ref, si_ref,
               *, n_chunk, pitch):
    L = SSM_CHUNK
    n_blk = SSM_SLAB_GROUPS * SSM_STATE // LANES
    n_re = n_blk * LANES
    lw = L * LANES
    m_ref[...] = _widen(pm_ref[...], False, (lw, lw), 4, 4, False)
    e_ref[...] = _widen(pe_ref[...], True, (lw, 2 * n_re), 4, 6, False)
    f_ref[...] = _widen(pf_ref[...], False, (2 * n_re, lw), 6, 4, True)
    tiles = [(b, c0) for b in range(2) for c0 in range(0, n_chunk, S5_ROW_TILE)]
    for b, c0 in tiles:
        r0 = b * n_chunk + c0
        for t in range(L):
            x_ref[r0:r0 + S5_ROW_TILE, t * LANES:(t + 1) * LANES] = (
                u_ref[pl.ds(r0 * L + t, S5_ROW_TILE, stride=L), :].astype(BF16))
        sl = jnp.dot(x_ref[r0:r0 + S5_ROW_TILE, :], e_ref[...], preferred_element_type=F32)
        for j in range(n_blk):
            base = (b * n_blk + j) * pitch + c0
            sr_ref[base:base + S5_ROW_TILE, :] = sl[:, j * LANES:(j + 1) * LANES]
            si_ref[base:base + S5_ROW_TILE, :] = sl[:, n_re + j * LANES:n_re + (j + 1) * LANES]
    ar, ai = ar_ref[...], ai_ref[...]
    half = LANES

    def scan_step(c, carry):
        s_re, s_im = carry
        rows = pl.ds(c, SUBLANES, stride=pitch)
        x_re, x_im = sr_ref[rows, :], si_ref[rows, :]
        sr_ref[rows, :] = s_re
        si_ref[rows, :] = s_im
        return ar * s_re - ai * s_im + x_re, ar * s_im + ai * s_re + x_im

    zero = jnp.zeros((SUBLANES, half), F32)
    lax.fori_loop(0, n_chunk, scan_step, (zero, zero), unroll=8)

    for b, c0 in tiles:
        r0 = b * n_chunk + c0
        chain = lambda ref, j: ref[(b * n_blk + j) * pitch + c0:(b * n_blk + j) * pitch + c0 + S5_ROW_TILE, :]
        sp = jnp.concatenate([chain(sr_ref, j) for j in range(n_blk)] + [chain(si_ref, j) for j in range(n_blk)],
                             axis=1).astype(BF16)
        xt = x_ref[r0:r0 + S5_ROW_TILE, :]
        y = (jnp.dot(xt, m_ref[...], preferred_element_type=F32)
             + jnp.dot(sp, f_ref[...], preferred_element_type=F32)
             + d_ref[...] * xt.astype(F32))
        for t in range(L):
            y_ref[pl.ds(r0 * L + t, S5_ROW_TILE, stride=L), :] = y[:, t * LANES:(t + 1) * LANES]


def _s5_core(u, tables, bsz, s_len):
    assert bsz == 2, "the scan packs (batch, lane block) into the 8 sublanes of one vreg"
    pm, pe, pf, a_r, a_i, d_vec = tables
    n_tok = bsz * s_len
    n_chunk = s_len // SSM_CHUNK
    pitch = n_chunk + SUBLANES
    lw = SSM_CHUNK * LANES
    n_state = 2 * SSM_SLAB_GROUPS * SSM_STATE
    slab = lambda shape, **kw: pl.BlockSpec((None,) + shape, lambda i: (i,) + (0,) * len(shape), **kw)
    once = dict(pipeline_mode=pl.Buffered(1))
    return pl.pallas_call(
        functools.partial(_s5_kernel, n_chunk=n_chunk, pitch=pitch),
        grid=(SSM_SLABS,),
        in_specs=[slab((n_tok, LANES), **once), slab(pm.shape[1:]), slab(pe.shape[1:]), slab(pf.shape[1:]),
                  slab((SUBLANES, LANES)), slab((SUBLANES, LANES)), slab((1, lw))],
        out_specs=slab((n_tok, LANES), **once),
        out_shape=jax.ShapeDtypeStruct((SSM_SLABS, n_tok, LANES), F32),
        scratch_shapes=[pltpu.VMEM((lw, lw), BF16), pltpu.VMEM((lw, n_state), BF16), pltpu.VMEM((n_state, lw), BF16),
                        pltpu.VMEM((bsz * n_chunk, lw), BF16),
                        pltpu.VMEM((SUBLANES * pitch, LANES), F32),
                        pltpu.VMEM((SUBLANES * pitch, LANES), F32)],
        compiler_params=_cparams(("parallel",)),
        name="s5_core",
    )(u, pm, pe, pf, a_r, a_i, d_vec)


def _attn_kernel(slope_ref, q_ref, kp_ref, kc_ref, vp_ref, vc_ref, o_ref, kk_ref, vv_ref, m_ref, l_ref, acc_ref):
    slab = pl.program_id(1)
    sb = pl.program_id(2)
    kk_ref[0:ATT_SB, :] = kp_ref[...]
    kk_ref[ATT_SB:2 * ATT_SB, :] = kc_ref[...]
    vv_ref[0:ATT_SB, :] = vp_ref[...]
    vv_ref[ATT_SB:2 * ATT_SB, :] = vc_ref[...]

    lane = lax.broadcasted_iota(jnp.int32, (ATT_BLK, LANES), 1)
    head0 = lane < ATT_HEAD_DIM
    qi = lax.broadcasted_iota(jnp.int32, (ATT_BLK, 2 * ATT_BLK), 0)
    kj = lax.broadcasted_iota(jnp.int32, (ATT_BLK, 2 * ATT_BLK), 1)
    back = qi + ATT_BLK - kj
    band = (back >= 0) & (back <= ATT_BLK)
    neg_steps = -back.astype(F32)
    slopes = (slope_ref[2 * slab], slope_ref[2 * slab + 1])

    for pat, dil in enumerate(DILATIONS):
        span = ATT_BLK * dil

        def tile(idx, carry, dil=dil, span=span, pat=pat):
            start = (idx // dil) * span + idx % dil
            seq_ok = jnp.logical_or(sb > 0, idx >= dil)
            valid = band & ((kj >= ATT_BLK) | seq_ok)
            if dil == 1:
                start = pl.multiple_of(start, ATT_BLK)
                rows_of = lambda first, n: pl.ds(first, n)
            else:
                rows_of = lambda first, n: pl.ds(first, n, stride=dil)
            rows = rows_of(start, ATT_BLK)
            qt = q_ref[rows, :]
            kt = kk_ref[rows_of(ATT_SB + start - span, 2 * ATT_BLK), :].astype(BF16)
            vt = vv_ref[rows_of(ATT_SB + start - span, 2 * ATT_BLK), :].astype(BF16)
            q0 = jnp.where(head0, qt, 0.0)
            parts = []
            for hh, qh in enumerate((q0, qt - q0)):
                s = lax.dot_general(qh.astype(BF16), kt, (((1,), (1,)), ((), ())), preferred_element_type=F32)
                s = jnp.where(valid, s + (slopes[hh] * float(dil)) * neg_steps, NEG_INF)
                m = jnp.max(s, axis=-1, keepdims=True)
                p = jnp.exp(s - m)
                l = jnp.sum(p, axis=-1, keepdims=True)
                o = jnp.dot(p.astype(BF16), vt, preferred_element_type=F32)
                parts.append((m, l, o))
            (m0, l0, o0), (m1, l1, o1) = parts
            m_t = jnp.where(head0, m0, m1)
            l_t = jnp.where(head0, l0, l1)
            o_t = jnp.where(head0, o0, o1)
            if pat == 0:
                m_ref[rows, :] = m_t
                l_ref[rows, :] = l_t
                acc_ref[rows, :] = o_t
            else:
                m_old = m_ref[rows, :]
                m_new = jnp.maximum(m_old, m_t)
                a = jnp.exp(m_old - m_new)
                b = jnp.exp(m_t - m_new)
                m_ref[rows, :] = m_new
                l_ref[rows, :] = a * l_ref[rows, :] + b * l_t
                acc_ref[rows, :] = a * acc_ref[rows, :] + b * o_t
            return carry

        lax.fori_loop(0, ATT_SB // ATT_BLK, tile, 0, unroll=4)

    o_ref[...] = (acc_ref[...] / l_ref[...]).astype(o_ref.dtype)


def _dilated_attention(q, k, v, bsz, s_len):
    n_slab = q.shape[0]
    shape4 = (n_slab, bsz, s_len, LANES)
    q, k, v = (t.reshape(shape4) for t in (q, k, v))
    slopes = jnp.asarray([2.0 ** (-8.0 * (h + 1) / ATT_HEADS) for h in range(ATT_HEADS)], F32)
    blk = (None, None, ATT_SB, LANES)
    cur = pl.BlockSpec(blk, lambda b, j, i, s: (j, b, i, 0))
    prev = pl.BlockSpec(blk, lambda b, j, i, s: (j, b, jnp.maximum(i - 1, 0), 0))
    out = pl.pallas_call(
        _attn_kernel,
        grid_spec=pltpu.PrefetchScalarGridSpec(
            num_scalar_prefetch=1,
            grid=(bsz, n_slab, s_len // ATT_SB),
            in_specs=[cur, prev, cur, prev, cur],
            out_specs=cur,
            scratch_shapes=[pltpu.VMEM((2 * ATT_SB, LANES), F32), pltpu.VMEM((2 * ATT_SB, LANES), F32),
                            pltpu.VMEM((ATT_SB, LANES), F32), pltpu.VMEM((ATT_SB, LANES), F32),
                            pltpu.VMEM((ATT_SB, LANES), F32)]),
        out_shape=jax.ShapeDtypeStruct(shape4, BF16),
        compiler_params=_cparams(("parallel", "parallel", "parallel")),
        name="dilated_attn",
    )(slopes, q, k, k, v, v)
    return out.reshape(n_slab, bsz * s_len, LANES)


def _route_epilogue(h, gain_ref, wr_ref, br_ref, tri_ref, cnt_ref, h_ref, hn_ref, ids_ref, gate_ref, cnt_out_ref):
    tm = h.shape[0]
    h_ref[...] = h
    hn = _rms(h, gain_ref[...])
    hn_ref[...] = hn
    hn_hi = hn.astype(BF16)
    hn_lo = (hn - hn_hi.astype(F32)).astype(BF16)
    logits = (jnp.dot(hn_hi, wr_ref[0], preferred_element_type=F32)
              + (jnp.dot(hn_hi, wr_ref[1], preferred_element_type=F32)
                 + jnp.dot(hn_lo, wr_ref[0], preferred_element_type=F32))) + br_ref[...]
    lane = lax.broadcasted_iota(jnp.int32, (tm, LANES), 1)
    big = jnp.int32(LANES)
    rmax = lambda t: jnp.max(t, axis=-1, keepdims=True)
    rmin = lambda t: jnp.min(t, axis=-1, keepdims=True)
    rsum = lambda t: jnp.sum(t, axis=-1, keepdims=True)
    gmask = lane < N_GROUPS
    gl = jnp.where(gmask, logits, -jnp.inf)
    gmax = rmax(gl)
    ge = jnp.where(gmask, jnp.exp(gl - gmax), 0.0)
    gprob = ge / rsum(ge)
    g_w = rmax(gprob)
    grp = rmin(jnp.where(gmask & (gprob == g_w), lane, big))
    group_of_lane = (lane - N_GROUPS) >> int(math.log2(EXPERTS_PER_GROUP))
    emask = (lane >= N_GROUPS) & (lane < N_GROUPS + N_EXPERTS) & (group_of_lane == grp)
    el = jnp.where(emask, logits, -jnp.inf)
    ee = jnp.where(emask, jnp.exp(el - rmax(el)), 0.0)
    ep = jnp.where(emask, ee / rsum(ee), -1.0)
    p1 = rmax(ep)
    i1 = rmin(jnp.where(ep == p1, lane, big))
    ep2 = jnp.where(lane == i1, -1.0, ep)
    p2 = rmax(ep2)
    i2 = rmin(jnp.where(ep2 == p2, lane, big))
    e1, e2 = i1 - N_GROUPS, i2 - N_GROUPS
    psum = p1 + p2
    gate1, gate2 = g_w * p1 / psum, g_w * p2 / psum
    oh1, oh2 = lane == e1, lane == e2
    member = (oh1 | oh2).astype(BF16)
    before = jnp.dot(tri_ref[...], member, preferred_element_type=F32) + cnt_ref[...]
    r1 = rsum(jnp.where(oh1, before, 0.0)).astype(jnp.int32)
    r2 = rsum(jnp.where(oh2, before, 0.0)).astype(jnp.int32)
    cnt_ref[...] = cnt_ref[...] + jnp.sum(member.astype(F32), axis=0, keepdims=True)
    ids_ref[...] = jnp.where(lane == 0, e1, jnp.where(lane == 1, e2, jnp.where(lane == 2, r1, jnp.where(lane == 3, r2, 0))))
    gate_ref[...] = jnp.where(lane == 0, gate1, jnp.where(lane == 1, gate2, 0.0))
    cnt_out_ref[...] = jnp.broadcast_to(cnt_ref[...], cnt_out_ref.shape)


def _router_operands(norm_gain, w_rg, b_rg, w_re, b_re):
    pad = LANES - N_GROUPS - N_EXPERTS
    wr = jnp.pad(jnp.concatenate([w_rg, w_re], axis=1).astype(F32), ((0, 0), (0, pad)))
    br = jnp.pad(jnp.concatenate([b_rg, b_re]).astype(F32), (0, pad))[None]
    r = jnp.arange(TOK_TILE)
    tri = (r[None, :] < r[:, None]).astype(BF16)
    wr_hi = wr.astype(BF16)
    wr_lo = (wr - wr_hi.astype(F32)).astype(BF16)
    return norm_gain[None].astype(F32), jnp.stack([wr_hi, wr_lo]), br, tri


def _route_specs(n_tok):
    full = lambda shape: pl.BlockSpec(shape, lambda i: (0,) * len(shape))
    in_specs = [full((1, D_MODEL)), full((2, D_MODEL, LANES)), full((1, LANES)), full((TOK_TILE, TOK_TILE))]
    tok = lambda w: pl.BlockSpec((TOK_TILE, w), lambda i: (i, 0))
    out_specs = [tok(D_MODEL), tok(D_MODEL), tok(LANES), tok(LANES), full((SUBLANES, LANES))]
    out_shape = [jax.ShapeDtypeStruct((n_tok, D_MODEL), F32), jax.ShapeDtypeStruct((n_tok, D_MODEL), F32),
                 jax.ShapeDtypeStruct((n_tok, LANES), jnp.int32), jax.ShapeDtypeStruct((n_tok, LANES), F32),
                 jax.ShapeDtypeStruct((SUBLANES, LANES), F32)]
    return in_specs, out_specs, out_shape


def _gelu_tanh(x):
    return 0.5 * x * (1.0 + jnp.tanh(math.sqrt(2.0 / math.pi) * (x + 0.044715 * (x * x * x))))


def _outproj_even_kernel(x_ref, y_ref, a_ref, wglu_ref, bglu_ref, wout_ref, gain_ref, wr_ref, br_ref, tri_ref,
                         h_ref, hn_ref, ids_ref, gate_ref, cnt_out_ref, cnt_ref):
    @pl.when(pl.program_id(0) == 0)
    def _():
        cnt_ref[...] = jnp.zeros_like(cnt_ref)

    y = _gelu_tanh(jnp.concatenate([y_ref[j] for j in range(SSM_SLABS)], axis=1))
    y = y * jax.nn.sigmoid(jnp.dot(y.astype(BF16), wglu_ref[...], preferred_element_type=F32) + bglu_ref[...])
    mix = jnp.dot(y.astype(BF16), wout_ref[0:SSM_WIDTH, :], preferred_element_type=F32)
    for j in range(ATT_WIDTH // LANES):
        rows = slice(SSM_WIDTH + j * LANES, SSM_WIDTH + (j + 1) * LANES)
        mix = mix + jnp.dot(a_ref[j], wout_ref[rows, :], preferred_element_type=F32)
    _route_epilogue(x_ref[...] + mix, gain_ref, wr_ref, br_ref, tri_ref, cnt_ref,
                    h_ref, hn_ref, ids_ref, gate_ref, cnt_out_ref)


def _outproj_even(x2, y_pre, attn, w_glu, b_glu, w_out, route_ops):
    n_tok = x2.shape[0]
    n_slab = attn.shape[0]
    r_in, r_out, r_shape = _route_specs(n_tok)
    full = lambda shape: pl.BlockSpec(shape, lambda i: (0,) * len(shape))
    return pl.pallas_call(
        _outproj_even_kernel,
        grid=(n_tok // TOK_TILE,),
        in_specs=[pl.BlockSpec((TOK_TILE, D_MODEL), lambda i: (i, 0)),
                  pl.BlockSpec((SSM_SLABS, TOK_TILE, LANES), lambda i: (0, i, 0)),
                  pl.BlockSpec((n_slab, TOK_TILE, LANES), lambda i: (0, i, 0)),
                  full(w_glu.shape), full((1, SSM_WIDTH)), full(w_out.shape)] + r_in,
        out_specs=r_out, out_shape=r_shape,
        scratch_shapes=[pltpu.VMEM((1, LANES), F32)],
        compiler_params=_cparams(("arbitrary",)),
        name="outproj_even",
    )(x2, y_pre, attn, w_glu.astype(BF16), b_glu[None].astype(F32), w_out.astype(BF16), *route_ops)


def _conv_layer_kernel(h_ref, gmix_ref, win_ref, cw_ref, wout_ref, gain_ref, wr_ref, br_ref, tri_ref,
                       ho_ref, hn_ref, ids_ref, gate_ref, cnt_out_ref, cnt_ref, zc_ref, *, tiles_per_seq):
    i = pl.program_id(0)

    @pl.when(i == 0)
    def _():
        cnt_ref[...] = jnp.zeros_like(cnt_ref)

    @pl.when(i % tiles_per_seq == 0)
    def _():
        zc_ref[0:SUBLANES, :] = jnp.zeros((SUBLANES, D_MODEL), F32)

    h = h_ref[...]
    tm = h.shape[0]
    hn = _rms(h, gmix_ref[...]).astype(BF16)
    c = D_MODEL
    b_gate = jnp.dot(hn, win_ref[:, 0:c], preferred_element_type=F32)
    zc = jnp.dot(hn, win_ref[:, c:2 * c], preferred_element_type=F32) * jnp.dot(hn, win_ref[:, 2 * c:3 * c], preferred_element_type=F32)
    zc_ref[SUBLANES:SUBLANES + tm, :] = zc
    conv = cw_ref[CONV_TAPS - 1:CONV_TAPS, :] * zc
    for back in range(1, CONV_TAPS):
        tap = CONV_TAPS - 1 - back
        conv = conv + cw_ref[tap:tap + 1, :] * zc_ref[SUBLANES - back:SUBLANES - back + tm, :]
    zc_ref[0:SUBLANES, :] = zc_ref[tm:tm + SUBLANES, :]
    mix = jnp.dot((b_gate * conv).astype(BF16), wout_ref[...], preferred_element_type=F32)
    _route_epilogue(h + mix, gain_ref, wr_ref, br_ref, tri_ref, cnt_ref,
                    ho_ref, hn_ref, ids_ref, gate_ref, cnt_out_ref)


def _conv_layer(h2, gain_mix, w_in, conv_w, w_out, route_ops, s_len):
    n_tok = h2.shape[0]
    r_in, r_out, r_shape = _route_specs(n_tok)
    full = lambda shape: pl.BlockSpec(shape, lambda i: (0,) * len(shape))
    return pl.pallas_call(
        functools.partial(_conv_layer_kernel, tiles_per_seq=s_len // TOK_TILE),
        grid=(n_tok // TOK_TILE,),
        in_specs=[pl.BlockSpec((TOK_TILE, D_MODEL), lambda i: (i, 0)), full((1, D_MODEL)),
                  full(w_in.shape), full(conv_w.shape), full(w_out.shape)] + r_in,
        out_specs=r_out, out_shape=r_shape,
        scratch_shapes=[pltpu.VMEM((1, LANES), F32), pltpu.VMEM((TOK_TILE + SUBLANES, D_MODEL), F32)],
        compiler_params=_cparams(("arbitrary",)),
        name="conv_layer",
    )(h2, gain_mix[None].astype(F32), w_in.astype(BF16), conv_w.astype(F32), w_out.astype(BF16), *route_ops)


def _dispatch_kernel(dest_ref, pend_ref, n_used_ref, hn_ref, xs_ref, zero_ref, sem, zsem, *, n_tok, n_blocks):
    base = pl.program_id(0) * ROW_TILE

    @pl.when(pl.program_id(0) == 0)
    def _():
        zero_ref[...] = jnp.zeros_like(zero_ref)
        zero_block = lambda row0: pltpu.make_async_copy(
            zero_ref, xs_ref.at[pl.ds(pl.multiple_of(row0, MOE_BLK), MOE_BLK)], zsem)

        def per_expert(act):
            for e in range(N_EXPERTS):
                prev_end = pend_ref[e - 1] if e else 0

                @pl.when(pend_ref[e] > prev_end)
                def _():
                    act(zero_block(pend_ref[e] - MOE_BLK))

        def per_tail(act):
            def body(b, carry):
                act(zero_block(b * MOE_BLK))
                return carry
            lax.fori_loop(n_used_ref[0], n_blocks, body, 0)

        for act in (lambda c: c.start(), lambda c: c.wait()):
            per_expert(act)
            per_tail(act)

    def issue(r, carry):
        for slot in range(2):
            d = dest_ref[slot * n_tok + base + r]
            pltpu.make_async_copy(hn_ref.at[pl.ds(r, 1)], xs_ref.at[pl.ds(d, 1)], sem).start()
        return carry

    lax.fori_loop(0, ROW_TILE, issue, 0, unroll=8)
    for _ in range(2):
        pltpu.make_async_copy(hn_ref, hn_ref, sem).wait()


def _dispatch(hn, dest_flat, pends, n_used, n_rows):
    n_tok = hn.shape[0]
    return pl.pallas_call(
        functools.partial(_dispatch_kernel, n_tok=n_tok, n_blocks=n_rows // MOE_BLK),
        grid_spec=pltpu.PrefetchScalarGridSpec(
            num_scalar_prefetch=3,
            grid=(n_tok // ROW_TILE,),
            in_specs=[pl.BlockSpec((ROW_TILE, D_MODEL), lambda i, *_: (i, 0))],
            out_specs=pl.BlockSpec(memory_space=pltpu.HBM),
            scratch_shapes=[pltpu.VMEM((MOE_BLK, D_MODEL), F32), pltpu.SemaphoreType.DMA, pltpu.SemaphoreType.DMA]),
        out_shape=jax.ShapeDtypeStruct((n_rows, D_MODEL), F32),
        compiler_params=_cparams(("arbitrary",)),
        name="moe_dispatch",
    )(dest_flat, pends, n_used, hn)


def _expert_kernel(blk_e_ref, n_used_ref, xs_ref, wg_ref, wu_ref, wd_ref, y_ref, wg_s, wu_s, wd_s):
    b = pl.program_id(0)

    @pl.when(b < n_used_ref[0])
    def _():
        new_expert = jnp.logical_or(b == 0, blk_e_ref[b] != blk_e_ref[jnp.maximum(b - 1, 0)])

        @pl.when(new_expert)
        def _():
            wg_s[...] = wg_ref[...].astype(BF16)
            wu_s[...] = wu_ref[...].astype(BF16)
            wd_s[...] = wd_ref[...].astype(BF16)

        x = xs_ref[...].astype(BF16)
        g = jnp.dot(x, wg_s[...], preferred_element_type=F32)
        u = jnp.dot(x, wu_s[...], preferred_element_type=F32)
        hb = (g * jax.nn.sigmoid(g) * u).astype(BF16)
        y_ref[...] = jnp.dot(hb, wd_s[...], preferred_element_type=F32)

    @pl.when(b >= n_used_ref[0])
    def _():
        y_ref[...] = jnp.zeros_like(y_ref)


def _expert_ffn(xs, blk_e, n_used, w_g, w_u, w_d, layer):
    n_rows = xs.shape[0]
    last = lambda b, n: jnp.maximum(jnp.minimum(b, n[0] - 1), 0)
    w_spec = lambda rows, cols: pl.BlockSpec((None, None, rows, cols), lambda b, e, n: (layer, e[last(b, n)], 0, 0))
    return pl.pallas_call(
        _expert_kernel,
        grid_spec=pltpu.PrefetchScalarGridSpec(
            num_scalar_prefetch=2,
            grid=(n_rows // MOE_BLK,),
            in_specs=[pl.BlockSpec((MOE_BLK, D_MODEL), lambda b, e, n: (last(b, n), 0)),
                      w_spec(D_MODEL, D_EXPERT), w_spec(D_MODEL, D_EXPERT), w_spec(D_EXPERT, D_MODEL)],
            out_specs=pl.BlockSpec((MOE_BLK, D_MODEL), lambda b, e, n: (b, 0)),
            scratch_shapes=[pltpu.VMEM((D_MODEL, D_EXPERT), BF16), pltpu.VMEM((D_MODEL, D_EXPERT), BF16),
                            pltpu.VMEM((D_EXPERT, D_MODEL), BF16)]),
        out_shape=jax.ShapeDtypeStruct((n_rows, D_MODEL), F32),
        compiler_params=_cparams(("arbitrary",)),
        name="moe_experts",
    )(blk_e, n_used, xs, w_g, w_u, w_d)


def _combine_kernel(dest_ref, h_ref, gate_ref, y_ref, o_ref, buf_ref, sem, *, n_tok):
    base = pl.program_id(0) * ROW_TILE

    def issue(r, carry):
        for slot in range(2):
            d = dest_ref[slot * n_tok + base + r]
            pltpu.make_async_copy(y_ref.at[pl.ds(d, 1)], buf_ref.at[slot, pl.ds(r, 1)], sem).start()
        return carry

    lax.fori_loop(0, ROW_TILE, issue, 0, unroll=8)
    for slot in range(2):
        pltpu.make_async_copy(buf_ref.at[slot], buf_ref.at[slot], sem).wait()
    gate = gate_ref[...]
    o_ref[...] = h_ref[...] + gate[:, 0:1] * buf_ref[0] + gate[:, 1:2] * buf_ref[1]


def _combine(h, gate, y_rows, dest_flat):
    n_tok = h.shape[0]
    return pl.pallas_call(
        functools.partial(_combine_kernel, n_tok=n_tok),
        grid_spec=pltpu.PrefetchScalarGridSpec(
            num_scalar_prefetch=1,
            grid=(n_tok // ROW_TILE,),
            in_specs=[pl.BlockSpec((ROW_TILE, D_MODEL), lambda i, d: (i, 0)),
                      pl.BlockSpec((ROW_TILE, LANES), lambda i, d: (i, 0)),
                      pl.BlockSpec(memory_space=pltpu.HBM)],
            out_specs=pl.BlockSpec((ROW_TILE, D_MODEL), lambda i, d: (i, 0)),
            scratch_shapes=[pltpu.VMEM((2, ROW_TILE, D_MODEL), F32), pltpu.SemaphoreType.DMA]),
        out_shape=jax.ShapeDtypeStruct((n_tok, D_MODEL), F32),
        compiler_params=_cparams(("arbitrary",)),
        name="moe_combine",
    )(dest_flat, h, gate, y_rows)


def _moe(h, hn, ids, gate, counts, w_g, w_u, w_d, layer):
    n_tok = h.shape[0]
    n_assign = 2 * n_tok
    n_blocks = n_assign // MOE_BLK + N_EXPERTS
    n_rows = n_blocks * MOE_BLK
    cnt = counts[0, :N_EXPERTS].astype(jnp.int32)
    padded = (cnt + MOE_BLK - 1) // MOE_BLK * MOE_BLK
    pends = jnp.cumsum(padded).astype(jnp.int32)
    pstarts = pends - padded
    ids_t = ids[:, 0:4].T
    dest = (pstarts[ids_t[0:2]] + ids_t[2:4]).reshape(n_assign)
    blk_start = jnp.arange(n_blocks, dtype=jnp.int32) * MOE_BLK
    blk_e = jnp.minimum(jnp.sum(pends[None, :] <= blk_start[:, None], axis=1), N_EXPERTS - 1).astype(jnp.int32)
    n_used = (pends[-1:] // MOE_BLK).astype(jnp.int32)
    xs = _dispatch(hn, dest, pends, n_used, n_rows)
    y_rows = _expert_ffn(xs, blk_e, n_used, w_g, w_u, w_d, layer)
    return _combine(h, gate, y_rows, dest)


def kernel(x, norm_mix, norm_ffn, w_in_even, ssm_a_re, ssm_a_im, ssm_b_re, ssm_b_im, ssm_c_re, ssm_c_im, ssm_d,
           ssm_log_step, w_glu, b_glu, q_norm, k_norm, w_out_even, w_in_conv, conv_w, w_out_conv, w_router_group,
           b_router_group, w_router_expert, b_router_expert, w_expert_gate, w_expert_up, w_expert_down):
    bsz, s_len, d = x.shape
    x2 = x.reshape(bsz * s_len, d)
    route = lambda layer: _router_operands(norm_ffn[layer], w_router_group[layer], b_router_group[layer],
                                           w_router_expert[layer], b_router_expert[layer])
    experts = lambda layer: (w_expert_gate, w_expert_up, w_expert_down, layer)

    u, q, k, v = _inproj_even(x2, norm_mix[0], w_in_even[0], q_norm[0], k_norm[0])
    tables = _s5_tables(ssm_a_re[0], ssm_a_im[0], ssm_b_re[0], ssm_b_im[0], ssm_c_re[0], ssm_c_im[0], ssm_d[0],
                        ssm_log_step[0])
    y_pre = _s5_core(u, tables, bsz, s_len)
    attn = _dilated_attention(q, k, v, bsz, s_len)
    h, hn, ids, gate, counts = _outproj_even(x2, y_pre, attn, w_glu[0], b_glu[0], w_out_even[0], route(0))
    h = _moe(h, hn, ids, gate, counts, *experts(0))

    h, hn, ids, gate, counts = _conv_layer(h, norm_mix[1], w_in_conv[0], conv_w[0], w_out_conv[0], route(1), s_len)
    h = _moe(h, hn, ids, gate, counts, *experts(1))
    return h.reshape(bsz, s_len, d)
```

```python
import functools
import math

import jax
import jax.numpy as jnp
from jax import lax
from jax.experimental import pallas as pl
from jax.experimental.pallas import tpu as pltpu

F32 = jnp.float32
BF16 = jnp.bfloat16

D_MODEL = 1024
SSM_GROUP = 16
SSM_GROUPS = 40
SSM_WIDTH = SSM_GROUP * SSM_GROUPS
SSM_STATE = 64
ATT_HEADS = 6
ATT_HEAD_DIM = 64
ATT_WIDTH = ATT_HEADS * ATT_HEAD_DIM
DILATIONS = (1, 4, 16)
ATT_BLK = 128
CONV_TAPS = 3
N_GROUPS = 4
EXPERTS_PER_GROUP = 8
N_EXPERTS = N_GROUPS * EXPERTS_PER_GROUP
D_EXPERT = 512
MOE_BLK = 256
RMS_EPS = 1e-6
NEG_INF = -1e30

LANES = 128
SUBLANES = 8
VMEM_LIMIT = 56 * 1024 * 1024

TOK_TILE = 512
SSM_CHUNK = 8
SSM_SLAB_GROUPS = LANES // SSM_GROUP
SSM_SLABS = SSM_WIDTH // LANES
S5_ROW_TILE = 256
ATT_SB = 2048
ROW_TILE = 256


def _cparams(sem):
    return pltpu.CompilerParams(dimension_semantics=sem, vmem_limit_bytes=VMEM_LIMIT)


def _rms(x, gain):
    return x * lax.rsqrt(jnp.mean(x * x, axis=-1, keepdims=True) + RMS_EPS) * gain


def _head_norm(t, gain, bd):
    tt = t * t
    hi = tt.astype(BF16)
    lo = (tt - hi.astype(F32)).astype(BF16)
    ss = jnp.dot(hi, bd, preferred_element_type=F32) + jnp.dot(lo, bd, preferred_element_type=F32)
    return t * lax.rsqrt(ss * (1.0 / ATT_HEAD_DIM) + RMS_EPS) * gain


def _inproj_even_kernel(x_ref, g_ref, w_ref, bd_ref, qn_ref, kn_ref, u_ref, q_ref, k_ref, v_ref):
    hn = _rms(x_ref[...], g_ref[...]).astype(BF16)
    proj = jnp.dot(hn, w_ref[...], preferred_element_type=F32)
    for j in range(SSM_SLABS):
        u_ref[j] = proj[:, j * LANES:(j + 1) * LANES]
    bd = bd_ref[...]
    o = SSM_WIDTH
    q = _head_norm(proj[:, o:o + ATT_WIDTH], qn_ref[...], bd) * (ATT_HEAD_DIM ** -0.5)
    k = _head_norm(proj[:, o + ATT_WIDTH:o + 2 * ATT_WIDTH], kn_ref[...], bd)
    v = proj[:, o + 2 * ATT_WIDTH:o + 3 * ATT_WIDTH]
    for j in range(ATT_WIDTH // LANES):
        q_ref[j] = q[:, j * LANES:(j + 1) * LANES]
        k_ref[j] = k[:, j * LANES:(j + 1) * LANES]
        v_ref[j] = v[:, j * LANES:(j + 1) * LANES]


def _inproj_even(x2, gain, w_in, q_norm, k_norm):
    n_tok = x2.shape[0]
    n_slab = ATT_WIDTH // LANES
    head_of = jnp.arange(ATT_WIDTH) // ATT_HEAD_DIM
    bd = (head_of[:, None] == head_of[None, :]).astype(BF16)
    qn = jnp.tile(q_norm.astype(F32), ATT_HEADS)[None]
    kn = jnp.tile(k_norm.astype(F32), ATT_HEADS)[None]
    full = lambda shape: pl.BlockSpec(shape, lambda i: (0,) * len(shape))
    slab = pl.BlockSpec((n_slab, TOK_TILE, LANES), lambda i: (0, i, 0))
    slab_shape = jax.ShapeDtypeStruct((n_slab, n_tok, LANES), F32)
    return pl.pallas_call(
        _inproj_even_kernel,
        grid=(n_tok // TOK_TILE,),
        in_specs=[pl.BlockSpec((TOK_TILE, D_MODEL), lambda i: (i, 0)), full((1, D_MODEL)),
                  full(w_in.shape), full(bd.shape), full(qn.shape), full(kn.shape)],
        out_specs=[pl.BlockSpec((SSM_SLABS, TOK_TILE, LANES), lambda i: (0, i, 0)), slab, slab, slab],
        out_shape=[jax.ShapeDtypeStruct((SSM_SLABS, n_tok, LANES), F32), slab_shape, slab_shape, slab_shape],
        compiler_params=_cparams(("parallel",)),
        name="inproj_even",
    )(x2, gain[None].astype(F32), w_in.astype(BF16), bd, qn, kn)


def _s5_tables(a_re, a_im, b_re, b_im, c_re, c_im, d_skip, log_step):
    f = lambda t: t.astype(F32)
    a_re, a_im, b_re, b_im, c_re, c_im = map(f, (a_re, a_im, b_re, b_im, c_re, c_im))
    L = SSM_CHUNK
    step = jnp.exp(f(log_step))[:, None]
    ks = jnp.arange(L + 1, dtype=F32)[:, None, None]
    mag = jnp.exp(ks * (a_re * step)[None])
    ang = ks * (a_im * step)[None]
    pw_re, pw_im = mag * jnp.cos(ang), mag * jnp.sin(ang)
    nr, ni = pw_re[1] - 1.0, pw_im[1]
    den = a_re * a_re + a_im * a_im
    z_re, z_im = (nr * a_re + ni * a_im) / den, (ni * a_re - nr * a_im) / den
    bb_re = z_re[..., None] * b_re - z_im[..., None] * b_im
    bb_im = z_re[..., None] * b_im + z_im[..., None] * b_re
    lb_re = pw_re[..., None] * bb_re[None] - pw_im[..., None] * bb_im[None]
    lb_im = pw_re[..., None] * bb_im[None] + pw_im[..., None] * bb_re[None]
    kk = jnp.einsum('gop,kgpi->gkio', c_re, lb_re[:L]) - jnp.einsum('gop,kgpi->gkio', c_im, lb_im[:L])
    ti = jnp.arange(L)
    lag = ti[None, :] - ti[:, None]
    m = jnp.where((lag >= 0)[None, :, :, None, None], kk[:, jnp.maximum(lag, 0)], 0.0)
    ns, gs = SSM_SLABS, SSM_SLAB_GROUPS
    lw = L * LANES
    pm = m.reshape(ns, gs, L, L, SSM_GROUP, SSM_GROUP).transpose(0, 2, 1, 4, 3, 5).reshape(ns, lw, L * SSM_GROUP)
    fold_e = lambda t: t[:L][::-1].reshape(L, ns, gs, SSM_STATE, SSM_GROUP).transpose(1, 0, 2, 4, 3).reshape(ns, lw, SSM_STATE)
    pe = jnp.concatenate([fold_e(lb_re), fold_e(lb_im)], axis=-1)
    pw1_re, pw1_im = (t[1:].transpose(1, 0, 2)[:, :, None, :] for t in (pw_re, pw_im))
    cl_re = c_re[:, None] * pw1_re - c_im[:, None] * pw1_im
    cl_im = c_re[:, None] * pw1_im + c_im[:, None] * pw1_re
    fold_f = lambda t: t.reshape(ns, gs, L, SSM_GROUP, SSM_STATE).transpose(0, 4, 2, 1, 3).reshape(ns, SSM_STATE, lw)
    pf = jnp.concatenate([fold_f(cl_re), fold_f(-cl_im)], axis=1)
    per_chain = lambda t: jnp.tile(t.reshape(ns, gs * SSM_STATE // LANES, LANES), (1, 2, 1))
    d_vec = jnp.tile(f(d_skip).reshape(ns, 1, LANES), (1, 1, L))
    return pm.astype(BF16), pe.astype(BF16), pf.astype(BF16), per_chain(pw_re[L]), per_chain(pw_im[L]), d_vec


def _iota2(shape):
    return lax.broadcasted_iota(jnp.int32, shape, 0), lax.broadcasted_iota(jnp.int32, shape, 1)


def _widen(compact, group_major_cols, shape, r_shift, c_shift, sel_rows):
    gmask = SSM_SLAB_GROUPS - 1
    if sel_rows:
        r, c = _iota2((shape[0], compact.shape[0]))
        sel = ((r >> 9) == (c >> 6)) & ((r & (SSM_STATE - 1)) == (c & (SSM_STATE - 1)))
        wide = jnp.dot(sel.astype(BF16), compact, preferred_element_type=F32)
    else:
        r, c = _iota2((compact.shape[1], shape[1]))
        if group_major_cols:
            sel = ((c >> 9) == (r >> 6)) & ((c & (SSM_STATE - 1)) == (r & (SSM_STATE - 1)))
        else:
            sel = ((c >> 7) == (r >> 4)) & ((c & (SSM_GROUP - 1)) == (r & (SSM_GROUP - 1)))
        wide = jnp.dot(compact, sel.astype(BF16), preferred_element_type=F32)
    r, c = _iota2(shape)
    keep = ((r >> r_shift) & gmask) == ((c >> c_shift) & gmask)
    return jnp.where(keep, wide, 0.0).astype(BF16)


def _s5_kernel(u_ref, pm_ref, pe_ref, pf_ref, ar_ref, ai_ref, d_ref, y_ref, m_ref, e_ref, f_ref, x_ref, sr_ref, si_ref,
               *, n_chunk, pitch):
    L = SSM_CHUNK
    n_blk = SSM_SLAB_GROUPS * SSM_STATE // LANES
    n_re = n_blk * LANES
    lw = L * LANES
    m_ref[...] = _widen(pm_ref[...], False, (lw, lw), 4, 4, False)
    e_ref[...] = _widen(pe_ref[...], True, (lw, 2 * n_re), 4, 6, False)
    f_ref[...] = _widen(pf_ref[...], False, (2 * n_re, lw), 6, 4, True)
    tiles = [(b, c0) for b in range(2) for c0 in range(0, n_chunk, S5_ROW_TILE)]
    for b, c0 in tiles:
        r0 = b * n_chunk + c0
        for t in range(L):
            x_ref[r0:r0 + S5_ROW_TILE, t * LANES:(t + 1) * LANES] = (
                u_ref[pl.ds(r0 * L + t, S5_ROW_TILE, stride=L), :].astype(BF16))
        sl = jnp.dot(x_ref[r0:r0 + S5_ROW_TILE, :], e_ref[...], preferred_element_type=F32)
        for j in range(n_blk):
            base = (b * n_blk + j) * pitch + c0
            sr_ref[base:base + S5_ROW_TILE, :] = sl[:, j * LANES:(j + 1) * LANES]
            si_ref[base:base + S5_ROW_TILE, :] = sl[:, n_re + j * LANES:n_re + (j + 1) * LANES]
    ar, ai = ar_ref[...], ai_ref[...]
    half = LANES

    def scan_step(c, carry):
        s_re, s_im = carry
        rows = pl.ds(c, SUBLANES, stride=pitch)
        x_re, x_im = sr_ref[rows, :], si_ref[rows, :]
        sr_ref[rows, :] = s_re
        si_ref[rows, :] = s_im
        return ar * s_re - ai * s_im + x_re, ar * s_im + ai * s_re + x_im

    zero = jnp.zeros((SUBLANES, half), F32)
    lax.fori_loop(0, n_chunk, scan_step, (zero, zero), unroll=8)

    for b, c0 in tiles:
        r0 = b * n_chunk + c0
        chain = lambda ref, j: ref[(b * n_blk + j) * pitch + c0:(b * n_blk + j) * pitch + c0 + S5_ROW_TILE, :]
        sp = jnp.concatenate([chain(sr_ref, j) for j in range(n_blk)] + [chain(si_ref, j) for j in range(n_blk)],
                             axis=1).astype(BF16)
        xt = x_ref[r0:r0 + S5_ROW_TILE, :]
        y = (jnp.dot(xt, m_ref[...], preferred_element_type=F32)
             + jnp.dot(sp, f_ref[...], preferred_element_type=F32)
             + d_ref[...] * xt.astype(F32))
        for t in range(L):
            y_ref[pl.ds(r0 * L + t, S5_ROW_TILE, stride=L), :] = y[:, t * LANES:(t + 1) * LANES]


def _s5_core(u, tables, bsz, s_len):
    assert bsz == 2, "the scan packs (batch, lane block) into the 8 sublanes of one vreg"
    pm, pe, pf, a_r, a_i, d_vec = tables
    n_tok = bsz * s_len
    n_chunk = s_len // SSM_CHUNK
    pitch = n_chunk + SUBLANES
    lw = SSM_CHUNK * LANES
    n_state = 2 * SSM_SLAB_GROUPS * SSM_STATE
    slab = lambda shape, **kw: pl.BlockSpec((None,) + shape, lambda i: (i,) + (0,) * len(shape), **kw)
    once = dict(pipeline_mode=pl.Buffered(1))
    return pl.pallas_call(
        functools.partial(_s5_kernel, n_chunk=n_chunk, pitch=pitch),
        grid=(SSM_SLABS,),
        in_specs=[slab((n_tok, LANES), **once), slab(pm.shape[1:]), slab(pe.shape[1:]), slab(pf.shape[1:]),
                  slab((SUBLANES, LANES)), slab((SUBLANES, LANES)), slab((1, lw))],
        out_specs=slab((n_tok, LANES), **once),
        out_shape=jax.ShapeDtypeStruct((SSM_SLABS, n_tok, LANES), F32),
        scratch_shapes=[pltpu.VMEM((lw, lw), BF16), pltpu.VMEM((lw, n_state), BF16), pltpu.VMEM((n_state, lw), BF16),
                        pltpu.VMEM((bsz * n_chunk, lw), BF16),
                        pltpu.VMEM((SUBLANES * pitch, LANES), F32),
                        pltpu.VMEM((SUBLANES * pitch, LANES), F32)],
        compiler_params=_cparams(("parallel",)),
        name="s5_core",
    )(u, pm, pe, pf, a_r, a_i, d_vec)


def _attn_kernel(slope_ref, q_ref, kp_ref, kc_ref, vp_ref, vc_ref, o_ref, kk_ref, vv_ref, m_ref, l_ref, acc_ref):
    slab = pl.program_id(1)
    sb = pl.program_id(2)
    kk_ref[0:ATT_SB, :] = kp_ref[...]
    kk_ref[ATT_SB:2 * ATT_SB, :] = kc_ref[...]
    vv_ref[0:ATT_SB, :] = vp_ref[...]
    vv_ref[ATT_SB:2 * ATT_SB, :] = vc_ref[...]

    lane = lax.broadcasted_iota(jnp.int32, (ATT_BLK, LANES), 1)
    head0 = lane < ATT_HEAD_DIM
    qi = lax.broadcasted_iota(jnp.int32, (ATT_BLK, 2 * ATT_BLK), 0)
    kj = lax.broadcasted_iota(jnp.int32, (ATT_BLK, 2 * ATT_BLK), 1)
    back = qi + ATT_BLK - kj
    band = (back >= 0) & (back <= ATT_BLK)
    neg_steps = -back.astype(F32)
    slopes = (slope_ref[2 * slab], slope_ref[2 * slab + 1])

    for pat, dil in enumerate(DILATIONS):
        span = ATT_BLK * dil

        def tile(idx, carry, dil=dil, span=span, pat=pat):
            start = (idx // dil) * span + idx % dil
            seq_ok = jnp.logical_or(sb > 0, idx >= dil)
            valid = band & ((kj >= ATT_BLK) | seq_ok)
            if dil == 1:
                start = pl.multiple_of(start, ATT_BLK)
                rows_of = lambda first, n: pl.ds(first, n)
            else:
                rows_of = lambda first, n: pl.ds(first, n, stride=dil)
            rows = rows_of(start, ATT_BLK)
            qt = q_ref[rows, :]
            kt = kk_ref[rows_of(ATT_SB + start - span, 2 * ATT_BLK), :].astype(BF16)
            vt = vv_ref[rows_of(ATT_SB + start - span, 2 * ATT_BLK), :].astype(BF16)
            q0 = jnp.where(head0, qt, 0.0)
            parts = []
            for hh, qh in enumerate((q0, qt - q0)):
                s = lax.dot_general(qh.astype(BF16), kt, (((1,), (1,)), ((), ())), preferred_element_type=F32)
                s = jnp.where(valid, s + (slopes[hh] * float(dil)) * neg_steps, NEG_INF)
                m = jnp.max(s, axis=-1, keepdims=True)
                p = jnp.exp(s - m)
                l = jnp.sum(p, axis=-1, keepdims=True)
                o = jnp.dot(p.astype(BF16), vt, preferred_element_type=F32)
                parts.append((m, l, o))
            (m0, l0, o0), (m1, l1, o1) = parts
            m_t = jnp.where(head0, m0, m1)
            l_t = jnp.where(head0, l0, l1)
            o_t = jnp.where(head0, o0, o1)
            if pat == 0:
                m_ref[rows, :] = m_t
                l_ref[rows, :] = l_t
                acc_ref[rows, :] = o_t
            else:
                m_old = m_ref[rows, :]
                m_new = jnp.maximum(m_old, m_t)
                a = jnp.exp(m_old - m_new)
                b = jnp.exp(m_t - m_new)
                m_ref[rows, :] = m_new
                l_ref[rows, :] = a * l_ref[rows, :] + b * l_t
                acc_ref[rows, :] = a * acc_ref[rows, :] + b * o_t
            return carry

        lax.fori_loop(0, ATT_SB // ATT_BLK, tile, 0, unroll=4)

    o_ref[...] = (acc_ref[...] / l_ref[...]).astype(o_ref.dtype)


def _dilated_attention(q, k, v, bsz, s_len):
    n_slab = q.shape[0]
    shape4 = (n_slab, bsz, s_len, LANES)
    q, k, v = (t.reshape(shape4) for t in (q, k, v))
    slopes = jnp.asarray([2.0 ** (-8.0 * (h + 1) / ATT_HEADS) for h in range(ATT_HEADS)], F32)
    blk = (None, None, ATT_SB, LANES)
    cur = pl.BlockSpec(blk, lambda b, j, i, s: (j, b, i, 0))
    prev = pl.BlockSpec(blk, lambda b, j, i, s: (j, b, jnp.maximum(i - 1, 0), 0))
    out = pl.pallas_call(
        _attn_kernel,
        grid_spec=pltpu.PrefetchScalarGridSpec(
            num_scalar_prefetch=1,
            grid=(bsz, n_slab, s_len // ATT_SB),
            in_specs=[cur, prev, cur, prev, cur],
            out_specs=cur,
            scratch_shapes=[pltpu.VMEM((2 * ATT_SB, LANES), F32), pltpu.VMEM((2 * ATT_SB, LANES), F32),
                            pltpu.VMEM((ATT_SB, LANES), F32), pltpu.VMEM((ATT_SB, LANES), F32),
                            pltpu.VMEM((ATT_SB, LANES), F32)]),
        out_shape=jax.ShapeDtypeStruct(shape4, BF16),
        compiler_params=_cparams(("parallel", "parallel", "parallel")),
        name="dilated_attn",
    )(slopes, q, k, k, v, v)
    return out.reshape(n_slab, bsz * s_len, LANES)


def _route_epilogue(h, gain_ref, wr_ref, br_ref, tri_ref, cnt_ref, h_ref, hn_ref, ids_ref, gate_ref, cnt_out_ref):
    tm = h.shape[0]
    h_ref[...] = h
    hn = _rms(h, gain_ref[...])
    hn_ref[...] = hn
    hn_hi = hn.astype(BF16)
    hn_lo = (hn - hn_hi.astype(F32)).astype(BF16)
    logits = (jnp.dot(hn_hi, wr_ref[0], preferred_element_type=F32)
              + (jnp.dot(hn_hi, wr_ref[1], preferred_element_type=F32)
                 + jnp.dot(hn_lo, wr_ref[0], preferred_element_type=F32))) + br_ref[...]
    lane = lax.broadcasted_iota(jnp.int32, (tm, LANES), 1)
    big = jnp.int32(LANES)
    rmax = lambda t: jnp.max(t, axis=-1, keepdims=True)
    rmin = lambda t: jnp.min(t, axis=-1, keepdims=True)
    rsum = lambda t: jnp.sum(t, axis=-1, keepdims=True)
    gmask = lane < N_GROUPS
    gl = jnp.where(gmask, logits, -jnp.inf)
    gmax = rmax(gl)
    ge = jnp.where(gmask, jnp.exp(gl - gmax), 0.0)
    gprob = ge / rsum(ge)
    g_w = rmax(gprob)
    grp = rmin(jnp.where(gmask & (gprob == g_w), lane, big))
    group_of_lane = (lane - N_GROUPS) >> int(math.log2(EXPERTS_PER_GROUP))
    emask = (lane >= N_GROUPS) & (lane < N_GROUPS + N_EXPERTS) & (group_of_lane == grp)
    el = jnp.where(emask, logits, -jnp.inf)
    ee = jnp.where(emask, jnp.exp(el - rmax(el)), 0.0)
    ep = jnp.where(emask, ee / rsum(ee), -1.0)
    p1 = rmax(ep)
    i1 = rmin(jnp.where(ep == p1, lane, big))
    ep2 = jnp.where(lane == i1, -1.0, ep)
    p2 = rmax(ep2)
    i2 = rmin(jnp.where(ep2 == p2, lane, big))
    e1, e2 = i1 - N_GROUPS, i2 - N_GROUPS
    psum = p1 + p2
    gate1, gate2 = g_w * p1 / psum, g_w * p2 / psum
    oh1, oh2 = lane == e1, lane == e2
    member = (oh1 | oh2).astype(BF16)
    before = jnp.dot(tri_ref[...], member, preferred_element_type=F32) + cnt_ref[...]
    r1 = rsum(jnp.where(oh1, before, 0.0)).astype(jnp.int32)
    r2 = rsum(jnp.where(oh2, before, 0.0)).astype(jnp.int32)
    cnt_ref[...] = cnt_ref[...] + jnp.sum(member.astype(F32), axis=0, keepdims=True)
    ids_ref[...] = jnp.where(lane == 0, e1, jnp.where(lane == 1, e2, jnp.where(lane == 2, r1, jnp.where(lane == 3, r2, 0))))
    gate_ref[...] = jnp.where(lane == 0, gate1, jnp.where(lane == 1, gate2, 0.0))
    cnt_out_ref[...] = jnp.broadcast_to(cnt_ref[...], cnt_out_ref.shape)


def _router_operands(norm_gain, w_rg, b_rg, w_re, b_re):
    pad = LANES - N_GROUPS - N_EXPERTS
    wr = jnp.pad(jnp.concatenate([w_rg, w_re], axis=1).astype(F32), ((0, 0), (0, pad)))
    br = jnp.pad(jnp.concatenate([b_rg, b_re]).astype(F32), (0, pad))[None]
    r = jnp.arange(TOK_TILE)
    tri = (r[None, :] < r[:, None]).astype(BF16)
    wr_hi = wr.astype(BF16)
    wr_lo = (wr - wr_hi.astype(F32)).astype(BF16)
    return norm_gain[None].astype(F32), jnp.stack([wr_hi, wr_lo]), br, tri


def _route_specs(n_tok):
    full = lambda shape: pl.BlockSpec(shape, lambda i: (0,) * len(shape))
    in_specs = [full((1, D_MODEL)), full((2, D_MODEL, LANES)), full((1, LANES)), full((TOK_TILE, TOK_TILE))]
    tok = lambda w: pl.BlockSpec((TOK_TILE, w), lambda i: (i, 0))
    out_specs = [tok(D_MODEL), tok(D_MODEL), tok(LANES), tok(LANES), full((SUBLANES, LANES))]
    out_shape = [jax.ShapeDtypeStruct((n_tok, D_MODEL), F32), jax.ShapeDtypeStruct((n_tok, D_MODEL), F32),
                 jax.ShapeDtypeStruct((n_tok, LANES), jnp.int32), jax.ShapeDtypeStruct((n_tok, LANES), F32),
                 jax.ShapeDtypeStruct((SUBLANES, LANES), F32)]
    return in_specs, out_specs, out_shape


def _gelu_tanh(x):
    return 0.5 * x * (1.0 + jnp.tanh(math.sqrt(2.0 / math.pi) * (x + 0.044715 * (x * x * x))))


def _outproj_even_kernel(x_ref, y_ref, a_ref, wglu_ref, bglu_ref, wout_ref, gain_ref, wr_ref, br_ref, tri_ref,
                         h_ref, hn_ref, ids_ref, gate_ref, cnt_out_ref, cnt_ref):
    @pl.when(pl.program_id(0) == 0)
    def _():
        cnt_ref[...] = jnp.zeros_like(cnt_ref)

    y = _gelu_tanh(jnp.concatenate([y_ref[j] for j in range(SSM_SLABS)], axis=1))
    y = y * jax.nn.sigmoid(jnp.dot(y.astype(BF16), wglu_ref[...], preferred_element_type=F32) + bglu_ref[...])
    mix = jnp.dot(y.astype(BF16), wout_ref[0:SSM_WIDTH, :], preferred_element_type=F32)
    for j in range(ATT_WIDTH // LANES):
        rows = slice(SSM_WIDTH + j * LANES, SSM_WIDTH + (j + 1) * LANES)
        mix = mix + jnp.dot(a_ref[j], wout_ref[rows, :], preferred_element_type=F32)
    _route_epilogue(x_ref[...] + mix, gain_ref, wr_ref, br_ref, tri_ref, cnt_ref,
                    h_ref, hn_ref, ids_ref, gate_ref, cnt_out_ref)


def _outproj_even(x2, y_pre, attn, w_glu, b_glu, w_out, route_ops):
    n_tok = x2.shape[0]
    n_slab = attn.shape[0]
    r_in, r_out, r_shape = _route_specs(n_tok)
    full = lambda shape: pl.BlockSpec(shape, lambda i: (0,) * len(shape))
    return pl.pallas_call(
        _outproj_even_kernel,
        grid=(n_tok // TOK_TILE,),
        in_specs=[pl.BlockSpec((TOK_TILE, D_MODEL), lambda i: (i, 0)),
                  pl.BlockSpec((SSM_SLABS, TOK_TILE, LANES), lambda i: (0, i, 0)),
                  pl.BlockSpec((n_slab, TOK_TILE, LANES), lambda i: (0, i, 0)),
                  full(w_glu.shape), full((1, SSM_WIDTH)), full(w_out.shape)] + r_in,
        out_specs=r_out, out_shape=r_shape,
        scratch_shapes=[pltpu.VMEM((1, LANES), F32)],
        compiler_params=_cparams(("arbitrary",)),
        name="outproj_even",
    )(x2, y_pre, attn, w_glu.astype(BF16), b_glu[None].astype(F32), w_out.astype(BF16), *route_ops)


def _conv_layer_kernel(h_ref, gmix_ref, win_ref, cw_ref, wout_ref, gain_ref, wr_ref, br_ref, tri_ref,
                       ho_ref, hn_ref, ids_ref, gate_ref, cnt_out_ref, cnt_ref, zc_ref, *, tiles_per_seq):
    i = pl.program_id(0)

    @pl.when(i == 0)
    def _():
        cnt_ref[...] = jnp.zeros_like(cnt_ref)

    @pl.when(i % tiles_per_seq == 0)
    def _():
        zc_ref[0:SUBLANES, :] = jnp.zeros((SUBLANES, D_MODEL), F32)

    h = h_ref[...]
    tm = h.shape[0]
    hn = _rms(h, gmix_ref[...]).astype(BF16)
    c = D_MODEL
    b_gate = jnp.dot(hn, win_ref[:, 0:c], preferred_element_type=F32)
    zc = jnp.dot(hn, win_ref[:, c:2 * c], preferred_element_type=F32) * jnp.dot(hn, win_ref[:, 2 * c:3 * c], preferred_element_type=F32)
    zc_ref[SUBLANES:SUBLANES + tm, :] = zc
    conv = cw_ref[CONV_TAPS - 1:CONV_TAPS, :] * zc
    for back in range(1, CONV_TAPS):
        tap = CONV_TAPS - 1 - back
        conv = conv + cw_ref[tap:tap + 1, :] * zc_ref[SUBLANES - back:SUBLANES - back + tm, :]
    zc_ref[0:SUBLANES, :] = zc_ref[tm:tm + SUBLANES, :]
    mix = jnp.dot((b_gate * conv).astype(BF16), wout_ref[...], preferred_element_type=F32)
    _route_epilogue(h + mix, gain_ref, wr_ref, br_ref, tri_ref, cnt_ref,
                    ho_ref, hn_ref, ids_ref, gate_ref, cnt_out_ref)


def _conv_layer(h2, gain_mix, w_in, conv_w, w_out, route_ops, s_len):
    n_tok = h2.shape[0]
    r_in, r_out, r_shape = _route_specs(n_tok)
    full = lambda shape: pl.BlockSpec(shape, lambda i: (0,) * len(shape))
    return pl.pallas_call(
        functools.partial(_conv_layer_kernel, tiles_per_seq=s_len // TOK_TILE),
        grid=(n_tok // TOK_TILE,),
        in_specs=[pl.BlockSpec((TOK_TILE, D_MODEL), lambda i: (i, 0)), full((1, D_MODEL)),
                  full(w_in.shape), full(conv_w.shape), full(w_out.shape)] + r_in,
        out_specs=r_out, out_shape=r_shape,
        scratch_shapes=[pltpu.VMEM((1, LANES), F32), pltpu.VMEM((TOK_TILE + SUBLANES, D_MODEL), F32)],
        compiler_params=_cparams(("arbitrary",)),
        name="conv_layer",
    )(h2, gain_mix[None].astype(F32), w_in.astype(BF16), conv_w.astype(F32), w_out.astype(BF16), *route_ops)


def _dispatch_kernel(dest_ref, pend_ref, n_used_ref, hn_ref, xs_ref, zero_ref, sem, zsem, *, n_tok, n_blocks):
    base = pl.program_id(0) * ROW_TILE

    @pl.when(pl.program_id(0) == 0)
    def _():
        zero_ref[...] = jnp.zeros_like(zero_ref)
        zero_block = lambda row0: pltpu.make_async_copy(
            zero_ref, xs_ref.at[pl.ds(pl.multiple_of(row0, MOE_BLK), MOE_BLK)], zsem)

        def per_expert(act):
            for e in range(N_EXPERTS):
                prev_end = pend_ref[e - 1] if e else 0

                @pl.when(pend_ref[e] > prev_end)
                def _():
                    act(zero_block(pend_ref[e] - MOE_BLK))

        def per_tail(act):
            def body(b, carry):
                act(zero_block(b * MOE_BLK))
                return carry
            lax.fori_loop(n_used_ref[0], n_blocks, body, 0)

        for act in (lambda c: c.start(), lambda c: c.wait()):
            per_expert(act)
            per_tail(act)

    def issue(r, carry):
        for slot in range(2):
            d = dest_ref[slot * n_tok + base + r]
            pltpu.make_async_copy(hn_ref.at[pl.ds(r, 1)], xs_ref.at[pl.ds(d, 1)], sem).start()
        return carry

    lax.fori_loop(0, ROW_TILE, issue, 0, unroll=8)
    for _ in range(2):
        pltpu.make_async_copy(hn_ref, hn_ref, sem).wait()


def _dispatch(hn, dest_flat, pends, n_used, n_rows):
    n_tok = hn.shape[0]
    return pl.pallas_call(
        functools.partial(_dispatch_kernel, n_tok=n_tok, n_blocks=n_rows // MOE_BLK),
        grid_spec=pltpu.PrefetchScalarGridSpec(
            num_scalar_prefetch=3,
            grid=(n_tok // ROW_TILE,),
            in_specs=[pl.BlockSpec((ROW_TILE, D_MODEL), lambda i, *_: (i, 0))],
            out_specs=pl.BlockSpec(memory_space=pltpu.HBM),
            scratch_shapes=[pltpu.VMEM((MOE_BLK, D_MODEL), F32), pltpu.SemaphoreType.DMA, pltpu.SemaphoreType.DMA]),
        out_shape=jax.ShapeDtypeStruct((n_rows, D_MODEL), F32),
        compiler_params=_cparams(("arbitrary",)),
        name="moe_dispatch",
    )(dest_flat, pends, n_used, hn)


def _expert_kernel(blk_e_ref, n_used_ref, xs_ref, wg_ref, wu_ref, wd_ref, y_ref, wg_s, wu_s, wd_s):
    b = pl.program_id(0)

    @pl.when(b < n_used_ref[0])
    def _():
        new_expert = jnp.logical_or(b == 0, blk_e_ref[b] != blk_e_ref[jnp.maximum(b - 1, 0)])

        @pl.when(new_expert)
        def _():
            wg_s[...] = wg_ref[...].astype(BF16)
            wu_s[...] = wu_ref[...].astype(BF16)
            wd_s[...] = wd_ref[...].astype(BF16)

        x = xs_ref[...].astype(BF16)
        g = jnp.dot(x, wg_s[...], preferred_element_type=F32)
        u = jnp.dot(x, wu_s[...], preferred_element_type=F32)
        hb = (g * jax.nn.sigmoid(g) * u).astype(BF16)
        y_ref[...] = jnp.dot(hb, wd_s[...], preferred_element_type=F32)

    @pl.when(b >= n_used_ref[0])
    def _():
        y_ref[...] = jnp.zeros_like(y_ref)


def _expert_ffn(xs, blk_e, n_used, w_g, w_u, w_d, layer):
    n_rows = xs.shape[0]
    last = lambda b, n: jnp.maximum(jnp.minimum(b, n[0] - 1), 0)
    w_spec = lambda rows, cols: pl.BlockSpec((None, None, rows, cols), lambda b, e, n: (layer, e[last(b, n)], 0, 0))
    return pl.pallas_call(
        _expert_kernel,
        grid_spec=pltpu.PrefetchScalarGridSpec(
            num_scalar_prefetch=2,
            grid=(n_rows // MOE_BLK,),
            in_specs=[pl.BlockSpec((MOE_BLK, D_MODEL), lambda b, e, n: (last(b, n), 0)),
                      w_spec(D_MODEL, D_EXPERT), w_spec(D_MODEL, D_EXPERT), w_spec(D_EXPERT, D_MODEL)],
            out_specs=pl.BlockSpec((MOE_BLK, D_MODEL), lambda b, e, n: (b, 0)),
            scratch_shapes=[pltpu.VMEM((D_MODEL, D_EXPERT), BF16), pltpu.VMEM((D_MODEL, D_EXPERT), BF16),
                            pltpu.VMEM((D_EXPERT, D_MODEL), BF16)]),
        out_shape=jax.ShapeDtypeStruct((n_rows, D_MODEL), F32),
        compiler_params=_cparams(("arbitrary",)),
        name="moe_experts",
    )(blk_e, n_used, xs, w_g, w_u, w_d)


def _combine_kernel(dest_ref, h_ref, gate_ref, y_ref, o_ref, buf_ref, sem, *, n_tok):
    base = pl.program_id(0) * ROW_TILE

    def issue(r, carry):
        for slot in range(2):
            d = dest_ref[slot * n_tok + base + r]
            pltpu.make_async_copy(y_ref.at[pl.ds(d, 1)], buf_ref.at[slot, pl.ds(r, 1)], sem).start()
        return carry

    lax.fori_loop(0, ROW_TILE, issue, 0, unroll=8)
    for slot in range(2):
        pltpu.make_async_copy(buf_ref.at[slot], buf_ref.at[slot], sem).wait()
    gate = gate_ref[...]
    o_ref[...] = h_ref[...] + gate[:, 0:1] * buf_ref[0] + gate[:, 1:2] * buf_ref[1]


def _combine(h, gate, y_rows, dest_flat):
    n_tok = h.shape[0]
    return pl.pallas_call(
        functools.partial(_combine_kernel, n_tok=n_tok),
        grid_spec=pltpu.PrefetchScalarGridSpec(
            num_scalar_prefetch=1,
            grid=(n_tok // ROW_TILE,),
            in_specs=[pl.BlockSpec((ROW_TILE, D_MODEL), lambda i, d: (i, 0)),
                      pl.BlockSpec((ROW_TILE, LANES), lambda i, d: (i, 0)),
                      pl.BlockSpec(memory_space=pltpu.HBM)],
            out_specs=pl.BlockSpec((ROW_TILE, D_MODEL), lambda i, d: (i, 0)),
            scratch_shapes=[pltpu.VMEM((2, ROW_TILE, D_MODEL), F32), pltpu.SemaphoreType.DMA]),
        out_shape=jax.ShapeDtypeStruct((n_tok, D_MODEL), F32),
        compiler_params=_cparams(("arbitrary",)),
        name="moe_combine",
    )(dest_flat, h, gate, y_rows)


def _moe(h, hn, ids, gate, counts, w_g, w_u, w_d, layer):
    n_tok = h.shape[0]
    n_assign = 2 * n_tok
    n_blocks = n_assign // MOE_BLK + N_EXPERTS
    n_rows = n_blocks * MOE_BLK
    cnt = counts[0, :N_EXPERTS].astype(jnp.int32)
    padded = (cnt + MOE_BLK - 1) // MOE_BLK * MOE_BLK
    pends = jnp.cumsum(padded).astype(jnp.int32)
    pstarts = pends - padded
    ids_t = ids[:, 0:4].T
    first_row = sum(jnp.where(ids_t[0:2] == e, pstarts[e], 0) for e in range(N_EXPERTS))
    dest = (first_row + ids_t[2:4]).reshape(n_assign)
    blk_start = jnp.arange(n_blocks, dtype=jnp.int32) * MOE_BLK
    blk_e = jnp.minimum(jnp.sum(pends[None, :] <= blk_start[:, None], axis=1), N_EXPERTS - 1).astype(jnp.int32)
    n_used = (pends[-1:] // MOE_BLK).astype(jnp.int32)
    xs = _dispatch(hn, dest, pends, n_used, n_rows)
    y_rows = _expert_ffn(xs, blk_e, n_used, w_g, w_u, w_d, layer)
    return _combine(h, gate, y_rows, dest)


def kernel(x, norm_mix, norm_ffn, w_in_even, ssm_a_re, ssm_a_im, ssm_b_re, ssm_b_im, ssm_c_re, ssm_c_im, ssm_d,
           ssm_log_step, w_glu, b_glu, q_norm, k_norm, w_out_even, w_in_conv, conv_w, w_out_conv, w_router_group,
           b_router_group, w_router_expert, b_router_expert, w_expert_gate, w_expert_up, w_expert_down):
    bsz, s_len, d = x.shape
    x2 = x.reshape(bsz * s_len, d)
    route = lambda layer: _router_operands(norm_ffn[layer], w_router_group[layer], b_router_group[layer],
                                           w_router_expert[layer], b_router_expert[layer])
    experts = lambda layer: (w_expert_gate, w_expert_up, w_expert_down, layer)

    u, q, k, v = _inproj_even(x2, norm_mix[0], w_in_even[0], q_norm[0], k_norm[0])
    tables = _s5_tables(ssm_a_re[0], ssm_a_im[0], ssm_b_re[0], ssm_b_im[0], ssm_c_re[0], ssm_c_im[0], ssm_d[0],
                        ssm_log_step[0])
    y_pre = _s5_core(u, tables, bsz, s_len)
    attn = _dilated_attention(q, k, v, bsz, s_len)
    h, hn, ids, gate, counts = _outproj_even(x2, y_pre, attn, w_glu[0], b_glu[0], w_out_even[0], route(0))
    h = _moe(h, hn, ids, gate, counts, *experts(0))

    h, hn, ids, gate, counts = _conv_layer(h, norm_mix[1], w_in_conv[0], conv_w[0], w_out_conv[0], route(1), s_len)
    h = _moe(h, hn, ids, gate, counts, *experts(1))
    return h.reshape(bsz, s_len, d)
```

```python
import functools
import math

import jax
import jax.numpy as jnp
from jax import lax
from jax.experimental import pallas as pl
from jax.experimental.pallas import tpu as pltpu

F32 = jnp.float32
BF16 = jnp.bfloat16

D_MODEL = 1024
SSM_GROUP = 16
SSM_GROUPS = 40
SSM_WIDTH = SSM_GROUP * SSM_GROUPS
SSM_STATE = 64
ATT_HEADS = 6
ATT_HEAD_DIM = 64
ATT_WIDTH = ATT_HEADS * ATT_HEAD_DIM
DILATIONS = (1, 4, 16)
ATT_BLK = 128
CONV_TAPS = 3
N_GROUPS = 4
EXPERTS_PER_GROUP = 8
N_EXPERTS = N_GROUPS * EXPERTS_PER_GROUP
D_EXPERT = 512
MOE_BLK = 256
RMS_EPS = 1e-6
NEG_INF = -1e30

LANES = 128
SUBLANES = 8
VMEM_LIMIT = 56 * 1024 * 1024

TOK_TILE = 512
SSM_CHUNK = 8
SSM_SLAB_GROUPS = LANES // SSM_GROUP
SSM_SLABS = SSM_WIDTH // LANES
S5_ROW_TILE = 256
ATT_SB = 2048
ROW_TILE = 256


def _cparams(sem):
    return pltpu.CompilerParams(dimension_semantics=sem, vmem_limit_bytes=VMEM_LIMIT)


def _rms(x, gain):
    return x * lax.rsqrt(jnp.mean(x * x, axis=-1, keepdims=True) + RMS_EPS) * gain


def _head_norm(t, gain, bd):
    tt = t * t
    hi = tt.astype(BF16)
    lo = (tt - hi.astype(F32)).astype(BF16)
    ss = jnp.dot(hi, bd, preferred_element_type=F32) + jnp.dot(lo, bd, preferred_element_type=F32)
    return t * lax.rsqrt(ss * (1.0 / ATT_HEAD_DIM) + RMS_EPS) * gain


def _inproj_even_kernel(x_ref, g_ref, w_ref, bd_ref, qn_ref, kn_ref, u_ref, q_ref, k_ref, v_ref):
    hn = _rms(x_ref[...], g_ref[...]).astype(BF16)
    proj = jnp.dot(hn, w_ref[...], preferred_element_type=F32)
    for j in range(SSM_SLABS):
        u_ref[j] = proj[:, j * LANES:(j + 1) * LANES]
    bd = bd_ref[...]
    o = SSM_WIDTH
    q = _head_norm(proj[:, o:o + ATT_WIDTH], qn_ref[...], bd) * (ATT_HEAD_DIM ** -0.5)
    k = _head_norm(proj[:, o + ATT_WIDTH:o + 2 * ATT_WIDTH], kn_ref[...], bd)
    v = proj[:, o + 2 * ATT_WIDTH:o + 3 * ATT_WIDTH]
    for j in range(ATT_WIDTH // LANES):
        q_ref[j] = q[:, j * LANES:(j + 1) * LANES]
        k_ref[j] = k[:, j * LANES:(j + 1) * LANES]
        v_ref[j] = v[:, j * LANES:(j + 1) * LANES]


def _inproj_even(x2, gain, w_in, q_norm, k_norm):
    n_tok = x2.shape[0]
    n_slab = ATT_WIDTH // LANES
    head_of = jnp.arange(ATT_WIDTH) // ATT_HEAD_DIM
    bd = (head_of[:, None] == head_of[None, :]).astype(BF16)
    qn = jnp.tile(q_norm.astype(F32), ATT_HEADS)[None]
    kn = jnp.tile(k_norm.astype(F32), ATT_HEADS)[None]
    full = lambda shape: pl.BlockSpec(shape, lambda i: (0,) * len(shape))
    slab = pl.BlockSpec((n_slab, TOK_TILE, LANES), lambda i: (0, i, 0))
    slab_shape = jax.ShapeDtypeStruct((n_slab, n_tok, LANES), F32)
    return pl.pallas_call(
        _inproj_even_kernel,
        grid=(n_tok // TOK_TILE,),
        in_specs=[pl.BlockSpec((TOK_TILE, D_MODEL), lambda i: (i, 0)), full((1, D_MODEL)),
                  full(w_in.shape), full(bd.shape), full(qn.shape), full(kn.shape)],
        out_specs=[pl.BlockSpec((SSM_SLABS, TOK_TILE, LANES), lambda i: (0, i, 0)), slab, slab, slab],
        out_shape=[jax.ShapeDtypeStruct((SSM_SLABS, n_tok, LANES), F32), slab_shape, slab_shape, slab_shape],
        compiler_params=_cparams(("parallel",)),
        name="inproj_even",
    )(x2, gain[None].astype(F32), w_in.astype(BF16), bd, qn, kn)


def _s5_tables(a_re, a_im, b_re, b_im, c_re, c_im, d_skip, log_step):
    f = lambda t: t.astype(F32)
    a_re, a_im, b_re, b_im, c_re, c_im = map(f, (a_re, a_im, b_re, b_im, c_re, c_im))
    L = SSM_CHUNK
    step = jnp.exp(f(log_step))[:, None]
    ks = jnp.arange(L + 1, dtype=F32)[:, None, None]
    mag = jnp.exp(ks * (a_re * step)[None])
    ang = ks * (a_im * step)[None]
    pw_re, pw_im = mag * jnp.cos(ang), mag * jnp.sin(ang)
    nr, ni = pw_re[1] - 1.0, pw_im[1]
    den = a_re * a_re + a_im * a_im
    z_re, z_im = (nr * a_re + ni * a_im) / den, (ni * a_re - nr * a_im) / den
    bb_re = z_re[..., None] * b_re - z_im[..., None] * b_im
    bb_im = z_re[..., None] * b_im + z_im[..., None] * b_re
    lb_re = pw_re[..., None] * bb_re[None] - pw_im[..., None] * bb_im[None]
    lb_im = pw_re[..., None] * bb_im[None] + pw_im[..., None] * bb_re[None]
    kk = jnp.einsum('gop,kgpi->gkio', c_re, lb_re[:L]) - jnp.einsum('gop,kgpi->gkio', c_im, lb_im[:L])
    ti = jnp.arange(L)
    lag = ti[None, :] - ti[:, None]
    m = jnp.where((lag >= 0)[None, :, :, None, None], kk[:, jnp.maximum(lag, 0)], 0.0)
    ns, gs = SSM_SLABS, SSM_SLAB_GROUPS
    lw = L * LANES
    pm = m.reshape(ns, gs, L, L, SSM_GROUP, SSM_GROUP).transpose(0, 2, 1, 4, 3, 5).reshape(ns, lw, L * SSM_GROUP)
    fold_e = lambda t: t[:L][::-1].reshape(L, ns, gs, SSM_STATE, SSM_GROUP).transpose(1, 0, 2, 4, 3).reshape(ns, lw, SSM_STATE)
    pe = jnp.concatenate([fold_e(lb_re), fold_e(lb_im)], axis=-1)
    pw1_re, pw1_im = (t[1:].transpose(1, 0, 2)[:, :, None, :] for t in (pw_re, pw_im))
    cl_re = c_re[:, None] * pw1_re - c_im[:, None] * pw1_im
    cl_im = c_re[:, None] * pw1_im + c_im[:, None] * pw1_re
    fold_f = lambda t: t.reshape(ns, gs, L, SSM_GROUP, SSM_STATE).transpose(0, 4, 2, 1, 3).reshape(ns, SSM_STATE, lw)
    pf = jnp.concatenate([fold_f(cl_re), fold_f(-cl_im)], axis=1)
    per_chain = lambda t: jnp.tile(t.reshape(ns, gs * SSM_STATE // LANES, LANES), (1, 2, 1))
    d_vec = jnp.tile(f(d_skip).reshape(ns, 1, LANES), (1, 1, L))
    return pm.astype(BF16), pe.astype(BF16), pf.astype(BF16), per_chain(pw_re[L]), per_chain(pw_im[L]), d_vec


def _iota2(shape):
    return lax.broadcasted_iota(jnp.int32, shape, 0), lax.broadcasted_iota(jnp.int32, shape, 1)


def _widen(compact, group_major_cols, shape, r_shift, c_shift, sel_rows):
    gmask = SSM_SLAB_GROUPS - 1
    if sel_rows:
        r, c = _iota2((shape[0], compact.shape[0]))
        sel = ((r >> 9) == (c >> 6)) & ((r & (SSM_STATE - 1)) == (c & (SSM_STATE - 1)))
        wide = jnp.dot(sel.astype(BF16), compact, preferred_element_type=F32)
    else:
        r, c = _iota2((compact.shape[1], shape[1]))
        if group_major_cols:
            sel = ((c >> 9) == (r >> 6)) & ((c & (SSM_STATE - 1)) == (r & (SSM_STATE - 1)))
        else:
            sel = ((c >> 7) == (r >> 4)) & ((c & (SSM_GROUP - 1)) == (r & (SSM_GROUP - 1)))
        wide = jnp.dot(compact, sel.astype(BF16), preferred_element_type=F32)
    r, c = _iota2(shape)
    keep = ((r >> r_shift) & gmask) == ((c >> c_shift) & gmask)
    return jnp.where(keep, wide, 0.0).astype(BF16)


def _s5_kernel(u_ref, pm_ref, pe_ref, pf_ref, ar_ref, ai_ref, d_ref, y_ref, m_ref, e_ref, f_ref, x_ref, sr_ref, si_ref,
               *, n_chunk, pitch):
    L = SSM_CHUNK
    n_blk = SSM_SLAB_GROUPS * SSM_STATE // LANES
    n_re = n_blk * LANES
    lw = L * LANES
    m_ref[...] = _widen(pm_ref[...], False, (lw, lw), 4, 4, False)
    e_ref[...] = _widen(pe_ref[...], True, (lw, 2 * n_re), 4, 6, False)
    f_ref[...] = _widen(pf_ref[...], False, (2 * n_re, lw), 6, 4, True)
    tiles = [(b, c0) for b in range(2) for c0 in range(0, n_chunk, S5_ROW_TILE)]
    for b, c0 in tiles:
        r0 = b * n_chunk + c0
        for t in range(L):
            x_ref[r0:r0 + S5_ROW_TILE, t * LANES:(t + 1) * LANES] = (
                u_ref[pl.ds(r0 * L + t, S5_ROW_TILE, stride=L), :].astype(BF16))
        sl = jnp.dot(x_ref[r0:r0 + S5_ROW_TILE, :], e_ref[...], preferred_element_type=F32)
        for j in range(n_blk):
            base = (b * n_blk + j) * pitch + c0
            sr_ref[base:base + S5_ROW_TILE, :] = sl[:, j * LANES:(j + 1) * LANES]
            si_ref[base:base + S5_ROW_TILE, :] = sl[:, n_re + j * LANES:n_re + (j + 1) * LANES]
    ar, ai = ar_ref[...], ai_ref[...]
    half = LANES

    def scan_step(c, carry):
        s_re, s_im = carry
        rows = pl.ds(c, SUBLANES, stride=pitch)
        x_re, x_im = sr_ref[rows, :], si_ref[rows, :]
        sr_ref[rows, :] = s_re
        si_ref[rows, :] = s_im
        return ar * s_re - ai * s_im + x_re, ar * s_im + ai * s_re + x_im

    zero = jnp.zeros((SUBLANES, half), F32)
    lax.fori_loop(0, n_chunk, scan_step, (zero, zero), unroll=8)

    for b, c0 in tiles:
        r0 = b * n_chunk + c0
        chain = lambda ref, j: ref[(b * n_blk + j) * pitch + c0:(b * n_blk + j) * pitch + c0 + S5_ROW_TILE, :]
        sp = jnp.concatenate([chain(sr_ref, j) for j in range(n_blk)] + [chain(si_ref, j) for j in range(n_blk)],
                             axis=1).astype(BF16)
        xt = x_ref[r0:r0 + S5_ROW_TILE, :]
        y = (jnp.dot(xt, m_ref[...], preferred_element_type=F32)
             + jnp.dot(sp, f_ref[...], preferred_element_type=F32)
             + d_ref[...] * xt.astype(F32))
        for t in range(L):
            y_ref[pl.ds(r0 * L + t, S5_ROW_TILE, stride=L), :] = y[:, t * LANES:(t + 1) * LANES]


def _s5_core(u, tables, bsz, s_len):
    assert bsz == 2, "the scan packs (batch, lane block) into the 8 sublanes of one vreg"
    pm, pe, pf, a_r, a_i, d_vec = tables
    n_tok = bsz * s_len
    n_chunk = s_len // SSM_CHUNK
    pitch = n_chunk + SUBLANES
    lw = SSM_CHUNK * LANES
    n_state = 2 * SSM_SLAB_GROUPS * SSM_STATE
    slab = lambda shape, **kw: pl.BlockSpec((None,) + shape, lambda i: (i,) + (0,) * len(shape), **kw)
    once = dict(pipeline_mode=pl.Buffered(1))
    return pl.pallas_call(
        functools.partial(_s5_kernel, n_chunk=n_chunk, pitch=pitch),
        grid=(SSM_SLABS,),
        in_specs=[slab((n_tok, LANES), **once), slab(pm.shape[1:]), slab(pe.shape[1:]), slab(pf.shape[1:]),
                  slab((SUBLANES, LANES)), slab((SUBLANES, LANES)), slab((1, lw))],
        out_specs=slab((n_tok, LANES), **once),
        out_shape=jax.ShapeDtypeStruct((SSM_SLABS, n_tok, LANES), F32),
        scratch_shapes=[pltpu.VMEM((lw, lw), BF16), pltpu.VMEM((lw, n_state), BF16), pltpu.VMEM((n_state, lw), BF16),
                        pltpu.VMEM((bsz * n_chunk, lw), BF16),
                        pltpu.VMEM((SUBLANES * pitch, LANES), F32),
                        pltpu.VMEM((SUBLANES * pitch, LANES), F32)],
        compiler_params=_cparams(("parallel",)),
        name="s5_core",
    )(u, pm, pe, pf, a_r, a_i, d_vec)


def _attn_kernel(slope_ref, q_ref, kp_ref, kc_ref, vp_ref, vc_ref, o_ref, kk_ref, vv_ref, m_ref, l_ref, acc_ref):
    slab = pl.program_id(1)
    sb = pl.program_id(2)
    kk_ref[0:ATT_SB, :] = kp_ref[...]
    kk_ref[ATT_SB:2 * ATT_SB, :] = kc_ref[...]
    vv_ref[0:ATT_SB, :] = vp_ref[...]
    vv_ref[ATT_SB:2 * ATT_SB, :] = vc_ref[...]

    lane = lax.broadcasted_iota(jnp.int32, (ATT_BLK, LANES), 1)
    head0 = lane < ATT_HEAD_DIM
    qi = lax.broadcasted_iota(jnp.int32, (ATT_BLK, 2 * ATT_BLK), 0)
    kj = lax.broadcasted_iota(jnp.int32, (ATT_BLK, 2 * ATT_BLK), 1)
    back = qi + ATT_BLK - kj
    band = (back >= 0) & (back <= ATT_BLK)
    neg_steps = -back.astype(F32)
    slopes = (slope_ref[2 * slab], slope_ref[2 * slab + 1])

    for pat, dil in enumerate(DILATIONS):
        span = ATT_BLK * dil

        def tile(idx, carry, dil=dil, span=span, pat=pat):
            start = (idx // dil) * span + idx % dil
            seq_ok = jnp.logical_or(sb > 0, idx >= dil)
            valid = band & ((kj >= ATT_BLK) | seq_ok)
            if dil == 1:
                start = pl.multiple_of(start, ATT_BLK)
                rows_of = lambda first, n: pl.ds(first, n)
            else:
                rows_of = lambda first, n: pl.ds(first, n, stride=dil)
            rows = rows_of(start, ATT_BLK)
            qt = q_ref[rows, :]
            kt = kk_ref[rows_of(ATT_SB + start - span, 2 * ATT_BLK), :].astype(BF16)
            vt = vv_ref[rows_of(ATT_SB + start - span, 2 * ATT_BLK), :].astype(BF16)
            q0 = jnp.where(head0, qt, 0.0)
            parts = []
            for hh, qh in enumerate((q0, qt - q0)):
                s = lax.dot_general(qh.astype(BF16), kt, (((1,), (1,)), ((), ())), preferred_element_type=F32)
                s = jnp.where(valid, s + (slopes[hh] * float(dil)) * neg_steps, NEG_INF)
                m = jnp.max(s, axis=-1, keepdims=True)
                p = jnp.exp(s - m)
                l = jnp.sum(p, axis=-1, keepdims=True)
                o = jnp.dot(p.astype(BF16), vt, preferred_element_type=F32)
                parts.append((m, l, o))
            (m0, l0, o0), (m1, l1, o1) = parts
            m_t = jnp.where(head0, m0, m1)
            l_t = jnp.where(head0, l0, l1)
            o_t = jnp.where(head0, o0, o1)
            if pat == 0:
                m_ref[rows, :] = m_t
                l_ref[rows, :] = l_t
                acc_ref[rows, :] = o_t
            else:
                m_old = m_ref[rows, :]
                m_new = jnp.maximum(m_old, m_t)
                a = jnp.exp(m_old - m_new)
                b = jnp.exp(m_t - m_new)
                m_ref[rows, :] = m_new
                l_ref[rows, :] = a * l_ref[rows, :] + b * l_t
                acc_ref[rows, :] = a * acc_ref[rows, :] + b * o_t
            return carry

        lax.fori_loop(0, ATT_SB // ATT_BLK, tile, 0, unroll=4)

    o_ref[...] = (acc_ref[...] / l_ref[...]).astype(o_ref.dtype)


def _dilated_attention(q, k, v, bsz, s_len):
    n_slab = q.shape[0]
    shape4 = (n_slab, bsz, s_len, LANES)
    q, k, v = (t.reshape(shape4) for t in (q, k, v))
    slopes = jnp.asarray([2.0 ** (-8.0 * (h + 1) / ATT_HEADS) for h in range(ATT_HEADS)], F32)
    blk = (None, None, ATT_SB, LANES)
    cur = pl.BlockSpec(blk, lambda b, j, i, s: (j, b, i, 0))
    prev = pl.BlockSpec(blk, lambda b, j, i, s: (j, b, jnp.maximum(i - 1, 0), 0))
    out = pl.pallas_call(
        _attn_kernel,
        grid_spec=pltpu.PrefetchScalarGridSpec(
            num_scalar_prefetch=1,
            grid=(bsz, n_slab, s_len // ATT_SB),
            in_specs=[cur, prev, cur, prev, cur],
            out_specs=cur,
            scratch_shapes=[pltpu.VMEM((2 * ATT_SB, LANES), F32), pltpu.VMEM((2 * ATT_SB, LANES), F32),
                            pltpu.VMEM((ATT_SB, LANES), F32), pltpu.VMEM((ATT_SB, LANES), F32),
                            pltpu.VMEM((ATT_SB, LANES), F32)]),
        out_shape=jax.ShapeDtypeStruct(shape4, BF16),
        compiler_params=_cparams(("parallel", "parallel", "parallel")),
        name="dilated_attn",
    )(slopes, q, k, k, v, v)
    return out.reshape(n_slab, bsz * s_len, LANES)


def _route_epilogue(h, gain_ref, wr_ref, br_ref, tri_ref, cnt_ref, h_ref, hn_ref, ids_ref, gate_ref, cnt_out_ref):
    tm = h.shape[0]
    h_ref[...] = h
    hn = _rms(h, gain_ref[...])
    hn_ref[...] = hn
    hn_hi = hn.astype(BF16)
    hn_lo = (hn - hn_hi.astype(F32)).astype(BF16)
    logits = (jnp.dot(hn_hi, wr_ref[0], preferred_element_type=F32)
              + (jnp.dot(hn_hi, wr_ref[1], preferred_element_type=F32)
                 + jnp.dot(hn_lo, wr_ref[0], preferred_element_type=F32))) + br_ref[...]
    lane = lax.broadcasted_iota(jnp.int32, (tm, LANES), 1)
    big = jnp.int32(LANES)
    rmax = lambda t: jnp.max(t, axis=-1, keepdims=True)
    rmin = lambda t: jnp.min(t, axis=-1, keepdims=True)
    rsum = lambda t: jnp.sum(t, axis=-1, keepdims=True)
    gmask = lane < N_GROUPS
    gl = jnp.where(gmask, logits, -jnp.inf)
    gmax = rmax(gl)
    ge = jnp.where(gmask, jnp.exp(gl - gmax), 0.0)
    gprob = ge / rsum(ge)
    g_w = rmax(gprob)
    grp = rmin(jnp.where(gmask & (gprob == g_w), lane, big))
    group_of_lane = (lane - N_GROUPS) >> int(math.log2(EXPERTS_PER_GROUP))
    emask = (lane >= N_GROUPS) & (lane < N_GROUPS + N_EXPERTS) & (group_of_lane == grp)
    el = jnp.where(emask, logits, -jnp.inf)
    ee = jnp.where(emask, jnp.exp(el - rmax(el)), 0.0)
    ep = jnp.where(emask, ee / rsum(ee), -1.0)
    p1 = rmax(ep)
    i1 = rmin(jnp.where(ep == p1, lane, big))
    ep2 = jnp.where(lane == i1, -1.0, ep)
    p2 = rmax(ep2)
    i2 = rmin(jnp.where(ep2 == p2, lane, big))
    e1, e2 = i1 - N_GROUPS, i2 - N_GROUPS
    psum = p1 + p2
    gate1, gate2 = g_w * p1 / psum, g_w * p2 / psum
    oh1, oh2 = lane == e1, lane == e2
    member = (oh1 | oh2).astype(BF16)
    before = jnp.dot(tri_ref[...], member, preferred_element_type=F32) + cnt_ref[...]
    r1 = rsum(jnp.where(oh1, before, 0.0)).astype(jnp.int32)
    r2 = rsum(jnp.where(oh2, before, 0.0)).astype(jnp.int32)
    cnt_ref[...] = cnt_ref[...] + jnp.sum(member.astype(F32), axis=0, keepdims=True)
    ids_ref[...] = jnp.where(lane == 0, e1, jnp.where(lane == 1, e2, jnp.where(lane == 2, r1, jnp.where(lane == 3, r2, 0))))
    gate_ref[...] = jnp.where(lane == 0, gate1, jnp.where(lane == 1, gate2, 0.0))
    cnt_out_ref[...] = jnp.broadcast_to(cnt_ref[...], cnt_out_ref.shape)


def _router_operands(norm_gain, w_rg, b_rg, w_re, b_re):
    pad = LANES - N_GROUPS - N_EXPERTS
    wr = jnp.pad(jnp.concatenate([w_rg, w_re], axis=1).astype(F32), ((0, 0), (0, pad)))
    br = jnp.pad(jnp.concatenate([b_rg, b_re]).astype(F32), (0, pad))[None]
    r = jnp.arange(TOK_TILE)
    tri = (r[None, :] < r[:, None]).astype(BF16)
    wr_hi = wr.astype(BF16)
    wr_lo = (wr - wr_hi.astype(F32)).astype(BF16)
    return norm_gain[None].astype(F32), jnp.stack([wr_hi, wr_lo]), br, tri


def _route_specs(n_tok):
    full = lambda shape: pl.BlockSpec(shape, lambda i: (0,) * len(shape))
    in_specs = [full((1, D_MODEL)), full((2, D_MODEL, LANES)), full((1, LANES)), full((TOK_TILE, TOK_TILE))]
    tok = lambda w: pl.BlockSpec((TOK_TILE, w), lambda i: (i, 0))
    out_specs = [tok(D_MODEL), tok(D_MODEL), tok(LANES), tok(LANES), full((SUBLANES, LANES))]
    out_shape = [jax.ShapeDtypeStruct((n_tok, D_MODEL), F32), jax.ShapeDtypeStruct((n_tok, D_MODEL), F32),
                 jax.ShapeDtypeStruct((n_tok, LANES), jnp.int32), jax.ShapeDtypeStruct((n_tok, LANES), F32),
                 jax.ShapeDtypeStruct((SUBLANES, LANES), F32)]
    return in_specs, out_specs, out_shape


def _gelu_tanh(x):
    return 0.5 * x * (1.0 + jnp.tanh(math.sqrt(2.0 / math.pi) * (x + 0.044715 * (x * x * x))))


def _outproj_even_kernel(x_ref, y_ref, a_ref, wglu_ref, bglu_ref, wout_ref, gain_ref, wr_ref, br_ref, tri_ref,
                         h_ref, hn_ref, ids_ref, gate_ref, cnt_out_ref, cnt_ref):
    @pl.when(pl.program_id(0) == 0)
    def _():
        cnt_ref[...] = jnp.zeros_like(cnt_ref)

    y = _gelu_tanh(jnp.concatenate([y_ref[j] for j in range(SSM_SLABS)], axis=1))
    y = y * jax.nn.sigmoid(jnp.dot(y.astype(BF16), wglu_ref[...], preferred_element_type=F32) + bglu_ref[...])
    mix = jnp.dot(y.astype(BF16), wout_ref[0:SSM_WIDTH, :], preferred_element_type=F32)
    for j in range(ATT_WIDTH // LANES):
        rows = slice(SSM_WIDTH + j * LANES, SSM_WIDTH + (j + 1) * LANES)
        mix = mix + jnp.dot(a_ref[j], wout_ref[rows, :], preferred_element_type=F32)
    _route_epilogue(x_ref[...] + mix, gain_ref, wr_ref, br_ref, tri_ref, cnt_ref,
                    h_ref, hn_ref, ids_ref, gate_ref, cnt_out_ref)


def _outproj_even(x2, y_pre, attn, w_glu, b_glu, w_out, route_ops):
    n_tok = x2.shape[0]
    n_slab = attn.shape[0]
    r_in, r_out, r_shape = _route_specs(n_tok)
    full = lambda shape: pl.BlockSpec(shape, lambda i: (0,) * len(shape))
    return pl.pallas_call(
        _outproj_even_kernel,
        grid=(n_tok // TOK_TILE,),
        in_specs=[pl.BlockSpec((TOK_TILE, D_MODEL), lambda i: (i, 0)),
                  pl.BlockSpec((SSM_SLABS, TOK_TILE, LANES), lambda i: (0, i, 0)),
                  pl.BlockSpec((n_slab, TOK_TILE, LANES), lambda i: (0, i, 0)),
                  full(w_glu.shape), full((1, SSM_WIDTH)), full(w_out.shape)] + r_in,
        out_specs=r_out, out_shape=r_shape,
        scratch_shapes=[pltpu.VMEM((1, LANES), F32)],
        compiler_params=_cparams(("arbitrary",)),
        name="outproj_even",
    )(x2, y_pre, attn, w_glu.astype(BF16), b_glu[None].astype(F32), w_out.astype(BF16), *route_ops)


def _conv_layer_kernel(h_ref, pgate_ref, y0_ref, y1_ref, gmix_ref, win_ref, cw_ref, wout_ref, gain_ref, wr_ref, br_ref,
                       tri_ref, ho_ref, hn_ref, ids_ref, gate_ref, cnt_out_ref, cnt_ref, zc_ref, *, tiles_per_seq):
    i = pl.program_id(0)

    @pl.when(i == 0)
    def _():
        cnt_ref[...] = jnp.zeros_like(cnt_ref)

    @pl.when(i % tiles_per_seq == 0)
    def _():
        zc_ref[0:SUBLANES, :] = jnp.zeros((SUBLANES, D_MODEL), F32)

    pgate = pgate_ref[...]
    h = h_ref[...] + pgate[:, 0:1] * y0_ref[...] + pgate[:, 1:2] * y1_ref[...]
    tm = h.shape[0]
    hn = _rms(h, gmix_ref[...]).astype(BF16)
    c = D_MODEL
    b_gate = jnp.dot(hn, win_ref[:, 0:c], preferred_element_type=F32)
    zc = jnp.dot(hn, win_ref[:, c:2 * c], preferred_element_type=F32) * jnp.dot(hn, win_ref[:, 2 * c:3 * c], preferred_element_type=F32)
    zc_ref[SUBLANES:SUBLANES + tm, :] = zc
    conv = cw_ref[CONV_TAPS - 1:CONV_TAPS, :] * zc
    for back in range(1, CONV_TAPS):
        tap = CONV_TAPS - 1 - back
        conv = conv + cw_ref[tap:tap + 1, :] * zc_ref[SUBLANES - back:SUBLANES - back + tm, :]
    zc_ref[0:SUBLANES, :] = zc_ref[tm:tm + SUBLANES, :]
    mix = jnp.dot((b_gate * conv).astype(BF16), wout_ref[...], preferred_element_type=F32)
    _route_epilogue(h + mix, gain_ref, wr_ref, br_ref, tri_ref, cnt_ref,
                    ho_ref, hn_ref, ids_ref, gate_ref, cnt_out_ref)


def _conv_layer(h1, pgate, y2, gain_mix, w_in, conv_w, w_out, route_ops, s_len):
    n_tok = h1.shape[0]
    r_in, r_out, r_shape = _route_specs(n_tok)
    full = lambda shape: pl.BlockSpec(shape, lambda i: (0,) * len(shape))
    return pl.pallas_call(
        functools.partial(_conv_layer_kernel, tiles_per_seq=s_len // TOK_TILE),
        grid=(n_tok // TOK_TILE,),
        in_specs=_moe_specs(n_tok) + [full((1, D_MODEL)), full(w_in.shape), full(conv_w.shape), full(w_out.shape)] + r_in,
        out_specs=r_out, out_shape=r_shape,
        scratch_shapes=[pltpu.VMEM((1, LANES), F32), pltpu.VMEM((TOK_TILE + SUBLANES, D_MODEL), F32)],
        compiler_params=_cparams(("arbitrary",)),
        name="conv_layer",
    )(h1, pgate, y2, y2, gain_mix[None].astype(F32), w_in.astype(BF16), conv_w.astype(F32), w_out.astype(BF16),
      *route_ops)


def _row_map_kernel(dest_ref, cnt_ref, pend_ref, src_ref, *, n_assign, n_rows):
    def fill(i, carry):
        src_ref[i] = n_assign + (i & (MOE_BLK - 1))
        return carry

    lax.fori_loop(0, MOE_BLK, fill, 0, unroll=8)
    for e in range(N_EXPERTS):
        first = MOE_BLK + (pend_ref[e - 1] if e else 0)
        lax.fori_loop(first + cnt_ref[e], MOE_BLK + pend_ref[e], fill, 0)
    lax.fori_loop(MOE_BLK + pend_ref[N_EXPERTS - 1], n_rows + 2 * MOE_BLK, fill, 0)

    def put(n, carry):
        src_ref[dest_ref[n] + MOE_BLK] = n
        return carry

    lax.fori_loop(0, n_assign, put, 0, unroll=8)


def _row_map(dest_flat, cnt, pends, n_rows):
    n_assign = dest_flat.shape[0]
    return pl.pallas_call(
        functools.partial(_row_map_kernel, n_assign=n_assign, n_rows=n_rows),
        grid_spec=pltpu.PrefetchScalarGridSpec(
            num_scalar_prefetch=3, grid=(1,), in_specs=[],
            out_specs=pl.BlockSpec(memory_space=pltpu.SMEM)),
        out_shape=jax.ShapeDtypeStruct((n_rows + 2 * MOE_BLK,), jnp.int32),
        compiler_params=_cparams(("arbitrary",)),
        name="moe_row_map",
    )(dest_flat, cnt, pends)


def _expert_kernel(blk_e_ref, n_used_ref, src_ref, hn_ref, wg_ref, wu_ref, wd_ref, y2_ref,
                   xbuf, ybuf, wg_s, wu_s, wd_s, gsem, ssem, *, n_tok):
    b = pl.program_id(0)
    n_used = n_used_ref[0]
    cur = b % 2
    nxt = 1 - cur

    def gather(blk, slot, i):
        row = src_ref[(blk + 1) * MOE_BLK + i] & (n_tok - 1)
        return pltpu.make_async_copy(hn_ref.at[pl.ds(row, 1)], xbuf.at[slot, pl.ds(i, 1)], gsem.at[slot])

    def scatter(blk, slot, i):
        row = src_ref[(blk + 1) * MOE_BLK + i]
        return pltpu.make_async_copy(ybuf.at[slot, pl.ds(i, 1)], y2_ref.at[pl.ds(row, 1)], ssem.at[slot])

    wait_block = lambda buf, sem, slot: pltpu.make_async_copy(buf.at[slot], buf.at[slot], sem.at[slot]).wait()

    @pl.when(b == 0)
    def _():
        ybuf[1] = jnp.zeros(ybuf.shape[1:], ybuf.dtype)
        for i in range(MOE_BLK):
            gather(0, 0, i).start()

    @pl.when(b < n_used)
    def _():
        new_expert = jnp.logical_or(b == 0, blk_e_ref[b] != blk_e_ref[jnp.maximum(b - 1, 0)])

        @pl.when(new_expert)
        def _():
            wg_s[...] = wg_ref[...].astype(BF16)
            wu_s[...] = wu_ref[...].astype(BF16)
            wd_s[...] = wd_ref[...].astype(BF16)

        wait_block(xbuf, gsem, cur)

        @pl.when(b >= 1)
        def _():
            wait_block(ybuf, ssem, cur)

        for i in range(MOE_BLK):
            gather(b + 1, nxt, i).start()
            scatter(b - 1, nxt, i).start()
        x = xbuf[cur].astype(BF16)
        g = jnp.dot(x, wg_s[...], preferred_element_type=F32)
        u = jnp.dot(x, wu_s[...], preferred_element_type=F32)
        hb = (g * jax.nn.sigmoid(g) * u).astype(BF16)
        ybuf[cur] = jnp.dot(hb, wd_s[...], preferred_element_type=F32)

    @pl.when(b == n_used)
    def _():
        wait_block(xbuf, gsem, cur)
        wait_block(ybuf, ssem, cur)
        for i in range(MOE_BLK):
            scatter(b - 1, nxt, i).start()
        wait_block(ybuf, ssem, nxt)


def _expert_ffn(hn, src, blk_e, n_used, w_g, w_u, w_d, layer):
    n_tok = hn.shape[0]
    assert n_tok & (n_tok - 1) == 0, "dump-row aliasing masks the token index with T - 1"
    n_blocks = (src.shape[0] - 2 * MOE_BLK) // MOE_BLK
    last = lambda b, n: jnp.maximum(jnp.minimum(b, n[0] - 1), 0)
    w_spec = lambda rows, cols: pl.BlockSpec((None, None, rows, cols), lambda b, e, n, s: (layer, e[last(b, n)], 0, 0))
    return pl.pallas_call(
        functools.partial(_expert_kernel, n_tok=n_tok),
        grid_spec=pltpu.PrefetchScalarGridSpec(
            num_scalar_prefetch=3,
            grid=(n_blocks + 1,),
            in_specs=[pl.BlockSpec(memory_space=pltpu.HBM),
                      w_spec(D_MODEL, D_EXPERT), w_spec(D_MODEL, D_EXPERT), w_spec(D_EXPERT, D_MODEL)],
            out_specs=pl.BlockSpec(memory_space=pltpu.HBM),
            scratch_shapes=[pltpu.VMEM((2, MOE_BLK, D_MODEL), F32), pltpu.VMEM((2, MOE_BLK, D_MODEL), F32),
                            pltpu.VMEM((D_MODEL, D_EXPERT), BF16), pltpu.VMEM((D_MODEL, D_EXPERT), BF16),
                            pltpu.VMEM((D_EXPERT, D_MODEL), BF16),
                            pltpu.SemaphoreType.DMA((2,)), pltpu.SemaphoreType.DMA((2,))]),
        out_shape=jax.ShapeDtypeStruct((2 * n_tok + MOE_BLK, D_MODEL), F32),
        compiler_params=_cparams(("arbitrary",)),
        name="moe_experts",
    )(blk_e, n_used, src, hn, w_g, w_u, w_d)


def _moe_add_kernel(h_ref, gate_ref, y0_ref, y1_ref, o_ref):
    gate = gate_ref[...]
    o_ref[...] = h_ref[...] + gate[:, 0:1] * y0_ref[...] + gate[:, 1:2] * y1_ref[...]


def _moe_specs(n_tok):
    slots = n_tok // TOK_TILE
    return [pl.BlockSpec((TOK_TILE, D_MODEL), lambda i: (i, 0)), pl.BlockSpec((TOK_TILE, LANES), lambda i: (i, 0)),
            pl.BlockSpec((TOK_TILE, D_MODEL), lambda i: (i, 0)), pl.BlockSpec((TOK_TILE, D_MODEL), lambda i: (slots + i, 0))]


def _moe_add(h, gate, y2):
    n_tok = h.shape[0]
    return pl.pallas_call(
        _moe_add_kernel,
        grid=(n_tok // TOK_TILE,),
        in_specs=_moe_specs(n_tok),
        out_specs=pl.BlockSpec((TOK_TILE, D_MODEL), lambda i: (i, 0)),
        out_shape=jax.ShapeDtypeStruct((n_tok, D_MODEL), F32),
        compiler_params=_cparams(("parallel",)),
        name="moe_add",
    )(h, gate, y2, y2)


def _moe(hn, ids, counts, w_g, w_u, w_d, layer):
    n_tok = hn.shape[0]
    n_assign = 2 * n_tok
    n_blocks = n_assign // MOE_BLK + N_EXPERTS
    n_rows = n_blocks * MOE_BLK
    cnt = counts[0, :N_EXPERTS].astype(jnp.int32)
    padded = (cnt + MOE_BLK - 1) // MOE_BLK * MOE_BLK
    pends = jnp.cumsum(padded).astype(jnp.int32)
    pstarts = pends - padded
    ids_t = ids[:, 0:4].T
    first_row = sum(jnp.where(ids_t[0:2] == e, pstarts[e], 0) for e in range(N_EXPERTS))
    dest = (first_row + ids_t[2:4]).reshape(n_assign)
    blk_start = jnp.arange(n_blocks + 1, dtype=jnp.int32) * MOE_BLK
    blk_e = jnp.minimum(jnp.sum(pends[None, :] <= blk_start[:, None], axis=1), N_EXPERTS - 1).astype(jnp.int32)
    n_used = (pends[-1:] // MOE_BLK).astype(jnp.int32)
    src = _row_map(dest, cnt, pends, n_rows)
    return _expert_ffn(hn, src, blk_e, n_used, w_g, w_u, w_d, layer)


def kernel(x, norm_mix, norm_ffn, w_in_even, ssm_a_re, ssm_a_im, ssm_b_re, ssm_b_im, ssm_c_re, ssm_c_im, ssm_d,
           ssm_log_step, w_glu, b_glu, q_norm, k_norm, w_out_even, w_in_conv, conv_w, w_out_conv, w_router_group,
           b_router_group, w_router_expert, b_router_expert, w_expert_gate, w_expert_up, w_expert_down):
    bsz, s_len, d = x.shape
    x2 = x.reshape(bsz * s_len, d)
    route = lambda layer: _router_operands(norm_ffn[layer], w_router_group[layer], b_router_group[layer],
                                           w_router_expert[layer], b_router_expert[layer])
    experts = lambda layer: (w_expert_gate, w_expert_up, w_expert_down, layer)

    u, q, k, v = _inproj_even(x2, norm_mix[0], w_in_even[0], q_norm[0], k_norm[0])
    tables = _s5_tables(ssm_a_re[0], ssm_a_im[0], ssm_b_re[0], ssm_b_im[0], ssm_c_re[0], ssm_c_im[0], ssm_d[0],
                        ssm_log_step[0])
    y_pre = _s5_core(u, tables, bsz, s_len)
    attn = _dilated_attention(q, k, v, bsz, s_len)
    h, hn, ids, gate, counts = _outproj_even(x2, y_pre, attn, w_glu[0], b_glu[0], w_out_even[0], route(0))
    y2 = _moe(hn, ids, counts, *experts(0))

    h, hn, ids, gate, counts = _conv_layer(h, gate, y2, norm_mix[1], w_in_conv[0], conv_w[0], w_out_conv[0],
                                           route(1), s_len)
    y2 = _moe(hn, ids, counts, *experts(1))
    return _moe_add(h, gate, y2).reshape(bsz, s_len, d)
```

```python
import functools
import math

import jax
import jax.numpy as jnp
from jax import lax
from jax.experimental import pallas as pl
from jax.experimental.pallas import tpu as pltpu

F32 = jnp.float32
BF16 = jnp.bfloat16

D_MODEL = 1024
SSM_GROUP = 16
SSM_GROUPS = 40
SSM_WIDTH = SSM_GROUP * SSM_GROUPS
SSM_STATE = 64
ATT_HEADS = 6
ATT_HEAD_DIM = 64
ATT_WIDTH = ATT_HEADS * ATT_HEAD_DIM
DILATIONS = (1, 4, 16)
ATT_BLK = 128
CONV_TAPS = 3
N_GROUPS = 4
EXPERTS_PER_GROUP = 8
N_EXPERTS = N_GROUPS * EXPERTS_PER_GROUP
D_EXPERT = 512
MOE_BLK = 256
RMS_EPS = 1e-6
NEG_INF = -1e30

LANES = 128
SUBLANES = 8
VMEM_LIMIT = 56 * 1024 * 1024

TOK_TILE = 512
SSM_CHUNK = 8
SSM_SLAB_GROUPS = LANES // SSM_GROUP
SSM_SLABS = SSM_WIDTH // LANES
S5_ROW_TILE = 256
ATT_SB = 2048
ROW_CHUNKS = D_MODEL // LANES


def _cparams(sem):
    return pltpu.CompilerParams(dimension_semantics=sem, vmem_limit_bytes=VMEM_LIMIT)


def _rms(x, gain):
    return x * lax.rsqrt(jnp.mean(x * x, axis=-1, keepdims=True) + RMS_EPS) * gain


def _load_row_tiles(ref, lead=()):
    n = ref.shape[-2] // ROW_CHUNKS
    return jnp.concatenate([ref[lead + (pl.ds(c, n, stride=ROW_CHUNKS), slice(None))] for c in range(ROW_CHUNKS)], axis=1)


def _store_row_tiles(ref, value, lead=()):
    n = value.shape[0]
    for c in range(ROW_CHUNKS):
        ref[lead + (pl.ds(c, n, stride=ROW_CHUNKS), slice(None))] = value[:, c * LANES:(c + 1) * LANES]


def _head_norm(t, gain, bd):
    tt = t * t
    hi = tt.astype(BF16)
    lo = (tt - hi.astype(F32)).astype(BF16)
    ss = jnp.dot(hi, bd, preferred_element_type=F32) + jnp.dot(lo, bd, preferred_element_type=F32)
    return t * lax.rsqrt(ss * (1.0 / ATT_HEAD_DIM) + RMS_EPS) * gain


def _inproj_even_kernel(x_ref, g_ref, w_ref, bd_ref, qn_ref, kn_ref, u_ref, q_ref, k_ref, v_ref):
    hn = _rms(x_ref[...], g_ref[...]).astype(BF16)
    proj = jnp.dot(hn, w_ref[...], preferred_element_type=F32)
    for j in range(SSM_SLABS):
        u_ref[j] = proj[:, j * LANES:(j + 1) * LANES]
    bd = bd_ref[...]
    o = SSM_WIDTH
    q = _head_norm(proj[:, o:o + ATT_WIDTH], qn_ref[...], bd) * (ATT_HEAD_DIM ** -0.5)
    k = _head_norm(proj[:, o + ATT_WIDTH:o + 2 * ATT_WIDTH], kn_ref[...], bd)
    v = proj[:, o + 2 * ATT_WIDTH:o + 3 * ATT_WIDTH]
    for j in range(ATT_WIDTH // LANES):
        q_ref[j] = q[:, j * LANES:(j + 1) * LANES]
        k_ref[j] = k[:, j * LANES:(j + 1) * LANES]
        v_ref[j] = v[:, j * LANES:(j + 1) * LANES]


def _inproj_even(x2, gain, w_in, q_norm, k_norm):
    n_tok = x2.shape[0]
    n_slab = ATT_WIDTH // LANES
    head_of = jnp.arange(ATT_WIDTH) // ATT_HEAD_DIM
    bd = (head_of[:, None] == head_of[None, :]).astype(BF16)
    qn = jnp.tile(q_norm.astype(F32), ATT_HEADS)[None]
    kn = jnp.tile(k_norm.astype(F32), ATT_HEADS)[None]
    full = lambda shape: pl.BlockSpec(shape, lambda i: (0,) * len(shape))
    slab = pl.BlockSpec((n_slab, TOK_TILE, LANES), lambda i: (0, i, 0))
    slab_shape = jax.ShapeDtypeStruct((n_slab, n_tok, LANES), F32)
    return pl.pallas_call(
        _inproj_even_kernel,
        grid=(n_tok // TOK_TILE,),
        in_specs=[pl.BlockSpec((TOK_TILE, D_MODEL), lambda i: (i, 0)), full((1, D_MODEL)),
                  full(w_in.shape), full(bd.shape), full(qn.shape), full(kn.shape)],
        out_specs=[pl.BlockSpec((SSM_SLABS, TOK_TILE, LANES), lambda i: (0, i, 0)), slab, slab, slab],
        out_shape=[jax.ShapeDtypeStruct((SSM_SLABS, n_tok, LANES), F32), slab_shape, slab_shape, slab_shape],
        compiler_params=_cparams(("parallel",)),
        name="inproj_even",
    )(x2, gain[None].astype(F32), w_in.astype(BF16), bd, qn, kn)


def _s5_tables(a_re, a_im, b_re, b_im, c_re, c_im, d_skip, log_step):
    f = lambda t: t.astype(F32)
    a_re, a_im, b_re, b_im, c_re, c_im = map(f, (a_re, a_im, b_re, b_im, c_re, c_im))
    L = SSM_CHUNK
    step = jnp.exp(f(log_step))[:, None]
    ks = jnp.arange(L + 1, dtype=F32)[:, None, None]
    mag = jnp.exp(ks * (a_re * step)[None])
    ang = ks * (a_im * step)[None]
    pw_re, pw_im = mag * jnp.cos(ang), mag * jnp.sin(ang)
    nr, ni = pw_re[1] - 1.0, pw_im[1]
    den = a_re * a_re + a_im * a_im
    z_re, z_im = (nr * a_re + ni * a_im) / den, (ni * a_re - nr * a_im) / den
    bb_re = z_re[..., None] * b_re - z_im[..., None] * b_im
    bb_im = z_re[..., None] * b_im + z_im[..., None] * b_re
    lb_re = pw_re[..., None] * bb_re[None] - pw_im[..., None] * bb_im[None]
    lb_im = pw_re[..., None] * bb_im[None] + pw_im[..., None] * bb_re[None]
    kk = jnp.einsum('gop,kgpi->gkio', c_re, lb_re[:L]) - jnp.einsum('gop,kgpi->gkio', c_im, lb_im[:L])
    ti = jnp.arange(L)
    lag = ti[None, :] - ti[:, None]
    m = jnp.where((lag >= 0)[None, :, :, None, None], kk[:, jnp.maximum(lag, 0)], 0.0)
    ns, gs = SSM_SLABS, SSM_SLAB_GROUPS
    lw = L * LANES
    pm = m.reshape(ns, gs, L, L, SSM_GROUP, SSM_GROUP).transpose(0, 2, 1, 4, 3, 5).reshape(ns, lw, L * SSM_GROUP)
    fold_e = lambda t: t[:L][::-1].reshape(L, ns, gs, SSM_STATE, SSM_GROUP).transpose(1, 0, 2, 4, 3).reshape(ns, lw, SSM_STATE)
    pe = jnp.concatenate([fold_e(lb_re), fold_e(lb_im)], axis=-1)
    pw1_re, pw1_im = (t[1:].transpose(1, 0, 2)[:, :, None, :] for t in (pw_re, pw_im))
    cl_re = c_re[:, None] * pw1_re - c_im[:, None] * pw1_im
    cl_im = c_re[:, None] * pw1_im + c_im[:, None] * pw1_re
    fold_f = lambda t: t.reshape(ns, gs, L, SSM_GROUP, SSM_STATE).transpose(0, 4, 2, 1, 3).reshape(ns, SSM_STATE, lw)
    pf = jnp.concatenate([fold_f(cl_re), fold_f(-cl_im)], axis=1)
    per_chain = lambda t: jnp.tile(t.reshape(ns, gs * SSM_STATE // LANES, LANES), (1, 2, 1))
    d_vec = jnp.tile(f(d_skip).reshape(ns, 1, LANES), (1, 1, L))
    return pm.astype(BF16), pe.astype(BF16), pf.astype(BF16), per_chain(pw_re[L]), per_chain(pw_im[L]), d_vec


def _iota2(shape):
    return lax.broadcasted_iota(jnp.int32, shape, 0), lax.broadcasted_iota(jnp.int32, shape, 1)


def _widen(compact, group_major_cols, shape, r_shift, c_shift, sel_rows):
    gmask = SSM_SLAB_GROUPS - 1
    if sel_rows:
        r, c = _iota2((shape[0], compact.shape[0]))
        sel = ((r >> 9) == (c >> 6)) & ((r & (SSM_STATE - 1)) == (c & (SSM_STATE - 1)))
        wide = jnp.dot(sel.astype(BF16), compact, preferred_element_type=F32)
    else:
        r, c = _iota2((compact.shape[1], shape[1]))
        if group_major_cols:
            sel = ((c >> 9) == (r >> 6)) & ((c & (SSM_STATE - 1)) == (r & (SSM_STATE - 1)))
        else:
            sel = ((c >> 7) == (r >> 4)) & ((c & (SSM_GROUP - 1)) == (r & (SSM_GROUP - 1)))
        wide = jnp.dot(compact, sel.astype(BF16), preferred_element_type=F32)
    r, c = _iota2(shape)
    keep = ((r >> r_shift) & gmask) == ((c >> c_shift) & gmask)
    return jnp.where(keep, wide, 0.0).astype(BF16)


def _s5_kernel(u_ref, pm_ref, pe_ref, pf_ref, ar_ref, ai_ref, d_ref, y_ref, m_ref, e_ref, f_ref, x_ref, sr_ref, si_ref,
               *, n_chunk, pitch):
    L = SSM_CHUNK
    n_blk = SSM_SLAB_GROUPS * SSM_STATE // LANES
    n_re = n_blk * LANES
    lw = L * LANES
    m_ref[...] = _widen(pm_ref[...], False, (lw, lw), 4, 4, False)
    e_ref[...] = _widen(pe_ref[...], True, (lw, 2 * n_re), 4, 6, False)
    f_ref[...] = _widen(pf_ref[...], False, (2 * n_re, lw), 6, 4, True)
    tiles = [(b, c0) for b in range(2) for c0 in range(0, n_chunk, S5_ROW_TILE)]
    for b, c0 in tiles:
        r0 = b * n_chunk + c0
        for t in range(L):
            x_ref[r0:r0 + S5_ROW_TILE, t * LANES:(t + 1) * LANES] = (
                u_ref[pl.ds(r0 * L + t, S5_ROW_TILE, stride=L), :].astype(BF16))
        sl = jnp.dot(x_ref[r0:r0 + S5_ROW_TILE, :], e_ref[...], preferred_element_type=F32)
        for j in range(n_blk):
            base = (b * n_blk + j) * pitch + c0
            sr_ref[base:base + S5_ROW_TILE, :] = sl[:, j * LANES:(j + 1) * LANES]
            si_ref[base:base + S5_ROW_TILE, :] = sl[:, n_re + j * LANES:n_re + (j + 1) * LANES]
    ar, ai = ar_ref[...], ai_ref[...]
    half = LANES

    def scan_step(c, carry):
        s_re, s_im = carry
        rows = pl.ds(c, SUBLANES, stride=pitch)
        x_re, x_im = sr_ref[rows, :], si_ref[rows, :]
        sr_ref[rows, :] = s_re
        si_ref[rows, :] = s_im
        return ar * s_re - ai * s_im + x_re, ar * s_im + ai * s_re + x_im

    zero = jnp.zeros((SUBLANES, half), F32)
    lax.fori_loop(0, n_chunk, scan_step, (zero, zero), unroll=8)

    for b, c0 in tiles:
        r0 = b * n_chunk + c0
        chain = lambda ref, j: ref[(b * n_blk + j) * pitch + c0:(b * n_blk + j) * pitch + c0 + S5_ROW_TILE, :]
        sp = jnp.concatenate([chain(sr_ref, j) for j in range(n_blk)] + [chain(si_ref, j) for j in range(n_blk)],
                             axis=1).astype(BF16)
        xt = x_ref[r0:r0 + S5_ROW_TILE, :]
        y = (jnp.dot(xt, m_ref[...], preferred_element_type=F32)
             + jnp.dot(sp, f_ref[...], preferred_element_type=F32)
             + d_ref[...] * xt.astype(F32))
        for t in range(L):
            y_ref[pl.ds(r0 * L + t, S5_ROW_TILE, stride=L), :] = y[:, t * LANES:(t + 1) * LANES]


def _s5_core(u, tables, bsz, s_len):
    assert bsz == 2, "the scan packs (batch, lane block) into the 8 sublanes of one vreg"
    pm, pe, pf, a_r, a_i, d_vec = tables
    n_tok = bsz * s_len
    n_chunk = s_len // SSM_CHUNK
    pitch = n_chunk + SUBLANES
    lw = SSM_CHUNK * LANES
    n_state = 2 * SSM_SLAB_GROUPS * SSM_STATE
    slab = lambda shape, **kw: pl.BlockSpec((None,) + shape, lambda i: (i,) + (0,) * len(shape), **kw)
    once = dict(pipeline_mode=pl.Buffered(1))
    return pl.pallas_call(
        functools.partial(_s5_kernel, n_chunk=n_chunk, pitch=pitch),
        grid=(SSM_SLABS,),
        in_specs=[slab((n_tok, LANES), **once), slab(pm.shape[1:]), slab(pe.shape[1:]), slab(pf.shape[1:]),
                  slab((SUBLANES, LANES)), slab((SUBLANES, LANES)), slab((1, lw))],
        out_specs=slab((n_tok, LANES), **once),
        out_shape=jax.ShapeDtypeStruct((SSM_SLABS, n_tok, LANES), F32),
        scratch_shapes=[pltpu.VMEM((lw, lw), BF16), pltpu.VMEM((lw, n_state), BF16), pltpu.VMEM((n_state, lw), BF16),
                        pltpu.VMEM((bsz * n_chunk, lw), BF16),
                        pltpu.VMEM((SUBLANES * pitch, LANES), F32),
                        pltpu.VMEM((SUBLANES * pitch, LANES), F32)],
        compiler_params=_cparams(("parallel",)),
        name="s5_core",
    )(u, pm, pe, pf, a_r, a_i, d_vec)


def _attn_kernel(slope_ref, q_ref, kp_ref, kc_ref, vp_ref, vc_ref, o_ref, kk_ref, vv_ref, m_ref, l_ref, acc_ref):
    slab = pl.program_id(1)
    sb = pl.program_id(2)
    kk_ref[0:ATT_SB, :] = kp_ref[...]
    kk_ref[ATT_SB:2 * ATT_SB, :] = kc_ref[...]
    vv_ref[0:ATT_SB, :] = vp_ref[...]
    vv_ref[ATT_SB:2 * ATT_SB, :] = vc_ref[...]

    lane = lax.broadcasted_iota(jnp.int32, (ATT_BLK, LANES), 1)
    head0 = lane < ATT_HEAD_DIM
    qi = lax.broadcasted_iota(jnp.int32, (ATT_BLK, 2 * ATT_BLK), 0)
    kj = lax.broadcasted_iota(jnp.int32, (ATT_BLK, 2 * ATT_BLK), 1)
    back = qi + ATT_BLK - kj
    band = (back >= 0) & (back <= ATT_BLK)
    neg_steps = -back.astype(F32)
    slopes = (slope_ref[2 * slab], slope_ref[2 * slab + 1])

    for pat, dil in enumerate(DILATIONS):
        span = ATT_BLK * dil

        def tile(idx, carry, dil=dil, span=span, pat=pat):
            start = (idx // dil) * span + idx % dil
            seq_ok = jnp.logical_or(sb > 0, idx >= dil)
            valid = band & ((kj >= ATT_BLK) | seq_ok)
            if dil == 1:
                start = pl.multiple_of(start, ATT_BLK)
                rows_of = lambda first, n: pl.ds(first, n)
            else:
                rows_of = lambda first, n: pl.ds(first, n, stride=dil)
            rows = rows_of(start, ATT_BLK)
            qt = q_ref[rows, :]
            kt = kk_ref[rows_of(ATT_SB + start - span, 2 * ATT_BLK), :].astype(BF16)
            vt = vv_ref[rows_of(ATT_SB + start - span, 2 * ATT_BLK), :].astype(BF16)
            q0 = jnp.where(head0, qt, 0.0)
            parts = []
            for hh, qh in enumerate((q0, qt - q0)):
                s = lax.dot_general(qh.astype(BF16), kt, (((1,), (1,)), ((), ())), preferred_element_type=F32)
                s = jnp.where(valid, s + (slopes[hh] * float(dil)) * neg_steps, NEG_INF)
                m = jnp.max(s, axis=-1, keepdims=True)
                p = jnp.exp(s - m)
                l = jnp.sum(p, axis=-1, keepdims=True)
                o = jnp.dot(p.astype(BF16), vt, preferred_element_type=F32)
                parts.append((m, l, o))
            (m0, l0, o0), (m1, l1, o1) = parts
            m_t = jnp.where(head0, m0, m1)
            l_t = jnp.where(head0, l0, l1)
            o_t = jnp.where(head0, o0, o1)
            if pat == 0:
                m_ref[rows, :] = m_t
                l_ref[rows, :] = l_t
                acc_ref[rows, :] = o_t
            else:
                m_old = m_ref[rows, :]
                m_new = jnp.maximum(m_old, m_t)
                a = jnp.exp(m_old - m_new)
                b = jnp.exp(m_t - m_new)
                m_ref[rows, :] = m_new
                l_ref[rows, :] = a * l_ref[rows, :] + b * l_t
                acc_ref[rows, :] = a * acc_ref[rows, :] + b * o_t
            return carry

        lax.fori_loop(0, ATT_SB // ATT_BLK, tile, 0, unroll=4)

    o_ref[...] = (acc_ref[...] / l_ref[...]).astype(o_ref.dtype)


def _dilated_attention(q, k, v, bsz, s_len):
    n_slab = q.shape[0]
    shape4 = (n_slab, bsz, s_len, LANES)
    q, k, v = (t.reshape(shape4) for t in (q, k, v))
    slopes = jnp.asarray([2.0 ** (-8.0 * (h + 1) / ATT_HEADS) for h in range(ATT_HEADS)], F32)
    blk = (None, None, ATT_SB, LANES)
    cur = pl.BlockSpec(blk, lambda b, j, i, s: (j, b, i, 0))
    prev = pl.BlockSpec(blk, lambda b, j, i, s: (j, b, jnp.maximum(i - 1, 0), 0))
    out = pl.pallas_call(
        _attn_kernel,
        grid_spec=pltpu.PrefetchScalarGridSpec(
            num_scalar_prefetch=1,
            grid=(bsz, n_slab, s_len // ATT_SB),
            in_specs=[cur, prev, cur, prev, cur],
            out_specs=cur,
            scratch_shapes=[pltpu.VMEM((2 * ATT_SB, LANES), F32), pltpu.VMEM((2 * ATT_SB, LANES), F32),
                            pltpu.VMEM((ATT_SB, LANES), F32), pltpu.VMEM((ATT_SB, LANES), F32),
                            pltpu.VMEM((ATT_SB, LANES), F32)]),
        out_shape=jax.ShapeDtypeStruct(shape4, BF16),
        compiler_params=_cparams(("parallel", "parallel", "parallel")),
        name="dilated_attn",
    )(slopes, q, k, k, v, v)
    return out.reshape(n_slab, bsz * s_len, LANES)


def _route_epilogue(h, gain_ref, wr_ref, br_ref, tri_ref, cnt_ref, h_ref, hn_ref, ids_ref, gate_ref, cnt_out_ref):
    tm = h.shape[0]
    h_ref[...] = h
    hn = _rms(h, gain_ref[...])
    _store_row_tiles(hn_ref, hn)
    hn_hi = hn.astype(BF16)
    hn_lo = (hn - hn_hi.astype(F32)).astype(BF16)
    logits = (jnp.dot(hn_hi, wr_ref[0], preferred_element_type=F32)
              + (jnp.dot(hn_hi, wr_ref[1], preferred_element_type=F32)
                 + jnp.dot(hn_lo, wr_ref[0], preferred_element_type=F32))) + br_ref[...]
    lane = lax.broadcasted_iota(jnp.int32, (tm, LANES), 1)
    big = jnp.int32(LANES)
    rmax = lambda t: jnp.max(t, axis=-1, keepdims=True)
    rmin = lambda t: jnp.min(t, axis=-1, keepdims=True)
    rsum = lambda t: jnp.sum(t, axis=-1, keepdims=True)
    gmask = lane < N_GROUPS
    gl = jnp.where(gmask, logits, -jnp.inf)
    gmax = rmax(gl)
    ge = jnp.where(gmask, jnp.exp(gl - gmax), 0.0)
    gprob = ge / rsum(ge)
    g_w = rmax(gprob)
    grp = rmin(jnp.where(gmask & (gprob == g_w), lane, big))
    group_of_lane = (lane - N_GROUPS) >> int(math.log2(EXPERTS_PER_GROUP))
    emask = (lane >= N_GROUPS) & (lane < N_GROUPS + N_EXPERTS) & (group_of_lane == grp)
    el = jnp.where(emask, logits, -jnp.inf)
    ee = jnp.where(emask, jnp.exp(el - rmax(el)), 0.0)
    ep = jnp.where(emask, ee / rsum(ee), -1.0)
    p1 = rmax(ep)
    i1 = rmin(jnp.where(ep == p1, lane, big))
    ep2 = jnp.where(lane == i1, -1.0, ep)
    p2 = rmax(ep2)
    i2 = rmin(jnp.where(ep2 == p2, lane, big))
    e1, e2 = i1 - N_GROUPS, i2 - N_GROUPS
    psum = p1 + p2
    gate1, gate2 = g_w * p1 / psum, g_w * p2 / psum
    oh1, oh2 = lane == e1, lane == e2
    member = (oh1 | oh2).astype(BF16)
    before = jnp.dot(tri_ref[...], member, preferred_element_type=F32) + cnt_ref[...]
    r1 = rsum(jnp.where(oh1, before, 0.0)).astype(jnp.int32)
    r2 = rsum(jnp.where(oh2, before, 0.0)).astype(jnp.int32)
    cnt_ref[...] = cnt_ref[...] + jnp.sum(member.astype(F32), axis=0, keepdims=True)
    ids_ref[...] = jnp.where(lane == 0, e1, jnp.where(lane == 1, e2, jnp.where(lane == 2, r1, jnp.where(lane == 3, r2, 0))))
    gate_ref[...] = jnp.where(lane == 0, gate1, jnp.where(lane == 1, gate2, 0.0))
    cnt_out_ref[...] = jnp.broadcast_to(cnt_ref[...], cnt_out_ref.shape)


def _router_operands(norm_gain, w_rg, b_rg, w_re, b_re):
    pad = LANES - N_GROUPS - N_EXPERTS
    wr = jnp.pad(jnp.concatenate([w_rg, w_re], axis=1).astype(F32), ((0, 0), (0, pad)))
    br = jnp.pad(jnp.concatenate([b_rg, b_re]).astype(F32), (0, pad))[None]
    r = jnp.arange(TOK_TILE)
    tri = (r[None, :] < r[:, None]).astype(BF16)
    wr_hi = wr.astype(BF16)
    wr_lo = (wr - wr_hi.astype(F32)).astype(BF16)
    return norm_gain[None].astype(F32), jnp.stack([wr_hi, wr_lo]), br, tri


def _route_specs(n_tok):
    full = lambda shape: pl.BlockSpec(shape, lambda i: (0,) * len(shape))
    in_specs = [full((1, D_MODEL)), full((2, D_MODEL, LANES)), full((1, LANES)), full((TOK_TILE, TOK_TILE))]
    tok = lambda w: pl.BlockSpec((TOK_TILE, w), lambda i: (i, 0))
    out_specs = [tok(D_MODEL), pl.BlockSpec((TOK_TILE * ROW_CHUNKS, LANES), lambda i: (i, 0)), tok(LANES), tok(LANES),
                 full((SUBLANES, LANES))]
    out_shape = [jax.ShapeDtypeStruct((n_tok, D_MODEL), F32), jax.ShapeDtypeStruct((n_tok * ROW_CHUNKS, LANES), F32),
                 jax.ShapeDtypeStruct((n_tok, LANES), jnp.int32), jax.ShapeDtypeStruct((n_tok, LANES), F32),
                 jax.ShapeDtypeStruct((SUBLANES, LANES), F32)]
    return in_specs, out_specs, out_shape


def _gelu_tanh(x):
    return 0.5 * x * (1.0 + jnp.tanh(math.sqrt(2.0 / math.pi) * (x + 0.044715 * (x * x * x))))


def _outproj_even_kernel(x_ref, y_ref, a_ref, wglu_ref, bglu_ref, wout_ref, gain_ref, wr_ref, br_ref, tri_ref,
                         h_ref, hn_ref, ids_ref, gate_ref, cnt_out_ref, cnt_ref):
    @pl.when(pl.program_id(0) == 0)
    def _():
        cnt_ref[...] = jnp.zeros_like(cnt_ref)

    y = _gelu_tanh(jnp.concatenate([y_ref[j] for j in range(SSM_SLABS)], axis=1))
    y = y * jax.nn.sigmoid(jnp.dot(y.astype(BF16), wglu_ref[...], preferred_element_type=F32) + bglu_ref[...])
    mix = jnp.dot(y.astype(BF16), wout_ref[0:SSM_WIDTH, :], preferred_element_type=F32)
    for j in range(ATT_WIDTH // LANES):
        rows = slice(SSM_WIDTH + j * LANES, SSM_WIDTH + (j + 1) * LANES)
        mix = mix + jnp.dot(a_ref[j], wout_ref[rows, :], preferred_element_type=F32)
    _route_epilogue(x_ref[...] + mix, gain_ref, wr_ref, br_ref, tri_ref, cnt_ref,
                    h_ref, hn_ref, ids_ref, gate_ref, cnt_out_ref)


def _outproj_even(x2, y_pre, attn, w_glu, b_glu, w_out, route_ops):
    n_tok = x2.shape[0]
    n_slab = attn.shape[0]
    r_in, r_out, r_shape = _route_specs(n_tok)
    full = lambda shape: pl.BlockSpec(shape, lambda i: (0,) * len(shape))
    return pl.pallas_call(
        _outproj_even_kernel,
        grid=(n_tok // TOK_TILE,),
        in_specs=[pl.BlockSpec((TOK_TILE, D_MODEL), lambda i: (i, 0)),
                  pl.BlockSpec((SSM_SLABS, TOK_TILE, LANES), lambda i: (0, i, 0)),
                  pl.BlockSpec((n_slab, TOK_TILE, LANES), lambda i: (0, i, 0)),
                  full(w_glu.shape), full((1, SSM_WIDTH)), full(w_out.shape)] + r_in,
        out_specs=r_out, out_shape=r_shape,
        scratch_shapes=[pltpu.VMEM((1, LANES), F32)],
        compiler_params=_cparams(("arbitrary",)),
        name="outproj_even",
    )(x2, y_pre, attn, w_glu.astype(BF16), b_glu[None].astype(F32), w_out.astype(BF16), *route_ops)


def _conv_layer_kernel(h_ref, pgate_ref, y0_ref, y1_ref, gmix_ref, win_ref, cw_ref, wout_ref, gain_ref, wr_ref, br_ref,
                       tri_ref, ho_ref, hn_ref, ids_ref, gate_ref, cnt_out_ref, cnt_ref, zc_ref, *, tiles_per_seq):
    i = pl.program_id(0)

    @pl.when(i == 0)
    def _():
        cnt_ref[...] = jnp.zeros_like(cnt_ref)

    @pl.when(i % tiles_per_seq == 0)
    def _():
        zc_ref[0:SUBLANES, :] = jnp.zeros((SUBLANES, D_MODEL), F32)

    pgate = pgate_ref[...]
    h = h_ref[...] + pgate[:, 0:1] * _load_row_tiles(y0_ref) + pgate[:, 1:2] * _load_row_tiles(y1_ref)
    tm = h.shape[0]
    hn = _rms(h, gmix_ref[...]).astype(BF16)
    c = D_MODEL
    b_gate = jnp.dot(hn, win_ref[:, 0:c], preferred_element_type=F32)
    zc = jnp.dot(hn, win_ref[:, c:2 * c], preferred_element_type=F32) * jnp.dot(hn, win_ref[:, 2 * c:3 * c], preferred_element_type=F32)
    zc_ref[SUBLANES:SUBLANES + tm, :] = zc
    conv = cw_ref[CONV_TAPS - 1:CONV_TAPS, :] * zc
    for back in range(1, CONV_TAPS):
        tap = CONV_TAPS - 1 - back
        conv = conv + cw_ref[tap:tap + 1, :] * zc_ref[SUBLANES - back:SUBLANES - back + tm, :]
    zc_ref[0:SUBLANES, :] = zc_ref[tm:tm + SUBLANES, :]
    mix = jnp.dot((b_gate * conv).astype(BF16), wout_ref[...], preferred_element_type=F32)
    _route_epilogue(h + mix, gain_ref, wr_ref, br_ref, tri_ref, cnt_ref,
                    ho_ref, hn_ref, ids_ref, gate_ref, cnt_out_ref)


def _conv_layer(h1, pgate, y2, gain_mix, w_in, conv_w, w_out, route_ops, s_len):
    n_tok = h1.shape[0]
    r_in, r_out, r_shape = _route_specs(n_tok)
    full = lambda shape: pl.BlockSpec(shape, lambda i: (0,) * len(shape))
    return pl.pallas_call(
        functools.partial(_conv_layer_kernel, tiles_per_seq=s_len // TOK_TILE),
        grid=(n_tok // TOK_TILE,),
        in_specs=_moe_specs(n_tok) + [full((1, D_MODEL)), full(w_in.shape), full(conv_w.shape), full(w_out.shape)] + r_in,
        out_specs=r_out, out_shape=r_shape,
        scratch_shapes=[pltpu.VMEM((1, LANES), F32), pltpu.VMEM((TOK_TILE + SUBLANES, D_MODEL), F32)],
        compiler_params=_cparams(("arbitrary",)),
        name="conv_layer",
    )(h1, pgate, y2, y2, gain_mix[None].astype(F32), w_in.astype(BF16), conv_w.astype(F32), w_out.astype(BF16),
      *route_ops)


def _row_map_kernel(dest_ref, cnt_ref, pend_ref, src_ref, *, n_assign, n_rows):
    def fill(i, carry):
        src_ref[i] = n_assign + (i & (MOE_BLK - 1))
        return carry

    lax.fori_loop(0, MOE_BLK, fill, 0, unroll=8)
    for e in range(N_EXPERTS):
        first = MOE_BLK + (pend_ref[e - 1] if e else 0)
        lax.fori_loop(first + cnt_ref[e], MOE_BLK + pend_ref[e], fill, 0)
    lax.fori_loop(MOE_BLK + pend_ref[N_EXPERTS - 1], n_rows + 2 * MOE_BLK, fill, 0)

    def put(n, carry):
        src_ref[dest_ref[n] + MOE_BLK] = n
        return carry

    lax.fori_loop(0, n_assign, put, 0, unroll=8)


def _row_map(dest_flat, cnt, pends, n_rows):
    n_assign = dest_flat.shape[0]
    return pl.pallas_call(
        functools.partial(_row_map_kernel, n_assign=n_assign, n_rows=n_rows),
        grid_spec=pltpu.PrefetchScalarGridSpec(
            num_scalar_prefetch=3, grid=(1,), in_specs=[],
            out_specs=pl.BlockSpec(memory_space=pltpu.SMEM)),
        out_shape=jax.ShapeDtypeStruct((n_rows + 2 * MOE_BLK,), jnp.int32),
        compiler_params=_cparams(("arbitrary",)),
        name="moe_row_map",
    )(dest_flat, cnt, pends)


def _expert_kernel(blk_e_ref, n_used_ref, src_ref, hn_ref, wg_ref, wu_ref, wd_ref, y2_ref,
                   xbuf, ybuf, wg_s, wu_s, wd_s, gsem, ssem, *, n_tok):
    b = pl.program_id(0)
    n_used = n_used_ref[0]
    cur = b % 2
    nxt = 1 - cur

    tile = lambda i: pl.ds(i * ROW_CHUNKS, ROW_CHUNKS)

    def gather(blk, slot, i):
        row = src_ref[(blk + 1) * MOE_BLK + i] & (n_tok - 1)
        return pltpu.make_async_copy(hn_ref.at[row], xbuf.at[slot, tile(i)], gsem.at[slot])

    def scatter(blk, slot, i):
        row = src_ref[(blk + 1) * MOE_BLK + i]
        return pltpu.make_async_copy(ybuf.at[slot, tile(i)], y2_ref.at[row], ssem.at[slot])

    wait_block = lambda buf, sem, slot: pltpu.make_async_copy(buf.at[slot], buf.at[slot], sem.at[slot]).wait()

    @pl.when(b == 0)
    def _():
        ybuf[1] = jnp.zeros(ybuf.shape[1:], ybuf.dtype)
        for i in range(MOE_BLK):
            gather(0, 0, i).start()

    @pl.when(b < n_used)
    def _():
        new_expert = jnp.logical_or(b == 0, blk_e_ref[b] != blk_e_ref[jnp.maximum(b - 1, 0)])

        @pl.when(new_expert)
        def _():
            wg_s[...] = wg_ref[...].astype(BF16)
            wu_s[...] = wu_ref[...].astype(BF16)
            wd_s[...] = wd_ref[...].astype(BF16)

        wait_block(xbuf, gsem, cur)

        @pl.when(b >= 1)
        def _():
            wait_block(ybuf, ssem, cur)

        for i in range(MOE_BLK):
            gather(b + 1, nxt, i).start()
            scatter(b - 1, nxt, i).start()
        x = _load_row_tiles(xbuf, (cur,)).astype(BF16)
        g = jnp.dot(x, wg_s[...], preferred_element_type=F32)
        u = jnp.dot(x, wu_s[...], preferred_element_type=F32)
        hb = (g * jax.nn.sigmoid(g) * u).astype(BF16)
        _store_row_tiles(ybuf, jnp.dot(hb, wd_s[...], preferred_element_type=F32), (cur,))

    @pl.when(b == n_used)
    def _():
        wait_block(xbuf, gsem, cur)
        wait_block(ybuf, ssem, cur)
        for i in range(MOE_BLK):
            scatter(b - 1, nxt, i).start()
        wait_block(ybuf, ssem, nxt)


def _expert_ffn(hn, src, blk_e, n_used, w_g, w_u, w_d, layer):
    n_tok = hn.shape[0] // ROW_CHUNKS
    assert n_tok & (n_tok - 1) == 0, "dump-row aliasing masks the token index with T - 1"
    n_blocks = (src.shape[0] - 2 * MOE_BLK) // MOE_BLK
    last = lambda b, n: jnp.maximum(jnp.minimum(b, n[0] - 1), 0)
    w_spec = lambda rows, cols: pl.BlockSpec((None, None, rows, cols), lambda b, e, n, s: (layer, e[last(b, n)], 0, 0))
    return pl.pallas_call(
        functools.partial(_expert_kernel, n_tok=n_tok),
        grid_spec=pltpu.PrefetchScalarGridSpec(
            num_scalar_prefetch=3,
            grid=(n_blocks + 1,),
            in_specs=[pl.BlockSpec(memory_space=pltpu.HBM),
                      w_spec(D_MODEL, D_EXPERT), w_spec(D_MODEL, D_EXPERT), w_spec(D_EXPERT, D_MODEL)],
            out_specs=pl.BlockSpec(memory_space=pltpu.HBM),
            scratch_shapes=[pltpu.VMEM((2, MOE_BLK * ROW_CHUNKS, LANES), F32),
                            pltpu.VMEM((2, MOE_BLK * ROW_CHUNKS, LANES), F32),
                            pltpu.VMEM((D_MODEL, D_EXPERT), BF16), pltpu.VMEM((D_MODEL, D_EXPERT), BF16),
                            pltpu.VMEM((D_EXPERT, D_MODEL), BF16),
                            pltpu.SemaphoreType.DMA((2,)), pltpu.SemaphoreType.DMA((2,))]),
        out_shape=jax.ShapeDtypeStruct((2 * n_tok + MOE_BLK, ROW_CHUNKS, LANES), F32),
        compiler_params=_cparams(("arbitrary",)),
        name="moe_experts",
    )(blk_e, n_used, src, hn.reshape(n_tok, ROW_CHUNKS, LANES), w_g, w_u, w_d)


def _moe_add_kernel(h_ref, gate_ref, y0_ref, y1_ref, o_ref):
    gate = gate_ref[...]
    o_ref[...] = h_ref[...] + gate[:, 0:1] * _load_row_tiles(y0_ref) + gate[:, 1:2] * _load_row_tiles(y1_ref)


def _moe_specs(n_tok):
    slots = n_tok // TOK_TILE
    y2_rows = (TOK_TILE * ROW_CHUNKS, LANES)
    return [pl.BlockSpec((TOK_TILE, D_MODEL), lambda i: (i, 0)), pl.BlockSpec((TOK_TILE, LANES), lambda i: (i, 0)),
            pl.BlockSpec(y2_rows, lambda i: (i, 0)), pl.BlockSpec(y2_rows, lambda i: (slots + i, 0))]


def _moe_add(h, gate, y2):
    n_tok = h.shape[0]
    return pl.pallas_call(
        _moe_add_kernel,
        grid=(n_tok // TOK_TILE,),
        in_specs=_moe_specs(n_tok),
        out_specs=pl.BlockSpec((TOK_TILE, D_MODEL), lambda i: (i, 0)),
        out_shape=jax.ShapeDtypeStruct((n_tok, D_MODEL), F32),
        compiler_params=_cparams(("parallel",)),
        name="moe_add",
    )(h, gate, y2, y2)


def _moe(hn, ids, counts, w_g, w_u, w_d, layer):
    n_tok = hn.shape[0] // ROW_CHUNKS
    n_assign = 2 * n_tok
    n_blocks = n_assign // MOE_BLK + N_EXPERTS
    n_rows = n_blocks * MOE_BLK
    cnt = counts[0, :N_EXPERTS].astype(jnp.int32)
    padded = (cnt + MOE_BLK - 1) // MOE_BLK * MOE_BLK
    pends = jnp.cumsum(padded).astype(jnp.int32)
    pstarts = pends - padded
    ids_t = ids[:, 0:4].T
    first_row = sum(jnp.where(ids_t[0:2] == e, pstarts[e], 0) for e in range(N_EXPERTS))
    dest = (first_row + ids_t[2:4]).reshape(n_assign)
    blk_start = jnp.arange(n_blocks + 1, dtype=jnp.int32) * MOE_BLK
    blk_e = jnp.minimum(jnp.sum(pends[None, :] <= blk_start[:, None], axis=1), N_EXPERTS - 1).astype(jnp.int32)
    n_used = (pends[-1:] // MOE_BLK).astype(jnp.int32)
    src = _row_map(dest, cnt, pends, n_rows)
    return _expert_ffn(hn, src, blk_e, n_used, w_g, w_u, w_d, layer).reshape(-1, LANES)


def kernel(x, norm_mix, norm_ffn, w_in_even, ssm_a_re, ssm_a_im, ssm_b_re, ssm_b_im, ssm_c_re, ssm_c_im, ssm_d,
           ssm_log_step, w_glu, b_glu, q_norm, k_norm, w_out_even, w_in_conv, conv_w, w_out_conv, w_router_group,
           b_router_group, w_router_expert, b_router_expert, w_expert_gate, w_expert_up, w_expert_down):
    bsz, s_len, d = x.shape
    x2 = x.reshape(bsz * s_len, d)
    route = lambda layer: _router_operands(norm_ffn[layer], w_router_group[layer], b_router_group[layer],
                                           w_router_expert[layer], b_router_expert[layer])
    experts = lambda layer: (w_expert_gate, w_expert_up, w_expert_down, layer)

    u, q, k, v = _inproj_even(x2, norm_mix[0], w_in_even[0], q_norm[0], k_norm[0])
    tables = _s5_tables(ssm_a_re[0], ssm_a_im[0], ssm_b_re[0], ssm_b_im[0], ssm_c_re[0], ssm_c_im[0], ssm_d[0],
                        ssm_log_step[0])
    y_pre = _s5_core(u, tables, bsz, s_len)
    attn = _dilated_attention(q, k, v, bsz, s_len)
    h, hn, ids, gate, counts = _outproj_even(x2, y_pre, attn, w_glu[0], b_glu[0], w_out_even[0], route(0))
    y2 = _moe(hn, ids, counts, *experts(0))

    h, hn, ids, gate, counts = _conv_layer(h, gate, y2, norm_mix[1], w_in_conv[0], conv_w[0], w_out_conv[0],
                                           route(1), s_len)
    y2 = _moe(hn, ids, counts, *experts(1))
    return _moe_add(h, gate, y2).reshape(bsz, s_len, d)
```

```python
import functools
import math

import jax
import jax.numpy as jnp
from jax import lax
from jax.experimental import pallas as pl
from jax.experimental.pallas import tpu as pltpu

F32 = jnp.float32
BF16 = jnp.bfloat16

D_MODEL = 1024
SSM_GROUP = 16
SSM_GROUPS = 40
SSM_WIDTH = SSM_GROUP * SSM_GROUPS
SSM_STATE = 64
ATT_HEADS = 6
ATT_HEAD_DIM = 64
ATT_WIDTH = ATT_HEADS * ATT_HEAD_DIM
DILATIONS = (1, 4, 16)
ATT_BLK = 128
CONV_TAPS = 3
N_GROUPS = 4
EXPERTS_PER_GROUP = 8
N_EXPERTS = N_GROUPS * EXPERTS_PER_GROUP
D_EXPERT = 512
MOE_BLK = 256
RMS_EPS = 1e-6
NEG_INF = -1e30

LANES = 128
SUBLANES = 8
VMEM_LIMIT = 56 * 1024 * 1024

TOK_TILE = 512
SSM_CHUNK = 8
SSM_SLAB_GROUPS = LANES // SSM_GROUP
SSM_SLABS = SSM_WIDTH // LANES
S5_ROW_TILE = 256
ATT_SB = 2048
ROW_CHUNKS = D_MODEL // LANES


def _cparams(sem):
    return pltpu.CompilerParams(dimension_semantics=sem, vmem_limit_bytes=VMEM_LIMIT)


def _rms(x, gain):
    return x * lax.rsqrt(jnp.mean(x * x, axis=-1, keepdims=True) + RMS_EPS) * gain


def _load_row_tiles(ref, lead=()):
    n = ref.shape[-2] // ROW_CHUNKS
    return jnp.concatenate([ref[lead + (pl.ds(c, n, stride=ROW_CHUNKS), slice(None))] for c in range(ROW_CHUNKS)], axis=1)


def _store_row_tiles(ref, value, lead=()):
    n = value.shape[0]
    for c in range(ROW_CHUNKS):
        ref[lead + (pl.ds(c, n, stride=ROW_CHUNKS), slice(None))] = value[:, c * LANES:(c + 1) * LANES]


def _head_norm(t, gain, bd):
    tt = t * t
    hi = tt.astype(BF16)
    lo = (tt - hi.astype(F32)).astype(BF16)
    ss = jnp.dot(hi, bd, preferred_element_type=F32) + jnp.dot(lo, bd, preferred_element_type=F32)
    return t * lax.rsqrt(ss * (1.0 / ATT_HEAD_DIM) + RMS_EPS) * gain


def _inproj_even_kernel(x_ref, g_ref, w_ref, bd_ref, qn_ref, kn_ref, u_ref, q_ref, k_ref, v_ref):
    hn = _rms(x_ref[...], g_ref[...]).astype(BF16)
    proj = jnp.dot(hn, w_ref[...], preferred_element_type=F32)
    for j in range(SSM_SLABS):
        u_ref[j] = proj[:, j * LANES:(j + 1) * LANES]
    bd = bd_ref[...]
    o = SSM_WIDTH
    q = _head_norm(proj[:, o:o + ATT_WIDTH], qn_ref[...], bd) * (ATT_HEAD_DIM ** -0.5)
    k = _head_norm(proj[:, o + ATT_WIDTH:o + 2 * ATT_WIDTH], kn_ref[...], bd)
    v = proj[:, o + 2 * ATT_WIDTH:o + 3 * ATT_WIDTH]
    for j in range(ATT_WIDTH // LANES):
        q_ref[j] = q[:, j * LANES:(j + 1) * LANES]
        k_ref[j] = k[:, j * LANES:(j + 1) * LANES]
        v_ref[j] = v[:, j * LANES:(j + 1) * LANES]


def _inproj_even(x2, gain, w_in, q_norm, k_norm):
    n_tok = x2.shape[0]
    n_slab = ATT_WIDTH // LANES
    head_of = jnp.arange(ATT_WIDTH) // ATT_HEAD_DIM
    bd = (head_of[:, None] == head_of[None, :]).astype(BF16)
    qn = jnp.tile(q_norm.astype(F32), ATT_HEADS)[None]
    kn = jnp.tile(k_norm.astype(F32), ATT_HEADS)[None]
    full = lambda shape: pl.BlockSpec(shape, lambda i: (0,) * len(shape))
    slab = pl.BlockSpec((n_slab, TOK_TILE, LANES), lambda i: (0, i, 0))
    slab_shape = jax.ShapeDtypeStruct((n_slab, n_tok, LANES), F32)
    return pl.pallas_call(
        _inproj_even_kernel,
        grid=(n_tok // TOK_TILE,),
        in_specs=[pl.BlockSpec((TOK_TILE, D_MODEL), lambda i: (i, 0)), full((1, D_MODEL)),
                  full(w_in.shape), full(bd.shape), full(qn.shape), full(kn.shape)],
        out_specs=[pl.BlockSpec((SSM_SLABS, TOK_TILE, LANES), lambda i: (0, i, 0)), slab, slab, slab],
        out_shape=[jax.ShapeDtypeStruct((SSM_SLABS, n_tok, LANES), F32), slab_shape, slab_shape, slab_shape],
        compiler_params=_cparams(("parallel",)),
        name="inproj_even",
    )(x2, gain[None].astype(F32), w_in.astype(BF16), bd, qn, kn)


def _s5_tables(a_re, a_im, b_re, b_im, c_re, c_im, d_skip, log_step):
    f = lambda t: t.astype(F32)
    a_re, a_im, b_re, b_im, c_re, c_im = map(f, (a_re, a_im, b_re, b_im, c_re, c_im))
    L = SSM_CHUNK
    step = jnp.exp(f(log_step))[:, None]
    ks = jnp.arange(L + 1, dtype=F32)[:, None, None]
    mag = jnp.exp(ks * (a_re * step)[None])
    ang = ks * (a_im * step)[None]
    pw_re, pw_im = mag * jnp.cos(ang), mag * jnp.sin(ang)
    nr, ni = pw_re[1] - 1.0, pw_im[1]
    den = a_re * a_re + a_im * a_im
    z_re, z_im = (nr * a_re + ni * a_im) / den, (ni * a_re - nr * a_im) / den
    bb_re = z_re[..., None] * b_re - z_im[..., None] * b_im
    bb_im = z_re[..., None] * b_im + z_im[..., None] * b_re
    lb_re = pw_re[..., None] * bb_re[None] - pw_im[..., None] * bb_im[None]
    lb_im = pw_re[..., None] * bb_im[None] + pw_im[..., None] * bb_re[None]
    kk = jnp.einsum('gop,kgpi->gkio', c_re, lb_re[:L]) - jnp.einsum('gop,kgpi->gkio', c_im, lb_im[:L])
    ti = jnp.arange(L)
    lag = ti[None, :] - ti[:, None]
    m = jnp.where((lag >= 0)[None, :, :, None, None], kk[:, jnp.maximum(lag, 0)], 0.0)
    ns, gs = SSM_SLABS, SSM_SLAB_GROUPS
    lw = L * LANES
    pm = m.reshape(ns, gs, L, L, SSM_GROUP, SSM_GROUP).transpose(0, 2, 1, 4, 3, 5).reshape(ns, lw, L * SSM_GROUP)
    fold_e = lambda t: t[:L][::-1].reshape(L, ns, gs, SSM_STATE, SSM_GROUP).transpose(1, 0, 2, 4, 3).reshape(ns, lw, SSM_STATE)
    pe = jnp.concatenate([fold_e(lb_re), fold_e(lb_im)], axis=-1)
    pw1_re, pw1_im = (t[1:].transpose(1, 0, 2)[:, :, None, :] for t in (pw_re, pw_im))
    cl_re = c_re[:, None] * pw1_re - c_im[:, None] * pw1_im
    cl_im = c_re[:, None] * pw1_im + c_im[:, None] * pw1_re
    fold_f = lambda t: t.reshape(ns, gs, L, SSM_GROUP, SSM_STATE).transpose(0, 4, 2, 1, 3).reshape(ns, SSM_STATE, lw)
    pf = jnp.concatenate([fold_f(cl_re), fold_f(-cl_im)], axis=1)
    per_chain = lambda t: jnp.tile(t.reshape(ns, gs * SSM_STATE // LANES, LANES), (1, 2, 1))
    d_vec = jnp.tile(f(d_skip).reshape(ns, 1, LANES), (1, 1, L))
    return pm.astype(BF16), pe.astype(BF16), pf.astype(BF16), per_chain(pw_re[L]), per_chain(pw_im[L]), d_vec


def _iota2(shape):
    return lax.broadcasted_iota(jnp.int32, shape, 0), lax.broadcasted_iota(jnp.int32, shape, 1)


def _widen(compact, group_major_cols, shape, r_shift, c_shift, sel_rows):
    gmask = SSM_SLAB_GROUPS - 1
    if sel_rows:
        r, c = _iota2((shape[0], compact.shape[0]))
        sel = ((r >> 9) == (c >> 6)) & ((r & (SSM_STATE - 1)) == (c & (SSM_STATE - 1)))
        wide = jnp.dot(sel.astype(BF16), compact, preferred_element_type=F32)
    else:
        r, c = _iota2((compact.shape[1], shape[1]))
        if group_major_cols:
            sel = ((c >> 9) == (r >> 6)) & ((c & (SSM_STATE - 1)) == (r & (SSM_STATE - 1)))
        else:
            sel = ((c >> 7) == (r >> 4)) & ((c & (SSM_GROUP - 1)) == (r & (SSM_GROUP - 1)))
        wide = jnp.dot(compact, sel.astype(BF16), preferred_element_type=F32)
    r, c = _iota2(shape)
    keep = ((r >> r_shift) & gmask) == ((c >> c_shift) & gmask)
    return jnp.where(keep, wide, 0.0).astype(BF16)


def _s5_kernel(u_ref, pm_ref, pe_ref, pf_ref, ar_ref, ai_ref, d_ref, y_ref, m_ref, e_ref, f_ref, x_ref, sr_ref, si_ref,
               *, n_chunk, pitch):
    L = SSM_CHUNK
    n_blk = SSM_SLAB_GROUPS * SSM_STATE // LANES
    n_re = n_blk * LANES
    lw = L * LANES
    m_ref[...] = _widen(pm_ref[...], False, (lw, lw), 4, 4, False)
    e_ref[...] = _widen(pe_ref[...], True, (lw, 2 * n_re), 4, 6, False)
    f_ref[...] = _widen(pf_ref[...], False, (2 * n_re, lw), 6, 4, True)
    tiles = [(b, c0) for b in range(2) for c0 in range(0, n_chunk, S5_ROW_TILE)]
    for b, c0 in tiles:
        r0 = b * n_chunk + c0
        for t in range(L):
            x_ref[r0:r0 + S5_ROW_TILE, t * LANES:(t + 1) * LANES] = (
                u_ref[pl.ds(r0 * L + t, S5_ROW_TILE, stride=L), :].astype(BF16))
        sl = jnp.dot(x_ref[r0:r0 + S5_ROW_TILE, :], e_ref[...], preferred_element_type=F32)
        for j in range(n_blk):
            base = (b * n_blk + j) * pitch + c0
            sr_ref[base:base + S5_ROW_TILE, :] = sl[:, j * LANES:(j + 1) * LANES]
            si_ref[base:base + S5_ROW_TILE, :] = sl[:, n_re + j * LANES:n_re + (j + 1) * LANES]
    ar, ai = ar_ref[...], ai_ref[...]
    half = LANES

    def scan_step(c, carry):
        s_re, s_im = carry
        rows = pl.ds(c, SUBLANES, stride=pitch)
        x_re, x_im = sr_ref[rows, :], si_ref[rows, :]
        sr_ref[rows, :] = s_re
        si_ref[rows, :] = s_im
        return ar * s_re - ai * s_im + x_re, ar * s_im + ai * s_re + x_im

    zero = jnp.zeros((SUBLANES, half), F32)
    lax.fori_loop(0, n_chunk, scan_step, (zero, zero), unroll=8)

    for b, c0 in tiles:
        r0 = b * n_chunk + c0
        chain = lambda ref, j: ref[(b * n_blk + j) * pitch + c0:(b * n_blk + j) * pitch + c0 + S5_ROW_TILE, :]
        sp = jnp.concatenate([chain(sr_ref, j) for j in range(n_blk)] + [chain(si_ref, j) for j in range(n_blk)],
                             axis=1).astype(BF16)
        xt = x_ref[r0:r0 + S5_ROW_TILE, :]
        y = (jnp.dot(xt, m_ref[...], preferred_element_type=F32)
             + jnp.dot(sp, f_ref[...], preferred_element_type=F32)
             + d_ref[...] * xt.astype(F32))
        for t in range(L):
            y_ref[pl.ds(r0 * L + t, S5_ROW_TILE, stride=L), :] = y[:, t * LANES:(t + 1) * LANES]


def _s5_core(u, tables, bsz, s_len):
    assert bsz == 2, "the scan packs (batch, lane block) into the 8 sublanes of one vreg"
    pm, pe, pf, a_r, a_i, d_vec = tables
    n_tok = bsz * s_len
    n_chunk = s_len // SSM_CHUNK
    pitch = n_chunk + SUBLANES
    lw = SSM_CHUNK * LANES
    n_state = 2 * SSM_SLAB_GROUPS * SSM_STATE
    slab = lambda shape, **kw: pl.BlockSpec((None,) + shape, lambda i: (i,) + (0,) * len(shape), **kw)
    once = dict(pipeline_mode=pl.Buffered(1))
    return pl.pallas_call(
        functools.partial(_s5_kernel, n_chunk=n_chunk, pitch=pitch),
        grid=(SSM_SLABS,),
        in_specs=[slab((n_tok, LANES), **once), slab(pm.shape[1:]), slab(pe.shape[1:]), slab(pf.shape[1:]),
                  slab((SUBLANES, LANES)), slab((SUBLANES, LANES)), slab((1, lw))],
        out_specs=slab((n_tok, LANES), **once),
        out_shape=jax.ShapeDtypeStruct((SSM_SLABS, n_tok, LANES), F32),
        scratch_shapes=[pltpu.VMEM((lw, lw), BF16), pltpu.VMEM((lw, n_state), BF16), pltpu.VMEM((n_state, lw), BF16),
                        pltpu.VMEM((bsz * n_chunk, lw), BF16),
                        pltpu.VMEM((SUBLANES * pitch, LANES), F32),
                        pltpu.VMEM((SUBLANES * pitch, LANES), F32)],
        compiler_params=_cparams(("parallel",)),
        name="s5_core",
    )(u, pm, pe, pf, a_r, a_i, d_vec)


def _attn_kernel(slope_ref, q_ref, kp_ref, kc_ref, vp_ref, vc_ref, o_ref, kk_ref, vv_ref, m_ref, l_ref, acc_ref):
    slab = pl.program_id(1)
    sb = pl.program_id(2)
    kk_ref[0:ATT_SB, :] = kp_ref[...]
    kk_ref[ATT_SB:2 * ATT_SB, :] = kc_ref[...]
    vv_ref[0:ATT_SB, :] = vp_ref[...]
    vv_ref[ATT_SB:2 * ATT_SB, :] = vc_ref[...]

    lane = lax.broadcasted_iota(jnp.int32, (ATT_BLK, LANES), 1)
    head0 = lane < ATT_HEAD_DIM
    qi = lax.broadcasted_iota(jnp.int32, (ATT_BLK, 2 * ATT_BLK), 0)
    kj = lax.broadcasted_iota(jnp.int32, (ATT_BLK, 2 * ATT_BLK), 1)
    back = qi + ATT_BLK - kj
    band = (back >= 0) & (back <= ATT_BLK)
    neg_steps = -back.astype(F32)
    slopes = (slope_ref[2 * slab], slope_ref[2 * slab + 1])

    for pat, dil in enumerate(DILATIONS):
        span = ATT_BLK * dil

        def tile(idx, carry, dil=dil, span=span, pat=pat):
            start = (idx // dil) * span + idx % dil
            seq_ok = jnp.logical_or(sb > 0, idx >= dil)
            valid = band & ((kj >= ATT_BLK) | seq_ok)
            if dil == 1:
                start = pl.multiple_of(start, ATT_BLK)
                rows_of = lambda first, n: pl.ds(first, n)
            else:
                rows_of = lambda first, n: pl.ds(first, n, stride=dil)
            rows = rows_of(start, ATT_BLK)
            qt = q_ref[rows, :]
            kt = kk_ref[rows_of(ATT_SB + start - span, 2 * ATT_BLK), :].astype(BF16)
            vt = vv_ref[rows_of(ATT_SB + start - span, 2 * ATT_BLK), :].astype(BF16)
            q0 = jnp.where(head0, qt, 0.0)
            parts = []
            for hh, qh in enumerate((q0, qt - q0)):
                s = lax.dot_general(qh.astype(BF16), kt, (((1,), (1,)), ((), ())), preferred_element_type=F32)
                s = jnp.where(valid, s + (slopes[hh] * float(dil)) * neg_steps, NEG_INF)
                m = jnp.max(s, axis=-1, keepdims=True)
                p = jnp.exp(s - m)
                l = jnp.sum(p, axis=-1, keepdims=True)
                o = jnp.dot(p.astype(BF16), vt, preferred_element_type=F32)
                parts.append((m, l, o))
            (m0, l0, o0), (m1, l1, o1) = parts
            m_t = jnp.where(head0, m0, m1)
            l_t = jnp.where(head0, l0, l1)
            o_t = jnp.where(head0, o0, o1)
            if pat == 0:
                m_ref[rows, :] = m_t
                l_ref[rows, :] = l_t
                acc_ref[rows, :] = o_t
            else:
                m_old = m_ref[rows, :]
                m_new = jnp.maximum(m_old, m_t)
                a = jnp.exp(m_old - m_new)
                b = jnp.exp(m_t - m_new)
                m_ref[rows, :] = m_new
                l_ref[rows, :] = a * l_ref[rows, :] + b * l_t
                acc_ref[rows, :] = a * acc_ref[rows, :] + b * o_t
            return carry

        lax.fori_loop(0, ATT_SB // ATT_BLK, tile, 0, unroll=4)

    o_ref[...] = (acc_ref[...] / l_ref[...]).astype(o_ref.dtype)


def _dilated_attention(q, k, v, bsz, s_len):
    n_slab = q.shape[0]
    shape4 = (n_slab, bsz, s_len, LANES)
    q, k, v = (t.reshape(shape4) for t in (q, k, v))
    slopes = jnp.asarray([2.0 ** (-8.0 * (h + 1) / ATT_HEADS) for h in range(ATT_HEADS)], F32)
    blk = (None, None, ATT_SB, LANES)
    cur = pl.BlockSpec(blk, lambda b, j, i, s: (j, b, i, 0))
    prev = pl.BlockSpec(blk, lambda b, j, i, s: (j, b, jnp.maximum(i - 1, 0), 0))
    out = pl.pallas_call(
        _attn_kernel,
        grid_spec=pltpu.PrefetchScalarGridSpec(
            num_scalar_prefetch=1,
            grid=(bsz, n_slab, s_len // ATT_SB),
            in_specs=[cur, prev, cur, prev, cur],
            out_specs=cur,
            scratch_shapes=[pltpu.VMEM((2 * ATT_SB, LANES), F32), pltpu.VMEM((2 * ATT_SB, LANES), F32),
                            pltpu.VMEM((ATT_SB, LANES), F32), pltpu.VMEM((ATT_SB, LANES), F32),
                            pltpu.VMEM((ATT_SB, LANES), F32)]),
        out_shape=jax.ShapeDtypeStruct(shape4, BF16),
        compiler_params=_cparams(("parallel", "parallel", "parallel")),
        name="dilated_attn",
    )(slopes, q, k, k, v, v)
    return out.reshape(n_slab, bsz * s_len, LANES)


def _route_epilogue(h, gain_ref, wr_ref, br_ref, tri_ref, cnt_ref, h_ref, hn_ref, ids_ref, gate_ref, cnt_out_ref):
    tm = h.shape[0]
    h_ref[...] = h
    hn = _rms(h, gain_ref[...])
    _store_row_tiles(hn_ref, hn)
    hn_hi = hn.astype(BF16)
    hn_lo = (hn - hn_hi.astype(F32)).astype(BF16)
    logits = (jnp.dot(hn_hi, wr_ref[0], preferred_element_type=F32)
              + (jnp.dot(hn_hi, wr_ref[1], preferred_element_type=F32)
                 + jnp.dot(hn_lo, wr_ref[0], preferred_element_type=F32))) + br_ref[...]
    lane = lax.broadcasted_iota(jnp.int32, (tm, LANES), 1)
    big = jnp.int32(LANES)
    rmax = lambda t: jnp.max(t, axis=-1, keepdims=True)
    rmin = lambda t: jnp.min(t, axis=-1, keepdims=True)
    rsum = lambda t: jnp.sum(t, axis=-1, keepdims=True)
    gmask = lane < N_GROUPS
    gl = jnp.where(gmask, logits, -jnp.inf)
    gmax = rmax(gl)
    ge = jnp.where(gmask, jnp.exp(gl - gmax), 0.0)
    gprob = ge / rsum(ge)
    g_w = rmax(gprob)
    grp = rmin(jnp.where(gmask & (gprob == g_w), lane, big))
    group_of_lane = (lane - N_GROUPS) >> int(math.log2(EXPERTS_PER_GROUP))
    emask = (lane >= N_GROUPS) & (lane < N_GROUPS + N_EXPERTS) & (group_of_lane == grp)
    el = jnp.where(emask, logits, -jnp.inf)
    ee = jnp.where(emask, jnp.exp(el - rmax(el)), 0.0)
    ep = jnp.where(emask, ee / rsum(ee), -1.0)
    p1 = rmax(ep)
    i1 = rmin(jnp.where(ep == p1, lane, big))
    ep2 = jnp.where(lane == i1, -1.0, ep)
    p2 = rmax(ep2)
    i2 = rmin(jnp.where(ep2 == p2, lane, big))
    e1, e2 = i1 - N_GROUPS, i2 - N_GROUPS
    psum = p1 + p2
    gate1, gate2 = g_w * p1 / psum, g_w * p2 / psum
    oh1, oh2 = lane == e1, lane == e2
    member = (oh1 | oh2).astype(BF16)
    before = jnp.dot(tri_ref[...], member, preferred_element_type=F32) + cnt_ref[...]
    r1 = rsum(jnp.where(oh1, before, 0.0)).astype(jnp.int32)
    r2 = rsum(jnp.where(oh2, before, 0.0)).astype(jnp.int32)
    cnt_ref[...] = cnt_ref[...] + jnp.sum(member.astype(F32), axis=0, keepdims=True)
    ids = jnp.where(lane == 0, e1, jnp.where(lane == 1, e2, jnp.where(lane == 2, r1, jnp.where(lane == 3, r2, 0))))
    ids_ref[...] = jnp.transpose(ids)[0:SUBLANES, :]
    gate_ref[...] = jnp.where(lane == 0, gate1, jnp.where(lane == 1, gate2, 0.0))
    cnt_out_ref[...] = jnp.broadcast_to(cnt_ref[...], cnt_out_ref.shape)


def _router_operands(norm_gain, w_rg, b_rg, w_re, b_re):
    pad = LANES - N_GROUPS - N_EXPERTS
    wr = jnp.pad(jnp.concatenate([w_rg, w_re], axis=1).astype(F32), ((0, 0), (0, pad)))
    br = jnp.pad(jnp.concatenate([b_rg, b_re]).astype(F32), (0, pad))[None]
    r = jnp.arange(TOK_TILE)
    tri = (r[None, :] < r[:, None]).astype(BF16)
    wr_hi = wr.astype(BF16)
    wr_lo = (wr - wr_hi.astype(F32)).astype(BF16)
    return norm_gain[None].astype(F32), jnp.stack([wr_hi, wr_lo]), br, tri


def _route_specs(n_tok):
    full = lambda shape: pl.BlockSpec(shape, lambda i: (0,) * len(shape))
    in_specs = [full((1, D_MODEL)), full((2, D_MODEL, LANES)), full((1, LANES)), full((TOK_TILE, TOK_TILE))]
    tok = lambda w: pl.BlockSpec((TOK_TILE, w), lambda i: (i, 0))
    out_specs = [tok(D_MODEL), pl.BlockSpec((TOK_TILE * ROW_CHUNKS, LANES), lambda i: (i, 0)),
                 pl.BlockSpec((SUBLANES, TOK_TILE), lambda i: (0, i)), tok(LANES), full((SUBLANES, LANES))]
    out_shape = [jax.ShapeDtypeStruct((n_tok, D_MODEL), F32), jax.ShapeDtypeStruct((n_tok * ROW_CHUNKS, LANES), F32),
                 jax.ShapeDtypeStruct((SUBLANES, n_tok), jnp.int32), jax.ShapeDtypeStruct((n_tok, LANES), F32),
                 jax.ShapeDtypeStruct((SUBLANES, LANES), F32)]
    return in_specs, out_specs, out_shape


def _gelu_tanh(x):
    return 0.5 * x * (1.0 + jnp.tanh(math.sqrt(2.0 / math.pi) * (x + 0.044715 * (x * x * x))))


def _outproj_even_kernel(x_ref, y_ref, a_ref, wglu_ref, bglu_ref, wout_ref, gain_ref, wr_ref, br_ref, tri_ref,
                         h_ref, hn_ref, ids_ref, gate_ref, cnt_out_ref, cnt_ref):
    @pl.when(pl.program_id(0) == 0)
    def _():
        cnt_ref[...] = jnp.zeros_like(cnt_ref)

    y = _gelu_tanh(jnp.concatenate([y_ref[j] for j in range(SSM_SLABS)], axis=1))
    y = y * jax.nn.sigmoid(jnp.dot(y.astype(BF16), wglu_ref[...], preferred_element_type=F32) + bglu_ref[...])
    mix = jnp.dot(y.astype(BF16), wout_ref[0:SSM_WIDTH, :], preferred_element_type=F32)
    for j in range(ATT_WIDTH // LANES):
        rows = slice(SSM_WIDTH + j * LANES, SSM_WIDTH + (j + 1) * LANES)
        mix = mix + jnp.dot(a_ref[j], wout_ref[rows, :], preferred_element_type=F32)
    _route_epilogue(x_ref[...] + mix, gain_ref, wr_ref, br_ref, tri_ref, cnt_ref,
                    h_ref, hn_ref, ids_ref, gate_ref, cnt_out_ref)


def _outproj_even(x2, y_pre, attn, w_glu, b_glu, w_out, route_ops):
    n_tok = x2.shape[0]
    n_slab = attn.shape[0]
    r_in, r_out, r_shape = _route_specs(n_tok)
    full = lambda shape: pl.BlockSpec(shape, lambda i: (0,) * len(shape))
    return pl.pallas_call(
        _outproj_even_kernel,
        grid=(n_tok // TOK_TILE,),
        in_specs=[pl.BlockSpec((TOK_TILE, D_MODEL), lambda i: (i, 0)),
                  pl.BlockSpec((SSM_SLABS, TOK_TILE, LANES), lambda i: (0, i, 0)),
                  pl.BlockSpec((n_slab, TOK_TILE, LANES), lambda i: (0, i, 0)),
                  full(w_glu.shape), full((1, SSM_WIDTH)), full(w_out.shape)] + r_in,
        out_specs=r_out, out_shape=r_shape,
        scratch_shapes=[pltpu.VMEM((1, LANES), F32)],
        compiler_params=_cparams(("arbitrary",)),
        name="outproj_even",
    )(x2, y_pre, attn, w_glu.astype(BF16), b_glu[None].astype(F32), w_out.astype(BF16), *route_ops)


def _conv_layer_kernel(h_ref, pgate_ref, y0_ref, y1_ref, gmix_ref, win_ref, cw_ref, wout_ref, gain_ref, wr_ref, br_ref,
                       tri_ref, ho_ref, hn_ref, ids_ref, gate_ref, cnt_out_ref, cnt_ref, zc_ref, *, tiles_per_seq):
    i = pl.program_id(0)

    @pl.when(i == 0)
    def _():
        cnt_ref[...] = jnp.zeros_like(cnt_ref)

    @pl.when(i % tiles_per_seq == 0)
    def _():
        zc_ref[0:SUBLANES, :] = jnp.zeros((SUBLANES, D_MODEL), F32)

    pgate = pgate_ref[...]
    h = h_ref[...] + pgate[:, 0:1] * _load_row_tiles(y0_ref) + pgate[:, 1:2] * _load_row_tiles(y1_ref)
    tm = h.shape[0]
    hn = _rms(h, gmix_ref[...]).astype(BF16)
    c = D_MODEL
    b_gate = jnp.dot(hn, win_ref[:, 0:c], preferred_element_type=F32)
    zc = jnp.dot(hn, win_ref[:, c:2 * c], preferred_element_type=F32) * jnp.dot(hn, win_ref[:, 2 * c:3 * c], preferred_element_type=F32)
    zc_ref[SUBLANES:SUBLANES + tm, :] = zc
    conv = cw_ref[CONV_TAPS - 1:CONV_TAPS, :] * zc
    for back in range(1, CONV_TAPS):
        tap = CONV_TAPS - 1 - back
        conv = conv + cw_ref[tap:tap + 1, :] * zc_ref[SUBLANES - back:SUBLANES - back + tm, :]
    zc_ref[0:SUBLANES, :] = zc_ref[tm:tm + SUBLANES, :]
    mix = jnp.dot((b_gate * conv).astype(BF16), wout_ref[...], preferred_element_type=F32)
    _route_epilogue(h + mix, gain_ref, wr_ref, br_ref, tri_ref, cnt_ref,
                    ho_ref, hn_ref, ids_ref, gate_ref, cnt_out_ref)


def _conv_layer(h1, pgate, y2, gain_mix, w_in, conv_w, w_out, route_ops, s_len):
    n_tok = h1.shape[0]
    r_in, r_out, r_shape = _route_specs(n_tok)
    full = lambda shape: pl.BlockSpec(shape, lambda i: (0,) * len(shape))
    return pl.pallas_call(
        functools.partial(_conv_layer_kernel, tiles_per_seq=s_len // TOK_TILE),
        grid=(n_tok // TOK_TILE,),
        in_specs=_moe_specs(n_tok) + [full((1, D_MODEL)), full(w_in.shape), full(conv_w.shape), full(w_out.shape)] + r_in,
        out_specs=r_out, out_shape=r_shape,
        scratch_shapes=[pltpu.VMEM((1, LANES), F32), pltpu.VMEM((TOK_TILE + SUBLANES, D_MODEL), F32)],
        compiler_params=_cparams(("arbitrary",)),
        name="conv_layer",
    )(h1, pgate, y2, y2, gain_mix[None].astype(F32), w_in.astype(BF16), conv_w.astype(F32), w_out.astype(BF16),
      *route_ops)


def _row_map_kernel(dest_ref, cnt_ref, pend_ref, src_ref, *, n_assign, n_rows):
    def fill(i, carry):
        src_ref[i] = n_assign + (i & (MOE_BLK - 1))
        return carry

    lax.fori_loop(0, MOE_BLK, fill, 0, unroll=8)
    for e in range(N_EXPERTS):
        first = MOE_BLK + (pend_ref[e - 1] if e else 0)
        lax.fori_loop(first + cnt_ref[e], MOE_BLK + pend_ref[e], fill, 0)
    lax.fori_loop(MOE_BLK + pend_ref[N_EXPERTS - 1], n_rows + 2 * MOE_BLK, fill, 0)

    def put(n, carry):
        src_ref[dest_ref[n] + MOE_BLK] = n
        return carry

    lax.fori_loop(0, n_assign, put, 0, unroll=8)


def _row_map(dest_flat, cnt, pends, n_rows):
    n_assign = dest_flat.shape[0]
    return pl.pallas_call(
        functools.partial(_row_map_kernel, n_assign=n_assign, n_rows=n_rows),
        grid_spec=pltpu.PrefetchScalarGridSpec(
            num_scalar_prefetch=3, grid=(1,), in_specs=[],
            out_specs=pl.BlockSpec(memory_space=pltpu.SMEM)),
        out_shape=jax.ShapeDtypeStruct((n_rows + 2 * MOE_BLK,), jnp.int32),
        compiler_params=_cparams(("arbitrary",)),
        name="moe_row_map",
    )(dest_flat, cnt, pends)


def _expert_kernel(blk_e_ref, n_used_ref, src_ref, hn_ref, wg_ref, wu_ref, wd_ref, y2_ref,
                   xbuf, ybuf, wg_s, wu_s, wd_s, gsem, ssem, *, n_tok):
    b = pl.program_id(0)
    n_used = n_used_ref[0]
    cur = b % 2
    nxt = 1 - cur

    tile = lambda i: pl.ds(i * ROW_CHUNKS, ROW_CHUNKS)

    def gather(blk, slot, i):
        row = src_ref[(blk + 1) * MOE_BLK + i] & (n_tok - 1)
        return pltpu.make_async_copy(hn_ref.at[row], xbuf.at[slot, tile(i)], gsem.at[slot])

    def scatter(blk, slot, i):
        row = src_ref[(blk + 1) * MOE_BLK + i]
        return pltpu.make_async_copy(ybuf.at[slot, tile(i)], y2_ref.at[row], ssem.at[slot])

    wait_block = lambda buf, sem, slot: pltpu.make_async_copy(buf.at[slot], buf.at[slot], sem.at[slot]).wait()

    @pl.when(b == 0)
    def _():
        ybuf[1] = jnp.zeros(ybuf.shape[1:], ybuf.dtype)
        for i in range(MOE_BLK):
            gather(0, 0, i).start()

    @pl.when(b < n_used)
    def _():
        new_expert = jnp.logical_or(b == 0, blk_e_ref[b] != blk_e_ref[jnp.maximum(b - 1, 0)])

        @pl.when(new_expert)
        def _():
            wg_s[...] = wg_ref[...].astype(BF16)
            wu_s[...] = wu_ref[...].astype(BF16)
            wd_s[...] = wd_ref[...].astype(BF16)

        wait_block(xbuf, gsem, cur)

        @pl.when(b >= 1)
        def _():
            wait_block(ybuf, ssem, cur)

        for i in range(MOE_BLK):
            gather(b + 1, nxt, i).start(priority=i % 2)
            scatter(b - 1, nxt, i).start(priority=i % 2)
        x = _load_row_tiles(xbuf, (cur,)).astype(BF16)
        g = jnp.dot(x, wg_s[...], preferred_element_type=F32)
        u = jnp.dot(x, wu_s[...], preferred_element_type=F32)
        hb = (g * jax.nn.sigmoid(g) * u).astype(BF16)
        _store_row_tiles(ybuf, jnp.dot(hb, wd_s[...], preferred_element_type=F32), (cur,))

    @pl.when(b == n_used)
    def _():
        wait_block(xbuf, gsem, cur)
        wait_block(ybuf, ssem, cur)
        for i in range(MOE_BLK):
            scatter(b - 1, nxt, i).start()
        wait_block(ybuf, ssem, nxt)


def _expert_ffn(hn, src, blk_e, n_used, w_g, w_u, w_d, layer):
    n_tok = hn.shape[0] // ROW_CHUNKS
    assert n_tok & (n_tok - 1) == 0, "dump-row aliasing masks the token index with T - 1"
    n_blocks = (src.shape[0] - 2 * MOE_BLK) // MOE_BLK
    last = lambda b, n: jnp.maximum(jnp.minimum(b, n[0] - 1), 0)
    w_spec = lambda rows, cols: pl.BlockSpec((None, None, rows, cols), lambda b, e, n, s: (layer, e[last(b, n)], 0, 0))
    return pl.pallas_call(
        functools.partial(_expert_kernel, n_tok=n_tok),
        grid_spec=pltpu.PrefetchScalarGridSpec(
            num_scalar_prefetch=3,
            grid=(n_blocks + 1,),
            in_specs=[pl.BlockSpec(memory_space=pltpu.HBM),
                      w_spec(D_MODEL, D_EXPERT), w_spec(D_MODEL, D_EXPERT), w_spec(D_EXPERT, D_MODEL)],
            out_specs=pl.BlockSpec(memory_space=pltpu.HBM),
            scratch_shapes=[pltpu.VMEM((2, MOE_BLK * ROW_CHUNKS, LANES), F32),
                            pltpu.VMEM((2, MOE_BLK * ROW_CHUNKS, LANES), F32),
                            pltpu.VMEM((D_MODEL, D_EXPERT), BF16), pltpu.VMEM((D_MODEL, D_EXPERT), BF16),
                            pltpu.VMEM((D_EXPERT, D_MODEL), BF16),
                            pltpu.SemaphoreType.DMA((2,)), pltpu.SemaphoreType.DMA((2,))]),
        out_shape=jax.ShapeDtypeStruct((2 * n_tok + MOE_BLK, ROW_CHUNKS, LANES), F32),
        compiler_params=_cparams(("arbitrary",)),
        name="moe_experts",
    )(blk_e, n_used, src, hn.reshape(n_tok, ROW_CHUNKS, LANES), w_g, w_u, w_d)


def _moe_add_kernel(h_ref, gate_ref, y0_ref, y1_ref, o_ref):
    gate = gate_ref[...]
    o_ref[...] = h_ref[...] + gate[:, 0:1] * _load_row_tiles(y0_ref) + gate[:, 1:2] * _load_row_tiles(y1_ref)


def _moe_specs(n_tok):
    slots = n_tok // TOK_TILE
    y2_rows = (TOK_TILE * ROW_CHUNKS, LANES)
    return [pl.BlockSpec((TOK_TILE, D_MODEL), lambda i: (i, 0)), pl.BlockSpec((TOK_TILE, LANES), lambda i: (i, 0)),
            pl.BlockSpec(y2_rows, lambda i: (i, 0)), pl.BlockSpec(y2_rows, lambda i: (slots + i, 0))]


def _moe_add(h, gate, y2):
    n_tok = h.shape[0]
    return pl.pallas_call(
        _moe_add_kernel,
        grid=(n_tok // TOK_TILE,),
        in_specs=_moe_specs(n_tok),
        out_specs=pl.BlockSpec((TOK_TILE, D_MODEL), lambda i: (i, 0)),
        out_shape=jax.ShapeDtypeStruct((n_tok, D_MODEL), F32),
        compiler_params=_cparams(("parallel",)),
        name="moe_add",
    )(h, gate, y2, y2)


def _moe(hn, ids, counts, w_g, w_u, w_d, layer):
    n_tok = hn.shape[0] // ROW_CHUNKS
    n_assign = 2 * n_tok
    n_blocks = n_assign // MOE_BLK + N_EXPERTS
    n_rows = n_blocks * MOE_BLK
    cnt = counts[0, :N_EXPERTS].astype(jnp.int32)
    padded = (cnt + MOE_BLK - 1) // MOE_BLK * MOE_BLK
    pends = jnp.cumsum(padded).astype(jnp.int32)
    pstarts = pends - padded
    first_row = sum(jnp.where(ids[0:2] == e, pstarts[e], 0) for e in range(N_EXPERTS))
    dest = (first_row + ids[2:4]).reshape(n_assign)
    blk_start = jnp.arange(n_blocks + 1, dtype=jnp.int32) * MOE_BLK
    blk_e = jnp.minimum(jnp.sum(pends[None, :] <= blk_start[:, None], axis=1), N_EXPERTS - 1).astype(jnp.int32)
    n_used = (pends[-1:] // MOE_BLK).astype(jnp.int32)
    src = _row_map(dest, cnt, pends, n_rows)
    return _expert_ffn(hn, src, blk_e, n_used, w_g, w_u, w_d, layer).reshape(-1, LANES)


def kernel(x, norm_mix, norm_ffn, w_in_even, ssm_a_re, ssm_a_im, ssm_b_re, ssm_b_im, ssm_c_re, ssm_c_im, ssm_d,
           ssm_log_step, w_glu, b_glu, q_norm, k_norm, w_out_even, w_in_conv, conv_w, w_out_conv, w_router_group,
           b_router_group, w_router_expert, b_router_expert, w_expert_gate, w_expert_up, w_expert_down):
    bsz, s_len, d = x.shape
    x2 = x.reshape(bsz * s_len, d)
    route = lambda layer: _router_operands(norm_ffn[layer], w_router_group[layer], b_router_group[layer],
                                           w_router_expert[layer], b_router_expert[layer])
    experts = lambda layer: (w_expert_gate, w_expert_up, w_expert_down, layer)

    u, q, k, v = _inproj_even(x2, norm_mix[0], w_in_even[0], q_norm[0], k_norm[0])
    tables = _s5_tables(ssm_a_re[0], ssm_a_im[0], ssm_b_re[0], ssm_b_im[0], ssm_c_re[0], ssm_c_im[0], ssm_d[0],
                        ssm_log_step[0])
    y_pre = _s5_core(u, tables, bsz, s_len)
    attn = _dilated_attention(q, k, v, bsz, s_len)
    h, hn, ids, gate, counts = _outproj_even(x2, y_pre, attn, w_glu[0], b_glu[0], w_out_even[0], route(0))
    y2 = _moe(hn, ids, counts, *experts(0))

    h, hn, ids, gate, counts = _conv_layer(h, gate, y2, norm_mix[1], w_in_conv[0], conv_w[0], w_out_conv[0],
                                           route(1), s_len)
    y2 = _moe(hn, ids, counts, *experts(1))
    return _moe_add(h, gate, y2).reshape(bsz, s_len, d)
```

```python
import functools
import math

import jax
import jax.numpy as jnp
from jax import lax
from jax.experimental import pallas as pl
from jax.experimental.pallas import tpu as pltpu

F32 = jnp.float32
BF16 = jnp.bfloat16

D_MODEL = 1024
SSM_GROUP = 16
SSM_GROUPS = 40
SSM_WIDTH = SSM_GROUP * SSM_GROUPS
SSM_STATE = 64
ATT_HEADS = 6
ATT_HEAD_DIM = 64
ATT_WIDTH = ATT_HEADS * ATT_HEAD_DIM
DILATIONS = (1, 4, 16)
ATT_BLK = 128
CONV_TAPS = 3
N_GROUPS = 4
EXPERTS_PER_GROUP = 8
N_EXPERTS = N_GROUPS * EXPERTS_PER_GROUP
D_EXPERT = 512
MOE_BLK = 256
RMS_EPS = 1e-6
NEG_INF = -1e30

LANES = 128
SUBLANES = 8
VMEM_LIMIT = 56 * 1024 * 1024

TOK_TILE = 512
SSM_CHUNK = 8
SSM_SLAB_GROUPS = LANES // SSM_GROUP
SSM_SLABS = SSM_WIDTH // LANES
S5_ROW_TILE = 256
ATT_SB = 2048
ROW_CHUNKS = D_MODEL // (2 * LANES)


def _cparams(sem):
    return pltpu.CompilerParams(dimension_semantics=sem, vmem_limit_bytes=VMEM_LIMIT)


def _rms(x, gain):
    return x * lax.rsqrt(jnp.mean(x * x, axis=-1, keepdims=True) + RMS_EPS) * gain


HIGH_HALF = 0xFFFF0000


def _load_row_tiles(ref, lead=()):
    n = ref.shape[-2] // ROW_CHUNKS
    bits = jnp.concatenate([ref[lead + (pl.ds(c, n, stride=ROW_CHUNKS), slice(None))] for c in range(ROW_CHUNKS)], axis=1)
    low = lax.bitcast_convert_type(bits << 16, F32)
    high = lax.bitcast_convert_type(bits & jnp.uint32(HIGH_HALF), F32)
    return jnp.concatenate([low, high], axis=1)


def _store_row_tiles(ref, value, lead=()):
    n, half = value.shape[0], value.shape[1] // 2
    bf16_bits = lambda t: lax.bitcast_convert_type(t.astype(BF16).astype(F32), jnp.uint32)
    bits = (bf16_bits(value[:, :half]) >> 16) | (bf16_bits(value[:, half:]) & jnp.uint32(HIGH_HALF))
    for c in range(ROW_CHUNKS):
        ref[lead + (pl.ds(c, n, stride=ROW_CHUNKS), slice(None))] = bits[:, c * LANES:(c + 1) * LANES]


def _head_norm(t, gain, bd):
    tt = t * t
    hi = tt.astype(BF16)
    lo = (tt - hi.astype(F32)).astype(BF16)
    ss = jnp.dot(hi, bd, preferred_element_type=F32) + jnp.dot(lo, bd, preferred_element_type=F32)
    return t * lax.rsqrt(ss * (1.0 / ATT_HEAD_DIM) + RMS_EPS) * gain


def _inproj_even_kernel(x_ref, g_ref, w_ref, bd_ref, qn_ref, kn_ref, u_ref, q_ref, k_ref, v_ref):
    hn = _rms(x_ref[...], g_ref[...]).astype(BF16)
    proj = jnp.dot(hn, w_ref[...], preferred_element_type=F32)
    for j in range(SSM_SLABS):
        u_ref[j] = proj[:, j * LANES:(j + 1) * LANES]
    bd = bd_ref[...]
    o = SSM_WIDTH
    q = _head_norm(proj[:, o:o + ATT_WIDTH], qn_ref[...], bd) * (ATT_HEAD_DIM ** -0.5)
    k = _head_norm(proj[:, o + ATT_WIDTH:o + 2 * ATT_WIDTH], kn_ref[...], bd)
    v = proj[:, o + 2 * ATT_WIDTH:o + 3 * ATT_WIDTH]
    for j in range(ATT_WIDTH // LANES):
        q_ref[j] = q[:, j * LANES:(j + 1) * LANES]
        k_ref[j] = k[:, j * LANES:(j + 1) * LANES]
        v_ref[j] = v[:, j * LANES:(j + 1) * LANES]


def _inproj_even(x2, gain, w_in, q_norm, k_norm):
    n_tok = x2.shape[0]
    n_slab = ATT_WIDTH // LANES
    head_of = jnp.arange(ATT_WIDTH) // ATT_HEAD_DIM
    bd = (head_of[:, None] == head_of[None, :]).astype(BF16)
    qn = jnp.tile(q_norm.astype(F32), ATT_HEADS)[None]
    kn = jnp.tile(k_norm.astype(F32), ATT_HEADS)[None]
    full = lambda shape: pl.BlockSpec(shape, lambda i: (0,) * len(shape))
    slab = pl.BlockSpec((n_slab, TOK_TILE, LANES), lambda i: (0, i, 0))
    slab_shape = jax.ShapeDtypeStruct((n_slab, n_tok, LANES), F32)
    return pl.pallas_call(
        _inproj_even_kernel,
        grid=(n_tok // TOK_TILE,),
        in_specs=[pl.BlockSpec((TOK_TILE, D_MODEL), lambda i: (i, 0)), full((1, D_MODEL)),
                  full(w_in.shape), full(bd.shape), full(qn.shape), full(kn.shape)],
        out_specs=[pl.BlockSpec((SSM_SLABS, TOK_TILE, LANES), lambda i: (0, i, 0)), slab, slab, slab],
        out_shape=[jax.ShapeDtypeStruct((SSM_SLABS, n_tok, LANES), F32), slab_shape, slab_shape, slab_shape],
        compiler_params=_cparams(("parallel",)),
        name="inproj_even",
    )(x2, gain[None].astype(F32), w_in.astype(BF16), bd, qn, kn)


def _s5_tables(a_re, a_im, b_re, b_im, c_re, c_im, d_skip, log_step):
    f = lambda t: t.astype(F32)
    a_re, a_im, b_re, b_im, c_re, c_im = map(f, (a_re, a_im, b_re, b_im, c_re, c_im))
    L = SSM_CHUNK
    step = jnp.exp(f(log_step))[:, None]
    ks = jnp.arange(L + 1, dtype=F32)[:, None, None]
    mag = jnp.exp(ks * (a_re * step)[None])
    ang = ks * (a_im * step)[None]
    pw_re, pw_im = mag * jnp.cos(ang), mag * jnp.sin(ang)
    nr, ni = pw_re[1] - 1.0, pw_im[1]
    den = a_re * a_re + a_im * a_im
    z_re, z_im = (nr * a_re + ni * a_im) / den, (ni * a_re - nr * a_im) / den
    bb_re = z_re[..., None] * b_re - z_im[..., None] * b_im
    bb_im = z_re[..., None] * b_im + z_im[..., None] * b_re
    lb_re = pw_re[..., None] * bb_re[None] - pw_im[..., None] * bb_im[None]
    lb_im = pw_re[..., None] * bb_im[None] + pw_im[..., None] * bb_re[None]
    kk = jnp.einsum('gop,kgpi->gkio', c_re, lb_re[:L]) - jnp.einsum('gop,kgpi->gkio', c_im, lb_im[:L])
    ti = jnp.arange(L)
    lag = ti[None, :] - ti[:, None]
    m = jnp.where((lag >= 0)[None, :, :, None, None], kk[:, jnp.maximum(lag, 0)], 0.0)
    ns, gs = SSM_SLABS, SSM_SLAB_GROUPS
    lw = L * LANES
    pm = m.reshape(ns, gs, L, L, SSM_GROUP, SSM_GROUP).transpose(0, 2, 1, 4, 3, 5).reshape(ns, lw, L * SSM_GROUP)
    fold_e = lambda t: t[:L][::-1].reshape(L, ns, gs, SSM_STATE, SSM_GROUP).transpose(1, 0, 2, 4, 3).reshape(ns, lw, SSM_STATE)
    pe = jnp.concatenate([fold_e(lb_re), fold_e(lb_im)], axis=-1)
    pw1_re, pw1_im = (t[1:].transpose(1, 0, 2)[:, :, None, :] for t in (pw_re, pw_im))
    cl_re = c_re[:, None] * pw1_re - c_im[:, None] * pw1_im
    cl_im = c_re[:, None] * pw1_im + c_im[:, None] * pw1_re
    fold_f = lambda t: t.reshape(ns, gs, L, SSM_GROUP, SSM_STATE).transpose(0, 4, 2, 1, 3).reshape(ns, SSM_STATE, lw)
    pf = jnp.concatenate([fold_f(cl_re), fold_f(-cl_im)], axis=1)
    per_chain = lambda t: jnp.tile(t.reshape(ns, gs * SSM_STATE // LANES, LANES), (1, 2, 1))
    d_vec = jnp.tile(f(d_skip).reshape(ns, 1, LANES), (1, 1, L))
    return pm.astype(BF16), pe.astype(BF16), pf.astype(BF16), per_chain(pw_re[L]), per_chain(pw_im[L]), d_vec


def _iota2(shape):
    return lax.broadcasted_iota(jnp.int32, shape, 0), lax.broadcasted_iota(jnp.int32, shape, 1)


def _widen(compact, group_major_cols, shape, r_shift, c_shift, sel_rows):
    gmask = SSM_SLAB_GROUPS - 1
    if sel_rows:
        r, c = _iota2((shape[0], compact.shape[0]))
        sel = ((r >> 9) == (c >> 6)) & ((r & (SSM_STATE - 1)) == (c & (SSM_STATE - 1)))
        wide = jnp.dot(sel.astype(BF16), compact, preferred_element_type=F32)
    else:
        r, c = _iota2((compact.shape[1], shape[1]))
        if group_major_cols:
            sel = ((c >> 9) == (r >> 6)) & ((c & (SSM_STATE - 1)) == (r & (SSM_STATE - 1)))
        else:
            sel = ((c >> 7) == (r >> 4)) & ((c & (SSM_GROUP - 1)) == (r & (SSM_GROUP - 1)))
        wide = jnp.dot(compact, sel.astype(BF16), preferred_element_type=F32)
    r, c = _iota2(shape)
    keep = ((r >> r_shift) & gmask) == ((c >> c_shift) & gmask)
    return jnp.where(keep, wide, 0.0).astype(BF16)


def _s5_kernel(u_ref, pm_ref, pe_ref, pf_ref, ar_ref, ai_ref, d_ref, y_ref, m_ref, e_ref, f_ref, x_ref, sr_ref, si_ref,
               *, n_chunk, pitch):
    L = SSM_CHUNK
    n_blk = SSM_SLAB_GROUPS * SSM_STATE // LANES
    n_re = n_blk * LANES
    lw = L * LANES
    m_ref[...] = _widen(pm_ref[...], False, (lw, lw), 4, 4, False)
    e_ref[...] = _widen(pe_ref[...], True, (lw, 2 * n_re), 4, 6, False)
    f_ref[...] = _widen(pf_ref[...], False, (2 * n_re, lw), 6, 4, True)
    tiles = [(b, c0) for b in range(2) for c0 in range(0, n_chunk, S5_ROW_TILE)]
    for b, c0 in tiles:
        r0 = b * n_chunk + c0
        for t in range(L):
            x_ref[r0:r0 + S5_ROW_TILE, t * LANES:(t + 1) * LANES] = (
                u_ref[pl.ds(r0 * L + t, S5_ROW_TILE, stride=L), :].astype(BF16))
        sl = jnp.dot(x_ref[r0:r0 + S5_ROW_TILE, :], e_ref[...], preferred_element_type=F32)
        for j in range(n_blk):
            base = (b * n_blk + j) * pitch + c0
            sr_ref[base:base + S5_ROW_TILE, :] = sl[:, j * LANES:(j + 1) * LANES]
            si_ref[base:base + S5_ROW_TILE, :] = sl[:, n_re + j * LANES:n_re + (j + 1) * LANES]
    ar, ai = ar_ref[...], ai_ref[...]
    half = LANES

    def scan_step(c, carry):
        s_re, s_im = carry
        rows = pl.ds(c, SUBLANES, stride=pitch)
        x_re, x_im = sr_ref[rows, :], si_ref[rows, :]
        sr_ref[rows, :] = s_re
        si_ref[rows, :] = s_im
        return ar * s_re - ai * s_im + x_re, ar * s_im + ai * s_re + x_im

    zero = jnp.zeros((SUBLANES, half), F32)
    lax.fori_loop(0, n_chunk, scan_step, (zero, zero), unroll=8)

    for b, c0 in tiles:
        r0 = b * n_chunk + c0
        chain = lambda ref, j: ref[(b * n_blk + j) * pitch + c0:(b * n_blk + j) * pitch + c0 + S5_ROW_TILE, :]
        sp = jnp.concatenate([chain(sr_ref, j) for j in range(n_blk)] + [chain(si_ref, j) for j in range(n_blk)],
                             axis=1).astype(BF16)
        xt = x_ref[r0:r0 + S5_ROW_TILE, :]
        y = (jnp.dot(xt, m_ref[...], preferred_element_type=F32)
             + jnp.dot(sp, f_ref[...], preferred_element_type=F32)
             + d_ref[...] * xt.astype(F32))
        for t in range(L):
            y_ref[pl.ds(r0 * L + t, S5_ROW_TILE, stride=L), :] = y[:, t * LANES:(t + 1) * LANES]


def _s5_core(u, tables, bsz, s_len):
    assert bsz == 2, "the scan packs (batch, lane block) into the 8 sublanes of one vreg"
    pm, pe, pf, a_r, a_i, d_vec = tables
    n_tok = bsz * s_len
    n_chunk = s_len // SSM_CHUNK
    pitch = n_chunk + SUBLANES
    lw = SSM_CHUNK * LANES
    n_state = 2 * SSM_SLAB_GROUPS * SSM_STATE
    slab = lambda shape, **kw: pl.BlockSpec((None,) + shape, lambda i: (i,) + (0,) * len(shape), **kw)
    once = dict(pipeline_mode=pl.Buffered(1))
    return pl.pallas_call(
        functools.partial(_s5_kernel, n_chunk=n_chunk, pitch=pitch),
        grid=(SSM_SLABS,),
        in_specs=[slab((n_tok, LANES), **once), slab(pm.shape[1:]), slab(pe.shape[1:]), slab(pf.shape[1:]),
                  slab((SUBLANES, LANES)), slab((SUBLANES, LANES)), slab((1, lw))],
        out_specs=slab((n_tok, LANES), **once),
        out_shape=jax.ShapeDtypeStruct((SSM_SLABS, n_tok, LANES), F32),
        scratch_shapes=[pltpu.VMEM((lw, lw), BF16), pltpu.VMEM((lw, n_state), BF16), pltpu.VMEM((n_state, lw), BF16),
                        pltpu.VMEM((bsz * n_chunk, lw), BF16),
                        pltpu.VMEM((SUBLANES * pitch, LANES), F32),
                        pltpu.VMEM((SUBLANES * pitch, LANES), F32)],
        compiler_params=_cparams(("parallel",)),
        name="s5_core",
    )(u, pm, pe, pf, a_r, a_i, d_vec)


def _attn_kernel(slope_ref, q_ref, kp_ref, kc_ref, vp_ref, vc_ref, o_ref, kk_ref, vv_ref, m_ref, l_ref, acc_ref):
    slab = pl.program_id(1)
    sb = pl.program_id(2)
    kk_ref[0:ATT_SB, :] = kp_ref[...]
    kk_ref[ATT_SB:2 * ATT_SB, :] = kc_ref[...]
    vv_ref[0:ATT_SB, :] = vp_ref[...]
    vv_ref[ATT_SB:2 * ATT_SB, :] = vc_ref[...]

    lane = lax.broadcasted_iota(jnp.int32, (ATT_BLK, LANES), 1)
    head0 = lane < ATT_HEAD_DIM
    qi = lax.broadcasted_iota(jnp.int32, (ATT_BLK, 2 * ATT_BLK), 0)
    kj = lax.broadcasted_iota(jnp.int32, (ATT_BLK, 2 * ATT_BLK), 1)
    back = qi + ATT_BLK - kj
    band = (back >= 0) & (back <= ATT_BLK)
    neg_steps = -back.astype(F32)
    slopes = (slope_ref[2 * slab], slope_ref[2 * slab + 1])

    for pat, dil in enumerate(DILATIONS):
        span = ATT_BLK * dil

        def tile(idx, carry, dil=dil, span=span, pat=pat):
            start = (idx // dil) * span + idx % dil
            seq_ok = jnp.logical_or(sb > 0, idx >= dil)
            valid = band & ((kj >= ATT_BLK) | seq_ok)
            if dil == 1:
                start = pl.multiple_of(start, ATT_BLK)
                rows_of = lambda first, n: pl.ds(first, n)
            else:
                rows_of = lambda first, n: pl.ds(first, n, stride=dil)
            rows = rows_of(start, ATT_BLK)
            qt = q_ref[rows, :]
            kt = kk_ref[rows_of(ATT_SB + start - span, 2 * ATT_BLK), :].astype(BF16)
            vt = vv_ref[rows_of(ATT_SB + start - span, 2 * ATT_BLK), :].astype(BF16)
            q0 = jnp.where(head0, qt, 0.0)
            parts = []
            for hh, qh in enumerate((q0, qt - q0)):
                s = lax.dot_general(qh.astype(BF16), kt, (((1,), (1,)), ((), ())), preferred_element_type=F32)
                s = jnp.where(valid, s + (slopes[hh] * float(dil)) * neg_steps, NEG_INF)
                m = jnp.max(s, axis=-1, keepdims=True)
                p = jnp.exp(s - m)
                l = jnp.sum(p, axis=-1, keepdims=True)
                o = jnp.dot(p.astype(BF16), vt, preferred_element_type=F32)
                parts.append((m, l, o))
            (m0, l0, o0), (m1, l1, o1) = parts
            m_t = jnp.where(head0, m0, m1)
            l_t = jnp.where(head0, l0, l1)
            o_t = jnp.where(head0, o0, o1)
            if pat == 0:
                m_ref[rows, :] = m_t
                l_ref[rows, :] = l_t
                acc_ref[rows, :] = o_t
            else:
                m_old = m_ref[rows, :]
                m_new = jnp.maximum(m_old, m_t)
                a = jnp.exp(m_old - m_new)
                b = jnp.exp(m_t - m_new)
                m_ref[rows, :] = m_new
                l_ref[rows, :] = a * l_ref[rows, :] + b * l_t
                acc_ref[rows, :] = a * acc_ref[rows, :] + b * o_t
            return carry

        lax.fori_loop(0, ATT_SB // ATT_BLK, tile, 0, unroll=4)

    o_ref[...] = (acc_ref[...] / l_ref[...]).astype(o_ref.dtype)


def _dilated_attention(q, k, v, bsz, s_len):
    n_slab = q.shape[0]
    shape4 = (n_slab, bsz, s_len, LANES)
    q, k, v = (t.reshape(shape4) for t in (q, k, v))
    slopes = jnp.asarray([2.0 ** (-8.0 * (h + 1) / ATT_HEADS) for h in range(ATT_HEADS)], F32)
    blk = (None, None, ATT_SB, LANES)
    cur = pl.BlockSpec(blk, lambda b, j, i, s: (j, b, i, 0))
    prev = pl.BlockSpec(blk, lambda b, j, i, s: (j, b, jnp.maximum(i - 1, 0), 0))
    out = pl.pallas_call(
        _attn_kernel,
        grid_spec=pltpu.PrefetchScalarGridSpec(
            num_scalar_prefetch=1,
            grid=(bsz, n_slab, s_len // ATT_SB),
            in_specs=[cur, prev, cur, prev, cur],
            out_specs=cur,
            scratch_shapes=[pltpu.VMEM((2 * ATT_SB, LANES), F32), pltpu.VMEM((2 * ATT_SB, LANES), F32),
                            pltpu.VMEM((ATT_SB, LANES), F32), pltpu.VMEM((ATT_SB, LANES), F32),
                            pltpu.VMEM((ATT_SB, LANES), F32)]),
        out_shape=jax.ShapeDtypeStruct(shape4, BF16),
        compiler_params=_cparams(("parallel", "parallel", "parallel")),
        name="dilated_attn",
    )(slopes, q, k, k, v, v)
    return out.reshape(n_slab, bsz * s_len, LANES)


def _route_epilogue(h, gain_ref, wr_ref, br_ref, tri_ref, cnt_ref, h_ref, hn_ref, ids_ref, gate_ref, cnt_out_ref):
    tm = h.shape[0]
    h_ref[...] = h
    hn = _rms(h, gain_ref[...])
    _store_row_tiles(hn_ref, hn)
    hn_hi = hn.astype(BF16)
    hn_lo = (hn - hn_hi.astype(F32)).astype(BF16)
    logits = (jnp.dot(hn_hi, wr_ref[0], preferred_element_type=F32)
              + (jnp.dot(hn_hi, wr_ref[1], preferred_element_type=F32)
                 + jnp.dot(hn_lo, wr_ref[0], preferred_element_type=F32))) + br_ref[...]
    lane = lax.broadcasted_iota(jnp.int32, (tm, LANES), 1)
    big = jnp.int32(LANES)
    rmax = lambda t: jnp.max(t, axis=-1, keepdims=True)
    rmin = lambda t: jnp.min(t, axis=-1, keepdims=True)
    rsum = lambda t: jnp.sum(t, axis=-1, keepdims=True)
    gmask = lane < N_GROUPS
    gl = jnp.where(gmask, logits, -jnp.inf)
    gmax = rmax(gl)
    ge = jnp.where(gmask, jnp.exp(gl - gmax), 0.0)
    gprob = ge / rsum(ge)
    g_w = rmax(gprob)
    grp = rmin(jnp.where(gmask & (gprob == g_w), lane, big))
    group_of_lane = (lane - N_GROUPS) >> int(math.log2(EXPERTS_PER_GROUP))
    emask = (lane >= N_GROUPS) & (lane < N_GROUPS + N_EXPERTS) & (group_of_lane == grp)
    el = jnp.where(emask, logits, -jnp.inf)
    ee = jnp.where(emask, jnp.exp(el - rmax(el)), 0.0)
    ep = jnp.where(emask, ee / rsum(ee), -1.0)
    p1 = rmax(ep)
    i1 = rmin(jnp.where(ep == p1, lane, big))
    ep2 = jnp.where(lane == i1, -1.0, ep)
    p2 = rmax(ep2)
    i2 = rmin(jnp.where(ep2 == p2, lane, big))
    e1, e2 = i1 - N_GROUPS, i2 - N_GROUPS
    psum = p1 + p2
    gate1, gate2 = g_w * p1 / psum, g_w * p2 / psum
    oh1, oh2 = lane == e1, lane == e2
    member = (oh1 | oh2).astype(BF16)
    before = jnp.dot(tri_ref[...], member, preferred_element_type=F32) + cnt_ref[...]
    r1 = rsum(jnp.where(oh1, before, 0.0)).astype(jnp.int32)
    r2 = rsum(jnp.where(oh2, before, 0.0)).astype(jnp.int32)
    cnt_ref[...] = cnt_ref[...] + jnp.sum(member.astype(F32), axis=0, keepdims=True)
    ids = jnp.where(lane == 0, e1, jnp.where(lane == 1, e2, jnp.where(lane == 2, r1, jnp.where(lane == 3, r2, 0))))
    ids_ref[...] = jnp.transpose(ids)[0:SUBLANES, :]
    gate_ref[...] = jnp.where(lane == 0, gate1, jnp.where(lane == 1, gate2, 0.0))
    cnt_out_ref[...] = jnp.broadcast_to(cnt_ref[...], cnt_out_ref.shape)


def _router_operands(norm_gain, w_rg, b_rg, w_re, b_re):
    pad = LANES - N_GROUPS - N_EXPERTS
    wr = jnp.pad(jnp.concatenate([w_rg, w_re], axis=1).astype(F32), ((0, 0), (0, pad)))
    br = jnp.pad(jnp.concatenate([b_rg, b_re]).astype(F32), (0, pad))[None]
    r = jnp.arange(TOK_TILE)
    tri = (r[None, :] < r[:, None]).astype(BF16)
    wr_hi = wr.astype(BF16)
    wr_lo = (wr - wr_hi.astype(F32)).astype(BF16)
    return norm_gain[None].astype(F32), jnp.stack([wr_hi, wr_lo]), br, tri


def _route_specs(n_tok):
    full = lambda shape: pl.BlockSpec(shape, lambda i: (0,) * len(shape))
    in_specs = [full((1, D_MODEL)), full((2, D_MODEL, LANES)), full((1, LANES)), full((TOK_TILE, TOK_TILE))]
    tok = lambda w: pl.BlockSpec((TOK_TILE, w), lambda i: (i, 0))
    out_specs = [tok(D_MODEL), pl.BlockSpec((TOK_TILE * ROW_CHUNKS, LANES), lambda i: (i, 0)),
                 pl.BlockSpec((SUBLANES, TOK_TILE), lambda i: (0, i)), tok(LANES), full((SUBLANES, LANES))]
    out_shape = [jax.ShapeDtypeStruct((n_tok, D_MODEL), F32), jax.ShapeDtypeStruct((n_tok * ROW_CHUNKS, LANES), jnp.uint32),
                 jax.ShapeDtypeStruct((SUBLANES, n_tok), jnp.int32), jax.ShapeDtypeStruct((n_tok, LANES), F32),
                 jax.ShapeDtypeStruct((SUBLANES, LANES), F32)]
    return in_specs, out_specs, out_shape


def _gelu_tanh(x):
    return 0.5 * x * (1.0 + jnp.tanh(math.sqrt(2.0 / math.pi) * (x + 0.044715 * (x * x * x))))


def _outproj_even_kernel(x_ref, y_ref, a_ref, wglu_ref, bglu_ref, wout_ref, gain_ref, wr_ref, br_ref, tri_ref,
                         h_ref, hn_ref, ids_ref, gate_ref, cnt_out_ref, cnt_ref):
    @pl.when(pl.program_id(0) == 0)
    def _():
        cnt_ref[...] = jnp.zeros_like(cnt_ref)

    y = _gelu_tanh(jnp.concatenate([y_ref[j] for j in range(SSM_SLABS)], axis=1))
    y = y * jax.nn.sigmoid(jnp.dot(y.astype(BF16), wglu_ref[...], preferred_element_type=F32) + bglu_ref[...])
    mix = jnp.dot(y.astype(BF16), wout_ref[0:SSM_WIDTH, :], preferred_element_type=F32)
    for j in range(ATT_WIDTH // LANES):
        rows = slice(SSM_WIDTH + j * LANES, SSM_WIDTH + (j + 1) * LANES)
        mix = mix + jnp.dot(a_ref[j], wout_ref[rows, :], preferred_element_type=F32)
    _route_epilogue(x_ref[...] + mix, gain_ref, wr_ref, br_ref, tri_ref, cnt_ref,
                    h_ref, hn_ref, ids_ref, gate_ref, cnt_out_ref)


def _outproj_even(x2, y_pre, attn, w_glu, b_glu, w_out, route_ops):
    n_tok = x2.shape[0]
    n_slab = attn.shape[0]
    r_in, r_out, r_shape = _route_specs(n_tok)
    full = lambda shape: pl.BlockSpec(shape, lambda i: (0,) * len(shape))
    return pl.pallas_call(
        _outproj_even_kernel,
        grid=(n_tok // TOK_TILE,),
        in_specs=[pl.BlockSpec((TOK_TILE, D_MODEL), lambda i: (i, 0)),
                  pl.BlockSpec((SSM_SLABS, TOK_TILE, LANES), lambda i: (0, i, 0)),
                  pl.BlockSpec((n_slab, TOK_TILE, LANES), lambda i: (0, i, 0)),
                  full(w_glu.shape), full((1, SSM_WIDTH)), full(w_out.shape)] + r_in,
        out_specs=r_out, out_shape=r_shape,
        scratch_shapes=[pltpu.VMEM((1, LANES), F32)],
        compiler_params=_cparams(("arbitrary",)),
        name="outproj_even",
    )(x2, y_pre, attn, w_glu.astype(BF16), b_glu[None].astype(F32), w_out.astype(BF16), *route_ops)


def _conv_layer_kernel(h_ref, pgate_ref, y0_ref, y1_ref, gmix_ref, win_ref, cw_ref, wout_ref, gain_ref, wr_ref, br_ref,
                       tri_ref, ho_ref, hn_ref, ids_ref, gate_ref, cnt_out_ref, cnt_ref, zc_ref, *, tiles_per_seq):
    i = pl.program_id(0)

    @pl.when(i == 0)
    def _():
        cnt_ref[...] = jnp.zeros_like(cnt_ref)

    @pl.when(i % tiles_per_seq == 0)
    def _():
        zc_ref[0:SUBLANES, :] = jnp.zeros((SUBLANES, D_MODEL), F32)

    pgate = pgate_ref[...]
    h = h_ref[...] + pgate[:, 0:1] * _load_row_tiles(y0_ref) + pgate[:, 1:2] * _load_row_tiles(y1_ref)
    tm = h.shape[0]
    hn = _rms(h, gmix_ref[...]).astype(BF16)
    c = D_MODEL
    b_gate = jnp.dot(hn, win_ref[:, 0:c], preferred_element_type=F32)
    zc = jnp.dot(hn, win_ref[:, c:2 * c], preferred_element_type=F32) * jnp.dot(hn, win_ref[:, 2 * c:3 * c], preferred_element_type=F32)
    zc_ref[SUBLANES:SUBLANES + tm, :] = zc
    conv = cw_ref[CONV_TAPS - 1:CONV_TAPS, :] * zc
    for back in range(1, CONV_TAPS):
        tap = CONV_TAPS - 1 - back
        conv = conv + cw_ref[tap:tap + 1, :] * zc_ref[SUBLANES - back:SUBLANES - back + tm, :]
    zc_ref[0:SUBLANES, :] = zc_ref[tm:tm + SUBLANES, :]
    mix = jnp.dot((b_gate * conv).astype(BF16), wout_ref[...], preferred_element_type=F32)
    _route_epilogue(h + mix, gain_ref, wr_ref, br_ref, tri_ref, cnt_ref,
                    ho_ref, hn_ref, ids_ref, gate_ref, cnt_out_ref)


def _conv_layer(h1, pgate, y2, gain_mix, w_in, conv_w, w_out, route_ops, s_len):
    n_tok = h1.shape[0]
    r_in, r_out, r_shape = _route_specs(n_tok)
    full = lambda shape: pl.BlockSpec(shape, lambda i: (0,) * len(shape))
    return pl.pallas_call(
        functools.partial(_conv_layer_kernel, tiles_per_seq=s_len // TOK_TILE),
        grid=(n_tok // TOK_TILE,),
        in_specs=_moe_specs(n_tok) + [full((1, D_MODEL)), full(w_in.shape), full(conv_w.shape), full(w_out.shape)] + r_in,
        out_specs=r_out, out_shape=r_shape,
        scratch_shapes=[pltpu.VMEM((1, LANES), F32), pltpu.VMEM((TOK_TILE + SUBLANES, D_MODEL), F32)],
        compiler_params=_cparams(("arbitrary",)),
        name="conv_layer",
    )(h1, pgate, y2, y2, gain_mix[None].astype(F32), w_in.astype(BF16), conv_w.astype(F32), w_out.astype(BF16),
      *route_ops)


def _row_map_kernel(dest_ref, cnt_ref, pend_ref, src_ref, *, n_assign, n_rows):
    def fill(i, carry):
        src_ref[i] = n_assign + (i & (MOE_BLK - 1))
        return carry

    lax.fori_loop(0, MOE_BLK, fill, 0, unroll=8)
    for e in range(N_EXPERTS):
        first = MOE_BLK + (pend_ref[e - 1] if e else 0)
        lax.fori_loop(first + cnt_ref[e], MOE_BLK + pend_ref[e], fill, 0)
    lax.fori_loop(MOE_BLK + pend_ref[N_EXPERTS - 1], n_rows + 2 * MOE_BLK, fill, 0)

    def put(n, carry):
        src_ref[dest_ref[n] + MOE_BLK] = n
        return carry

    lax.fori_loop(0, n_assign, put, 0, unroll=8)


def _row_map(dest_flat, cnt, pends, n_rows):
    n_assign = dest_flat.shape[0]
    return pl.pallas_call(
        functools.partial(_row_map_kernel, n_assign=n_assign, n_rows=n_rows),
        grid_spec=pltpu.PrefetchScalarGridSpec(
            num_scalar_prefetch=3, grid=(1,), in_specs=[],
            out_specs=pl.BlockSpec(memory_space=pltpu.SMEM)),
        out_shape=jax.ShapeDtypeStruct((n_rows + 2 * MOE_BLK,), jnp.int32),
        compiler_params=_cparams(("arbitrary",)),
        name="moe_row_map",
    )(dest_flat, cnt, pends)


def _expert_kernel(blk_e_ref, n_used_ref, src_ref, hn_ref, wg_ref, wu_ref, wd_ref, y2_ref,
                   xbuf, ybuf, wg_s, wu_s, wd_s, gsem, ssem, *, n_tok):
    b = pl.program_id(0)
    n_used = n_used_ref[0]
    cur = b % 2
    nxt = 1 - cur

    tile = lambda i: pl.ds(i * ROW_CHUNKS, ROW_CHUNKS)

    def gather(blk, slot, i):
        row = src_ref[(blk + 1) * MOE_BLK + i] & (n_tok - 1)
        return pltpu.make_async_copy(hn_ref.at[row], xbuf.at[slot, tile(i)], gsem.at[slot])

    def scatter(blk, slot, i):
        row = src_ref[(blk + 1) * MOE_BLK + i]
        return pltpu.make_async_copy(ybuf.at[slot, tile(i)], y2_ref.at[row], ssem.at[slot])

    wait_block = lambda buf, sem, slot: pltpu.make_async_copy(buf.at[slot], buf.at[slot], sem.at[slot]).wait()

    @pl.when(b == 0)
    def _():
        ybuf[1] = jnp.zeros(ybuf.shape[1:], ybuf.dtype)
        for i in range(MOE_BLK):
            gather(0, 0, i).start()

    @pl.when(b < n_used)
    def _():
        new_expert = jnp.logical_or(b == 0, blk_e_ref[b] != blk_e_ref[jnp.maximum(b - 1, 0)])

        @pl.when(new_expert)
        def _():
            wg_s[...] = wg_ref[...].astype(BF16)
            wu_s[...] = wu_ref[...].astype(BF16)
            wd_s[...] = wd_ref[...].astype(BF16)

        wait_block(xbuf, gsem, cur)

        @pl.when(b >= 1)
        def _():
            wait_block(ybuf, ssem, cur)

        for i in range(MOE_BLK):
            gather(b + 1, nxt, i).start(priority=i % 2)
            scatter(b - 1, nxt, i).start(priority=i % 2)
        x = _load_row_tiles(xbuf, (cur,)).astype(BF16)
        g = jnp.dot(x, wg_s[...], preferred_element_type=F32)
        u = jnp.dot(x, wu_s[...], preferred_element_type=F32)
        hb = (g * jax.nn.sigmoid(g) * u).astype(BF16)
        _store_row_tiles(ybuf, jnp.dot(hb, wd_s[...], preferred_element_type=F32), (cur,))

    @pl.when(b == n_used)
    def _():
        wait_block(xbuf, gsem, cur)
        wait_block(ybuf, ssem, cur)
        for i in range(MOE_BLK):
            scatter(b - 1, nxt, i).start()
        wait_block(ybuf, ssem, nxt)


def _expert_ffn(hn, src, blk_e, n_used, w_g, w_u, w_d, layer):
    n_tok = hn.shape[0] // ROW_CHUNKS
    assert n_tok & (n_tok - 1) == 0, "dump-row aliasing masks the token index with T - 1"
    n_blocks = (src.shape[0] - 2 * MOE_BLK) // MOE_BLK
    last = lambda b, n: jnp.maximum(jnp.minimum(b, n[0] - 1), 0)
    w_spec = lambda rows, cols: pl.BlockSpec((None, None, rows, cols), lambda b, e, n, s: (layer, e[last(b, n)], 0, 0))
    return pl.pallas_call(
        functools.partial(_expert_kernel, n_tok=n_tok),
        grid_spec=pltpu.PrefetchScalarGridSpec(
            num_scalar_prefetch=3,
            grid=(n_blocks + 1,),
            in_specs=[pl.BlockSpec(memory_space=pltpu.HBM),
                      w_spec(D_MODEL, D_EXPERT), w_spec(D_MODEL, D_EXPERT), w_spec(D_EXPERT, D_MODEL)],
            out_specs=pl.BlockSpec(memory_space=pltpu.HBM),
            scratch_shapes=[pltpu.VMEM((2, MOE_BLK * ROW_CHUNKS, LANES), jnp.uint32),
                            pltpu.VMEM((2, MOE_BLK * ROW_CHUNKS, LANES), jnp.uint32),
                            pltpu.VMEM((D_MODEL, D_EXPERT), BF16), pltpu.VMEM((D_MODEL, D_EXPERT), BF16),
                            pltpu.VMEM((D_EXPERT, D_MODEL), BF16),
                            pltpu.SemaphoreType.DMA((2,)), pltpu.SemaphoreType.DMA((2,))]),
        out_shape=jax.ShapeDtypeStruct((2 * n_tok + MOE_BLK, ROW_CHUNKS, LANES), jnp.uint32),
        compiler_params=_cparams(("arbitrary",)),
        name="moe_experts",
    )(blk_e, n_used, src, hn.reshape(n_tok, ROW_CHUNKS, LANES), w_g, w_u, w_d)


def _moe_add_kernel(h_ref, gate_ref, y0_ref, y1_ref, o_ref):
    gate = gate_ref[...]
    o_ref[...] = h_ref[...] + gate[:, 0:1] * _load_row_tiles(y0_ref) + gate[:, 1:2] * _load_row_tiles(y1_ref)


def _moe_specs(n_tok):
    slots = n_tok // TOK_TILE
    y2_rows = (TOK_TILE * ROW_CHUNKS, LANES)
    return [pl.BlockSpec((TOK_TILE, D_MODEL), lambda i: (i, 0)), pl.BlockSpec((TOK_TILE, LANES), lambda i: (i, 0)),
            pl.BlockSpec(y2_rows, lambda i: (i, 0)), pl.BlockSpec(y2_rows, lambda i: (slots + i, 0))]


def _moe_add(h, gate, y2):
    n_tok = h.shape[0]
    return pl.pallas_call(
        _moe_add_kernel,
        grid=(n_tok // TOK_TILE,),
        in_specs=_moe_specs(n_tok),
        out_specs=pl.BlockSpec((TOK_TILE, D_MODEL), lambda i: (i, 0)),
        out_shape=jax.ShapeDtypeStruct((n_tok, D_MODEL), F32),
        compiler_params=_cparams(("parallel",)),
        name="moe_add",
    )(h, gate, y2, y2)


def _moe(hn, ids, counts, w_g, w_u, w_d, layer):
    n_tok = hn.shape[0] // ROW_CHUNKS
    n_assign = 2 * n_tok
    n_blocks = n_assign // MOE_BLK + N_EXPERTS
    n_rows = n_blocks * MOE_BLK
    cnt = counts[0, :N_EXPERTS].astype(jnp.int32)
    padded = (cnt + MOE_BLK - 1) // MOE_BLK * MOE_BLK
    pends = jnp.cumsum(padded).astype(jnp.int32)
    pstarts = pends - padded
    first_row = sum(jnp.where(ids[0:2] == e, pstarts[e], 0) for e in range(N_EXPERTS))
    dest = (first_row + ids[2:4]).reshape(n_assign)
    blk_start = jnp.arange(n_blocks + 1, dtype=jnp.int32) * MOE_BLK
    blk_e = jnp.minimum(jnp.sum(pends[None, :] <= blk_start[:, None], axis=1), N_EXPERTS - 1).astype(jnp.int32)
    n_used = (pends[-1:] // MOE_BLK).astype(jnp.int32)
    src = _row_map(dest, cnt, pends, n_rows)
    return _expert_ffn(hn, src, blk_e, n_used, w_g, w_u, w_d, layer).reshape(-1, LANES)


def kernel(x, norm_mix, norm_ffn, w_in_even, ssm_a_re, ssm_a_im, ssm_b_re, ssm_b_im, ssm_c_re, ssm_c_im, ssm_d,
           ssm_log_step, w_glu, b_glu, q_norm, k_norm, w_out_even, w_in_conv, conv_w, w_out_conv, w_router_group,
           b_router_group, w_router_expert, b_router_expert, w_expert_gate, w_expert_up, w_expert_down):
    bsz, s_len, d = x.shape
    x2 = x.reshape(bsz * s_len, d)
    route = lambda layer: _router_operands(norm_ffn[layer], w_router_group[layer], b_router_group[layer],
                                           w_router_expert[layer], b_router_expert[layer])
    experts = lambda layer: (w_expert_gate, w_expert_up, w_expert_down, layer)

    u, q, k, v = _inproj_even(x2, norm_mix[0], w_in_even[0], q_norm[0], k_norm[0])
    tables = _s5_tables(ssm_a_re[0], ssm_a_im[0], ssm_b_re[0], ssm_b_im[0], ssm_c_re[0], ssm_c_im[0], ssm_d[0],
                        ssm_log_step[0])
    y_pre = _s5_core(u, tables, bsz, s_len)
    attn = _dilated_attention(q, k, v, bsz, s_len)
    h, hn, ids, gate, counts = _outproj_even(x2, y_pre, attn, w_glu[0], b_glu[0], w_out_even[0], route(0))
    y2 = _moe(hn, ids, counts, *experts(0))

    h, hn, ids, gate, counts = _conv_layer(h, gate, y2, norm_mix[1], w_in_conv[0], conv_w[0], w_out_conv[0],
                                           route(1), s_len)
    y2 = _moe(hn, ids, counts, *experts(1))
    return _moe_add(h, gate, y2).reshape(bsz, s_len, d)
```

```python
import functools
import math

import jax
import jax.numpy as jnp
from jax import lax
from jax.experimental import pallas as pl
from jax.experimental.pallas import tpu as pltpu

F32 = jnp.float32
BF16 = jnp.bfloat16

D_MODEL = 1024
SSM_GROUP = 16
SSM_GROUPS = 40
SSM_WIDTH = SSM_GROUP * SSM_GROUPS
SSM_STATE = 64
ATT_HEADS = 6
ATT_HEAD_DIM = 64
ATT_WIDTH = ATT_HEADS * ATT_HEAD_DIM
DILATIONS = (1, 4, 16)
ATT_BLK = 128
CONV_TAPS = 3
N_GROUPS = 4
EXPERTS_PER_GROUP = 8
N_EXPERTS = N_GROUPS * EXPERTS_PER_GROUP
D_EXPERT = 512
MOE_BLK = 256
RMS_EPS = 1e-6
NEG_INF = -1e30

LANES = 128
SUBLANES = 8
VMEM_LIMIT = 56 * 1024 * 1024

TOK_TILE = 512
SSM_CHUNK = 8
SSM_SLAB_GROUPS = LANES // SSM_GROUP
SSM_SLABS = SSM_WIDTH // LANES
S5_ROW_TILE = 256
ATT_SB = 2048
ROW_CHUNKS = D_MODEL // (2 * LANES)


def _cparams(sem):
    return pltpu.CompilerParams(dimension_semantics=sem, vmem_limit_bytes=VMEM_LIMIT)


def _rms(x, gain):
    return x * lax.rsqrt(jnp.mean(x * x, axis=-1, keepdims=True) + RMS_EPS) * gain


HIGH_HALF = 0xFFFF0000


def _load_row_tiles(ref, lead=()):
    n = ref.shape[-2] // ROW_CHUNKS
    bits = jnp.concatenate([ref[lead + (pl.ds(c, n, stride=ROW_CHUNKS), slice(None))] for c in range(ROW_CHUNKS)], axis=1)
    low = lax.bitcast_convert_type(bits << 16, F32)
    high = lax.bitcast_convert_type(bits & jnp.uint32(HIGH_HALF), F32)
    return jnp.concatenate([low, high], axis=1)


def _store_row_tiles(ref, value, lead=()):
    n, half = value.shape[0], value.shape[1] // 2
    bf16_bits = lambda t: lax.bitcast_convert_type(t.astype(BF16).astype(F32), jnp.uint32)
    bits = (bf16_bits(value[:, :half]) >> 16) | (bf16_bits(value[:, half:]) & jnp.uint32(HIGH_HALF))
    for c in range(ROW_CHUNKS):
        ref[lead + (pl.ds(c, n, stride=ROW_CHUNKS), slice(None))] = bits[:, c * LANES:(c + 1) * LANES]


def _head_norm(t, gain, bd):
    tt = t * t
    hi = tt.astype(BF16)
    lo = (tt - hi.astype(F32)).astype(BF16)
    ss = jnp.dot(hi, bd, preferred_element_type=F32) + jnp.dot(lo, bd, preferred_element_type=F32)
    return t * lax.rsqrt(ss * (1.0 / ATT_HEAD_DIM) + RMS_EPS) * gain


def _inproj_even_kernel(x_ref, g_ref, w_ref, bd_ref, qn_ref, kn_ref, u_ref, q_ref, k_ref, v_ref):
    hn = _rms(x_ref[...], g_ref[...]).astype(BF16)
    proj = jnp.dot(hn, w_ref[...], preferred_element_type=F32)
    for j in range(SSM_SLABS):
        u_ref[j] = proj[:, j * LANES:(j + 1) * LANES]
    bd = bd_ref[...]
    o = SSM_WIDTH
    q = _head_norm(proj[:, o:o + ATT_WIDTH], qn_ref[...], bd) * (ATT_HEAD_DIM ** -0.5)
    k = _head_norm(proj[:, o + ATT_WIDTH:o + 2 * ATT_WIDTH], kn_ref[...], bd)
    v = proj[:, o + 2 * ATT_WIDTH:o + 3 * ATT_WIDTH]
    for j in range(ATT_WIDTH // LANES):
        q_ref[j] = q[:, j * LANES:(j + 1) * LANES]
        k_ref[j] = k[:, j * LANES:(j + 1) * LANES]
        v_ref[j] = v[:, j * LANES:(j + 1) * LANES]


def _inproj_even(x2, gain, w_in, q_norm, k_norm):
    n_tok = x2.shape[0]
    n_slab = ATT_WIDTH // LANES
    head_of = jnp.arange(ATT_WIDTH) // ATT_HEAD_DIM
    bd = (head_of[:, None] == head_of[None, :]).astype(BF16)
    qn = jnp.tile(q_norm.astype(F32), ATT_HEADS)[None]
    kn = jnp.tile(k_norm.astype(F32), ATT_HEADS)[None]
    full = lambda shape: pl.BlockSpec(shape, lambda i: (0,) * len(shape))
    slab = pl.BlockSpec((n_slab, TOK_TILE, LANES), lambda i: (0, i, 0))
    slab_shape = jax.ShapeDtypeStruct((n_slab, n_tok, LANES), F32)
    return pl.pallas_call(
        _inproj_even_kernel,
        grid=(n_tok // TOK_TILE,),
        in_specs=[pl.BlockSpec((TOK_TILE, D_MODEL), lambda i: (i, 0)), full((1, D_MODEL)),
                  full(w_in.shape), full(bd.shape), full(qn.shape), full(kn.shape)],
        out_specs=[pl.BlockSpec((SSM_SLABS, TOK_TILE, LANES), lambda i: (0, i, 0)), slab, slab, slab],
        out_shape=[jax.ShapeDtypeStruct((SSM_SLABS, n_tok, LANES), F32), slab_shape, slab_shape, slab_shape],
        compiler_params=_cparams(("parallel",)),
        name="inproj_even",
    )(x2, gain[None].astype(F32), w_in.astype(BF16), bd, qn, kn)


def _s5_tables(a_re, a_im, b_re, b_im, c_re, c_im, d_skip, log_step):
    f = lambda t: t.astype(F32)
    a_re, a_im, b_re, b_im, c_re, c_im = map(f, (a_re, a_im, b_re, b_im, c_re, c_im))
    L = SSM_CHUNK
    step = jnp.exp(f(log_step))[:, None]
    ks = jnp.arange(L + 1, dtype=F32)[:, None, None]
    mag = jnp.exp(ks * (a_re * step)[None])
    ang = ks * (a_im * step)[None]
    pw_re, pw_im = mag * jnp.cos(ang), mag * jnp.sin(ang)
    nr, ni = pw_re[1] - 1.0, pw_im[1]
    den = a_re * a_re + a_im * a_im
    z_re, z_im = (nr * a_re + ni * a_im) / den, (ni * a_re - nr * a_im) / den
    bb_re = z_re[..., None] * b_re - z_im[..., None] * b_im
    bb_im = z_re[..., None] * b_im + z_im[..., None] * b_re
    lb_re = pw_re[..., None] * bb_re[None] - pw_im[..., None] * bb_im[None]
    lb_im = pw_re[..., None] * bb_im[None] + pw_im[..., None] * bb_re[None]
    kk = jnp.einsum('gop,kgpi->gkio', c_re, lb_re[:L]) - jnp.einsum('gop,kgpi->gkio', c_im, lb_im[:L])
    ti = jnp.arange(L)
    lag = ti[None, :] - ti[:, None]
    m = jnp.where((lag >= 0)[None, :, :, None, None], kk[:, jnp.maximum(lag, 0)], 0.0)
    ns, gs = SSM_SLABS, SSM_SLAB_GROUPS
    lw = L * LANES
    pm = m.reshape(ns, gs, L, L, SSM_GROUP, SSM_GROUP).transpose(0, 2, 1, 4, 3, 5).reshape(ns, lw, L * SSM_GROUP)
    fold_e = lambda t: t[:L][::-1].reshape(L, ns, gs, SSM_STATE, SSM_GROUP).transpose(1, 0, 2, 4, 3).reshape(ns, lw, SSM_STATE)
    pe = jnp.concatenate([fold_e(lb_re), fold_e(lb_im)], axis=-1)
    pw1_re, pw1_im = (t[1:].transpose(1, 0, 2)[:, :, None, :] for t in (pw_re, pw_im))
    cl_re = c_re[:, None] * pw1_re - c_im[:, None] * pw1_im
    cl_im = c_re[:, None] * pw1_im + c_im[:, None] * pw1_re
    fold_f = lambda t: t.reshape(ns, gs, L, SSM_GROUP, SSM_STATE).transpose(0, 4, 2, 1, 3).reshape(ns, SSM_STATE, lw)
    pf = jnp.concatenate([fold_f(cl_re), fold_f(-cl_im)], axis=1)
    per_chain = lambda t: jnp.tile(t.reshape(ns, gs * SSM_STATE // LANES, LANES), (1, 2, 1))
    d_vec = jnp.tile(f(d_skip).reshape(ns, 1, LANES), (1, 1, L))
    return pm.astype(BF16), pe.astype(BF16), pf.astype(BF16), per_chain(pw_re[L]), per_chain(pw_im[L]), d_vec


def _iota2(shape):
    return lax.broadcasted_iota(jnp.int32, shape, 0), lax.broadcasted_iota(jnp.int32, shape, 1)


def _widen(compact, group_major_cols, shape, r_shift, c_shift, sel_rows):
    gmask = SSM_SLAB_GROUPS - 1
    if sel_rows:
        r, c = _iota2((shape[0], compact.shape[0]))
        sel = ((r >> 9) == (c >> 6)) & ((r & (SSM_STATE - 1)) == (c & (SSM_STATE - 1)))
        wide = jnp.dot(sel.astype(BF16), compact, preferred_element_type=F32)
    else:
        r, c = _iota2((compact.shape[1], shape[1]))
        if group_major_cols:
            sel = ((c >> 9) == (r >> 6)) & ((c & (SSM_STATE - 1)) == (r & (SSM_STATE - 1)))
        else:
            sel = ((c >> 7) == (r >> 4)) & ((c & (SSM_GROUP - 1)) == (r & (SSM_GROUP - 1)))
        wide = jnp.dot(compact, sel.astype(BF16), preferred_element_type=F32)
    r, c = _iota2(shape)
    keep = ((r >> r_shift) & gmask) == ((c >> c_shift) & gmask)
    return jnp.where(keep, wide, 0.0).astype(BF16)


def _s5_kernel(u_ref, pm_ref, pe_ref, pf_ref, ar_ref, ai_ref, d_ref, y_ref, m_ref, e_ref, f_ref, x_ref, sr_ref, si_ref,
               *, n_chunk, pitch):
    L = SSM_CHUNK
    n_blk = SSM_SLAB_GROUPS * SSM_STATE // LANES
    n_re = n_blk * LANES
    lw = L * LANES
    m_ref[...] = _widen(pm_ref[...], False, (lw, lw), 4, 4, False)
    e_ref[...] = _widen(pe_ref[...], True, (lw, 2 * n_re), 4, 6, False)
    f_ref[...] = _widen(pf_ref[...], False, (2 * n_re, lw), 6, 4, True)
    tiles = [(b, c0) for b in range(2) for c0 in range(0, n_chunk, S5_ROW_TILE)]
    for b, c0 in tiles:
        r0 = b * n_chunk + c0
        for t in range(L):
            x_ref[r0:r0 + S5_ROW_TILE, t * LANES:(t + 1) * LANES] = (
                u_ref[pl.ds(r0 * L + t, S5_ROW_TILE, stride=L), :].astype(BF16))
        sl = jnp.dot(x_ref[r0:r0 + S5_ROW_TILE, :], e_ref[...], preferred_element_type=F32)
        for j in range(n_blk):
            base = (b * n_blk + j) * pitch + c0
            sr_ref[base:base + S5_ROW_TILE, :] = sl[:, j * LANES:(j + 1) * LANES]
            si_ref[base:base + S5_ROW_TILE, :] = sl[:, n_re + j * LANES:n_re + (j + 1) * LANES]
    ar, ai = ar_ref[...], ai_ref[...]
    half = LANES

    def scan_step(c, carry):
        s_re, s_im = carry
        rows = pl.ds(c, SUBLANES, stride=pitch)
        x_re, x_im = sr_ref[rows, :], si_ref[rows, :]
        sr_ref[rows, :] = s_re
        si_ref[rows, :] = s_im
        return ar * s_re - ai * s_im + x_re, ar * s_im + ai * s_re + x_im

    zero = jnp.zeros((SUBLANES, half), F32)
    lax.fori_loop(0, n_chunk, scan_step, (zero, zero), unroll=8)

    for b, c0 in tiles:
        r0 = b * n_chunk + c0
        chain = lambda ref, j: ref[(b * n_blk + j) * pitch + c0:(b * n_blk + j) * pitch + c0 + S5_ROW_TILE, :]
        sp = jnp.concatenate([chain(sr_ref, j) for j in range(n_blk)] + [chain(si_ref, j) for j in range(n_blk)],
                             axis=1).astype(BF16)
        xt = x_ref[r0:r0 + S5_ROW_TILE, :]
        y = (jnp.dot(xt, m_ref[...], preferred_element_type=F32)
             + jnp.dot(sp, f_ref[...], preferred_element_type=F32)
             + d_ref[...] * xt.astype(F32))
        for t in range(L):
            y_ref[pl.ds(r0 * L + t, S5_ROW_TILE, stride=L), :] = y[:, t * LANES:(t + 1) * LANES]


def _s5_core(u, tables, bsz, s_len):
    assert bsz == 2, "the scan packs (batch, lane block) into the 8 sublanes of one vreg"
    pm, pe, pf, a_r, a_i, d_vec = tables
    n_tok = bsz * s_len
    n_chunk = s_len // SSM_CHUNK
    pitch = n_chunk + SUBLANES
    lw = SSM_CHUNK * LANES
    n_state = 2 * SSM_SLAB_GROUPS * SSM_STATE
    slab = lambda shape, **kw: pl.BlockSpec((None,) + shape, lambda i: (i,) + (0,) * len(shape), **kw)
    once = dict(pipeline_mode=pl.Buffered(1))
    return pl.pallas_call(
        functools.partial(_s5_kernel, n_chunk=n_chunk, pitch=pitch),
        grid=(SSM_SLABS,),
        in_specs=[slab((n_tok, LANES), **once), slab(pm.shape[1:]), slab(pe.shape[1:]), slab(pf.shape[1:]),
                  slab((SUBLANES, LANES)), slab((SUBLANES, LANES)), slab((1, lw))],
        out_specs=slab((n_tok, LANES), **once),
        out_shape=jax.ShapeDtypeStruct((SSM_SLABS, n_tok, LANES), F32),
        scratch_shapes=[pltpu.VMEM((lw, lw), BF16), pltpu.VMEM((lw, n_state), BF16), pltpu.VMEM((n_state, lw), BF16),
                        pltpu.VMEM((bsz * n_chunk, lw), BF16),
                        pltpu.VMEM((SUBLANES * pitch, LANES), F32),
                        pltpu.VMEM((SUBLANES * pitch, LANES), F32)],
        compiler_params=_cparams(("parallel",)),
        name="s5_core",
    )(u, pm, pe, pf, a_r, a_i, d_vec)


def _attn_kernel(slope_ref, q_ref, kp_ref, kc_ref, vp_ref, vc_ref, o_ref, kk_ref, vv_ref, m_ref, l_ref, acc_ref):
    slab = pl.program_id(1)
    sb = pl.program_id(2)
    kk_ref[0:ATT_SB, :] = kp_ref[...]
    kk_ref[ATT_SB:2 * ATT_SB, :] = kc_ref[...]
    vv_ref[0:ATT_SB, :] = vp_ref[...]
    vv_ref[ATT_SB:2 * ATT_SB, :] = vc_ref[...]

    lane = lax.broadcasted_iota(jnp.int32, (ATT_BLK, LANES), 1)
    head0 = lane < ATT_HEAD_DIM
    qi = lax.broadcasted_iota(jnp.int32, (ATT_BLK, 2 * ATT_BLK), 0)
    kj = lax.broadcasted_iota(jnp.int32, (ATT_BLK, 2 * ATT_BLK), 1)
    back = qi + ATT_BLK - kj
    band = (back >= 0) & (back <= ATT_BLK)
    neg_steps = -back.astype(F32)
    slopes = (slope_ref[2 * slab], slope_ref[2 * slab + 1])

    for pat, dil in enumerate(DILATIONS):
        span = ATT_BLK * dil

        def tile(idx, carry, dil=dil, span=span, pat=pat):
            start = (idx // dil) * span + idx % dil
            seq_ok = jnp.logical_or(sb > 0, idx >= dil)
            valid = band & ((kj >= ATT_BLK) | seq_ok)
            if dil == 1:
                start = pl.multiple_of(start, ATT_BLK)
                rows_of = lambda first, n: pl.ds(first, n)
            else:
                rows_of = lambda first, n: pl.ds(first, n, stride=dil)
            rows = rows_of(start, ATT_BLK)
            qt = q_ref[rows, :]
            kt = kk_ref[rows_of(ATT_SB + start - span, 2 * ATT_BLK), :].astype(BF16)
            vt = vv_ref[rows_of(ATT_SB + start - span, 2 * ATT_BLK), :].astype(BF16)
            q0 = jnp.where(head0, qt, 0.0)
            parts = []
            for hh, qh in enumerate((q0, qt - q0)):
                s = lax.dot_general(qh.astype(BF16), kt, (((1,), (1,)), ((), ())), preferred_element_type=F32)
                s = jnp.where(valid, s + (slopes[hh] * float(dil)) * neg_steps, NEG_INF)
                m = jnp.max(s, axis=-1, keepdims=True)
                p = jnp.exp(s - m)
                l = jnp.sum(p, axis=-1, keepdims=True)
                o = jnp.dot(p.astype(BF16), vt, preferred_element_type=F32)
                parts.append((m, l, o))
            (m0, l0, o0), (m1, l1, o1) = parts
            m_t = jnp.where(head0, m0, m1)
            l_t = jnp.where(head0, l0, l1)
            o_t = jnp.where(head0, o0, o1)
            if pat == 0:
                m_ref[rows, :] = m_t
                l_ref[rows, :] = l_t
                acc_ref[rows, :] = o_t
            else:
                m_old = m_ref[rows, :]
                m_new = jnp.maximum(m_old, m_t)
                a = jnp.exp(m_old - m_new)
                b = jnp.exp(m_t - m_new)
                m_ref[rows, :] = m_new
                l_ref[rows, :] = a * l_ref[rows, :] + b * l_t
                acc_ref[rows, :] = a * acc_ref[rows, :] + b * o_t
            return carry

        lax.fori_loop(0, ATT_SB // ATT_BLK, tile, 0, unroll=4)

    o_ref[...] = (acc_ref[...] / l_ref[...]).astype(o_ref.dtype)


def _dilated_attention(q, k, v, bsz, s_len):
    n_slab = q.shape[0]
    shape4 = (n_slab, bsz, s_len, LANES)
    q, k, v = (t.reshape(shape4) for t in (q, k, v))
    slopes = jnp.asarray([2.0 ** (-8.0 * (h + 1) / ATT_HEADS) for h in range(ATT_HEADS)], F32)
    blk = (None, None, ATT_SB, LANES)
    cur = pl.BlockSpec(blk, lambda b, j, i, s: (j, b, i, 0))
    prev = pl.BlockSpec(blk, lambda b, j, i, s: (j, b, jnp.maximum(i - 1, 0), 0))
    out = pl.pallas_call(
        _attn_kernel,
        grid_spec=pltpu.PrefetchScalarGridSpec(
            num_scalar_prefetch=1,
            grid=(bsz, n_slab, s_len // ATT_SB),
            in_specs=[cur, prev, cur, prev, cur],
            out_specs=cur,
            scratch_shapes=[pltpu.VMEM((2 * ATT_SB, LANES), F32), pltpu.VMEM((2 * ATT_SB, LANES), F32),
                            pltpu.VMEM((ATT_SB, LANES), F32), pltpu.VMEM((ATT_SB, LANES), F32),
                            pltpu.VMEM((ATT_SB, LANES), F32)]),
        out_shape=jax.ShapeDtypeStruct(shape4, BF16),
        compiler_params=_cparams(("parallel", "parallel", "parallel")),
        name="dilated_attn",
    )(slopes, q, k, k, v, v)
    return out.reshape(n_slab, bsz * s_len, LANES)


def _route_epilogue(h, gain_ref, wr_ref, br_ref, tri_ref, cnt_ref, h_ref, hn_ref, ids_ref, gate_ref, cnt_out_ref):
    tm = h.shape[0]
    h_ref[...] = h
    hn = _rms(h, gain_ref[...])
    _store_row_tiles(hn_ref, hn)
    hn_hi = hn.astype(BF16)
    hn_lo = (hn - hn_hi.astype(F32)).astype(BF16)
    logits = (jnp.dot(hn_hi, wr_ref[0], preferred_element_type=F32)
              + (jnp.dot(hn_hi, wr_ref[1], preferred_element_type=F32)
                 + jnp.dot(hn_lo, wr_ref[0], preferred_element_type=F32))) + br_ref[...]
    lane = lax.broadcasted_iota(jnp.int32, (tm, LANES), 1)
    big = jnp.int32(LANES)
    rmax = lambda t: jnp.max(t, axis=-1, keepdims=True)
    rmin = lambda t: jnp.min(t, axis=-1, keepdims=True)
    rsum = lambda t: jnp.sum(t, axis=-1, keepdims=True)
    gmask = lane < N_GROUPS
    gl = jnp.where(gmask, logits, -jnp.inf)
    gmax = rmax(gl)
    ge = jnp.where(gmask, jnp.exp(gl - gmax), 0.0)
    gprob = ge / rsum(ge)
    g_w = rmax(gprob)
    grp = rmin(jnp.where(gmask & (gprob == g_w), lane, big))
    group_of_lane = (lane - N_GROUPS) >> int(math.log2(EXPERTS_PER_GROUP))
    emask = (lane >= N_GROUPS) & (lane < N_GROUPS + N_EXPERTS) & (group_of_lane == grp)
    el = jnp.where(emask, logits, -jnp.inf)
    ee = jnp.where(emask, jnp.exp(el - rmax(el)), 0.0)
    ep = jnp.where(emask, ee / rsum(ee), -1.0)
    p1 = rmax(ep)
    i1 = rmin(jnp.where(ep == p1, lane, big))
    ep2 = jnp.where(lane == i1, -1.0, ep)
    p2 = rmax(ep2)
    i2 = rmin(jnp.where(ep2 == p2, lane, big))
    e1, e2 = i1 - N_GROUPS, i2 - N_GROUPS
    psum = p1 + p2
    gate1, gate2 = g_w * p1 / psum, g_w * p2 / psum
    oh1, oh2 = lane == e1, lane == e2
    member = (oh1 | oh2).astype(BF16)
    before = jnp.dot(tri_ref[...], member, preferred_element_type=F32) + cnt_ref[...]
    r1 = rsum(jnp.where(oh1, before, 0.0)).astype(jnp.int32)
    r2 = rsum(jnp.where(oh2, before, 0.0)).astype(jnp.int32)
    cnt_ref[...] = cnt_ref[...] + jnp.sum(member.astype(F32), axis=0, keepdims=True)
    ids = jnp.where(lane == 0, e1, jnp.where(lane == 1, e2, jnp.where(lane == 2, r1, jnp.where(lane == 3, r2, 0))))
    ids_ref[...] = jnp.transpose(ids)[0:SUBLANES, :]
    gate_ref[...] = jnp.where(lane == 0, gate1, jnp.where(lane == 1, gate2, 0.0))
    cnt_out_ref[...] = jnp.broadcast_to(cnt_ref[...], cnt_out_ref.shape)


def _router_operands(norm_gain, w_rg, b_rg, w_re, b_re):
    pad = LANES - N_GROUPS - N_EXPERTS
    wr = jnp.pad(jnp.concatenate([w_rg, w_re], axis=1).astype(F32), ((0, 0), (0, pad)))
    br = jnp.pad(jnp.concatenate([b_rg, b_re]).astype(F32), (0, pad))[None]
    r = jnp.arange(TOK_TILE)
    tri = (r[None, :] < r[:, None]).astype(BF16)
    wr_hi = wr.astype(BF16)
    wr_lo = (wr - wr_hi.astype(F32)).astype(BF16)
    return norm_gain[None].astype(F32), jnp.stack([wr_hi, wr_lo]), br, tri


def _route_specs(n_tok):
    full = lambda shape: pl.BlockSpec(shape, lambda i: (0,) * len(shape))
    in_specs = [full((1, D_MODEL)), full((2, D_MODEL, LANES)), full((1, LANES)), full((TOK_TILE, TOK_TILE))]
    tok = lambda w: pl.BlockSpec((TOK_TILE, w), lambda i: (i, 0))
    out_specs = [tok(D_MODEL), pl.BlockSpec((TOK_TILE * ROW_CHUNKS, LANES), lambda i: (i, 0)),
                 pl.BlockSpec((SUBLANES, TOK_TILE), lambda i: (0, i)), tok(LANES), full((SUBLANES, LANES))]
    out_shape = [jax.ShapeDtypeStruct((n_tok, D_MODEL), F32), jax.ShapeDtypeStruct((n_tok * ROW_CHUNKS, LANES), jnp.uint32),
                 jax.ShapeDtypeStruct((SUBLANES, n_tok), jnp.int32), jax.ShapeDtypeStruct((n_tok, LANES), F32),
                 jax.ShapeDtypeStruct((SUBLANES, LANES), F32)]
    return in_specs, out_specs, out_shape


def _gelu_tanh(x):
    return 0.5 * x * (1.0 + jnp.tanh(math.sqrt(2.0 / math.pi) * (x + 0.044715 * (x * x * x))))


def _outproj_even_kernel(x_ref, y_ref, a_ref, wglu_ref, bglu_ref, wout_ref, gain_ref, wr_ref, br_ref, tri_ref,
                         h_ref, hn_ref, ids_ref, gate_ref, cnt_out_ref, cnt_ref):
    @pl.when(pl.program_id(0) == 0)
    def _():
        cnt_ref[...] = jnp.zeros_like(cnt_ref)

    y = _gelu_tanh(jnp.concatenate([y_ref[j] for j in range(SSM_SLABS)], axis=1))
    y = y * jax.nn.sigmoid(jnp.dot(y.astype(BF16), wglu_ref[...], preferred_element_type=F32) + bglu_ref[...])
    mix = jnp.dot(y.astype(BF16), wout_ref[0:SSM_WIDTH, :], preferred_element_type=F32)
    for j in range(ATT_WIDTH // LANES):
        rows = slice(SSM_WIDTH + j * LANES, SSM_WIDTH + (j + 1) * LANES)
        mix = mix + jnp.dot(a_ref[j], wout_ref[rows, :], preferred_element_type=F32)
    _route_epilogue(x_ref[...] + mix, gain_ref, wr_ref, br_ref, tri_ref, cnt_ref,
                    h_ref, hn_ref, ids_ref, gate_ref, cnt_out_ref)


def _outproj_even(x2, y_pre, attn, w_glu, b_glu, w_out, route_ops):
    n_tok = x2.shape[0]
    n_slab = attn.shape[0]
    r_in, r_out, r_shape = _route_specs(n_tok)
    full = lambda shape: pl.BlockSpec(shape, lambda i: (0,) * len(shape))
    return pl.pallas_call(
        _outproj_even_kernel,
        grid=(n_tok // TOK_TILE,),
        in_specs=[pl.BlockSpec((TOK_TILE, D_MODEL), lambda i: (i, 0)),
                  pl.BlockSpec((SSM_SLABS, TOK_TILE, LANES), lambda i: (0, i, 0)),
                  pl.BlockSpec((n_slab, TOK_TILE, LANES), lambda i: (0, i, 0)),
                  full(w_glu.shape), full((1, SSM_WIDTH)), full(w_out.shape)] + r_in,
        out_specs=r_out, out_shape=r_shape,
        scratch_shapes=[pltpu.VMEM((1, LANES), F32)],
        compiler_params=_cparams(("arbitrary",)),
        name="outproj_even",
    )(x2, y_pre, attn, w_glu.astype(BF16), b_glu[None].astype(F32), w_out.astype(BF16), *route_ops)


def _conv_layer_kernel(h_ref, pgate_ref, y0_ref, y1_ref, gmix_ref, win_ref, cw_ref, wout_ref, gain_ref, wr_ref, br_ref,
                       tri_ref, ho_ref, hn_ref, ids_ref, gate_ref, cnt_out_ref, cnt_ref, zc_ref, *, tiles_per_seq):
    i = pl.program_id(0)

    @pl.when(i == 0)
    def _():
        cnt_ref[...] = jnp.zeros_like(cnt_ref)

    @pl.when(i % tiles_per_seq == 0)
    def _():
        zc_ref[0:SUBLANES, :] = jnp.zeros((SUBLANES, D_MODEL), F32)

    pgate = pgate_ref[...]
    h = h_ref[...] + pgate[:, 0:1] * _load_row_tiles(y0_ref) + pgate[:, 1:2] * _load_row_tiles(y1_ref)
    tm = h.shape[0]
    hn = _rms(h, gmix_ref[...]).astype(BF16)
    c = D_MODEL
    b_gate = jnp.dot(hn, win_ref[:, 0:c], preferred_element_type=F32)
    zc = jnp.dot(hn, win_ref[:, c:2 * c], preferred_element_type=F32) * jnp.dot(hn, win_ref[:, 2 * c:3 * c], preferred_element_type=F32)
    zc_ref[SUBLANES:SUBLANES + tm, :] = zc
    conv = cw_ref[CONV_TAPS - 1:CONV_TAPS, :] * zc
    for back in range(1, CONV_TAPS):
        tap = CONV_TAPS - 1 - back
        conv = conv + cw_ref[tap:tap + 1, :] * zc_ref[SUBLANES - back:SUBLANES - back + tm, :]
    zc_ref[0:SUBLANES, :] = zc_ref[tm:tm + SUBLANES, :]
    mix = jnp.dot((b_gate * conv).astype(BF16), wout_ref[...], preferred_element_type=F32)
    _route_epilogue(h + mix, gain_ref, wr_ref, br_ref, tri_ref, cnt_ref,
                    ho_ref, hn_ref, ids_ref, gate_ref, cnt_out_ref)


def _conv_layer(h1, pgate, y2, gain_mix, w_in, conv_w, w_out, route_ops, s_len):
    n_tok = h1.shape[0]
    r_in, r_out, r_shape = _route_specs(n_tok)
    full = lambda shape: pl.BlockSpec(shape, lambda i: (0,) * len(shape))
    return pl.pallas_call(
        functools.partial(_conv_layer_kernel, tiles_per_seq=s_len // TOK_TILE),
        grid=(n_tok // TOK_TILE,),
        in_specs=_moe_specs(n_tok) + [full((1, D_MODEL)), full(w_in.shape), full(conv_w.shape), full(w_out.shape)] + r_in,
        out_specs=r_out, out_shape=r_shape,
        scratch_shapes=[pltpu.VMEM((1, LANES), F32), pltpu.VMEM((TOK_TILE + SUBLANES, D_MODEL), F32)],
        compiler_params=_cparams(("arbitrary",)),
        name="conv_layer",
    )(h1, pgate, y2, y2, gain_mix[None].astype(F32), w_in.astype(BF16), conv_w.astype(F32), w_out.astype(BF16),
      *route_ops)


def _row_map_kernel(dest_ref, cnt_ref, pend_ref, src_ref, *, n_assign, n_rows):
    def fill(i, carry):
        src_ref[i] = n_assign + (i & (MOE_BLK - 1))
        return carry

    lax.fori_loop(0, MOE_BLK, fill, 0, unroll=8)
    for e in range(N_EXPERTS):
        first = MOE_BLK + (pend_ref[e - 1] if e else 0)
        lax.fori_loop(first + cnt_ref[e], MOE_BLK + pend_ref[e], fill, 0)
    lax.fori_loop(MOE_BLK + pend_ref[N_EXPERTS - 1], n_rows + 2 * MOE_BLK, fill, 0)

    def put(n, carry):
        src_ref[dest_ref[n] + MOE_BLK] = n
        return carry

    lax.fori_loop(0, n_assign, put, 0, unroll=8)


def _row_map(dest_flat, cnt, pends, n_rows):
    n_assign = dest_flat.shape[0]
    return pl.pallas_call(
        functools.partial(_row_map_kernel, n_assign=n_assign, n_rows=n_rows),
        grid_spec=pltpu.PrefetchScalarGridSpec(
            num_scalar_prefetch=3, grid=(1,), in_specs=[],
            out_specs=pl.BlockSpec(memory_space=pltpu.SMEM)),
        out_shape=jax.ShapeDtypeStruct((n_rows + 2 * MOE_BLK,), jnp.int32),
        compiler_params=_cparams(("arbitrary",)),
        name="moe_row_map",
    )(dest_flat, cnt, pends)


def _expert_kernel(blk_e_ref, n_used_ref, src_ref, hn_ref, wg_ref, wu_ref, wd_ref, y2_ref,
                   xbuf, ybuf, wg_s, wu_s, wd_s, ssem, *, n_tok):
    b = pl.program_id(0)
    n_used = n_used_ref[0]
    cur = b % 2
    nxt = 1 - cur

    tile = lambda i: pl.ds(i * ROW_CHUNKS, ROW_CHUNKS)

    def scatter(blk, slot, i):
        row = src_ref[(blk + 1) * MOE_BLK + i]
        return pltpu.make_async_copy(ybuf.at[slot, tile(i)], y2_ref.at[row], ssem.at[slot])

    wait_block = lambda slot: pltpu.make_async_copy(ybuf.at[slot], ybuf.at[slot], ssem.at[slot]).wait()

    @pl.when(b == 0)
    def _():
        ybuf[1] = jnp.zeros(ybuf.shape[1:], ybuf.dtype)

    @pl.when(b < n_used)
    def _():
        new_expert = jnp.logical_or(b == 0, blk_e_ref[b] != blk_e_ref[jnp.maximum(b - 1, 0)])

        @pl.when(new_expert)
        def _():
            wg_s[...] = wg_ref[...].astype(BF16)
            wu_s[...] = wu_ref[...].astype(BF16)
            wd_s[...] = wd_ref[...].astype(BF16)

        @pl.when(b >= 1)
        def _():
            wait_block(cur)

        for i in range(MOE_BLK):
            tok = src_ref[(b + 1) * MOE_BLK + i] & (n_tok - 1)
            xbuf[tile(i), :] = hn_ref[pl.ds(pl.multiple_of(tok * ROW_CHUNKS, ROW_CHUNKS), ROW_CHUNKS), :]
            scatter(b - 1, nxt, i).start(priority=i % 2)
        x = _load_row_tiles(xbuf).astype(BF16)
        g = jnp.dot(x, wg_s[...], preferred_element_type=F32)
        u = jnp.dot(x, wu_s[...], preferred_element_type=F32)
        hb = (g * jax.nn.sigmoid(g) * u).astype(BF16)
        _store_row_tiles(ybuf, jnp.dot(hb, wd_s[...], preferred_element_type=F32), (cur,))

    @pl.when(b == n_used)
    def _():
        wait_block(cur)
        for i in range(MOE_BLK):
            scatter(b - 1, nxt, i).start(priority=i % 2)
        wait_block(nxt)


def _expert_ffn(hn, src, blk_e, n_used, w_g, w_u, w_d, layer):
    n_tok = hn.shape[0] // ROW_CHUNKS
    assert n_tok & (n_tok - 1) == 0, "dump-row aliasing masks the token index with T - 1"
    n_blocks = (src.shape[0] - 2 * MOE_BLK) // MOE_BLK
    last = lambda b, n: jnp.maximum(jnp.minimum(b, n[0] - 1), 0)
    w_spec = lambda rows, cols: pl.BlockSpec((None, None, rows, cols), lambda b, e, n, s: (layer, e[last(b, n)], 0, 0))
    return pl.pallas_call(
        functools.partial(_expert_kernel, n_tok=n_tok),
        grid_spec=pltpu.PrefetchScalarGridSpec(
            num_scalar_prefetch=3,
            grid=(n_blocks + 1,),
            in_specs=[pl.BlockSpec(hn.shape, lambda b, e, n, s: (0, 0), pipeline_mode=pl.Buffered(1)),
                      w_spec(D_MODEL, D_EXPERT), w_spec(D_MODEL, D_EXPERT), w_spec(D_EXPERT, D_MODEL)],
            out_specs=pl.BlockSpec(memory_space=pltpu.HBM),
            scratch_shapes=[pltpu.VMEM((MOE_BLK * ROW_CHUNKS, LANES), jnp.uint32),
                            pltpu.VMEM((2, MOE_BLK * ROW_CHUNKS, LANES), jnp.uint32),
                            pltpu.VMEM((D_MODEL, D_EXPERT), BF16), pltpu.VMEM((D_MODEL, D_EXPERT), BF16),
                            pltpu.VMEM((D_EXPERT, D_MODEL), BF16),
                            pltpu.SemaphoreType.DMA((2,))]),
        out_shape=jax.ShapeDtypeStruct((2 * n_tok + MOE_BLK, ROW_CHUNKS, LANES), jnp.uint32),
        compiler_params=_cparams(("arbitrary",)),
        name="moe_experts",
    )(blk_e, n_used, src, hn, w_g, w_u, w_d)


def _moe_add_kernel(h_ref, gate_ref, y0_ref, y1_ref, o_ref):
    gate = gate_ref[...]
    o_ref[...] = h_ref[...] + gate[:, 0:1] * _load_row_tiles(y0_ref) + gate[:, 1:2] * _load_row_tiles(y1_ref)


def _moe_specs(n_tok):
    slots = n_tok // TOK_TILE
    y2_rows = (TOK_TILE * ROW_CHUNKS, LANES)
    return [pl.BlockSpec((TOK_TILE, D_MODEL), lambda i: (i, 0)), pl.BlockSpec((TOK_TILE, LANES), lambda i: (i, 0)),
            pl.BlockSpec(y2_rows, lambda i: (i, 0)), pl.BlockSpec(y2_rows, lambda i: (slots + i, 0))]


def _moe_add(h, gate, y2):
    n_tok = h.shape[0]
    return pl.pallas_call(
        _moe_add_kernel,
        grid=(n_tok // TOK_TILE,),
        in_specs=_moe_specs(n_tok),
        out_specs=pl.BlockSpec((TOK_TILE, D_MODEL), lambda i: (i, 0)),
        out_shape=jax.ShapeDtypeStruct((n_tok, D_MODEL), F32),
        compiler_params=_cparams(("parallel",)),
        name="moe_add",
    )(h, gate, y2, y2)


def _moe(hn, ids, counts, w_g, w_u, w_d, layer):
    n_tok = hn.shape[0] // ROW_CHUNKS
    n_assign = 2 * n_tok
    n_blocks = n_assign // MOE_BLK + N_EXPERTS
    n_rows = n_blocks * MOE_BLK
    cnt = counts[0, :N_EXPERTS].astype(jnp.int32)
    padded = (cnt + MOE_BLK - 1) // MOE_BLK * MOE_BLK
    pends = jnp.cumsum(padded).astype(jnp.int32)
    pstarts = pends - padded
    first_row = sum(jnp.where(ids[0:2] == e, pstarts[e], 0) for e in range(N_EXPERTS))
    dest = (first_row + ids[2:4]).reshape(n_assign)
    blk_start = jnp.arange(n_blocks + 1, dtype=jnp.int32) * MOE_BLK
    blk_e = jnp.minimum(jnp.sum(pends[None, :] <= blk_start[:, None], axis=1), N_EXPERTS - 1).astype(jnp.int32)
    n_used = (pends[-1:] // MOE_BLK).astype(jnp.int32)
    src = _row_map(dest, cnt, pends, n_rows)
    return _expert_ffn(hn, src, blk_e, n_used, w_g, w_u, w_d, layer).reshape(-1, LANES)


def kernel(x, norm_mix, norm_ffn, w_in_even, ssm_a_re, ssm_a_im, ssm_b_re, ssm_b_im, ssm_c_re, ssm_c_im, ssm_d,
           ssm_log_step, w_glu, b_glu, q_norm, k_norm, w_out_even, w_in_conv, conv_w, w_out_conv, w_router_group,
           b_router_group, w_router_expert, b_router_expert, w_expert_gate, w_expert_up, w_expert_down):
    bsz, s_len, d = x.shape
    x2 = x.reshape(bsz * s_len, d)
    route = lambda layer: _router_operands(norm_ffn[layer], w_router_group[layer], b_router_group[layer],
                                           w_router_expert[layer], b_router_expert[layer])
    experts = lambda layer: (w_expert_gate, w_expert_up, w_expert_down, layer)

    u, q, k, v = _inproj_even(x2, norm_mix[0], w_in_even[0], q_norm[0], k_norm[0])
    tables = _s5_tables(ssm_a_re[0], ssm_a_im[0], ssm_b_re[0], ssm_b_im[0], ssm_c_re[0], ssm_c_im[0], ssm_d[0],
                        ssm_log_step[0])
    y_pre = _s5_core(u, tables, bsz, s_len)
    attn = _dilated_attention(q, k, v, bsz, s_len)
    h, hn, ids, gate, counts = _outproj_even(x2, y_pre, attn, w_glu[0], b_glu[0], w_out_even[0], route(0))
    y2 = _moe(hn, ids, counts, *experts(0))

    h, hn, ids, gate, counts = _conv_layer(h, gate, y2, norm_mix[1], w_in_conv[0], conv_w[0], w_out_conv[0],
                                           route(1), s_len)
    y2 = _moe(hn, ids, counts, *experts(1))
    return _moe_add(h, gate, y2).reshape(bsz, s_len, d)
```

```python
import functools
import math

import jax
import jax.numpy as jnp
from jax import lax
from jax.experimental import pallas as pl
from jax.experimental.pallas import tpu as pltpu
from jax.experimental.pallas import tpu_sc as plsc

F32 = jnp.float32
BF16 = jnp.bfloat16

D_MODEL = 1024
SSM_GROUP = 16
SSM_GROUPS = 40
SSM_WIDTH = SSM_GROUP * SSM_GROUPS
SSM_STATE = 64
ATT_HEADS = 6
ATT_HEAD_DIM = 64
ATT_WIDTH = ATT_HEADS * ATT_HEAD_DIM
DILATIONS = (16, 4, 1)
ATT_BLK = 128
CONV_TAPS = 3
N_GROUPS = 4
EXPERTS_PER_GROUP = 8
N_EXPERTS = N_GROUPS * EXPERTS_PER_GROUP
D_EXPERT = 512
MOE_BLK = 256
RMS_EPS = 1e-6
NEG_INF = -1e30

LANES = 128
SUBLANES = 8
SC_CORES, SC_SUBCORES, SC_LANES = 2, 16, 16
VMEM_LIMIT = 56 * 1024 * 1024

TOK_TILE = 512
SSM_CHUNK = 8
SSM_SLAB_GROUPS = LANES // SSM_GROUP
SSM_SLABS = SSM_WIDTH // LANES
S5_ROW_TILE = 256
ATT_SB = 2048
ROW_CHUNKS = D_MODEL // (2 * LANES)


def _cparams(sem):
    return pltpu.CompilerParams(dimension_semantics=sem, vmem_limit_bytes=VMEM_LIMIT)


def _rms(x, gain):
    return x * lax.rsqrt(jnp.mean(x * x, axis=-1, keepdims=True) + RMS_EPS) * gain


HIGH_HALF = 0xFFFF0000


def _load_row_tiles(ref, lead=()):
    n = ref.shape[-2] // ROW_CHUNKS
    bits = jnp.concatenate([ref[lead + (pl.ds(c, n, stride=ROW_CHUNKS), slice(None))] for c in range(ROW_CHUNKS)], axis=1)
    low = lax.bitcast_convert_type(bits << 16, F32)
    high = lax.bitcast_convert_type(bits & jnp.uint32(HIGH_HALF), F32)
    return jnp.concatenate([low, high], axis=1)


def _store_row_tiles(ref, value, lead=()):
    n, half = value.shape[0], value.shape[1] // 2
    bf16_bits = lambda t: lax.bitcast_convert_type(t.astype(BF16).astype(F32), jnp.uint32)
    bits = (bf16_bits(value[:, :half]) >> 16) | (bf16_bits(value[:, half:]) & jnp.uint32(HIGH_HALF))
    for c in range(ROW_CHUNKS):
        ref[lead + (pl.ds(c, n, stride=ROW_CHUNKS), slice(None))] = bits[:, c * LANES:(c + 1) * LANES]


def _head_norm(t, gain, bd):
    tt = t * t
    hi = tt.astype(BF16)
    lo = (tt - hi.astype(F32)).astype(BF16)
    ss = jnp.dot(hi, bd, preferred_element_type=F32) + jnp.dot(lo, bd, preferred_element_type=F32)
    return t * lax.rsqrt(ss * (1.0 / ATT_HEAD_DIM) + RMS_EPS) * gain


def _inproj_even_kernel(x_ref, g_ref, w_ref, bd_ref, qn_ref, kn_ref, u_ref, q_ref, k_ref, v_ref):
    hn = _rms(x_ref[...], g_ref[...]).astype(BF16)
    proj = jnp.dot(hn, w_ref[...], preferred_element_type=F32)
    for j in range(SSM_SLABS):
        u_ref[j] = proj[:, j * LANES:(j + 1) * LANES]
    bd = bd_ref[...]
    o = SSM_WIDTH
    q = _head_norm(proj[:, o:o + ATT_WIDTH], qn_ref[...], bd) * (ATT_HEAD_DIM ** -0.5)
    k = _head_norm(proj[:, o + ATT_WIDTH:o + 2 * ATT_WIDTH], kn_ref[...], bd)
    v = proj[:, o + 2 * ATT_WIDTH:o + 3 * ATT_WIDTH]
    for j in range(ATT_WIDTH // LANES):
        q_ref[j] = q[:, j * LANES:(j + 1) * LANES]
        k_ref[j] = k[:, j * LANES:(j + 1) * LANES]
        v_ref[j] = v[:, j * LANES:(j + 1) * LANES]


def _inproj_even(x2, gain, w_in, q_norm, k_norm):
    n_tok = x2.shape[0]
    n_slab = ATT_WIDTH // LANES
    head_of = jnp.arange(ATT_WIDTH) // ATT_HEAD_DIM
    bd = (head_of[:, None] == head_of[None, :]).astype(BF16)
    qn = jnp.tile(q_norm.astype(F32), ATT_HEADS)[None]
    kn = jnp.tile(k_norm.astype(F32), ATT_HEADS)[None]
    full = lambda shape: pl.BlockSpec(shape, lambda i: (0,) * len(shape))
    slab = pl.BlockSpec((n_slab, TOK_TILE, LANES), lambda i: (0, i, 0))
    slab_shape = jax.ShapeDtypeStruct((n_slab, n_tok, LANES), F32)
    return pl.pallas_call(
        _inproj_even_kernel,
        grid=(n_tok // TOK_TILE,),
        in_specs=[pl.BlockSpec((TOK_TILE, D_MODEL), lambda i: (i, 0)), full((1, D_MODEL)),
                  full(w_in.shape), full(bd.shape), full(qn.shape), full(kn.shape)],
        out_specs=[pl.BlockSpec((SSM_SLABS, TOK_TILE, LANES), lambda i: (0, i, 0)), slab, slab, slab],
        out_shape=[jax.ShapeDtypeStruct((SSM_SLABS, n_tok, LANES), F32), slab_shape, slab_shape, slab_shape],
        compiler_params=_cparams(("parallel",)),
        name="inproj_even",
    )(x2, gain[None].astype(F32), w_in.astype(BF16), bd, qn, kn)


def _s5_tables(a_re, a_im, b_re, b_im, c_re, c_im, d_skip, log_step):
    f = lambda t: t.astype(F32)
    a_re, a_im, b_re, b_im, c_re, c_im = map(f, (a_re, a_im, b_re, b_im, c_re, c_im))
    L = SSM_CHUNK
    step = jnp.exp(f(log_step))[:, None]
    ks = jnp.arange(L + 1, dtype=F32)[:, None, None]
    mag = jnp.exp(ks * (a_re * step)[None])
    ang = ks * (a_im * step)[None]
    pw_re, pw_im = mag * jnp.cos(ang), mag * jnp.sin(ang)
    nr, ni = pw_re[1] - 1.0, pw_im[1]
    den = a_re * a_re + a_im * a_im
    z_re, z_im = (nr * a_re + ni * a_im) / den, (ni * a_re - nr * a_im) / den
    bb_re = z_re[..., None] * b_re - z_im[..., None] * b_im
    bb_im = z_re[..., None] * b_im + z_im[..., None] * b_re
    lb_re = pw_re[..., None] * bb_re[None] - pw_im[..., None] * bb_im[None]
    lb_im = pw_re[..., None] * bb_im[None] + pw_im[..., None] * bb_re[None]
    kk = jnp.einsum('gop,kgpi->gkio', c_re, lb_re[:L]) - jnp.einsum('gop,kgpi->gkio', c_im, lb_im[:L])
    ti = jnp.arange(L)
    lag = ti[None, :] - ti[:, None]
    m = jnp.where((lag >= 0)[None, :, :, None, None], kk[:, jnp.maximum(lag, 0)], 0.0)
    ns, gs = SSM_SLABS, SSM_SLAB_GROUPS
    lw = L * LANES
    pm = m.reshape(ns, gs, L, L, SSM_GROUP, SSM_GROUP).transpose(0, 2, 1, 4, 3, 5).reshape(ns, lw, L * SSM_GROUP)
    fold_e = lambda t: t[:L][::-1].reshape(L, ns, gs, SSM_STATE, SSM_GROUP).transpose(1, 0, 2, 4, 3).reshape(ns, lw, SSM_STATE)
    pe = jnp.concatenate([fold_e(lb_re), fold_e(lb_im)], axis=-1)
    pw1_re, pw1_im = (t[1:].transpose(1, 0, 2)[:, :, None, :] for t in (pw_re, pw_im))
    cl_re = c_re[:, None] * pw1_re - c_im[:, None] * pw1_im
    cl_im = c_re[:, None] * pw1_im + c_im[:, None] * pw1_re
    fold_f = lambda t: t.reshape(ns, gs, L, SSM_GROUP, SSM_STATE).transpose(0, 4, 2, 1, 3).reshape(ns, SSM_STATE, lw)
    pf = jnp.concatenate([fold_f(cl_re), fold_f(-cl_im)], axis=1)
    per_chain = lambda t: jnp.tile(t.reshape(ns, gs * SSM_STATE // LANES, LANES), (1, 2, 1))
    d_vec = jnp.tile(f(d_skip).reshape(ns, 1, LANES), (1, 1, L))
    return pm.astype(BF16), pe.astype(BF16), pf.astype(BF16), per_chain(pw_re[L]), per_chain(pw_im[L]), d_vec


def _iota2(shape):
    return lax.broadcasted_iota(jnp.int32, shape, 0), lax.broadcasted_iota(jnp.int32, shape, 1)


def _widen(compact, group_major_cols, shape, r_shift, c_shift, sel_rows):
    gmask = SSM_SLAB_GROUPS - 1
    if sel_rows:
        r, c = _iota2((shape[0], compact.shape[0]))
        sel = ((r >> 9) == (c >> 6)) & ((r & (SSM_STATE - 1)) == (c & (SSM_STATE - 1)))
        wide = jnp.dot(sel.astype(BF16), compact, preferred_element_type=F32)
    else:
        r, c = _iota2((compact.shape[1], shape[1]))
        if group_major_cols:
            sel = ((c >> 9) == (r >> 6)) & ((c & (SSM_STATE - 1)) == (r & (SSM_STATE - 1)))
        else:
            sel = ((c >> 7) == (r >> 4)) & ((c & (SSM_GROUP - 1)) == (r & (SSM_GROUP - 1)))
        wide = jnp.dot(compact, sel.astype(BF16), preferred_element_type=F32)
    r, c = _iota2(shape)
    keep = ((r >> r_shift) & gmask) == ((c >> c_shift) & gmask)
    return jnp.where(keep, wide, 0.0).astype(BF16)


def _s5_kernel(u_ref, pm_ref, pe_ref, pf_ref, ar_ref, ai_ref, d_ref, y_ref, m_ref, e_ref, f_ref, x_ref, sr_ref, si_ref,
               *, n_chunk, pitch):
    L = SSM_CHUNK
    n_blk = SSM_SLAB_GROUPS * SSM_STATE // LANES
    n_re = n_blk * LANES
    lw = L * LANES
    m_ref[...] = _widen(pm_ref[...], False, (lw, lw), 4, 4, False)
    e_ref[...] = _widen(pe_ref[...], True, (lw, 2 * n_re), 4, 6, False)
    f_ref[...] = _widen(pf_ref[...], False, (2 * n_re, lw), 6, 4, True)
    tiles = [(b, c0) for b in range(2) for c0 in range(0, n_chunk, S5_ROW_TILE)]
    for b, c0 in tiles:
        r0 = b * n_chunk + c0
        for t in range(L):
            x_ref[r0:r0 + S5_ROW_TILE, t * LANES:(t + 1) * LANES] = (
                u_ref[pl.ds(r0 * L + t, S5_ROW_TILE, stride=L), :].astype(BF16))
        sl = jnp.dot(x_ref[r0:r0 + S5_ROW_TILE, :], e_ref[...], preferred_element_type=F32)
        for j in range(n_blk):
            base = (b * n_blk + j) * pitch + c0
            sr_ref[base:base + S5_ROW_TILE, :] = sl[:, j * LANES:(j + 1) * LANES]
            si_ref[base:base + S5_ROW_TILE, :] = sl[:, n_re + j * LANES:n_re + (j + 1) * LANES]
    ar, ai = ar_ref[...], ai_ref[...]
    half = LANES

    def scan_step(c, carry):
        s_re, s_im = carry
        rows = pl.ds(c, SUBLANES, stride=pitch)
        x_re, x_im = sr_ref[rows, :], si_ref[rows, :]
        sr_ref[rows, :] = s_re
        si_ref[rows, :] = s_im
        return ar * s_re - ai * s_im + x_re, ar * s_im + ai * s_re + x_im

    zero = jnp.zeros((SUBLANES, half), F32)
    lax.fori_loop(0, n_chunk, scan_step, (zero, zero), unroll=8)

    for b, c0 in tiles:
        r0 = b * n_chunk + c0
        chain = lambda ref, j: ref[(b * n_blk + j) * pitch + c0:(b * n_blk + j) * pitch + c0 + S5_ROW_TILE, :]
        sp = jnp.concatenate([chain(sr_ref, j) for j in range(n_blk)] + [chain(si_ref, j) for j in range(n_blk)],
                             axis=1).astype(BF16)
        xt = x_ref[r0:r0 + S5_ROW_TILE, :]
        y = (jnp.dot(xt, m_ref[...], preferred_element_type=F32)
             + jnp.dot(sp, f_ref[...], preferred_element_type=F32)
             + d_ref[...] * xt.astype(F32))
        for t in range(L):
            y_ref[pl.ds(r0 * L + t, S5_ROW_TILE, stride=L), :] = y[:, t * LANES:(t + 1) * LANES]


def _s5_core(u, tables, bsz, s_len):
    assert bsz == 2, "the scan packs (batch, lane block) into the 8 sublanes of one vreg"
    pm, pe, pf, a_r, a_i, d_vec = tables
    n_tok = bsz * s_len
    n_chunk = s_len // SSM_CHUNK
    pitch = n_chunk + SUBLANES
    lw = SSM_CHUNK * LANES
    n_state = 2 * SSM_SLAB_GROUPS * SSM_STATE
    slab = lambda shape, **kw: pl.BlockSpec((None,) + shape, lambda i: (i,) + (0,) * len(shape), **kw)
    once = dict(pipeline_mode=pl.Buffered(1))
    return pl.pallas_call(
        functools.partial(_s5_kernel, n_chunk=n_chunk, pitch=pitch),
        grid=(SSM_SLABS,),
        in_specs=[slab((n_tok, LANES), **once), slab(pm.shape[1:]), slab(pe.shape[1:]), slab(pf.shape[1:]),
                  slab((SUBLANES, LANES)), slab((SUBLANES, LANES)), slab((1, lw))],
        out_specs=slab((n_tok, LANES), **once),
        out_shape=jax.ShapeDtypeStruct((SSM_SLABS, n_tok, LANES), F32),
        scratch_shapes=[pltpu.VMEM((lw, lw), BF16), pltpu.VMEM((lw, n_state), BF16), pltpu.VMEM((n_state, lw), BF16),
                        pltpu.VMEM((bsz * n_chunk, lw), BF16),
                        pltpu.VMEM((SUBLANES * pitch, LANES), F32),
                        pltpu.VMEM((SUBLANES * pitch, LANES), F32)],
        compiler_params=_cparams(("parallel",)),
        name="s5_core",
    )(u, pm, pe, pf, a_r, a_i, d_vec)


def _attn_kernel(slope_ref, q_ref, kp_ref, kc_ref, vp_ref, vc_ref, o_ref, kk_ref, vv_ref, m_ref, l_ref, acc_ref):
    slab = pl.program_id(1)
    sb = pl.program_id(2)
    kk_ref[0:ATT_SB, :] = kp_ref[...]
    kk_ref[ATT_SB:2 * ATT_SB, :] = kc_ref[...]
    vv_ref[0:ATT_SB, :] = vp_ref[...]
    vv_ref[ATT_SB:2 * ATT_SB, :] = vc_ref[...]

    lane = lax.broadcasted_iota(jnp.int32, (ATT_BLK, LANES), 1)
    head0 = lane < ATT_HEAD_DIM
    qi = lax.broadcasted_iota(jnp.int32, (ATT_BLK, 2 * ATT_BLK), 0)
    kj = lax.broadcasted_iota(jnp.int32, (ATT_BLK, 2 * ATT_BLK), 1)
    back = qi + ATT_BLK - kj
    band = (back >= 0) & (back <= ATT_BLK)
    neg_steps = -back.astype(F32)
    slopes = (slope_ref[2 * slab], slope_ref[2 * slab + 1])

    for pat, dil in enumerate(DILATIONS):
        span = ATT_BLK * dil

        def tile(idx, carry, dil=dil, span=span, pat=pat):
            start = (idx // dil) * span + idx % dil
            seq_ok = jnp.logical_or(sb > 0, idx >= dil)
            valid = band & ((kj >= ATT_BLK) | seq_ok)
            if dil == 1:
                start = pl.multiple_of(start, ATT_BLK)
                rows_of = lambda first, n: pl.ds(first, n)
            else:
                rows_of = lambda first, n: pl.ds(first, n, stride=dil)
            rows = rows_of(start, ATT_BLK)
            qt = q_ref[rows, :]
            kt = kk_ref[rows_of(ATT_SB + start - span, 2 * ATT_BLK), :].astype(BF16)
            vt = vv_ref[rows_of(ATT_SB + start - span, 2 * ATT_BLK), :].astype(BF16)
            q0 = jnp.where(head0, qt, 0.0)
            parts = []
            for hh, qh in enumerate((q0, qt - q0)):
                s = lax.dot_general(qh.astype(BF16), kt, (((1,), (1,)), ((), ())), preferred_element_type=F32)
                s = jnp.where(valid, s + (slopes[hh] * float(dil)) * neg_steps, NEG_INF)
                m = jnp.max(s, axis=-1, keepdims=True)
                p = jnp.exp(s - m)
                l = jnp.sum(p, axis=-1, keepdims=True)
                o = jnp.dot(p.astype(BF16), vt, preferred_element_type=F32)
                parts.append((m, l, o))
            (m0, l0, o0), (m1, l1, o1) = parts
            m_t = jnp.where(head0, m0, m1)
            l_t = jnp.where(head0, l0, l1)
            o_t = jnp.where(head0, o0, o1)
            if pat == 0:
                m_ref[rows, :] = m_t
                l_ref[rows, :] = l_t
                acc_ref[rows, :] = o_t
            else:
                m_old = m_ref[rows, :]
                m_new = jnp.maximum(m_old, m_t)
                a = jnp.exp(m_old - m_new)
                b = jnp.exp(m_t - m_new)
                m_ref[rows, :] = m_new
                l_ref[rows, :] = a * l_ref[rows, :] + b * l_t
                acc_ref[rows, :] = a * acc_ref[rows, :] + b * o_t
            return carry

        lax.fori_loop(0, ATT_SB // ATT_BLK, tile, 0, unroll=4)

    o_ref[...] = (acc_ref[...] / l_ref[...]).astype(o_ref.dtype)


def _dilated_attention(q, k, v, bsz, s_len):
    n_slab = q.shape[0]
    shape4 = (n_slab, bsz, s_len, LANES)
    q, k, v = (t.reshape(shape4) for t in (q, k, v))
    slopes = jnp.asarray([2.0 ** (-8.0 * (h + 1) / ATT_HEADS) for h in range(ATT_HEADS)], F32)
    blk = (None, None, ATT_SB, LANES)
    cur = pl.BlockSpec(blk, lambda b, j, i, s: (j, b, i, 0))
    prev = pl.BlockSpec(blk, lambda b, j, i, s: (j, b, jnp.maximum(i - 1, 0), 0))
    out = pl.pallas_call(
        _attn_kernel,
        grid_spec=pltpu.PrefetchScalarGridSpec(
            num_scalar_prefetch=1,
            grid=(bsz, n_slab, s_len // ATT_SB),
            in_specs=[cur, prev, cur, prev, cur],
            out_specs=cur,
            scratch_shapes=[pltpu.VMEM((2 * ATT_SB, LANES), F32), pltpu.VMEM((2 * ATT_SB, LANES), F32),
                            pltpu.VMEM((ATT_SB, LANES), F32), pltpu.VMEM((ATT_SB, LANES), F32),
                            pltpu.VMEM((ATT_SB, LANES), F32)]),
        out_shape=jax.ShapeDtypeStruct(shape4, BF16),
        compiler_params=_cparams(("parallel", "parallel", "parallel")),
        name="dilated_attn",
    )(slopes, q, k, k, v, v)
    return out.reshape(n_slab, bsz * s_len, LANES)


def _route_epilogue(h, gain_ref, wr_ref, br_ref, tri_ref, cnt_ref, h_ref, hn_ref, ids_ref, gate_ref, cnt_out_ref):
    tm = h.shape[0]
    h_ref[...] = h
    hn = _rms(h, gain_ref[...])
    _store_row_tiles(hn_ref, hn)
    hn_hi = hn.astype(BF16)
    hn_lo = (hn - hn_hi.astype(F32)).astype(BF16)
    logits = (jnp.dot(hn_hi, wr_ref[0], preferred_element_type=F32)
              + (jnp.dot(hn_hi, wr_ref[1], preferred_element_type=F32)
                 + jnp.dot(hn_lo, wr_ref[0], preferred_element_type=F32))) + br_ref[...]
    lane = lax.broadcasted_iota(jnp.int32, (tm, LANES), 1)
    big = jnp.int32(LANES)
    rmax = lambda t: jnp.max(t, axis=-1, keepdims=True)
    rmin = lambda t: jnp.min(t, axis=-1, keepdims=True)
    rsum = lambda t: jnp.sum(t, axis=-1, keepdims=True)
    gmask = lane < N_GROUPS
    gl = jnp.where(gmask, logits, -jnp.inf)
    gmax = rmax(gl)
    ge = jnp.where(gmask, jnp.exp(gl - gmax), 0.0)
    gprob = ge / rsum(ge)
    g_w = rmax(gprob)
    grp = rmin(jnp.where(gmask & (gprob == g_w), lane, big))
    group_of_lane = (lane - N_GROUPS) >> int(math.log2(EXPERTS_PER_GROUP))
    emask = (lane >= N_GROUPS) & (lane < N_GROUPS + N_EXPERTS) & (group_of_lane == grp)
    el = jnp.where(emask, logits, -jnp.inf)
    ee = jnp.where(emask, jnp.exp(el - rmax(el)), 0.0)
    ep = jnp.where(emask, ee / rsum(ee), -1.0)
    p1 = rmax(ep)
    i1 = rmin(jnp.where(ep == p1, lane, big))
    ep2 = jnp.where(lane == i1, -1.0, ep)
    p2 = rmax(ep2)
    i2 = rmin(jnp.where(ep2 == p2, lane, big))
    e1, e2 = i1 - N_GROUPS, i2 - N_GROUPS
    psum = p1 + p2
    gate1, gate2 = g_w * p1 / psum, g_w * p2 / psum
    oh1, oh2 = lane == e1, lane == e2
    member = (oh1 | oh2).astype(BF16)
    before = jnp.dot(tri_ref[...], member, preferred_element_type=F32) + cnt_ref[...]
    r1 = rsum(jnp.where(oh1, before, 0.0)).astype(jnp.int32)
    r2 = rsum(jnp.where(oh2, before, 0.0)).astype(jnp.int32)
    cnt_ref[...] = cnt_ref[...] + jnp.sum(member.astype(F32), axis=0, keepdims=True)
    ids = jnp.where(lane == 0, e1, jnp.where(lane == 1, e2, jnp.where(lane == 2, r1, jnp.where(lane == 3, r2, 0))))
    ids_ref[...] = jnp.transpose(ids)[0:SUBLANES, :]
    gate_ref[...] = jnp.where(lane == 0, gate1, jnp.where(lane == 1, gate2, 0.0))
    cnt_out_ref[...] = jnp.broadcast_to(cnt_ref[...], cnt_out_ref.shape)


def _router_operands(norm_gain, w_rg, b_rg, w_re, b_re):
    pad = LANES - N_GROUPS - N_EXPERTS
    wr = jnp.pad(jnp.concatenate([w_rg, w_re], axis=1).astype(F32), ((0, 0), (0, pad)))
    br = jnp.pad(jnp.concatenate([b_rg, b_re]).astype(F32), (0, pad))[None]
    r = jnp.arange(TOK_TILE)
    tri = (r[None, :] < r[:, None]).astype(BF16)
    wr_hi = wr.astype(BF16)
    wr_lo = (wr - wr_hi.astype(F32)).astype(BF16)
    return norm_gain[None].astype(F32), jnp.stack([wr_hi, wr_lo]), br, tri


def _route_specs(n_tok):
    full = lambda shape: pl.BlockSpec(shape, lambda i: (0,) * len(shape))
    in_specs = [full((1, D_MODEL)), full((2, D_MODEL, LANES)), full((1, LANES)), full((TOK_TILE, TOK_TILE))]
    tok = lambda w: pl.BlockSpec((TOK_TILE, w), lambda i: (i, 0))
    out_specs = [tok(D_MODEL), pl.BlockSpec((TOK_TILE * ROW_CHUNKS, LANES), lambda i: (i, 0)),
                 pl.BlockSpec((SUBLANES, TOK_TILE), lambda i: (0, i)), tok(LANES), full((SUBLANES, LANES))]
    out_shape = [jax.ShapeDtypeStruct((n_tok, D_MODEL), F32), jax.ShapeDtypeStruct((n_tok * ROW_CHUNKS, LANES), jnp.uint32),
                 jax.ShapeDtypeStruct((SUBLANES, n_tok), jnp.int32), jax.ShapeDtypeStruct((n_tok, LANES), F32),
                 jax.ShapeDtypeStruct((SUBLANES, LANES), F32)]
    return in_specs, out_specs, out_shape


def _gelu_tanh(x):
    return 0.5 * x * (1.0 + jnp.tanh(math.sqrt(2.0 / math.pi) * (x + 0.044715 * (x * x * x))))


def _outproj_even_kernel(x_ref, y_ref, a_ref, wglu_ref, bglu_ref, wout_ref, gain_ref, wr_ref, br_ref, tri_ref,
                         h_ref, hn_ref, ids_ref, gate_ref, cnt_out_ref, cnt_ref):
    @pl.when(pl.program_id(0) == 0)
    def _():
        cnt_ref[...] = jnp.zeros_like(cnt_ref)

    y = _gelu_tanh(jnp.concatenate([y_ref[j] for j in range(SSM_SLABS)], axis=1))
    y = y * jax.nn.sigmoid(jnp.dot(y.astype(BF16), wglu_ref[...], preferred_element_type=F32) + bglu_ref[...])
    mix = jnp.dot(y.astype(BF16), wout_ref[0:SSM_WIDTH, :], preferred_element_type=F32)
    for j in range(ATT_WIDTH // LANES):
        rows = slice(SSM_WIDTH + j * LANES, SSM_WIDTH + (j + 1) * LANES)
        mix = mix + jnp.dot(a_ref[j], wout_ref[rows, :], preferred_element_type=F32)
    _route_epilogue(x_ref[...] + mix, gain_ref, wr_ref, br_ref, tri_ref, cnt_ref,
                    h_ref, hn_ref, ids_ref, gate_ref, cnt_out_ref)


def _outproj_even(x2, y_pre, attn, w_glu, b_glu, w_out, route_ops):
    n_tok = x2.shape[0]
    n_slab = attn.shape[0]
    r_in, r_out, r_shape = _route_specs(n_tok)
    full = lambda shape: pl.BlockSpec(shape, lambda i: (0,) * len(shape))
    return pl.pallas_call(
        _outproj_even_kernel,
        grid=(n_tok // TOK_TILE,),
        in_specs=[pl.BlockSpec((TOK_TILE, D_MODEL), lambda i: (i, 0)),
                  pl.BlockSpec((SSM_SLABS, TOK_TILE, LANES), lambda i: (0, i, 0)),
                  pl.BlockSpec((n_slab, TOK_TILE, LANES), lambda i: (0, i, 0)),
                  full(w_glu.shape), full((1, SSM_WIDTH)), full(w_out.shape)] + r_in,
        out_specs=r_out, out_shape=r_shape,
        scratch_shapes=[pltpu.VMEM((1, LANES), F32)],
        compiler_params=_cparams(("arbitrary",)),
        name="outproj_even",
    )(x2, y_pre, attn, w_glu.astype(BF16), b_glu[None].astype(F32), w_out.astype(BF16), *route_ops)


def _conv_layer_kernel(h_ref, pgate_ref, y0_ref, y1_ref, gmix_ref, win_ref, cw_ref, wout_ref, gain_ref, wr_ref, br_ref,
                       tri_ref, ho_ref, hn_ref, ids_ref, gate_ref, cnt_out_ref, cnt_ref, zc_ref, *, tiles_per_seq):
    i = pl.program_id(0)

    @pl.when(i == 0)
    def _():
        cnt_ref[...] = jnp.zeros_like(cnt_ref)

    @pl.when(i % tiles_per_seq == 0)
    def _():
        zc_ref[0:SUBLANES, :] = jnp.zeros((SUBLANES, D_MODEL), F32)

    pgate = pgate_ref[...]
    h = h_ref[...] + pgate[:, 0:1] * _load_row_tiles(y0_ref) + pgate[:, 1:2] * _load_row_tiles(y1_ref)
    tm = h.shape[0]
    hn = _rms(h, gmix_ref[...]).astype(BF16)
    c = D_MODEL
    b_gate = jnp.dot(hn, win_ref[:, 0:c], preferred_element_type=F32)
    zc = jnp.dot(hn, win_ref[:, c:2 * c], preferred_element_type=F32) * jnp.dot(hn, win_ref[:, 2 * c:3 * c], preferred_element_type=F32)
    zc_ref[SUBLANES:SUBLANES + tm, :] = zc
    conv = cw_ref[CONV_TAPS - 1:CONV_TAPS, :] * zc
    for back in range(1, CONV_TAPS):
        tap = CONV_TAPS - 1 - back
        conv = conv + cw_ref[tap:tap + 1, :] * zc_ref[SUBLANES - back:SUBLANES - back + tm, :]
    zc_ref[0:SUBLANES, :] = zc_ref[tm:tm + SUBLANES, :]
    mix = jnp.dot((b_gate * conv).astype(BF16), wout_ref[...], preferred_element_type=F32)
    _route_epilogue(h + mix, gain_ref, wr_ref, br_ref, tri_ref, cnt_ref,
                    ho_ref, hn_ref, ids_ref, gate_ref, cnt_out_ref)


def _conv_layer(h1, pgate, y2, gain_mix, w_in, conv_w, w_out, route_ops, s_len):
    n_tok = h1.shape[0]
    r_in, r_out, r_shape = _route_specs(n_tok)
    full = lambda shape: pl.BlockSpec(shape, lambda i: (0,) * len(shape))
    return pl.pallas_call(
        functools.partial(_conv_layer_kernel, tiles_per_seq=s_len // TOK_TILE),
        grid=(n_tok // TOK_TILE,),
        in_specs=_moe_specs(n_tok) + [full((1, D_MODEL)), full(w_in.shape), full(conv_w.shape), full(w_out.shape)] + r_in,
        out_specs=r_out, out_shape=r_shape,
        scratch_shapes=[pltpu.VMEM((1, LANES), F32), pltpu.VMEM((TOK_TILE + SUBLANES, D_MODEL), F32)],
        compiler_params=_cparams(("arbitrary",)),
        name="conv_layer",
    )(h1, pgate, y2, y2, gain_mix[None].astype(F32), w_in.astype(BF16), conv_w.astype(F32), w_out.astype(BF16),
      *route_ops)


def _row_map(dest_flat, n_rows):
    n_assign = dest_flat.shape[0]
    n_src = n_rows + 2 * MOE_BLK
    mesh = plsc.VectorSubcoreMesh(core_axis_name="core", subcore_axis_name="subcore",
                                  num_cores=SC_CORES, num_subcores=SC_SUBCORES)

    @functools.partial(
        pl.kernel, mesh=mesh, out_type=jax.ShapeDtypeStruct((n_src,), jnp.int32),
        scratch_types=[pltpu.VMEM((n_assign,), jnp.int32), pltpu.VMEM((n_src,), jnp.int32)],
        compiler_params=pltpu.CompilerParams(needs_layout_passes=False), name="moe_row_map")
    def row_map(dest_hbm, src_hbm, dest_v, src_v):
        @pl.when(jnp.logical_and(lax.axis_index("core") == 0, lax.axis_index("subcore") == 0))
        def _():
            pltpu.sync_copy(dest_hbm, dest_v)
            lanes = lax.iota(jnp.int32, SC_LANES)

            @pl.loop(0, n_src, step=SC_LANES)
            def _(i):
                src_v[pl.ds(i, SC_LANES)] = n_assign + ((i + lanes) & (MOE_BLK - 1))

            @pl.loop(0, n_assign, step=SC_LANES)
            def _(n):
                plsc.store_scatter(src_v, [dest_v[pl.ds(n, SC_LANES)] + MOE_BLK], n + lanes)

            pltpu.sync_copy(src_v, src_hbm)

    return row_map(dest_flat)


def _expert_kernel(blk_e_ref, n_used_ref, src_ref, hn_ref, wg_ref, wu_ref, wd_ref, y2_ref,
                   xbuf, ybuf, wg_s, wu_s, wd_s, ssem, *, n_tok):
    b = pl.program_id(0)
    n_used = n_used_ref[0]
    cur = b % 2
    nxt = 1 - cur

    tile = lambda i: pl.ds(i * ROW_CHUNKS, ROW_CHUNKS)

    def scatter(blk, slot, i):
        row = src_ref[(blk + 1) * MOE_BLK + i]
        return pltpu.make_async_copy(ybuf.at[slot, tile(i)], y2_ref.at[row], ssem.at[slot])

    wait_block = lambda slot: pltpu.make_async_copy(ybuf.at[slot], ybuf.at[slot], ssem.at[slot]).wait()

    @pl.when(b == 0)
    def _():
        ybuf[1] = jnp.zeros(ybuf.shape[1:], ybuf.dtype)

    @pl.when(b < n_used)
    def _():
        new_expert = jnp.logical_or(b == 0, blk_e_ref[b] != blk_e_ref[jnp.maximum(b - 1, 0)])

        @pl.when(new_expert)
        def _():
            wg_s[...] = wg_ref[...].astype(BF16)
            wu_s[...] = wu_ref[...].astype(BF16)
            wd_s[...] = wd_ref[...].astype(BF16)

        @pl.when(b >= 1)
        def _():
            wait_block(cur)

        for i in range(MOE_BLK):
            tok = src_ref[(b + 1) * MOE_BLK + i] & (n_tok - 1)
            xbuf[tile(i), :] = hn_ref[pl.ds(pl.multiple_of(tok * ROW_CHUNKS, ROW_CHUNKS), ROW_CHUNKS), :]
            scatter(b - 1, nxt, i).start(priority=i % 2)
        x = _load_row_tiles(xbuf).astype(BF16)
        g = jnp.dot(x, wg_s[...], preferred_element_type=F32)
        u = jnp.dot(x, wu_s[...], preferred_element_type=F32)
        hb = (g * jax.nn.sigmoid(g) * u).astype(BF16)
        _store_row_tiles(ybuf, jnp.dot(hb, wd_s[...], preferred_element_type=F32), (cur,))

    @pl.when(b == n_used)
    def _():
        wait_block(cur)
        for i in range(MOE_BLK):
            scatter(b - 1, nxt, i).start(priority=i % 2)
        wait_block(nxt)


def _expert_ffn(hn, src, blk_e, n_used, w_g, w_u, w_d, layer):
    n_tok = hn.shape[0] // ROW_CHUNKS
    assert n_tok & (n_tok - 1) == 0, "dump-row aliasing masks the token index with T - 1"
    n_blocks = (src.shape[0] - 2 * MOE_BLK) // MOE_BLK
    last = lambda b, n: jnp.maximum(jnp.minimum(b, n[0] - 1), 0)
    w_spec = lambda rows, cols: pl.BlockSpec((None, None, rows, cols), lambda b, e, n, s: (layer, e[last(b, n)], 0, 0))
    return pl.pallas_call(
        functools.partial(_expert_kernel, n_tok=n_tok),
        grid_spec=pltpu.PrefetchScalarGridSpec(
            num_scalar_prefetch=3,
            grid=(n_blocks + 1,),
            in_specs=[pl.BlockSpec(hn.shape, lambda b, e, n, s: (0, 0), pipeline_mode=pl.Buffered(1)),
                      w_spec(D_MODEL, D_EXPERT), w_spec(D_MODEL, D_EXPERT), w_spec(D_EXPERT, D_MODEL)],
            out_specs=pl.BlockSpec(memory_space=pltpu.HBM),
            scratch_shapes=[pltpu.VMEM((MOE_BLK * ROW_CHUNKS, LANES), jnp.uint32),
                            pltpu.VMEM((2, MOE_BLK * ROW_CHUNKS, LANES), jnp.uint32),
                            pltpu.VMEM((D_MODEL, D_EXPERT), BF16), pltpu.VMEM((D_MODEL, D_EXPERT), BF16),
                            pltpu.VMEM((D_EXPERT, D_MODEL), BF16),
                            pltpu.SemaphoreType.DMA((2,))]),
        out_shape=jax.ShapeDtypeStruct((2 * n_tok + MOE_BLK, ROW_CHUNKS, LANES), jnp.uint32),
        compiler_params=_cparams(("arbitrary",)),
        name="moe_experts",
    )(blk_e, n_used, src, hn, w_g, w_u, w_d)


def _moe_add_kernel(h_ref, gate_ref, y0_ref, y1_ref, o_ref):
    gate = gate_ref[...]
    o_ref[...] = h_ref[...] + gate[:, 0:1] * _load_row_tiles(y0_ref) + gate[:, 1:2] * _load_row_tiles(y1_ref)


def _moe_specs(n_tok):
    slots = n_tok // TOK_TILE
    y2_rows = (TOK_TILE * ROW_CHUNKS, LANES)
    return [pl.BlockSpec((TOK_TILE, D_MODEL), lambda i: (i, 0)), pl.BlockSpec((TOK_TILE, LANES), lambda i: (i, 0)),
            pl.BlockSpec(y2_rows, lambda i: (i, 0)), pl.BlockSpec(y2_rows, lambda i: (slots + i, 0))]


def _moe_add(h, gate, y2):
    n_tok = h.shape[0]
    return pl.pallas_call(
        _moe_add_kernel,
        grid=(n_tok // TOK_TILE,),
        in_specs=_moe_specs(n_tok),
        out_specs=pl.BlockSpec((TOK_TILE, D_MODEL), lambda i: (i, 0)),
        out_shape=jax.ShapeDtypeStruct((n_tok, D_MODEL), F32),
        compiler_params=_cparams(("parallel",)),
        name="moe_add",
    )(h, gate, y2, y2)


def _moe(hn, ids, counts, w_g, w_u, w_d, layer):
    n_tok = hn.shape[0] // ROW_CHUNKS
    n_assign = 2 * n_tok
    n_blocks = n_assign // MOE_BLK + N_EXPERTS
    n_rows = n_blocks * MOE_BLK
    cnt = counts[0, :N_EXPERTS].astype(jnp.int32)
    padded = (cnt + MOE_BLK - 1) // MOE_BLK * MOE_BLK
    pends = jnp.cumsum(padded).astype(jnp.int32)
    pstarts = pends - padded
    first_row = sum(jnp.where(ids[0:2] == e, pstarts[e], 0) for e in range(N_EXPERTS))
    dest = (first_row + ids[2:4]).reshape(n_assign)
    blk_start = jnp.arange(n_blocks + 1, dtype=jnp.int32) * MOE_BLK
    blk_e = jnp.minimum(jnp.sum(pends[None, :] <= blk_start[:, None], axis=1), N_EXPERTS - 1).astype(jnp.int32)
    n_used = (pends[-1:] // MOE_BLK).astype(jnp.int32)
    src = _row_map(dest, n_rows)
    return _expert_ffn(hn, src, blk_e, n_used, w_g, w_u, w_d, layer).reshape(-1, LANES)


def kernel(x, norm_mix, norm_ffn, w_in_even, ssm_a_re, ssm_a_im, ssm_b_re, ssm_b_im, ssm_c_re, ssm_c_im, ssm_d,
           ssm_log_step, w_glu, b_glu, q_norm, k_norm, w_out_even, w_in_conv, conv_w, w_out_conv, w_router_group,
           b_router_group, w_router_expert, b_router_expert, w_expert_gate, w_expert_up, w_expert_down):
    bsz, s_len, d = x.shape
    x2 = x.reshape(bsz * s_len, d)
    route = lambda layer: _router_operands(norm_ffn[layer], w_router_group[layer], b_router_group[layer],
                                           w_router_expert[layer], b_router_expert[layer])
    experts = lambda layer: (w_expert_gate, w_expert_up, w_expert_down, layer)

    u, q, k, v = _inproj_even(x2, norm_mix[0], w_in_even[0], q_norm[0], k_norm[0])
    tables = _s5_tables(ssm_a_re[0], ssm_a_im[0], ssm_b_re[0], ssm_b_im[0], ssm_c_re[0], ssm_c_im[0], ssm_d[0],
                        ssm_log_step[0])
    y_pre = _s5_core(u, tables, bsz, s_len)
    attn = _dilated_attention(q, k, v, bsz, s_len)
    h, hn, ids, gate, counts = _outproj_even(x2, y_pre, attn, w_glu[0], b_glu[0], w_out_even[0], route(0))
    y2 = _moe(hn, ids, counts, *experts(0))

    h, hn, ids, gate, counts = _conv_layer(h, gate, y2, norm_mix[1], w_in_conv[0], conv_w[0], w_out_conv[0],
                                           route(1), s_len)
    y2 = _moe(hn, ids, counts, *experts(1))
    return _moe_add(h, gate, y2).reshape(bsz, s_len, d)
```

```python
import functools
import math

import jax
import jax.numpy as jnp
from jax import lax
from jax.experimental import pallas as pl
from jax.experimental.pallas import tpu as pltpu
from jax.experimental.pallas import tpu_sc as plsc

F32 = jnp.float32
BF16 = jnp.bfloat16

D_MODEL = 1024
SSM_GROUP = 16
SSM_GROUPS = 40
SSM_WIDTH = SSM_GROUP * SSM_GROUPS
SSM_STATE = 64
ATT_HEADS = 6
ATT_HEAD_DIM = 64
ATT_WIDTH = ATT_HEADS * ATT_HEAD_DIM
DILATIONS = (16, 4, 1)
ATT_BLK = 128
CONV_TAPS = 3
N_GROUPS = 4
EXPERTS_PER_GROUP = 8
N_EXPERTS = N_GROUPS * EXPERTS_PER_GROUP
D_EXPERT = 512
MOE_BLK = 256
RMS_EPS = 1e-6
NEG_INF = -1e30

LANES = 128
SUBLANES = 8
SC_CORES, SC_SUBCORES, SC_LANES = 2, 16, 16
VMEM_LIMIT = 56 * 1024 * 1024

TOK_TILE = 512
SSM_CHUNK = 8
SSM_SLAB_GROUPS = LANES // SSM_GROUP
SSM_SLABS = SSM_WIDTH // LANES
S5_ROW_TILE = 256
ATT_SB = 2048
ROW_CHUNKS = D_MODEL // (2 * LANES)


def _cparams(sem):
    return pltpu.CompilerParams(dimension_semantics=sem, vmem_limit_bytes=VMEM_LIMIT)


def _rms(x, gain):
    return x * lax.rsqrt(jnp.mean(x * x, axis=-1, keepdims=True) + RMS_EPS) * gain


HIGH_HALF = 0xFFFF0000


def _load_row_tiles(ref, lead=()):
    n = ref.shape[-2] // ROW_CHUNKS
    bits = jnp.concatenate([ref[lead + (pl.ds(c, n, stride=ROW_CHUNKS), slice(None))] for c in range(ROW_CHUNKS)], axis=1)
    low = lax.bitcast_convert_type(bits << 16, F32)
    high = lax.bitcast_convert_type(bits & jnp.uint32(HIGH_HALF), F32)
    return jnp.concatenate([low, high], axis=1)


def _store_row_tiles(ref, value, lead=()):
    n, half = value.shape[0], value.shape[1] // 2
    bf16_bits = lambda t: lax.bitcast_convert_type(t.astype(BF16).astype(F32), jnp.uint32)
    bits = (bf16_bits(value[:, :half]) >> 16) | (bf16_bits(value[:, half:]) & jnp.uint32(HIGH_HALF))
    for c in range(ROW_CHUNKS):
        ref[lead + (pl.ds(c, n, stride=ROW_CHUNKS), slice(None))] = bits[:, c * LANES:(c + 1) * LANES]


def _head_norm(t, gain, bd):
    tt = t * t
    hi = tt.astype(BF16)
    lo = (tt - hi.astype(F32)).astype(BF16)
    ss = jnp.dot(hi, bd, preferred_element_type=F32) + jnp.dot(lo, bd, preferred_element_type=F32)
    return t * lax.rsqrt(ss * (1.0 / ATT_HEAD_DIM) + RMS_EPS) * gain


def _inproj_even_kernel(x_ref, g_ref, w_ref, bd_ref, qn_ref, kn_ref, u_ref, q_ref, k_ref, v_ref):
    hn = _rms(x_ref[...], g_ref[...]).astype(BF16)
    proj = jnp.dot(hn, w_ref[...], preferred_element_type=F32)
    for j in range(SSM_SLABS):
        u_ref[j] = proj[:, j * LANES:(j + 1) * LANES]
    bd = bd_ref[...]
    o = SSM_WIDTH
    q = _head_norm(proj[:, o:o + ATT_WIDTH], qn_ref[...], bd) * (ATT_HEAD_DIM ** -0.5)
    k = _head_norm(proj[:, o + ATT_WIDTH:o + 2 * ATT_WIDTH], kn_ref[...], bd)
    v = proj[:, o + 2 * ATT_WIDTH:o + 3 * ATT_WIDTH]
    for j in range(ATT_WIDTH // LANES):
        q_ref[j] = q[:, j * LANES:(j + 1) * LANES]
        k_ref[j] = k[:, j * LANES:(j + 1) * LANES]
        v_ref[j] = v[:, j * LANES:(j + 1) * LANES]


def _inproj_even(x2, gain, w_in, q_norm, k_norm):
    n_tok = x2.shape[0]
    n_slab = ATT_WIDTH // LANES
    head_of = jnp.arange(ATT_WIDTH) // ATT_HEAD_DIM
    bd = (head_of[:, None] == head_of[None, :]).astype(BF16)
    qn = jnp.tile(q_norm.astype(F32), ATT_HEADS)[None]
    kn = jnp.tile(k_norm.astype(F32), ATT_HEADS)[None]
    full = lambda shape: pl.BlockSpec(shape, lambda i: (0,) * len(shape))
    slab = pl.BlockSpec((n_slab, TOK_TILE, LANES), lambda i: (0, i, 0))
    slab_shape = jax.ShapeDtypeStruct((n_slab, n_tok, LANES), F32)
    return pl.pallas_call(
        _inproj_even_kernel,
        grid=(n_tok // TOK_TILE,),
        in_specs=[pl.BlockSpec((TOK_TILE, D_MODEL), lambda i: (i, 0)), full((1, D_MODEL)),
                  full(w_in.shape), full(bd.shape), full(qn.shape), full(kn.shape)],
        out_specs=[pl.BlockSpec((SSM_SLABS, TOK_TILE, LANES), lambda i: (0, i, 0)), slab, slab, slab],
        out_shape=[jax.ShapeDtypeStruct((SSM_SLABS, n_tok, LANES), F32), slab_shape, slab_shape, slab_shape],
        compiler_params=_cparams(("parallel",)),
        name="inproj_even",
    )(x2, gain[None].astype(F32), w_in.astype(BF16), bd, qn, kn)


def _s5_tables(a_re, a_im, b_re, b_im, c_re, c_im, d_skip, log_step):
    f = lambda t: t.astype(F32)
    a_re, a_im, b_re, b_im, c_re, c_im = map(f, (a_re, a_im, b_re, b_im, c_re, c_im))
    L = SSM_CHUNK
    step = jnp.exp(f(log_step))[:, None]
    ks = jnp.arange(L + 1, dtype=F32)[:, None, None]
    mag = jnp.exp(ks * (a_re * step)[None])
    ang = ks * (a_im * step)[None]
    pw_re, pw_im = mag * jnp.cos(ang), mag * jnp.sin(ang)
    nr, ni = pw_re[1] - 1.0, pw_im[1]
    den = a_re * a_re + a_im * a_im
    z_re, z_im = (nr * a_re + ni * a_im) / den, (ni * a_re - nr * a_im) / den
    bb_re = z_re[..., None] * b_re - z_im[..., None] * b_im
    bb_im = z_re[..., None] * b_im + z_im[..., None] * b_re
    lb_re = pw_re[..., None] * bb_re[None] - pw_im[..., None] * bb_im[None]
    lb_im = pw_re[..., None] * bb_im[None] + pw_im[..., None] * bb_re[None]
    kk = jnp.einsum('gop,kgpi->gkio', c_re, lb_re[:L]) - jnp.einsum('gop,kgpi->gkio', c_im, lb_im[:L])
    ti = jnp.arange(L)
    lag = ti[None, :] - ti[:, None]
    m = jnp.where((lag >= 0)[None, :, :, None, None], kk[:, jnp.maximum(lag, 0)], 0.0)
    ns, gs = SSM_SLABS, SSM_SLAB_GROUPS
    lw = L * LANES
    pm = m.reshape(ns, gs, L, L, SSM_GROUP, SSM_GROUP).transpose(0, 2, 1, 4, 3, 5).reshape(ns, lw, L * SSM_GROUP)
    fold_e = lambda t: t[:L][::-1].reshape(L, ns, gs, SSM_STATE, SSM_GROUP).transpose(1, 0, 2, 4, 3).reshape(ns, lw, SSM_STATE)
    pe = jnp.concatenate([fold_e(lb_re), fold_e(lb_im)], axis=-1)
    pw1_re, pw1_im = (t[1:].transpose(1, 0, 2)[:, :, None, :] for t in (pw_re, pw_im))
    cl_re = c_re[:, None] * pw1_re - c_im[:, None] * pw1_im
    cl_im = c_re[:, None] * pw1_im + c_im[:, None] * pw1_re
    fold_f = lambda t: t.reshape(ns, gs, L, SSM_GROUP, SSM_STATE).transpose(0, 4, 2, 1, 3).reshape(ns, SSM_STATE, lw)
    pf = jnp.concatenate([fold_f(cl_re), fold_f(-cl_im)], axis=1)
    per_chain = lambda t: jnp.tile(t.reshape(ns, gs * SSM_STATE // LANES, LANES), (1, 2, 1))
    d_vec = jnp.tile(f(d_skip).reshape(ns, 1, LANES), (1, 1, L))
    return pm.astype(BF16), pe.astype(BF16), pf.astype(BF16), per_chain(pw_re[L]), per_chain(pw_im[L]), d_vec


def _iota2(shape):
    return lax.broadcasted_iota(jnp.int32, shape, 0), lax.broadcasted_iota(jnp.int32, shape, 1)


def _widen(compact, group_major_cols, shape, r_shift, c_shift, sel_rows):
    gmask = SSM_SLAB_GROUPS - 1
    if sel_rows:
        r, c = _iota2((shape[0], compact.shape[0]))
        sel = ((r >> 9) == (c >> 6)) & ((r & (SSM_STATE - 1)) == (c & (SSM_STATE - 1)))
        wide = jnp.dot(sel.astype(BF16), compact, preferred_element_type=F32)
    else:
        r, c = _iota2((compact.shape[1], shape[1]))
        if group_major_cols:
            sel = ((c >> 9) == (r >> 6)) & ((c & (SSM_STATE - 1)) == (r & (SSM_STATE - 1)))
        else:
            sel = ((c >> 7) == (r >> 4)) & ((c & (SSM_GROUP - 1)) == (r & (SSM_GROUP - 1)))
        wide = jnp.dot(compact, sel.astype(BF16), preferred_element_type=F32)
    r, c = _iota2(shape)
    keep = ((r >> r_shift) & gmask) == ((c >> c_shift) & gmask)
    return jnp.where(keep, wide, 0.0).astype(BF16)


def _s5_kernel(u_ref, pm_ref, pe_ref, pf_ref, ar_ref, ai_ref, d_ref, y_ref, m_ref, e_ref, f_ref, x_ref, sr_ref, si_ref,
               *, n_chunk, pitch):
    L = SSM_CHUNK
    n_blk = SSM_SLAB_GROUPS * SSM_STATE // LANES
    n_re = n_blk * LANES
    lw = L * LANES
    m_ref[...] = _widen(pm_ref[...], False, (lw, lw), 4, 4, False)
    e_ref[...] = _widen(pe_ref[...], True, (lw, 2 * n_re), 4, 6, False)
    f_ref[...] = _widen(pf_ref[...], False, (2 * n_re, lw), 6, 4, True)
    tiles = [(b, c0) for b in range(2) for c0 in range(0, n_chunk, S5_ROW_TILE)]
    for b, c0 in tiles:
        r0 = b * n_chunk + c0
        for t in range(L):
            x_ref[r0:r0 + S5_ROW_TILE, t * LANES:(t + 1) * LANES] = (
                u_ref[pl.ds(r0 * L + t, S5_ROW_TILE, stride=L), :].astype(BF16))
        sl = jnp.dot(x_ref[r0:r0 + S5_ROW_TILE, :], e_ref[...], preferred_element_type=F32)
        for j in range(n_blk):
            base = (b * n_blk + j) * pitch + c0
            sr_ref[base:base + S5_ROW_TILE, :] = sl[:, j * LANES:(j + 1) * LANES]
            si_ref[base:base + S5_ROW_TILE, :] = sl[:, n_re + j * LANES:n_re + (j + 1) * LANES]
    ar, ai = ar_ref[...], ai_ref[...]
    half = LANES

    def scan_step(c, carry):
        s_re, s_im = carry
        rows = pl.ds(c, SUBLANES, stride=pitch)
        x_re, x_im = sr_ref[rows, :], si_ref[rows, :]
        sr_ref[rows, :] = s_re
        si_ref[rows, :] = s_im
        return ar * s_re - ai * s_im + x_re, ar * s_im + ai * s_re + x_im

    zero = jnp.zeros((SUBLANES, half), F32)
    lax.fori_loop(0, n_chunk, scan_step, (zero, zero), unroll=8)

    for b, c0 in tiles:
        r0 = b * n_chunk + c0
        chain = lambda ref, j: ref[(b * n_blk + j) * pitch + c0:(b * n_blk + j) * pitch + c0 + S5_ROW_TILE, :]
        sp = jnp.concatenate([chain(sr_ref, j) for j in range(n_blk)] + [chain(si_ref, j) for j in range(n_blk)],
                             axis=1).astype(BF16)
        xt = x_ref[r0:r0 + S5_ROW_TILE, :]
        y = (jnp.dot(xt, m_ref[...], preferred_element_type=F32)
             + jnp.dot(sp, f_ref[...], preferred_element_type=F32)
             + d_ref[...] * xt.astype(F32))
        for t in range(L):
            y_ref[pl.ds(r0 * L + t, S5_ROW_TILE, stride=L), :] = y[:, t * LANES:(t + 1) * LANES]


def _s5_core(u, tables, bsz, s_len):
    assert bsz == 2, "the scan packs (batch, lane block) into the 8 sublanes of one vreg"
    pm, pe, pf, a_r, a_i, d_vec = tables
    n_tok = bsz * s_len
    n_chunk = s_len // SSM_CHUNK
    pitch = n_chunk + SUBLANES
    lw = SSM_CHUNK * LANES
    n_state = 2 * SSM_SLAB_GROUPS * SSM_STATE
    slab = lambda shape, **kw: pl.BlockSpec((None,) + shape, lambda i: (i,) + (0,) * len(shape), **kw)
    once = dict(pipeline_mode=pl.Buffered(1))
    return pl.pallas_call(
        functools.partial(_s5_kernel, n_chunk=n_chunk, pitch=pitch),
        grid=(SSM_SLABS,),
        in_specs=[slab((n_tok, LANES), **once), slab(pm.shape[1:]), slab(pe.shape[1:]), slab(pf.shape[1:]),
                  slab((SUBLANES, LANES)), slab((SUBLANES, LANES)), slab((1, lw))],
        out_specs=slab((n_tok, LANES), **once),
        out_shape=jax.ShapeDtypeStruct((SSM_SLABS, n_tok, LANES), F32),
        scratch_shapes=[pltpu.VMEM((lw, lw), BF16), pltpu.VMEM((lw, n_state), BF16), pltpu.VMEM((n_state, lw), BF16),
                        pltpu.VMEM((bsz * n_chunk, lw), BF16),
                        pltpu.VMEM((SUBLANES * pitch, LANES), F32),
                        pltpu.VMEM((SUBLANES * pitch, LANES), F32)],
        compiler_params=_cparams(("parallel",)),
        name="s5_core",
    )(u, pm, pe, pf, a_r, a_i, d_vec)


def _attn_kernel(slope_ref, q_ref, kp_ref, kc_ref, vp_ref, vc_ref, o_ref, kk_ref, vv_ref, m_ref, l_ref, acc_ref):
    slab = pl.program_id(1)
    sb = pl.program_id(2)
    kk_ref[0:ATT_SB, :] = kp_ref[...]
    kk_ref[ATT_SB:2 * ATT_SB, :] = kc_ref[...]
    vv_ref[0:ATT_SB, :] = vp_ref[...]
    vv_ref[ATT_SB:2 * ATT_SB, :] = vc_ref[...]

    lane = lax.broadcasted_iota(jnp.int32, (ATT_BLK, LANES), 1)
    head0 = lane < ATT_HEAD_DIM
    qi = lax.broadcasted_iota(jnp.int32, (ATT_BLK, 2 * ATT_BLK), 0)
    kj = lax.broadcasted_iota(jnp.int32, (ATT_BLK, 2 * ATT_BLK), 1)
    back = qi + ATT_BLK - kj
    band = (back >= 0) & (back <= ATT_BLK)
    neg_steps = -back.astype(F32)
    slopes = (slope_ref[2 * slab], slope_ref[2 * slab + 1])

    for pat, dil in enumerate(DILATIONS):
        span = ATT_BLK * dil

        def tile(idx, carry, dil=dil, span=span, pat=pat):
            start = (idx // dil) * span + idx % dil
            seq_ok = jnp.logical_or(sb > 0, idx >= dil)
            valid = band & ((kj >= ATT_BLK) | seq_ok)
            if dil == 1:
                start = pl.multiple_of(start, ATT_BLK)
                rows_of = lambda first, n: pl.ds(first, n)
            else:
                rows_of = lambda first, n: pl.ds(first, n, stride=dil)
            rows = rows_of(start, ATT_BLK)
            qt = q_ref[rows, :]
            kt = kk_ref[rows_of(ATT_SB + start - span, 2 * ATT_BLK), :].astype(BF16)
            vt = vv_ref[rows_of(ATT_SB + start - span, 2 * ATT_BLK), :].astype(BF16)
            q0 = jnp.where(head0, qt, 0.0)
            parts = []
            for hh, qh in enumerate((q0, qt - q0)):
                s = lax.dot_general(qh.astype(BF16), kt, (((1,), (1,)), ((), ())), preferred_element_type=F32)
                s = jnp.where(valid, s + (slopes[hh] * float(dil)) * neg_steps, NEG_INF)
                m = jnp.max(s, axis=-1, keepdims=True)
                p = jnp.exp(s - m)
                l = jnp.sum(p, axis=-1, keepdims=True)
                o = jnp.dot(p.astype(BF16), vt, preferred_element_type=F32)
                parts.append((m, l, o))
            (m0, l0, o0), (m1, l1, o1) = parts
            m_t = jnp.where(head0, m0, m1)
            l_t = jnp.where(head0, l0, l1)
            o_t = jnp.where(head0, o0, o1)
            if pat == 0:
                m_ref[rows, :] = m_t
                l_ref[rows, :] = l_t
                acc_ref[rows, :] = o_t
            else:
                m_old = m_ref[rows, :]
                m_new = jnp.maximum(m_old, m_t)
                a = jnp.exp(m_old - m_new)
                b = jnp.exp(m_t - m_new)
                m_ref[rows, :] = m_new
                l_ref[rows, :] = a * l_ref[rows, :] + b * l_t
                acc_ref[rows, :] = a * acc_ref[rows, :] + b * o_t
            return carry

        lax.fori_loop(0, ATT_SB // ATT_BLK, tile, 0, unroll=4)

    o_ref[...] = (acc_ref[...] / l_ref[...]).astype(o_ref.dtype)


def _dilated_attention(q, k, v, bsz, s_len):
    n_slab = q.shape[0]
    shape4 = (n_slab, bsz, s_len, LANES)
    q, k, v = (t.reshape(shape4) for t in (q, k, v))
    slopes = jnp.asarray([2.0 ** (-8.0 * (h + 1) / ATT_HEADS) for h in range(ATT_HEADS)], F32)
    blk = (None, None, ATT_SB, LANES)
    cur = pl.BlockSpec(blk, lambda b, j, i, s: (j, b, i, 0))
    prev = pl.BlockSpec(blk, lambda b, j, i, s: (j, b, jnp.maximum(i - 1, 0), 0))
    out = pl.pallas_call(
        _attn_kernel,
        grid_spec=pltpu.PrefetchScalarGridSpec(
            num_scalar_prefetch=1,
            grid=(bsz, n_slab, s_len // ATT_SB),
            in_specs=[cur, prev, cur, prev, cur],
            out_specs=cur,
            scratch_shapes=[pltpu.VMEM((2 * ATT_SB, LANES), F32), pltpu.VMEM((2 * ATT_SB, LANES), F32),
                            pltpu.VMEM((ATT_SB, LANES), F32), pltpu.VMEM((ATT_SB, LANES), F32),
                            pltpu.VMEM((ATT_SB, LANES), F32)]),
        out_shape=jax.ShapeDtypeStruct(shape4, BF16),
        compiler_params=_cparams(("parallel", "parallel", "parallel")),
        name="dilated_attn",
    )(slopes, q, k, k, v, v)
    return out.reshape(n_slab, bsz * s_len, LANES)


def _route_epilogue(h, gain_ref, wr_ref, br_ref, tri_ref, cnt_ref, h_ref, hn_ref, ids_ref, gate_ref, cnt_out_ref):
    tm = h.shape[0]
    h_ref[...] = h
    hn = _rms(h, gain_ref[...])
    _store_row_tiles(hn_ref, hn)
    hn_hi = hn.astype(BF16)
    hn_lo = (hn - hn_hi.astype(F32)).astype(BF16)
    logits = (jnp.dot(hn_hi, wr_ref[0], preferred_element_type=F32)
              + (jnp.dot(hn_hi, wr_ref[1], preferred_element_type=F32)
                 + jnp.dot(hn_lo, wr_ref[0], preferred_element_type=F32))) + br_ref[...]
    lane = lax.broadcasted_iota(jnp.int32, (tm, LANES), 1)
    big = jnp.int32(LANES)
    rmax = lambda t: jnp.max(t, axis=-1, keepdims=True)
    rmin = lambda t: jnp.min(t, axis=-1, keepdims=True)
    rsum = lambda t: jnp.sum(t, axis=-1, keepdims=True)
    gmask = lane < N_GROUPS
    gl = jnp.where(gmask, logits, -jnp.inf)
    gmax = rmax(gl)
    ge = jnp.where(gmask, jnp.exp(gl - gmax), 0.0)
    gprob = ge / rsum(ge)
    g_w = rmax(gprob)
    grp = rmin(jnp.where(gmask & (gprob == g_w), lane, big))
    group_of_lane = (lane - N_GROUPS) >> int(math.log2(EXPERTS_PER_GROUP))
    emask = (lane >= N_GROUPS) & (lane < N_GROUPS + N_EXPERTS) & (group_of_lane == grp)
    el = jnp.where(emask, logits, -jnp.inf)
    ee = jnp.where(emask, jnp.exp(el - rmax(el)), 0.0)
    ep = jnp.where(emask, ee / rsum(ee), -1.0)
    p1 = rmax(ep)
    i1 = rmin(jnp.where(ep == p1, lane, big))
    ep2 = jnp.where(lane == i1, -1.0, ep)
    p2 = rmax(ep2)
    i2 = rmin(jnp.where(ep2 == p2, lane, big))
    e1, e2 = i1 - N_GROUPS, i2 - N_GROUPS
    psum = p1 + p2
    gate1, gate2 = g_w * p1 / psum, g_w * p2 / psum
    oh1, oh2 = lane == e1, lane == e2
    member = (oh1 | oh2).astype(BF16)
    before = jnp.dot(tri_ref[...], member, preferred_element_type=F32) + cnt_ref[...]
    r1 = rsum(jnp.where(oh1, before, 0.0)).astype(jnp.int32)
    r2 = rsum(jnp.where(oh2, before, 0.0)).astype(jnp.int32)
    cnt_ref[...] = cnt_ref[...] + jnp.sum(member.astype(F32), axis=0, keepdims=True)
    ids = jnp.where(lane == 0, e1, jnp.where(lane == 1, e2, jnp.where(lane == 2, r1, jnp.where(lane == 3, r2, 0))))
    ids_ref[...] = jnp.transpose(ids)[0:SUBLANES, :]
    gate_ref[...] = jnp.where(lane == 0, gate1, jnp.where(lane == 1, gate2, 0.0))
    cnt_out_ref[...] = jnp.broadcast_to(cnt_ref[...], cnt_out_ref.shape)


def _router_operands(norm_gain, w_rg, b_rg, w_re, b_re):
    pad = LANES - N_GROUPS - N_EXPERTS
    wr = jnp.pad(jnp.concatenate([w_rg, w_re], axis=1).astype(F32), ((0, 0), (0, pad)))
    br = jnp.pad(jnp.concatenate([b_rg, b_re]).astype(F32), (0, pad))[None]
    r = jnp.arange(TOK_TILE)
    tri = (r[None, :] < r[:, None]).astype(BF16)
    wr_hi = wr.astype(BF16)
    wr_lo = (wr - wr_hi.astype(F32)).astype(BF16)
    return norm_gain[None].astype(F32), jnp.stack([wr_hi, wr_lo]), br, tri


def _route_specs(n_tok):
    full = lambda shape: pl.BlockSpec(shape, lambda i: (0,) * len(shape))
    in_specs = [full((1, D_MODEL)), full((2, D_MODEL, LANES)), full((1, LANES)), full((TOK_TILE, TOK_TILE))]
    tok = lambda w: pl.BlockSpec((TOK_TILE, w), lambda i: (i, 0))
    out_specs = [tok(D_MODEL), pl.BlockSpec((TOK_TILE * ROW_CHUNKS, LANES), lambda i: (i, 0)),
                 pl.BlockSpec((SUBLANES, TOK_TILE), lambda i: (0, i)), tok(LANES), full((SUBLANES, LANES))]
    out_shape = [jax.ShapeDtypeStruct((n_tok, D_MODEL), F32), jax.ShapeDtypeStruct((n_tok * ROW_CHUNKS, LANES), jnp.uint32),
                 jax.ShapeDtypeStruct((SUBLANES, n_tok), jnp.int32), jax.ShapeDtypeStruct((n_tok, LANES), F32),
                 jax.ShapeDtypeStruct((SUBLANES, LANES), F32)]
    return in_specs, out_specs, out_shape


def _gelu_tanh(x):
    return 0.5 * x * (1.0 + jnp.tanh(math.sqrt(2.0 / math.pi) * (x + 0.044715 * (x * x * x))))


def _outproj_even_kernel(x_ref, y_ref, a_ref, wglu_ref, bglu_ref, wout_ref, gain_ref, wr_ref, br_ref, tri_ref,
                         h_ref, hn_ref, ids_ref, gate_ref, cnt_out_ref, cnt_ref):
    @pl.when(pl.program_id(0) == 0)
    def _():
        cnt_ref[...] = jnp.zeros_like(cnt_ref)

    y = _gelu_tanh(jnp.concatenate([y_ref[j] for j in range(SSM_SLABS)], axis=1))
    y = y * jax.nn.sigmoid(jnp.dot(y.astype(BF16), wglu_ref[...], preferred_element_type=F32) + bglu_ref[...])
    mix = jnp.dot(y.astype(BF16), wout_ref[0:SSM_WIDTH, :], preferred_element_type=F32)
    for j in range(ATT_WIDTH // LANES):
        rows = slice(SSM_WIDTH + j * LANES, SSM_WIDTH + (j + 1) * LANES)
        mix = mix + jnp.dot(a_ref[j], wout_ref[rows, :], preferred_element_type=F32)
    _route_epilogue(x_ref[...] + mix, gain_ref, wr_ref, br_ref, tri_ref, cnt_ref,
                    h_ref, hn_ref, ids_ref, gate_ref, cnt_out_ref)


def _outproj_even(x2, y_pre, attn, w_glu, b_glu, w_out, route_ops):
    n_tok = x2.shape[0]
    n_slab = attn.shape[0]
    r_in, r_out, r_shape = _route_specs(n_tok)
    full = lambda shape: pl.BlockSpec(shape, lambda i: (0,) * len(shape))
    return pl.pallas_call(
        _outproj_even_kernel,
        grid=(n_tok // TOK_TILE,),
        in_specs=[pl.BlockSpec((TOK_TILE, D_MODEL), lambda i: (i, 0)),
                  pl.BlockSpec((SSM_SLABS, TOK_TILE, LANES), lambda i: (0, i, 0)),
                  pl.BlockSpec((n_slab, TOK_TILE, LANES), lambda i: (0, i, 0)),
                  full(w_glu.shape), full((1, SSM_WIDTH)), full(w_out.shape)] + r_in,
        out_specs=r_out, out_shape=r_shape,
        scratch_shapes=[pltpu.VMEM((1, LANES), F32)],
        compiler_params=_cparams(("arbitrary",)),
        name="outproj_even",
    )(x2, y_pre, attn, w_glu.astype(BF16), b_glu[None].astype(F32), w_out.astype(BF16), *route_ops)


def _conv_layer_kernel(h_ref, pgate_ref, y0_ref, y1_ref, gmix_ref, win_ref, cw_ref, wout_ref, gain_ref, wr_ref, br_ref,
                       tri_ref, ho_ref, hn_ref, ids_ref, gate_ref, cnt_out_ref, cnt_ref, zc_ref, *, tiles_per_seq):
    i = pl.program_id(0)

    @pl.when(i == 0)
    def _():
        cnt_ref[...] = jnp.zeros_like(cnt_ref)

    @pl.when(i % tiles_per_seq == 0)
    def _():
        zc_ref[0:SUBLANES, :] = jnp.zeros((SUBLANES, D_MODEL), F32)

    pgate = pgate_ref[...]
    h = h_ref[...] + pgate[:, 0:1] * _load_row_tiles(y0_ref) + pgate[:, 1:2] * _load_row_tiles(y1_ref)
    tm = h.shape[0]
    hn = _rms(h, gmix_ref[...]).astype(BF16)
    c = D_MODEL
    b_gate = jnp.dot(hn, win_ref[:, 0:c], preferred_element_type=F32)
    zc = jnp.dot(hn, win_ref[:, c:2 * c], preferred_element_type=F32) * jnp.dot(hn, win_ref[:, 2 * c:3 * c], preferred_element_type=F32)
    zc_ref[SUBLANES:SUBLANES + tm, :] = zc
    conv = cw_ref[CONV_TAPS - 1:CONV_TAPS, :] * zc
    for back in range(1, CONV_TAPS):
        tap = CONV_TAPS - 1 - back
        conv = conv + cw_ref[tap:tap + 1, :] * zc_ref[SUBLANES - back:SUBLANES - back + tm, :]
    zc_ref[0:SUBLANES, :] = zc_ref[tm:tm + SUBLANES, :]
    mix = jnp.dot((b_gate * conv).astype(BF16), wout_ref[...], preferred_element_type=F32)
    _route_epilogue(h + mix, gain_ref, wr_ref, br_ref, tri_ref, cnt_ref,
                    ho_ref, hn_ref, ids_ref, gate_ref, cnt_out_ref)


def _conv_layer(h1, pgate, y2, gain_mix, w_in, conv_w, w_out, route_ops, s_len):
    n_tok = h1.shape[0]
    r_in, r_out, r_shape = _route_specs(n_tok)
    full = lambda shape: pl.BlockSpec(shape, lambda i: (0,) * len(shape))
    return pl.pallas_call(
        functools.partial(_conv_layer_kernel, tiles_per_seq=s_len // TOK_TILE),
        grid=(n_tok // TOK_TILE,),
        in_specs=_moe_specs(n_tok) + [full((1, D_MODEL)), full(w_in.shape), full(conv_w.shape), full(w_out.shape)] + r_in,
        out_specs=r_out, out_shape=r_shape,
        scratch_shapes=[pltpu.VMEM((1, LANES), F32), pltpu.VMEM((TOK_TILE + SUBLANES, D_MODEL), F32)],
        compiler_params=_cparams(("arbitrary",)),
        name="conv_layer",
    )(h1, pgate, y2, y2, gain_mix[None].astype(F32), w_in.astype(BF16), conv_w.astype(F32), w_out.astype(BF16),
      *route_ops)


def _row_map(experts, ranks, first_row, n_rows):
    n_slot, n_tok = len(experts), experts[0].shape[0]
    n_assign = n_slot * n_tok
    n_src = n_rows + 2 * MOE_BLK
    mesh = plsc.VectorSubcoreMesh(core_axis_name="core", subcore_axis_name="subcore",
                                  num_cores=SC_CORES, num_subcores=SC_SUBCORES)
    tok_vec = pltpu.VMEM((n_tok,), jnp.int32)

    @functools.partial(
        pl.kernel, mesh=mesh, out_type=jax.ShapeDtypeStruct((n_src,), jnp.int32),
        scratch_types=[tok_vec] * (2 * n_slot) + [pltpu.VMEM((N_EXPERTS,), jnp.int32), pltpu.VMEM((n_src,), jnp.int32)],
        compiler_params=pltpu.CompilerParams(needs_layout_passes=False), name="moe_row_map")
    def row_map(*refs):
        ins, src_hbm, scratch = refs[:2 * n_slot + 1], refs[2 * n_slot + 1], refs[2 * n_slot + 2:]
        first_v, src_v = scratch[2 * n_slot], scratch[2 * n_slot + 1]

        @pl.when(jnp.logical_and(lax.axis_index("core") == 0, lax.axis_index("subcore") == 0))
        def _():
            for hbm, vmem in zip(ins, scratch):
                pltpu.sync_copy(hbm, vmem)
            lanes = lax.iota(jnp.int32, SC_LANES)

            @pl.loop(0, n_src, step=SC_LANES)
            def _(i):
                src_v[pl.ds(i, SC_LANES)] = n_assign + ((i + lanes) & (MOE_BLK - 1))

            for slot in range(n_slot):
                e_v, r_v = scratch[slot], scratch[n_slot + slot]

                @pl.loop(0, n_tok, step=SC_LANES)
                def _(t):
                    row = plsc.load_gather(first_v, [e_v[pl.ds(t, SC_LANES)]]) + r_v[pl.ds(t, SC_LANES)]
                    plsc.store_scatter(src_v, [row + MOE_BLK], slot * n_tok + t + lanes)

            pltpu.sync_copy(src_v, src_hbm)

    return row_map(*experts, *ranks, first_row)


def _expert_kernel(blk_e_ref, n_used_ref, src_ref, next_e_ref, wslot_ref, hn_ref, wg_ref, wu_ref, wd_ref, y2_ref,
                   xbuf, ybuf, wg_f, wu_f, wd_f, wg_s, wu_s, wd_s, ssem, wsem, *, n_tok, layer):
    b = pl.program_id(0)
    n_used = n_used_ref[0]
    cur = b % 2
    nxt = 1 - cur

    tile = lambda i: pl.ds(i * ROW_CHUNKS, ROW_CHUNKS)

    def scatter(blk, slot, i):
        row = src_ref[(blk + 1) * MOE_BLK + i]
        return pltpu.make_async_copy(ybuf.at[slot, tile(i)], y2_ref.at[row], ssem.at[slot])

    wait_block = lambda slot: pltpu.make_async_copy(ybuf.at[slot], ybuf.at[slot], ssem.at[slot]).wait()

    def weights(expert, slot, act):
        for hbm, buf in ((wg_ref, wg_f), (wu_ref, wu_f), (wd_ref, wd_f)):
            act(pltpu.make_async_copy(hbm.at[layer, expert], buf.at[slot], wsem.at[slot]))

    @pl.when(b == 0)
    def _():
        ybuf[1] = jnp.zeros(ybuf.shape[1:], ybuf.dtype)
        weights(blk_e_ref[0], wslot_ref[0], lambda c: c.start())

    @pl.when(b < n_used)
    def _():
        new_expert = jnp.logical_or(b == 0, blk_e_ref[b] != blk_e_ref[jnp.maximum(b - 1, 0)])

        @pl.when(new_expert)
        def _():
            slot = wslot_ref[b]
            weights(blk_e_ref[b], slot, lambda c: c.wait())
            wg_s[...] = wg_f[slot].astype(BF16)
            wu_s[...] = wu_f[slot].astype(BF16)
            wd_s[...] = wd_f[slot].astype(BF16)

            @pl.when(next_e_ref[b] >= 0)
            def _():
                weights(next_e_ref[b], 1 - slot, lambda c: c.start())

        @pl.when(b >= 1)
        def _():
            wait_block(cur)

        for i in range(MOE_BLK):
            tok = src_ref[(b + 1) * MOE_BLK + i] & (n_tok - 1)
            xbuf[tile(i), :] = hn_ref[pl.ds(pl.multiple_of(tok * ROW_CHUNKS, ROW_CHUNKS), ROW_CHUNKS), :]
            scatter(b - 1, nxt, i).start(priority=i % 2)
        x = _load_row_tiles(xbuf).astype(BF16)
        g = jnp.dot(x, wg_s[...], preferred_element_type=F32)
        u = jnp.dot(x, wu_s[...], preferred_element_type=F32)
        hb = (g * jax.nn.sigmoid(g) * u).astype(BF16)
        _store_row_tiles(ybuf, jnp.dot(hb, wd_s[...], preferred_element_type=F32), (cur,))

    @pl.when(b == n_used)
    def _():
        wait_block(cur)
        for i in range(MOE_BLK):
            scatter(b - 1, nxt, i).start(priority=i % 2)
        wait_block(nxt)


def _expert_ffn(hn, src, blk_e, n_used, next_e, wslot, w_g, w_u, w_d, layer):
    n_tok = hn.shape[0] // ROW_CHUNKS
    assert n_tok & (n_tok - 1) == 0, "dump-row aliasing masks the token index with T - 1"
    n_blocks = (src.shape[0] - 2 * MOE_BLK) // MOE_BLK
    in_hbm = pl.BlockSpec(memory_space=pltpu.HBM)
    return pl.pallas_call(
        functools.partial(_expert_kernel, n_tok=n_tok, layer=layer),
        grid_spec=pltpu.PrefetchScalarGridSpec(
            num_scalar_prefetch=5,
            grid=(n_blocks + 1,),
            in_specs=[pl.BlockSpec(hn.shape, lambda b, *_: (0, 0), pipeline_mode=pl.Buffered(1)),
                      in_hbm, in_hbm, in_hbm],
            out_specs=pl.BlockSpec(memory_space=pltpu.HBM),
            scratch_shapes=[pltpu.VMEM((MOE_BLK * ROW_CHUNKS, LANES), jnp.uint32),
                            pltpu.VMEM((2, MOE_BLK * ROW_CHUNKS, LANES), jnp.uint32),
                            pltpu.VMEM((2, D_MODEL, D_EXPERT), F32), pltpu.VMEM((2, D_MODEL, D_EXPERT), F32),
                            pltpu.VMEM((2, D_EXPERT, D_MODEL), F32),
                            pltpu.VMEM((D_MODEL, D_EXPERT), BF16), pltpu.VMEM((D_MODEL, D_EXPERT), BF16),
                            pltpu.VMEM((D_EXPERT, D_MODEL), BF16),
                            pltpu.SemaphoreType.DMA((2,)), pltpu.SemaphoreType.DMA((2,))]),
        out_shape=jax.ShapeDtypeStruct((2 * n_tok + MOE_BLK, ROW_CHUNKS, LANES), jnp.uint32),
        compiler_params=_cparams(("arbitrary",)),
        name="moe_experts",
    )(blk_e, n_used, src, next_e, wslot, hn, w_g, w_u, w_d)


def _moe_add_kernel(h_ref, gate_ref, y0_ref, y1_ref, o_ref):
    gate = gate_ref[...]
    o_ref[...] = h_ref[...] + gate[:, 0:1] * _load_row_tiles(y0_ref) + gate[:, 1:2] * _load_row_tiles(y1_ref)


def _moe_specs(n_tok):
    slots = n_tok // TOK_TILE
    y2_rows = (TOK_TILE * ROW_CHUNKS, LANES)
    return [pl.BlockSpec((TOK_TILE, D_MODEL), lambda i: (i, 0)), pl.BlockSpec((TOK_TILE, LANES), lambda i: (i, 0)),
            pl.BlockSpec(y2_rows, lambda i: (i, 0)), pl.BlockSpec(y2_rows, lambda i: (slots + i, 0))]


def _moe_add(h, gate, y2):
    n_tok = h.shape[0]
    return pl.pallas_call(
        _moe_add_kernel,
        grid=(n_tok // TOK_TILE,),
        in_specs=_moe_specs(n_tok),
        out_specs=pl.BlockSpec((TOK_TILE, D_MODEL), lambda i: (i, 0)),
        out_shape=jax.ShapeDtypeStruct((n_tok, D_MODEL), F32),
        compiler_params=_cparams(("parallel",)),
        name="moe_add",
    )(h, gate, y2, y2)


def _moe(hn, ids, counts, w_g, w_u, w_d, layer):
    n_tok = hn.shape[0] // ROW_CHUNKS
    n_assign = 2 * n_tok
    n_blocks = n_assign // MOE_BLK + N_EXPERTS
    n_rows = n_blocks * MOE_BLK
    cnt = counts[0, :N_EXPERTS].astype(jnp.int32)
    padded = (cnt + MOE_BLK - 1) // MOE_BLK * MOE_BLK
    pends = jnp.cumsum(padded).astype(jnp.int32)
    pstarts = pends - padded
    blk_start = jnp.arange(n_blocks + 1, dtype=jnp.int32) * MOE_BLK
    blk_e = jnp.minimum(jnp.sum(pends[None, :] <= blk_start[:, None], axis=1), N_EXPERTS - 1).astype(jnp.int32)
    n_used = (pends[-1:] // MOE_BLK).astype(jnp.int32)
    used = padded > 0
    e_idx = jnp.arange(N_EXPERTS, dtype=jnp.int32)
    later = jnp.where(used[None, :] & (e_idx[None, :] > e_idx[:, None]), e_idx[None, :], N_EXPERTS)
    next_used = jnp.min(later, axis=1)
    next_used = jnp.where(next_used < N_EXPERTS, next_used, -1).astype(jnp.int32)
    ordinal = (jnp.cumsum(used.astype(jnp.int32)) - 1).astype(jnp.int32)
    src = _row_map((ids[0], ids[1]), (ids[2], ids[3]), pstarts, n_rows)
    y2 = _expert_ffn(hn, src, blk_e, n_used, next_used[blk_e], ordinal[blk_e] % 2, w_g, w_u, w_d, layer)
    return y2.reshape(-1, LANES)


def kernel(x, norm_mix, norm_ffn, w_in_even, ssm_a_re, ssm_a_im, ssm_b_re, ssm_b_im, ssm_c_re, ssm_c_im, ssm_d,
           ssm_log_step, w_glu, b_glu, q_norm, k_norm, w_out_even, w_in_conv, conv_w, w_out_conv, w_router_group,
           b_router_group, w_router_expert, b_router_expert, w_expert_gate, w_expert_up, w_expert_down):
    bsz, s_len, d = x.shape
    x2 = x.reshape(bsz * s_len, d)
    route = lambda layer: _router_operands(norm_ffn[layer], w_router_group[layer], b_router_group[layer],
                                           w_router_expert[layer], b_router_expert[layer])
    experts = lambda layer: (w_expert_gate, w_expert_up, w_expert_down, layer)

    u, q, k, v = _inproj_even(x2, norm_mix[0], w_in_even[0], q_norm[0], k_norm[0])
    tables = _s5_tables(ssm_a_re[0], ssm_a_im[0], ssm_b_re[0], ssm_b_im[0], ssm_c_re[0], ssm_c_im[0], ssm_d[0],
                        ssm_log_step[0])
    y_pre = _s5_core(u, tables, bsz, s_len)
    attn = _dilated_attention(q, k, v, bsz, s_len)
    h, hn, ids, gate, counts = _outproj_even(x2, y_pre, attn, w_glu[0], b_glu[0], w_out_even[0], route(0))
    y2 = _moe(hn, ids, counts, *experts(0))

    h, hn, ids, gate, counts = _conv_layer(h, gate, y2, norm_mix[1], w_in_conv[0], conv_w[0], w_out_conv[0],
                                           route(1), s_len)
    y2 = _moe(hn, ids, counts, *experts(1))
    return _moe_add(h, gate, y2).reshape(bsz, s_len, d)
```

```python
import functools
import math

import jax
import jax.numpy as jnp
from jax import lax
from jax.experimental import pallas as pl
from jax.experimental.pallas import tpu as pltpu
from jax.experimental.pallas import tpu_sc as plsc

F32 = jnp.float32
BF16 = jnp.bfloat16

D_MODEL = 1024
SSM_GROUP = 16
SSM_GROUPS = 40
SSM_WIDTH = SSM_GROUP * SSM_GROUPS
SSM_STATE = 64
ATT_HEADS = 6
ATT_HEAD_DIM = 64
ATT_WIDTH = ATT_HEADS * ATT_HEAD_DIM
DILATIONS = (16, 4, 1)
ATT_BLK = 128
CONV_TAPS = 3
N_GROUPS = 4
EXPERTS_PER_GROUP = 8
N_EXPERTS = N_GROUPS * EXPERTS_PER_GROUP
D_EXPERT = 512
MOE_BLK = 256
RMS_EPS = 1e-6
NEG_INF = -1e30

LANES = 128
SUBLANES = 8
SC_CORES, SC_SUBCORES, SC_LANES = 2, 16, 16
VMEM_LIMIT = 56 * 1024 * 1024

TOK_TILE = 512
SSM_CHUNK = 8
SSM_SLAB_GROUPS = LANES // SSM_GROUP
SSM_SLABS = SSM_WIDTH // LANES
S5_ROW_TILE = 256
ATT_SB = 2048
ROW_CHUNKS = D_MODEL // (2 * LANES)


def _cparams(sem):
    return pltpu.CompilerParams(dimension_semantics=sem, vmem_limit_bytes=VMEM_LIMIT)


def _rms(x, gain):
    return x * lax.rsqrt(jnp.mean(x * x, axis=-1, keepdims=True) + RMS_EPS) * gain


HIGH_HALF = 0xFFFF0000


def _load_row_tiles(ref, lead=()):
    n = ref.shape[-2] // ROW_CHUNKS
    bits = jnp.concatenate([ref[lead + (pl.ds(c, n, stride=ROW_CHUNKS), slice(None))] for c in range(ROW_CHUNKS)], axis=1)
    low = lax.bitcast_convert_type(bits << 16, F32)
    high = lax.bitcast_convert_type(bits & jnp.uint32(HIGH_HALF), F32)
    return jnp.concatenate([low, high], axis=1)


def _store_row_tiles(ref, value, lead=()):
    n, half = value.shape[0], value.shape[1] // 2
    bf16_bits = lambda t: lax.bitcast_convert_type(t.astype(BF16).astype(F32), jnp.uint32)
    bits = (bf16_bits(value[:, :half]) >> 16) | (bf16_bits(value[:, half:]) & jnp.uint32(HIGH_HALF))
    for c in range(ROW_CHUNKS):
        ref[lead + (pl.ds(c, n, stride=ROW_CHUNKS), slice(None))] = bits[:, c * LANES:(c + 1) * LANES]


def _head_norm(t, gain, bd):
    tt = t * t
    hi = tt.astype(BF16)
    lo = (tt - hi.astype(F32)).astype(BF16)
    ss = jnp.dot(hi, bd, preferred_element_type=F32) + jnp.dot(lo, bd, preferred_element_type=F32)
    return t * lax.rsqrt(ss * (1.0 / ATT_HEAD_DIM) + RMS_EPS) * gain


def _inproj_even_kernel(x_ref, g_ref, w_ref, bd_ref, qn_ref, kn_ref, u_ref, q_ref, k_ref, v_ref):
    hn = _rms(x_ref[...], g_ref[...]).astype(BF16)
    proj = jnp.dot(hn, w_ref[...], preferred_element_type=F32)
    for j in range(SSM_SLABS):
        u_ref[j] = proj[:, j * LANES:(j + 1) * LANES]
    bd = bd_ref[...]
    o = SSM_WIDTH
    q = _head_norm(proj[:, o:o + ATT_WIDTH], qn_ref[...], bd) * (ATT_HEAD_DIM ** -0.5)
    k = _head_norm(proj[:, o + ATT_WIDTH:o + 2 * ATT_WIDTH], kn_ref[...], bd)
    v = proj[:, o + 2 * ATT_WIDTH:o + 3 * ATT_WIDTH]
    for j in range(ATT_WIDTH // LANES):
        q_ref[j] = q[:, j * LANES:(j + 1) * LANES]
        k_ref[j] = k[:, j * LANES:(j + 1) * LANES]
        v_ref[j] = v[:, j * LANES:(j + 1) * LANES]


def _inproj_even(x2, gain, w_in, q_norm, k_norm):
    n_tok = x2.shape[0]
    n_slab = ATT_WIDTH // LANES
    head_of = jnp.arange(ATT_WIDTH) // ATT_HEAD_DIM
    bd = (head_of[:, None] == head_of[None, :]).astype(BF16)
    qn = jnp.tile(q_norm.astype(F32), ATT_HEADS)[None]
    kn = jnp.tile(k_norm.astype(F32), ATT_HEADS)[None]
    full = lambda shape: pl.BlockSpec(shape, lambda i: (0,) * len(shape))
    slab = pl.BlockSpec((n_slab, TOK_TILE, LANES), lambda i: (0, i, 0))
    slab_shape = jax.ShapeDtypeStruct((n_slab, n_tok, LANES), F32)
    return pl.pallas_call(
        _inproj_even_kernel,
        grid=(n_tok // TOK_TILE,),
        in_specs=[pl.BlockSpec((TOK_TILE, D_MODEL), lambda i: (i, 0)), full((1, D_MODEL)),
                  full(w_in.shape), full(bd.shape), full(qn.shape), full(kn.shape)],
        out_specs=[pl.BlockSpec((SSM_SLABS, TOK_TILE, LANES), lambda i: (0, i, 0)), slab, slab, slab],
        out_shape=[jax.ShapeDtypeStruct((SSM_SLABS, n_tok, LANES), F32), slab_shape, slab_shape, slab_shape],
        compiler_params=_cparams(("parallel",)),
        name="inproj_even",
    )(x2, gain[None].astype(F32), w_in.astype(BF16), bd, qn, kn)


def _s5_tables(a_re, a_im, b_re, b_im, c_re, c_im, d_skip, log_step):
    f = lambda t: t.astype(F32)
    a_re, a_im, b_re, b_im, c_re, c_im = map(f, (a_re, a_im, b_re, b_im, c_re, c_im))
    L = SSM_CHUNK
    step = jnp.exp(f(log_step))[:, None]
    ks = jnp.arange(L + 1, dtype=F32)[:, None, None]
    mag = jnp.exp(ks * (a_re * step)[None])
    ang = ks * (a_im * step)[None]
    pw_re, pw_im = mag * jnp.cos(ang), mag * jnp.sin(ang)
    nr, ni = pw_re[1] - 1.0, pw_im[1]
    den = a_re * a_re + a_im * a_im
    z_re, z_im = (nr * a_re + ni * a_im) / den, (ni * a_re - nr * a_im) / den
    bb_re = z_re[..., None] * b_re - z_im[..., None] * b_im
    bb_im = z_re[..., None] * b_im + z_im[..., None] * b_re
    lb_re = pw_re[..., None] * bb_re[None] - pw_im[..., None] * bb_im[None]
    lb_im = pw_re[..., None] * bb_im[None] + pw_im[..., None] * bb_re[None]
    kk = jnp.einsum('gop,kgpi->gkio', c_re, lb_re[:L]) - jnp.einsum('gop,kgpi->gkio', c_im, lb_im[:L])
    ti = jnp.arange(L)
    lag = ti[None, :] - ti[:, None]
    m = jnp.where((lag >= 0)[None, :, :, None, None], kk[:, jnp.maximum(lag, 0)], 0.0)
    ns, gs = SSM_SLABS, SSM_SLAB_GROUPS
    lw = L * LANES
    pm = m.reshape(ns, gs, L, L, SSM_GROUP, SSM_GROUP).transpose(0, 2, 1, 4, 3, 5).reshape(ns, lw, L * SSM_GROUP)
    fold_e = lambda t: t[:L][::-1].reshape(L, ns, gs, SSM_STATE, SSM_GROUP).transpose(1, 0, 2, 4, 3).reshape(ns, lw, SSM_STATE)
    pe = jnp.concatenate([fold_e(lb_re), fold_e(lb_im)], axis=-1)
    pw1_re, pw1_im = (t[1:].transpose(1, 0, 2)[:, :, None, :] for t in (pw_re, pw_im))
    cl_re = c_re[:, None] * pw1_re - c_im[:, None] * pw1_im
    cl_im = c_re[:, None] * pw1_im + c_im[:, None] * pw1_re
    fold_f = lambda t: t.reshape(ns, gs, L, SSM_GROUP, SSM_STATE).transpose(0, 4, 2, 1, 3).reshape(ns, SSM_STATE, lw)
    pf = jnp.concatenate([fold_f(cl_re), fold_f(-cl_im)], axis=1)
    per_chain = lambda t: jnp.tile(t.reshape(ns, gs * SSM_STATE // LANES, LANES), (1, 2, 1))
    d_vec = jnp.tile(f(d_skip).reshape(ns, 1, LANES), (1, 1, L))
    return pm.astype(BF16), pe.astype(BF16), pf.astype(BF16), per_chain(pw_re[L]), per_chain(pw_im[L]), d_vec


def _iota2(shape):
    return lax.broadcasted_iota(jnp.int32, shape, 0), lax.broadcasted_iota(jnp.int32, shape, 1)


def _widen(compact, group_major_cols, shape, r_shift, c_shift, sel_rows):
    gmask = SSM_SLAB_GROUPS - 1
    if sel_rows:
        r, c = _iota2((shape[0], compact.shape[0]))
        sel = ((r >> 9) == (c >> 6)) & ((r & (SSM_STATE - 1)) == (c & (SSM_STATE - 1)))
        wide = jnp.dot(sel.astype(BF16), compact, preferred_element_type=F32)
    else:
        r, c = _iota2((compact.shape[1], shape[1]))
        if group_major_cols:
            sel = ((c >> 9) == (r >> 6)) & ((c & (SSM_STATE - 1)) == (r & (SSM_STATE - 1)))
        else:
            sel = ((c >> 7) == (r >> 4)) & ((c & (SSM_GROUP - 1)) == (r & (SSM_GROUP - 1)))
        wide = jnp.dot(compact, sel.astype(BF16), preferred_element_type=F32)
    r, c = _iota2(shape)
    keep = ((r >> r_shift) & gmask) == ((c >> c_shift) & gmask)
    return jnp.where(keep, wide, 0.0).astype(BF16)


def _s5_kernel(u_ref, pm_ref, pe_ref, pf_ref, ar_ref, ai_ref, d_ref, y_ref, m_ref, e_ref, f_ref, x_ref, sr_ref, si_ref,
               *, n_chunk, pitch):
    L = SSM_CHUNK
    n_blk = SSM_SLAB_GROUPS * SSM_STATE // LANES
    n_re = n_blk * LANES
    lw = L * LANES
    m_ref[...] = _widen(pm_ref[...], False, (lw, lw), 4, 4, False)
    e_ref[...] = _widen(pe_ref[...], True, (lw, 2 * n_re), 4, 6, False)
    f_ref[...] = _widen(pf_ref[...], False, (2 * n_re, lw), 6, 4, True)
    tiles = [(b, c0) for b in range(2) for c0 in range(0, n_chunk, S5_ROW_TILE)]
    for b, c0 in tiles:
        r0 = b * n_chunk + c0
        for t in range(L):
            x_ref[r0:r0 + S5_ROW_TILE, t * LANES:(t + 1) * LANES] = (
                u_ref[pl.ds(r0 * L + t, S5_ROW_TILE, stride=L), :].astype(BF16))
        sl = jnp.dot(x_ref[r0:r0 + S5_ROW_TILE, :], e_ref[...], preferred_element_type=F32)
        for j in range(n_blk):
            base = (b * n_blk + j) * pitch + c0
            sr_ref[base:base + S5_ROW_TILE, :] = sl[:, j * LANES:(j + 1) * LANES]
            si_ref[base:base + S5_ROW_TILE, :] = sl[:, n_re + j * LANES:n_re + (j + 1) * LANES]
    ar, ai = ar_ref[...], ai_ref[...]
    half = LANES

    def scan_step(c, carry):
        s_re, s_im = carry
        rows = pl.ds(c, SUBLANES, stride=pitch)
        x_re, x_im = sr_ref[rows, :], si_ref[rows, :]
        sr_ref[rows, :] = s_re
        si_ref[rows, :] = s_im
        return ar * s_re - ai * s_im + x_re, ar * s_im + ai * s_re + x_im

    zero = jnp.zeros((SUBLANES, half), F32)
    lax.fori_loop(0, n_chunk, scan_step, (zero, zero), unroll=8)

    for b, c0 in tiles:
        r0 = b * n_chunk + c0
        chain = lambda ref, j: ref[(b * n_blk + j) * pitch + c0:(b * n_blk + j) * pitch + c0 + S5_ROW_TILE, :]
        sp = jnp.concatenate([chain(sr_ref, j) for j in range(n_blk)] + [chain(si_ref, j) for j in range(n_blk)],
                             axis=1).astype(BF16)
        xt = x_ref[r0:r0 + S5_ROW_TILE, :]
        y = (jnp.dot(xt, m_ref[...], preferred_element_type=F32)
             + jnp.dot(sp, f_ref[...], preferred_element_type=F32)
             + d_ref[...] * xt.astype(F32))
        for t in range(L):
            y_ref[pl.ds(r0 * L + t, S5_ROW_TILE, stride=L), :] = y[:, t * LANES:(t + 1) * LANES]


def _s5_core(u, tables, bsz, s_len):
    assert bsz == 2, "the scan packs (batch, lane block) into the 8 sublanes of one vreg"
    pm, pe, pf, a_r, a_i, d_vec = tables
    n_tok = bsz * s_len
    n_chunk = s_len // SSM_CHUNK
    pitch = n_chunk + SUBLANES
    lw = SSM_CHUNK * LANES
    n_state = 2 * SSM_SLAB_GROUPS * SSM_STATE
    slab = lambda shape, **kw: pl.BlockSpec((None,) + shape, lambda i: (i,) + (0,) * len(shape), **kw)
    once = dict(pipeline_mode=pl.Buffered(1))
    return pl.pallas_call(
        functools.partial(_s5_kernel, n_chunk=n_chunk, pitch=pitch),
        grid=(SSM_SLABS,),
        in_specs=[slab((n_tok, LANES), **once), slab(pm.shape[1:]), slab(pe.shape[1:]), slab(pf.shape[1:]),
                  slab((SUBLANES, LANES)), slab((SUBLANES, LANES)), slab((1, lw))],
        out_specs=slab((n_tok, LANES), **once),
        out_shape=jax.ShapeDtypeStruct((SSM_SLABS, n_tok, LANES), F32),
        scratch_shapes=[pltpu.VMEM((lw, lw), BF16), pltpu.VMEM((lw, n_state), BF16), pltpu.VMEM((n_state, lw), BF16),
                        pltpu.VMEM((bsz * n_chunk, lw), BF16),
                        pltpu.VMEM((SUBLANES * pitch, LANES), F32),
                        pltpu.VMEM((SUBLANES * pitch, LANES), F32)],
        compiler_params=_cparams(("parallel",)),
        name="s5_core",
    )(u, pm, pe, pf, a_r, a_i, d_vec)


def _attn_kernel(slope_ref, q_ref, kp_ref, kc_ref, vp_ref, vc_ref, o_ref,
                 q4_ref, k4_ref, v4_ref, k1_ref, v1_ref, m_ref, l_ref, acc_ref):
    slab = pl.program_id(1)
    sb = pl.program_id(2)
    fine = DILATIONS[1]
    nq, nk = ATT_SB // fine, 2 * ATT_SB // fine
    for r in range(fine):
        q4_ref[r * nq:(r + 1) * nq, :] = q_ref[pl.ds(r, nq, stride=fine), :]
        for dst, prev, cur in ((k4_ref, kp_ref, kc_ref), (v4_ref, vp_ref, vc_ref)):
            dst[r * nk:r * nk + nq, :] = prev[pl.ds(r, nq, stride=fine), :]
            dst[r * nk + nq:(r + 1) * nk, :] = cur[pl.ds(r, nq, stride=fine), :]
    for dst, prev, cur in ((k1_ref, kp_ref, kc_ref), (v1_ref, vp_ref, vc_ref)):
        dst[0:ATT_BLK, :] = prev[ATT_SB - ATT_BLK:ATT_SB, :]
        dst[ATT_BLK:ATT_BLK + ATT_SB, :] = cur[...]

    lane = lax.broadcasted_iota(jnp.int32, (ATT_BLK, LANES), 1)
    head0 = lane < ATT_HEAD_DIM
    qi = lax.broadcasted_iota(jnp.int32, (ATT_BLK, 2 * ATT_BLK), 0)
    kj = lax.broadcasted_iota(jnp.int32, (ATT_BLK, 2 * ATT_BLK), 1)
    back = qi + ATT_BLK - kj
    band = (back >= 0) & (back <= ATT_BLK)
    neg_steps = -back.astype(F32)
    slopes = (slope_ref[2 * slab], slope_ref[2 * slab + 1])

    for pat, dil in enumerate(DILATIONS):
        span = ATT_BLK * dil

        def tile(idx, carry, dil=dil, span=span, pat=pat):
            start = (idx // dil) * span + idx % dil
            seq_ok = jnp.logical_or(sb > 0, idx >= dil)
            valid = band & ((kj >= ATT_BLK) | seq_ok)
            if dil == 1:
                start = pl.multiple_of(start, ATT_BLK)
                rows = pl.ds(start, ATT_BLK)
                qt, k_src, v_src, k_rows = q_ref[rows, :], k1_ref, v1_ref, pl.ds(start, 2 * ATT_BLK)
            elif dil == fine:
                first = pl.multiple_of((idx // fine) * ATT_BLK, ATT_BLK)
                rows = pl.ds(start, ATT_BLK, stride=dil)
                qt = q4_ref[pl.ds((idx % fine) * nq + first, ATT_BLK), :]
                k_src, v_src = k4_ref, v4_ref
                k_rows = pl.ds((idx % fine) * nk + (nq - ATT_BLK) + first, 2 * ATT_BLK)
            else:
                rows = pl.ds(start, ATT_BLK, stride=dil)
                qt = q4_ref[pl.ds((idx % fine) * nq + idx // fine, ATT_BLK, stride=fine), :]
                k_src, v_src = k4_ref, v4_ref
                k_rows = pl.ds((idx % fine) * nk + idx // fine, 2 * ATT_BLK, stride=fine)
            kt = k_src[k_rows, :].astype(BF16)
            vt = v_src[k_rows, :].astype(BF16)
            q0 = jnp.where(head0, qt, 0.0)
            parts = []
            for hh, qh in enumerate((q0, qt - q0)):
                s = lax.dot_general(qh.astype(BF16), kt, (((1,), (1,)), ((), ())), preferred_element_type=F32)
                s = jnp.where(valid, s + (slopes[hh] * float(dil)) * neg_steps, NEG_INF)
                m = jnp.max(s, axis=-1, keepdims=True)
                p = jnp.exp(s - m)
                l = jnp.sum(p, axis=-1, keepdims=True)
                o = jnp.dot(p.astype(BF16), vt, preferred_element_type=F32)
                parts.append((m, l, o))
            (m0, l0, o0), (m1, l1, o1) = parts
            m_t = jnp.where(head0, m0, m1)
            l_t = jnp.where(head0, l0, l1)
            o_t = jnp.where(head0, o0, o1)
            if pat == 0:
                m_ref[rows, :] = m_t
                l_ref[rows, :] = l_t
                acc_ref[rows, :] = o_t
            else:
                m_old = m_ref[rows, :]
                m_new = jnp.maximum(m_old, m_t)
                a = jnp.exp(m_old - m_new)
                b = jnp.exp(m_t - m_new)
                m_ref[rows, :] = m_new
                l_ref[rows, :] = a * l_ref[rows, :] + b * l_t
                acc_ref[rows, :] = a * acc_ref[rows, :] + b * o_t
            return carry

        lax.fori_loop(0, ATT_SB // ATT_BLK, tile, 0, unroll=4)

    o_ref[...] = (acc_ref[...] / l_ref[...]).astype(o_ref.dtype)


def _dilated_attention(q, k, v, bsz, s_len):
    assert DILATIONS == (DILATIONS[1] ** 2, DILATIONS[1], 1) and ATT_SB == ATT_BLK * DILATIONS[0]
    n_slab = q.shape[0]
    shape4 = (n_slab, bsz, s_len, LANES)
    q, k, v = (t.reshape(shape4) for t in (q, k, v))
    slopes = jnp.asarray([2.0 ** (-8.0 * (h + 1) / ATT_HEADS) for h in range(ATT_HEADS)], F32)
    blk = (None, None, ATT_SB, LANES)
    cur = pl.BlockSpec(blk, lambda b, j, i, s: (j, b, i, 0))
    prev = pl.BlockSpec(blk, lambda b, j, i, s: (j, b, jnp.maximum(i - 1, 0), 0))
    out = pl.pallas_call(
        _attn_kernel,
        grid_spec=pltpu.PrefetchScalarGridSpec(
            num_scalar_prefetch=1,
            grid=(bsz, n_slab, s_len // ATT_SB),
            in_specs=[cur, prev, cur, prev, cur],
            out_specs=cur,
            scratch_shapes=[pltpu.VMEM((ATT_SB, LANES), F32),
                            pltpu.VMEM((2 * ATT_SB, LANES), F32), pltpu.VMEM((2 * ATT_SB, LANES), F32),
                            pltpu.VMEM((ATT_BLK + ATT_SB, LANES), F32), pltpu.VMEM((ATT_BLK + ATT_SB, LANES), F32),
                            pltpu.VMEM((ATT_SB, LANES), F32), pltpu.VMEM((ATT_SB, LANES), F32),
                            pltpu.VMEM((ATT_SB, LANES), F32)]),
        out_shape=jax.ShapeDtypeStruct(shape4, BF16),
        compiler_params=_cparams(("parallel", "parallel", "parallel")),
        name="dilated_attn",
    )(slopes, q, k, k, v, v)
    return out.reshape(n_slab, bsz * s_len, LANES)


def _route_epilogue(h, gain_ref, wr_ref, br_ref, tri_ref, cnt_ref, h_ref, hn_ref, ids_ref, gate_ref, cnt_out_ref):
    tm = h.shape[0]
    h_ref[...] = h
    hn = _rms(h, gain_ref[...])
    _store_row_tiles(hn_ref, hn)
    hn_hi = hn.astype(BF16)
    hn_lo = (hn - hn_hi.astype(F32)).astype(BF16)
    logits = (jnp.dot(hn_hi, wr_ref[0], preferred_element_type=F32)
              + (jnp.dot(hn_hi, wr_ref[1], preferred_element_type=F32)
                 + jnp.dot(hn_lo, wr_ref[0], preferred_element_type=F32))) + br_ref[...]
    lane = lax.broadcasted_iota(jnp.int32, (tm, LANES), 1)
    big = jnp.int32(LANES)
    rmax = lambda t: jnp.max(t, axis=-1, keepdims=True)
    rmin = lambda t: jnp.min(t, axis=-1, keepdims=True)
    rsum = lambda t: jnp.sum(t, axis=-1, keepdims=True)
    gmask = lane < N_GROUPS
    gl = jnp.where(gmask, logits, -jnp.inf)
    gmax = rmax(gl)
    ge = jnp.where(gmask, jnp.exp(gl - gmax), 0.0)
    gprob = ge / rsum(ge)
    g_w = rmax(gprob)
    grp = rmin(jnp.where(gmask & (gprob == g_w), lane, big))
    group_of_lane = (lane - N_GROUPS) >> int(math.log2(EXPERTS_PER_GROUP))
    emask = (lane >= N_GROUPS) & (lane < N_GROUPS + N_EXPERTS) & (group_of_lane == grp)
    el = jnp.where(emask, logits, -jnp.inf)
    ee = jnp.where(emask, jnp.exp(el - rmax(el)), 0.0)
    ep = jnp.where(emask, ee / rsum(ee), -1.0)
    p1 = rmax(ep)
    i1 = rmin(jnp.where(ep == p1, lane, big))
    ep2 = jnp.where(lane == i1, -1.0, ep)
    p2 = rmax(ep2)
    i2 = rmin(jnp.where(ep2 == p2, lane, big))
    e1, e2 = i1 - N_GROUPS, i2 - N_GROUPS
    psum = p1 + p2
    gate1, gate2 = g_w * p1 / psum, g_w * p2 / psum
    oh1, oh2 = lane == e1, lane == e2
    member = (oh1 | oh2).astype(BF16)
    before = jnp.dot(tri_ref[...], member, preferred_element_type=F32) + cnt_ref[...]
    r1 = rsum(jnp.where(oh1, before, 0.0)).astype(jnp.int32)
    r2 = rsum(jnp.where(oh2, before, 0.0)).astype(jnp.int32)
    cnt_ref[...] = cnt_ref[...] + jnp.sum(member.astype(F32), axis=0, keepdims=True)
    ids = jnp.where(lane == 0, e1, jnp.where(lane == 1, e2, jnp.where(lane == 2, r1, jnp.where(lane == 3, r2, 0))))
    ids_ref[...] = jnp.transpose(ids)[0:SUBLANES, :]
    gate_ref[...] = jnp.where(lane == 0, gate1, jnp.where(lane == 1, gate2, 0.0))
    cnt_out_ref[...] = jnp.broadcast_to(cnt_ref[...], cnt_out_ref.shape)


def _router_operands(norm_gain, w_rg, b_rg, w_re, b_re):
    pad = LANES - N_GROUPS - N_EXPERTS
    wr = jnp.pad(jnp.concatenate([w_rg, w_re], axis=1).astype(F32), ((0, 0), (0, pad)))
    br = jnp.pad(jnp.concatenate([b_rg, b_re]).astype(F32), (0, pad))[None]
    r = jnp.arange(TOK_TILE)
    tri = (r[None, :] < r[:, None]).astype(BF16)
    wr_hi = wr.astype(BF16)
    wr_lo = (wr - wr_hi.astype(F32)).astype(BF16)
    return norm_gain[None].astype(F32), jnp.stack([wr_hi, wr_lo]), br, tri


def _route_specs(n_tok):
    full = lambda shape: pl.BlockSpec(shape, lambda i: (0,) * len(shape))
    in_specs = [full((1, D_MODEL)), full((2, D_MODEL, LANES)), full((1, LANES)), full((TOK_TILE, TOK_TILE))]
    tok = lambda w: pl.BlockSpec((TOK_TILE, w), lambda i: (i, 0))
    out_specs = [tok(D_MODEL), pl.BlockSpec((TOK_TILE * ROW_CHUNKS, LANES), lambda i: (i, 0)),
                 pl.BlockSpec((SUBLANES, TOK_TILE), lambda i: (0, i)), tok(LANES), full((SUBLANES, LANES))]
    out_shape = [jax.ShapeDtypeStruct((n_tok, D_MODEL), F32), jax.ShapeDtypeStruct((n_tok * ROW_CHUNKS, LANES), jnp.uint32),
                 jax.ShapeDtypeStruct((SUBLANES, n_tok), jnp.int32), jax.ShapeDtypeStruct((n_tok, LANES), F32),
                 jax.ShapeDtypeStruct((SUBLANES, LANES), F32)]
    return in_specs, out_specs, out_shape


def _gelu_tanh(x):
    return 0.5 * x * (1.0 + jnp.tanh(math.sqrt(2.0 / math.pi) * (x + 0.044715 * (x * x * x))))


def _outproj_even_kernel(x_ref, y_ref, a_ref, wglu_ref, bglu_ref, wout_ref, gain_ref, wr_ref, br_ref, tri_ref,
                         h_ref, hn_ref, ids_ref, gate_ref, cnt_out_ref, cnt_ref):
    @pl.when(pl.program_id(0) == 0)
    def _():
        cnt_ref[...] = jnp.zeros_like(cnt_ref)

    y = _gelu_tanh(jnp.concatenate([y_ref[j] for j in range(SSM_SLABS)], axis=1))
    y = y * jax.nn.sigmoid(jnp.dot(y.astype(BF16), wglu_ref[...], preferred_element_type=F32) + bglu_ref[...])
    mix = jnp.dot(y.astype(BF16), wout_ref[0:SSM_WIDTH, :], preferred_element_type=F32)
    for j in range(ATT_WIDTH // LANES):
        rows = slice(SSM_WIDTH + j * LANES, SSM_WIDTH + (j + 1) * LANES)
        mix = mix + jnp.dot(a_ref[j], wout_ref[rows, :], preferred_element_type=F32)
    _route_epilogue(x_ref[...] + mix, gain_ref, wr_ref, br_ref, tri_ref, cnt_ref,
                    h_ref, hn_ref, ids_ref, gate_ref, cnt_out_ref)


def _outproj_even(x2, y_pre, attn, w_glu, b_glu, w_out, route_ops):
    n_tok = x2.shape[0]
    n_slab = attn.shape[0]
    r_in, r_out, r_shape = _route_specs(n_tok)
    full = lambda shape: pl.BlockSpec(shape, lambda i: (0,) * len(shape))
    return pl.pallas_call(
        _outproj_even_kernel,
        grid=(n_tok // TOK_TILE,),
        in_specs=[pl.BlockSpec((TOK_TILE, D_MODEL), lambda i: (i, 0)),
                  pl.BlockSpec((SSM_SLABS, TOK_TILE, LANES), lambda i: (0, i, 0)),
                  pl.BlockSpec((n_slab, TOK_TILE, LANES), lambda i: (0, i, 0)),
                  full(w_glu.shape), full((1, SSM_WIDTH)), full(w_out.shape)] + r_in,
        out_specs=r_out, out_shape=r_shape,
        scratch_shapes=[pltpu.VMEM((1, LANES), F32)],
        compiler_params=_cparams(("arbitrary",)),
        name="outproj_even",
    )(x2, y_pre, attn, w_glu.astype(BF16), b_glu[None].astype(F32), w_out.astype(BF16), *route_ops)


def _conv_layer_kernel(h_ref, pgate_ref, y0_ref, y1_ref, gmix_ref, win_ref, cw_ref, wout_ref, gain_ref, wr_ref, br_ref,
                       tri_ref, ho_ref, hn_ref, ids_ref, gate_ref, cnt_out_ref, cnt_ref, zc_ref, *, tiles_per_seq):
    i = pl.program_id(0)

    @pl.when(i == 0)
    def _():
        cnt_ref[...] = jnp.zeros_like(cnt_ref)

    @pl.when(i % tiles_per_seq == 0)
    def _():
        zc_ref[0:SUBLANES, :] = jnp.zeros((SUBLANES, D_MODEL), F32)

    pgate = pgate_ref[...]
    h = h_ref[...] + pgate[:, 0:1] * _load_row_tiles(y0_ref) + pgate[:, 1:2] * _load_row_tiles(y1_ref)
    tm = h.shape[0]
    hn = _rms(h, gmix_ref[...]).astype(BF16)
    c = D_MODEL
    b_gate = jnp.dot(hn, win_ref[:, 0:c], preferred_element_type=F32)
    zc = jnp.dot(hn, win_ref[:, c:2 * c], preferred_element_type=F32) * jnp.dot(hn, win_ref[:, 2 * c:3 * c], preferred_element_type=F32)
    zc_ref[SUBLANES:SUBLANES + tm, :] = zc
    conv = cw_ref[CONV_TAPS - 1:CONV_TAPS, :] * zc
    for back in range(1, CONV_TAPS):
        tap = CONV_TAPS - 1 - back
        conv = conv + cw_ref[tap:tap + 1, :] * zc_ref[SUBLANES - back:SUBLANES - back + tm, :]
    zc_ref[0:SUBLANES, :] = zc_ref[tm:tm + SUBLANES, :]
    mix = jnp.dot((b_gate * conv).astype(BF16), wout_ref[...], preferred_element_type=F32)
    _route_epilogue(h + mix, gain_ref, wr_ref, br_ref, tri_ref, cnt_ref,
                    ho_ref, hn_ref, ids_ref, gate_ref, cnt_out_ref)


def _conv_layer(h1, pgate, y2, gain_mix, w_in, conv_w, w_out, route_ops, s_len):
    n_tok = h1.shape[0]
    r_in, r_out, r_shape = _route_specs(n_tok)
    full = lambda shape: pl.BlockSpec(shape, lambda i: (0,) * len(shape))
    return pl.pallas_call(
        functools.partial(_conv_layer_kernel, tiles_per_seq=s_len // TOK_TILE),
        grid=(n_tok // TOK_TILE,),
        in_specs=_moe_specs(n_tok) + [full((1, D_MODEL)), full(w_in.shape), full(conv_w.shape), full(w_out.shape)] + r_in,
        out_specs=r_out, out_shape=r_shape,
        scratch_shapes=[pltpu.VMEM((1, LANES), F32), pltpu.VMEM((TOK_TILE + SUBLANES, D_MODEL), F32)],
        compiler_params=_cparams(("arbitrary",)),
        name="conv_layer",
    )(h1, pgate, y2, y2, gain_mix[None].astype(F32), w_in.astype(BF16), conv_w.astype(F32), w_out.astype(BF16),
      *route_ops)


def _row_map(experts, ranks, first_row, n_rows):
    n_slot, n_tok = len(experts), experts[0].shape[0]
    n_assign = n_slot * n_tok
    n_src = n_rows + 2 * MOE_BLK
    mesh = plsc.VectorSubcoreMesh(core_axis_name="core", subcore_axis_name="subcore",
                                  num_cores=SC_CORES, num_subcores=SC_SUBCORES)
    tok_vec = pltpu.VMEM((n_tok,), jnp.int32)

    @functools.partial(
        pl.kernel, mesh=mesh, out_type=jax.ShapeDtypeStruct((n_src,), jnp.int32),
        scratch_types=[tok_vec] * (2 * n_slot) + [pltpu.VMEM((N_EXPERTS,), jnp.int32), pltpu.VMEM((n_src,), jnp.int32)],
        compiler_params=pltpu.CompilerParams(needs_layout_passes=False), name="moe_row_map")
    def row_map(*refs):
        ins, src_hbm, scratch = refs[:2 * n_slot + 1], refs[2 * n_slot + 1], refs[2 * n_slot + 2:]
        first_v, src_v = scratch[2 * n_slot], scratch[2 * n_slot + 1]

        @pl.when(jnp.logical_and(lax.axis_index("core") == 0, lax.axis_index("subcore") == 0))
        def _():
            for hbm, vmem in zip(ins, scratch):
                pltpu.sync_copy(hbm, vmem)
            lanes = lax.iota(jnp.int32, SC_LANES)

            @pl.loop(0, n_src, step=SC_LANES)
            def _(i):
                src_v[pl.ds(i, SC_LANES)] = n_assign + ((i + lanes) & (MOE_BLK - 1))

            for slot in range(n_slot):
                e_v, r_v = scratch[slot], scratch[n_slot + slot]

                @pl.loop(0, n_tok, step=SC_LANES)
                def _(t):
                    row = plsc.load_gather(first_v, [e_v[pl.ds(t, SC_LANES)]]) + r_v[pl.ds(t, SC_LANES)]
                    plsc.store_scatter(src_v, [row + MOE_BLK], slot * n_tok + t + lanes)

            pltpu.sync_copy(src_v, src_hbm)

    return row_map(*experts, *ranks, first_row)


def _expert_kernel(blk_e_ref, n_used_ref, src_ref, next_e_ref, wslot_ref, hn_ref, wg_ref, wu_ref, wd_ref, y2_ref,
                   xbuf, ybuf, wg_f, wu_f, wd_f, wg_s, wu_s, wd_s, ssem, wsem, *, n_tok, layer):
    b = pl.program_id(0)
    n_used = n_used_ref[0]
    cur = b % 2
    nxt = 1 - cur

    tile = lambda i: pl.ds(i * ROW_CHUNKS, ROW_CHUNKS)

    def scatter(blk, slot, i):
        row = src_ref[(blk + 1) * MOE_BLK + i]
        return pltpu.make_async_copy(ybuf.at[slot, tile(i)], y2_ref.at[row], ssem.at[slot])

    wait_block = lambda slot: pltpu.make_async_copy(ybuf.at[slot], ybuf.at[slot], ssem.at[slot]).wait()

    def weights(expert, slot, act):
        for hbm, buf in ((wg_ref, wg_f), (wu_ref, wu_f), (wd_ref, wd_f)):
            act(pltpu.make_async_copy(hbm.at[layer, expert], buf.at[slot], wsem.at[slot]))

    @pl.when(b == 0)
    def _():
        ybuf[1] = jnp.zeros(ybuf.shape[1:], ybuf.dtype)
        weights(blk_e_ref[0], wslot_ref[0], lambda c: c.start())

    @pl.when(b < n_used)
    def _():
        new_expert = jnp.logical_or(b == 0, blk_e_ref[b] != blk_e_ref[jnp.maximum(b - 1, 0)])

        @pl.when(new_expert)
        def _():
            slot = wslot_ref[b]
            weights(blk_e_ref[b], slot, lambda c: c.wait())
            wg_s[...] = wg_f[slot].astype(BF16)
            wu_s[...] = wu_f[slot].astype(BF16)
            wd_s[...] = wd_f[slot].astype(BF16)

            @pl.when(next_e_ref[b] >= 0)
            def _():
                weights(next_e_ref[b], 1 - slot, lambda c: c.start())

        @pl.when(b >= 1)
        def _():
            wait_block(cur)

        for i in range(MOE_BLK):
            tok = src_ref[(b + 1) * MOE_BLK + i] & (n_tok - 1)
            xbuf[tile(i), :] = hn_ref[pl.ds(pl.multiple_of(tok * ROW_CHUNKS, ROW_CHUNKS), ROW_CHUNKS), :]
            scatter(b - 1, nxt, i).start(priority=i % 2)
        x = _load_row_tiles(xbuf).astype(BF16)
        g = jnp.dot(x, wg_s[...], preferred_element_type=F32)
        u = jnp.dot(x, wu_s[...], preferred_element_type=F32)
        hb = (g * jax.nn.sigmoid(g) * u).astype(BF16)
        _store_row_tiles(ybuf, jnp.dot(hb, wd_s[...], preferred_element_type=F32), (cur,))

    @pl.when(b == n_used)
    def _():
        wait_block(cur)
        for i in range(MOE_BLK):
            scatter(b - 1, nxt, i).start(priority=i % 2)
        wait_block(nxt)


def _expert_ffn(hn, src, blk_e, n_used, next_e, wslot, w_g, w_u, w_d, layer):
    n_tok = hn.shape[0] // ROW_CHUNKS
    assert n_tok & (n_tok - 1) == 0, "dump-row aliasing masks the token index with T - 1"
    n_blocks = (src.shape[0] - 2 * MOE_BLK) // MOE_BLK
    in_hbm = pl.BlockSpec(memory_space=pltpu.HBM)
    return pl.pallas_call(
        functools.partial(_expert_kernel, n_tok=n_tok, layer=layer),
        grid_spec=pltpu.PrefetchScalarGridSpec(
            num_scalar_prefetch=5,
            grid=(n_blocks + 1,),
            in_specs=[pl.BlockSpec(hn.shape, lambda b, *_: (0, 0), pipeline_mode=pl.Buffered(1)),
                      in_hbm, in_hbm, in_hbm],
            out_specs=pl.BlockSpec(memory_space=pltpu.HBM),
            scratch_shapes=[pltpu.VMEM((MOE_BLK * ROW_CHUNKS, LANES), jnp.uint32),
                            pltpu.VMEM((2, MOE_BLK * ROW_CHUNKS, LANES), jnp.uint32),
                            pltpu.VMEM((2, D_MODEL, D_EXPERT), F32), pltpu.VMEM((2, D_MODEL, D_EXPERT), F32),
                            pltpu.VMEM((2, D_EXPERT, D_MODEL), F32),
                            pltpu.VMEM((D_MODEL, D_EXPERT), BF16), pltpu.VMEM((D_MODEL, D_EXPERT), BF16),
                            pltpu.VMEM((D_EXPERT, D_MODEL), BF16),
                            pltpu.SemaphoreType.DMA((2,)), pltpu.SemaphoreType.DMA((2,))]),
        out_shape=jax.ShapeDtypeStruct((2 * n_tok + MOE_BLK, ROW_CHUNKS, LANES), jnp.uint32),
        compiler_params=_cparams(("arbitrary",)),
        name="moe_experts",
    )(blk_e, n_used, src, next_e, wslot, hn, w_g, w_u, w_d)


def _moe_add_kernel(h_ref, gate_ref, y0_ref, y1_ref, o_ref):
    gate = gate_ref[...]
    o_ref[...] = h_ref[...] + gate[:, 0:1] * _load_row_tiles(y0_ref) + gate[:, 1:2] * _load_row_tiles(y1_ref)


def _moe_specs(n_tok):
    slots = n_tok // TOK_TILE
    y2_rows = (TOK_TILE * ROW_CHUNKS, LANES)
    return [pl.BlockSpec((TOK_TILE, D_MODEL), lambda i: (i, 0)), pl.BlockSpec((TOK_TILE, LANES), lambda i: (i, 0)),
            pl.BlockSpec(y2_rows, lambda i: (i, 0)), pl.BlockSpec(y2_rows, lambda i: (slots + i, 0))]


def _moe_add(h, gate, y2):
    n_tok = h.shape[0]
    return pl.pallas_call(
        _moe_add_kernel,
        grid=(n_tok // TOK_TILE,),
        in_specs=_moe_specs(n_tok),
        out_specs=pl.BlockSpec((TOK_TILE, D_MODEL), lambda i: (i, 0)),
        out_shape=jax.ShapeDtypeStruct((n_tok, D_MODEL), F32),
        compiler_params=_cparams(("parallel",)),
        name="moe_add",
    )(h, gate, y2, y2)


def _moe(hn, ids, counts, w_g, w_u, w_d, layer):
    n_tok = hn.shape[0] // ROW_CHUNKS
    n_assign = 2 * n_tok
    n_blocks = n_assign // MOE_BLK + N_EXPERTS
    n_rows = n_blocks * MOE_BLK
    cnt = counts[0, :N_EXPERTS].astype(jnp.int32)
    padded = (cnt + MOE_BLK - 1) // MOE_BLK * MOE_BLK
    pends = jnp.cumsum(padded).astype(jnp.int32)
    pstarts = pends - padded
    blk_start = jnp.arange(n_blocks + 1, dtype=jnp.int32) * MOE_BLK
    blk_e = jnp.minimum(jnp.sum(pends[None, :] <= blk_start[:, None], axis=1), N_EXPERTS - 1).astype(jnp.int32)
    n_used = (pends[-1:] // MOE_BLK).astype(jnp.int32)
    used = (padded > 0)[None, :]
    e_idx = jnp.arange(N_EXPERTS, dtype=jnp.int32)[None, :]
    next_e = jnp.min(jnp.where(used & (e_idx > blk_e[:, None]), e_idx, N_EXPERTS), axis=1)
    next_e = jnp.where(next_e < N_EXPERTS, next_e, -1).astype(jnp.int32)
    ordinal = jnp.sum((used & (e_idx <= blk_e[:, None])).astype(jnp.int32), axis=1) - 1
    src = _row_map((ids[0], ids[1]), (ids[2], ids[3]), pstarts, n_rows)
    y2 = _expert_ffn(hn, src, blk_e, n_used, next_e, (ordinal % 2).astype(jnp.int32), w_g, w_u, w_d, layer)
    return y2.reshape(-1, LANES)


def kernel(x, norm_mix, norm_ffn, w_in_even, ssm_a_re, ssm_a_im, ssm_b_re, ssm_b_im, ssm_c_re, ssm_c_im, ssm_d,
           ssm_log_step, w_glu, b_glu, q_norm, k_norm, w_out_even, w_in_conv, conv_w, w_out_conv, w_router_group,
           b_router_group, w_router_expert, b_router_expert, w_expert_gate, w_expert_up, w_expert_down):
    bsz, s_len, d = x.shape
    x2 = x.reshape(bsz * s_len, d)
    route = lambda layer: _router_operands(norm_ffn[layer], w_router_group[layer], b_router_group[layer],
                                           w_router_expert[layer], b_router_expert[layer])
    experts = lambda layer: (w_expert_gate, w_expert_up, w_expert_down, layer)

    u, q, k, v = _inproj_even(x2, norm_mix[0], w_in_even[0], q_norm[0], k_norm[0])
    tables = _s5_tables(ssm_a_re[0], ssm_a_im[0], ssm_b_re[0], ssm_b_im[0], ssm_c_re[0], ssm_c_im[0], ssm_d[0],
                        ssm_log_step[0])
    y_pre = _s5_core(u, tables, bsz, s_len)
    attn = _dilated_attention(q, k, v, bsz, s_len)
    h, hn, ids, gate, counts = _outproj_even(x2, y_pre, attn, w_glu[0], b_glu[0], w_out_even[0], route(0))
    y2 = _moe(hn, ids, counts, *experts(0))

    h, hn, ids, gate, counts = _conv_layer(h, gate, y2, norm_mix[1], w_in_conv[0], conv_w[0], w_out_conv[0],
                                           route(1), s_len)
    y2 = _moe(hn, ids, counts, *experts(1))
    return _moe_add(h, gate, y2).reshape(bsz, s_len, d)
```

```python
import functools
import math

import jax
import jax.numpy as jnp
from jax import lax
from jax.experimental import pallas as pl
from jax.experimental.pallas import tpu as pltpu
from jax.experimental.pallas import tpu_sc as plsc

F32 = jnp.float32
BF16 = jnp.bfloat16

D_MODEL = 1024
SSM_GROUP = 16
SSM_GROUPS = 40
SSM_WIDTH = SSM_GROUP * SSM_GROUPS
SSM_STATE = 64
ATT_HEADS = 6
ATT_HEAD_DIM = 64
ATT_WIDTH = ATT_HEADS * ATT_HEAD_DIM
DILATIONS = (16, 4, 1)
ATT_BLK = 128
CONV_TAPS = 3
N_GROUPS = 4
EXPERTS_PER_GROUP = 8
N_EXPERTS = N_GROUPS * EXPERTS_PER_GROUP
D_EXPERT = 512
MOE_BLK = 256
RMS_EPS = 1e-6
NEG_INF = -1e30

LANES = 128
SUBLANES = 8
SC_CORES, SC_SUBCORES, SC_LANES = 2, 16, 16
VMEM_LIMIT = 56 * 1024 * 1024

TOK_TILE = 512
SSM_CHUNK = 8
SSM_SLAB_GROUPS = LANES // SSM_GROUP
SSM_SLABS = SSM_WIDTH // LANES
S5_ROW_TILE = 256
ATT_SB = 2048
ROW_CHUNKS = D_MODEL // (2 * LANES)


def _cparams(sem):
    return pltpu.CompilerParams(dimension_semantics=sem, vmem_limit_bytes=VMEM_LIMIT)


def _rms(x, gain):
    return x * lax.rsqrt(jnp.mean(x * x, axis=-1, keepdims=True) + RMS_EPS) * gain


HIGH_HALF = 0xFFFF0000


def _load_row_tiles(ref, lead=(), first=0, n=None):
    n = ref.shape[-2] // ROW_CHUNKS if n is None else n
    bits = jnp.concatenate([ref[lead + (pl.ds(first * ROW_CHUNKS + c, n, stride=ROW_CHUNKS), slice(None))]
                            for c in range(ROW_CHUNKS)], axis=1)
    low = lax.bitcast_convert_type(bits << 16, F32)
    high = lax.bitcast_convert_type(bits & jnp.uint32(HIGH_HALF), F32)
    return jnp.concatenate([low, high], axis=1)


def _store_row_tiles(ref, value, lead=()):
    n, half = value.shape[0], value.shape[1] // 2
    bf16_bits = lambda t: lax.bitcast_convert_type(t.astype(BF16).astype(F32), jnp.uint32)
    bits = (bf16_bits(value[:, :half]) >> 16) | (bf16_bits(value[:, half:]) & jnp.uint32(HIGH_HALF))
    for c in range(ROW_CHUNKS):
        ref[lead + (pl.ds(c, n, stride=ROW_CHUNKS), slice(None))] = bits[:, c * LANES:(c + 1) * LANES]


def _head_norm(t, gain, bd):
    tt = t * t
    hi = tt.astype(BF16)
    lo = (tt - hi.astype(F32)).astype(BF16)
    ss = jnp.dot(hi, bd, preferred_element_type=F32) + jnp.dot(lo, bd, preferred_element_type=F32)
    return t * lax.rsqrt(ss * (1.0 / ATT_HEAD_DIM) + RMS_EPS) * gain


def _inproj_even_kernel(x_ref, g_ref, w_ref, bd_ref, qn_ref, kn_ref, u_ref, q_ref, k_ref, v_ref):
    hn = _rms(x_ref[...], g_ref[...]).astype(BF16)
    proj = jnp.dot(hn, w_ref[...], preferred_element_type=F32)
    for j in range(SSM_SLABS):
        u_ref[j] = proj[:, j * LANES:(j + 1) * LANES]
    bd = bd_ref[...]
    o = SSM_WIDTH
    q = _head_norm(proj[:, o:o + ATT_WIDTH], qn_ref[...], bd) * (ATT_HEAD_DIM ** -0.5)
    k = _head_norm(proj[:, o + ATT_WIDTH:o + 2 * ATT_WIDTH], kn_ref[...], bd)
    v = proj[:, o + 2 * ATT_WIDTH:o + 3 * ATT_WIDTH]
    for j in range(ATT_WIDTH // LANES):
        q_ref[j] = q[:, j * LANES:(j + 1) * LANES]
        k_ref[j] = k[:, j * LANES:(j + 1) * LANES]
        v_ref[j] = v[:, j * LANES:(j + 1) * LANES]


def _inproj_even(x2, gain, w_in, q_norm, k_norm):
    n_tok = x2.shape[0]
    n_slab = ATT_WIDTH // LANES
    head_of = jnp.arange(ATT_WIDTH) // ATT_HEAD_DIM
    bd = (head_of[:, None] == head_of[None, :]).astype(BF16)
    qn = jnp.tile(q_norm.astype(F32), ATT_HEADS)[None]
    kn = jnp.tile(k_norm.astype(F32), ATT_HEADS)[None]
    full = lambda shape: pl.BlockSpec(shape, lambda i: (0,) * len(shape))
    slab = pl.BlockSpec((n_slab, TOK_TILE, LANES), lambda i: (0, i, 0))
    slab_shape = jax.ShapeDtypeStruct((n_slab, n_tok, LANES), F32)
    return pl.pallas_call(
        _inproj_even_kernel,
        grid=(n_tok // TOK_TILE,),
        in_specs=[pl.BlockSpec((TOK_TILE, D_MODEL), lambda i: (i, 0)), full((1, D_MODEL)),
                  full(w_in.shape), full(bd.shape), full(qn.shape), full(kn.shape)],
        out_specs=[pl.BlockSpec((SSM_SLABS, TOK_TILE, LANES), lambda i: (0, i, 0)), slab, slab, slab],
        out_shape=[jax.ShapeDtypeStruct((SSM_SLABS, n_tok, LANES), F32), slab_shape, slab_shape, slab_shape],
        compiler_params=_cparams(("parallel",)),
        name="inproj_even",
    )(x2, gain[None].astype(F32), w_in.astype(BF16), bd, qn, kn)


def _s5_tables(a_re, a_im, b_re, b_im, c_re, c_im, d_skip, log_step):
    f = lambda t: t.astype(F32)
    a_re, a_im, b_re, b_im, c_re, c_im = map(f, (a_re, a_im, b_re, b_im, c_re, c_im))
    L = SSM_CHUNK
    step = jnp.exp(f(log_step))[:, None]
    ks = jnp.arange(L + 1, dtype=F32)[:, None, None]
    mag = jnp.exp(ks * (a_re * step)[None])
    ang = ks * (a_im * step)[None]
    pw_re, pw_im = mag * jnp.cos(ang), mag * jnp.sin(ang)
    nr, ni = pw_re[1] - 1.0, pw_im[1]
    den = a_re * a_re + a_im * a_im
    z_re, z_im = (nr * a_re + ni * a_im) / den, (ni * a_re - nr * a_im) / den
    bb_re = z_re[..., None] * b_re - z_im[..., None] * b_im
    bb_im = z_re[..., None] * b_im + z_im[..., None] * b_re
    lb_re = pw_re[..., None] * bb_re[None] - pw_im[..., None] * bb_im[None]
    lb_im = pw_re[..., None] * bb_im[None] + pw_im[..., None] * bb_re[None]
    kk = jnp.einsum('gop,kgpi->gkio', c_re, lb_re[:L]) - jnp.einsum('gop,kgpi->gkio', c_im, lb_im[:L])
    ti = jnp.arange(L)
    lag = ti[None, :] - ti[:, None]
    m = jnp.where((lag >= 0)[None, :, :, None, None], kk[:, jnp.maximum(lag, 0)], 0.0)
    ns, gs = SSM_SLABS, SSM_SLAB_GROUPS
    lw = L * LANES
    pm = m.reshape(ns, gs, L, L, SSM_GROUP, SSM_GROUP).transpose(0, 2, 1, 4, 3, 5).reshape(ns, lw, L * SSM_GROUP)
    fold_e = lambda t: t[:L][::-1].reshape(L, ns, gs, SSM_STATE, SSM_GROUP).transpose(1, 0, 2, 4, 3).reshape(ns, lw, SSM_STATE)
    pe = jnp.concatenate([fold_e(lb_re), fold_e(lb_im)], axis=-1)
    pw1_re, pw1_im = (t[1:].transpose(1, 0, 2)[:, :, None, :] for t in (pw_re, pw_im))
    cl_re = c_re[:, None] * pw1_re - c_im[:, None] * pw1_im
    cl_im = c_re[:, None] * pw1_im + c_im[:, None] * pw1_re
    fold_f = lambda t: t.reshape(ns, gs, L, SSM_GROUP, SSM_STATE).transpose(0, 4, 2, 1, 3).reshape(ns, SSM_STATE, lw)
    pf = jnp.concatenate([fold_f(cl_re), fold_f(-cl_im)], axis=1)
    per_chain = lambda t: jnp.tile(t.reshape(ns, gs * SSM_STATE // LANES, LANES), (1, 2, 1))
    d_vec = jnp.tile(f(d_skip).reshape(ns, 1, LANES), (1, 1, L))
    return pm.astype(BF16), pe.astype(BF16), pf.astype(BF16), per_chain(pw_re[L]), per_chain(pw_im[L]), d_vec


def _iota2(shape):
    return lax.broadcasted_iota(jnp.int32, shape, 0), lax.broadcasted_iota(jnp.int32, shape, 1)


def _widen(compact, group_major_cols, shape, r_shift, c_shift, sel_rows):
    gmask = SSM_SLAB_GROUPS - 1
    if sel_rows:
        r, c = _iota2((shape[0], compact.shape[0]))
        sel = ((r >> 9) == (c >> 6)) & ((r & (SSM_STATE - 1)) == (c & (SSM_STATE - 1)))
        wide = jnp.dot(sel.astype(BF16), compact, preferred_element_type=F32)
    else:
        r, c = _iota2((compact.shape[1], shape[1]))
        if group_major_cols:
            sel = ((c >> 9) == (r >> 6)) & ((c & (SSM_STATE - 1)) == (r & (SSM_STATE - 1)))
        else:
            sel = ((c >> 7) == (r >> 4)) & ((c & (SSM_GROUP - 1)) == (r & (SSM_GROUP - 1)))
        wide = jnp.dot(compact, sel.astype(BF16), preferred_element_type=F32)
    r, c = _iota2(shape)
    keep = ((r >> r_shift) & gmask) == ((c >> c_shift) & gmask)
    return jnp.where(keep, wide, 0.0).astype(BF16)


def _s5_kernel(u_ref, pm_ref, pe_ref, pf_ref, ar_ref, ai_ref, d_ref, y_ref, m_ref, e_ref, f_ref, x_ref, sr_ref, si_ref,
               *, n_chunk, pitch):
    L = SSM_CHUNK
    n_blk = SSM_SLAB_GROUPS * SSM_STATE // LANES
    n_re = n_blk * LANES
    lw = L * LANES
    m_ref[...] = _widen(pm_ref[...], False, (lw, lw), 4, 4, False)
    e_ref[...] = _widen(pe_ref[...], True, (lw, 2 * n_re), 4, 6, False)
    f_ref[...] = _widen(pf_ref[...], False, (2 * n_re, lw), 6, 4, True)
    tiles = [(b, c0) for b in range(2) for c0 in range(0, n_chunk, S5_ROW_TILE)]
    for b, c0 in tiles:
        r0 = b * n_chunk + c0
        for t in range(L):
            x_ref[r0:r0 + S5_ROW_TILE, t * LANES:(t + 1) * LANES] = (
                u_ref[pl.ds(r0 * L + t, S5_ROW_TILE, stride=L), :].astype(BF16))
        sl = jnp.dot(x_ref[r0:r0 + S5_ROW_TILE, :], e_ref[...], preferred_element_type=F32)
        for j in range(n_blk):
            base = (b * n_blk + j) * pitch + c0
            sr_ref[base:base + S5_ROW_TILE, :] = sl[:, j * LANES:(j + 1) * LANES]
            si_ref[base:base + S5_ROW_TILE, :] = sl[:, n_re + j * LANES:n_re + (j + 1) * LANES]
    ar, ai = ar_ref[...], ai_ref[...]
    half = LANES

    def scan_step(c, carry):
        s_re, s_im = carry
        rows = pl.ds(c, SUBLANES, stride=pitch)
        x_re, x_im = sr_ref[rows, :], si_ref[rows, :]
        sr_ref[rows, :] = s_re
        si_ref[rows, :] = s_im
        return ar * s_re - ai * s_im + x_re, ar * s_im + ai * s_re + x_im

    zero = jnp.zeros((SUBLANES, half), F32)
    lax.fori_loop(0, n_chunk, scan_step, (zero, zero), unroll=8)

    for b, c0 in tiles:
        r0 = b * n_chunk + c0
        chain = lambda ref, j: ref[(b * n_blk + j) * pitch + c0:(b * n_blk + j) * pitch + c0 + S5_ROW_TILE, :]
        sp = jnp.concatenate([chain(sr_ref, j) for j in range(n_blk)] + [chain(si_ref, j) for j in range(n_blk)],
                             axis=1).astype(BF16)
        xt = x_ref[r0:r0 + S5_ROW_TILE, :]
        y = (jnp.dot(xt, m_ref[...], preferred_element_type=F32)
             + jnp.dot(sp, f_ref[...], preferred_element_type=F32)
             + d_ref[...] * xt.astype(F32))
        for t in range(L):
            y_ref[pl.ds(r0 * L + t, S5_ROW_TILE, stride=L), :] = y[:, t * LANES:(t + 1) * LANES]


def _s5_core(u, tables, bsz, s_len):
    assert bsz == 2, "the scan packs (batch, lane block) into the 8 sublanes of one vreg"
    pm, pe, pf, a_r, a_i, d_vec = tables
    n_tok = bsz * s_len
    n_chunk = s_len // SSM_CHUNK
    pitch = n_chunk + SUBLANES
    lw = SSM_CHUNK * LANES
    n_state = 2 * SSM_SLAB_GROUPS * SSM_STATE
    slab = lambda shape, **kw: pl.BlockSpec((None,) + shape, lambda i: (i,) + (0,) * len(shape), **kw)
    once = dict(pipeline_mode=pl.Buffered(1))
    return pl.pallas_call(
        functools.partial(_s5_kernel, n_chunk=n_chunk, pitch=pitch),
        grid=(SSM_SLABS,),
        in_specs=[slab((n_tok, LANES), **once), slab(pm.shape[1:]), slab(pe.shape[1:]), slab(pf.shape[1:]),
                  slab((SUBLANES, LANES)), slab((SUBLANES, LANES)), slab((1, lw))],
        out_specs=slab((n_tok, LANES), **once),
        out_shape=jax.ShapeDtypeStruct((SSM_SLABS, n_tok, LANES), F32),
        scratch_shapes=[pltpu.VMEM((lw, lw), BF16), pltpu.VMEM((lw, n_state), BF16), pltpu.VMEM((n_state, lw), BF16),
                        pltpu.VMEM((bsz * n_chunk, lw), BF16),
                        pltpu.VMEM((SUBLANES * pitch, LANES), F32),
                        pltpu.VMEM((SUBLANES * pitch, LANES), F32)],
        compiler_params=_cparams(("parallel",)),
        name="s5_core",
    )(u, pm, pe, pf, a_r, a_i, d_vec)


def _attn_kernel(slope_ref, q_ref, kp_ref, kc_ref, vp_ref, vc_ref, o_ref,
                 q4_ref, k4_ref, v4_ref, k1_ref, v1_ref, m_ref, l_ref, acc_ref):
    slab = pl.program_id(1)
    sb = pl.program_id(2)
    fine = DILATIONS[1]
    nq, nk = ATT_SB // fine, 2 * ATT_SB // fine
    for r in range(fine):
        q4_ref[r * nq:(r + 1) * nq, :] = q_ref[pl.ds(r, nq, stride=fine), :]
        for dst, prev, cur in ((k4_ref, kp_ref, kc_ref), (v4_ref, vp_ref, vc_ref)):
            dst[r * nk:r * nk + nq, :] = prev[pl.ds(r, nq, stride=fine), :]
            dst[r * nk + nq:(r + 1) * nk, :] = cur[pl.ds(r, nq, stride=fine), :]
    for dst, prev, cur in ((k1_ref, kp_ref, kc_ref), (v1_ref, vp_ref, vc_ref)):
        dst[0:ATT_BLK, :] = prev[ATT_SB - ATT_BLK:ATT_SB, :]
        dst[ATT_BLK:ATT_BLK + ATT_SB, :] = cur[...]

    lane = lax.broadcasted_iota(jnp.int32, (ATT_BLK, LANES), 1)
    head0 = lane < ATT_HEAD_DIM
    qi = lax.broadcasted_iota(jnp.int32, (ATT_BLK, 2 * ATT_BLK), 0)
    kj = lax.broadcasted_iota(jnp.int32, (ATT_BLK, 2 * ATT_BLK), 1)
    back = qi + ATT_BLK - kj
    band = (back >= 0) & (back <= ATT_BLK)
    neg_steps = -back.astype(F32)
    slopes = (slope_ref[2 * slab], slope_ref[2 * slab + 1])

    for pat, dil in enumerate(DILATIONS):
        span = ATT_BLK * dil

        def tile(idx, carry, dil=dil, span=span, pat=pat):
            start = (idx // dil) * span + idx % dil
            seq_ok = jnp.logical_or(sb > 0, idx >= dil)
            valid = band & ((kj >= ATT_BLK) | seq_ok)
            if dil == 1:
                start = pl.multiple_of(start, ATT_BLK)
                rows = pl.ds(start, ATT_BLK)
                qt, k_src, v_src, k_rows = q_ref[rows, :], k1_ref, v1_ref, pl.ds(start, 2 * ATT_BLK)
            elif dil == fine:
                first = pl.multiple_of((idx // fine) * ATT_BLK, ATT_BLK)
                rows = pl.ds(start, ATT_BLK, stride=dil)
                qt = q4_ref[pl.ds((idx % fine) * nq + first, ATT_BLK), :]
                k_src, v_src = k4_ref, v4_ref
                k_rows = pl.ds((idx % fine) * nk + (nq - ATT_BLK) + first, 2 * ATT_BLK)
            else:
                rows = pl.ds(start, ATT_BLK, stride=dil)
                qt = q4_ref[pl.ds((idx % fine) * nq + idx // fine, ATT_BLK, stride=fine), :]
                k_src, v_src = k4_ref, v4_ref
                k_rows = pl.ds((idx % fine) * nk + idx // fine, 2 * ATT_BLK, stride=fine)
            kt = k_src[k_rows, :].astype(BF16)
            vt = v_src[k_rows, :].astype(BF16)
            q0 = jnp.where(head0, qt, 0.0)
            parts = []
            for hh, qh in enumerate((q0, qt - q0)):
                s = lax.dot_general(qh.astype(BF16), kt, (((1,), (1,)), ((), ())), preferred_element_type=F32)
                s = jnp.where(valid, s + (slopes[hh] * float(dil)) * neg_steps, NEG_INF)
                m = jnp.max(s, axis=-1, keepdims=True)
                p = jnp.exp(s - m)
                l = jnp.sum(p, axis=-1, keepdims=True)
                o = jnp.dot(p.astype(BF16), vt, preferred_element_type=F32)
                parts.append((m, l, o))
            (m0, l0, o0), (m1, l1, o1) = parts
            m_t = jnp.where(head0, m0, m1)
            l_t = jnp.where(head0, l0, l1)
            o_t = jnp.where(head0, o0, o1)
            if pat == 0:
                m_ref[rows, :] = m_t
                l_ref[rows, :] = l_t
                acc_ref[rows, :] = o_t
            else:
                m_old = m_ref[rows, :]
                m_new = jnp.maximum(m_old, m_t)
                a = jnp.exp(m_old - m_new)
                b = jnp.exp(m_t - m_new)
                m_ref[rows, :] = m_new
                l_ref[rows, :] = a * l_ref[rows, :] + b * l_t
                acc_ref[rows, :] = a * acc_ref[rows, :] + b * o_t
            return carry

        lax.fori_loop(0, ATT_SB // ATT_BLK, tile, 0, unroll=4)

    o_ref[...] = (acc_ref[...] / l_ref[...]).astype(o_ref.dtype)


def _dilated_attention(q, k, v, bsz, s_len):
    assert DILATIONS == (DILATIONS[1] ** 2, DILATIONS[1], 1) and ATT_SB == ATT_BLK * DILATIONS[0]
    n_slab = q.shape[0]
    shape4 = (n_slab, bsz, s_len, LANES)
    q, k, v = (t.reshape(shape4) for t in (q, k, v))
    slopes = jnp.asarray([2.0 ** (-8.0 * (h + 1) / ATT_HEADS) for h in range(ATT_HEADS)], F32)
    blk = (None, None, ATT_SB, LANES)
    cur = pl.BlockSpec(blk, lambda b, j, i, s: (j, b, i, 0))
    prev = pl.BlockSpec(blk, lambda b, j, i, s: (j, b, jnp.maximum(i - 1, 0), 0))
    out = pl.pallas_call(
        _attn_kernel,
        grid_spec=pltpu.PrefetchScalarGridSpec(
            num_scalar_prefetch=1,
            grid=(bsz, n_slab, s_len // ATT_SB),
            in_specs=[cur, prev, cur, prev, cur],
            out_specs=cur,
            scratch_shapes=[pltpu.VMEM((ATT_SB, LANES), F32),
                            pltpu.VMEM((2 * ATT_SB, LANES), F32), pltpu.VMEM((2 * ATT_SB, LANES), F32),
                            pltpu.VMEM((ATT_BLK + ATT_SB, LANES), F32), pltpu.VMEM((ATT_BLK + ATT_SB, LANES), F32),
                            pltpu.VMEM((ATT_SB, LANES), F32), pltpu.VMEM((ATT_SB, LANES), F32),
                            pltpu.VMEM((ATT_SB, LANES), F32)]),
        out_shape=jax.ShapeDtypeStruct(shape4, BF16),
        compiler_params=_cparams(("parallel", "parallel", "parallel")),
        name="dilated_attn",
    )(slopes, q, k, k, v, v)
    return out.reshape(n_slab, bsz * s_len, LANES)


def _route_epilogue(h, gain_ref, wr_ref, br_ref, tri_ref, cnt_ref, h_ref, hn_ref, ids_ref, gate_ref, cnt_out_ref):
    tm = h.shape[0]
    h_ref[...] = h
    hn = _rms(h, gain_ref[...])
    _store_row_tiles(hn_ref, hn)
    hn_hi = hn.astype(BF16)
    hn_lo = (hn - hn_hi.astype(F32)).astype(BF16)
    hi_part = jnp.dot(hn_hi, wr_ref[...], preferred_element_type=F32)
    logits = (hi_part[:, :LANES] + (hi_part[:, LANES:]
                                    + jnp.dot(hn_lo, wr_ref[:, :LANES], preferred_element_type=F32))) + br_ref[...]
    lane = lax.broadcasted_iota(jnp.int32, (tm, LANES), 1)
    big = jnp.int32(LANES)
    rmax = lambda t: jnp.max(t, axis=-1, keepdims=True)
    rmin = lambda t: jnp.min(t, axis=-1, keepdims=True)
    rsum = lambda t: jnp.sum(t, axis=-1, keepdims=True)
    gmask = lane < N_GROUPS
    gl = jnp.where(gmask, logits, -jnp.inf)
    gmax = rmax(gl)
    ge = jnp.where(gmask, jnp.exp(gl - gmax), 0.0)
    gprob = ge / rsum(ge)
    g_w = rmax(gprob)
    grp = rmin(jnp.where(gmask & (gprob == g_w), lane, big))
    group_of_lane = (lane - N_GROUPS) >> int(math.log2(EXPERTS_PER_GROUP))
    emask = (lane >= N_GROUPS) & (lane < N_GROUPS + N_EXPERTS) & (group_of_lane == grp)
    el = jnp.where(emask, logits, -jnp.inf)
    ee = jnp.where(emask, jnp.exp(el - rmax(el)), 0.0)
    ep = jnp.where(emask, ee / rsum(ee), -1.0)
    p1 = rmax(ep)
    i1 = rmin(jnp.where(ep == p1, lane, big))
    ep2 = jnp.where(lane == i1, -1.0, ep)
    p2 = rmax(ep2)
    i2 = rmin(jnp.where(ep2 == p2, lane, big))
    e1, e2 = i1 - N_GROUPS, i2 - N_GROUPS
    psum = p1 + p2
    gate1, gate2 = g_w * p1 / psum, g_w * p2 / psum
    oh1, oh2 = lane == e1, lane == e2
    member = (oh1 | oh2).astype(BF16)
    before = jnp.dot(tri_ref[...], member, preferred_element_type=F32) + cnt_ref[...]
    r1 = rsum(jnp.where(oh1, before, 0.0)).astype(jnp.int32)
    r2 = rsum(jnp.where(oh2, before, 0.0)).astype(jnp.int32)
    cnt_ref[...] = cnt_ref[...] + jnp.sum(member.astype(F32), axis=0, keepdims=True)
    ids = jnp.where(lane == 0, e1, jnp.where(lane == 1, e2, jnp.where(lane == 2, r1, jnp.where(lane == 3, r2, 0))))
    ids_ref[...] = jnp.transpose(ids)[0:SUBLANES, :]
    gate_ref[...] = jnp.where(lane == 0, gate1, jnp.where(lane == 1, gate2, 0.0))
    cnt_out_ref[...] = jnp.broadcast_to(cnt_ref[...], cnt_out_ref.shape)


def _router_operands(norm_gain, w_rg, b_rg, w_re, b_re):
    pad = LANES - N_GROUPS - N_EXPERTS
    wr = jnp.pad(jnp.concatenate([w_rg, w_re], axis=1).astype(F32), ((0, 0), (0, pad)))
    br = jnp.pad(jnp.concatenate([b_rg, b_re]).astype(F32), (0, pad))[None]
    r = jnp.arange(TOK_TILE)
    tri = (r[None, :] < r[:, None]).astype(BF16)
    wr_hi = wr.astype(BF16)
    wr_lo = (wr - wr_hi.astype(F32)).astype(BF16)
    return norm_gain[None].astype(F32), jnp.concatenate([wr_hi, wr_lo], axis=1), br, tri


def _route_specs(n_tok):
    full = lambda shape: pl.BlockSpec(shape, lambda i: (0,) * len(shape))
    in_specs = [full((1, D_MODEL)), full((D_MODEL, 2 * LANES)), full((1, LANES)), full((TOK_TILE, TOK_TILE))]
    tok = lambda w: pl.BlockSpec((TOK_TILE, w), lambda i: (i, 0))
    out_specs = [tok(D_MODEL), pl.BlockSpec((TOK_TILE * ROW_CHUNKS, LANES), lambda i: (i, 0)),
                 pl.BlockSpec((SUBLANES, TOK_TILE), lambda i: (0, i)), tok(LANES), full((SUBLANES, LANES))]
    out_shape = [jax.ShapeDtypeStruct((n_tok, D_MODEL), F32), jax.ShapeDtypeStruct((n_tok * ROW_CHUNKS, LANES), jnp.uint32),
                 jax.ShapeDtypeStruct((SUBLANES, n_tok), jnp.int32), jax.ShapeDtypeStruct((n_tok, LANES), F32),
                 jax.ShapeDtypeStruct((SUBLANES, LANES), F32)]
    return in_specs, out_specs, out_shape


def _gelu_tanh(x):
    return 0.5 * x * (1.0 + jnp.tanh(math.sqrt(2.0 / math.pi) * (x + 0.044715 * (x * x * x))))


def _outproj_even_kernel(x_ref, y_ref, a_ref, wglu_ref, bglu_ref, wout_ref, gain_ref, wr_ref, br_ref, tri_ref,
                         h_ref, hn_ref, ids_ref, gate_ref, cnt_out_ref, cnt_ref):
    @pl.when(pl.program_id(0) == 0)
    def _():
        cnt_ref[...] = jnp.zeros_like(cnt_ref)

    y = _gelu_tanh(jnp.concatenate([y_ref[j] for j in range(SSM_SLABS)], axis=1))
    y = y * jax.nn.sigmoid(jnp.dot(y.astype(BF16), wglu_ref[...], preferred_element_type=F32) + bglu_ref[...])
    mix = jnp.dot(y.astype(BF16), wout_ref[0:SSM_WIDTH, :], preferred_element_type=F32)
    for j in range(ATT_WIDTH // LANES):
        rows = slice(SSM_WIDTH + j * LANES, SSM_WIDTH + (j + 1) * LANES)
        mix = mix + jnp.dot(a_ref[j], wout_ref[rows, :], preferred_element_type=F32)
    _route_epilogue(x_ref[...] + mix, gain_ref, wr_ref, br_ref, tri_ref, cnt_ref,
                    h_ref, hn_ref, ids_ref, gate_ref, cnt_out_ref)


def _outproj_even(x2, y_pre, attn, w_glu, b_glu, w_out, route_ops):
    n_tok = x2.shape[0]
    n_slab = attn.shape[0]
    r_in, r_out, r_shape = _route_specs(n_tok)
    full = lambda shape: pl.BlockSpec(shape, lambda i: (0,) * len(shape))
    return pl.pallas_call(
        _outproj_even_kernel,
        grid=(n_tok // TOK_TILE,),
        in_specs=[pl.BlockSpec((TOK_TILE, D_MODEL), lambda i: (i, 0)),
                  pl.BlockSpec((SSM_SLABS, TOK_TILE, LANES), lambda i: (0, i, 0)),
                  pl.BlockSpec((n_slab, TOK_TILE, LANES), lambda i: (0, i, 0)),
                  full(w_glu.shape), full((1, SSM_WIDTH)), full(w_out.shape)] + r_in,
        out_specs=r_out, out_shape=r_shape,
        scratch_shapes=[pltpu.VMEM((1, LANES), F32)],
        compiler_params=_cparams(("arbitrary",)),
        name="outproj_even",
    )(x2, y_pre, attn, w_glu.astype(BF16), b_glu[None].astype(F32), w_out.astype(BF16), *route_ops)


def _conv_layer_kernel(h_ref, pgate_ref, y0_ref, y1_ref, gmix_ref, win_ref, cw_ref, wout_ref, gain_ref, wr_ref, br_ref,
                       tri_ref, ho_ref, hn_ref, ids_ref, gate_ref, cnt_out_ref, cnt_ref, zc_ref, *, tiles_per_seq):
    i = pl.program_id(0)

    @pl.when(i == 0)
    def _():
        cnt_ref[...] = jnp.zeros_like(cnt_ref)

    @pl.when(i % tiles_per_seq == 0)
    def _():
        zc_ref[0:SUBLANES, :] = jnp.zeros((SUBLANES, D_MODEL), F32)

    tm = h_ref.shape[0]
    c = D_MODEL
    half = tm // 2
    parts = (0, half)
    h_in, hn = {}, {}
    for r0 in parts:
        rows = slice(r0, r0 + half)
        pgate = pgate_ref[rows, :]
        h_in[r0] = (h_ref[rows, :] + pgate[:, 0:1] * _load_row_tiles(y0_ref, first=r0, n=half)
                    + pgate[:, 1:2] * _load_row_tiles(y1_ref, first=r0, n=half))
        hn[r0] = _rms(h_in[r0], gmix_ref[...]).astype(BF16)
    b_gate, zc = {}, {}
    for r0 in parts:
        b_gate[r0] = jnp.dot(hn[r0], win_ref[:, 0:c], preferred_element_type=F32)
        zc[r0] = (jnp.dot(hn[r0], win_ref[:, c:2 * c], preferred_element_type=F32)
                  * jnp.dot(hn[r0], win_ref[:, 2 * c:3 * c], preferred_element_type=F32))
    h_out = []
    for r0 in parts:
        z0 = SUBLANES + r0
        zc_ref[z0:z0 + half, :] = zc[r0]
        conv = cw_ref[CONV_TAPS - 1:CONV_TAPS, :] * zc[r0]
        for back in range(1, CONV_TAPS):
            tap = CONV_TAPS - 1 - back
            conv = conv + cw_ref[tap:tap + 1, :] * zc_ref[z0 - back:z0 - back + half, :]
        mix = jnp.dot((b_gate[r0] * conv).astype(BF16), wout_ref[...], preferred_element_type=F32)
        h_out.append(h_in[r0] + mix)
    zc_ref[0:SUBLANES, :] = zc_ref[tm:tm + SUBLANES, :]
    _route_epilogue(jnp.concatenate(h_out, axis=0), gain_ref, wr_ref, br_ref, tri_ref, cnt_ref,
                    ho_ref, hn_ref, ids_ref, gate_ref, cnt_out_ref)


def _conv_layer(h1, pgate, y2, gain_mix, w_in, conv_w, w_out, route_ops, s_len):
    n_tok = h1.shape[0]
    r_in, r_out, r_shape = _route_specs(n_tok)
    full = lambda shape: pl.BlockSpec(shape, lambda i: (0,) * len(shape))
    return pl.pallas_call(
        functools.partial(_conv_layer_kernel, tiles_per_seq=s_len // TOK_TILE),
        grid=(n_tok // TOK_TILE,),
        in_specs=_moe_specs(n_tok) + [full((1, D_MODEL)), full(w_in.shape), full(conv_w.shape), full(w_out.shape)] + r_in,
        out_specs=r_out, out_shape=r_shape,
        scratch_shapes=[pltpu.VMEM((1, LANES), F32), pltpu.VMEM((TOK_TILE + SUBLANES, D_MODEL), F32)],
        compiler_params=_cparams(("arbitrary",)),
        name="conv_layer",
    )(h1, pgate, y2, y2, gain_mix[None].astype(F32), w_in.astype(BF16), conv_w.astype(F32), w_out.astype(BF16),
      *route_ops)


def _row_map(experts, ranks, first_row, n_rows):
    n_slot, n_tok = len(experts), experts[0].shape[0]
    n_assign = n_slot * n_tok
    n_src = n_rows + 2 * MOE_BLK
    mesh = plsc.VectorSubcoreMesh(core_axis_name="core", subcore_axis_name="subcore",
                                  num_cores=SC_CORES, num_subcores=SC_SUBCORES)
    tok_vec = pltpu.VMEM((n_tok,), jnp.int32)

    @functools.partial(
        pl.kernel, mesh=mesh, out_type=jax.ShapeDtypeStruct((n_src,), jnp.int32),
        scratch_types=[tok_vec] * (2 * n_slot) + [pltpu.VMEM((N_EXPERTS,), jnp.int32), pltpu.VMEM((n_src,), jnp.int32)],
        compiler_params=pltpu.CompilerParams(needs_layout_passes=False), name="moe_row_map")
    def row_map(*refs):
        ins, src_hbm, scratch = refs[:2 * n_slot + 1], refs[2 * n_slot + 1], refs[2 * n_slot + 2:]
        first_v, src_v = scratch[2 * n_slot], scratch[2 * n_slot + 1]

        @pl.when(jnp.logical_and(lax.axis_index("core") == 0, lax.axis_index("subcore") == 0))
        def _():
            for hbm, vmem in zip(ins, scratch):
                pltpu.sync_copy(hbm, vmem)
            lanes = lax.iota(jnp.int32, SC_LANES)

            @pl.loop(0, n_src, step=SC_LANES)
            def _(i):
                src_v[pl.ds(i, SC_LANES)] = n_assign + ((i + lanes) & (MOE_BLK - 1))

            for slot in range(n_slot):
                e_v, r_v = scratch[slot], scratch[n_slot + slot]

                @pl.loop(0, n_tok, step=SC_LANES)
                def _(t):
                    row = plsc.load_gather(first_v, [e_v[pl.ds(t, SC_LANES)]]) + r_v[pl.ds(t, SC_LANES)]
                    plsc.store_scatter(src_v, [row + MOE_BLK], slot * n_tok + t + lanes)

            pltpu.sync_copy(src_v, src_hbm)

    return row_map(*experts, *ranks, first_row)


def _expert_kernel(blk_e_ref, n_used_ref, src_ref, next_e_ref, wslot_ref, hn_ref, wg_ref, wu_ref, wd_ref, y2_ref,
                   xbuf, ybuf, wg_f, wu_f, wd_f, wg_s, wu_s, wd_s, ssem, wsem, *, n_tok, layer):
    b = pl.program_id(0)
    n_used = n_used_ref[0]
    cur = b % 2
    nxt = 1 - cur

    tile = lambda i: pl.ds(i * ROW_CHUNKS, ROW_CHUNKS)

    def scatter(blk, slot, i):
        row = src_ref[(blk + 1) * MOE_BLK + i]
        return pltpu.make_async_copy(ybuf.at[slot, tile(i)], y2_ref.at[row], ssem.at[slot])

    wait_block = lambda slot: pltpu.make_async_copy(ybuf.at[slot], ybuf.at[slot], ssem.at[slot]).wait()

    def weights(expert, slot, act):
        for hbm, buf in ((wg_ref, wg_f), (wu_ref, wu_f), (wd_ref, wd_f)):
            act(pltpu.make_async_copy(hbm.at[layer, expert], buf.at[slot], wsem.at[slot]))

    @pl.when(b == 0)
    def _():
        ybuf[1] = jnp.zeros(ybuf.shape[1:], ybuf.dtype)
        weights(blk_e_ref[0], wslot_ref[0], lambda c: c.start())

    @pl.when(b < n_used)
    def _():
        new_expert = jnp.logical_or(b == 0, blk_e_ref[b] != blk_e_ref[jnp.maximum(b - 1, 0)])

        @pl.when(new_expert)
        def _():
            slot = wslot_ref[b]
            weights(blk_e_ref[b], slot, lambda c: c.wait())
            wg_s[...] = wg_f[slot].astype(BF16)
            wu_s[...] = wu_f[slot].astype(BF16)
            wd_s[...] = wd_f[slot].astype(BF16)

            @pl.when(next_e_ref[b] >= 0)
            def _():
                weights(next_e_ref[b], 1 - slot, lambda c: c.start())

        @pl.when(b >= 1)
        def _():
            wait_block(cur)

        for i in range(MOE_BLK):
            tok = src_ref[(b + 1) * MOE_BLK + i] & (n_tok - 1)
            xbuf[tile(i), :] = hn_ref[pl.ds(pl.multiple_of(tok * ROW_CHUNKS, ROW_CHUNKS), ROW_CHUNKS), :]
            scatter(b - 1, nxt, i).start(priority=i % 2)
        x = _load_row_tiles(xbuf).astype(BF16)
        g = jnp.dot(x, wg_s[...], preferred_element_type=F32)
        u = jnp.dot(x, wu_s[...], preferred_element_type=F32)
        hb = (g * jax.nn.sigmoid(g) * u).astype(BF16)
        _store_row_tiles(ybuf, jnp.dot(hb, wd_s[...], preferred_element_type=F32), (cur,))

    @pl.when(b == n_used)
    def _():
        wait_block(cur)
        for i in range(MOE_BLK):
            scatter(b - 1, nxt, i).start(priority=i % 2)
        wait_block(nxt)


def _expert_ffn(hn, src, blk_e, n_used, next_e, wslot, w_g, w_u, w_d, layer):
    n_tok = hn.shape[0] // ROW_CHUNKS
    assert n_tok & (n_tok - 1) == 0, "dump-row aliasing masks the token index with T - 1"
    n_blocks = (src.shape[0] - 2 * MOE_BLK) // MOE_BLK
    in_hbm = pl.BlockSpec(memory_space=pltpu.HBM)
    return pl.pallas_call(
        functools.partial(_expert_kernel, n_tok=n_tok, layer=layer),
        grid_spec=pltpu.PrefetchScalarGridSpec(
            num_scalar_prefetch=5,
            grid=(n_blocks + 1,),
            in_specs=[pl.BlockSpec(hn.shape, lambda b, *_: (0, 0), pipeline_mode=pl.Buffered(1)),
                      in_hbm, in_hbm, in_hbm],
            out_specs=pl.BlockSpec(memory_space=pltpu.HBM),
            scratch_shapes=[pltpu.VMEM((MOE_BLK * ROW_CHUNKS, LANES), jnp.uint32),
                            pltpu.VMEM((2, MOE_BLK * ROW_CHUNKS, LANES), jnp.uint32),
                            pltpu.VMEM((2, D_MODEL, D_EXPERT), F32), pltpu.VMEM((2, D_MODEL, D_EXPERT), F32),
                            pltpu.VMEM((2, D_EXPERT, D_MODEL), F32),
                            pltpu.VMEM((D_MODEL, D_EXPERT), BF16), pltpu.VMEM((D_MODEL, D_EXPERT), BF16),
                            pltpu.VMEM((D_EXPERT, D_MODEL), BF16),
                            pltpu.SemaphoreType.DMA((2,)), pltpu.SemaphoreType.DMA((2,))]),
        out_shape=jax.ShapeDtypeStruct((2 * n_tok + MOE_BLK, ROW_CHUNKS, LANES), jnp.uint32),
        compiler_params=_cparams(("arbitrary",)),
        name="moe_experts",
    )(blk_e, n_used, src, next_e, wslot, hn, w_g, w_u, w_d)


def _moe_add_kernel(h_ref, gate_ref, y0_ref, y1_ref, o_ref):
    gate = gate_ref[...]
    o_ref[...] = h_ref[...] + gate[:, 0:1] * _load_row_tiles(y0_ref) + gate[:, 1:2] * _load_row_tiles(y1_ref)


def _moe_specs(n_tok):
    slots = n_tok // TOK_TILE
    y2_rows = (TOK_TILE * ROW_CHUNKS, LANES)
    return [pl.BlockSpec((TOK_TILE, D_MODEL), lambda i: (i, 0)), pl.BlockSpec((TOK_TILE, LANES), lambda i: (i, 0)),
            pl.BlockSpec(y2_rows, lambda i: (i, 0)), pl.BlockSpec(y2_rows, lambda i: (slots + i, 0))]


def _moe_add(h, gate, y2):
    n_tok = h.shape[0]
    return pl.pallas_call(
        _moe_add_kernel,
        grid=(n_tok // TOK_TILE,),
        in_specs=_moe_specs(n_tok),
        out_specs=pl.BlockSpec((TOK_TILE, D_MODEL), lambda i: (i, 0)),
        out_shape=jax.ShapeDtypeStruct((n_tok, D_MODEL), F32),
        compiler_params=_cparams(("parallel",)),
        name="moe_add",
    )(h, gate, y2, y2)


def _moe(hn, ids, counts, w_g, w_u, w_d, layer):
    n_tok = hn.shape[0] // ROW_CHUNKS
    n_assign = 2 * n_tok
    n_blocks = n_assign // MOE_BLK + N_EXPERTS
    n_rows = n_blocks * MOE_BLK
    cnt = counts[0, :N_EXPERTS].astype(jnp.int32)
    padded = (cnt + MOE_BLK - 1) // MOE_BLK * MOE_BLK
    pends = jnp.cumsum(padded).astype(jnp.int32)
    pstarts = pends - padded
    blk_start = jnp.arange(n_blocks + 1, dtype=jnp.int32) * MOE_BLK
    blk_e = jnp.minimum(jnp.sum(pends[None, :] <= blk_start[:, None], axis=1), N_EXPERTS - 1).astype(jnp.int32)
    n_used = (pends[-1:] // MOE_BLK).astype(jnp.int32)
    used = (padded > 0)[None, :]
    e_idx = jnp.arange(N_EXPERTS, dtype=jnp.int32)[None, :]
    next_e = jnp.min(jnp.where(used & (e_idx > blk_e[:, None]), e_idx, N_EXPERTS), axis=1)
    next_e = jnp.where(next_e < N_EXPERTS, next_e, -1).astype(jnp.int32)
    ordinal = jnp.sum((used & (e_idx <= blk_e[:, None])).astype(jnp.int32), axis=1) - 1
    src = _row_map((ids[0], ids[1]), (ids[2], ids[3]), pstarts, n_rows)
    y2 = _expert_ffn(hn, src, blk_e, n_used, next_e, (ordinal % 2).astype(jnp.int32), w_g, w_u, w_d, layer)
    return y2.reshape(-1, LANES)


def kernel(x, norm_mix, norm_ffn, w_in_even, ssm_a_re, ssm_a_im, ssm_b_re, ssm_b_im, ssm_c_re, ssm_c_im, ssm_d,
           ssm_log_step, w_glu, b_glu, q_norm, k_norm, w_out_even, w_in_conv, conv_w, w_out_conv, w_router_group,
           b_router_group, w_router_expert, b_router_expert, w_expert_gate, w_expert_up, w_expert_down):
    bsz, s_len, d = x.shape
    x2 = x.reshape(bsz * s_len, d)
    route = lambda layer: _router_operands(norm_ffn[layer], w_router_group[layer], b_router_group[layer],
                                           w_router_expert[layer], b_router_expert[layer])
    experts = lambda layer: (w_expert_gate, w_expert_up, w_expert_down, layer)

    u, q, k, v = _inproj_even(x2, norm_mix[0], w_in_even[0], q_norm[0], k_norm[0])
    tables = _s5_tables(ssm_a_re[0], ssm_a_im[0], ssm_b_re[0], ssm_b_im[0], ssm_c_re[0], ssm_c_im[0], ssm_d[0],
                        ssm_log_step[0])
    y_pre = _s5_core(u, tables, bsz, s_len)
    attn = _dilated_attention(q, k, v, bsz, s_len)
    h, hn, ids, gate, counts = _outproj_even(x2, y_pre, attn, w_glu[0], b_glu[0], w_out_even[0], route(0))
    y2 = _moe(hn, ids, counts, *experts(0))

    h, hn, ids, gate, counts = _conv_layer(h, gate, y2, norm_mix[1], w_in_conv[0], conv_w[0], w_out_conv[0],
                                           route(1), s_len)
    y2 = _moe(hn, ids, counts, *experts(1))
    return _moe_add(h, gate, y2).reshape(bsz, s_len, d)
```

```python
import functools
import math

import jax
import jax.numpy as jnp
from jax import lax
from jax.experimental import pallas as pl
from jax.experimental.pallas import tpu as pltpu
from jax.experimental.pallas import tpu_sc as plsc

F32 = jnp.float32
BF16 = jnp.bfloat16

D_MODEL = 1024
SSM_GROUP = 16
SSM_GROUPS = 40
SSM_WIDTH = SSM_GROUP * SSM_GROUPS
SSM_STATE = 64
ATT_HEADS = 6
ATT_HEAD_DIM = 64
ATT_WIDTH = ATT_HEADS * ATT_HEAD_DIM
DILATIONS = (16, 4, 1)
ATT_BLK = 128
CONV_TAPS = 3
N_GROUPS = 4
EXPERTS_PER_GROUP = 8
N_EXPERTS = N_GROUPS * EXPERTS_PER_GROUP
D_EXPERT = 512
MOE_BLK = 256
RMS_EPS = 1e-6
NEG_INF = -1e30

LANES = 128
SUBLANES = 8
SC_CORES, SC_SUBCORES, SC_LANES = 2, 16, 16
VMEM_LIMIT = 56 * 1024 * 1024

TOK_TILE = 512
SSM_CHUNK = 8
SSM_SLAB_GROUPS = LANES // SSM_GROUP
SSM_SLABS = SSM_WIDTH // LANES
S5_ROW_TILE = 256
ATT_SB = 2048
ROW_CHUNKS = D_MODEL // (2 * LANES)


def _cparams(sem):
    return pltpu.CompilerParams(dimension_semantics=sem, vmem_limit_bytes=VMEM_LIMIT)


def _rms(x, gain):
    return x * lax.rsqrt(jnp.mean(x * x, axis=-1, keepdims=True) + RMS_EPS) * gain


HIGH_HALF = 0xFFFF0000


def _load_row_tiles(ref, lead=(), first=0, n=None):
    n = ref.shape[-2] // ROW_CHUNKS if n is None else n
    bits = jnp.concatenate([ref[lead + (pl.ds(first * ROW_CHUNKS + c, n, stride=ROW_CHUNKS), slice(None))]
                            for c in range(ROW_CHUNKS)], axis=1)
    low = lax.bitcast_convert_type(bits << 16, F32)
    high = lax.bitcast_convert_type(bits & jnp.uint32(HIGH_HALF), F32)
    return jnp.concatenate([low, high], axis=1)


def _store_row_tiles(ref, value, lead=()):
    n, half = value.shape[0], value.shape[1] // 2
    bf16_bits = lambda t: lax.bitcast_convert_type(t.astype(BF16).astype(F32), jnp.uint32)
    bits = (bf16_bits(value[:, :half]) >> 16) | (bf16_bits(value[:, half:]) & jnp.uint32(HIGH_HALF))
    for c in range(ROW_CHUNKS):
        ref[lead + (pl.ds(c, n, stride=ROW_CHUNKS), slice(None))] = bits[:, c * LANES:(c + 1) * LANES]


def _head_norm(t, gain, bd):
    tt = t * t
    hi = tt.astype(BF16)
    lo = (tt - hi.astype(F32)).astype(BF16)
    ss = jnp.dot(hi, bd, preferred_element_type=F32) + jnp.dot(lo, bd, preferred_element_type=F32)
    return t * lax.rsqrt(ss * (1.0 / ATT_HEAD_DIM) + RMS_EPS) * gain


def _inproj_even_kernel(x_ref, g_ref, w_ref, bd_ref, qn_ref, kn_ref, u_ref, q_ref, k_ref, v_ref):
    hn = _rms(x_ref[...], g_ref[...]).astype(BF16)
    proj = jnp.dot(hn, w_ref[...], preferred_element_type=F32)
    for j in range(SSM_SLABS):
        u_ref[j] = proj[:, j * LANES:(j + 1) * LANES]
    bd = bd_ref[...]
    o = SSM_WIDTH
    q = _head_norm(proj[:, o:o + ATT_WIDTH], qn_ref[...], bd) * (ATT_HEAD_DIM ** -0.5)
    k = _head_norm(proj[:, o + ATT_WIDTH:o + 2 * ATT_WIDTH], kn_ref[...], bd)
    v = proj[:, o + 2 * ATT_WIDTH:o + 3 * ATT_WIDTH]
    for j in range(ATT_WIDTH // LANES):
        q_ref[j] = q[:, j * LANES:(j + 1) * LANES]
        k_ref[j] = k[:, j * LANES:(j + 1) * LANES]
        v_ref[j] = v[:, j * LANES:(j + 1) * LANES]


def _inproj_even(x2, gain, w_in, q_norm, k_norm):
    n_tok = x2.shape[0]
    n_slab = ATT_WIDTH // LANES
    head_of = jnp.arange(ATT_WIDTH) // ATT_HEAD_DIM
    bd = (head_of[:, None] == head_of[None, :]).astype(BF16)
    qn = jnp.tile(q_norm.astype(F32), ATT_HEADS)[None]
    kn = jnp.tile(k_norm.astype(F32), ATT_HEADS)[None]
    full = lambda shape: pl.BlockSpec(shape, lambda i: (0,) * len(shape))
    slab = pl.BlockSpec((n_slab, TOK_TILE, LANES), lambda i: (0, i, 0))
    slab_shape = jax.ShapeDtypeStruct((n_slab, n_tok, LANES), F32)
    return pl.pallas_call(
        _inproj_even_kernel,
        grid=(n_tok // TOK_TILE,),
        in_specs=[pl.BlockSpec((TOK_TILE, D_MODEL), lambda i: (i, 0)), full((1, D_MODEL)),
                  full(w_in.shape), full(bd.shape), full(qn.shape), full(kn.shape)],
        out_specs=[pl.BlockSpec((SSM_SLABS, TOK_TILE, LANES), lambda i: (0, i, 0)), slab, slab, slab],
        out_shape=[jax.ShapeDtypeStruct((SSM_SLABS, n_tok, LANES), F32), slab_shape, slab_shape, slab_shape],
        compiler_params=_cparams(("parallel",)),
        name="inproj_even",
    )(x2, gain[None].astype(F32), w_in.astype(BF16), bd, qn, kn)


def _s5_tables(a_re, a_im, b_re, b_im, c_re, c_im, d_skip, log_step):
    f = lambda t: t.astype(F32)
    a_re, a_im, b_re, b_im, c_re, c_im = map(f, (a_re, a_im, b_re, b_im, c_re, c_im))
    L = SSM_CHUNK
    step = jnp.exp(f(log_step))[:, None]
    ks = jnp.arange(L + 1, dtype=F32)[:, None, None]
    mag = jnp.exp(ks * (a_re * step)[None])
    ang = ks * (a_im * step)[None]
    pw_re, pw_im = mag * jnp.cos(ang), mag * jnp.sin(ang)
    nr, ni = pw_re[1] - 1.0, pw_im[1]
    den = a_re * a_re + a_im * a_im
    z_re, z_im = (nr * a_re + ni * a_im) / den, (ni * a_re - nr * a_im) / den
    bb_re = z_re[..., None] * b_re - z_im[..., None] * b_im
    bb_im = z_re[..., None] * b_im + z_im[..., None] * b_re
    lb_re = pw_re[..., None] * bb_re[None] - pw_im[..., None] * bb_im[None]
    lb_im = pw_re[..., None] * bb_im[None] + pw_im[..., None] * bb_re[None]
    kk = jnp.einsum('gop,kgpi->gkio', c_re, lb_re[:L]) - jnp.einsum('gop,kgpi->gkio', c_im, lb_im[:L])
    ti = jnp.arange(L)
    lag = ti[None, :] - ti[:, None]
    m = jnp.where((lag >= 0)[None, :, :, None, None], kk[:, jnp.maximum(lag, 0)], 0.0)
    ns, gs = SSM_SLABS, SSM_SLAB_GROUPS
    lw = L * LANES
    pm = m.reshape(ns, gs, L, L, SSM_GROUP, SSM_GROUP).transpose(0, 2, 1, 4, 3, 5).reshape(ns, lw, L * SSM_GROUP)
    fold_e = lambda t: t[:L][::-1].reshape(L, ns, gs, SSM_STATE, SSM_GROUP).transpose(1, 0, 2, 4, 3).reshape(ns, lw, SSM_STATE)
    pe = jnp.concatenate([fold_e(lb_re), fold_e(lb_im)], axis=-1)
    pw1_re, pw1_im = (t[1:].transpose(1, 0, 2)[:, :, None, :] for t in (pw_re, pw_im))
    cl_re = c_re[:, None] * pw1_re - c_im[:, None] * pw1_im
    cl_im = c_re[:, None] * pw1_im + c_im[:, None] * pw1_re
    fold_f = lambda t: t.reshape(ns, gs, L, SSM_GROUP, SSM_STATE).transpose(0, 4, 2, 1, 3).reshape(ns, SSM_STATE, lw)
    pf = jnp.concatenate([fold_f(cl_re), fold_f(-cl_im)], axis=1)
    per_chain = lambda t: jnp.tile(t.reshape(ns, gs * SSM_STATE // LANES, LANES), (1, 2, 1))
    d_vec = jnp.tile(f(d_skip).reshape(ns, 1, LANES), (1, 1, L))
    return pm.astype(BF16), pe.astype(BF16), pf.astype(BF16), per_chain(pw_re[L]), per_chain(pw_im[L]), d_vec


def _iota2(shape):
    return lax.broadcasted_iota(jnp.int32, shape, 0), lax.broadcasted_iota(jnp.int32, shape, 1)


def _widen(compact, group_major_cols, shape, r_shift, c_shift, sel_rows):
    gmask = SSM_SLAB_GROUPS - 1
    if sel_rows:
        r, c = _iota2((shape[0], compact.shape[0]))
        sel = ((r >> 9) == (c >> 6)) & ((r & (SSM_STATE - 1)) == (c & (SSM_STATE - 1)))
        wide = jnp.dot(sel.astype(BF16), compact, preferred_element_type=F32)
    else:
        r, c = _iota2((compact.shape[1], shape[1]))
        if group_major_cols:
            sel = ((c >> 9) == (r >> 6)) & ((c & (SSM_STATE - 1)) == (r & (SSM_STATE - 1)))
        else:
            sel = ((c >> 7) == (r >> 4)) & ((c & (SSM_GROUP - 1)) == (r & (SSM_GROUP - 1)))
        wide = jnp.dot(compact, sel.astype(BF16), preferred_element_type=F32)
    r, c = _iota2(shape)
    keep = ((r >> r_shift) & gmask) == ((c >> c_shift) & gmask)
    return jnp.where(keep, wide, 0.0).astype(BF16)


def _s5_kernel(u_ref, pm_ref, pe_ref, pf_ref, ar_ref, ai_ref, d_ref, y_ref, m_ref, e_ref, f_ref, x_ref, sr_ref, si_ref,
               *, n_chunk, pitch):
    L = SSM_CHUNK
    n_blk = SSM_SLAB_GROUPS * SSM_STATE // LANES
    n_re = n_blk * LANES
    lw = L * LANES
    m_ref[...] = _widen(pm_ref[...], False, (lw, lw), 4, 4, False)
    e_ref[...] = _widen(pe_ref[...], True, (lw, 2 * n_re), 4, 6, False)
    f_ref[...] = _widen(pf_ref[...], False, (2 * n_re, lw), 6, 4, True)
    tiles = [(b, c0) for b in range(2) for c0 in range(0, n_chunk, S5_ROW_TILE)]
    for b, c0 in tiles:
        r0 = b * n_chunk + c0
        for t in range(L):
            x_ref[r0:r0 + S5_ROW_TILE, t * LANES:(t + 1) * LANES] = (
                u_ref[pl.ds(r0 * L + t, S5_ROW_TILE, stride=L), :].astype(BF16))
        sl = jnp.dot(x_ref[r0:r0 + S5_ROW_TILE, :], e_ref[...], preferred_element_type=F32)
        for j in range(n_blk):
            base = (b * n_blk + j) * pitch + c0
            sr_ref[base:base + S5_ROW_TILE, :] = sl[:, j * LANES:(j + 1) * LANES]
            si_ref[base:base + S5_ROW_TILE, :] = sl[:, n_re + j * LANES:n_re + (j + 1) * LANES]
    ar, ai = ar_ref[...], ai_ref[...]
    half = LANES

    def scan_step(c, carry):
        s_re, s_im = carry
        rows = pl.ds(c, SUBLANES, stride=pitch)
        x_re, x_im = sr_ref[rows, :], si_ref[rows, :]
        sr_ref[rows, :] = s_re
        si_ref[rows, :] = s_im
        return ar * s_re - ai * s_im + x_re, ar * s_im + ai * s_re + x_im

    zero = jnp.zeros((SUBLANES, half), F32)
    lax.fori_loop(0, n_chunk, scan_step, (zero, zero), unroll=8)

    for b, c0 in tiles:
        r0 = b * n_chunk + c0
        chain = lambda ref, j: ref[(b * n_blk + j) * pitch + c0:(b * n_blk + j) * pitch + c0 + S5_ROW_TILE, :]
        sp = jnp.concatenate([chain(sr_ref, j) for j in range(n_blk)] + [chain(si_ref, j) for j in range(n_blk)],
                             axis=1).astype(BF16)
        xt = x_ref[r0:r0 + S5_ROW_TILE, :]
        y = (jnp.dot(xt, m_ref[...], preferred_element_type=F32)
             + jnp.dot(sp, f_ref[...], preferred_element_type=F32)
             + d_ref[...] * xt.astype(F32))
        for t in range(L):
            y_ref[pl.ds(r0 * L + t, S5_ROW_TILE, stride=L), :] = y[:, t * LANES:(t + 1) * LANES]


def _s5_core(u, tables, bsz, s_len):
    assert bsz == 2, "the scan packs (batch, lane block) into the 8 sublanes of one vreg"
    pm, pe, pf, a_r, a_i, d_vec = tables
    n_tok = bsz * s_len
    n_chunk = s_len // SSM_CHUNK
    pitch = n_chunk + SUBLANES
    lw = SSM_CHUNK * LANES
    n_state = 2 * SSM_SLAB_GROUPS * SSM_STATE
    slab = lambda shape, **kw: pl.BlockSpec((None,) + shape, lambda i: (i,) + (0,) * len(shape), **kw)
    once = dict(pipeline_mode=pl.Buffered(1))
    return pl.pallas_call(
        functools.partial(_s5_kernel, n_chunk=n_chunk, pitch=pitch),
        grid=(SSM_SLABS,),
        in_specs=[slab((n_tok, LANES), **once), slab(pm.shape[1:]), slab(pe.shape[1:]), slab(pf.shape[1:]),
                  slab((SUBLANES, LANES)), slab((SUBLANES, LANES)), slab((1, lw))],
        out_specs=slab((n_tok, LANES), **once),
        out_shape=jax.ShapeDtypeStruct((SSM_SLABS, n_tok, LANES), F32),
        scratch_shapes=[pltpu.VMEM((lw, lw), BF16), pltpu.VMEM((lw, n_state), BF16), pltpu.VMEM((n_state, lw), BF16),
                        pltpu.VMEM((bsz * n_chunk, lw), BF16),
                        pltpu.VMEM((SUBLANES * pitch, LANES), F32),
                        pltpu.VMEM((SUBLANES * pitch, LANES), F32)],
        compiler_params=_cparams(("parallel",)),
        name="s5_core",
    )(u, pm, pe, pf, a_r, a_i, d_vec)


def _attn_kernel(slope_ref, q_ref, kp_ref, kc_ref, vp_ref, vc_ref, o_ref,
                 q4_ref, k4_ref, v4_ref, k1_ref, v1_ref, m_ref, l_ref, acc_ref):
    slab = pl.program_id(1)
    sb = pl.program_id(2)
    fine = DILATIONS[1]
    nq, nk = ATT_SB // fine, 2 * ATT_SB // fine
    for r in range(fine):
        q4_ref[r * nq:(r + 1) * nq, :] = q_ref[pl.ds(r, nq, stride=fine), :]
        for dst, prev, cur in ((k4_ref, kp_ref, kc_ref), (v4_ref, vp_ref, vc_ref)):
            dst[r * nk:r * nk + nq, :] = prev[pl.ds(r, nq, stride=fine), :]
            dst[r * nk + nq:(r + 1) * nk, :] = cur[pl.ds(r, nq, stride=fine), :]
    for dst, prev, cur in ((k1_ref, kp_ref, kc_ref), (v1_ref, vp_ref, vc_ref)):
        dst[0:ATT_BLK, :] = prev[ATT_SB - ATT_BLK:ATT_SB, :]
        dst[ATT_BLK:ATT_BLK + ATT_SB, :] = cur[...]

    lane = lax.broadcasted_iota(jnp.int32, (ATT_BLK, LANES), 1)
    head0 = lane < ATT_HEAD_DIM
    qi = lax.broadcasted_iota(jnp.int32, (ATT_BLK, 2 * ATT_BLK), 0)
    kj = lax.broadcasted_iota(jnp.int32, (ATT_BLK, 2 * ATT_BLK), 1)
    back = qi + ATT_BLK - kj
    band = (back >= 0) & (back <= ATT_BLK)
    neg_steps = -back.astype(F32)
    slopes = (slope_ref[2 * slab], slope_ref[2 * slab + 1])

    for pat, dil in enumerate(DILATIONS):
        span = ATT_BLK * dil

        def tile(idx, carry, dil=dil, span=span, pat=pat):
            start = (idx // dil) * span + idx % dil
            seq_ok = jnp.logical_or(sb > 0, idx >= dil)
            valid = band & ((kj >= ATT_BLK) | seq_ok)
            if dil == 1:
                start = pl.multiple_of(start, ATT_BLK)
                rows = pl.ds(start, ATT_BLK)
                qt, k_src, v_src, k_rows = q_ref[rows, :], k1_ref, v1_ref, pl.ds(start, 2 * ATT_BLK)
            elif dil == fine:
                first = pl.multiple_of((idx // fine) * ATT_BLK, ATT_BLK)
                rows = pl.ds(start, ATT_BLK, stride=dil)
                qt = q4_ref[pl.ds((idx % fine) * nq + first, ATT_BLK), :]
                k_src, v_src = k4_ref, v4_ref
                k_rows = pl.ds((idx % fine) * nk + (nq - ATT_BLK) + first, 2 * ATT_BLK)
            else:
                rows = pl.ds(start, ATT_BLK, stride=dil)
                qt = q4_ref[pl.ds((idx % fine) * nq + idx // fine, ATT_BLK, stride=fine), :]
                k_src, v_src = k4_ref, v4_ref
                k_rows = pl.ds((idx % fine) * nk + idx // fine, 2 * ATT_BLK, stride=fine)
            kt = k_src[k_rows, :].astype(BF16)
            vt = v_src[k_rows, :].astype(BF16)
            q0 = jnp.where(head0, qt, 0.0)
            parts = []
            for hh, qh in enumerate((q0, qt - q0)):
                s = lax.dot_general(qh.astype(BF16), kt, (((1,), (1,)), ((), ())), preferred_element_type=F32)
                s = jnp.where(valid, s + (slopes[hh] * float(dil)) * neg_steps, NEG_INF)
                m = jnp.max(s, axis=-1, keepdims=True)
                p = jnp.exp(s - m)
                l = jnp.sum(p, axis=-1, keepdims=True)
                o = jnp.dot(p.astype(BF16), vt, preferred_element_type=F32)
                parts.append((m, l, o))
            (m0, l0, o0), (m1, l1, o1) = parts
            m_t = jnp.where(head0, m0, m1)
            l_t = jnp.where(head0, l0, l1)
            o_t = jnp.where(head0, o0, o1)
            if pat == 0:
                m_ref[rows, :] = m_t
                l_ref[rows, :] = l_t
                acc_ref[rows, :] = o_t
            else:
                m_old = m_ref[rows, :]
                m_new = jnp.maximum(m_old, m_t)
                a = jnp.exp(m_old - m_new)
                b = jnp.exp(m_t - m_new)
                m_ref[rows, :] = m_new
                l_ref[rows, :] = a * l_ref[rows, :] + b * l_t
                acc_ref[rows, :] = a * acc_ref[rows, :] + b * o_t
            return carry

        lax.fori_loop(0, ATT_SB // ATT_BLK, tile, 0, unroll=4)

    o_ref[...] = (acc_ref[...] / l_ref[...]).astype(o_ref.dtype)


def _dilated_attention(q, k, v, bsz, s_len):
    assert DILATIONS == (DILATIONS[1] ** 2, DILATIONS[1], 1) and ATT_SB == ATT_BLK * DILATIONS[0]
    n_slab = q.shape[0]
    shape4 = (n_slab, bsz, s_len, LANES)
    q, k, v = (t.reshape(shape4) for t in (q, k, v))
    slopes = jnp.asarray([2.0 ** (-8.0 * (h + 1) / ATT_HEADS) for h in range(ATT_HEADS)], F32)
    blk = (None, None, ATT_SB, LANES)
    cur = pl.BlockSpec(blk, lambda b, j, i, s: (j, b, i, 0))
    prev = pl.BlockSpec(blk, lambda b, j, i, s: (j, b, jnp.maximum(i - 1, 0), 0))
    out = pl.pallas_call(
        _attn_kernel,
        grid_spec=pltpu.PrefetchScalarGridSpec(
            num_scalar_prefetch=1,
            grid=(bsz, n_slab, s_len // ATT_SB),
            in_specs=[cur, prev, cur, prev, cur],
            out_specs=cur,
            scratch_shapes=[pltpu.VMEM((ATT_SB, LANES), F32),
                            pltpu.VMEM((2 * ATT_SB, LANES), F32), pltpu.VMEM((2 * ATT_SB, LANES), F32),
                            pltpu.VMEM((ATT_BLK + ATT_SB, LANES), F32), pltpu.VMEM((ATT_BLK + ATT_SB, LANES), F32),
                            pltpu.VMEM((ATT_SB, LANES), F32), pltpu.VMEM((ATT_SB, LANES), F32),
                            pltpu.VMEM((ATT_SB, LANES), F32)]),
        out_shape=jax.ShapeDtypeStruct(shape4, BF16),
        compiler_params=_cparams(("parallel", "parallel", "parallel")),
        name="dilated_attn",
    )(slopes, q, k, k, v, v)
    return out.reshape(n_slab, bsz * s_len, LANES)


def _route_epilogue(h, gain_ref, wr_ref, br_ref, tri_ref, cnt_ref, h_ref, hn_ref, ids_ref, gate_ref, cnt_out_ref):
    tm = h.shape[0]
    h_ref[...] = h
    hn = _rms(h, gain_ref[...])
    _store_row_tiles(hn_ref, hn)
    hn_hi = hn.astype(BF16)
    hn_lo = (hn - hn_hi.astype(F32)).astype(BF16)
    hi_part = jnp.dot(hn_hi, wr_ref[...], preferred_element_type=F32)
    logits = (hi_part[:, :LANES] + (hi_part[:, LANES:]
                                    + jnp.dot(hn_lo, wr_ref[:, :LANES], preferred_element_type=F32))) + br_ref[...]
    lane = lax.broadcasted_iota(jnp.int32, (tm, LANES), 1)
    big = jnp.int32(LANES)
    rmax = lambda t: jnp.max(t, axis=-1, keepdims=True)
    rmin = lambda t: jnp.min(t, axis=-1, keepdims=True)
    rsum = lambda t: jnp.sum(t, axis=-1, keepdims=True)
    gmask = lane < N_GROUPS
    gl = jnp.where(gmask, logits, -jnp.inf)
    gmax = rmax(gl)
    ge = jnp.where(gmask, jnp.exp(gl - gmax), 0.0)
    gprob = ge / rsum(ge)
    g_w = rmax(gprob)
    grp = rmin(jnp.where(gmask & (gprob == g_w), lane, big))
    group_of_lane = (lane - N_GROUPS) >> int(math.log2(EXPERTS_PER_GROUP))
    emask = (lane >= N_GROUPS) & (lane < N_GROUPS + N_EXPERTS) & (group_of_lane == grp)
    el = jnp.where(emask, logits, -jnp.inf)
    ee = jnp.where(emask, jnp.exp(el - rmax(el)), 0.0)
    ep = jnp.where(emask, ee / rsum(ee), -1.0)
    p1 = rmax(ep)
    i1 = rmin(jnp.where(ep == p1, lane, big))
    ep2 = jnp.where(lane == i1, -1.0, ep)
    p2 = rmax(ep2)
    i2 = rmin(jnp.where(ep2 == p2, lane, big))
    e1, e2 = i1 - N_GROUPS, i2 - N_GROUPS
    psum = p1 + p2
    gate1, gate2 = g_w * p1 / psum, g_w * p2 / psum
    oh1, oh2 = lane == e1, lane == e2
    member = (oh1 | oh2).astype(BF16)
    before = jnp.dot(tri_ref[...], member, preferred_element_type=F32) + cnt_ref[...]
    r1 = rsum(jnp.where(oh1, before, 0.0)).astype(jnp.int32)
    r2 = rsum(jnp.where(oh2, before, 0.0)).astype(jnp.int32)
    cnt_ref[...] = cnt_ref[...] + jnp.sum(member.astype(F32), axis=0, keepdims=True)
    ids = jnp.where(lane == 0, e1, jnp.where(lane == 1, e2, jnp.where(lane == 2, r1, jnp.where(lane == 3, r2, 0))))
    ids_ref[...] = jnp.transpose(ids)[0:SUBLANES, :]
    gate_ref[...] = jnp.where(lane == 0, gate1, jnp.where(lane == 1, gate2, 0.0))
    cnt_out_ref[...] = jnp.broadcast_to(cnt_ref[...], cnt_out_ref.shape)


def _router_operands(norm_gain, w_rg, b_rg, w_re, b_re):
    pad = LANES - N_GROUPS - N_EXPERTS
    wr = jnp.pad(jnp.concatenate([w_rg, w_re], axis=1).astype(F32), ((0, 0), (0, pad)))
    br = jnp.pad(jnp.concatenate([b_rg, b_re]).astype(F32), (0, pad))[None]
    r = jnp.arange(TOK_TILE)
    tri = (r[None, :] < r[:, None]).astype(BF16)
    wr_hi = wr.astype(BF16)
    wr_lo = (wr - wr_hi.astype(F32)).astype(BF16)
    return norm_gain[None].astype(F32), jnp.concatenate([wr_hi, wr_lo], axis=1), br, tri


def _route_specs(n_tok):
    full = lambda shape: pl.BlockSpec(shape, lambda i: (0,) * len(shape))
    in_specs = [full((1, D_MODEL)), full((D_MODEL, 2 * LANES)), full((1, LANES)), full((TOK_TILE, TOK_TILE))]
    tok = lambda w: pl.BlockSpec((TOK_TILE, w), lambda i: (i, 0))
    out_specs = [tok(D_MODEL), pl.BlockSpec((TOK_TILE * ROW_CHUNKS, LANES), lambda i: (i, 0)),
                 pl.BlockSpec((SUBLANES, TOK_TILE), lambda i: (0, i)), tok(LANES), full((SUBLANES, LANES))]
    out_shape = [jax.ShapeDtypeStruct((n_tok, D_MODEL), F32), jax.ShapeDtypeStruct((n_tok * ROW_CHUNKS, LANES), jnp.uint32),
                 jax.ShapeDtypeStruct((SUBLANES, n_tok), jnp.int32), jax.ShapeDtypeStruct((n_tok, LANES), F32),
                 jax.ShapeDtypeStruct((SUBLANES, LANES), F32)]
    return in_specs, out_specs, out_shape


def _gelu_tanh(x):
    return 0.5 * x * (1.0 + jnp.tanh(math.sqrt(2.0 / math.pi) * (x + 0.044715 * (x * x * x))))


def _outproj_even_kernel(x_ref, y_ref, a_ref, wglu_ref, bglu_ref, wout_ref, gain_ref, wr_ref, br_ref, tri_ref,
                         h_ref, hn_ref, ids_ref, gate_ref, cnt_out_ref, cnt_ref):
    @pl.when(pl.program_id(0) == 0)
    def _():
        cnt_ref[...] = jnp.zeros_like(cnt_ref)

    half = x_ref.shape[0] // 2
    parts = [slice(r0, r0 + half) for r0 in (0, half)]
    y = [_gelu_tanh(jnp.concatenate([y_ref[j, rows, :] for j in range(SSM_SLABS)], axis=1)) for rows in parts]
    glu = [jnp.dot(t.astype(BF16), wglu_ref[...], preferred_element_type=F32) for t in y]
    h_out = []
    for rows, t, g in zip(parts, y, glu):
        t = t * jax.nn.sigmoid(g + bglu_ref[...])
        mix = jnp.dot(t.astype(BF16), wout_ref[0:SSM_WIDTH, :], preferred_element_type=F32)
        for j in range(ATT_WIDTH // LANES):
            w_rows = slice(SSM_WIDTH + j * LANES, SSM_WIDTH + (j + 1) * LANES)
            mix = mix + jnp.dot(a_ref[j, rows, :], wout_ref[w_rows, :], preferred_element_type=F32)
        h_out.append(x_ref[rows, :] + mix)
    _route_epilogue(jnp.concatenate(h_out, axis=0), gain_ref, wr_ref, br_ref, tri_ref, cnt_ref,
                    h_ref, hn_ref, ids_ref, gate_ref, cnt_out_ref)


def _outproj_even(x2, y_pre, attn, w_glu, b_glu, w_out, route_ops):
    n_tok = x2.shape[0]
    n_slab = attn.shape[0]
    r_in, r_out, r_shape = _route_specs(n_tok)
    full = lambda shape: pl.BlockSpec(shape, lambda i: (0,) * len(shape))
    return pl.pallas_call(
        _outproj_even_kernel,
        grid=(n_tok // TOK_TILE,),
        in_specs=[pl.BlockSpec((TOK_TILE, D_MODEL), lambda i: (i, 0)),
                  pl.BlockSpec((SSM_SLABS, TOK_TILE, LANES), lambda i: (0, i, 0)),
                  pl.BlockSpec((n_slab, TOK_TILE, LANES), lambda i: (0, i, 0)),
                  full(w_glu.shape), full((1, SSM_WIDTH)), full(w_out.shape)] + r_in,
        out_specs=r_out, out_shape=r_shape,
        scratch_shapes=[pltpu.VMEM((1, LANES), F32)],
        compiler_params=_cparams(("arbitrary",)),
        name="outproj_even",
    )(x2, y_pre, attn, w_glu.astype(BF16), b_glu[None].astype(F32), w_out.astype(BF16), *route_ops)


def _conv_layer_kernel(h_ref, pgate_ref, y0_ref, y1_ref, gmix_ref, win_ref, cw_ref, wout_ref, gain_ref, wr_ref, br_ref,
                       tri_ref, ho_ref, hn_ref, ids_ref, gate_ref, cnt_out_ref, cnt_ref, zc_ref, *, tiles_per_seq):
    i = pl.program_id(0)

    @pl.when(i == 0)
    def _():
        cnt_ref[...] = jnp.zeros_like(cnt_ref)

    @pl.when(i % tiles_per_seq == 0)
    def _():
        zc_ref[0:SUBLANES, :] = jnp.zeros((SUBLANES, D_MODEL), F32)

    tm = h_ref.shape[0]
    c = D_MODEL
    half = tm // 2
    parts = (0, half)
    h_in, hn = {}, {}
    for r0 in parts:
        rows = slice(r0, r0 + half)
        pgate = pgate_ref[rows, :]
        h_in[r0] = (h_ref[rows, :] + pgate[:, 0:1] * _load_row_tiles(y0_ref, first=r0, n=half)
                    + pgate[:, 1:2] * _load_row_tiles(y1_ref, first=r0, n=half))
        hn[r0] = _rms(h_in[r0], gmix_ref[...]).astype(BF16)
    b_gate, zc = {}, {}
    for r0 in parts:
        b_gate[r0] = jnp.dot(hn[r0], win_ref[:, 0:c], preferred_element_type=F32)
        zc[r0] = (jnp.dot(hn[r0], win_ref[:, c:2 * c], preferred_element_type=F32)
                  * jnp.dot(hn[r0], win_ref[:, 2 * c:3 * c], preferred_element_type=F32))
    h_out = []
    for r0 in parts:
        z0 = SUBLANES + r0
        zc_ref[z0:z0 + half, :] = zc[r0]
        conv = cw_ref[CONV_TAPS - 1:CONV_TAPS, :] * zc[r0]
        for back in range(1, CONV_TAPS):
            tap = CONV_TAPS - 1 - back
            conv = conv + cw_ref[tap:tap + 1, :] * zc_ref[z0 - back:z0 - back + half, :]
        mix = jnp.dot((b_gate[r0] * conv).astype(BF16), wout_ref[...], preferred_element_type=F32)
        h_out.append(h_in[r0] + mix)
    zc_ref[0:SUBLANES, :] = zc_ref[tm:tm + SUBLANES, :]
    _route_epilogue(jnp.concatenate(h_out, axis=0), gain_ref, wr_ref, br_ref, tri_ref, cnt_ref,
                    ho_ref, hn_ref, ids_ref, gate_ref, cnt_out_ref)


def _conv_layer(h1, pgate, y2, gain_mix, w_in, conv_w, w_out, route_ops, s_len):
    n_tok = h1.shape[0]
    r_in, r_out, r_shape = _route_specs(n_tok)
    full = lambda shape: pl.BlockSpec(shape, lambda i: (0,) * len(shape))
    return pl.pallas_call(
        functools.partial(_conv_layer_kernel, tiles_per_seq=s_len // TOK_TILE),
        grid=(n_tok // TOK_TILE,),
        in_specs=_moe_specs(n_tok) + [full((1, D_MODEL)), full(w_in.shape), full(conv_w.shape), full(w_out.shape)] + r_in,
        out_specs=r_out, out_shape=r_shape,
        scratch_shapes=[pltpu.VMEM((1, LANES), F32), pltpu.VMEM((TOK_TILE + SUBLANES, D_MODEL), F32)],
        compiler_params=_cparams(("arbitrary",)),
        name="conv_layer",
    )(h1, pgate, y2, y2, gain_mix[None].astype(F32), w_in.astype(BF16), conv_w.astype(F32), w_out.astype(BF16),
      *route_ops)


def _row_map(experts, ranks, first_row, n_rows):
    n_slot, n_tok = len(experts), experts[0].shape[0]
    n_assign = n_slot * n_tok
    n_src = n_rows + 2 * MOE_BLK
    mesh = plsc.VectorSubcoreMesh(core_axis_name="core", subcore_axis_name="subcore",
                                  num_cores=SC_CORES, num_subcores=SC_SUBCORES)
    tok_vec = pltpu.VMEM((n_tok,), jnp.int32)

    @functools.partial(
        pl.kernel, mesh=mesh, out_type=jax.ShapeDtypeStruct((n_src,), jnp.int32),
        scratch_types=[tok_vec] * (2 * n_slot) + [pltpu.VMEM((N_EXPERTS,), jnp.int32), pltpu.VMEM((n_src,), jnp.int32)],
        compiler_params=pltpu.CompilerParams(needs_layout_passes=False), name="moe_row_map")
    def row_map(*refs):
        ins, src_hbm, scratch = refs[:2 * n_slot + 1], refs[2 * n_slot + 1], refs[2 * n_slot + 2:]
        first_v, src_v = scratch[2 * n_slot], scratch[2 * n_slot + 1]

        @pl.when(jnp.logical_and(lax.axis_index("core") == 0, lax.axis_index("subcore") == 0))
        def _():
            for hbm, vmem in zip(ins, scratch):
                pltpu.sync_copy(hbm, vmem)
            lanes = lax.iota(jnp.int32, SC_LANES)

            @pl.loop(0, n_src, step=SC_LANES)
            def _(i):
                src_v[pl.ds(i, SC_LANES)] = n_assign + ((i + lanes) & (MOE_BLK - 1))

            for slot in range(n_slot):
                e_v, r_v = scratch[slot], scratch[n_slot + slot]

                @pl.loop(0, n_tok, step=SC_LANES)
                def _(t):
                    row = plsc.load_gather(first_v, [e_v[pl.ds(t, SC_LANES)]]) + r_v[pl.ds(t, SC_LANES)]
                    plsc.store_scatter(src_v, [row + MOE_BLK], slot * n_tok + t + lanes)

            pltpu.sync_copy(src_v, src_hbm)

    return row_map(*experts, *ranks, first_row)


def _expert_kernel(blk_e_ref, n_used_ref, src_ref, next_e_ref, wslot_ref, hn_ref, wg_ref, wu_ref, wd_ref, y2_ref,
                   xbuf, ybuf, wg_f, wu_f, wd_f, wg_s, wu_s, wd_s, ssem, wsem, *, n_tok, layer):
    b = pl.program_id(0)
    n_used = n_used_ref[0]
    cur = b % 2
    nxt = 1 - cur

    tile = lambda i: pl.ds(i * ROW_CHUNKS, ROW_CHUNKS)

    def scatter(blk, slot, i):
        row = src_ref[(blk + 1) * MOE_BLK + i]
        return pltpu.make_async_copy(ybuf.at[slot, tile(i)], y2_ref.at[row], ssem.at[slot])

    wait_block = lambda slot: pltpu.make_async_copy(ybuf.at[slot], ybuf.at[slot], ssem.at[slot]).wait()

    def weights(expert, slot, act):
        for hbm, buf in ((wg_ref, wg_f), (wu_ref, wu_f), (wd_ref, wd_f)):
            act(pltpu.make_async_copy(hbm.at[layer, expert], buf.at[slot], wsem.at[slot]))

    @pl.when(b == 0)
    def _():
        ybuf[1] = jnp.zeros(ybuf.shape[1:], ybuf.dtype)
        weights(blk_e_ref[0], wslot_ref[0], lambda c: c.start(priority=1))

    @pl.when(b < n_used)
    def _():
        new_expert = jnp.logical_or(b == 0, blk_e_ref[b] != blk_e_ref[jnp.maximum(b - 1, 0)])

        @pl.when(new_expert)
        def _():
            slot = wslot_ref[b]
            weights(blk_e_ref[b], slot, lambda c: c.wait())
            wg_s[...] = wg_f[slot].astype(BF16)
            wu_s[...] = wu_f[slot].astype(BF16)
            wd_s[...] = wd_f[slot].astype(BF16)

            @pl.when(next_e_ref[b] >= 0)
            def _():
                weights(next_e_ref[b], 1 - slot, lambda c: c.start(priority=1))

        @pl.when(b >= 1)
        def _():
            wait_block(cur)

        for i in range(MOE_BLK):
            tok = src_ref[(b + 1) * MOE_BLK + i] & (n_tok - 1)
            xbuf[tile(i), :] = hn_ref[pl.ds(pl.multiple_of(tok * ROW_CHUNKS, ROW_CHUNKS), ROW_CHUNKS), :]
            scatter(b - 1, nxt, i).start()
        x = _load_row_tiles(xbuf).astype(BF16)
        g = jnp.dot(x, wg_s[...], preferred_element_type=F32)
        u = jnp.dot(x, wu_s[...], preferred_element_type=F32)
        hb = (g * jax.nn.sigmoid(g) * u).astype(BF16)
        _store_row_tiles(ybuf, jnp.dot(hb, wd_s[...], preferred_element_type=F32), (cur,))

    @pl.when(b == n_used)
    def _():
        wait_block(cur)
        for i in range(MOE_BLK):
            scatter(b - 1, nxt, i).start()
        wait_block(nxt)


def _expert_ffn(hn, src, blk_e, n_used, next_e, wslot, w_g, w_u, w_d, layer):
    n_tok = hn.shape[0] // ROW_CHUNKS
    assert n_tok & (n_tok - 1) == 0, "dump-row aliasing masks the token index with T - 1"
    n_blocks = (src.shape[0] - 2 * MOE_BLK) // MOE_BLK
    in_hbm = pl.BlockSpec(memory_space=pltpu.HBM)
    return pl.pallas_call(
        functools.partial(_expert_kernel, n_tok=n_tok, layer=layer),
        grid_spec=pltpu.PrefetchScalarGridSpec(
            num_scalar_prefetch=5,
            grid=(n_blocks + 1,),
            in_specs=[pl.BlockSpec(hn.shape, lambda b, *_: (0, 0), pipeline_mode=pl.Buffered(1)),
                      in_hbm, in_hbm, in_hbm],
            out_specs=pl.BlockSpec(memory_space=pltpu.HBM),
            scratch_shapes=[pltpu.VMEM((MOE_BLK * ROW_CHUNKS, LANES), jnp.uint32),
                            pltpu.VMEM((2, MOE_BLK * ROW_CHUNKS, LANES), jnp.uint32),
                            pltpu.VMEM((2, D_MODEL, D_EXPERT), F32), pltpu.VMEM((2, D_MODEL, D_EXPERT), F32),
                            pltpu.VMEM((2, D_EXPERT, D_MODEL), F32),
                            pltpu.VMEM((D_MODEL, D_EXPERT), BF16), pltpu.VMEM((D_MODEL, D_EXPERT), BF16),
                            pltpu.VMEM((D_EXPERT, D_MODEL), BF16),
                            pltpu.SemaphoreType.DMA((2,)), pltpu.SemaphoreType.DMA((2,))]),
        out_shape=jax.ShapeDtypeStruct((2 * n_tok + MOE_BLK, ROW_CHUNKS, LANES), jnp.uint32),
        compiler_params=_cparams(("arbitrary",)),
        name="moe_experts",
    )(blk_e, n_used, src, next_e, wslot, hn, w_g, w_u, w_d)


def _moe_add_kernel(h_ref, gate_ref, y0_ref, y1_ref, o_ref):
    gate = gate_ref[...]
    o_ref[...] = h_ref[...] + gate[:, 0:1] * _load_row_tiles(y0_ref) + gate[:, 1:2] * _load_row_tiles(y1_ref)


def _moe_specs(n_tok):
    slots = n_tok // TOK_TILE
    y2_rows = (TOK_TILE * ROW_CHUNKS, LANES)
    return [pl.BlockSpec((TOK_TILE, D_MODEL), lambda i: (i, 0)), pl.BlockSpec((TOK_TILE, LANES), lambda i: (i, 0)),
            pl.BlockSpec(y2_rows, lambda i: (i, 0)), pl.BlockSpec(y2_rows, lambda i: (slots + i, 0))]


def _moe_add(h, gate, y2):
    n_tok = h.shape[0]
    return pl.pallas_call(
        _moe_add_kernel,
        grid=(n_tok // TOK_TILE,),
        in_specs=_moe_specs(n_tok),
        out_specs=pl.BlockSpec((TOK_TILE, D_MODEL), lambda i: (i, 0)),
        out_shape=jax.ShapeDtypeStruct((n_tok, D_MODEL), F32),
        compiler_params=_cparams(("parallel",)),
        name="moe_add",
    )(h, gate, y2, y2)


def _moe(hn, ids, counts, w_g, w_u, w_d, layer):
    n_tok = hn.shape[0] // ROW_CHUNKS
    n_assign = 2 * n_tok
    n_blocks = n_assign // MOE_BLK + N_EXPERTS
    n_rows = n_blocks * MOE_BLK
    cnt = counts[0, :N_EXPERTS].astype(jnp.int32)
    padded = (cnt + MOE_BLK - 1) // MOE_BLK * MOE_BLK
    pends = jnp.cumsum(padded).astype(jnp.int32)
    pstarts = pends - padded
    blk_start = jnp.arange(n_blocks + 1, dtype=jnp.int32) * MOE_BLK
    blk_e = jnp.minimum(jnp.sum(pends[None, :] <= blk_start[:, None], axis=1), N_EXPERTS - 1).astype(jnp.int32)
    n_used = (pends[-1:] // MOE_BLK).astype(jnp.int32)
    used = (padded > 0)[None, :]
    e_idx = jnp.arange(N_EXPERTS, dtype=jnp.int32)[None, :]
    next_e = jnp.min(jnp.where(used & (e_idx > blk_e[:, None]), e_idx, N_EXPERTS), axis=1)
    next_e = jnp.where(next_e < N_EXPERTS, next_e, -1).astype(jnp.int32)
    ordinal = jnp.sum((used & (e_idx <= blk_e[:, None])).astype(jnp.int32), axis=1) - 1
    src = _row_map((ids[0], ids[1]), (ids[2], ids[3]), pstarts, n_rows)
    y2 = _expert_ffn(hn, src, blk_e, n_used, next_e, (ordinal % 2).astype(jnp.int32), w_g, w_u, w_d, layer)
    return y2.reshape(-1, LANES)


def kernel(x, norm_mix, norm_ffn, w_in_even, ssm_a_re, ssm_a_im, ssm_b_re, ssm_b_im, ssm_c_re, ssm_c_im, ssm_d,
           ssm_log_step, w_glu, b_glu, q_norm, k_norm, w_out_even, w_in_conv, conv_w, w_out_conv, w_router_group,
           b_router_group, w_router_expert, b_router_expert, w_expert_gate, w_expert_up, w_expert_down):
    bsz, s_len, d = x.shape
    x2 = x.reshape(bsz * s_len, d)
    route = lambda layer: _router_operands(norm_ffn[layer], w_router_group[layer], b_router_group[layer],
                                           w_router_expert[layer], b_router_expert[layer])
    experts = lambda layer: (w_expert_gate, w_expert_up, w_expert_down, layer)

    u, q, k, v = _inproj_even(x2, norm_mix[0], w_in_even[0], q_norm[0], k_norm[0])
    tables = _s5_tables(ssm_a_re[0], ssm_a_im[0], ssm_b_re[0], ssm_b_im[0], ssm_c_re[0], ssm_c_im[0], ssm_d[0],
                        ssm_log_step[0])
    y_pre = _s5_core(u, tables, bsz, s_len)
    attn = _dilated_attention(q, k, v, bsz, s_len)
    h, hn, ids, gate, counts = _outproj_even(x2, y_pre, attn, w_glu[0], b_glu[0], w_out_even[0], route(0))
    y2 = _moe(hn, ids, counts, *experts(0))

    h, hn, ids, gate, counts = _conv_layer(h, gate, y2, norm_mix[1], w_in_conv[0], conv_w[0], w_out_conv[0],
                                           route(1), s_len)
    y2 = _moe(hn, ids, counts, *experts(1))
    return _moe_add(h, gate, y2).reshape(bsz, s_len, d)
```

```python
import functools
import math

import jax
import jax.numpy as jnp
from jax import lax
from jax.experimental import pallas as pl
from jax.experimental.pallas import tpu as pltpu
from jax.experimental.pallas import tpu_sc as plsc

F32 = jnp.float32
BF16 = jnp.bfloat16

D_MODEL = 1024
SSM_GROUP = 16
SSM_GROUPS = 40
SSM_WIDTH = SSM_GROUP * SSM_GROUPS
SSM_STATE = 64
ATT_HEADS = 6
ATT_HEAD_DIM = 64
ATT_WIDTH = ATT_HEADS * ATT_HEAD_DIM
DILATIONS = (16, 4, 1)
ATT_BLK = 128
CONV_TAPS = 3
N_GROUPS = 4
EXPERTS_PER_GROUP = 8
N_EXPERTS = N_GROUPS * EXPERTS_PER_GROUP
D_EXPERT = 512
MOE_BLK = 256
RMS_EPS = 1e-6
NEG_INF = -1e30

LANES = 128
SUBLANES = 8
SC_CORES, SC_SUBCORES, SC_LANES = 2, 16, 16
VMEM_LIMIT = 56 * 1024 * 1024

TOK_TILE = 512
SSM_CHUNK = 8
SSM_SLAB_GROUPS = LANES // SSM_GROUP
SSM_SLABS = SSM_WIDTH // LANES
S5_ROW_TILE = 256
ATT_SB = 2048
ATT_GROUP = 4
ROW_CHUNKS = D_MODEL // (2 * LANES)


def _cparams(sem):
    return pltpu.CompilerParams(dimension_semantics=sem, vmem_limit_bytes=VMEM_LIMIT)


def _rms(x, gain):
    return x * lax.rsqrt(jnp.mean(x * x, axis=-1, keepdims=True) + RMS_EPS) * gain


HIGH_HALF = 0xFFFF0000


def _load_row_tiles(ref, lead=(), first=0, n=None):
    n = ref.shape[-2] // ROW_CHUNKS if n is None else n
    bits = jnp.concatenate([ref[lead + (pl.ds(first * ROW_CHUNKS + c, n, stride=ROW_CHUNKS), slice(None))]
                            for c in range(ROW_CHUNKS)], axis=1)
    low = lax.bitcast_convert_type(bits << 16, F32)
    high = lax.bitcast_convert_type(bits & jnp.uint32(HIGH_HALF), F32)
    return jnp.concatenate([low, high], axis=1)


def _store_row_tiles(ref, value, lead=()):
    n, half = value.shape[0], value.shape[1] // 2
    bf16_bits = lambda t: lax.bitcast_convert_type(t.astype(BF16).astype(F32), jnp.uint32)
    bits = (bf16_bits(value[:, :half]) >> 16) | (bf16_bits(value[:, half:]) & jnp.uint32(HIGH_HALF))
    for c in range(ROW_CHUNKS):
        ref[lead + (pl.ds(c, n, stride=ROW_CHUNKS), slice(None))] = bits[:, c * LANES:(c + 1) * LANES]


def _head_norm(t, gain, bd):
    tt = t * t
    hi = tt.astype(BF16)
    lo = (tt - hi.astype(F32)).astype(BF16)
    ss = jnp.dot(hi, bd, preferred_element_type=F32) + jnp.dot(lo, bd, preferred_element_type=F32)
    return t * lax.rsqrt(ss * (1.0 / ATT_HEAD_DIM) + RMS_EPS) * gain


def _inproj_even_kernel(x_ref, g_ref, w_ref, bd_ref, qn_ref, kn_ref, u_ref, q_ref, k_ref, v_ref):
    half = x_ref.shape[0] // 2
    parts = [slice(r0, r0 + half) for r0 in (0, half)]
    hn = [_rms(x_ref[rows, :], g_ref[...]).astype(BF16) for rows in parts]
    projs = [jnp.dot(t, w_ref[...], preferred_element_type=F32) for t in hn]
    bd = bd_ref[...]
    o = SSM_WIDTH
    for rows, proj in zip(parts, projs):
        for j in range(SSM_SLABS):
            u_ref[j, rows, :] = proj[:, j * LANES:(j + 1) * LANES]
        q = _head_norm(proj[:, o:o + ATT_WIDTH], qn_ref[...], bd) * (ATT_HEAD_DIM ** -0.5)
        k = _head_norm(proj[:, o + ATT_WIDTH:o + 2 * ATT_WIDTH], kn_ref[...], bd)
        v = proj[:, o + 2 * ATT_WIDTH:o + 3 * ATT_WIDTH]
        for j in range(ATT_WIDTH // LANES):
            q_ref[j, rows, :] = q[:, j * LANES:(j + 1) * LANES]
            k_ref[j, rows, :] = k[:, j * LANES:(j + 1) * LANES]
            v_ref[j, rows, :] = v[:, j * LANES:(j + 1) * LANES]


def _inproj_even(x2, gain, w_in, q_norm, k_norm):
    n_tok = x2.shape[0]
    n_slab = ATT_WIDTH // LANES
    head_of = jnp.arange(ATT_WIDTH) // ATT_HEAD_DIM
    bd = (head_of[:, None] == head_of[None, :]).astype(BF16)
    qn = jnp.tile(q_norm.astype(F32), ATT_HEADS)[None]
    kn = jnp.tile(k_norm.astype(F32), ATT_HEADS)[None]
    full = lambda shape: pl.BlockSpec(shape, lambda i: (0,) * len(shape))
    slab = pl.BlockSpec((n_slab, TOK_TILE, LANES), lambda i: (0, i, 0))
    slab_shape = jax.ShapeDtypeStruct((n_slab, n_tok, LANES), F32)
    return pl.pallas_call(
        _inproj_even_kernel,
        grid=(n_tok // TOK_TILE,),
        in_specs=[pl.BlockSpec((TOK_TILE, D_MODEL), lambda i: (i, 0)), full((1, D_MODEL)),
                  full(w_in.shape), full(bd.shape), full(qn.shape), full(kn.shape)],
        out_specs=[pl.BlockSpec((SSM_SLABS, TOK_TILE, LANES), lambda i: (0, i, 0)), slab, slab, slab],
        out_shape=[jax.ShapeDtypeStruct((SSM_SLABS, n_tok, LANES), F32), slab_shape, slab_shape, slab_shape],
        compiler_params=_cparams(("parallel",)),
        name="inproj_even",
    )(x2, gain[None].astype(F32), w_in.astype(BF16), bd, qn, kn)


def _s5_tables(a_re, a_im, b_re, b_im, c_re, c_im, d_skip, log_step):
    f = lambda t: t.astype(F32)
    a_re, a_im, b_re, b_im, c_re, c_im = map(f, (a_re, a_im, b_re, b_im, c_re, c_im))
    L = SSM_CHUNK
    step = jnp.exp(f(log_step))[:, None]
    ks = jnp.arange(L + 1, dtype=F32)[:, None, None]
    mag = jnp.exp(ks * (a_re * step)[None])
    ang = ks * (a_im * step)[None]
    pw_re, pw_im = mag * jnp.cos(ang), mag * jnp.sin(ang)
    nr, ni = pw_re[1] - 1.0, pw_im[1]
    den = a_re * a_re + a_im * a_im
    z_re, z_im = (nr * a_re + ni * a_im) / den, (ni * a_re - nr * a_im) / den
    bb_re = z_re[..., None] * b_re - z_im[..., None] * b_im
    bb_im = z_re[..., None] * b_im + z_im[..., None] * b_re
    lb_re = pw_re[..., None] * bb_re[None] - pw_im[..., None] * bb_im[None]
    lb_im = pw_re[..., None] * bb_im[None] + pw_im[..., None] * bb_re[None]
    kk = jnp.einsum('gop,kgpi->gkio', c_re, lb_re[:L]) - jnp.einsum('gop,kgpi->gkio', c_im, lb_im[:L])
    ti = jnp.arange(L)
    lag = ti[None, :] - ti[:, None]
    m = jnp.where((lag >= 0)[None, :, :, None, None], kk[:, jnp.maximum(lag, 0)], 0.0)
    ns, gs = SSM_SLABS, SSM_SLAB_GROUPS
    lw = L * LANES
    pm = m.reshape(ns, gs, L, L, SSM_GROUP, SSM_GROUP).transpose(0, 2, 1, 4, 3, 5).reshape(ns, lw, L * SSM_GROUP)
    fold_e = lambda t: t[:L][::-1].reshape(L, ns, gs, SSM_STATE, SSM_GROUP).transpose(1, 0, 2, 4, 3).reshape(ns, lw, SSM_STATE)
    pe = jnp.concatenate([fold_e(lb_re), fold_e(lb_im)], axis=-1)
    pw1_re, pw1_im = (t[1:].transpose(1, 0, 2)[:, :, None, :] for t in (pw_re, pw_im))
    cl_re = c_re[:, None] * pw1_re - c_im[:, None] * pw1_im
    cl_im = c_re[:, None] * pw1_im + c_im[:, None] * pw1_re
    fold_f = lambda t: t.reshape(ns, gs, L, SSM_GROUP, SSM_STATE).transpose(0, 4, 2, 1, 3).reshape(ns, SSM_STATE, lw)
    pf = jnp.concatenate([fold_f(cl_re), fold_f(-cl_im)], axis=1)
    per_chain = lambda t: jnp.tile(t.reshape(ns, gs * SSM_STATE // LANES, LANES), (1, 2, 1))
    d_vec = jnp.tile(f(d_skip).reshape(ns, 1, LANES), (1, 1, L))
    return pm.astype(BF16), pe.astype(BF16), pf.astype(BF16), per_chain(pw_re[L]), per_chain(pw_im[L]), d_vec


def _iota2(shape):
    return lax.broadcasted_iota(jnp.int32, shape, 0), lax.broadcasted_iota(jnp.int32, shape, 1)


def _widen(compact, group_major_cols, shape, r_shift, c_shift, sel_rows):
    gmask = SSM_SLAB_GROUPS - 1
    if sel_rows:
        r, c = _iota2((shape[0], compact.shape[0]))
        sel = ((r >> 9) == (c >> 6)) & ((r & (SSM_STATE - 1)) == (c & (SSM_STATE - 1)))
        wide = jnp.dot(sel.astype(BF16), compact, preferred_element_type=F32)
    else:
        r, c = _iota2((compact.shape[1], shape[1]))
        if group_major_cols:
            sel = ((c >> 9) == (r >> 6)) & ((c & (SSM_STATE - 1)) == (r & (SSM_STATE - 1)))
        else:
            sel = ((c >> 7) == (r >> 4)) & ((c & (SSM_GROUP - 1)) == (r & (SSM_GROUP - 1)))
        wide = jnp.dot(compact, sel.astype(BF16), preferred_element_type=F32)
    r, c = _iota2(shape)
    keep = ((r >> r_shift) & gmask) == ((c >> c_shift) & gmask)
    return jnp.where(keep, wide, 0.0).astype(BF16)


def _s5_kernel(u_ref, pm_ref, pe_ref, pf_ref, ar_ref, ai_ref, d_ref, y_ref, m_ref, e_ref, f_ref, x_ref, sr_ref, si_ref,
               *, n_chunk, pitch):
    L = SSM_CHUNK
    n_blk = SSM_SLAB_GROUPS * SSM_STATE // LANES
    n_re = n_blk * LANES
    lw = L * LANES
    m_ref[...] = _widen(pm_ref[...], False, (lw, lw), 4, 4, False)
    e_ref[...] = _widen(pe_ref[...], True, (lw, 2 * n_re), 4, 6, False)
    f_ref[...] = _widen(pf_ref[...], False, (2 * n_re, lw), 6, 4, True)
    tiles = [(b, c0) for b in range(2) for c0 in range(0, n_chunk, S5_ROW_TILE)]
    for b, c0 in tiles:
        r0 = b * n_chunk + c0
        for t in range(L):
            x_ref[r0:r0 + S5_ROW_TILE, t * LANES:(t + 1) * LANES] = (
                u_ref[pl.ds(r0 * L + t, S5_ROW_TILE, stride=L), :].astype(BF16))
        sl = jnp.dot(x_ref[r0:r0 + S5_ROW_TILE, :], e_ref[...], preferred_element_type=F32)
        for j in range(n_blk):
            base = (b * n_blk + j) * pitch + c0
            sr_ref[base:base + S5_ROW_TILE, :] = sl[:, j * LANES:(j + 1) * LANES]
            si_ref[base:base + S5_ROW_TILE, :] = sl[:, n_re + j * LANES:n_re + (j + 1) * LANES]
    ar, ai = ar_ref[...], ai_ref[...]
    half = LANES

    def scan_step(c, carry):
        s_re, s_im = carry
        rows = pl.ds(c, SUBLANES, stride=pitch)
        x_re, x_im = sr_ref[rows, :], si_ref[rows, :]
        sr_ref[rows, :] = s_re
        si_ref[rows, :] = s_im
        return ar * s_re - ai * s_im + x_re, ar * s_im + ai * s_re + x_im

    zero = jnp.zeros((SUBLANES, half), F32)
    lax.fori_loop(0, n_chunk, scan_step, (zero, zero), unroll=8)

    for b, c0 in tiles:
        r0 = b * n_chunk + c0
        chain = lambda ref, j: ref[(b * n_blk + j) * pitch + c0:(b * n_blk + j) * pitch + c0 + S5_ROW_TILE, :]
        sp = jnp.concatenate([chain(sr_ref, j) for j in range(n_blk)] + [chain(si_ref, j) for j in range(n_blk)],
                             axis=1).astype(BF16)
        xt = x_ref[r0:r0 + S5_ROW_TILE, :]
        y = (jnp.dot(xt, m_ref[...], preferred_element_type=F32)
             + jnp.dot(sp, f_ref[...], preferred_element_type=F32)
             + d_ref[...] * xt.astype(F32))
        for t in range(L):
            y_ref[pl.ds(r0 * L + t, S5_ROW_TILE, stride=L), :] = y[:, t * LANES:(t + 1) * LANES]


def _s5_core(u, tables, bsz, s_len):
    assert bsz == 2, "the scan packs (batch, lane block) into the 8 sublanes of one vreg"
    pm, pe, pf, a_r, a_i, d_vec = tables
    n_tok = bsz * s_len
    n_chunk = s_len // SSM_CHUNK
    pitch = n_chunk + SUBLANES
    lw = SSM_CHUNK * LANES
    n_state = 2 * SSM_SLAB_GROUPS * SSM_STATE
    slab = lambda shape, **kw: pl.BlockSpec((None,) + shape, lambda i: (i,) + (0,) * len(shape), **kw)
    once = dict(pipeline_mode=pl.Buffered(1))
    return pl.pallas_call(
        functools.partial(_s5_kernel, n_chunk=n_chunk, pitch=pitch),
        grid=(SSM_SLABS,),
        in_specs=[slab((n_tok, LANES), **once), slab(pm.shape[1:]), slab(pe.shape[1:]), slab(pf.shape[1:]),
                  slab((SUBLANES, LANES)), slab((SUBLANES, LANES)), slab((1, lw))],
        out_specs=slab((n_tok, LANES), **once),
        out_shape=jax.ShapeDtypeStruct((SSM_SLABS, n_tok, LANES), F32),
        scratch_shapes=[pltpu.VMEM((lw, lw), BF16), pltpu.VMEM((lw, n_state), BF16), pltpu.VMEM((n_state, lw), BF16),
                        pltpu.VMEM((bsz * n_chunk, lw), BF16),
                        pltpu.VMEM((SUBLANES * pitch, LANES), F32),
                        pltpu.VMEM((SUBLANES * pitch, LANES), F32)],
        compiler_params=_cparams(("parallel",)),
        name="s5_core",
    )(u, pm, pe, pf, a_r, a_i, d_vec)


def _attn_kernel(slope_ref, q_ref, kp_ref, kc_ref, vp_ref, vc_ref, o_ref,
                 q4_ref, k4_ref, v4_ref, k1_ref, v1_ref, m_ref, l_ref, acc_ref):
    slab = pl.program_id(1)
    sb = pl.program_id(2)
    fine = DILATIONS[1]
    nq, nk = ATT_SB // fine, 2 * ATT_SB // fine
    for r in range(fine):
        q4_ref[r * nq:(r + 1) * nq, :] = q_ref[pl.ds(r, nq, stride=fine), :]
        for dst, prev, cur in ((k4_ref, kp_ref, kc_ref), (v4_ref, vp_ref, vc_ref)):
            dst[r * nk:r * nk + nq, :] = prev[pl.ds(r, nq, stride=fine), :]
            dst[r * nk + nq:(r + 1) * nk, :] = cur[pl.ds(r, nq, stride=fine), :]
    for dst, prev, cur in ((k1_ref, kp_ref, kc_ref), (v1_ref, vp_ref, vc_ref)):
        dst[0:ATT_BLK, :] = prev[ATT_SB - ATT_BLK:ATT_SB, :]
        dst[ATT_BLK:ATT_BLK + ATT_SB, :] = cur[...]

    lane = lax.broadcasted_iota(jnp.int32, (ATT_BLK, LANES), 1)
    head0 = lane < ATT_HEAD_DIM
    qi = lax.broadcasted_iota(jnp.int32, (ATT_BLK, 2 * ATT_BLK), 0)
    kj = lax.broadcasted_iota(jnp.int32, (ATT_BLK, 2 * ATT_BLK), 1)
    back = qi + ATT_BLK - kj
    band = (back >= 0) & (back <= ATT_BLK)
    neg_steps = -back.astype(F32)
    slopes = (slope_ref[2 * slab], slope_ref[2 * slab + 1])

    for pat, dil in enumerate(DILATIONS):
        span = ATT_BLK * dil

        def tile(idx, carry, dil=dil, span=span, pat=pat):
            start = (idx // dil) * span + idx % dil
            seq_ok = jnp.logical_or(sb > 0, idx >= dil)
            valid = band & ((kj >= ATT_BLK) | seq_ok)
            if dil == 1:
                start = pl.multiple_of(start, ATT_BLK)
                rows = pl.ds(start, ATT_BLK)
                qt, k_src, v_src, k_rows = q_ref[rows, :], k1_ref, v1_ref, pl.ds(start, 2 * ATT_BLK)
            elif dil == fine:
                first = pl.multiple_of((idx // fine) * ATT_BLK, ATT_BLK)
                rows = pl.ds(start, ATT_BLK, stride=dil)
                qt = q4_ref[pl.ds((idx % fine) * nq + first, ATT_BLK), :]
                k_src, v_src = k4_ref, v4_ref
                k_rows = pl.ds((idx % fine) * nk + (nq - ATT_BLK) + first, 2 * ATT_BLK)
            else:
                rows = pl.ds(start, ATT_BLK, stride=dil)
                qt = q4_ref[pl.ds((idx % fine) * nq + idx // fine, ATT_BLK, stride=fine), :]
                k_src, v_src = k4_ref, v4_ref
                k_rows = pl.ds((idx % fine) * nk + idx // fine, 2 * ATT_BLK, stride=fine)
            kt = k_src[k_rows, :].astype(BF16)
            vt = v_src[k_rows, :].astype(BF16)
            q0 = jnp.where(head0, qt, 0.0)
            scores = [lax.dot_general(qh.astype(BF16), kt, (((1,), (1,)), ((), ())), preferred_element_type=F32)
                      for qh in (q0, qt - q0)]
            return rows, valid, vt, scores

        def softmax(valid, s, hh, dil=dil):
            s = jnp.where(valid, s + (slopes[hh] * float(dil)) * neg_steps, NEG_INF)
            m = jnp.max(s, axis=-1, keepdims=True)
            p = jnp.exp(s - m)
            return m, jnp.sum(p, axis=-1, keepdims=True), p.astype(BF16)

        def merge(rows, parts, pat=pat):
            (m0, l0, o0), (m1, l1, o1) = parts
            m_t = jnp.where(head0, m0, m1)
            l_t = jnp.where(head0, l0, l1)
            o_t = jnp.where(head0, o0, o1)
            if pat == 0:
                m_ref[rows, :] = m_t
                l_ref[rows, :] = l_t
                acc_ref[rows, :] = o_t
            else:
                m_old = m_ref[rows, :]
                m_new = jnp.maximum(m_old, m_t)
                a = jnp.exp(m_old - m_new)
                b = jnp.exp(m_t - m_new)
                m_ref[rows, :] = m_new
                l_ref[rows, :] = a * l_ref[rows, :] + b * l_t
                acc_ref[rows, :] = a * acc_ref[rows, :] + b * o_t

        def group(gi, carry):
            tiles = [tile(gi * ATT_GROUP + t, None) for t in range(ATT_GROUP)]
            soft = [[softmax(valid, s, hh) for hh, s in enumerate(scores)] for _, valid, _, scores in tiles]
            outs = [[(m, l, jnp.dot(p, vt, preferred_element_type=F32)) for m, l, p in per_head]
                    for (_, _, vt, _), per_head in zip(tiles, soft)]
            for (rows, _, _, _), parts in zip(tiles, outs):
                merge(rows, parts)
            return carry

        lax.fori_loop(0, ATT_SB // ATT_BLK // ATT_GROUP, group, 0)

    o_ref[...] = (acc_ref[...] / l_ref[...]).astype(o_ref.dtype)


def _dilated_attention(q, k, v, bsz, s_len):
    assert DILATIONS == (DILATIONS[1] ** 2, DILATIONS[1], 1) and ATT_SB == ATT_BLK * DILATIONS[0]
    n_slab = q.shape[0]
    shape4 = (n_slab, bsz, s_len, LANES)
    q, k, v = (t.reshape(shape4) for t in (q, k, v))
    slopes = jnp.asarray([2.0 ** (-8.0 * (h + 1) / ATT_HEADS) for h in range(ATT_HEADS)], F32)
    blk = (None, None, ATT_SB, LANES)
    cur = pl.BlockSpec(blk, lambda b, j, i, s: (j, b, i, 0))
    prev = pl.BlockSpec(blk, lambda b, j, i, s: (j, b, jnp.maximum(i - 1, 0), 0))
    out = pl.pallas_call(
        _attn_kernel,
        grid_spec=pltpu.PrefetchScalarGridSpec(
            num_scalar_prefetch=1,
            grid=(bsz, n_slab, s_len // ATT_SB),
            in_specs=[cur, prev, cur, prev, cur],
            out_specs=cur,
            scratch_shapes=[pltpu.VMEM((ATT_SB, LANES), F32),
                            pltpu.VMEM((2 * ATT_SB, LANES), F32), pltpu.VMEM((2 * ATT_SB, LANES), F32),
                            pltpu.VMEM((ATT_BLK + ATT_SB, LANES), F32), pltpu.VMEM((ATT_BLK + ATT_SB, LANES), F32),
                            pltpu.VMEM((ATT_SB, LANES), F32), pltpu.VMEM((ATT_SB, LANES), F32),
                            pltpu.VMEM((ATT_SB, LANES), F32)]),
        out_shape=jax.ShapeDtypeStruct(shape4, BF16),
        compiler_params=_cparams(("parallel", "parallel", "parallel")),
        name="dilated_attn",
    )(slopes, q, k, k, v, v)
    return out.reshape(n_slab, bsz * s_len, LANES)


def _route_epilogue(h, gain_ref, wr_ref, br_ref, tri_ref, cnt_ref, h_ref, hn_ref, ids_ref, gate_ref, cnt_out_ref):
    tm = h.shape[0]
    h_ref[...] = h
    hn = _rms(h, gain_ref[...])
    _store_row_tiles(hn_ref, hn)
    hn_hi = hn.astype(BF16)
    hn_lo = (hn - hn_hi.astype(F32)).astype(BF16)
    hi_part = jnp.dot(hn_hi, wr_ref[...], preferred_element_type=F32)
    logits = (hi_part[:, :LANES] + (hi_part[:, LANES:]
                                    + jnp.dot(hn_lo, wr_ref[:, :LANES], preferred_element_type=F32))) + br_ref[...]
    lane = lax.broadcasted_iota(jnp.int32, (tm, LANES), 1)
    big = jnp.int32(LANES)
    rmax = lambda t: jnp.max(t, axis=-1, keepdims=True)
    rmin = lambda t: jnp.min(t, axis=-1, keepdims=True)
    rsum = lambda t: jnp.sum(t, axis=-1, keepdims=True)
    gmask = lane < N_GROUPS
    gl = jnp.where(gmask, logits, -jnp.inf)
    gmax = rmax(gl)
    ge = jnp.where(gmask, jnp.exp(gl - gmax), 0.0)
    gprob = ge / rsum(ge)
    g_w = rmax(gprob)
    grp = rmin(jnp.where(gmask & (gprob == g_w), lane, big))
    group_of_lane = (lane - N_GROUPS) >> int(math.log2(EXPERTS_PER_GROUP))
    emask = (lane >= N_GROUPS) & (lane < N_GROUPS + N_EXPERTS) & (group_of_lane == grp)
    el = jnp.where(emask, logits, -jnp.inf)
    ee = jnp.where(emask, jnp.exp(el - rmax(el)), 0.0)
    ep = jnp.where(emask, ee / rsum(ee), -1.0)
    p1 = rmax(ep)
    i1 = rmin(jnp.where(ep == p1, lane, big))
    ep2 = jnp.where(lane == i1, -1.0, ep)
    p2 = rmax(ep2)
    i2 = rmin(jnp.where(ep2 == p2, lane, big))
    e1, e2 = i1 - N_GROUPS, i2 - N_GROUPS
    psum = p1 + p2
    gate1, gate2 = g_w * p1 / psum, g_w * p2 / psum
    oh1, oh2 = lane == e1, lane == e2
    member = (oh1 | oh2).astype(BF16)
    before = jnp.dot(tri_ref[...], member, preferred_element_type=F32) + cnt_ref[...]
    r1 = rsum(jnp.where(oh1, before, 0.0)).astype(jnp.int32)
    r2 = rsum(jnp.where(oh2, before, 0.0)).astype(jnp.int32)
    cnt_ref[...] = cnt_ref[...] + jnp.sum(member.astype(F32), axis=0, keepdims=True)
    ids = jnp.where(lane == 0, e1, jnp.where(lane == 1, e2, jnp.where(lane == 2, r1, jnp.where(lane == 3, r2, 0))))
    ids_ref[...] = jnp.transpose(ids)[0:SUBLANES, :]
    gate_ref[...] = jnp.where(lane == 0, gate1, jnp.where(lane == 1, gate2, 0.0))
    cnt_out_ref[...] = jnp.broadcast_to(cnt_ref[...], cnt_out_ref.shape)


def _router_operands(norm_gain, w_rg, b_rg, w_re, b_re):
    pad = LANES - N_GROUPS - N_EXPERTS
    wr = jnp.pad(jnp.concatenate([w_rg, w_re], axis=1).astype(F32), ((0, 0), (0, pad)))
    br = jnp.pad(jnp.concatenate([b_rg, b_re]).astype(F32), (0, pad))[None]
    r = jnp.arange(TOK_TILE)
    tri = (r[None, :] < r[:, None]).astype(BF16)
    wr_hi = wr.astype(BF16)
    wr_lo = (wr - wr_hi.astype(F32)).astype(BF16)
    return norm_gain[None].astype(F32), jnp.concatenate([wr_hi, wr_lo], axis=1), br, tri


def _route_specs(n_tok):
    full = lambda shape: pl.BlockSpec(shape, lambda i: (0,) * len(shape))
    in_specs = [full((1, D_MODEL)), full((D_MODEL, 2 * LANES)), full((1, LANES)), full((TOK_TILE, TOK_TILE))]
    tok = lambda w: pl.BlockSpec((TOK_TILE, w), lambda i: (i, 0))
    out_specs = [tok(D_MODEL), pl.BlockSpec((TOK_TILE * ROW_CHUNKS, LANES), lambda i: (i, 0)),
                 pl.BlockSpec((SUBLANES, TOK_TILE), lambda i: (0, i)), tok(LANES), full((SUBLANES, LANES))]
    out_shape = [jax.ShapeDtypeStruct((n_tok, D_MODEL), F32), jax.ShapeDtypeStruct((n_tok * ROW_CHUNKS, LANES), jnp.uint32),
                 jax.ShapeDtypeStruct((SUBLANES, n_tok), jnp.int32), jax.ShapeDtypeStruct((n_tok, LANES), F32),
                 jax.ShapeDtypeStruct((SUBLANES, LANES), F32)]
    return in_specs, out_specs, out_shape


def _gelu_tanh(x):
    return 0.5 * x * (1.0 + jnp.tanh(math.sqrt(2.0 / math.pi) * (x + 0.044715 * (x * x * x))))


def _outproj_even_kernel(x_ref, y_ref, a_ref, wglu_ref, bglu_ref, wout_ref, gain_ref, wr_ref, br_ref, tri_ref,
                         h_ref, hn_ref, ids_ref, gate_ref, cnt_out_ref, cnt_ref):
    @pl.when(pl.program_id(0) == 0)
    def _():
        cnt_ref[...] = jnp.zeros_like(cnt_ref)

    half = x_ref.shape[0] // 2
    parts = [slice(r0, r0 + half) for r0 in (0, half)]
    y = [_gelu_tanh(jnp.concatenate([y_ref[j, rows, :] for j in range(SSM_SLABS)], axis=1)) for rows in parts]
    glu = [jnp.dot(t.astype(BF16), wglu_ref[...], preferred_element_type=F32) for t in y]
    h_out = []
    for rows, t, g in zip(parts, y, glu):
        t = t * jax.nn.sigmoid(g + bglu_ref[...])
        mix = jnp.dot(t.astype(BF16), wout_ref[0:SSM_WIDTH, :], preferred_element_type=F32)
        for j in range(ATT_WIDTH // LANES):
            w_rows = slice(SSM_WIDTH + j * LANES, SSM_WIDTH + (j + 1) * LANES)
            mix = mix + jnp.dot(a_ref[j, rows, :], wout_ref[w_rows, :], preferred_element_type=F32)
        h_out.append(x_ref[rows, :] + mix)
    _route_epilogue(jnp.concatenate(h_out, axis=0), gain_ref, wr_ref, br_ref, tri_ref, cnt_ref,
                    h_ref, hn_ref, ids_ref, gate_ref, cnt_out_ref)


def _outproj_even(x2, y_pre, attn, w_glu, b_glu, w_out, route_ops):
    n_tok = x2.shape[0]
    n_slab = attn.shape[0]
    r_in, r_out, r_shape = _route_specs(n_tok)
    full = lambda shape: pl.BlockSpec(shape, lambda i: (0,) * len(shape))
    return pl.pallas_call(
        _outproj_even_kernel,
        grid=(n_tok // TOK_TILE,),
        in_specs=[pl.BlockSpec((TOK_TILE, D_MODEL), lambda i: (i, 0)),
                  pl.BlockSpec((SSM_SLABS, TOK_TILE, LANES), lambda i: (0, i, 0)),
                  pl.BlockSpec((n_slab, TOK_TILE, LANES), lambda i: (0, i, 0)),
                  full(w_glu.shape), full((1, SSM_WIDTH)), full(w_out.shape)] + r_in,
        out_specs=r_out, out_shape=r_shape,
        scratch_shapes=[pltpu.VMEM((1, LANES), F32)],
        compiler_params=_cparams(("arbitrary",)),
        name="outproj_even",
    )(x2, y_pre, attn, w_glu.astype(BF16), b_glu[None].astype(F32), w_out.astype(BF16), *route_ops)


def _conv_layer_kernel(h_ref, pgate_ref, y0_ref, y1_ref, gmix_ref, win_ref, cw_ref, wout_ref, gain_ref, wr_ref, br_ref,
                       tri_ref, ho_ref, hn_ref, ids_ref, gate_ref, cnt_out_ref, cnt_ref, zc_ref, *, tiles_per_seq):
    i = pl.program_id(0)

    @pl.when(i == 0)
    def _():
        cnt_ref[...] = jnp.zeros_like(cnt_ref)

    @pl.when(i % tiles_per_seq == 0)
    def _():
        zc_ref[0:SUBLANES, :] = jnp.zeros((SUBLANES, D_MODEL), F32)

    tm = h_ref.shape[0]
    c = D_MODEL
    half = tm // 2
    parts = (0, half)
    h_in, hn = {}, {}
    for r0 in parts:
        rows = slice(r0, r0 + half)
        pgate = pgate_ref[rows, :]
        h_in[r0] = (h_ref[rows, :] + pgate[:, 0:1] * _load_row_tiles(y0_ref, first=r0, n=half)
                    + pgate[:, 1:2] * _load_row_tiles(y1_ref, first=r0, n=half))
        hn[r0] = _rms(h_in[r0], gmix_ref[...]).astype(BF16)
    b_gate, zc = {}, {}
    for r0 in parts:
        b_gate[r0] = jnp.dot(hn[r0], win_ref[:, 0:c], preferred_element_type=F32)
        zc[r0] = (jnp.dot(hn[r0], win_ref[:, c:2 * c], preferred_element_type=F32)
                  * jnp.dot(hn[r0], win_ref[:, 2 * c:3 * c], preferred_element_type=F32))
    h_out = []
    for r0 in parts:
        z0 = SUBLANES + r0
        zc_ref[z0:z0 + half, :] = zc[r0]
        conv = cw_ref[CONV_TAPS - 1:CONV_TAPS, :] * zc[r0]
        for back in range(1, CONV_TAPS):
            tap = CONV_TAPS - 1 - back
            conv = conv + cw_ref[tap:tap + 1, :] * zc_ref[z0 - back:z0 - back + half, :]
        mix = jnp.dot((b_gate[r0] * conv).astype(BF16), wout_ref[...], preferred_element_type=F32)
        h_out.append(h_in[r0] + mix)
    zc_ref[0:SUBLANES, :] = zc_ref[tm:tm + SUBLANES, :]
    _route_epilogue(jnp.concatenate(h_out, axis=0), gain_ref, wr_ref, br_ref, tri_ref, cnt_ref,
                    ho_ref, hn_ref, ids_ref, gate_ref, cnt_out_ref)


def _conv_layer(h1, pgate, y2, gain_mix, w_in, conv_w, w_out, route_ops, s_len):
    n_tok = h1.shape[0]
    r_in, r_out, r_shape = _route_specs(n_tok)
    full = lambda shape: pl.BlockSpec(shape, lambda i: (0,) * len(shape))
    return pl.pallas_call(
        functools.partial(_conv_layer_kernel, tiles_per_seq=s_len // TOK_TILE),
        grid=(n_tok // TOK_TILE,),
        in_specs=_moe_specs(n_tok) + [full((1, D_MODEL)), full(w_in.shape), full(conv_w.shape), full(w_out.shape)] + r_in,
        out_specs=r_out, out_shape=r_shape,
        scratch_shapes=[pltpu.VMEM((1, LANES), F32), pltpu.VMEM((TOK_TILE + SUBLANES, D_MODEL), F32)],
        compiler_params=_cparams(("arbitrary",)),
        name="conv_layer",
    )(h1, pgate, y2, y2, gain_mix[None].astype(F32), w_in.astype(BF16), conv_w.astype(F32), w_out.astype(BF16),
      *route_ops)


def _row_map(experts, ranks, first_row, n_rows):
    n_slot, n_tok = len(experts), experts[0].shape[0]
    n_assign = n_slot * n_tok
    n_src = n_rows + 2 * MOE_BLK
    mesh = plsc.VectorSubcoreMesh(core_axis_name="core", subcore_axis_name="subcore",
                                  num_cores=SC_CORES, num_subcores=SC_SUBCORES)
    tok_vec = pltpu.VMEM((n_tok,), jnp.int32)

    @functools.partial(
        pl.kernel, mesh=mesh, out_type=jax.ShapeDtypeStruct((n_src,), jnp.int32),
        scratch_types=[tok_vec] * (2 * n_slot) + [pltpu.VMEM((N_EXPERTS,), jnp.int32), pltpu.VMEM((n_src,), jnp.int32)],
        compiler_params=pltpu.CompilerParams(needs_layout_passes=False), name="moe_row_map")
    def row_map(*refs):
        ins, src_hbm, scratch = refs[:2 * n_slot + 1], refs[2 * n_slot + 1], refs[2 * n_slot + 2:]
        first_v, src_v = scratch[2 * n_slot], scratch[2 * n_slot + 1]

        @pl.when(jnp.logical_and(lax.axis_index("core") == 0, lax.axis_index("subcore") == 0))
        def _():
            for hbm, vmem in zip(ins, scratch):
                pltpu.sync_copy(hbm, vmem)
            lanes = lax.iota(jnp.int32, SC_LANES)

            @pl.loop(0, n_src, step=SC_LANES)
            def _(i):
                src_v[pl.ds(i, SC_LANES)] = n_assign + ((i + lanes) & (MOE_BLK - 1))

            for slot in range(n_slot):
                e_v, r_v = scratch[slot], scratch[n_slot + slot]

                @pl.loop(0, n_tok, step=SC_LANES)
                def _(t):
                    row = plsc.load_gather(first_v, [e_v[pl.ds(t, SC_LANES)]]) + r_v[pl.ds(t, SC_LANES)]
                    plsc.store_scatter(src_v, [row + MOE_BLK], slot * n_tok + t + lanes)

            pltpu.sync_copy(src_v, src_hbm)

    return row_map(*experts, *ranks, first_row)


def _expert_kernel(blk_e_ref, n_used_ref, src_ref, next_e_ref, wslot_ref, hn_ref, wg_ref, wu_ref, wd_ref, y2_ref,
                   xbuf, ybuf, wg_f, wu_f, wd_f, wg_s, wu_s, wd_s, ssem, wsem, *, n_tok, layer):
    b = pl.program_id(0)
    n_used = n_used_ref[0]
    cur = b % 2
    nxt = 1 - cur

    tile = lambda i: pl.ds(i * ROW_CHUNKS, ROW_CHUNKS)

    def scatter(blk, slot, i):
        row = src_ref[(blk + 1) * MOE_BLK + i]
        return pltpu.make_async_copy(ybuf.at[slot, tile(i)], y2_ref.at[row], ssem.at[slot])

    wait_block = lambda slot: pltpu.make_async_copy(ybuf.at[slot], ybuf.at[slot], ssem.at[slot]).wait()

    def weights(expert, slot, act):
        for hbm, buf in ((wg_ref, wg_f), (wu_ref, wu_f), (wd_ref, wd_f)):
            act(pltpu.make_async_copy(hbm.at[layer, expert], buf.at[slot], wsem.at[slot]))

    @pl.when(b == 0)
    def _():
        ybuf[1] = jnp.zeros(ybuf.shape[1:], ybuf.dtype)
        weights(blk_e_ref[0], wslot_ref[0], lambda c: c.start())

    @pl.when(b < n_used)
    def _():
        new_expert = jnp.logical_or(b == 0, blk_e_ref[b] != blk_e_ref[jnp.maximum(b - 1, 0)])

        @pl.when(new_expert)
        def _():
            slot = wslot_ref[b]
            weights(blk_e_ref[b], slot, lambda c: c.wait())
            wg_s[...] = wg_f[slot].astype(BF16)
            wu_s[...] = wu_f[slot].astype(BF16)
            wd_s[...] = wd_f[slot].astype(BF16)

            @pl.when(next_e_ref[b] >= 0)
            def _():
                weights(next_e_ref[b], 1 - slot, lambda c: c.start())

        @pl.when(b >= 1)
        def _():
            wait_block(cur)

        for i in range(MOE_BLK):
            tok = src_ref[(b + 1) * MOE_BLK + i] & (n_tok - 1)
            xbuf[tile(i), :] = hn_ref[pl.ds(pl.multiple_of(tok * ROW_CHUNKS, ROW_CHUNKS), ROW_CHUNKS), :]
            scatter(b - 1, nxt, i).start(priority=i % 2)
        x = _load_row_tiles(xbuf).astype(BF16)
        g = jnp.dot(x, wg_s[...], preferred_element_type=F32)
        u = jnp.dot(x, wu_s[...], preferred_element_type=F32)
        hb = (g * jax.nn.sigmoid(g) * u).astype(BF16)
        _store_row_tiles(ybuf, jnp.dot(hb, wd_s[...], preferred_element_type=F32), (cur,))

    @pl.when(b == n_used)
    def _():
        wait_block(cur)
        for i in range(MOE_BLK):
            scatter(b - 1, nxt, i).start(priority=i % 2)
        wait_block(nxt)


def _expert_ffn(hn, src, blk_e, n_used, next_e, wslot, w_g, w_u, w_d, layer):
    n_tok = hn.shape[0] // ROW_CHUNKS
    assert n_tok & (n_tok - 1) == 0, "dump-row aliasing masks the token index with T - 1"
    n_blocks = (src.shape[0] - 2 * MOE_BLK) // MOE_BLK
    in_hbm = pl.BlockSpec(memory_space=pltpu.HBM)
    return pl.pallas_call(
        functools.partial(_expert_kernel, n_tok=n_tok, layer=layer),
        grid_spec=pltpu.PrefetchScalarGridSpec(
            num_scalar_prefetch=5,
            grid=(n_blocks + 1,),
            in_specs=[pl.BlockSpec(hn.shape, lambda b, *_: (0, 0), pipeline_mode=pl.Buffered(1)),
                      in_hbm, in_hbm, in_hbm],
            out_specs=pl.BlockSpec(memory_space=pltpu.HBM),
            scratch_shapes=[pltpu.VMEM((MOE_BLK * ROW_CHUNKS, LANES), jnp.uint32),
                            pltpu.VMEM((2, MOE_BLK * ROW_CHUNKS, LANES), jnp.uint32),
                            pltpu.VMEM((2, D_MODEL, D_EXPERT), F32), pltpu.VMEM((2, D_MODEL, D_EXPERT), F32),
                            pltpu.VMEM((2, D_EXPERT, D_MODEL), F32),
                            pltpu.VMEM((D_MODEL, D_EXPERT), BF16), pltpu.VMEM((D_MODEL, D_EXPERT), BF16),
                            pltpu.VMEM((D_EXPERT, D_MODEL), BF16),
                            pltpu.SemaphoreType.DMA((2,)), pltpu.SemaphoreType.DMA((2,))]),
        out_shape=jax.ShapeDtypeStruct((2 * n_tok + MOE_BLK, ROW_CHUNKS, LANES), jnp.uint32),
        compiler_params=_cparams(("arbitrary",)),
        name="moe_experts",
    )(blk_e, n_used, src, next_e, wslot, hn, w_g, w_u, w_d)


def _moe_add_kernel(h_ref, gate_ref, y0_ref, y1_ref, o_ref):
    gate = gate_ref[...]
    o_ref[...] = h_ref[...] + gate[:, 0:1] * _load_row_tiles(y0_ref) + gate[:, 1:2] * _load_row_tiles(y1_ref)


def _moe_specs(n_tok):
    slots = n_tok // TOK_TILE
    y2_rows = (TOK_TILE * ROW_CHUNKS, LANES)
    return [pl.BlockSpec((TOK_TILE, D_MODEL), lambda i: (i, 0)), pl.BlockSpec((TOK_TILE, LANES), lambda i: (i, 0)),
            pl.BlockSpec(y2_rows, lambda i: (i, 0)), pl.BlockSpec(y2_rows, lambda i: (slots + i, 0))]


def _moe_add(h, gate, y2):
    n_tok = h.shape[0]
    return pl.pallas_call(
        _moe_add_kernel,
        grid=(n_tok // TOK_TILE,),
        in_specs=_moe_specs(n_tok),
        out_specs=pl.BlockSpec((TOK_TILE, D_MODEL), lambda i: (i, 0)),
        out_shape=jax.ShapeDtypeStruct((n_tok, D_MODEL), F32),
        compiler_params=_cparams(("parallel",)),
        name="moe_add",
    )(h, gate, y2, y2)


def _moe(hn, ids, counts, w_g, w_u, w_d, layer):
    n_tok = hn.shape[0] // ROW_CHUNKS
    n_assign = 2 * n_tok
    n_blocks = n_assign // MOE_BLK + N_EXPERTS
    n_rows = n_blocks * MOE_BLK
    cnt = counts[0, :N_EXPERTS].astype(jnp.int32)
    padded = (cnt + MOE_BLK - 1) // MOE_BLK * MOE_BLK
    pends = jnp.cumsum(padded).astype(jnp.int32)
    pstarts = pends - padded
    blk_start = jnp.arange(n_blocks + 1, dtype=jnp.int32) * MOE_BLK
    blk_e = jnp.minimum(jnp.sum(pends[None, :] <= blk_start[:, None], axis=1), N_EXPERTS - 1).astype(jnp.int32)
    n_used = (pends[-1:] // MOE_BLK).astype(jnp.int32)
    used = (padded > 0)[None, :]
    e_idx = jnp.arange(N_EXPERTS, dtype=jnp.int32)[None, :]
    next_e = jnp.min(jnp.where(used & (e_idx > blk_e[:, None]), e_idx, N_EXPERTS), axis=1)
    next_e = jnp.where(next_e < N_EXPERTS, next_e, -1).astype(jnp.int32)
    ordinal = jnp.sum((used & (e_idx <= blk_e[:, None])).astype(jnp.int32), axis=1) - 1
    src = _row_map((ids[0], ids[1]), (ids[2], ids[3]), pstarts, n_rows)
    y2 = _expert_ffn(hn, src, blk_e, n_used, next_e, (ordinal % 2).astype(jnp.int32), w_g, w_u, w_d, layer)
    return y2.reshape(-1, LANES)


def kernel(x, norm_mix, norm_ffn, w_in_even, ssm_a_re, ssm_a_im, ssm_b_re, ssm_b_im, ssm_c_re, ssm_c_im, ssm_d,
           ssm_log_step, w_glu, b_glu, q_norm, k_norm, w_out_even, w_in_conv, conv_w, w_out_conv, w_router_group,
           b_router_group, w_router_expert, b_router_expert, w_expert_gate, w_expert_up, w_expert_down):
    bsz, s_len, d = x.shape
    x2 = x.reshape(bsz * s_len, d)
    route = lambda layer: _router_operands(norm_ffn[layer], w_router_group[layer], b_router_group[layer],
                                           w_router_expert[layer], b_router_expert[layer])
    experts = lambda layer: (w_expert_gate, w_expert_up, w_expert_down, layer)

    u, q, k, v = _inproj_even(x2, norm_mix[0], w_in_even[0], q_norm[0], k_norm[0])
    tables = _s5_tables(ssm_a_re[0], ssm_a_im[0], ssm_b_re[0], ssm_b_im[0], ssm_c_re[0], ssm_c_im[0], ssm_d[0],
                        ssm_log_step[0])
    y_pre = _s5_core(u, tables, bsz, s_len)
    attn = _dilated_attention(q, k, v, bsz, s_len)
    h, hn, ids, gate, counts = _outproj_even(x2, y_pre, attn, w_glu[0], b_glu[0], w_out_even[0], route(0))
    y2 = _moe(hn, ids, counts, *experts(0))

    h, hn, ids, gate, counts = _conv_layer(h, gate, y2, norm_mix[1], w_in_conv[0], conv_w[0], w_out_conv[0],
                                           route(1), s_len)
    y2 = _moe(hn, ids, counts, *experts(1))
    return _moe_add(h, gate, y2).reshape(bsz, s_len, d)
```

```python
import functools
import math

import jax
import jax.numpy as jnp
from jax import lax
from jax.experimental import pallas as pl
from jax.experimental.pallas import tpu as pltpu
from jax.experimental.pallas import tpu_sc as plsc

F32 = jnp.float32
BF16 = jnp.bfloat16

D_MODEL = 1024
SSM_GROUP = 16
SSM_GROUPS = 40
SSM_WIDTH = SSM_GROUP * SSM_GROUPS
SSM_STATE = 64
ATT_HEADS = 6
ATT_HEAD_DIM = 64
ATT_WIDTH = ATT_HEADS * ATT_HEAD_DIM
DILATIONS = (16, 4, 1)
ATT_BLK = 128
CONV_TAPS = 3
N_GROUPS = 4
EXPERTS_PER_GROUP = 8
N_EXPERTS = N_GROUPS * EXPERTS_PER_GROUP
D_EXPERT = 512
MOE_BLK = 256
RMS_EPS = 1e-6
NEG_INF = -1e30

LANES = 128
SUBLANES = 8
SC_CORES, SC_SUBCORES, SC_LANES = 2, 16, 16
VMEM_LIMIT = 56 * 1024 * 1024

TOK_TILE = 512
ROW_GROUPS = 2
SSM_CHUNK = 8
SSM_SLAB_GROUPS = LANES // SSM_GROUP
SSM_SLABS = SSM_WIDTH // LANES
S5_ROW_TILE = 256
ATT_SB = 2048
ATT_GROUP = 4
ROW_CHUNKS = D_MODEL // (2 * LANES)


def _cparams(sem):
    return pltpu.CompilerParams(dimension_semantics=sem, vmem_limit_bytes=VMEM_LIMIT)


def _rms(x, gain):
    return x * lax.rsqrt(jnp.mean(x * x, axis=-1, keepdims=True) + RMS_EPS) * gain


HIGH_HALF = 0xFFFF0000


def _load_row_tiles(ref, lead=(), first=0, n=None):
    n = ref.shape[-2] // ROW_CHUNKS if n is None else n
    bits = jnp.concatenate([ref[lead + (pl.ds(first * ROW_CHUNKS + c, n, stride=ROW_CHUNKS), slice(None))]
                            for c in range(ROW_CHUNKS)], axis=1)
    low = lax.bitcast_convert_type(bits << 16, F32)
    high = lax.bitcast_convert_type(bits & jnp.uint32(HIGH_HALF), F32)
    return jnp.concatenate([low, high], axis=1)


def _store_row_tiles(ref, value, lead=(), first=0):
    n, half = value.shape[0], value.shape[1] // 2
    bf16_bits = lambda t: lax.bitcast_convert_type(t.astype(BF16).astype(F32), jnp.uint32)
    bits = (bf16_bits(value[:, :half]) >> 16) | (bf16_bits(value[:, half:]) & jnp.uint32(HIGH_HALF))
    for c in range(ROW_CHUNKS):
        rows = pl.ds(first * ROW_CHUNKS + c, n, stride=ROW_CHUNKS)
        ref[lead + (rows, slice(None))] = bits[:, c * LANES:(c + 1) * LANES]


def _head_norm(t, gain, bd):
    tt = t * t
    hi = tt.astype(BF16)
    lo = (tt - hi.astype(F32)).astype(BF16)
    ss = jnp.dot(hi, bd, preferred_element_type=F32) + jnp.dot(lo, bd, preferred_element_type=F32)
    return t * lax.rsqrt(ss * (1.0 / ATT_HEAD_DIM) + RMS_EPS) * gain


def _inproj_even_kernel(x_ref, g_ref, w_ref, bd_ref, qn_ref, kn_ref, u_ref, q_ref, k_ref, v_ref):
    half = x_ref.shape[0] // ROW_GROUPS
    parts = [slice(g * half, (g + 1) * half) for g in range(ROW_GROUPS)]
    hn = [_rms(x_ref[rows, :], g_ref[...]).astype(BF16) for rows in parts]
    projs = [jnp.dot(t, w_ref[...], preferred_element_type=F32) for t in hn]
    bd = bd_ref[...]
    o = SSM_WIDTH
    for rows, proj in zip(parts, projs):
        for j in range(SSM_SLABS):
            u_ref[j, rows, :] = proj[:, j * LANES:(j + 1) * LANES]
        q = _head_norm(proj[:, o:o + ATT_WIDTH], qn_ref[...], bd) * (ATT_HEAD_DIM ** -0.5)
        k = _head_norm(proj[:, o + ATT_WIDTH:o + 2 * ATT_WIDTH], kn_ref[...], bd)
        v = proj[:, o + 2 * ATT_WIDTH:o + 3 * ATT_WIDTH]
        for j in range(ATT_WIDTH // LANES):
            q_ref[j, rows, :] = q[:, j * LANES:(j + 1) * LANES]
            k_ref[j, rows, :] = k[:, j * LANES:(j + 1) * LANES]
            v_ref[j, rows, :] = v[:, j * LANES:(j + 1) * LANES]


def _inproj_even(x2, gain, w_in, q_norm, k_norm):
    n_tok = x2.shape[0]
    n_slab = ATT_WIDTH // LANES
    head_of = jnp.arange(ATT_WIDTH) // ATT_HEAD_DIM
    bd = (head_of[:, None] == head_of[None, :]).astype(BF16)
    qn = jnp.tile(q_norm.astype(F32), ATT_HEADS)[None]
    kn = jnp.tile(k_norm.astype(F32), ATT_HEADS)[None]
    full = lambda shape: pl.BlockSpec(shape, lambda i: (0,) * len(shape))
    slab = pl.BlockSpec((n_slab, TOK_TILE, LANES), lambda i: (0, i, 0))
    slab_shape = jax.ShapeDtypeStruct((n_slab, n_tok, LANES), F32)
    return pl.pallas_call(
        _inproj_even_kernel,
        grid=(n_tok // TOK_TILE,),
        in_specs=[pl.BlockSpec((TOK_TILE, D_MODEL), lambda i: (i, 0)), full((1, D_MODEL)),
                  full(w_in.shape), full(bd.shape), full(qn.shape), full(kn.shape)],
        out_specs=[pl.BlockSpec((SSM_SLABS, TOK_TILE, LANES), lambda i: (0, i, 0)), slab, slab, slab],
        out_shape=[jax.ShapeDtypeStruct((SSM_SLABS, n_tok, LANES), F32), slab_shape, slab_shape, slab_shape],
        compiler_params=_cparams(("parallel",)),
        name="inproj_even",
    )(x2, gain[None].astype(F32), w_in.astype(BF16), bd, qn, kn)


def _s5_tables(a_re, a_im, b_re, b_im, c_re, c_im, d_skip, log_step):
    f = lambda t: t.astype(F32)
    a_re, a_im, b_re, b_im, c_re, c_im = map(f, (a_re, a_im, b_re, b_im, c_re, c_im))
    L = SSM_CHUNK
    step = jnp.exp(f(log_step))[:, None]
    ks = jnp.arange(L + 1, dtype=F32)[:, None, None]
    mag = jnp.exp(ks * (a_re * step)[None])
    ang = ks * (a_im * step)[None]
    pw_re, pw_im = mag * jnp.cos(ang), mag * jnp.sin(ang)
    nr, ni = pw_re[1] - 1.0, pw_im[1]
    den = a_re * a_re + a_im * a_im
    z_re, z_im = (nr * a_re + ni * a_im) / den, (ni * a_re - nr * a_im) / den
    bb_re = z_re[..., None] * b_re - z_im[..., None] * b_im
    bb_im = z_re[..., None] * b_im + z_im[..., None] * b_re
    lb_re = pw_re[..., None] * bb_re[None] - pw_im[..., None] * bb_im[None]
    lb_im = pw_re[..., None] * bb_im[None] + pw_im[..., None] * bb_re[None]
    kk = jnp.einsum('gop,kgpi->gkio', c_re, lb_re[:L]) - jnp.einsum('gop,kgpi->gkio', c_im, lb_im[:L])
    ti = jnp.arange(L)
    lag = ti[None, :] - ti[:, None]
    m = jnp.where((lag >= 0)[None, :, :, None, None], kk[:, jnp.maximum(lag, 0)], 0.0)
    ns, gs = SSM_SLABS, SSM_SLAB_GROUPS
    lw = L * LANES
    pm = m.reshape(ns, gs, L, L, SSM_GROUP, SSM_GROUP).transpose(0, 2, 1, 4, 3, 5).reshape(ns, lw, L * SSM_GROUP)
    fold_e = lambda t: t[:L][::-1].reshape(L, ns, gs, SSM_STATE, SSM_GROUP).transpose(1, 0, 2, 4, 3).reshape(ns, lw, SSM_STATE)
    pe = jnp.concatenate([fold_e(lb_re), fold_e(lb_im)], axis=-1)
    pw1_re, pw1_im = (t[1:].transpose(1, 0, 2)[:, :, None, :] for t in (pw_re, pw_im))
    cl_re = c_re[:, None] * pw1_re - c_im[:, None] * pw1_im
    cl_im = c_re[:, None] * pw1_im + c_im[:, None] * pw1_re
    fold_f = lambda t: t.reshape(ns, gs, L, SSM_GROUP, SSM_STATE).transpose(0, 4, 2, 1, 3).reshape(ns, SSM_STATE, lw)
    pf = jnp.concatenate([fold_f(cl_re), fold_f(-cl_im)], axis=1)
    per_chain = lambda t: jnp.tile(t.reshape(ns, gs * SSM_STATE // LANES, LANES), (1, 2, 1))
    d_vec = jnp.tile(f(d_skip).reshape(ns, 1, LANES), (1, 1, L))
    return pm.astype(BF16), pe.astype(BF16), pf.astype(BF16), per_chain(pw_re[L]), per_chain(pw_im[L]), d_vec


def _iota2(shape):
    return lax.broadcasted_iota(jnp.int32, shape, 0), lax.broadcasted_iota(jnp.int32, shape, 1)


def _widen(compact, group_major_cols, shape, r_shift, c_shift, sel_rows):
    gmask = SSM_SLAB_GROUPS - 1
    if sel_rows:
        r, c = _iota2((shape[0], compact.shape[0]))
        sel = ((r >> 9) == (c >> 6)) & ((r & (SSM_STATE - 1)) == (c & (SSM_STATE - 1)))
        wide = jnp.dot(sel.astype(BF16), compact, preferred_element_type=F32)
    else:
        r, c = _iota2((compact.shape[1], shape[1]))
        if group_major_cols:
            sel = ((c >> 9) == (r >> 6)) & ((c & (SSM_STATE - 1)) == (r & (SSM_STATE - 1)))
        else:
            sel = ((c >> 7) == (r >> 4)) & ((c & (SSM_GROUP - 1)) == (r & (SSM_GROUP - 1)))
        wide = jnp.dot(compact, sel.astype(BF16), preferred_element_type=F32)
    r, c = _iota2(shape)
    keep = ((r >> r_shift) & gmask) == ((c >> c_shift) & gmask)
    return jnp.where(keep, wide, 0.0).astype(BF16)


def _s5_kernel(u_ref, pm_ref, pe_ref, pf_ref, ar_ref, ai_ref, d_ref, y_ref, m_ref, e_ref, f_ref, x_ref, sr_ref, si_ref,
               *, n_chunk, pitch):
    L = SSM_CHUNK
    n_blk = SSM_SLAB_GROUPS * SSM_STATE // LANES
    n_re = n_blk * LANES
    lw = L * LANES
    m_ref[...] = _widen(pm_ref[...], False, (lw, lw), 4, 4, False)
    e_ref[...] = _widen(pe_ref[...], True, (lw, 2 * n_re), 4, 6, False)
    f_ref[...] = _widen(pf_ref[...], False, (2 * n_re, lw), 6, 4, True)
    tiles = [(b, c0) for b in range(2) for c0 in range(0, n_chunk, S5_ROW_TILE)]
    for b, c0 in tiles:
        r0 = b * n_chunk + c0
        for t in range(L):
            x_ref[r0:r0 + S5_ROW_TILE, t * LANES:(t + 1) * LANES] = (
                u_ref[pl.ds(r0 * L + t, S5_ROW_TILE, stride=L), :].astype(BF16))
        sl = jnp.dot(x_ref[r0:r0 + S5_ROW_TILE, :], e_ref[...], preferred_element_type=F32)
        for j in range(n_blk):
            base = (b * n_blk + j) * pitch + c0
            sr_ref[base:base + S5_ROW_TILE, :] = sl[:, j * LANES:(j + 1) * LANES]
            si_ref[base:base + S5_ROW_TILE, :] = sl[:, n_re + j * LANES:n_re + (j + 1) * LANES]
    ar, ai = ar_ref[...], ai_ref[...]
    half = LANES

    def scan_step(c, carry):
        s_re, s_im = carry
        rows = pl.ds(c, SUBLANES, stride=pitch)
        x_re, x_im = sr_ref[rows, :], si_ref[rows, :]
        sr_ref[rows, :] = s_re
        si_ref[rows, :] = s_im
        return ar * s_re - ai * s_im + x_re, ar * s_im + ai * s_re + x_im

    zero = jnp.zeros((SUBLANES, half), F32)
    lax.fori_loop(0, n_chunk, scan_step, (zero, zero), unroll=8)

    for b, c0 in tiles:
        r0 = b * n_chunk + c0
        chain = lambda ref, j: ref[(b * n_blk + j) * pitch + c0:(b * n_blk + j) * pitch + c0 + S5_ROW_TILE, :]
        sp = jnp.concatenate([chain(sr_ref, j) for j in range(n_blk)] + [chain(si_ref, j) for j in range(n_blk)],
                             axis=1).astype(BF16)
        xt = x_ref[r0:r0 + S5_ROW_TILE, :]
        y = (jnp.dot(xt, m_ref[...], preferred_element_type=F32)
             + jnp.dot(sp, f_ref[...], preferred_element_type=F32)
             + d_ref[...] * xt.astype(F32))
        for t in range(L):
            y_ref[pl.ds(r0 * L + t, S5_ROW_TILE, stride=L), :] = y[:, t * LANES:(t + 1) * LANES]


def _s5_core(u, tables, bsz, s_len):
    assert bsz == 2, "the scan packs (batch, lane block) into the 8 sublanes of one vreg"
    pm, pe, pf, a_r, a_i, d_vec = tables
    n_tok = bsz * s_len
    n_chunk = s_len // SSM_CHUNK
    pitch = n_chunk + SUBLANES
    lw = SSM_CHUNK * LANES
    n_state = 2 * SSM_SLAB_GROUPS * SSM_STATE
    slab = lambda shape, **kw: pl.BlockSpec((None,) + shape, lambda i: (i,) + (0,) * len(shape), **kw)
    return pl.pallas_call(
        functools.partial(_s5_kernel, n_chunk=n_chunk, pitch=pitch),
        grid=(SSM_SLABS,),
        in_specs=[slab((n_tok, LANES)), slab(pm.shape[1:]), slab(pe.shape[1:]), slab(pf.shape[1:]),
                  slab((SUBLANES, LANES)), slab((SUBLANES, LANES)), slab((1, lw))],
        out_specs=slab((n_tok, LANES)),
        out_shape=jax.ShapeDtypeStruct((SSM_SLABS, n_tok, LANES), F32),
        scratch_shapes=[pltpu.VMEM((lw, lw), BF16), pltpu.VMEM((lw, n_state), BF16), pltpu.VMEM((n_state, lw), BF16),
                        pltpu.VMEM((bsz * n_chunk, lw), BF16),
                        pltpu.VMEM((SUBLANES * pitch, LANES), F32),
                        pltpu.VMEM((SUBLANES * pitch, LANES), F32)],
        compiler_params=_cparams(("parallel",)),
        name="s5_core",
    )(u, pm, pe, pf, a_r, a_i, d_vec)


def _attn_kernel(slope_ref, q_ref, kp_ref, kc_ref, vp_ref, vc_ref, o_ref,
                 q4_ref, k4_ref, v4_ref, k1_ref, v1_ref, m_ref, l_ref, acc_ref):
    slab = pl.program_id(1)
    sb = pl.program_id(2)
    fine = DILATIONS[1]
    nq, nk = ATT_SB // fine, 2 * ATT_SB // fine
    for r in range(fine):
        q4_ref[r * nq:(r + 1) * nq, :] = q_ref[pl.ds(r, nq, stride=fine), :]
        for dst, prev, cur in ((k4_ref, kp_ref, kc_ref), (v4_ref, vp_ref, vc_ref)):
            dst[r * nk:r * nk + nq, :] = prev[pl.ds(r, nq, stride=fine), :]
            dst[r * nk + nq:(r + 1) * nk, :] = cur[pl.ds(r, nq, stride=fine), :]
    for dst, prev, cur in ((k1_ref, kp_ref, kc_ref), (v1_ref, vp_ref, vc_ref)):
        dst[0:ATT_BLK, :] = prev[ATT_SB - ATT_BLK:ATT_SB, :]
        dst[ATT_BLK:ATT_BLK + ATT_SB, :] = cur[...]

    lane = lax.broadcasted_iota(jnp.int32, (ATT_BLK, LANES), 1)
    head0 = lane < ATT_HEAD_DIM
    qi = lax.broadcasted_iota(jnp.int32, (ATT_BLK, 2 * ATT_BLK), 0)
    kj = lax.broadcasted_iota(jnp.int32, (ATT_BLK, 2 * ATT_BLK), 1)
    back = qi + ATT_BLK - kj
    band = (back >= 0) & (back <= ATT_BLK)
    neg_steps = -back.astype(F32)
    slopes = (slope_ref[2 * slab], slope_ref[2 * slab + 1])

    for pat, dil in enumerate(DILATIONS):
        span = ATT_BLK * dil

        def tile(idx, carry, dil=dil, span=span, pat=pat):
            start = (idx // dil) * span + idx % dil
            seq_ok = jnp.logical_or(sb > 0, idx >= dil)
            valid = band & ((kj >= ATT_BLK) | seq_ok)
            if dil == 1:
                start = pl.multiple_of(start, ATT_BLK)
                rows = pl.ds(start, ATT_BLK)
                qt, k_src, v_src, k_rows = q_ref[rows, :], k1_ref, v1_ref, pl.ds(start, 2 * ATT_BLK)
            elif dil == fine:
                first = pl.multiple_of((idx // fine) * ATT_BLK, ATT_BLK)
                rows = pl.ds(start, ATT_BLK, stride=dil)
                qt = q4_ref[pl.ds((idx % fine) * nq + first, ATT_BLK), :]
                k_src, v_src = k4_ref, v4_ref
                k_rows = pl.ds((idx % fine) * nk + (nq - ATT_BLK) + first, 2 * ATT_BLK)
            else:
                rows = pl.ds(start, ATT_BLK, stride=dil)
                qt = q4_ref[pl.ds((idx % fine) * nq + idx // fine, ATT_BLK, stride=fine), :]
                k_src, v_src = k4_ref, v4_ref
                k_rows = pl.ds((idx % fine) * nk + idx // fine, 2 * ATT_BLK, stride=fine)
            kt = k_src[k_rows, :].astype(BF16)
            vt = v_src[k_rows, :].astype(BF16)
            q0 = jnp.where(head0, qt, 0.0)
            scores = [lax.dot_general(qh.astype(BF16), kt, (((1,), (1,)), ((), ())), preferred_element_type=F32)
                      for qh in (q0, qt - q0)]
            return rows, valid, vt, scores

        def softmax(valid, s, hh, dil=dil):
            s = jnp.where(valid, s + (slopes[hh] * float(dil)) * neg_steps, NEG_INF)
            m = jnp.max(s, axis=-1, keepdims=True)
            p = jnp.exp(s - m)
            return m, jnp.sum(p, axis=-1, keepdims=True), p.astype(BF16)

        def merge(rows, parts, pat=pat):
            (m0, l0, o0), (m1, l1, o1) = parts
            m_t = jnp.where(head0, m0, m1)
            l_t = jnp.where(head0, l0, l1)
            o_t = jnp.where(head0, o0, o1)
            if pat == 0:
                m_ref[rows, :] = m_t
                l_ref[rows, :] = l_t
                acc_ref[rows, :] = o_t
            else:
                m_old = m_ref[rows, :]
                m_new = jnp.maximum(m_old, m_t)
                a = jnp.exp(m_old - m_new)
                b = jnp.exp(m_t - m_new)
                m_ref[rows, :] = m_new
                l_ref[rows, :] = a * l_ref[rows, :] + b * l_t
                acc_ref[rows, :] = a * acc_ref[rows, :] + b * o_t

        def group(gi, carry):
            tiles = [tile(gi * ATT_GROUP + t, None) for t in range(ATT_GROUP)]
            soft = [[softmax(valid, s, hh) for hh, s in enumerate(scores)] for _, valid, _, scores in tiles]
            outs = [[(m, l, jnp.dot(p, vt, preferred_element_type=F32)) for m, l, p in per_head]
                    for (_, _, vt, _), per_head in zip(tiles, soft)]
            for (rows, _, _, _), parts in zip(tiles, outs):
                merge(rows, parts)
            return carry

        lax.fori_loop(0, ATT_SB // ATT_BLK // ATT_GROUP, group, 0)

    o_ref[...] = (acc_ref[...] / l_ref[...]).astype(o_ref.dtype)


def _dilated_attention(q, k, v, bsz, s_len):
    assert DILATIONS == (DILATIONS[1] ** 2, DILATIONS[1], 1) and ATT_SB == ATT_BLK * DILATIONS[0]
    n_slab = q.shape[0]
    shape4 = (n_slab, bsz, s_len, LANES)
    q, k, v = (t.reshape(shape4) for t in (q, k, v))
    slopes = jnp.asarray([2.0 ** (-8.0 * (h + 1) / ATT_HEADS) for h in range(ATT_HEADS)], F32)
    blk = (None, None, ATT_SB, LANES)
    cur = pl.BlockSpec(blk, lambda b, j, i, s: (j, b, i, 0))
    prev = pl.BlockSpec(blk, lambda b, j, i, s: (j, b, jnp.maximum(i - 1, 0), 0))
    out = pl.pallas_call(
        _attn_kernel,
        grid_spec=pltpu.PrefetchScalarGridSpec(
            num_scalar_prefetch=1,
            grid=(bsz, n_slab, s_len // ATT_SB),
            in_specs=[cur, prev, cur, prev, cur],
            out_specs=cur,
            scratch_shapes=[pltpu.VMEM((ATT_SB, LANES), F32),
                            pltpu.VMEM((2 * ATT_SB, LANES), F32), pltpu.VMEM((2 * ATT_SB, LANES), F32),
                            pltpu.VMEM((ATT_BLK + ATT_SB, LANES), F32), pltpu.VMEM((ATT_BLK + ATT_SB, LANES), F32),
                            pltpu.VMEM((ATT_SB, LANES), F32), pltpu.VMEM((ATT_SB, LANES), F32),
                            pltpu.VMEM((ATT_SB, LANES), F32)]),
        out_shape=jax.ShapeDtypeStruct(shape4, BF16),
        compiler_params=_cparams(("parallel", "parallel", "parallel")),
        name="dilated_attn",
    )(slopes, q, k, k, v, v)
    return out.reshape(n_slab, bsz * s_len, LANES)


def _route_epilogue(h_parts, gain_ref, wr_ref, br_ref, tri_ref, cnt_ref, h_ref, hn_ref, ids_ref, gate_ref, cnt_out_ref):
    sub = h_parts[0].shape[0]
    tm = sub * len(h_parts)
    splits = []
    for g, h in enumerate(h_parts):
        h_ref[g * sub:(g + 1) * sub, :] = h
        hn = _rms(h, gain_ref[...])
        _store_row_tiles(hn_ref, hn, first=g * sub)
        hn_hi = hn.astype(BF16)
        splits.append((hn_hi, (hn - hn_hi.astype(F32)).astype(BF16)))
    logit_parts = []
    for hn_hi, hn_lo in splits:
        hi_part = jnp.dot(hn_hi, wr_ref[...], preferred_element_type=F32)
        logit_parts.append(hi_part[:, :LANES] + (hi_part[:, LANES:]
                                                 + jnp.dot(hn_lo, wr_ref[:, :LANES], preferred_element_type=F32)))
    logits = jnp.concatenate(logit_parts, axis=0) + br_ref[...]
    lane = lax.broadcasted_iota(jnp.int32, (tm, LANES), 1)
    big = jnp.int32(LANES)
    rmax = lambda t: jnp.max(t, axis=-1, keepdims=True)
    rmin = lambda t: jnp.min(t, axis=-1, keepdims=True)
    rsum = lambda t: jnp.sum(t, axis=-1, keepdims=True)
    gmask = lane < N_GROUPS
    gl = jnp.where(gmask, logits, -jnp.inf)
    gmax = rmax(gl)
    ge = jnp.where(gmask, jnp.exp(gl - gmax), 0.0)
    gprob = ge / rsum(ge)
    g_w = rmax(gprob)
    grp = rmin(jnp.where(gmask & (gprob == g_w), lane, big))
    group_of_lane = (lane - N_GROUPS) >> int(math.log2(EXPERTS_PER_GROUP))
    emask = (lane >= N_GROUPS) & (lane < N_GROUPS + N_EXPERTS) & (group_of_lane == grp)
    el = jnp.where(emask, logits, -jnp.inf)
    ee = jnp.where(emask, jnp.exp(el - rmax(el)), 0.0)
    ep = jnp.where(emask, ee / rsum(ee), -1.0)
    p1 = rmax(ep)
    i1 = rmin(jnp.where(ep == p1, lane, big))
    ep2 = jnp.where(lane == i1, -1.0, ep)
    p2 = rmax(ep2)
    i2 = rmin(jnp.where(ep2 == p2, lane, big))
    e1, e2 = i1 - N_GROUPS, i2 - N_GROUPS
    psum = p1 + p2
    gate1, gate2 = g_w * p1 / psum, g_w * p2 / psum
    oh1, oh2 = lane == e1, lane == e2
    member = (oh1 | oh2).astype(BF16)
    before = jnp.dot(tri_ref[...], member, preferred_element_type=F32) + cnt_ref[...]
    r1 = rsum(jnp.where(oh1, before, 0.0)).astype(jnp.int32)
    r2 = rsum(jnp.where(oh2, before, 0.0)).astype(jnp.int32)
    cnt_ref[...] = cnt_ref[...] + jnp.sum(member.astype(F32), axis=0, keepdims=True)
    ids = jnp.where(lane == 0, e1, jnp.where(lane == 1, e2, jnp.where(lane == 2, r1, jnp.where(lane == 3, r2, 0))))
    ids_ref[...] = jnp.transpose(ids)[0:SUBLANES, :]
    gate_ref[...] = jnp.where(lane == 0, gate1, jnp.where(lane == 1, gate2, 0.0))
    cnt_out_ref[...] = jnp.broadcast_to(cnt_ref[...], cnt_out_ref.shape)


def _router_operands(norm_gain, w_rg, b_rg, w_re, b_re):
    pad = LANES - N_GROUPS - N_EXPERTS
    wr = jnp.pad(jnp.concatenate([w_rg, w_re], axis=1).astype(F32), ((0, 0), (0, pad)))
    br = jnp.pad(jnp.concatenate([b_rg, b_re]).astype(F32), (0, pad))[None]
    r = jnp.arange(TOK_TILE)
    tri = (r[None, :] < r[:, None]).astype(BF16)
    wr_hi = wr.astype(BF16)
    wr_lo = (wr - wr_hi.astype(F32)).astype(BF16)
    return norm_gain[None].astype(F32), jnp.concatenate([wr_hi, wr_lo], axis=1), br, tri


def _route_specs(n_tok):
    full = lambda shape: pl.BlockSpec(shape, lambda i: (0,) * len(shape))
    in_specs = [full((1, D_MODEL)), full((D_MODEL, 2 * LANES)), full((1, LANES)), full((TOK_TILE, TOK_TILE))]
    tok = lambda w: pl.BlockSpec((TOK_TILE, w), lambda i: (i, 0))
    out_specs = [tok(D_MODEL), pl.BlockSpec((TOK_TILE * ROW_CHUNKS, LANES), lambda i: (i, 0)),
                 pl.BlockSpec((SUBLANES, TOK_TILE), lambda i: (0, i)), tok(LANES), full((SUBLANES, LANES))]
    out_shape = [jax.ShapeDtypeStruct((n_tok, D_MODEL), F32), jax.ShapeDtypeStruct((n_tok * ROW_CHUNKS, LANES), jnp.uint32),
                 jax.ShapeDtypeStruct((SUBLANES, n_tok), jnp.int32), jax.ShapeDtypeStruct((n_tok, LANES), F32),
                 jax.ShapeDtypeStruct((SUBLANES, LANES), F32)]
    return in_specs, out_specs, out_shape


def _gelu_tanh(x):
    return 0.5 * x * (1.0 + jnp.tanh(math.sqrt(2.0 / math.pi) * (x + 0.044715 * (x * x * x))))


def _outproj_even_kernel(x_ref, y_ref, a_ref, wglu_ref, bglu_ref, wout_ref, gain_ref, wr_ref, br_ref, tri_ref,
                         h_ref, hn_ref, ids_ref, gate_ref, cnt_out_ref, cnt_ref):
    @pl.when(pl.program_id(0) == 0)
    def _():
        cnt_ref[...] = jnp.zeros_like(cnt_ref)

    half = x_ref.shape[0] // ROW_GROUPS
    parts = [slice(g * half, (g + 1) * half) for g in range(ROW_GROUPS)]
    y = [_gelu_tanh(jnp.concatenate([y_ref[j, rows, :] for j in range(SSM_SLABS)], axis=1)) for rows in parts]
    glu = [jnp.dot(t.astype(BF16), wglu_ref[...], preferred_element_type=F32) for t in y]
    h_out = []
    for rows, t, g in zip(parts, y, glu):
        t = t * jax.nn.sigmoid(g + bglu_ref[...])
        mix = jnp.dot(t.astype(BF16), wout_ref[0:SSM_WIDTH, :], preferred_element_type=F32)
        for j in range(ATT_WIDTH // LANES):
            w_rows = slice(SSM_WIDTH + j * LANES, SSM_WIDTH + (j + 1) * LANES)
            mix = mix + jnp.dot(a_ref[j, rows, :], wout_ref[w_rows, :], preferred_element_type=F32)
        h_out.append(x_ref[rows, :] + mix)
    _route_epilogue(h_out, gain_ref, wr_ref, br_ref, tri_ref, cnt_ref,
                    h_ref, hn_ref, ids_ref, gate_ref, cnt_out_ref)


def _outproj_even(x2, y_pre, attn, w_glu, b_glu, w_out, route_ops):
    n_tok = x2.shape[0]
    n_slab = attn.shape[0]
    r_in, r_out, r_shape = _route_specs(n_tok)
    full = lambda shape: pl.BlockSpec(shape, lambda i: (0,) * len(shape))
    return pl.pallas_call(
        _outproj_even_kernel,
        grid=(n_tok // TOK_TILE,),
        in_specs=[pl.BlockSpec((TOK_TILE, D_MODEL), lambda i: (i, 0)),
                  pl.BlockSpec((SSM_SLABS, TOK_TILE, LANES), lambda i: (0, i, 0)),
                  pl.BlockSpec((n_slab, TOK_TILE, LANES), lambda i: (0, i, 0)),
                  full(w_glu.shape), full((1, SSM_WIDTH)), full(w_out.shape)] + r_in,
        out_specs=r_out, out_shape=r_shape,
        scratch_shapes=[pltpu.VMEM((1, LANES), F32)],
        compiler_params=_cparams(("arbitrary",)),
        name="outproj_even",
    )(x2, y_pre, attn, w_glu.astype(BF16), b_glu[None].astype(F32), w_out.astype(BF16), *route_ops)


def _conv_layer_kernel(h_ref, pgate_ref, y0_ref, y1_ref, gmix_ref, win_ref, cw_ref, wout_ref, gain_ref, wr_ref, br_ref,
                       tri_ref, ho_ref, hn_ref, ids_ref, gate_ref, cnt_out_ref, cnt_ref, zc_ref, *, tiles_per_seq):
    i = pl.program_id(0)

    @pl.when(i == 0)
    def _():
        cnt_ref[...] = jnp.zeros_like(cnt_ref)

    @pl.when(i % tiles_per_seq == 0)
    def _():
        zc_ref[0:SUBLANES, :] = jnp.zeros((SUBLANES, D_MODEL), F32)

    tm = h_ref.shape[0]
    c = D_MODEL
    half = tm // ROW_GROUPS
    parts = tuple(g * half for g in range(ROW_GROUPS))
    h_in, hn = {}, {}
    for r0 in parts:
        rows = slice(r0, r0 + half)
        pgate = pgate_ref[rows, :]
        h_in[r0] = (h_ref[rows, :] + pgate[:, 0:1] * _load_row_tiles(y0_ref, first=r0, n=half)
                    + pgate[:, 1:2] * _load_row_tiles(y1_ref, first=r0, n=half))
        hn[r0] = _rms(h_in[r0], gmix_ref[...]).astype(BF16)
    b_gate, zc = {}, {}
    for r0 in parts:
        b_gate[r0] = jnp.dot(hn[r0], win_ref[:, 0:c], preferred_element_type=F32)
        zc[r0] = (jnp.dot(hn[r0], win_ref[:, c:2 * c], preferred_element_type=F32)
                  * jnp.dot(hn[r0], win_ref[:, 2 * c:3 * c], preferred_element_type=F32))
    h_out = []
    for r0 in parts:
        z0 = SUBLANES + r0
        zc_ref[z0:z0 + half, :] = zc[r0]
        conv = cw_ref[CONV_TAPS - 1:CONV_TAPS, :] * zc[r0]
        for back in range(1, CONV_TAPS):
            tap = CONV_TAPS - 1 - back
            conv = conv + cw_ref[tap:tap + 1, :] * zc_ref[z0 - back:z0 - back + half, :]
        mix = jnp.dot((b_gate[r0] * conv).astype(BF16), wout_ref[...], preferred_element_type=F32)
        h_out.append(h_in[r0] + mix)
    zc_ref[0:SUBLANES, :] = zc_ref[tm:tm + SUBLANES, :]
    _route_epilogue(h_out, gain_ref, wr_ref, br_ref, tri_ref, cnt_ref,
                    ho_ref, hn_ref, ids_ref, gate_ref, cnt_out_ref)


def _conv_layer(h1, pgate, y2, gain_mix, w_in, conv_w, w_out, route_ops, s_len):
    n_tok = h1.shape[0]
    r_in, r_out, r_shape = _route_specs(n_tok)
    full = lambda shape: pl.BlockSpec(shape, lambda i: (0,) * len(shape))
    return pl.pallas_call(
        functools.partial(_conv_layer_kernel, tiles_per_seq=s_len // TOK_TILE),
        grid=(n_tok // TOK_TILE,),
        in_specs=_moe_specs(n_tok) + [full((1, D_MODEL)), full(w_in.shape), full(conv_w.shape), full(w_out.shape)] + r_in,
        out_specs=r_out, out_shape=r_shape,
        scratch_shapes=[pltpu.VMEM((1, LANES), F32), pltpu.VMEM((TOK_TILE + SUBLANES, D_MODEL), F32)],
        compiler_params=_cparams(("arbitrary",)),
        name="conv_layer",
    )(h1, pgate, y2, y2, gain_mix[None].astype(F32), w_in.astype(BF16), conv_w.astype(F32), w_out.astype(BF16),
      *route_ops)


def _row_map(experts, ranks, first_row, n_rows):
    n_slot, n_tok = len(experts), experts[0].shape[0]
    n_assign = n_slot * n_tok
    n_src = n_rows + 2 * MOE_BLK
    mesh = plsc.VectorSubcoreMesh(core_axis_name="core", subcore_axis_name="subcore",
                                  num_cores=SC_CORES, num_subcores=SC_SUBCORES)
    tok_vec = pltpu.VMEM((n_tok,), jnp.int32)

    @functools.partial(
        pl.kernel, mesh=mesh, out_type=jax.ShapeDtypeStruct((n_src,), jnp.int32),
        scratch_types=[tok_vec] * (2 * n_slot) + [pltpu.VMEM((N_EXPERTS,), jnp.int32), pltpu.VMEM((n_src,), jnp.int32)],
        compiler_params=pltpu.CompilerParams(needs_layout_passes=False), name="moe_row_map")
    def row_map(*refs):
        ins, src_hbm, scratch = refs[:2 * n_slot + 1], refs[2 * n_slot + 1], refs[2 * n_slot + 2:]
        first_v, src_v = scratch[2 * n_slot], scratch[2 * n_slot + 1]

        @pl.when(jnp.logical_and(lax.axis_index("core") == 0, lax.axis_index("subcore") == 0))
        def _():
            for hbm, vmem in zip(ins, scratch):
                pltpu.sync_copy(hbm, vmem)
            lanes = lax.iota(jnp.int32, SC_LANES)

            @pl.loop(0, n_src, step=SC_LANES)
            def _(i):
                src_v[pl.ds(i, SC_LANES)] = n_assign + ((i + lanes) & (MOE_BLK - 1))

            for slot in range(n_slot):
                e_v, r_v = scratch[slot], scratch[n_slot + slot]

                @pl.loop(0, n_tok, step=SC_LANES)
                def _(t):
                    row = plsc.load_gather(first_v, [e_v[pl.ds(t, SC_LANES)]]) + r_v[pl.ds(t, SC_LANES)]
                    plsc.store_scatter(src_v, [row + MOE_BLK], slot * n_tok + t + lanes)

            pltpu.sync_copy(src_v, src_hbm)

    return row_map(*experts, *ranks, first_row)


def _expert_kernel(blk_e_ref, n_used_ref, src_ref, next_e_ref, wslot_ref, hn_ref, wg_ref, wu_ref, wd_ref, y2_ref,
                   xbuf, ybuf, wg_f, wu_f, wd_f, wg_s, wu_s, wd_s, ssem, wsem, *, n_tok, layer):
    b = pl.program_id(0)
    n_used = n_used_ref[0]
    cur = b % 2
    nxt = 1 - cur

    tile = lambda i: pl.ds(i * ROW_CHUNKS, ROW_CHUNKS)

    def scatter(blk, slot, i):
        row = src_ref[(blk + 1) * MOE_BLK + i]
        return pltpu.make_async_copy(ybuf.at[slot, tile(i)], y2_ref.at[row], ssem.at[slot])

    wait_block = lambda slot: pltpu.make_async_copy(ybuf.at[slot], ybuf.at[slot], ssem.at[slot]).wait()

    def weights(expert, slot, act):
        for hbm, buf in ((wg_ref, wg_f), (wu_ref, wu_f), (wd_ref, wd_f)):
            act(pltpu.make_async_copy(hbm.at[layer, expert], buf.at[slot], wsem.at[slot]))

    @pl.when(b == 0)
    def _():
        ybuf[1] = jnp.zeros(ybuf.shape[1:], ybuf.dtype)
        weights(blk_e_ref[0], wslot_ref[0], lambda c: c.start())

    @pl.when(b < n_used)
    def _():
        new_expert = jnp.logical_or(b == 0, blk_e_ref[b] != blk_e_ref[jnp.maximum(b - 1, 0)])

        @pl.when(new_expert)
        def _():
            slot = wslot_ref[b]
            weights(blk_e_ref[b], slot, lambda c: c.wait())
            wg_s[...] = wg_f[slot].astype(BF16)
            wu_s[...] = wu_f[slot].astype(BF16)
            wd_s[...] = wd_f[slot].astype(BF16)

            @pl.when(next_e_ref[b] >= 0)
            def _():
                weights(next_e_ref[b], 1 - slot, lambda c: c.start())

        @pl.when(b >= 1)
        def _():
            wait_block(cur)

        for i in range(MOE_BLK):
            tok = src_ref[(b + 1) * MOE_BLK + i] & (n_tok - 1)
            xbuf[tile(i), :] = hn_ref[pl.ds(pl.multiple_of(tok * ROW_CHUNKS, ROW_CHUNKS), ROW_CHUNKS), :]
            scatter(b - 1, nxt, i).start(priority=i % 2)
        x = _load_row_tiles(xbuf).astype(BF16)
        g = jnp.dot(x, wg_s[...], preferred_element_type=F32)
        u = jnp.dot(x, wu_s[...], preferred_element_type=F32)
        hb = (g * jax.nn.sigmoid(g) * u).astype(BF16)
        _store_row_tiles(ybuf, jnp.dot(hb, wd_s[...], preferred_element_type=F32), (cur,))

    @pl.when(b == n_used)
    def _():
        wait_block(cur)
        for i in range(MOE_BLK):
            scatter(b - 1, nxt, i).start(priority=i % 2)
        wait_block(nxt)


def _expert_ffn(hn, src, blk_e, n_used, next_e, wslot, w_g, w_u, w_d, layer):
    n_tok = hn.shape[0] // ROW_CHUNKS
    assert n_tok & (n_tok - 1) == 0, "dump-row aliasing masks the token index with T - 1"
    n_blocks = (src.shape[0] - 2 * MOE_BLK) // MOE_BLK
    in_hbm = pl.BlockSpec(memory_space=pltpu.HBM)
    return pl.pallas_call(
        functools.partial(_expert_kernel, n_tok=n_tok, layer=layer),
        grid_spec=pltpu.PrefetchScalarGridSpec(
            num_scalar_prefetch=5,
            grid=(n_blocks + 1,),
            in_specs=[pl.BlockSpec(hn.shape, lambda b, *_: (0, 0), pipeline_mode=pl.Buffered(1)),
                      in_hbm, in_hbm, in_hbm],
            out_specs=pl.BlockSpec(memory_space=pltpu.HBM),
            scratch_shapes=[pltpu.VMEM((MOE_BLK * ROW_CHUNKS, LANES), jnp.uint32),
                            pltpu.VMEM((2, MOE_BLK * ROW_CHUNKS, LANES), jnp.uint32),
                            pltpu.VMEM((2, D_MODEL, D_EXPERT), F32), pltpu.VMEM((2, D_MODEL, D_EXPERT), F32),
                            pltpu.VMEM((2, D_EXPERT, D_MODEL), F32),
                            pltpu.VMEM((D_MODEL, D_EXPERT), BF16), pltpu.VMEM((D_MODEL, D_EXPERT), BF16),
                            pltpu.VMEM((D_EXPERT, D_MODEL), BF16),
                            pltpu.SemaphoreType.DMA((2,)), pltpu.SemaphoreType.DMA((2,))]),
        out_shape=jax.ShapeDtypeStruct((2 * n_tok + MOE_BLK, ROW_CHUNKS, LANES), jnp.uint32),
        compiler_params=_cparams(("arbitrary",)),
        name="moe_experts",
    )(blk_e, n_used, src, next_e, wslot, hn, w_g, w_u, w_d)


def _moe_add_kernel(h_ref, gate_ref, y0_ref, y1_ref, o_ref):
    gate = gate_ref[...]
    o_ref[...] = h_ref[...] + gate[:, 0:1] * _load_row_tiles(y0_ref) + gate[:, 1:2] * _load_row_tiles(y1_ref)


def _moe_specs(n_tok):
    slots = n_tok // TOK_TILE
    y2_rows = (TOK_TILE * ROW_CHUNKS, LANES)
    return [pl.BlockSpec((TOK_TILE, D_MODEL), lambda i: (i, 0)), pl.BlockSpec((TOK_TILE, LANES), lambda i: (i, 0)),
            pl.BlockSpec(y2_rows, lambda i: (i, 0)), pl.BlockSpec(y2_rows, lambda i: (slots + i, 0))]


def _moe_add(h, gate, y2):
    n_tok = h.shape[0]
    return pl.pallas_call(
        _moe_add_kernel,
        grid=(n_tok // TOK_TILE,),
        in_specs=_moe_specs(n_tok),
        out_specs=pl.BlockSpec((TOK_TILE, D_MODEL), lambda i: (i, 0)),
        out_shape=jax.ShapeDtypeStruct((n_tok, D_MODEL), F32),
        compiler_params=_cparams(("parallel",)),
        name="moe_add",
    )(h, gate, y2, y2)


def _moe(hn, ids, counts, w_g, w_u, w_d, layer):
    n_tok = hn.shape[0] // ROW_CHUNKS
    n_assign = 2 * n_tok
    n_blocks = n_assign // MOE_BLK + N_EXPERTS
    n_rows = n_blocks * MOE_BLK
    cnt = counts[0, :N_EXPERTS].astype(jnp.int32)
    padded = (cnt + MOE_BLK - 1) // MOE_BLK * MOE_BLK
    pends = jnp.cumsum(padded).astype(jnp.int32)
    pstarts = pends - padded
    blk_start = jnp.arange(n_blocks + 1, dtype=jnp.int32) * MOE_BLK
    blk_e = jnp.minimum(jnp.sum(pends[None, :] <= blk_start[:, None], axis=1), N_EXPERTS - 1).astype(jnp.int32)
    n_used = (pends[-1:] // MOE_BLK).astype(jnp.int32)
    used = (padded > 0)[None, :]
    e_idx = jnp.arange(N_EXPERTS, dtype=jnp.int32)[None, :]
    next_e = jnp.min(jnp.where(used & (e_idx > blk_e[:, None]), e_idx, N_EXPERTS), axis=1)
    next_e = jnp.where(next_e < N_EXPERTS, next_e, -1).astype(jnp.int32)
    ordinal = jnp.sum((used & (e_idx <= blk_e[:, None])).astype(jnp.int32), axis=1) - 1
    src = _row_map((ids[0], ids[1]), (ids[2], ids[3]), pstarts, n_rows)
    y2 = _expert_ffn(hn, src, blk_e, n_used, next_e, (ordinal % 2).astype(jnp.int32), w_g, w_u, w_d, layer)
    return y2.reshape(-1, LANES)


def kernel(x, norm_mix, norm_ffn, w_in_even, ssm_a_re, ssm_a_im, ssm_b_re, ssm_b_im, ssm_c_re, ssm_c_im, ssm_d,
           ssm_log_step, w_glu, b_glu, q_norm, k_norm, w_out_even, w_in_conv, conv_w, w_out_conv, w_router_group,
           b_router_group, w_router_expert, b_router_expert, w_expert_gate, w_expert_up, w_expert_down):
    bsz, s_len, d = x.shape
    x2 = x.reshape(bsz * s_len, d)
    route = lambda layer: _router_operands(norm_ffn[layer], w_router_group[layer], b_router_group[layer],
                                           w_router_expert[layer], b_router_expert[layer])
    experts = lambda layer: (w_expert_gate, w_expert_up, w_expert_down, layer)

    u, q, k, v = _inproj_even(x2, norm_mix[0], w_in_even[0], q_norm[0], k_norm[0])
    tables = _s5_tables(ssm_a_re[0], ssm_a_im[0], ssm_b_re[0], ssm_b_im[0], ssm_c_re[0], ssm_c_im[0], ssm_d[0],
                        ssm_log_step[0])
    y_pre = _s5_core(u, tables, bsz, s_len)
    attn = _dilated_attention(q, k, v, bsz, s_len)
    h, hn, ids, gate, counts = _outproj_even(x2, y_pre, attn, w_glu[0], b_glu[0], w_out_even[0], route(0))
    y2 = _moe(hn, ids, counts, *experts(0))

    h, hn, ids, gate, counts = _conv_layer(h, gate, y2, norm_mix[1], w_in_conv[0], conv_w[0], w_out_conv[0],
                                           route(1), s_len)
    y2 = _moe(hn, ids, counts, *experts(1))
    return _moe_add(h, gate, y2).reshape(bsz, s_len, d)
```

```python
import functools
import math

import jax
import jax.numpy as jnp
from jax import lax
from jax.experimental import pallas as pl
from jax.experimental.pallas import tpu as pltpu
from jax.experimental.pallas import tpu_sc as plsc

F32 = jnp.float32
BF16 = jnp.bfloat16

D_MODEL = 1024
SSM_GROUP = 16
SSM_GROUPS = 40
SSM_WIDTH = SSM_GROUP * SSM_GROUPS
SSM_STATE = 64
ATT_HEADS = 6
ATT_HEAD_DIM = 64
ATT_WIDTH = ATT_HEADS * ATT_HEAD_DIM
DILATIONS = (16, 4, 1)
ATT_BLK = 128
CONV_TAPS = 3
N_GROUPS = 4
EXPERTS_PER_GROUP = 8
N_EXPERTS = N_GROUPS * EXPERTS_PER_GROUP
D_EXPERT = 512
MOE_BLK = 256
RMS_EPS = 1e-6
NEG_INF = -1e30

LANES = 128
SUBLANES = 8
SC_CORES, SC_SUBCORES, SC_LANES = 2, 16, 16
VMEM_LIMIT = 56 * 1024 * 1024

TOK_TILE = 512
ROW_GROUPS = 2
SSM_CHUNK = 8
SSM_SLAB_GROUPS = LANES // SSM_GROUP
SSM_SLABS = SSM_WIDTH // LANES
S5_ROW_TILE = 256
ATT_SB = 2048
ATT_GROUP = 4
ROW_CHUNKS = D_MODEL // (2 * LANES)


def _cparams(sem):
    return pltpu.CompilerParams(dimension_semantics=sem, vmem_limit_bytes=VMEM_LIMIT)


def _rms(x, gain):
    return x * lax.rsqrt(jnp.mean(x * x, axis=-1, keepdims=True) + RMS_EPS) * gain


HIGH_HALF = 0xFFFF0000


def _load_row_tiles(ref, lead=(), first=0, n=None):
    n = ref.shape[-2] // ROW_CHUNKS if n is None else n
    bits = jnp.concatenate([ref[lead + (pl.ds(first * ROW_CHUNKS + c, n, stride=ROW_CHUNKS), slice(None))]
                            for c in range(ROW_CHUNKS)], axis=1)
    low = lax.bitcast_convert_type(bits << 16, F32)
    high = lax.bitcast_convert_type(bits & jnp.uint32(HIGH_HALF), F32)
    return jnp.concatenate([low, high], axis=1)


def _store_row_tiles(ref, value, lead=(), first=0):
    n, half = value.shape[0], value.shape[1] // 2
    bf16_bits = lambda t: lax.bitcast_convert_type(t.astype(BF16).astype(F32), jnp.uint32)
    bits = (bf16_bits(value[:, :half]) >> 16) | (bf16_bits(value[:, half:]) & jnp.uint32(HIGH_HALF))
    for c in range(ROW_CHUNKS):
        rows = pl.ds(first * ROW_CHUNKS + c, n, stride=ROW_CHUNKS)
        ref[lead + (rows, slice(None))] = bits[:, c * LANES:(c + 1) * LANES]


def _head_norm(t, gain, bd):
    tt = t * t
    hi = tt.astype(BF16)
    lo = (tt - hi.astype(F32)).astype(BF16)
    ss = jnp.dot(hi, bd, preferred_element_type=F32) + jnp.dot(lo, bd, preferred_element_type=F32)
    return t * lax.rsqrt(ss * (1.0 / ATT_HEAD_DIM) + RMS_EPS) * gain


def _inproj_even_kernel(x_ref, g_ref, w_ref, bd_ref, qn_ref, kn_ref, u_ref, q_ref, k_ref, v_ref):
    half = x_ref.shape[0] // ROW_GROUPS
    parts = [slice(g * half, (g + 1) * half) for g in range(ROW_GROUPS)]
    hn = [_rms(x_ref[rows, :], g_ref[...]).astype(BF16) for rows in parts]
    projs = [jnp.dot(t, w_ref[...], preferred_element_type=F32) for t in hn]
    bd = bd_ref[...]
    o = SSM_WIDTH
    for rows, proj in zip(parts, projs):
        for j in range(SSM_SLABS):
            u_ref[j, rows, :] = proj[:, j * LANES:(j + 1) * LANES]
        q = _head_norm(proj[:, o:o + ATT_WIDTH], qn_ref[...], bd) * (ATT_HEAD_DIM ** -0.5)
        k = _head_norm(proj[:, o + ATT_WIDTH:o + 2 * ATT_WIDTH], kn_ref[...], bd)
        v = proj[:, o + 2 * ATT_WIDTH:o + 3 * ATT_WIDTH]
        for j in range(ATT_WIDTH // LANES):
            q_ref[j, rows, :] = q[:, j * LANES:(j + 1) * LANES]
            k_ref[j, rows, :] = k[:, j * LANES:(j + 1) * LANES]
            v_ref[j, rows, :] = v[:, j * LANES:(j + 1) * LANES]


def _inproj_even(x2, gain, w_in, q_norm, k_norm):
    n_tok = x2.shape[0]
    n_slab = ATT_WIDTH // LANES
    head_of = jnp.arange(ATT_WIDTH) // ATT_HEAD_DIM
    bd = (head_of[:, None] == head_of[None, :]).astype(BF16)
    qn = jnp.tile(q_norm.astype(F32), ATT_HEADS)[None]
    kn = jnp.tile(k_norm.astype(F32), ATT_HEADS)[None]
    full = lambda shape: pl.BlockSpec(shape, lambda i: (0,) * len(shape))
    slab = pl.BlockSpec((n_slab, TOK_TILE, LANES), lambda i: (0, i, 0))
    slab_shape = jax.ShapeDtypeStruct((n_slab, n_tok, LANES), F32)
    return pl.pallas_call(
        _inproj_even_kernel,
        grid=(n_tok // TOK_TILE,),
        in_specs=[pl.BlockSpec((TOK_TILE, D_MODEL), lambda i: (i, 0)), full((1, D_MODEL)),
                  full(w_in.shape), full(bd.shape), full(qn.shape), full(kn.shape)],
        out_specs=[pl.BlockSpec((SSM_SLABS, TOK_TILE, LANES), lambda i: (0, i, 0)), slab, slab, slab],
        out_shape=[jax.ShapeDtypeStruct((SSM_SLABS, n_tok, LANES), F32), slab_shape, slab_shape, slab_shape],
        compiler_params=_cparams(("parallel",)),
        name="inproj_even",
    )(x2, gain[None].astype(F32), w_in.astype(BF16), bd, qn, kn)


def _s5_tables(a_re, a_im, b_re, b_im, c_re, c_im, d_skip, log_step):
    f = lambda t: t.astype(F32)
    a_re, a_im, b_re, b_im, c_re, c_im = map(f, (a_re, a_im, b_re, b_im, c_re, c_im))
    L = SSM_CHUNK
    step = jnp.exp(f(log_step))[:, None]
    ks = jnp.arange(L + 1, dtype=F32)[:, None, None]
    mag = jnp.exp(ks * (a_re * step)[None])
    ang = ks * (a_im * step)[None]
    pw_re, pw_im = mag * jnp.cos(ang), mag * jnp.sin(ang)
    nr, ni = pw_re[1] - 1.0, pw_im[1]
    den = a_re * a_re + a_im * a_im
    z_re, z_im = (nr * a_re + ni * a_im) / den, (ni * a_re - nr * a_im) / den
    bb_re = z_re[..., None] * b_re - z_im[..., None] * b_im
    bb_im = z_re[..., None] * b_im + z_im[..., None] * b_re
    lb_re = pw_re[..., None] * bb_re[None] - pw_im[..., None] * bb_im[None]
    lb_im = pw_re[..., None] * bb_im[None] + pw_im[..., None] * bb_re[None]
    kk = jnp.einsum('gop,kgpi->gkio', c_re, lb_re[:L]) - jnp.einsum('gop,kgpi->gkio', c_im, lb_im[:L])
    ti = jnp.arange(L)
    lag = ti[None, :] - ti[:, None]
    m = jnp.where((lag >= 0)[None, :, :, None, None], kk[:, jnp.maximum(lag, 0)], 0.0)
    ns, gs = SSM_SLABS, SSM_SLAB_GROUPS
    lw = L * LANES
    pm = m.reshape(ns, gs, L, L, SSM_GROUP, SSM_GROUP).transpose(0, 2, 1, 4, 3, 5).reshape(ns, lw, L * SSM_GROUP)
    fold_e = lambda t: t[:L][::-1].reshape(L, ns, gs, SSM_STATE, SSM_GROUP).transpose(1, 0, 2, 4, 3).reshape(ns, lw, SSM_STATE)
    pe = jnp.concatenate([fold_e(lb_re), fold_e(lb_im)], axis=-1)
    pw1_re, pw1_im = (t[1:].transpose(1, 0, 2)[:, :, None, :] for t in (pw_re, pw_im))
    cl_re = c_re[:, None] * pw1_re - c_im[:, None] * pw1_im
    cl_im = c_re[:, None] * pw1_im + c_im[:, None] * pw1_re
    fold_f = lambda t: t.reshape(ns, gs, L, SSM_GROUP, SSM_STATE).transpose(0, 4, 2, 1, 3).reshape(ns, SSM_STATE, lw)
    pf = jnp.concatenate([fold_f(cl_re), fold_f(-cl_im)], axis=1)
    per_chain = lambda t: jnp.tile(t.reshape(ns, gs * SSM_STATE // LANES, LANES), (1, 2, 1))
    d_vec = jnp.tile(f(d_skip).reshape(ns, 1, LANES), (1, 1, L))
    return pm.astype(BF16), pe.astype(BF16), pf.astype(BF16), per_chain(pw_re[L]), per_chain(pw_im[L]), d_vec


def _log2(n):
    assert n & (n - 1) == 0, "index fields are split with shifts"
    return n.bit_length() - 1


def _iota2(shape):
    return lax.broadcasted_iota(jnp.int32, shape, 0), lax.broadcasted_iota(jnp.int32, shape, 1)


def _widen(compact, group_major_cols, shape, r_shift, c_shift, sel_rows):
    gmask = SSM_SLAB_GROUPS - 1
    state_bits, wide_state_bits = _log2(SSM_STATE), _log2(SSM_SLAB_GROUPS * SSM_STATE)
    chan_bits, lane_bits = _log2(SSM_GROUP), _log2(LANES)
    if sel_rows:
        r, c = _iota2((shape[0], compact.shape[0]))
        sel = ((r >> wide_state_bits) == (c >> state_bits)) & ((r & (SSM_STATE - 1)) == (c & (SSM_STATE - 1)))
        wide = jnp.dot(sel.astype(BF16), compact, preferred_element_type=F32)
    else:
        r, c = _iota2((compact.shape[1], shape[1]))
        if group_major_cols:
            sel = ((c >> wide_state_bits) == (r >> state_bits)) & ((c & (SSM_STATE - 1)) == (r & (SSM_STATE - 1)))
        else:
            sel = ((c >> lane_bits) == (r >> chan_bits)) & ((c & (SSM_GROUP - 1)) == (r & (SSM_GROUP - 1)))
        wide = jnp.dot(compact, sel.astype(BF16), preferred_element_type=F32)
    r, c = _iota2(shape)
    keep = ((r >> r_shift) & gmask) == ((c >> c_shift) & gmask)
    return jnp.where(keep, wide, 0.0).astype(BF16)


def _s5_kernel(u_ref, pm_ref, pe_ref, pf_ref, ar_ref, ai_ref, d_ref, y_ref, m_ref, e_ref, f_ref, x_ref, sr_ref, si_ref,
               *, n_chunk, pitch):
    L = SSM_CHUNK
    n_blk = SSM_SLAB_GROUPS * SSM_STATE // LANES
    n_re = n_blk * LANES
    lw = L * LANES
    chan_bits, state_bits = _log2(SSM_GROUP), _log2(SSM_STATE)
    m_ref[...] = _widen(pm_ref[...], False, (lw, lw), chan_bits, chan_bits, False)
    e_ref[...] = _widen(pe_ref[...], True, (lw, 2 * n_re), chan_bits, state_bits, False)
    f_ref[...] = _widen(pf_ref[...], False, (2 * n_re, lw), state_bits, chan_bits, True)
    tiles = [(b, c0) for b in range(2) for c0 in range(0, n_chunk, S5_ROW_TILE)]
    for b, c0 in tiles:
        r0 = b * n_chunk + c0
        for t in range(L):
            x_ref[r0:r0 + S5_ROW_TILE, t * LANES:(t + 1) * LANES] = (
                u_ref[pl.ds(r0 * L + t, S5_ROW_TILE, stride=L), :].astype(BF16))
        sl = jnp.dot(x_ref[r0:r0 + S5_ROW_TILE, :], e_ref[...], preferred_element_type=F32)
        for j in range(n_blk):
            base = (b * n_blk + j) * pitch + c0
            sr_ref[base:base + S5_ROW_TILE, :] = sl[:, j * LANES:(j + 1) * LANES]
            si_ref[base:base + S5_ROW_TILE, :] = sl[:, n_re + j * LANES:n_re + (j + 1) * LANES]
    ar, ai = ar_ref[...], ai_ref[...]
    half = LANES

    def scan_step(c, carry):
        s_re, s_im = carry
        rows = pl.ds(c, SUBLANES, stride=pitch)
        x_re, x_im = sr_ref[rows, :], si_ref[rows, :]
        sr_ref[rows, :] = s_re
        si_ref[rows, :] = s_im
        return ar * s_re - ai * s_im + x_re, ar * s_im + ai * s_re + x_im

    zero = jnp.zeros((SUBLANES, half), F32)
    lax.fori_loop(0, n_chunk, scan_step, (zero, zero), unroll=8)

    for b, c0 in tiles:
        r0 = b * n_chunk + c0
        chain = lambda ref, j: ref[(b * n_blk + j) * pitch + c0:(b * n_blk + j) * pitch + c0 + S5_ROW_TILE, :]
        sp = jnp.concatenate([chain(sr_ref, j) for j in range(n_blk)] + [chain(si_ref, j) for j in range(n_blk)],
                             axis=1).astype(BF16)
        xt = x_ref[r0:r0 + S5_ROW_TILE, :]
        y = (jnp.dot(xt, m_ref[...], preferred_element_type=F32)
             + jnp.dot(sp, f_ref[...], preferred_element_type=F32)
             + d_ref[...] * xt.astype(F32))
        for t in range(L):
            y_ref[pl.ds(r0 * L + t, S5_ROW_TILE, stride=L), :] = y[:, t * LANES:(t + 1) * LANES]


def _s5_core(u, tables, bsz, s_len):
    assert bsz == 2, "the scan packs (batch, lane block) into the 8 sublanes of one vreg"
    pm, pe, pf, a_r, a_i, d_vec = tables
    n_tok = bsz * s_len
    n_chunk = s_len // SSM_CHUNK
    pitch = n_chunk + SUBLANES
    lw = SSM_CHUNK * LANES
    n_state = 2 * SSM_SLAB_GROUPS * SSM_STATE
    slab = lambda shape, **kw: pl.BlockSpec((None,) + shape, lambda i: (i,) + (0,) * len(shape), **kw)
    return pl.pallas_call(
        functools.partial(_s5_kernel, n_chunk=n_chunk, pitch=pitch),
        grid=(SSM_SLABS,),
        in_specs=[slab((n_tok, LANES)), slab(pm.shape[1:]), slab(pe.shape[1:]), slab(pf.shape[1:]),
                  slab((SUBLANES, LANES)), slab((SUBLANES, LANES)), slab((1, lw))],
        out_specs=slab((n_tok, LANES)),
        out_shape=jax.ShapeDtypeStruct((SSM_SLABS, n_tok, LANES), F32),
        scratch_shapes=[pltpu.VMEM((lw, lw), BF16), pltpu.VMEM((lw, n_state), BF16), pltpu.VMEM((n_state, lw), BF16),
                        pltpu.VMEM((bsz * n_chunk, lw), BF16),
                        pltpu.VMEM((SUBLANES * pitch, LANES), F32),
                        pltpu.VMEM((SUBLANES * pitch, LANES), F32)],
        compiler_params=_cparams(("parallel",)),
        name="s5_core",
    )(u, pm, pe, pf, a_r, a_i, d_vec)


def _attn_kernel(slope_ref, q_ref, kp_ref, kc_ref, vp_ref, vc_ref, o_ref,
                 q4_ref, k4_ref, v4_ref, k1_ref, v1_ref, m_ref, l_ref, acc_ref):
    slab = pl.program_id(1)
    sb = pl.program_id(2)
    fine = DILATIONS[1]
    nq, nk = ATT_SB // fine, 2 * ATT_SB // fine
    for r in range(fine):
        q4_ref[r * nq:(r + 1) * nq, :] = q_ref[pl.ds(r, nq, stride=fine), :]
        for dst, prev, cur in ((k4_ref, kp_ref, kc_ref), (v4_ref, vp_ref, vc_ref)):
            dst[r * nk:r * nk + nq, :] = prev[pl.ds(r, nq, stride=fine), :]
            dst[r * nk + nq:(r + 1) * nk, :] = cur[pl.ds(r, nq, stride=fine), :]
    for dst, prev, cur in ((k1_ref, kp_ref, kc_ref), (v1_ref, vp_ref, vc_ref)):
        dst[0:ATT_BLK, :] = prev[ATT_SB - ATT_BLK:ATT_SB, :]
        dst[ATT_BLK:ATT_BLK + ATT_SB, :] = cur[...]

    lane = lax.broadcasted_iota(jnp.int32, (ATT_BLK, LANES), 1)
    head0 = lane < ATT_HEAD_DIM
    qi = lax.broadcasted_iota(jnp.int32, (ATT_BLK, 2 * ATT_BLK), 0)
    kj = lax.broadcasted_iota(jnp.int32, (ATT_BLK, 2 * ATT_BLK), 1)
    back = qi + ATT_BLK - kj
    band = (back >= 0) & (back <= ATT_BLK)
    neg_steps = -back.astype(F32)
    slopes = (slope_ref[2 * slab], slope_ref[2 * slab + 1])

    for pat, dil in enumerate(DILATIONS):
        span = ATT_BLK * dil

        def tile(idx, carry, dil=dil, span=span, pat=pat):
            start = (idx // dil) * span + idx % dil
            seq_ok = jnp.logical_or(sb > 0, idx >= dil)
            valid = band & ((kj >= ATT_BLK) | seq_ok)
            if dil == 1:
                start = pl.multiple_of(start, ATT_BLK)
                rows = pl.ds(start, ATT_BLK)
                qt, k_src, v_src, k_rows = q_ref[rows, :], k1_ref, v1_ref, pl.ds(start, 2 * ATT_BLK)
            elif dil == fine:
                first = pl.multiple_of((idx // fine) * ATT_BLK, ATT_BLK)
                rows = pl.ds(start, ATT_BLK, stride=dil)
                qt = q4_ref[pl.ds((idx % fine) * nq + first, ATT_BLK), :]
                k_src, v_src = k4_ref, v4_ref
                k_rows = pl.ds((idx % fine) * nk + (nq - ATT_BLK) + first, 2 * ATT_BLK)
            else:
                rows = pl.ds(start, ATT_BLK, stride=dil)
                qt = q4_ref[pl.ds((idx % fine) * nq + idx // fine, ATT_BLK, stride=fine), :]
                k_src, v_src = k4_ref, v4_ref
                k_rows = pl.ds((idx % fine) * nk + idx // fine, 2 * ATT_BLK, stride=fine)
            kt = k_src[k_rows, :].astype(BF16)
            vt = v_src[k_rows, :].astype(BF16)
            q0 = jnp.where(head0, qt, 0.0)
            scores = [lax.dot_general(qh.astype(BF16), kt, (((1,), (1,)), ((), ())), preferred_element_type=F32)
                      for qh in (q0, qt - q0)]
            return rows, valid, vt, scores

        def softmax(valid, s, hh, dil=dil):
            s = jnp.where(valid, s + (slopes[hh] * float(dil)) * neg_steps, NEG_INF)
            m = jnp.max(s, axis=-1, keepdims=True)
            p = jnp.exp(s - m)
            return m, jnp.sum(p, axis=-1, keepdims=True), p.astype(BF16)

        def merge(rows, parts, pat=pat):
            (m0, l0, o0), (m1, l1, o1) = parts
            m_t = jnp.where(head0, m0, m1)
            l_t = jnp.where(head0, l0, l1)
            o_t = jnp.where(head0, o0, o1)
            if pat == 0:
                m_ref[rows, :] = m_t
                l_ref[rows, :] = l_t
                acc_ref[rows, :] = o_t
            else:
                m_old = m_ref[rows, :]
                m_new = jnp.maximum(m_old, m_t)
                a = jnp.exp(m_old - m_new)
                b = jnp.exp(m_t - m_new)
                m_ref[rows, :] = m_new
                l_ref[rows, :] = a * l_ref[rows, :] + b * l_t
                acc_ref[rows, :] = a * acc_ref[rows, :] + b * o_t

        def group(gi, carry):
            tiles = [tile(gi * ATT_GROUP + t, None) for t in range(ATT_GROUP)]
            soft = [[softmax(valid, s, hh) for hh, s in enumerate(scores)] for _, valid, _, scores in tiles]
            outs = [[(m, l, jnp.dot(p, vt, preferred_element_type=F32)) for m, l, p in per_head]
                    for (_, _, vt, _), per_head in zip(tiles, soft)]
            for (rows, _, _, _), parts in zip(tiles, outs):
                merge(rows, parts)
            return carry

        lax.fori_loop(0, ATT_SB // ATT_BLK // ATT_GROUP, group, 0)

    o_ref[...] = (acc_ref[...] / l_ref[...]).astype(o_ref.dtype)


def _dilated_attention(q, k, v, bsz, s_len):
    assert DILATIONS == (DILATIONS[1] ** 2, DILATIONS[1], 1) and ATT_SB == ATT_BLK * DILATIONS[0]
    n_slab = q.shape[0]
    shape4 = (n_slab, bsz, s_len, LANES)
    q, k, v = (t.reshape(shape4) for t in (q, k, v))
    slopes = jnp.asarray([2.0 ** (-8.0 * (h + 1) / ATT_HEADS) for h in range(ATT_HEADS)], F32)
    blk = (None, None, ATT_SB, LANES)
    cur = pl.BlockSpec(blk, lambda b, j, i, s: (j, b, i, 0))
    prev = pl.BlockSpec(blk, lambda b, j, i, s: (j, b, jnp.maximum(i - 1, 0), 0))
    out = pl.pallas_call(
        _attn_kernel,
        grid_spec=pltpu.PrefetchScalarGridSpec(
            num_scalar_prefetch=1,
            grid=(bsz, n_slab, s_len // ATT_SB),
            in_specs=[cur, prev, cur, prev, cur],
            out_specs=cur,
            scratch_shapes=[pltpu.VMEM((ATT_SB, LANES), F32),
                            pltpu.VMEM((2 * ATT_SB, LANES), F32), pltpu.VMEM((2 * ATT_SB, LANES), F32),
                            pltpu.VMEM((ATT_BLK + ATT_SB, LANES), F32), pltpu.VMEM((ATT_BLK + ATT_SB, LANES), F32),
                            pltpu.VMEM((ATT_SB, LANES), F32), pltpu.VMEM((ATT_SB, LANES), F32),
                            pltpu.VMEM((ATT_SB, LANES), F32)]),
        out_shape=jax.ShapeDtypeStruct(shape4, BF16),
        compiler_params=_cparams(("parallel", "parallel", "parallel")),
        name="dilated_attn",
    )(slopes, q, k, k, v, v)
    return out.reshape(n_slab, bsz * s_len, LANES)


def _route_epilogue(h_parts, gain_ref, wr_ref, br_ref, tri_ref, cnt_ref, h_ref, hn_ref, ids_ref, gate_ref, cnt_out_ref):
    sub = h_parts[0].shape[0]
    tm = sub * len(h_parts)
    splits = []
    for g, h in enumerate(h_parts):
        h_ref[g * sub:(g + 1) * sub, :] = h
        hn = _rms(h, gain_ref[...])
        _store_row_tiles(hn_ref, hn, first=g * sub)
        hn_hi = hn.astype(BF16)
        splits.append((hn_hi, (hn - hn_hi.astype(F32)).astype(BF16)))
    logit_parts = []
    for hn_hi, hn_lo in splits:
        hi_part = jnp.dot(hn_hi, wr_ref[...], preferred_element_type=F32)
        logit_parts.append(hi_part[:, :LANES] + (hi_part[:, LANES:]
                                                 + jnp.dot(hn_lo, wr_ref[:, :LANES], preferred_element_type=F32)))
    logits = jnp.concatenate(logit_parts, axis=0) + br_ref[...]
    lane = lax.broadcasted_iota(jnp.int32, (tm, LANES), 1)
    big = jnp.int32(LANES)
    rmax = lambda t: jnp.max(t, axis=-1, keepdims=True)
    rmin = lambda t: jnp.min(t, axis=-1, keepdims=True)
    rsum = lambda t: jnp.sum(t, axis=-1, keepdims=True)
    gmask = lane < N_GROUPS
    gl = jnp.where(gmask, logits, -jnp.inf)
    gmax = rmax(gl)
    ge = jnp.where(gmask, jnp.exp(gl - gmax), 0.0)
    gprob = ge / rsum(ge)
    g_w = rmax(gprob)
    grp = rmin(jnp.where(gmask & (gprob == g_w), lane, big))
    group_of_lane = (lane - N_GROUPS) >> int(math.log2(EXPERTS_PER_GROUP))
    emask = (lane >= N_GROUPS) & (lane < N_GROUPS + N_EXPERTS) & (group_of_lane == grp)
    el = jnp.where(emask, logits, -jnp.inf)
    ee = jnp.where(emask, jnp.exp(el - rmax(el)), 0.0)
    ep = jnp.where(emask, ee / rsum(ee), -1.0)
    p1 = rmax(ep)
    i1 = rmin(jnp.where(ep == p1, lane, big))
    ep2 = jnp.where(lane == i1, -1.0, ep)
    p2 = rmax(ep2)
    i2 = rmin(jnp.where(ep2 == p2, lane, big))
    e1, e2 = i1 - N_GROUPS, i2 - N_GROUPS
    psum = p1 + p2
    gate1, gate2 = g_w * p1 / psum, g_w * p2 / psum
    oh1, oh2 = lane == e1, lane == e2
    member = (oh1 | oh2).astype(BF16)
    before = jnp.dot(tri_ref[...], member, preferred_element_type=F32) + cnt_ref[...]
    r1 = rsum(jnp.where(oh1, before, 0.0)).astype(jnp.int32)
    r2 = rsum(jnp.where(oh2, before, 0.0)).astype(jnp.int32)
    cnt_ref[...] = cnt_ref[...] + jnp.sum(member.astype(F32), axis=0, keepdims=True)
    ids = jnp.where(lane == 0, e1, jnp.where(lane == 1, e2, jnp.where(lane == 2, r1, jnp.where(lane == 3, r2, 0))))
    ids_ref[...] = jnp.transpose(ids)[0:SUBLANES, :]
    gate_ref[...] = jnp.where(lane == 0, gate1, jnp.where(lane == 1, gate2, 0.0))
    cnt_out_ref[...] = jnp.broadcast_to(cnt_ref[...], cnt_out_ref.shape)


def _router_operands(norm_gain, w_rg, b_rg, w_re, b_re):
    pad = LANES - N_GROUPS - N_EXPERTS
    wr = jnp.pad(jnp.concatenate([w_rg, w_re], axis=1).astype(F32), ((0, 0), (0, pad)))
    br = jnp.pad(jnp.concatenate([b_rg, b_re]).astype(F32), (0, pad))[None]
    r = jnp.arange(TOK_TILE)
    tri = (r[None, :] < r[:, None]).astype(BF16)
    wr_hi = wr.astype(BF16)
    wr_lo = (wr - wr_hi.astype(F32)).astype(BF16)
    return norm_gain[None].astype(F32), jnp.concatenate([wr_hi, wr_lo], axis=1), br, tri


def _route_specs(n_tok):
    full = lambda shape: pl.BlockSpec(shape, lambda i: (0,) * len(shape))
    in_specs = [full((1, D_MODEL)), full((D_MODEL, 2 * LANES)), full((1, LANES)), full((TOK_TILE, TOK_TILE))]
    tok = lambda w: pl.BlockSpec((TOK_TILE, w), lambda i: (i, 0))
    out_specs = [tok(D_MODEL), pl.BlockSpec((TOK_TILE * ROW_CHUNKS, LANES), lambda i: (i, 0)),
                 pl.BlockSpec((SUBLANES, TOK_TILE), lambda i: (0, i)), tok(LANES), full((SUBLANES, LANES))]
    out_shape = [jax.ShapeDtypeStruct((n_tok, D_MODEL), F32), jax.ShapeDtypeStruct((n_tok * ROW_CHUNKS, LANES), jnp.uint32),
                 jax.ShapeDtypeStruct((SUBLANES, n_tok), jnp.int32), jax.ShapeDtypeStruct((n_tok, LANES), F32),
                 jax.ShapeDtypeStruct((SUBLANES, LANES), F32)]
    return in_specs, out_specs, out_shape


def _gelu_tanh(x):
    return 0.5 * x * (1.0 + jnp.tanh(math.sqrt(2.0 / math.pi) * (x + 0.044715 * (x * x * x))))


def _outproj_even_kernel(x_ref, y_ref, a_ref, wglu_ref, bglu_ref, wout_ref, gain_ref, wr_ref, br_ref, tri_ref,
                         h_ref, hn_ref, ids_ref, gate_ref, cnt_out_ref, cnt_ref):
    @pl.when(pl.program_id(0) == 0)
    def _():
        cnt_ref[...] = jnp.zeros_like(cnt_ref)

    half = x_ref.shape[0] // ROW_GROUPS
    parts = [slice(g * half, (g + 1) * half) for g in range(ROW_GROUPS)]
    y = [_gelu_tanh(jnp.concatenate([y_ref[j, rows, :] for j in range(SSM_SLABS)], axis=1)) for rows in parts]
    glu = [jnp.dot(t.astype(BF16), wglu_ref[...], preferred_element_type=F32) for t in y]
    h_out = []
    for rows, t, g in zip(parts, y, glu):
        t = t * jax.nn.sigmoid(g + bglu_ref[...])
        mix = jnp.dot(t.astype(BF16), wout_ref[0:SSM_WIDTH, :], preferred_element_type=F32)
        for j in range(ATT_WIDTH // LANES):
            w_rows = slice(SSM_WIDTH + j * LANES, SSM_WIDTH + (j + 1) * LANES)
            mix = mix + jnp.dot(a_ref[j, rows, :], wout_ref[w_rows, :], preferred_element_type=F32)
        h_out.append(x_ref[rows, :] + mix)
    _route_epilogue(h_out, gain_ref, wr_ref, br_ref, tri_ref, cnt_ref,
                    h_ref, hn_ref, ids_ref, gate_ref, cnt_out_ref)


def _outproj_even(x2, y_pre, attn, w_glu, b_glu, w_out, route_ops):
    n_tok = x2.shape[0]
    n_slab = attn.shape[0]
    r_in, r_out, r_shape = _route_specs(n_tok)
    full = lambda shape: pl.BlockSpec(shape, lambda i: (0,) * len(shape))
    return pl.pallas_call(
        _outproj_even_kernel,
        grid=(n_tok // TOK_TILE,),
        in_specs=[pl.BlockSpec((TOK_TILE, D_MODEL), lambda i: (i, 0)),
                  pl.BlockSpec((SSM_SLABS, TOK_TILE, LANES), lambda i: (0, i, 0)),
                  pl.BlockSpec((n_slab, TOK_TILE, LANES), lambda i: (0, i, 0)),
                  full(w_glu.shape), full((1, SSM_WIDTH)), full(w_out.shape)] + r_in,
        out_specs=r_out, out_shape=r_shape,
        scratch_shapes=[pltpu.VMEM((1, LANES), F32)],
        compiler_params=_cparams(("arbitrary",)),
        name="outproj_even",
    )(x2, y_pre, attn, w_glu.astype(BF16), b_glu[None].astype(F32), w_out.astype(BF16), *route_ops)


def _conv_layer_kernel(h_ref, pgate_ref, y0_ref, y1_ref, gmix_ref, win_ref, cw_ref, wout_ref, gain_ref, wr_ref, br_ref,
                       tri_ref, ho_ref, hn_ref, ids_ref, gate_ref, cnt_out_ref, cnt_ref, zc_ref, *, tiles_per_seq):
    i = pl.program_id(0)

    @pl.when(i == 0)
    def _():
        cnt_ref[...] = jnp.zeros_like(cnt_ref)

    @pl.when(i % tiles_per_seq == 0)
    def _():
        zc_ref[0:SUBLANES, :] = jnp.zeros((SUBLANES, D_MODEL), F32)

    tm = h_ref.shape[0]
    c = D_MODEL
    half = tm // ROW_GROUPS
    parts = tuple(g * half for g in range(ROW_GROUPS))
    h_in, hn = {}, {}
    for r0 in parts:
        rows = slice(r0, r0 + half)
        pgate = pgate_ref[rows, :]
        h_in[r0] = (h_ref[rows, :] + pgate[:, 0:1] * _load_row_tiles(y0_ref, first=r0, n=half)
                    + pgate[:, 1:2] * _load_row_tiles(y1_ref, first=r0, n=half))
        hn[r0] = _rms(h_in[r0], gmix_ref[...]).astype(BF16)
    b_gate, zc = {}, {}
    for r0 in parts:
        b_gate[r0] = jnp.dot(hn[r0], win_ref[:, 0:c], preferred_element_type=F32)
        zc[r0] = (jnp.dot(hn[r0], win_ref[:, c:2 * c], preferred_element_type=F32)
                  * jnp.dot(hn[r0], win_ref[:, 2 * c:3 * c], preferred_element_type=F32))
    h_out = []
    for r0 in parts:
        z0 = SUBLANES + r0
        zc_ref[z0:z0 + half, :] = zc[r0]
        conv = cw_ref[CONV_TAPS - 1:CONV_TAPS, :] * zc[r0]
        for back in range(1, CONV_TAPS):
            tap = CONV_TAPS - 1 - back
            conv = conv + cw_ref[tap:tap + 1, :] * zc_ref[z0 - back:z0 - back + half, :]
        mix = jnp.dot((b_gate[r0] * conv).astype(BF16), wout_ref[...], preferred_element_type=F32)
        h_out.append(h_in[r0] + mix)
    zc_ref[0:SUBLANES, :] = zc_ref[tm:tm + SUBLANES, :]
    _route_epilogue(h_out, gain_ref, wr_ref, br_ref, tri_ref, cnt_ref,
                    ho_ref, hn_ref, ids_ref, gate_ref, cnt_out_ref)


def _conv_layer(h1, pgate, y2, gain_mix, w_in, conv_w, w_out, route_ops, s_len):
    n_tok = h1.shape[0]
    r_in, r_out, r_shape = _route_specs(n_tok)
    full = lambda shape: pl.BlockSpec(shape, lambda i: (0,) * len(shape))
    return pl.pallas_call(
        functools.partial(_conv_layer_kernel, tiles_per_seq=s_len // TOK_TILE),
        grid=(n_tok // TOK_TILE,),
        in_specs=_moe_specs(n_tok) + [full((1, D_MODEL)), full(w_in.shape), full(conv_w.shape), full(w_out.shape)] + r_in,
        out_specs=r_out, out_shape=r_shape,
        scratch_shapes=[pltpu.VMEM((1, LANES), F32), pltpu.VMEM((TOK_TILE + SUBLANES, D_MODEL), F32)],
        compiler_params=_cparams(("arbitrary",)),
        name="conv_layer",
    )(h1, pgate, y2, y2, gain_mix[None].astype(F32), w_in.astype(BF16), conv_w.astype(F32), w_out.astype(BF16),
      *route_ops)


def _row_map(experts, ranks, first_row, n_rows):
    n_slot, n_tok = len(experts), experts[0].shape[0]
    n_assign = n_slot * n_tok
    n_src = n_rows + 2 * MOE_BLK
    mesh = plsc.VectorSubcoreMesh(core_axis_name="core", subcore_axis_name="subcore",
                                  num_cores=SC_CORES, num_subcores=SC_SUBCORES)
    tok_vec = pltpu.VMEM((n_tok,), jnp.int32)

    @functools.partial(
        pl.kernel, mesh=mesh, out_type=jax.ShapeDtypeStruct((n_src,), jnp.int32),
        scratch_types=[tok_vec] * (2 * n_slot) + [pltpu.VMEM((N_EXPERTS,), jnp.int32), pltpu.VMEM((n_src,), jnp.int32)],
        compiler_params=pltpu.CompilerParams(needs_layout_passes=False), name="moe_row_map")
    def row_map(*refs):
        ins, src_hbm, scratch = refs[:2 * n_slot + 2], refs[2 * n_slot + 2], refs[2 * n_slot + 3:]
        first_v, src_v = scratch[2 * n_slot], scratch[2 * n_slot + 1]

        @pl.when(jnp.logical_and(lax.axis_index("core") == 0, lax.axis_index("subcore") == 0))
        def _():
            for hbm, vmem in zip(ins, scratch):
                pltpu.sync_copy(hbm, vmem)
            lanes = lax.iota(jnp.int32, SC_LANES)

            for slot in range(n_slot):
                e_v, r_v = scratch[slot], scratch[n_slot + slot]

                @pl.loop(0, n_tok, step=SC_LANES)
                def _(t):
                    row = plsc.load_gather(first_v, [e_v[pl.ds(t, SC_LANES)]]) + r_v[pl.ds(t, SC_LANES)]
                    plsc.store_scatter(src_v, [row + MOE_BLK], slot * n_tok + t + lanes)

            pltpu.sync_copy(src_v, src_hbm)

    dump_rows = n_assign + (jnp.arange(n_src, dtype=jnp.int32) & (MOE_BLK - 1))
    return row_map(*experts, *ranks, first_row, dump_rows)


def _expert_kernel(blk_e_ref, n_used_ref, src_ref, next_e_ref, wslot_ref, hn_ref, wg_ref, wu_ref, wd_ref, y2_ref,
                   xbuf, ybuf, wg_f, wu_f, wd_f, wg_s, wu_s, wd_s, ssem, wsem, *, n_tok, layer):
    b = pl.program_id(0)
    n_used = n_used_ref[0]
    cur = b % 2
    nxt = 1 - cur

    tile = lambda i: pl.ds(i * ROW_CHUNKS, ROW_CHUNKS)

    def scatter(blk, slot, i):
        row = src_ref[(blk + 1) * MOE_BLK + i]
        return pltpu.make_async_copy(ybuf.at[slot, tile(i)], y2_ref.at[row], ssem.at[slot])

    wait_block = lambda slot: pltpu.make_async_copy(ybuf.at[slot], ybuf.at[slot], ssem.at[slot]).wait()

    def weights(expert, slot, act):
        for hbm, buf in ((wg_ref, wg_f), (wu_ref, wu_f), (wd_ref, wd_f)):
            act(pltpu.make_async_copy(hbm.at[layer, expert], buf.at[slot], wsem.at[slot]))

    @pl.when(b == 0)
    def _():
        ybuf[1] = jnp.zeros(ybuf.shape[1:], ybuf.dtype)
        weights(blk_e_ref[0], wslot_ref[0], lambda c: c.start())

    @pl.when(b < n_used)
    def _():
        new_expert = jnp.logical_or(b == 0, blk_e_ref[b] != blk_e_ref[jnp.maximum(b - 1, 0)])

        @pl.when(new_expert)
        def _():
            slot = wslot_ref[b]
            weights(blk_e_ref[b], slot, lambda c: c.wait())
            wg_s[...] = wg_f[slot].astype(BF16)
            wu_s[...] = wu_f[slot].astype(BF16)
            wd_s[...] = wd_f[slot].astype(BF16)

            @pl.when(next_e_ref[b] >= 0)
            def _():
                weights(next_e_ref[b], 1 - slot, lambda c: c.start())

        @pl.when(b >= 1)
        def _():
            wait_block(cur)

        for i in range(MOE_BLK):
            tok = src_ref[(b + 1) * MOE_BLK + i] & (n_tok - 1)
            xbuf[tile(i), :] = hn_ref[pl.ds(pl.multiple_of(tok * ROW_CHUNKS, ROW_CHUNKS), ROW_CHUNKS), :]
            scatter(b - 1, nxt, i).start(priority=i % 2)
        x = _load_row_tiles(xbuf).astype(BF16)
        g = jnp.dot(x, wg_s[...], preferred_element_type=F32)
        u = jnp.dot(x, wu_s[...], preferred_element_type=F32)
        hb = (g * jax.nn.sigmoid(g) * u).astype(BF16)
        _store_row_tiles(ybuf, jnp.dot(hb, wd_s[...], preferred_element_type=F32), (cur,))

    @pl.when(b == n_used)
    def _():
        wait_block(cur)
        for i in range(MOE_BLK):
            scatter(b - 1, nxt, i).start(priority=i % 2)
        wait_block(nxt)


def _expert_ffn(hn, src, blk_e, n_used, next_e, wslot, w_g, w_u, w_d, layer):
    n_tok = hn.shape[0] // ROW_CHUNKS
    assert n_tok & (n_tok - 1) == 0, "dump-row aliasing masks the token index with T - 1"
    n_blocks = (src.shape[0] - 2 * MOE_BLK) // MOE_BLK
    in_hbm = pl.BlockSpec(memory_space=pltpu.HBM)
    return pl.pallas_call(
        functools.partial(_expert_kernel, n_tok=n_tok, layer=layer),
        grid_spec=pltpu.PrefetchScalarGridSpec(
            num_scalar_prefetch=5,
            grid=(n_blocks + 1,),
            in_specs=[pl.BlockSpec(hn.shape, lambda b, *_: (0, 0), pipeline_mode=pl.Buffered(1)),
                      in_hbm, in_hbm, in_hbm],
            out_specs=pl.BlockSpec(memory_space=pltpu.HBM),
            scratch_shapes=[pltpu.VMEM((MOE_BLK * ROW_CHUNKS, LANES), jnp.uint32),
                            pltpu.VMEM((2, MOE_BLK * ROW_CHUNKS, LANES), jnp.uint32),
                            pltpu.VMEM((2, D_MODEL, D_EXPERT), F32), pltpu.VMEM((2, D_MODEL, D_EXPERT), F32),
                            pltpu.VMEM((2, D_EXPERT, D_MODEL), F32),
                            pltpu.VMEM((D_MODEL, D_EXPERT), BF16), pltpu.VMEM((D_MODEL, D_EXPERT), BF16),
                            pltpu.VMEM((D_EXPERT, D_MODEL), BF16),
                            pltpu.SemaphoreType.DMA((2,)), pltpu.SemaphoreType.DMA((2,))]),
        out_shape=jax.ShapeDtypeStruct((2 * n_tok + MOE_BLK, ROW_CHUNKS, LANES), jnp.uint32),
        compiler_params=_cparams(("arbitrary",)),
        name="moe_experts",
    )(blk_e, n_used, src, next_e, wslot, hn, w_g, w_u, w_d)


def _moe_add_kernel(h_ref, gate_ref, y0_ref, y1_ref, o_ref):
    gate = gate_ref[...]
    o_ref[...] = h_ref[...] + gate[:, 0:1] * _load_row_tiles(y0_ref) + gate[:, 1:2] * _load_row_tiles(y1_ref)


def _moe_specs(n_tok):
    slots = n_tok // TOK_TILE
    y2_rows = (TOK_TILE * ROW_CHUNKS, LANES)
    return [pl.BlockSpec((TOK_TILE, D_MODEL), lambda i: (i, 0)), pl.BlockSpec((TOK_TILE, LANES), lambda i: (i, 0)),
            pl.BlockSpec(y2_rows, lambda i: (i, 0)), pl.BlockSpec(y2_rows, lambda i: (slots + i, 0))]


def _moe_add(h, gate, y2):
    n_tok = h.shape[0]
    return pl.pallas_call(
        _moe_add_kernel,
        grid=(n_tok // TOK_TILE,),
        in_specs=_moe_specs(n_tok),
        out_specs=pl.BlockSpec((TOK_TILE, D_MODEL), lambda i: (i, 0)),
        out_shape=jax.ShapeDtypeStruct((n_tok, D_MODEL), F32),
        compiler_params=_cparams(("parallel",)),
        name="moe_add",
    )(h, gate, y2, y2)


def _moe(hn, ids, counts, w_g, w_u, w_d, layer):
    n_tok = hn.shape[0] // ROW_CHUNKS
    n_assign = 2 * n_tok
    n_blocks = n_assign // MOE_BLK + N_EXPERTS
    n_rows = n_blocks * MOE_BLK
    cnt = counts[0, :N_EXPERTS].astype(jnp.int32)
    padded = (cnt + MOE_BLK - 1) // MOE_BLK * MOE_BLK
    pends = jnp.cumsum(padded).astype(jnp.int32)
    pstarts = pends - padded
    blk_start = jnp.arange(n_blocks + 1, dtype=jnp.int32) * MOE_BLK
    blk_e = jnp.minimum(jnp.sum(pends[None, :] <= blk_start[:, None], axis=1), N_EXPERTS - 1).astype(jnp.int32)
    n_used = (pends[-1:] // MOE_BLK).astype(jnp.int32)
    used = (padded > 0)[None, :]
    e_idx = jnp.arange(N_EXPERTS, dtype=jnp.int32)[None, :]
    next_e = jnp.min(jnp.where(used & (e_idx > blk_e[:, None]), e_idx, N_EXPERTS), axis=1)
    next_e = jnp.where(next_e < N_EXPERTS, next_e, -1).astype(jnp.int32)
    ordinal = jnp.sum((used & (e_idx <= blk_e[:, None])).astype(jnp.int32), axis=1) - 1
    src = _row_map((ids[0], ids[1]), (ids[2], ids[3]), pstarts, n_rows)
    y2 = _expert_ffn(hn, src, blk_e, n_used, next_e, (ordinal % 2).astype(jnp.int32), w_g, w_u, w_d, layer)
    return y2.reshape(-1, LANES)


def kernel(x, norm_mix, norm_ffn, w_in_even, ssm_a_re, ssm_a_im, ssm_b_re, ssm_b_im, ssm_c_re, ssm_c_im, ssm_d,
           ssm_log_step, w_glu, b_glu, q_norm, k_norm, w_out_even, w_in_conv, conv_w, w_out_conv, w_router_group,
           b_router_group, w_router_expert, b_router_expert, w_expert_gate, w_expert_up, w_expert_down):
    bsz, s_len, d = x.shape
    x2 = x.reshape(bsz * s_len, d)
    route = lambda layer: _router_operands(norm_ffn[layer], w_router_group[layer], b_router_group[layer],
                                           w_router_expert[layer], b_router_expert[layer])
    experts = lambda layer: (w_expert_gate, w_expert_up, w_expert_down, layer)

    u, q, k, v = _inproj_even(x2, norm_mix[0], w_in_even[0], q_norm[0], k_norm[0])
    tables = _s5_tables(ssm_a_re[0], ssm_a_im[0], ssm_b_re[0], ssm_b_im[0], ssm_c_re[0], ssm_c_im[0], ssm_d[0],
                        ssm_log_step[0])
    y_pre = _s5_core(u, tables, bsz, s_len)
    attn = _dilated_attention(q, k, v, bsz, s_len)
    h, hn, ids, gate, counts = _outproj_even(x2, y_pre, attn, w_glu[0], b_glu[0], w_out_even[0], route(0))
    y2 = _moe(hn, ids, counts, *experts(0))

    h, hn, ids, gate, counts = _conv_layer(h, gate, y2, norm_mix[1], w_in_conv[0], conv_w[0], w_out_conv[0],
                                           route(1), s_len)
    y2 = _moe(hn, ids, counts, *experts(1))
    return _moe_add(h, gate, y2).reshape(bsz, s_len, d)
```

```python
import functools
import math

import jax
import jax.numpy as jnp
from jax import lax
from jax.experimental import pallas as pl
from jax.experimental.pallas import tpu as pltpu
from jax.experimental.pallas import tpu_sc as plsc

F32 = jnp.float32
BF16 = jnp.bfloat16

D_MODEL = 1024
SSM_GROUP = 16
SSM_GROUPS = 40
SSM_WIDTH = SSM_GROUP * SSM_GROUPS
SSM_STATE = 64
ATT_HEADS = 6
ATT_HEAD_DIM = 64
ATT_WIDTH = ATT_HEADS * ATT_HEAD_DIM
DILATIONS = (16, 4, 1)
ATT_BLK = 128
CONV_TAPS = 3
N_GROUPS = 4
EXPERTS_PER_GROUP = 8
N_EXPERTS = N_GROUPS * EXPERTS_PER_GROUP
D_EXPERT = 512
MOE_BLK = 256
RMS_EPS = 1e-6
NEG_INF = -1e30

LANES = 128
SUBLANES = 8
SC_CORES, SC_SUBCORES, SC_LANES = 2, 16, 16
VMEM_LIMIT = 56 * 1024 * 1024

TOK_TILE = 512
ROW_GROUPS = 2
SSM_CHUNK = 8
SSM_SLAB_GROUPS = LANES // SSM_GROUP
SSM_SLABS = SSM_WIDTH // LANES
S5_ROW_TILE = 256
ATT_SB = 2048
ATT_GROUP = 4
ROW_CHUNKS = D_MODEL // (2 * LANES)


def _cparams(sem):
    return pltpu.CompilerParams(dimension_semantics=sem, vmem_limit_bytes=VMEM_LIMIT)


def _rms(x, gain):
    return x * lax.rsqrt(jnp.mean(x * x, axis=-1, keepdims=True) + RMS_EPS) * gain


HIGH_HALF = 0xFFFF0000


def _load_row_tiles(ref, lead=(), first=0, n=None):
    n = ref.shape[-2] // ROW_CHUNKS if n is None else n
    bits = jnp.concatenate([ref[lead + (pl.ds(first * ROW_CHUNKS + c, n, stride=ROW_CHUNKS), slice(None))]
                            for c in range(ROW_CHUNKS)], axis=1)
    low = lax.bitcast_convert_type(bits << 16, F32)
    high = lax.bitcast_convert_type(bits & jnp.uint32(HIGH_HALF), F32)
    return jnp.concatenate([low, high], axis=1)


def _store_row_tiles(ref, value, lead=(), first=0):
    n, half = value.shape[0], value.shape[1] // 2
    bf16_bits = lambda t: lax.bitcast_convert_type(t.astype(BF16).astype(F32), jnp.uint32)
    bits = (bf16_bits(value[:, :half]) >> 16) | (bf16_bits(value[:, half:]) & jnp.uint32(HIGH_HALF))
    for c in range(ROW_CHUNKS):
        rows = pl.ds(first * ROW_CHUNKS + c, n, stride=ROW_CHUNKS)
        ref[lead + (rows, slice(None))] = bits[:, c * LANES:(c + 1) * LANES]


def _head_norm(t, gain, bd):
    tt = t * t
    hi = tt.astype(BF16)
    lo = (tt - hi.astype(F32)).astype(BF16)
    ss = jnp.dot(hi, bd, preferred_element_type=F32) + jnp.dot(lo, bd, preferred_element_type=F32)
    return t * lax.rsqrt(ss * (1.0 / ATT_HEAD_DIM) + RMS_EPS) * gain


def _inproj_even_kernel(x_ref, g_ref, w_ref, bd_ref, qn_ref, kn_ref, u_ref, q_ref, k_ref, v_ref):
    half = x_ref.shape[0] // ROW_GROUPS
    parts = [slice(g * half, (g + 1) * half) for g in range(ROW_GROUPS)]
    hn = [_rms(x_ref[rows, :], g_ref[...]).astype(BF16) for rows in parts]
    projs = [jnp.dot(t, w_ref[...], preferred_element_type=F32) for t in hn]
    bd = bd_ref[...]
    o = SSM_WIDTH
    for rows, proj in zip(parts, projs):
        for j in range(SSM_SLABS):
            u_ref[j, rows, :] = proj[:, j * LANES:(j + 1) * LANES]
        q = _head_norm(proj[:, o:o + ATT_WIDTH], qn_ref[...], bd) * (ATT_HEAD_DIM ** -0.5)
        k = _head_norm(proj[:, o + ATT_WIDTH:o + 2 * ATT_WIDTH], kn_ref[...], bd)
        v = proj[:, o + 2 * ATT_WIDTH:o + 3 * ATT_WIDTH]
        for j in range(ATT_WIDTH // LANES):
            q_ref[j, rows, :] = q[:, j * LANES:(j + 1) * LANES]
            k_ref[j, rows, :] = k[:, j * LANES:(j + 1) * LANES]
            v_ref[j, rows, :] = v[:, j * LANES:(j + 1) * LANES]


def _inproj_even(x2, gain, w_in, q_norm, k_norm):
    n_tok = x2.shape[0]
    n_slab = ATT_WIDTH // LANES
    head_of = jnp.arange(ATT_WIDTH) // ATT_HEAD_DIM
    bd = (head_of[:, None] == head_of[None, :]).astype(BF16)
    qn = jnp.tile(q_norm.astype(F32), ATT_HEADS)[None]
    kn = jnp.tile(k_norm.astype(F32), ATT_HEADS)[None]
    full = lambda shape: pl.BlockSpec(shape, lambda i: (0,) * len(shape))
    slab = pl.BlockSpec((n_slab, TOK_TILE, LANES), lambda i: (0, i, 0))
    slab_shape = jax.ShapeDtypeStruct((n_slab, n_tok, LANES), F32)
    return pl.pallas_call(
        _inproj_even_kernel,
        grid=(n_tok // TOK_TILE,),
        in_specs=[pl.BlockSpec((TOK_TILE, D_MODEL), lambda i: (i, 0)), full((1, D_MODEL)),
                  full(w_in.shape), full(bd.shape), full(qn.shape), full(kn.shape)],
        out_specs=[pl.BlockSpec((SSM_SLABS, TOK_TILE, LANES), lambda i: (0, i, 0)), slab, slab, slab],
        out_shape=[jax.ShapeDtypeStruct((SSM_SLABS, n_tok, LANES), F32), slab_shape, slab_shape, slab_shape],
        compiler_params=_cparams(("parallel",)),
        name="inproj_even",
    )(x2, gain[None].astype(F32), w_in.astype(BF16), bd, qn, kn)


def _s5_tables(a_re, a_im, b_re, b_im, c_re, c_im, d_skip, log_step):
    f = lambda t: t.astype(F32)
    a_re, a_im, b_re, b_im, c_re, c_im = map(f, (a_re, a_im, b_re, b_im, c_re, c_im))
    L = SSM_CHUNK
    step = jnp.exp(f(log_step))[:, None]
    ks = jnp.arange(L + 1, dtype=F32)[:, None, None]
    mag = jnp.exp(ks * (a_re * step)[None])
    ang = ks * (a_im * step)[None]
    pw_re, pw_im = mag * jnp.cos(ang), mag * jnp.sin(ang)
    nr, ni = pw_re[1] - 1.0, pw_im[1]
    den = a_re * a_re + a_im * a_im
    z_re, z_im = (nr * a_re + ni * a_im) / den, (ni * a_re - nr * a_im) / den
    bb_re = z_re[..., None] * b_re - z_im[..., None] * b_im
    bb_im = z_re[..., None] * b_im + z_im[..., None] * b_re
    lb_re = pw_re[..., None] * bb_re[None] - pw_im[..., None] * bb_im[None]
    lb_im = pw_re[..., None] * bb_im[None] + pw_im[..., None] * bb_re[None]
    kk = jnp.einsum('gop,kgpi->gkio', c_re, lb_re[:L]) - jnp.einsum('gop,kgpi->gkio', c_im, lb_im[:L])
    ti = jnp.arange(L)
    lag = ti[None, :] - ti[:, None]
    m = jnp.where((lag >= 0)[None, :, :, None, None], kk[:, jnp.maximum(lag, 0)], 0.0)
    ns, gs = SSM_SLABS, SSM_SLAB_GROUPS
    lw = L * LANES
    pm = m.reshape(ns, gs, L, L, SSM_GROUP, SSM_GROUP).transpose(0, 2, 1, 4, 3, 5).reshape(ns, lw, L * SSM_GROUP)
    fold_e = lambda t: t[:L][::-1].reshape(L, ns, gs, SSM_STATE, SSM_GROUP).transpose(1, 0, 2, 4, 3).reshape(ns, lw, SSM_STATE)
    pe = jnp.concatenate([fold_e(lb_re), fold_e(lb_im)], axis=-1)
    pw1_re, pw1_im = (t[1:].transpose(1, 0, 2)[:, :, None, :] for t in (pw_re, pw_im))
    cl_re = c_re[:, None] * pw1_re - c_im[:, None] * pw1_im
    cl_im = c_re[:, None] * pw1_im + c_im[:, None] * pw1_re
    fold_f = lambda t: t.reshape(ns, gs, L, SSM_GROUP, SSM_STATE).transpose(0, 4, 2, 1, 3).reshape(ns, SSM_STATE, lw)
    pf = jnp.concatenate([fold_f(cl_re), fold_f(-cl_im)], axis=1)
    per_chain = lambda t: jnp.tile(t.reshape(ns, gs * SSM_STATE // LANES, LANES), (1, 2, 1))
    d_vec = jnp.tile(f(d_skip).reshape(ns, 1, LANES), (1, 1, L))
    return pm, pe, pf, per_chain(pw_re[L]), per_chain(pw_im[L]), d_vec


def _log2(n):
    assert n & (n - 1) == 0, "index fields are split with shifts"
    return n.bit_length() - 1


def _iota2(shape):
    return lax.broadcasted_iota(jnp.int32, shape, 0), lax.broadcasted_iota(jnp.int32, shape, 1)


def _widen(compact, group_major_cols, shape, r_shift, c_shift, sel_rows):
    gmask = SSM_SLAB_GROUPS - 1
    state_bits, wide_state_bits = _log2(SSM_STATE), _log2(SSM_SLAB_GROUPS * SSM_STATE)
    chan_bits, lane_bits = _log2(SSM_GROUP), _log2(LANES)
    if sel_rows:
        r, c = _iota2((shape[0], compact.shape[0]))
        sel = ((r >> wide_state_bits) == (c >> state_bits)) & ((r & (SSM_STATE - 1)) == (c & (SSM_STATE - 1)))
        wide = jnp.dot(sel.astype(BF16), compact, preferred_element_type=F32)
    else:
        r, c = _iota2((compact.shape[1], shape[1]))
        if group_major_cols:
            sel = ((c >> wide_state_bits) == (r >> state_bits)) & ((c & (SSM_STATE - 1)) == (r & (SSM_STATE - 1)))
        else:
            sel = ((c >> lane_bits) == (r >> chan_bits)) & ((c & (SSM_GROUP - 1)) == (r & (SSM_GROUP - 1)))
        wide = jnp.dot(compact, sel.astype(BF16), preferred_element_type=F32)
    r, c = _iota2(shape)
    keep = ((r >> r_shift) & gmask) == ((c >> c_shift) & gmask)
    return jnp.where(keep, wide, 0.0).astype(BF16)


def _s5_kernel(u_ref, pm_ref, pe_ref, pf_ref, ar_ref, ai_ref, d_ref, y_ref, m_ref, e_ref, f_ref, x_ref, sr_ref, si_ref,
               *, n_chunk, pitch):
    L = SSM_CHUNK
    n_blk = SSM_SLAB_GROUPS * SSM_STATE // LANES
    n_re = n_blk * LANES
    lw = L * LANES
    chan_bits, state_bits = _log2(SSM_GROUP), _log2(SSM_STATE)
    m_ref[...] = _widen(pm_ref[...].astype(BF16), False, (lw, lw), chan_bits, chan_bits, False)
    e_ref[...] = _widen(pe_ref[...].astype(BF16), True, (lw, 2 * n_re), chan_bits, state_bits, False)
    f_ref[...] = _widen(pf_ref[...].astype(BF16), False, (2 * n_re, lw), state_bits, chan_bits, True)
    tiles = [(b, c0) for b in range(2) for c0 in range(0, n_chunk, S5_ROW_TILE)]
    for b, c0 in tiles:
        r0 = b * n_chunk + c0
        for t in range(L):
            x_ref[r0:r0 + S5_ROW_TILE, t * LANES:(t + 1) * LANES] = (
                u_ref[pl.ds(r0 * L + t, S5_ROW_TILE, stride=L), :].astype(BF16))
        sl = jnp.dot(x_ref[r0:r0 + S5_ROW_TILE, :], e_ref[...], preferred_element_type=F32)
        for j in range(n_blk):
            base = (b * n_blk + j) * pitch + c0
            sr_ref[base:base + S5_ROW_TILE, :] = sl[:, j * LANES:(j + 1) * LANES]
            si_ref[base:base + S5_ROW_TILE, :] = sl[:, n_re + j * LANES:n_re + (j + 1) * LANES]
    ar, ai = ar_ref[...], ai_ref[...]
    half = LANES

    def scan_step(c, carry):
        s_re, s_im = carry
        rows = pl.ds(c, SUBLANES, stride=pitch)
        x_re, x_im = sr_ref[rows, :], si_ref[rows, :]
        sr_ref[rows, :] = s_re
        si_ref[rows, :] = s_im
        return ar * s_re - ai * s_im + x_re, ar * s_im + ai * s_re + x_im

    zero = jnp.zeros((SUBLANES, half), F32)
    lax.fori_loop(0, n_chunk, scan_step, (zero, zero), unroll=8)

    for b, c0 in tiles:
        r0 = b * n_chunk + c0
        chain = lambda ref, j: ref[(b * n_blk + j) * pitch + c0:(b * n_blk + j) * pitch + c0 + S5_ROW_TILE, :]
        sp = jnp.concatenate([chain(sr_ref, j) for j in range(n_blk)] + [chain(si_ref, j) for j in range(n_blk)],
                             axis=1).astype(BF16)
        xt = x_ref[r0:r0 + S5_ROW_TILE, :]
        y = (jnp.dot(xt, m_ref[...], preferred_element_type=F32)
             + jnp.dot(sp, f_ref[...], preferred_element_type=F32)
             + d_ref[...] * xt.astype(F32))
        for t in range(L):
            y_ref[pl.ds(r0 * L + t, S5_ROW_TILE, stride=L), :] = y[:, t * LANES:(t + 1) * LANES]


def _s5_core(u, tables, bsz, s_len):
    assert bsz == 2, "the scan packs (batch, lane block) into the 8 sublanes of one vreg"
    pm, pe, pf, a_r, a_i, d_vec = tables
    n_tok = bsz * s_len
    n_chunk = s_len // SSM_CHUNK
    pitch = n_chunk + SUBLANES
    lw = SSM_CHUNK * LANES
    n_state = 2 * SSM_SLAB_GROUPS * SSM_STATE
    slab = lambda shape, **kw: pl.BlockSpec((None,) + shape, lambda i: (i,) + (0,) * len(shape), **kw)
    return pl.pallas_call(
        functools.partial(_s5_kernel, n_chunk=n_chunk, pitch=pitch),
        grid=(SSM_SLABS,),
        in_specs=[slab((n_tok, LANES)), slab(pm.shape[1:]), slab(pe.shape[1:]), slab(pf.shape[1:]),
                  slab((SUBLANES, LANES)), slab((SUBLANES, LANES)), slab((1, lw))],
        out_specs=slab((n_tok, LANES)),
        out_shape=jax.ShapeDtypeStruct((SSM_SLABS, n_tok, LANES), F32),
        scratch_shapes=[pltpu.VMEM((lw, lw), BF16), pltpu.VMEM((lw, n_state), BF16), pltpu.VMEM((n_state, lw), BF16),
                        pltpu.VMEM((bsz * n_chunk, lw), BF16),
                        pltpu.VMEM((SUBLANES * pitch, LANES), F32),
                        pltpu.VMEM((SUBLANES * pitch, LANES), F32)],
        compiler_params=_cparams(("parallel",)),
        name="s5_core",
    )(u, pm, pe, pf, a_r, a_i, d_vec)


def _attn_kernel(slope_ref, q_ref, kp_ref, kc_ref, vp_ref, vc_ref, o_ref,
                 q4_ref, k4_ref, v4_ref, k1_ref, v1_ref, m_ref, l_ref, acc_ref):
    slab = pl.program_id(1)
    sb = pl.program_id(2)
    fine = DILATIONS[1]
    nq, nk = ATT_SB // fine, 2 * ATT_SB // fine
    for r in range(fine):
        q4_ref[r * nq:(r + 1) * nq, :] = q_ref[pl.ds(r, nq, stride=fine), :]
        for dst, prev, cur in ((k4_ref, kp_ref, kc_ref), (v4_ref, vp_ref, vc_ref)):
            dst[r * nk:r * nk + nq, :] = prev[pl.ds(r, nq, stride=fine), :]
            dst[r * nk + nq:(r + 1) * nk, :] = cur[pl.ds(r, nq, stride=fine), :]
    for dst, prev, cur in ((k1_ref, kp_ref, kc_ref), (v1_ref, vp_ref, vc_ref)):
        dst[0:ATT_BLK, :] = prev[ATT_SB - ATT_BLK:ATT_SB, :]
        dst[ATT_BLK:ATT_BLK + ATT_SB, :] = cur[...]

    lane = lax.broadcasted_iota(jnp.int32, (ATT_BLK, LANES), 1)
    head0 = lane < ATT_HEAD_DIM
    qi = lax.broadcasted_iota(jnp.int32, (ATT_BLK, 2 * ATT_BLK), 0)
    kj = lax.broadcasted_iota(jnp.int32, (ATT_BLK, 2 * ATT_BLK), 1)
    back = qi + ATT_BLK - kj
    band = (back >= 0) & (back <= ATT_BLK)
    neg_steps = -back.astype(F32)
    slopes = (slope_ref[2 * slab], slope_ref[2 * slab + 1])

    for pat, dil in enumerate(DILATIONS):
        span = ATT_BLK * dil
        bias = [jnp.where(band, (slopes[hh] * float(dil)) * neg_steps, NEG_INF) for hh in range(2)]

        def tile(idx, carry, dil=dil, span=span, pat=pat):
            start = (idx // dil) * span + idx % dil
            seq_ok = jnp.logical_or(sb > 0, idx >= dil)
            valid = (kj >= ATT_BLK) | seq_ok
            if dil == 1:
                start = pl.multiple_of(start, ATT_BLK)
                rows = pl.ds(start, ATT_BLK)
                qt, k_src, v_src, k_rows = q_ref[rows, :], k1_ref, v1_ref, pl.ds(start, 2 * ATT_BLK)
            elif dil == fine:
                first = pl.multiple_of((idx // fine) * ATT_BLK, ATT_BLK)
                rows = pl.ds(start, ATT_BLK, stride=dil)
                qt = q4_ref[pl.ds((idx % fine) * nq + first, ATT_BLK), :]
                k_src, v_src = k4_ref, v4_ref
                k_rows = pl.ds((idx % fine) * nk + (nq - ATT_BLK) + first, 2 * ATT_BLK)
            else:
                rows = pl.ds(start, ATT_BLK, stride=dil)
                qt = q4_ref[pl.ds((idx % fine) * nq + idx // fine, ATT_BLK, stride=fine), :]
                k_src, v_src = k4_ref, v4_ref
                k_rows = pl.ds((idx % fine) * nk + idx // fine, 2 * ATT_BLK, stride=fine)
            kt = k_src[k_rows, :].astype(BF16)
            vt = v_src[k_rows, :].astype(BF16)
            q0 = jnp.where(head0, qt, 0.0)
            scores = [lax.dot_general(qh.astype(BF16), kt, (((1,), (1,)), ((), ())), preferred_element_type=F32)
                      for qh in (q0, qt - q0)]
            return rows, valid, vt, scores

        def softmax(valid, s, hh, bias=bias):
            s = jnp.where(valid, s + bias[hh], NEG_INF)
            m = jnp.max(s, axis=-1, keepdims=True)
            p = jnp.exp(s - m)
            return m, jnp.sum(p, axis=-1, keepdims=True), p.astype(BF16)

        def merge(rows, parts, pat=pat):
            (m0, l0, o0), (m1, l1, o1) = parts
            m_t = jnp.where(head0, m0, m1)
            l_t = jnp.where(head0, l0, l1)
            o_t = jnp.where(head0, o0, o1)
            if pat == 0:
                m_ref[rows, :] = m_t
                l_ref[rows, :] = l_t
                acc_ref[rows, :] = o_t
            else:
                m_old = m_ref[rows, :]
                m_new = jnp.maximum(m_old, m_t)
                a = jnp.exp(m_old - m_new)
                b = jnp.exp(m_t - m_new)
                m_ref[rows, :] = m_new
                l_ref[rows, :] = a * l_ref[rows, :] + b * l_t
                acc_ref[rows, :] = a * acc_ref[rows, :] + b * o_t

        def group(gi, carry):
            tiles = [tile(gi * ATT_GROUP + t, None) for t in range(ATT_GROUP)]
            soft = [[softmax(valid, s, hh) for hh, s in enumerate(scores)] for _, valid, _, scores in tiles]
            outs = [[(m, l, jnp.dot(p, vt, preferred_element_type=F32)) for m, l, p in per_head]
                    for (_, _, vt, _), per_head in zip(tiles, soft)]
            for (rows, _, _, _), parts in zip(tiles, outs):
                merge(rows, parts)
            return carry

        lax.fori_loop(0, ATT_SB // ATT_BLK // ATT_GROUP, group, 0)

    o_ref[...] = (acc_ref[...] / l_ref[...]).astype(o_ref.dtype)


def _dilated_attention(q, k, v, bsz, s_len):
    assert DILATIONS == (DILATIONS[1] ** 2, DILATIONS[1], 1) and ATT_SB == ATT_BLK * DILATIONS[0]
    n_slab = q.shape[0]
    shape4 = (n_slab, bsz, s_len, LANES)
    q, k, v = (t.reshape(shape4) for t in (q, k, v))
    slopes = jnp.asarray([2.0 ** (-8.0 * (h + 1) / ATT_HEADS) for h in range(ATT_HEADS)], F32)
    blk = (None, None, ATT_SB, LANES)
    cur = pl.BlockSpec(blk, lambda b, j, i, s: (j, b, i, 0))
    prev = pl.BlockSpec(blk, lambda b, j, i, s: (j, b, jnp.maximum(i - 1, 0), 0))
    out = pl.pallas_call(
        _attn_kernel,
        grid_spec=pltpu.PrefetchScalarGridSpec(
            num_scalar_prefetch=1,
            grid=(bsz, n_slab, s_len // ATT_SB),
            in_specs=[cur, prev, cur, prev, cur],
            out_specs=cur,
            scratch_shapes=[pltpu.VMEM((ATT_SB, LANES), F32),
                            pltpu.VMEM((2 * ATT_SB, LANES), F32), pltpu.VMEM((2 * ATT_SB, LANES), F32),
                            pltpu.VMEM((ATT_BLK + ATT_SB, LANES), F32), pltpu.VMEM((ATT_BLK + ATT_SB, LANES), F32),
                            pltpu.VMEM((ATT_SB, LANES), F32), pltpu.VMEM((ATT_SB, LANES), F32),
                            pltpu.VMEM((ATT_SB, LANES), F32)]),
        out_shape=jax.ShapeDtypeStruct(shape4, BF16),
        compiler_params=_cparams(("parallel", "parallel", "parallel")),
        name="dilated_attn",
    )(slopes, q, k, k, v, v)
    return out.reshape(n_slab, bsz * s_len, LANES)


def _route_epilogue(h_parts, gain_ref, wr_ref, br_ref, tri_ref, cnt_ref, h_ref, hn_ref, ids_ref, gate_ref, cnt_out_ref):
    sub = h_parts[0].shape[0]
    tm = sub * len(h_parts)
    splits = []
    for g, h in enumerate(h_parts):
        h_ref[g * sub:(g + 1) * sub, :] = h
        hn = _rms(h, gain_ref[...])
        _store_row_tiles(hn_ref, hn, first=g * sub)
        hn_hi = hn.astype(BF16)
        splits.append((hn_hi, (hn - hn_hi.astype(F32)).astype(BF16)))
    logit_parts = []
    for hn_hi, hn_lo in splits:
        hi_part = jnp.dot(hn_hi, wr_ref[...], preferred_element_type=F32)
        logit_parts.append(hi_part[:, :LANES] + (hi_part[:, LANES:]
                                                 + jnp.dot(hn_lo, wr_ref[:, :LANES], preferred_element_type=F32)))
    logits = jnp.concatenate(logit_parts, axis=0) + br_ref[...]
    lane = lax.broadcasted_iota(jnp.int32, (tm, LANES), 1)
    big = jnp.int32(LANES)
    rmax = lambda t: jnp.max(t, axis=-1, keepdims=True)
    rmin = lambda t: jnp.min(t, axis=-1, keepdims=True)
    rsum = lambda t: jnp.sum(t, axis=-1, keepdims=True)
    gmask = lane < N_GROUPS
    gl = jnp.where(gmask, logits, -jnp.inf)
    gmax = rmax(gl)
    ge = jnp.where(gmask, jnp.exp(gl - gmax), 0.0)
    gprob = ge / rsum(ge)
    g_w = rmax(gprob)
    grp = rmin(jnp.where(gmask & (gprob == g_w), lane, big))
    group_of_lane = (lane - N_GROUPS) >> int(math.log2(EXPERTS_PER_GROUP))
    emask = (lane >= N_GROUPS) & (lane < N_GROUPS + N_EXPERTS) & (group_of_lane == grp)
    el = jnp.where(emask, logits, -jnp.inf)
    ee = jnp.where(emask, jnp.exp(el - rmax(el)), 0.0)
    ep = jnp.where(emask, ee / rsum(ee), -1.0)
    p1 = rmax(ep)
    i1 = rmin(jnp.where(ep == p1, lane, big))
    ep2 = jnp.where(lane == i1, -1.0, ep)
    p2 = rmax(ep2)
    i2 = rmin(jnp.where(ep2 == p2, lane, big))
    e1, e2 = i1 - N_GROUPS, i2 - N_GROUPS
    psum = p1 + p2
    gate1, gate2 = g_w * p1 / psum, g_w * p2 / psum
    oh1, oh2 = lane == e1, lane == e2
    member = (oh1 | oh2).astype(BF16)
    before = jnp.dot(tri_ref[...], member, preferred_element_type=F32) + cnt_ref[...]
    r1 = rsum(jnp.where(oh1, before, 0.0)).astype(jnp.int32)
    r2 = rsum(jnp.where(oh2, before, 0.0)).astype(jnp.int32)
    cnt_ref[...] = cnt_ref[...] + jnp.sum(member.astype(F32), axis=0, keepdims=True)
    ids = jnp.where(lane == 0, e1, jnp.where(lane == 1, e2, jnp.where(lane == 2, r1, jnp.where(lane == 3, r2, 0))))
    ids_ref[...] = jnp.transpose(ids)[0:SUBLANES, :]
    gate_ref[...] = jnp.where(lane == 0, gate1, jnp.where(lane == 1, gate2, 0.0))
    cnt_out_ref[...] = jnp.broadcast_to(cnt_ref[...], cnt_out_ref.shape)


def _router_operands(norm_gain, w_rg, b_rg, w_re, b_re):
    pad = LANES - N_GROUPS - N_EXPERTS
    wr = jnp.pad(jnp.concatenate([w_rg, w_re], axis=1).astype(F32), ((0, 0), (0, pad)))
    br = jnp.pad(jnp.concatenate([b_rg, b_re]).astype(F32), (0, pad))[None]
    r = jnp.arange(TOK_TILE)
    tri = (r[None, :] < r[:, None]).astype(BF16)
    wr_hi = wr.astype(BF16)
    wr_lo = (wr - wr_hi.astype(F32)).astype(BF16)
    return norm_gain[None].astype(F32), jnp.concatenate([wr_hi, wr_lo], axis=1), br, tri


def _route_specs(n_tok):
    full = lambda shape: pl.BlockSpec(shape, lambda i: (0,) * len(shape))
    in_specs = [full((1, D_MODEL)), full((D_MODEL, 2 * LANES)), full((1, LANES)), full((TOK_TILE, TOK_TILE))]
    tok = lambda w: pl.BlockSpec((TOK_TILE, w), lambda i: (i, 0))
    out_specs = [tok(D_MODEL), pl.BlockSpec((TOK_TILE * ROW_CHUNKS, LANES), lambda i: (i, 0)),
                 pl.BlockSpec((SUBLANES, TOK_TILE), lambda i: (0, i)), tok(LANES), full((SUBLANES, LANES))]
    out_shape = [jax.ShapeDtypeStruct((n_tok, D_MODEL), F32), jax.ShapeDtypeStruct((n_tok * ROW_CHUNKS, LANES), jnp.uint32),
                 jax.ShapeDtypeStruct((SUBLANES, n_tok), jnp.int32), jax.ShapeDtypeStruct((n_tok, LANES), F32),
                 jax.ShapeDtypeStruct((SUBLANES, LANES), F32)]
    return in_specs, out_specs, out_shape


def _gelu_tanh(x):
    return 0.5 * x * (1.0 + jnp.tanh(math.sqrt(2.0 / math.pi) * (x + 0.044715 * (x * x * x))))


def _outproj_even_kernel(x_ref, y_ref, a_ref, wglu_ref, bglu_ref, wout_ref, gain_ref, wr_ref, br_ref, tri_ref,
                         h_ref, hn_ref, ids_ref, gate_ref, cnt_out_ref, cnt_ref):
    @pl.when(pl.program_id(0) == 0)
    def _():
        cnt_ref[...] = jnp.zeros_like(cnt_ref)

    half = x_ref.shape[0] // ROW_GROUPS
    parts = [slice(g * half, (g + 1) * half) for g in range(ROW_GROUPS)]
    y = [_gelu_tanh(jnp.concatenate([y_ref[j, rows, :] for j in range(SSM_SLABS)], axis=1)) for rows in parts]
    glu = [jnp.dot(t.astype(BF16), wglu_ref[...], preferred_element_type=F32) for t in y]
    h_out = []
    for rows, t, g in zip(parts, y, glu):
        t = t * jax.nn.sigmoid(g + bglu_ref[...])
        mix = jnp.dot(t.astype(BF16), wout_ref[0:SSM_WIDTH, :], preferred_element_type=F32)
        for j in range(ATT_WIDTH // LANES):
            w_rows = slice(SSM_WIDTH + j * LANES, SSM_WIDTH + (j + 1) * LANES)
            mix = mix + jnp.dot(a_ref[j, rows, :], wout_ref[w_rows, :], preferred_element_type=F32)
        h_out.append(x_ref[rows, :] + mix)
    _route_epilogue(h_out, gain_ref, wr_ref, br_ref, tri_ref, cnt_ref,
                    h_ref, hn_ref, ids_ref, gate_ref, cnt_out_ref)


def _outproj_even(x2, y_pre, attn, w_glu, b_glu, w_out, route_ops):
    n_tok = x2.shape[0]
    n_slab = attn.shape[0]
    r_in, r_out, r_shape = _route_specs(n_tok)
    full = lambda shape: pl.BlockSpec(shape, lambda i: (0,) * len(shape))
    return pl.pallas_call(
        _outproj_even_kernel,
        grid=(n_tok // TOK_TILE,),
        in_specs=[pl.BlockSpec((TOK_TILE, D_MODEL), lambda i: (i, 0)),
                  pl.BlockSpec((SSM_SLABS, TOK_TILE, LANES), lambda i: (0, i, 0)),
                  pl.BlockSpec((n_slab, TOK_TILE, LANES), lambda i: (0, i, 0)),
                  full(w_glu.shape), full((1, SSM_WIDTH)), full(w_out.shape)] + r_in,
        out_specs=r_out, out_shape=r_shape,
        scratch_shapes=[pltpu.VMEM((1, LANES), F32)],
        compiler_params=_cparams(("arbitrary",)),
        name="outproj_even",
    )(x2, y_pre, attn, w_glu.astype(BF16), b_glu[None].astype(F32), w_out.astype(BF16), *route_ops)


def _conv_layer_kernel(h_ref, pgate_ref, y0_ref, y1_ref, gmix_ref, win_ref, cw_ref, wout_ref, gain_ref, wr_ref, br_ref,
                       tri_ref, ho_ref, hn_ref, ids_ref, gate_ref, cnt_out_ref, cnt_ref, zc_ref, *, tiles_per_seq):
    i = pl.program_id(0)

    @pl.when(i == 0)
    def _():
        cnt_ref[...] = jnp.zeros_like(cnt_ref)

    @pl.when(i % tiles_per_seq == 0)
    def _():
        zc_ref[0:SUBLANES, :] = jnp.zeros((SUBLANES, D_MODEL), F32)

    tm = h_ref.shape[0]
    c = D_MODEL
    half = tm // ROW_GROUPS
    parts = tuple(g * half for g in range(ROW_GROUPS))
    h_in, hn = {}, {}
    for r0 in parts:
        rows = slice(r0, r0 + half)
        pgate = pgate_ref[rows, :]
        h_in[r0] = (h_ref[rows, :] + pgate[:, 0:1] * _load_row_tiles(y0_ref, first=r0, n=half)
                    + pgate[:, 1:2] * _load_row_tiles(y1_ref, first=r0, n=half))
        hn[r0] = _rms(h_in[r0], gmix_ref[...]).astype(BF16)
    b_gate, zc = {}, {}
    for r0 in parts:
        b_gate[r0] = jnp.dot(hn[r0], win_ref[:, 0:c], preferred_element_type=F32)
        zc[r0] = (jnp.dot(hn[r0], win_ref[:, c:2 * c], preferred_element_type=F32)
                  * jnp.dot(hn[r0], win_ref[:, 2 * c:3 * c], preferred_element_type=F32))
    h_out = []
    for r0 in parts:
        z0 = SUBLANES + r0
        zc_ref[z0:z0 + half, :] = zc[r0]
        conv = cw_ref[CONV_TAPS - 1:CONV_TAPS, :] * zc[r0]
        for back in range(1, CONV_TAPS):
            tap = CONV_TAPS - 1 - back
            conv = conv + cw_ref[tap:tap + 1, :] * zc_ref[z0 - back:z0 - back + half, :]
        mix = jnp.dot((b_gate[r0] * conv).astype(BF16), wout_ref[...], preferred_element_type=F32)
        h_out.append(h_in[r0] + mix)
    zc_ref[0:SUBLANES, :] = zc_ref[tm:tm + SUBLANES, :]
    _route_epilogue(h_out, gain_ref, wr_ref, br_ref, tri_ref, cnt_ref,
                    ho_ref, hn_ref, ids_ref, gate_ref, cnt_out_ref)


def _conv_layer(h1, pgate, y2, gain_mix, w_in, conv_w, w_out, route_ops, s_len):
    n_tok = h1.shape[0]
    r_in, r_out, r_shape = _route_specs(n_tok)
    full = lambda shape: pl.BlockSpec(shape, lambda i: (0,) * len(shape))
    return pl.pallas_call(
        functools.partial(_conv_layer_kernel, tiles_per_seq=s_len // TOK_TILE),
        grid=(n_tok // TOK_TILE,),
        in_specs=_moe_specs(n_tok) + [full((1, D_MODEL)), full(w_in.shape), full(conv_w.shape), full(w_out.shape)] + r_in,
        out_specs=r_out, out_shape=r_shape,
        scratch_shapes=[pltpu.VMEM((1, LANES), F32), pltpu.VMEM((TOK_TILE + SUBLANES, D_MODEL), F32)],
        compiler_params=_cparams(("arbitrary",)),
        name="conv_layer",
    )(h1, pgate, y2, y2, gain_mix[None].astype(F32), w_in.astype(BF16), conv_w.astype(F32), w_out.astype(BF16),
      *route_ops)


def _row_map(experts, ranks, first_row, n_rows):
    n_slot, n_tok = len(experts), experts[0].shape[0]
    n_assign = n_slot * n_tok
    n_src = n_rows + 2 * MOE_BLK
    mesh = plsc.VectorSubcoreMesh(core_axis_name="core", subcore_axis_name="subcore",
                                  num_cores=SC_CORES, num_subcores=SC_SUBCORES)
    tok_vec = pltpu.VMEM((n_tok,), jnp.int32)

    @functools.partial(
        pl.kernel, mesh=mesh, out_type=jax.ShapeDtypeStruct((n_src,), jnp.int32),
        scratch_types=[tok_vec] * (2 * n_slot) + [pltpu.VMEM((N_EXPERTS,), jnp.int32), pltpu.VMEM((n_src,), jnp.int32)],
        compiler_params=pltpu.CompilerParams(needs_layout_passes=False), name="moe_row_map")
    def row_map(*refs):
        ins, src_hbm, scratch = refs[:2 * n_slot + 2], refs[2 * n_slot + 2], refs[2 * n_slot + 3:]
        first_v, src_v = scratch[2 * n_slot], scratch[2 * n_slot + 1]

        @pl.when(jnp.logical_and(lax.axis_index("core") == 0, lax.axis_index("subcore") == 0))
        def _():
            for hbm, vmem in zip(ins, scratch):
                pltpu.sync_copy(hbm, vmem)
            lanes = lax.iota(jnp.int32, SC_LANES)

            for slot in range(n_slot):
                e_v, r_v = scratch[slot], scratch[n_slot + slot]

                @pl.loop(0, n_tok, step=SC_LANES)
                def _(t):
                    row = plsc.load_gather(first_v, [e_v[pl.ds(t, SC_LANES)]]) + r_v[pl.ds(t, SC_LANES)]
                    plsc.store_scatter(src_v, [row + MOE_BLK], slot * n_tok + t + lanes)

            pltpu.sync_copy(src_v, src_hbm)

    dump_rows = n_assign + (jnp.arange(n_src, dtype=jnp.int32) & (MOE_BLK - 1))
    return row_map(*experts, *ranks, first_row, dump_rows)


def _expert_kernel(blk_e_ref, n_used_ref, src_ref, next_e_ref, wslot_ref, hn_ref, wg_ref, wu_ref, wd_ref, y2_ref,
                   xbuf, ybuf, wg_f, wu_f, wd_f, wg_s, wu_s, wd_s, ssem, wsem, *, n_tok, layer):
    b = pl.program_id(0)
    n_used = n_used_ref[0]
    cur = b % 2
    nxt = 1 - cur

    tile = lambda i: pl.ds(i * ROW_CHUNKS, ROW_CHUNKS)

    def scatter(blk, slot, i):
        row = src_ref[(blk + 1) * MOE_BLK + i]
        return pltpu.make_async_copy(ybuf.at[slot, tile(i)], y2_ref.at[row], ssem.at[slot])

    wait_block = lambda slot: pltpu.make_async_copy(ybuf.at[slot], ybuf.at[slot], ssem.at[slot]).wait()

    def weights(expert, slot, act):
        for hbm, buf in ((wg_ref, wg_f), (wu_ref, wu_f), (wd_ref, wd_f)):
            act(pltpu.make_async_copy(hbm.at[layer, expert], buf.at[slot], wsem.at[slot]))

    @pl.when(b == 0)
    def _():
        ybuf[1] = jnp.zeros(ybuf.shape[1:], ybuf.dtype)
        weights(blk_e_ref[0], wslot_ref[0], lambda c: c.start())

    @pl.when(b < n_used)
    def _():
        new_expert = jnp.logical_or(b == 0, blk_e_ref[b] != blk_e_ref[jnp.maximum(b - 1, 0)])

        @pl.when(new_expert)
        def _():
            slot = wslot_ref[b]
            weights(blk_e_ref[b], slot, lambda c: c.wait())
            wg_s[...] = wg_f[slot].astype(BF16)
            wu_s[...] = wu_f[slot].astype(BF16)
            wd_s[...] = wd_f[slot].astype(BF16)

            @pl.when(next_e_ref[b] >= 0)
            def _():
                weights(next_e_ref[b], 1 - slot, lambda c: c.start())

        @pl.when(b >= 1)
        def _():
            wait_block(cur)

        for i in range(MOE_BLK):
            tok = src_ref[(b + 1) * MOE_BLK + i] & (n_tok - 1)
            xbuf[tile(i), :] = hn_ref[pl.ds(pl.multiple_of(tok * ROW_CHUNKS, ROW_CHUNKS), ROW_CHUNKS), :]
            scatter(b - 1, nxt, i).start(priority=i % 2)
        x = _load_row_tiles(xbuf).astype(BF16)
        g = jnp.dot(x, wg_s[...], preferred_element_type=F32)
        u = jnp.dot(x, wu_s[...], preferred_element_type=F32)
        hb = (g * jax.nn.sigmoid(g) * u).astype(BF16)
        _store_row_tiles(ybuf, jnp.dot(hb, wd_s[...], preferred_element_type=F32), (cur,))

    @pl.when(b == n_used)
    def _():
        wait_block(cur)
        for i in range(MOE_BLK):
            scatter(b - 1, nxt, i).start(priority=i % 2)
        wait_block(nxt)


def _expert_ffn(hn, src, blk_e, n_used, next_e, wslot, w_g, w_u, w_d, layer):
    n_tok = hn.shape[0] // ROW_CHUNKS
    assert n_tok & (n_tok - 1) == 0, "dump-row aliasing masks the token index with T - 1"
    n_blocks = (src.shape[0] - 2 * MOE_BLK) // MOE_BLK
    in_hbm = pl.BlockSpec(memory_space=pltpu.HBM)
    return pl.pallas_call(
        functools.partial(_expert_kernel, n_tok=n_tok, layer=layer),
        grid_spec=pltpu.PrefetchScalarGridSpec(
            num_scalar_prefetch=5,
            grid=(n_blocks + 1,),
            in_specs=[pl.BlockSpec(hn.shape, lambda b, *_: (0, 0), pipeline_mode=pl.Buffered(1)),
                      in_hbm, in_hbm, in_hbm],
            out_specs=pl.BlockSpec(memory_space=pltpu.HBM),
            scratch_shapes=[pltpu.VMEM((MOE_BLK * ROW_CHUNKS, LANES), jnp.uint32),
                            pltpu.VMEM((2, MOE_BLK * ROW_CHUNKS, LANES), jnp.uint32),
                            pltpu.VMEM((2, D_MODEL, D_EXPERT), F32), pltpu.VMEM((2, D_MODEL, D_EXPERT), F32),
                            pltpu.VMEM((2, D_EXPERT, D_MODEL), F32),
                            pltpu.VMEM((D_MODEL, D_EXPERT), BF16), pltpu.VMEM((D_MODEL, D_EXPERT), BF16),
                            pltpu.VMEM((D_EXPERT, D_MODEL), BF16),
                            pltpu.SemaphoreType.DMA((2,)), pltpu.SemaphoreType.DMA((2,))]),
        out_shape=jax.ShapeDtypeStruct((2 * n_tok + MOE_BLK, ROW_CHUNKS, LANES), jnp.uint32),
        compiler_params=_cparams(("arbitrary",)),
        name="moe_experts",
    )(blk_e, n_used, src, next_e, wslot, hn, w_g, w_u, w_d)


def _moe_add_kernel(h_ref, gate_ref, y0_ref, y1_ref, o_ref):
    gate = gate_ref[...]
    o_ref[...] = h_ref[...] + gate[:, 0:1] * _load_row_tiles(y0_ref) + gate[:, 1:2] * _load_row_tiles(y1_ref)


def _moe_specs(n_tok):
    slots = n_tok // TOK_TILE
    y2_rows = (TOK_TILE * ROW_CHUNKS, LANES)
    return [pl.BlockSpec((TOK_TILE, D_MODEL), lambda i: (i, 0)), pl.BlockSpec((TOK_TILE, LANES), lambda i: (i, 0)),
            pl.BlockSpec(y2_rows, lambda i: (i, 0)), pl.BlockSpec(y2_rows, lambda i: (slots + i, 0))]


def _moe_add(h, gate, y2):
    n_tok = h.shape[0]
    return pl.pallas_call(
        _moe_add_kernel,
        grid=(n_tok // TOK_TILE,),
        in_specs=_moe_specs(n_tok),
        out_specs=pl.BlockSpec((TOK_TILE, D_MODEL), lambda i: (i, 0)),
        out_shape=jax.ShapeDtypeStruct((n_tok, D_MODEL), F32),
        compiler_params=_cparams(("parallel",)),
        name="moe_add",
    )(h, gate, y2, y2)


def _moe(hn, ids, counts, w_g, w_u, w_d, layer):
    n_tok = hn.shape[0] // ROW_CHUNKS
    n_assign = 2 * n_tok
    n_blocks = n_assign // MOE_BLK + N_EXPERTS
    n_rows = n_blocks * MOE_BLK
    cnt = counts[0, :N_EXPERTS].astype(jnp.int32)
    padded = (cnt + MOE_BLK - 1) // MOE_BLK * MOE_BLK
    pends = jnp.cumsum(padded).astype(jnp.int32)
    pstarts = pends - padded
    blk_start = jnp.arange(n_blocks + 1, dtype=jnp.int32) * MOE_BLK
    blk_e = jnp.minimum(jnp.sum(pends[None, :] <= blk_start[:, None], axis=1), N_EXPERTS - 1).astype(jnp.int32)
    n_used = (pends[-1:] // MOE_BLK).astype(jnp.int32)
    used = (padded > 0)[None, :]
    e_idx = jnp.arange(N_EXPERTS, dtype=jnp.int32)[None, :]
    next_e = jnp.min(jnp.where(used & (e_idx > blk_e[:, None]), e_idx, N_EXPERTS), axis=1)
    next_e = jnp.where(next_e < N_EXPERTS, next_e, -1).astype(jnp.int32)
    ordinal = jnp.sum((used & (e_idx <= blk_e[:, None])).astype(jnp.int32), axis=1) - 1
    src = _row_map((ids[0], ids[1]), (ids[2], ids[3]), pstarts, n_rows)
    y2 = _expert_ffn(hn, src, blk_e, n_used, next_e, (ordinal % 2).astype(jnp.int32), w_g, w_u, w_d, layer)
    return y2.reshape(-1, LANES)


def kernel(x, norm_mix, norm_ffn, w_in_even, ssm_a_re, ssm_a_im, ssm_b_re, ssm_b_im, ssm_c_re, ssm_c_im, ssm_d,
           ssm_log_step, w_glu, b_glu, q_norm, k_norm, w_out_even, w_in_conv, conv_w, w_out_conv, w_router_group,
           b_router_group, w_router_expert, b_router_expert, w_expert_gate, w_expert_up, w_expert_down):
    bsz, s_len, d = x.shape
    x2 = x.reshape(bsz * s_len, d)
    route = lambda layer: _router_operands(norm_ffn[layer], w_router_group[layer], b_router_group[layer],
                                           w_router_expert[layer], b_router_expert[layer])
    experts = lambda layer: (w_expert_gate, w_expert_up, w_expert_down, layer)

    u, q, k, v = _inproj_even(x2, norm_mix[0], w_in_even[0], q_norm[0], k_norm[0])
    tables = _s5_tables(ssm_a_re[0], ssm_a_im[0], ssm_b_re[0], ssm_b_im[0], ssm_c_re[0], ssm_c_im[0], ssm_d[0],
                        ssm_log_step[0])
    y_pre = _s5_core(u, tables, bsz, s_len)
    attn = _dilated_attention(q, k, v, bsz, s_len)
    h, hn, ids, gate, counts = _outproj_even(x2, y_pre, attn, w_glu[0], b_glu[0], w_out_even[0], route(0))
    y2 = _moe(hn, ids, counts, *experts(0))

    h, hn, ids, gate, counts = _conv_layer(h, gate, y2, norm_mix[1], w_in_conv[0], conv_w[0], w_out_conv[0],
                                           route(1), s_len)
    y2 = _moe(hn, ids, counts, *experts(1))
    return _moe_add(h, gate, y2).reshape(bsz, s_len, d)
```

```python
import functools
import math

import jax
import jax.numpy as jnp
from jax import lax
from jax.experimental import pallas as pl
from jax.experimental.pallas import tpu as pltpu
from jax.experimental.pallas import tpu_sc as plsc

F32 = jnp.float32
BF16 = jnp.bfloat16

D_MODEL = 1024
SSM_GROUP = 16
SSM_GROUPS = 40
SSM_WIDTH = SSM_GROUP * SSM_GROUPS
SSM_STATE = 64
ATT_HEADS = 6
ATT_HEAD_DIM = 64
ATT_WIDTH = ATT_HEADS * ATT_HEAD_DIM
DILATIONS = (16, 4, 1)
ATT_BLK = 128
CONV_TAPS = 3
N_GROUPS = 4
EXPERTS_PER_GROUP = 8
N_EXPERTS = N_GROUPS * EXPERTS_PER_GROUP
D_EXPERT = 512
MOE_BLK = 256
RMS_EPS = 1e-6
NEG_INF = -1e30

LANES = 128
SUBLANES = 8
SC_CORES, SC_SUBCORES, SC_LANES = 2, 16, 16
VMEM_LIMIT = 56 * 1024 * 1024

TOK_TILE = 512
ROW_GROUPS = 2
SSM_CHUNK = 8
SSM_SLAB_GROUPS = LANES // SSM_GROUP
SSM_SLABS = SSM_WIDTH // LANES
S5_ROW_TILE = 256
ATT_SB = 2048
ATT_GROUP = 4
ROW_CHUNKS = D_MODEL // (2 * LANES)


def _cparams(sem):
    return pltpu.CompilerParams(dimension_semantics=sem, vmem_limit_bytes=VMEM_LIMIT)


def _rms(x, gain):
    return x * lax.rsqrt(jnp.mean(x * x, axis=-1, keepdims=True) + RMS_EPS) * gain


HIGH_HALF = 0xFFFF0000


def _load_row_tiles(ref, lead=(), first=0, n=None):
    n = ref.shape[-2] // ROW_CHUNKS if n is None else n
    bits = jnp.concatenate([ref[lead + (pl.ds(first * ROW_CHUNKS + c, n, stride=ROW_CHUNKS), slice(None))]
                            for c in range(ROW_CHUNKS)], axis=1)
    low = lax.bitcast_convert_type(bits << 16, F32)
    high = lax.bitcast_convert_type(bits & jnp.uint32(HIGH_HALF), F32)
    return jnp.concatenate([low, high], axis=1)


def _store_row_tiles(ref, value, lead=(), first=0):
    n, half = value.shape[0], value.shape[1] // 2
    bf16_bits = lambda t: lax.bitcast_convert_type(t.astype(BF16).astype(F32), jnp.uint32)
    bits = (bf16_bits(value[:, :half]) >> 16) | (bf16_bits(value[:, half:]) & jnp.uint32(HIGH_HALF))
    for c in range(ROW_CHUNKS):
        rows = pl.ds(first * ROW_CHUNKS + c, n, stride=ROW_CHUNKS)
        ref[lead + (rows, slice(None))] = bits[:, c * LANES:(c + 1) * LANES]


def _head_norm(t, gain, bd):
    tt = t * t
    hi = tt.astype(BF16)
    lo = (tt - hi.astype(F32)).astype(BF16)
    ss = jnp.dot(hi, bd, preferred_element_type=F32) + jnp.dot(lo, bd, preferred_element_type=F32)
    return t * lax.rsqrt(ss * (1.0 / ATT_HEAD_DIM) + RMS_EPS) * gain


def _inproj_even_kernel(x_ref, g_ref, w_ref, bd_ref, qn_ref, kn_ref, u_ref, q_ref, k_ref, v_ref):
    half = x_ref.shape[0] // ROW_GROUPS
    parts = [slice(g * half, (g + 1) * half) for g in range(ROW_GROUPS)]
    hn = [_rms(x_ref[rows, :], g_ref[...]).astype(BF16) for rows in parts]
    projs = [jnp.dot(t, w_ref[...], preferred_element_type=F32) for t in hn]
    bd = bd_ref[...]
    o = SSM_WIDTH
    for rows, proj in zip(parts, projs):
        for j in range(SSM_SLABS):
            u_ref[j, rows, :] = proj[:, j * LANES:(j + 1) * LANES]
        q = _head_norm(proj[:, o:o + ATT_WIDTH], qn_ref[...], bd) * (ATT_HEAD_DIM ** -0.5)
        k = _head_norm(proj[:, o + ATT_WIDTH:o + 2 * ATT_WIDTH], kn_ref[...], bd)
        v = proj[:, o + 2 * ATT_WIDTH:o + 3 * ATT_WIDTH]
        for j in range(ATT_WIDTH // LANES):
            q_ref[j, rows, :] = q[:, j * LANES:(j + 1) * LANES]
            k_ref[j, rows, :] = k[:, j * LANES:(j + 1) * LANES]
            v_ref[j, rows, :] = v[:, j * LANES:(j + 1) * LANES]


def _inproj_even(x2, gain, w_in, q_norm, k_norm):
    n_tok = x2.shape[0]
    n_slab = ATT_WIDTH // LANES
    head_of = jnp.arange(ATT_WIDTH) // ATT_HEAD_DIM
    bd = (head_of[:, None] == head_of[None, :]).astype(BF16)
    qn = jnp.tile(q_norm.astype(F32), ATT_HEADS)[None]
    kn = jnp.tile(k_norm.astype(F32), ATT_HEADS)[None]
    full = lambda shape: pl.BlockSpec(shape, lambda i: (0,) * len(shape))
    slab = pl.BlockSpec((n_slab, TOK_TILE, LANES), lambda i: (0, i, 0))
    slab_shape = jax.ShapeDtypeStruct((n_slab, n_tok, LANES), F32)
    return pl.pallas_call(
        _inproj_even_kernel,
        grid=(n_tok // TOK_TILE,),
        in_specs=[pl.BlockSpec((TOK_TILE, D_MODEL), lambda i: (i, 0)), full((1, D_MODEL)),
                  full(w_in.shape), full(bd.shape), full(qn.shape), full(kn.shape)],
        out_specs=[pl.BlockSpec((SSM_SLABS, TOK_TILE, LANES), lambda i: (0, i, 0)), slab, slab, slab],
        out_shape=[jax.ShapeDtypeStruct((SSM_SLABS, n_tok, LANES), F32), slab_shape, slab_shape, slab_shape],
        compiler_params=_cparams(("parallel",)),
        name="inproj_even",
    )(x2, gain[None].astype(F32), w_in.astype(BF16), bd, qn, kn)


def _s5_tables(a_re, a_im, b_re, b_im, c_re, c_im, d_skip, log_step):
    f = lambda t: t.astype(F32)
    a_re, a_im, b_re, b_im, c_re, c_im = map(f, (a_re, a_im, b_re, b_im, c_re, c_im))
    L = SSM_CHUNK
    step = jnp.exp(f(log_step))[:, None]
    ks = jnp.arange(L + 1, dtype=F32)[:, None, None]
    mag = jnp.exp(ks * (a_re * step)[None])
    ang = ks * (a_im * step)[None]
    pw_re, pw_im = mag * jnp.cos(ang), mag * jnp.sin(ang)
    nr, ni = pw_re[1] - 1.0, pw_im[1]
    den = a_re * a_re + a_im * a_im
    z_re, z_im = (nr * a_re + ni * a_im) / den, (ni * a_re - nr * a_im) / den
    bb_re = z_re[..., None] * b_re - z_im[..., None] * b_im
    bb_im = z_re[..., None] * b_im + z_im[..., None] * b_re
    lb_re = pw_re[..., None] * bb_re[None] - pw_im[..., None] * bb_im[None]
    lb_im = pw_re[..., None] * bb_im[None] + pw_im[..., None] * bb_re[None]
    kk = jnp.einsum('gop,kgpi->gkio', c_re, lb_re[:L]) - jnp.einsum('gop,kgpi->gkio', c_im, lb_im[:L])
    ti = jnp.arange(L)
    lag = ti[None, :] - ti[:, None]
    m = jnp.where((lag >= 0)[None, :, :, None, None], kk[:, jnp.maximum(lag, 0)], 0.0)
    ns, gs = SSM_SLABS, SSM_SLAB_GROUPS
    lw = L * LANES
    pm = m.reshape(ns, gs, L, L, SSM_GROUP, SSM_GROUP).transpose(0, 2, 1, 4, 3, 5).reshape(ns, lw, L * SSM_GROUP)
    fold_e = lambda t: t[:L][::-1].reshape(L, ns, gs, SSM_STATE, SSM_GROUP).transpose(1, 0, 2, 4, 3).reshape(ns, lw, SSM_STATE)
    pe = jnp.concatenate([fold_e(lb_re), fold_e(lb_im)], axis=-1)
    pw1_re, pw1_im = (t[1:].transpose(1, 0, 2)[:, :, None, :] for t in (pw_re, pw_im))
    cl_re = c_re[:, None] * pw1_re - c_im[:, None] * pw1_im
    cl_im = c_re[:, None] * pw1_im + c_im[:, None] * pw1_re
    fold_f = lambda t: t.reshape(ns, gs, L, SSM_GROUP, SSM_STATE).transpose(0, 4, 2, 1, 3).reshape(ns, SSM_STATE, lw)
    pf = jnp.concatenate([fold_f(cl_re), fold_f(-cl_im)], axis=1)
    per_chain = lambda t: jnp.tile(t.reshape(ns, gs * SSM_STATE // LANES, LANES), (1, 2, 1))
    d_vec = jnp.tile(f(d_skip).reshape(ns, 1, LANES), (1, 1, L))
    return pm.astype(BF16), pe.astype(BF16), pf.astype(BF16), per_chain(pw_re[L]), per_chain(pw_im[L]), d_vec


def _log2(n):
    assert n & (n - 1) == 0, "index fields are split with shifts"
    return n.bit_length() - 1


def _iota2(shape):
    return lax.broadcasted_iota(jnp.int32, shape, 0), lax.broadcasted_iota(jnp.int32, shape, 1)


def _widen(compact, group_major_cols, shape, r_shift, c_shift, sel_rows):
    gmask = SSM_SLAB_GROUPS - 1
    state_bits, wide_state_bits = _log2(SSM_STATE), _log2(SSM_SLAB_GROUPS * SSM_STATE)
    chan_bits, lane_bits = _log2(SSM_GROUP), _log2(LANES)
    if sel_rows:
        r, c = _iota2((shape[0], compact.shape[0]))
        sel = ((r >> wide_state_bits) == (c >> state_bits)) & ((r & (SSM_STATE - 1)) == (c & (SSM_STATE - 1)))
        wide = jnp.dot(sel.astype(BF16), compact, preferred_element_type=F32)
    else:
        r, c = _iota2((compact.shape[1], shape[1]))
        if group_major_cols:
            sel = ((c >> wide_state_bits) == (r >> state_bits)) & ((c & (SSM_STATE - 1)) == (r & (SSM_STATE - 1)))
        else:
            sel = ((c >> lane_bits) == (r >> chan_bits)) & ((c & (SSM_GROUP - 1)) == (r & (SSM_GROUP - 1)))
        wide = jnp.dot(compact, sel.astype(BF16), preferred_element_type=F32)
    r, c = _iota2(shape)
    keep = ((r >> r_shift) & gmask) == ((c >> c_shift) & gmask)
    return jnp.where(keep, wide, 0.0).astype(BF16)


def _s5_kernel(u_ref, pm_ref, pe_ref, pf_ref, ar_ref, ai_ref, d_ref, y_ref, m_ref, e_ref, f_ref, x_ref, sr_ref, si_ref,
               *, n_chunk, pitch):
    L = SSM_CHUNK
    n_blk = SSM_SLAB_GROUPS * SSM_STATE // LANES
    n_re = n_blk * LANES
    lw = L * LANES
    chan_bits, state_bits = _log2(SSM_GROUP), _log2(SSM_STATE)
    m_ref[...] = _widen(pm_ref[...], False, (lw, lw), chan_bits, chan_bits, False)
    e_ref[...] = _widen(pe_ref[...], True, (lw, 2 * n_re), chan_bits, state_bits, False)
    f_ref[...] = _widen(pf_ref[...], False, (2 * n_re, lw), state_bits, chan_bits, True)
    tiles = [(b, c0) for b in range(2) for c0 in range(0, n_chunk, S5_ROW_TILE)]
    for b, c0 in tiles:
        r0 = b * n_chunk + c0
        for t in range(L):
            x_ref[r0:r0 + S5_ROW_TILE, t * LANES:(t + 1) * LANES] = (
                u_ref[pl.ds(r0 * L + t, S5_ROW_TILE, stride=L), :].astype(BF16))
        sl = jnp.dot(x_ref[r0:r0 + S5_ROW_TILE, :], e_ref[...], preferred_element_type=F32)
        for j in range(n_blk):
            base = (b * n_blk + j) * pitch + c0
            sr_ref[base:base + S5_ROW_TILE, :] = sl[:, j * LANES:(j + 1) * LANES]
            si_ref[base:base + S5_ROW_TILE, :] = sl[:, n_re + j * LANES:n_re + (j + 1) * LANES]
    ar, ai = ar_ref[...], ai_ref[...]
    half = LANES

    def scan_step(c, carry):
        s_re, s_im = carry
        rows = pl.ds(c, SUBLANES, stride=pitch)
        x_re, x_im = sr_ref[rows, :], si_ref[rows, :]
        sr_ref[rows, :] = s_re
        si_ref[rows, :] = s_im
        return ar * s_re - ai * s_im + x_re, ar * s_im + ai * s_re + x_im

    zero = jnp.zeros((SUBLANES, half), F32)
    lax.fori_loop(0, n_chunk, scan_step, (zero, zero), unroll=8)

    for b, c0 in tiles:
        r0 = b * n_chunk + c0
        chain = lambda ref, j: ref[(b * n_blk + j) * pitch + c0:(b * n_blk + j) * pitch + c0 + S5_ROW_TILE, :]
        sp = jnp.concatenate([chain(sr_ref, j) for j in range(n_blk)] + [chain(si_ref, j) for j in range(n_blk)],
                             axis=1).astype(BF16)
        xt = x_ref[r0:r0 + S5_ROW_TILE, :]
        y = (jnp.dot(xt, m_ref[...], preferred_element_type=F32)
             + jnp.dot(sp, f_ref[...], preferred_element_type=F32)
             + d_ref[...] * xt.astype(F32))
        for t in range(L):
            y_ref[pl.ds(r0 * L + t, S5_ROW_TILE, stride=L), :] = y[:, t * LANES:(t + 1) * LANES]


def _s5_core(u, tables, bsz, s_len):
    assert bsz == 2, "the scan packs (batch, lane block) into the 8 sublanes of one vreg"
    pm, pe, pf, a_r, a_i, d_vec = tables
    n_tok = bsz * s_len
    n_chunk = s_len // SSM_CHUNK
    pitch = n_chunk + SUBLANES
    lw = SSM_CHUNK * LANES
    n_state = 2 * SSM_SLAB_GROUPS * SSM_STATE
    slab = lambda shape, **kw: pl.BlockSpec((None,) + shape, lambda i: (i,) + (0,) * len(shape), **kw)
    return pl.pallas_call(
        functools.partial(_s5_kernel, n_chunk=n_chunk, pitch=pitch),
        grid=(SSM_SLABS,),
        in_specs=[slab((n_tok, LANES)), slab(pm.shape[1:]), slab(pe.shape[1:]), slab(pf.shape[1:]),
                  slab((SUBLANES, LANES)), slab((SUBLANES, LANES)), slab((1, lw))],
        out_specs=slab((n_tok, LANES)),
        out_shape=jax.ShapeDtypeStruct((SSM_SLABS, n_tok, LANES), F32),
        scratch_shapes=[pltpu.VMEM((lw, lw), BF16), pltpu.VMEM((lw, n_state), BF16), pltpu.VMEM((n_state, lw), BF16),
                        pltpu.VMEM((bsz * n_chunk, lw), BF16),
                        pltpu.VMEM((SUBLANES * pitch, LANES), F32),
                        pltpu.VMEM((SUBLANES * pitch, LANES), F32)],
        compiler_params=_cparams(("parallel",)),
        name="s5_core",
    )(u, pm, pe, pf, a_r, a_i, d_vec)


def _attn_kernel(slope_ref, q_ref, kp_ref, kc_ref, vp_ref, vc_ref, o_ref,
                 q4_ref, k4_ref, v4_ref, k1_ref, v1_ref, m_ref, l_ref, acc_ref):
    slab = pl.program_id(1)
    sb = pl.program_id(2)
    fine = DILATIONS[1]
    nq, nk = ATT_SB // fine, 2 * ATT_SB // fine
    for r in range(fine):
        q4_ref[r * nq:(r + 1) * nq, :] = q_ref[pl.ds(r, nq, stride=fine), :]
        for dst, prev, cur in ((k4_ref, kp_ref, kc_ref), (v4_ref, vp_ref, vc_ref)):
            dst[r * nk:r * nk + nq, :] = prev[pl.ds(r, nq, stride=fine), :]
            dst[r * nk + nq:(r + 1) * nk, :] = cur[pl.ds(r, nq, stride=fine), :]
    for dst, prev, cur in ((k1_ref, kp_ref, kc_ref), (v1_ref, vp_ref, vc_ref)):
        dst[0:ATT_BLK, :] = prev[ATT_SB - ATT_BLK:ATT_SB, :]
        dst[ATT_BLK:ATT_BLK + ATT_SB, :] = cur[...]

    lane = lax.broadcasted_iota(jnp.int32, (ATT_BLK, LANES), 1)
    head0 = lane < ATT_HEAD_DIM
    qi = lax.broadcasted_iota(jnp.int32, (ATT_BLK, 2 * ATT_BLK), 0)
    kj = lax.broadcasted_iota(jnp.int32, (ATT_BLK, 2 * ATT_BLK), 1)
    back = qi + ATT_BLK - kj
    band = (back >= 0) & (back <= ATT_BLK)
    neg_steps = -back.astype(F32)
    slopes = (slope_ref[2 * slab], slope_ref[2 * slab + 1])

    for pat, dil in enumerate(DILATIONS):
        span = ATT_BLK * dil
        bias = [jnp.where(band, (slopes[hh] * float(dil)) * neg_steps, NEG_INF) for hh in range(2)]

        def tile(idx, carry, dil=dil, span=span, pat=pat):
            start = (idx // dil) * span + idx % dil
            seq_ok = jnp.logical_or(sb > 0, idx >= dil)
            valid = (kj >= ATT_BLK) | seq_ok
            if dil == 1:
                start = pl.multiple_of(start, ATT_BLK)
                rows = pl.ds(start, ATT_BLK)
                qt, k_src, v_src, k_rows = q_ref[rows, :], k1_ref, v1_ref, pl.ds(start, 2 * ATT_BLK)
            elif dil == fine:
                first = pl.multiple_of((idx // fine) * ATT_BLK, ATT_BLK)
                rows = pl.ds(start, ATT_BLK, stride=dil)
                qt = q4_ref[pl.ds((idx % fine) * nq + first, ATT_BLK), :]
                k_src, v_src = k4_ref, v4_ref
                k_rows = pl.ds((idx % fine) * nk + (nq - ATT_BLK) + first, 2 * ATT_BLK)
            else:
                rows = pl.ds(start, ATT_BLK, stride=dil)
                qt = q4_ref[pl.ds((idx % fine) * nq + idx // fine, ATT_BLK, stride=fine), :]
                k_src, v_src = k4_ref, v4_ref
                k_rows = pl.ds((idx % fine) * nk + idx // fine, 2 * ATT_BLK, stride=fine)
            kt = k_src[k_rows, :].astype(BF16)
            vt = v_src[k_rows, :].astype(BF16)
            q0 = jnp.where(head0, qt, 0.0)
            scores = [lax.dot_general(qh.astype(BF16), kt, (((1,), (1,)), ((), ())), preferred_element_type=F32)
                      for qh in (q0, qt - q0)]
            return rows, valid, vt, scores

        def softmax(valid, s, hh, bias=bias):
            s = jnp.where(valid, s + bias[hh], NEG_INF)
            m = jnp.max(s, axis=-1, keepdims=True)
            p = jnp.exp(s - m)
            return m, jnp.sum(p, axis=-1, keepdims=True), p.astype(BF16)

        def merge(rows, parts, pat=pat):
            (m0, l0, o0), (m1, l1, o1) = parts
            m_t = jnp.where(head0, m0, m1)
            l_t = jnp.where(head0, l0, l1)
            o_t = jnp.where(head0, o0, o1)
            if pat == 0:
                m_ref[rows, :] = m_t
                l_ref[rows, :] = l_t
                acc_ref[rows, :] = o_t
            else:
                m_old = m_ref[rows, :]
                m_new = jnp.maximum(m_old, m_t)
                a = jnp.exp(m_old - m_new)
                b = jnp.exp(m_t - m_new)
                m_ref[rows, :] = m_new
                l_ref[rows, :] = a * l_ref[rows, :] + b * l_t
                acc_ref[rows, :] = a * acc_ref[rows, :] + b * o_t

        def group(gi, carry):
            tiles = [tile(gi * ATT_GROUP + t, None) for t in range(ATT_GROUP)]
            soft = [[softmax(valid, s, hh) for hh, s in enumerate(scores)] for _, valid, _, scores in tiles]
            outs = [[(m, l, jnp.dot(p, vt, preferred_element_type=F32)) for m, l, p in per_head]
                    for (_, _, vt, _), per_head in zip(tiles, soft)]
            for (rows, _, _, _), parts in zip(tiles, outs):
                merge(rows, parts)
            return carry

        lax.fori_loop(0, ATT_SB // ATT_BLK // ATT_GROUP, group, 0)

    o_ref[...] = (acc_ref[...] / l_ref[...]).astype(o_ref.dtype)


def _dilated_attention(q, k, v, bsz, s_len):
    assert DILATIONS == (DILATIONS[1] ** 2, DILATIONS[1], 1) and ATT_SB == ATT_BLK * DILATIONS[0]
    n_slab = q.shape[0]
    shape4 = (n_slab, bsz, s_len, LANES)
    q, k, v = (t.reshape(shape4) for t in (q, k, v))
    slopes = jnp.asarray([2.0 ** (-8.0 * (h + 1) / ATT_HEADS) for h in range(ATT_HEADS)], F32)
    blk = (None, None, ATT_SB, LANES)
    cur = pl.BlockSpec(blk, lambda b, j, i, s: (j, b, i, 0))
    prev = pl.BlockSpec(blk, lambda b, j, i, s: (j, b, jnp.maximum(i - 1, 0), 0))
    out = pl.pallas_call(
        _attn_kernel,
        grid_spec=pltpu.PrefetchScalarGridSpec(
            num_scalar_prefetch=1,
            grid=(bsz, n_slab, s_len // ATT_SB),
            in_specs=[cur, prev, cur, prev, cur],
            out_specs=cur,
            scratch_shapes=[pltpu.VMEM((ATT_SB, LANES), F32),
                            pltpu.VMEM((2 * ATT_SB, LANES), F32), pltpu.VMEM((2 * ATT_SB, LANES), F32),
                            pltpu.VMEM((ATT_BLK + ATT_SB, LANES), F32), pltpu.VMEM((ATT_BLK + ATT_SB, LANES), F32),
                            pltpu.VMEM((ATT_SB, LANES), F32), pltpu.VMEM((ATT_SB, LANES), F32),
                            pltpu.VMEM((ATT_SB, LANES), F32)]),
        out_shape=jax.ShapeDtypeStruct(shape4, BF16),
        compiler_params=_cparams(("parallel", "parallel", "parallel")),
        name="dilated_attn",
    )(slopes, q, k, k, v, v)
    return out.reshape(n_slab, bsz * s_len, LANES)


def _route_epilogue(h_parts, gain_ref, wr_ref, br_ref, tri_ref, cnt_ref, h_ref, hn_ref, ids_ref, gate_ref, cnt_out_ref):
    sub = h_parts[0].shape[0]
    tm = sub * len(h_parts)
    splits = []
    for g, h in enumerate(h_parts):
        h_ref[g * sub:(g + 1) * sub, :] = h
        hn = _rms(h, gain_ref[...])
        _store_row_tiles(hn_ref, hn, first=g * sub)
        hn_hi = hn.astype(BF16)
        splits.append((hn_hi, (hn - hn_hi.astype(F32)).astype(BF16)))
    logit_parts = []
    for hn_hi, hn_lo in splits:
        hi_part = jnp.dot(hn_hi, wr_ref[...], preferred_element_type=F32)
        logit_parts.append(hi_part[:, :LANES] + (hi_part[:, LANES:]
                                                 + jnp.dot(hn_lo, wr_ref[:, :LANES], preferred_element_type=F32)))
    logits = jnp.concatenate(logit_parts, axis=0) + br_ref[...]
    lane = lax.broadcasted_iota(jnp.int32, (tm, LANES), 1)
    big = jnp.int32(LANES)
    rmax = lambda t: jnp.max(t, axis=-1, keepdims=True)
    rmin = lambda t: jnp.min(t, axis=-1, keepdims=True)
    rsum = lambda t: jnp.sum(t, axis=-1, keepdims=True)
    gmask = lane < N_GROUPS
    gl = jnp.where(gmask, logits, -jnp.inf)
    gmax = rmax(gl)
    ge = jnp.where(gmask, jnp.exp(gl - gmax), 0.0)
    gprob = ge / rsum(ge)
    g_w = rmax(gprob)
    grp = rmin(jnp.where(gmask & (gprob == g_w), lane, big))
    group_of_lane = (lane - N_GROUPS) >> int(math.log2(EXPERTS_PER_GROUP))
    emask = (lane >= N_GROUPS) & (lane < N_GROUPS + N_EXPERTS) & (group_of_lane == grp)
    el = jnp.where(emask, logits, -jnp.inf)
    ee = jnp.where(emask, jnp.exp(el - rmax(el)), 0.0)
    ep = jnp.where(emask, ee / rsum(ee), -1.0)
    p1 = rmax(ep)
    i1 = rmin(jnp.where(ep == p1, lane, big))
    ep2 = jnp.where(lane == i1, -1.0, ep)
    p2 = rmax(ep2)
    i2 = rmin(jnp.where(ep2 == p2, lane, big))
    e1, e2 = i1 - N_GROUPS, i2 - N_GROUPS
    psum = p1 + p2
    gate1, gate2 = g_w * p1 / psum, g_w * p2 / psum
    oh1, oh2 = lane == e1, lane == e2
    member = (oh1 | oh2).astype(BF16)
    before = jnp.dot(tri_ref[...], member, preferred_element_type=F32) + cnt_ref[...]
    r1 = rsum(jnp.where(oh1, before, 0.0)).astype(jnp.int32)
    r2 = rsum(jnp.where(oh2, before, 0.0)).astype(jnp.int32)
    cnt_ref[...] = cnt_ref[...] + jnp.sum(member.astype(F32), axis=0, keepdims=True)
    ids = jnp.where(lane == 0, e1, jnp.where(lane == 1, e2, jnp.where(lane == 2, r1, jnp.where(lane == 3, r2, 0))))
    ids_ref[...] = jnp.transpose(ids)[0:SUBLANES, :]
    gate_ref[...] = jnp.where(lane == 0, gate1, jnp.where(lane == 1, gate2, 0.0))
    cnt_out_ref[...] = jnp.broadcast_to(cnt_ref[...], cnt_out_ref.shape)


def _router_operands(norm_gain, w_rg, b_rg, w_re, b_re):
    pad = LANES - N_GROUPS - N_EXPERTS
    wr = jnp.pad(jnp.concatenate([w_rg, w_re], axis=1).astype(F32), ((0, 0), (0, pad)))
    br = jnp.pad(jnp.concatenate([b_rg, b_re]).astype(F32), (0, pad))[None]
    r = jnp.arange(TOK_TILE)
    tri = (r[None, :] < r[:, None]).astype(BF16)
    wr_hi = wr.astype(BF16)
    wr_lo = (wr - wr_hi.astype(F32)).astype(BF16)
    return norm_gain[None].astype(F32), jnp.concatenate([wr_hi, wr_lo], axis=1), br, tri


def _route_specs(n_tok):
    full = lambda shape: pl.BlockSpec(shape, lambda i: (0,) * len(shape))
    in_specs = [full((1, D_MODEL)), full((D_MODEL, 2 * LANES)), full((1, LANES)), full((TOK_TILE, TOK_TILE))]
    tok = lambda w: pl.BlockSpec((TOK_TILE, w), lambda i: (i, 0))
    out_specs = [tok(D_MODEL), pl.BlockSpec((TOK_TILE * ROW_CHUNKS, LANES), lambda i: (i, 0)),
                 pl.BlockSpec((SUBLANES, TOK_TILE), lambda i: (0, i)), tok(LANES), full((SUBLANES, LANES))]
    out_shape = [jax.ShapeDtypeStruct((n_tok, D_MODEL), F32), jax.ShapeDtypeStruct((n_tok * ROW_CHUNKS, LANES), jnp.uint32),
                 jax.ShapeDtypeStruct((SUBLANES, n_tok), jnp.int32), jax.ShapeDtypeStruct((n_tok, LANES), F32),
                 jax.ShapeDtypeStruct((SUBLANES, LANES), F32)]
    return in_specs, out_specs, out_shape


def _gelu_tanh(x):
    return 0.5 * x * (1.0 + jnp.tanh(math.sqrt(2.0 / math.pi) * (x + 0.044715 * (x * x * x))))


def _outproj_even_kernel(x_ref, y_ref, a_ref, wglu_ref, bglu_ref, wout_ref, gain_ref, wr_ref, br_ref, tri_ref,
                         h_ref, hn_ref, ids_ref, gate_ref, cnt_out_ref, cnt_ref):
    @pl.when(pl.program_id(0) == 0)
    def _():
        cnt_ref[...] = jnp.zeros_like(cnt_ref)

    half = x_ref.shape[0] // ROW_GROUPS
    parts = [slice(g * half, (g + 1) * half) for g in range(ROW_GROUPS)]
    y = [_gelu_tanh(jnp.concatenate([y_ref[j, rows, :] for j in range(SSM_SLABS)], axis=1)) for rows in parts]
    glu = [jnp.dot(t.astype(BF16), wglu_ref[...], preferred_element_type=F32) for t in y]
    h_out = []
    for rows, t, g in zip(parts, y, glu):
        t = t * jax.nn.sigmoid(g + bglu_ref[...])
        mix = jnp.dot(t.astype(BF16), wout_ref[0:SSM_WIDTH, :], preferred_element_type=F32)
        for j in range(ATT_WIDTH // LANES):
            w_rows = slice(SSM_WIDTH + j * LANES, SSM_WIDTH + (j + 1) * LANES)
            mix = mix + jnp.dot(a_ref[j, rows, :], wout_ref[w_rows, :], preferred_element_type=F32)
        h_out.append(x_ref[rows, :] + mix)
    _route_epilogue(h_out, gain_ref, wr_ref, br_ref, tri_ref, cnt_ref,
                    h_ref, hn_ref, ids_ref, gate_ref, cnt_out_ref)


def _outproj_even(x2, y_pre, attn, w_glu, b_glu, w_out, route_ops):
    n_tok = x2.shape[0]
    n_slab = attn.shape[0]
    r_in, r_out, r_shape = _route_specs(n_tok)
    full = lambda shape: pl.BlockSpec(shape, lambda i: (0,) * len(shape))
    return pl.pallas_call(
        _outproj_even_kernel,
        grid=(n_tok // TOK_TILE,),
        in_specs=[pl.BlockSpec((TOK_TILE, D_MODEL), lambda i: (i, 0)),
                  pl.BlockSpec((SSM_SLABS, TOK_TILE, LANES), lambda i: (0, i, 0)),
                  pl.BlockSpec((n_slab, TOK_TILE, LANES), lambda i: (0, i, 0)),
                  full(w_glu.shape), full((1, SSM_WIDTH)), full(w_out.shape)] + r_in,
        out_specs=r_out, out_shape=r_shape,
        scratch_shapes=[pltpu.VMEM((1, LANES), F32)],
        compiler_params=_cparams(("arbitrary",)),
        name="outproj_even",
    )(x2, y_pre, attn, w_glu.astype(BF16), b_glu[None].astype(F32), w_out.astype(BF16), *route_ops)


def _conv_layer_kernel(h_ref, pgate_ref, y0_ref, y1_ref, gmix_ref, win_ref, cw_ref, wout_ref, gain_ref, wr_ref, br_ref,
                       tri_ref, ho_ref, hn_ref, ids_ref, gate_ref, cnt_out_ref, cnt_ref, zc_ref, *, tiles_per_seq):
    i = pl.program_id(0)

    @pl.when(i == 0)
    def _():
        cnt_ref[...] = jnp.zeros_like(cnt_ref)

    @pl.when(i % tiles_per_seq == 0)
    def _():
        zc_ref[0:SUBLANES, :] = jnp.zeros((SUBLANES, D_MODEL), F32)

    tm = h_ref.shape[0]
    c = D_MODEL
    half = tm // ROW_GROUPS
    parts = tuple(g * half for g in range(ROW_GROUPS))
    h_in, hn = {}, {}
    for r0 in parts:
        rows = slice(r0, r0 + half)
        pgate = pgate_ref[rows, :]
        h_in[r0] = (h_ref[rows, :] + pgate[:, 0:1] * _load_row_tiles(y0_ref, first=r0, n=half)
                    + pgate[:, 1:2] * _load_row_tiles(y1_ref, first=r0, n=half))
        hn[r0] = _rms(h_in[r0], gmix_ref[...]).astype(BF16)
    b_gate, zc = {}, {}
    for r0 in parts:
        b_gate[r0] = jnp.dot(hn[r0], win_ref[:, 0:c], preferred_element_type=F32)
        zc[r0] = (jnp.dot(hn[r0], win_ref[:, c:2 * c], preferred_element_type=F32)
                  * jnp.dot(hn[r0], win_ref[:, 2 * c:3 * c], preferred_element_type=F32))
    h_out = []
    for r0 in parts:
        z0 = SUBLANES + r0
        zc_ref[z0:z0 + half, :] = zc[r0]
        conv = cw_ref[CONV_TAPS - 1:CONV_TAPS, :] * zc[r0]
        for back in range(1, CONV_TAPS):
            tap = CONV_TAPS - 1 - back
            conv = conv + cw_ref[tap:tap + 1, :] * zc_ref[z0 - back:z0 - back + half, :]
        mix = jnp.dot((b_gate[r0] * conv).astype(BF16), wout_ref[...], preferred_element_type=F32)
        h_out.append(h_in[r0] + mix)
    zc_ref[0:SUBLANES, :] = zc_ref[tm:tm + SUBLANES, :]
    _route_epilogue(h_out, gain_ref, wr_ref, br_ref, tri_ref, cnt_ref,
                    ho_ref, hn_ref, ids_ref, gate_ref, cnt_out_ref)


def _conv_layer(h1, pgate, y2, gain_mix, w_in, conv_w, w_out, route_ops, s_len):
    n_tok = h1.shape[0]
    r_in, r_out, r_shape = _route_specs(n_tok)
    full = lambda shape: pl.BlockSpec(shape, lambda i: (0,) * len(shape))
    return pl.pallas_call(
        functools.partial(_conv_layer_kernel, tiles_per_seq=s_len // TOK_TILE),
        grid=(n_tok // TOK_TILE,),
        in_specs=_moe_specs(n_tok) + [full((1, D_MODEL)), full(w_in.shape), full(conv_w.shape), full(w_out.shape)] + r_in,
        out_specs=r_out, out_shape=r_shape,
        scratch_shapes=[pltpu.VMEM((1, LANES), F32), pltpu.VMEM((TOK_TILE + SUBLANES, D_MODEL), F32)],
        compiler_params=_cparams(("arbitrary",)),
        name="conv_layer",
    )(h1, pgate, y2, y2, gain_mix[None].astype(F32), w_in.astype(BF16), conv_w.astype(F32), w_out.astype(BF16),
      *route_ops)


def _row_map(experts, ranks, first_row, n_rows):
    n_slot, n_tok = len(experts), experts[0].shape[0]
    n_assign = n_slot * n_tok
    n_src = n_rows + 2 * MOE_BLK
    mesh = plsc.VectorSubcoreMesh(core_axis_name="core", subcore_axis_name="subcore",
                                  num_cores=SC_CORES, num_subcores=SC_SUBCORES)
    tok_vec = pltpu.VMEM((n_tok,), jnp.int32)

    @functools.partial(
        pl.kernel, mesh=mesh, out_type=jax.ShapeDtypeStruct((n_src,), jnp.int32),
        scratch_types=[tok_vec] * (2 * n_slot) + [pltpu.VMEM((N_EXPERTS,), jnp.int32), pltpu.VMEM((n_src,), jnp.int32)],
        compiler_params=pltpu.CompilerParams(needs_layout_passes=False), name="moe_row_map")
    def row_map(*refs):
        ins, src_hbm, scratch = refs[:2 * n_slot + 2], refs[2 * n_slot + 2], refs[2 * n_slot + 3:]
        first_v, src_v = scratch[2 * n_slot], scratch[2 * n_slot + 1]

        @pl.when(jnp.logical_and(lax.axis_index("core") == 0, lax.axis_index("subcore") == 0))
        def _():
            for hbm, vmem in zip(ins, scratch):
                pltpu.sync_copy(hbm, vmem)
            lanes = lax.iota(jnp.int32, SC_LANES)

            for slot in range(n_slot):
                e_v, r_v = scratch[slot], scratch[n_slot + slot]

                @pl.loop(0, n_tok, step=SC_LANES)
                def _(t):
                    row = plsc.load_gather(first_v, [e_v[pl.ds(t, SC_LANES)]]) + r_v[pl.ds(t, SC_LANES)]
                    plsc.store_scatter(src_v, [row + MOE_BLK], slot * n_tok + t + lanes)

            pltpu.sync_copy(src_v, src_hbm)

    dump_rows = n_assign + (jnp.arange(n_src, dtype=jnp.int32) & (MOE_BLK - 1))
    return row_map(*experts, *ranks, first_row, dump_rows)


def _expert_kernel(blk_e_ref, n_used_ref, src_ref, next_e_ref, wslot_ref, hn_ref, wg_ref, wu_ref, wd_ref, y2_ref,
                   xbuf, ybuf, wg_f, wu_f, wd_f, wg_s, wu_s, wd_s, ssem, wsem, *, n_tok, layer):
    b = pl.program_id(0)
    n_used = n_used_ref[0]
    cur = b % 2
    nxt = 1 - cur

    tile = lambda i: pl.ds(i * ROW_CHUNKS, ROW_CHUNKS)

    def scatter(blk, slot, i):
        row = src_ref[(blk + 1) * MOE_BLK + i]
        return pltpu.make_async_copy(ybuf.at[slot, tile(i)], y2_ref.at[row], ssem.at[slot])

    wait_block = lambda slot: pltpu.make_async_copy(ybuf.at[slot], ybuf.at[slot], ssem.at[slot]).wait()

    def weights(expert, slot, act):
        for hbm, buf in ((wg_ref, wg_f), (wu_ref, wu_f), (wd_ref, wd_f)):
            act(pltpu.make_async_copy(hbm.at[layer, expert], buf.at[slot], wsem.at[slot]))

    @pl.when(b == 0)
    def _():
        ybuf[1] = jnp.zeros(ybuf.shape[1:], ybuf.dtype)
        weights(blk_e_ref[0], wslot_ref[0], lambda c: c.start())

    @pl.when(b < n_used)
    def _():
        new_expert = jnp.logical_or(b == 0, blk_e_ref[b] != blk_e_ref[jnp.maximum(b - 1, 0)])

        @pl.when(new_expert)
        def _():
            slot = wslot_ref[b]
            weights(blk_e_ref[b], slot, lambda c: c.wait())
            wg_s[...] = wg_f[slot].astype(BF16)
            wu_s[...] = wu_f[slot].astype(BF16)
            wd_s[...] = wd_f[slot].astype(BF16)

            @pl.when(next_e_ref[b] >= 0)
            def _():
                weights(next_e_ref[b], 1 - slot, lambda c: c.start())

        @pl.when(b >= 1)
        def _():
            wait_block(cur)

        for i in range(MOE_BLK):
            tok = src_ref[(b + 1) * MOE_BLK + i] & (n_tok - 1)
            xbuf[tile(i), :] = hn_ref[pl.ds(pl.multiple_of(tok * ROW_CHUNKS, ROW_CHUNKS), ROW_CHUNKS), :]
            scatter(b - 1, nxt, i).start(priority=i % 2)
        x = _load_row_tiles(xbuf).astype(BF16)
        g = jnp.dot(x, wg_s[...], preferred_element_type=F32)
        u = jnp.dot(x, wu_s[...], preferred_element_type=F32)
        hb = (g * jax.nn.sigmoid(g) * u).astype(BF16)
        _store_row_tiles(ybuf, jnp.dot(hb, wd_s[...], preferred_element_type=F32), (cur,))

    @pl.when(b == n_used)
    def _():
        wait_block(cur)
        for i in range(MOE_BLK):
            scatter(b - 1, nxt, i).start(priority=i % 2)
        wait_block(nxt)


def _expert_ffn(hn, src, blk_e, n_used, next_e, wslot, w_g, w_u, w_d, layer):
    n_tok = hn.shape[0] // ROW_CHUNKS
    assert n_tok & (n_tok - 1) == 0, "dump-row aliasing masks the token index with T - 1"
    n_blocks = (src.shape[0] - 2 * MOE_BLK) // MOE_BLK
    in_hbm = pl.BlockSpec(memory_space=pltpu.HBM)
    return pl.pallas_call(
        functools.partial(_expert_kernel, n_tok=n_tok, layer=layer),
        grid_spec=pltpu.PrefetchScalarGridSpec(
            num_scalar_prefetch=5,
            grid=(n_blocks + 1,),
            in_specs=[pl.BlockSpec(hn.shape, lambda b, *_: (0, 0), pipeline_mode=pl.Buffered(1)),
                      in_hbm, in_hbm, in_hbm],
            out_specs=pl.BlockSpec(memory_space=pltpu.HBM),
            scratch_shapes=[pltpu.VMEM((MOE_BLK * ROW_CHUNKS, LANES), jnp.uint32),
                            pltpu.VMEM((2, MOE_BLK * ROW_CHUNKS, LANES), jnp.uint32),
                            pltpu.VMEM((2, D_MODEL, D_EXPERT), F32), pltpu.VMEM((2, D_MODEL, D_EXPERT), F32),
                            pltpu.VMEM((2, D_EXPERT, D_MODEL), F32),
                            pltpu.VMEM((D_MODEL, D_EXPERT), BF16), pltpu.VMEM((D_MODEL, D_EXPERT), BF16),
                            pltpu.VMEM((D_EXPERT, D_MODEL), BF16),
                            pltpu.SemaphoreType.DMA((2,)), pltpu.SemaphoreType.DMA((2,))]),
        out_shape=jax.ShapeDtypeStruct((2 * n_tok + MOE_BLK, ROW_CHUNKS, LANES), jnp.uint32),
        compiler_params=_cparams(("arbitrary",)),
        name="moe_experts",
    )(blk_e, n_used, src, next_e, wslot, hn, w_g, w_u, w_d)


def _moe_add_kernel(h_ref, gate_ref, y0_ref, y1_ref, o_ref):
    gate = gate_ref[...]
    o_ref[...] = h_ref[...] + gate[:, 0:1] * _load_row_tiles(y0_ref) + gate[:, 1:2] * _load_row_tiles(y1_ref)


def _moe_specs(n_tok):
    slots = n_tok // TOK_TILE
    y2_rows = (TOK_TILE * ROW_CHUNKS, LANES)
    return [pl.BlockSpec((TOK_TILE, D_MODEL), lambda i: (i, 0)), pl.BlockSpec((TOK_TILE, LANES), lambda i: (i, 0)),
            pl.BlockSpec(y2_rows, lambda i: (i, 0)), pl.BlockSpec(y2_rows, lambda i: (slots + i, 0))]


def _moe_add(h, gate, y2):
    n_tok = h.shape[0]
    return pl.pallas_call(
        _moe_add_kernel,
        grid=(n_tok // TOK_TILE,),
        in_specs=_moe_specs(n_tok),
        out_specs=pl.BlockSpec((TOK_TILE, D_MODEL), lambda i: (i, 0)),
        out_shape=jax.ShapeDtypeStruct((n_tok, D_MODEL), F32),
        compiler_params=_cparams(("parallel",)),
        name="moe_add",
    )(h, gate, y2, y2)


def _moe(hn, ids, counts, w_g, w_u, w_d, layer):
    n_tok = hn.shape[0] // ROW_CHUNKS
    n_assign = 2 * n_tok
    n_blocks = n_assign // MOE_BLK + N_EXPERTS
    n_rows = n_blocks * MOE_BLK
    cnt = counts[0, :N_EXPERTS].astype(jnp.int32)
    padded = (cnt + MOE_BLK - 1) // MOE_BLK * MOE_BLK
    pends = jnp.cumsum(padded).astype(jnp.int32)
    pstarts = pends - padded
    blk_start = jnp.arange(n_blocks + 1, dtype=jnp.int32) * MOE_BLK
    blk_e = jnp.minimum(jnp.sum(pends[None, :] <= blk_start[:, None], axis=1), N_EXPERTS - 1).astype(jnp.int32)
    n_used = (pends[-1:] // MOE_BLK).astype(jnp.int32)
    used = (padded > 0)[None, :]
    e_idx = jnp.arange(N_EXPERTS, dtype=jnp.int32)[None, :]
    next_e = jnp.min(jnp.where(used & (e_idx > blk_e[:, None]), e_idx, N_EXPERTS), axis=1)
    next_e = jnp.where(next_e < N_EXPERTS, next_e, -1).astype(jnp.int32)
    ordinal = jnp.sum((used & (e_idx <= blk_e[:, None])).astype(jnp.int32), axis=1) - 1
    src = _row_map((ids[0], ids[1]), (ids[2], ids[3]), pstarts, n_rows)
    y2 = _expert_ffn(hn, src, blk_e, n_used, next_e, (ordinal % 2).astype(jnp.int32), w_g, w_u, w_d, layer)
    return y2.reshape(-1, LANES)


def kernel(x, norm_mix, norm_ffn, w_in_even, ssm_a_re, ssm_a_im, ssm_b_re, ssm_b_im, ssm_c_re, ssm_c_im, ssm_d,
           ssm_log_step, w_glu, b_glu, q_norm, k_norm, w_out_even, w_in_conv, conv_w, w_out_conv, w_router_group,
           b_router_group, w_router_expert, b_router_expert, w_expert_gate, w_expert_up, w_expert_down):
    bsz, s_len, d = x.shape
    x2 = x.reshape(bsz * s_len, d)
    route = lambda layer: _router_operands(norm_ffn[layer], w_router_group[layer], b_router_group[layer],
                                           w_router_expert[layer], b_router_expert[layer])
    experts = lambda layer: (w_expert_gate, w_expert_up, w_expert_down, layer)

    u, q, k, v = _inproj_even(x2, norm_mix[0], w_in_even[0], q_norm[0], k_norm[0])
    tables = _s5_tables(ssm_a_re[0], ssm_a_im[0], ssm_b_re[0], ssm_b_im[0], ssm_c_re[0], ssm_c_im[0], ssm_d[0],
                        ssm_log_step[0])
    y_pre = _s5_core(u, tables, bsz, s_len)
    attn = _dilated_attention(q, k, v, bsz, s_len)
    h, hn, ids, gate, counts = _outproj_even(x2, y_pre, attn, w_glu[0], b_glu[0], w_out_even[0], route(0))
    y2 = _moe(hn, ids, counts, *experts(0))

    h, hn, ids, gate, counts = _conv_layer(h, gate, y2, norm_mix[1], w_in_conv[0], conv_w[0], w_out_conv[0],
                                           route(1), s_len)
    y2 = _moe(hn, ids, counts, *experts(1))
    return _moe_add(h, gate, y2).reshape(bsz, s_len, d)
```

```python
import functools
import math

import jax
import jax.numpy as jnp
from jax import lax
from jax.experimental import pallas as pl
from jax.experimental.pallas import tpu as pltpu
from jax.experimental.pallas import tpu_sc as plsc

F32 = jnp.float32
BF16 = jnp.bfloat16

D_MODEL = 1024
SSM_GROUP = 16
SSM_GROUPS = 40
SSM_WIDTH = SSM_GROUP * SSM_GROUPS
SSM_STATE = 64
ATT_HEADS = 6
ATT_HEAD_DIM = 64
ATT_WIDTH = ATT_HEADS * ATT_HEAD_DIM
DILATIONS = (16, 4, 1)
ATT_BLK = 128
CONV_TAPS = 3
N_GROUPS = 4
EXPERTS_PER_GROUP = 8
N_EXPERTS = N_GROUPS * EXPERTS_PER_GROUP
D_EXPERT = 512
MOE_BLK = 256
RMS_EPS = 1e-6
NEG_INF = -1e30

LANES = 128
SUBLANES = 8
SC_CORES, SC_SUBCORES, SC_LANES = 2, 16, 16
VMEM_LIMIT = 56 * 1024 * 1024

TOK_TILE = 512
ROW_GROUPS = 2
SSM_CHUNK = 8
SSM_SLAB_GROUPS = LANES // SSM_GROUP
SSM_SLABS = SSM_WIDTH // LANES
S5_ROW_TILE = 256
ATT_SB = 2048
ATT_GROUP = 4
ROW_CHUNKS = D_MODEL // (2 * LANES)


def _cparams(sem):
    return pltpu.CompilerParams(dimension_semantics=sem, vmem_limit_bytes=VMEM_LIMIT)


def _rms(x, gain):
    return x * lax.rsqrt(jnp.mean(x * x, axis=-1, keepdims=True) + RMS_EPS) * gain


HIGH_HALF = 0xFFFF0000


def _load_row_tiles(ref, lead=(), first=0, n=None):
    n = ref.shape[-2] // ROW_CHUNKS if n is None else n
    bits = jnp.concatenate([ref[lead + (pl.ds(first * ROW_CHUNKS + c, n, stride=ROW_CHUNKS), slice(None))]
                            for c in range(ROW_CHUNKS)], axis=1)
    low = lax.bitcast_convert_type(bits << 16, F32)
    high = lax.bitcast_convert_type(bits & jnp.uint32(HIGH_HALF), F32)
    return jnp.concatenate([low, high], axis=1)


def _store_row_tiles(ref, value, lead=(), first=0):
    n, half = value.shape[0], value.shape[1] // 2
    bf16_bits = lambda t: lax.bitcast_convert_type(t.astype(BF16).astype(F32), jnp.uint32)
    bits = (bf16_bits(value[:, :half]) >> 16) | (bf16_bits(value[:, half:]) & jnp.uint32(HIGH_HALF))
    for c in range(ROW_CHUNKS):
        rows = pl.ds(first * ROW_CHUNKS + c, n, stride=ROW_CHUNKS)
        ref[lead + (rows, slice(None))] = bits[:, c * LANES:(c + 1) * LANES]


def _head_norm(t, gain, bd):
    tt = t * t
    hi = tt.astype(BF16)
    lo = (tt - hi.astype(F32)).astype(BF16)
    ss = jnp.dot(hi, bd, preferred_element_type=F32) + jnp.dot(lo, bd, preferred_element_type=F32)
    return t * lax.rsqrt(ss * (1.0 / ATT_HEAD_DIM) + RMS_EPS) * gain


def _inproj_even_kernel(x_ref, g_ref, w_ref, bd_ref, qn_ref, kn_ref, u_ref, q_ref, k_ref, v_ref):
    half = x_ref.shape[0] // ROW_GROUPS
    parts = [slice(g * half, (g + 1) * half) for g in range(ROW_GROUPS)]
    hn = [_rms(x_ref[rows, :], g_ref[...]).astype(BF16) for rows in parts]
    projs = [jnp.dot(t, w_ref[...], preferred_element_type=F32) for t in hn]
    bd = bd_ref[...]
    o = SSM_WIDTH
    for rows, proj in zip(parts, projs):
        for j in range(SSM_SLABS):
            u_ref[j, rows, :] = proj[:, j * LANES:(j + 1) * LANES]
        q = _head_norm(proj[:, o:o + ATT_WIDTH], qn_ref[...], bd) * (ATT_HEAD_DIM ** -0.5)
        k = _head_norm(proj[:, o + ATT_WIDTH:o + 2 * ATT_WIDTH], kn_ref[...], bd)
        v = proj[:, o + 2 * ATT_WIDTH:o + 3 * ATT_WIDTH]
        for j in range(ATT_WIDTH // LANES):
            q_ref[j, rows, :] = q[:, j * LANES:(j + 1) * LANES]
            k_ref[j, rows, :] = k[:, j * LANES:(j + 1) * LANES]
            v_ref[j, rows, :] = v[:, j * LANES:(j + 1) * LANES]


def _inproj_even(x2, gain, w_in, q_norm, k_norm):
    n_tok = x2.shape[0]
    n_slab = ATT_WIDTH // LANES
    head_of = jnp.arange(ATT_WIDTH) // ATT_HEAD_DIM
    bd = (head_of[:, None] == head_of[None, :]).astype(BF16)
    qn = jnp.tile(q_norm.astype(F32), ATT_HEADS)[None]
    kn = jnp.tile(k_norm.astype(F32), ATT_HEADS)[None]
    full = lambda shape: pl.BlockSpec(shape, lambda i: (0,) * len(shape))
    slab = pl.BlockSpec((n_slab, TOK_TILE, LANES), lambda i: (0, i, 0))
    slab_shape = jax.ShapeDtypeStruct((n_slab, n_tok, LANES), F32)
    return pl.pallas_call(
        _inproj_even_kernel,
        grid=(n_tok // TOK_TILE,),
        in_specs=[pl.BlockSpec((TOK_TILE, D_MODEL), lambda i: (i, 0)), full((1, D_MODEL)),
                  full(w_in.shape), full(bd.shape), full(qn.shape), full(kn.shape)],
        out_specs=[pl.BlockSpec((SSM_SLABS, TOK_TILE, LANES), lambda i: (0, i, 0)), slab, slab, slab],
        out_shape=[jax.ShapeDtypeStruct((SSM_SLABS, n_tok, LANES), F32), slab_shape, slab_shape, slab_shape],
        compiler_params=_cparams(("parallel",)),
        name="inproj_even",
    )(x2, gain[None].astype(F32), w_in.astype(BF16), bd, qn, kn)


def _s5_tables(a_re, a_im, b_re, b_im, c_re, c_im, d_skip, log_step):
    f = lambda t: t.astype(F32)
    a_re, a_im, b_re, b_im, c_re, c_im = map(f, (a_re, a_im, b_re, b_im, c_re, c_im))
    L = SSM_CHUNK
    step = jnp.exp(f(log_step))[:, None]
    ks = jnp.arange(L + 1, dtype=F32)[:, None, None]
    mag = jnp.exp(ks * (a_re * step)[None])
    ang = ks * (a_im * step)[None]
    pw_re, pw_im = mag * jnp.cos(ang), mag * jnp.sin(ang)
    nr, ni = pw_re[1] - 1.0, pw_im[1]
    den = a_re * a_re + a_im * a_im
    z_re, z_im = (nr * a_re + ni * a_im) / den, (ni * a_re - nr * a_im) / den
    bb_re = z_re[..., None] * b_re - z_im[..., None] * b_im
    bb_im = z_re[..., None] * b_im + z_im[..., None] * b_re
    lb_re = pw_re[..., None] * bb_re[None] - pw_im[..., None] * bb_im[None]
    lb_im = pw_re[..., None] * bb_im[None] + pw_im[..., None] * bb_re[None]
    kk = jnp.einsum('gop,kgpi->gkio', c_re, lb_re[:L]) - jnp.einsum('gop,kgpi->gkio', c_im, lb_im[:L])
    ti = jnp.arange(L)
    lag = ti[None, :] - ti[:, None]
    m = jnp.where((lag >= 0)[None, :, :, None, None], kk[:, jnp.maximum(lag, 0)], 0.0)
    ns, gs = SSM_SLABS, SSM_SLAB_GROUPS
    lw = L * LANES
    pm = m.reshape(ns, gs, L, L, SSM_GROUP, SSM_GROUP).transpose(0, 2, 1, 4, 3, 5).reshape(ns, lw, L * SSM_GROUP)
    fold_e = lambda t: t[:L][::-1].reshape(L, ns, gs, SSM_STATE, SSM_GROUP).transpose(1, 0, 2, 4, 3).reshape(ns, lw, SSM_STATE)
    pe = jnp.concatenate([fold_e(lb_re), fold_e(lb_im)], axis=-1)
    pw1_re, pw1_im = (t[1:].transpose(1, 0, 2)[:, :, None, :] for t in (pw_re, pw_im))
    cl_re = c_re[:, None] * pw1_re - c_im[:, None] * pw1_im
    cl_im = c_re[:, None] * pw1_im + c_im[:, None] * pw1_re
    fold_f = lambda t: t.reshape(ns, gs, L, SSM_GROUP, SSM_STATE).transpose(0, 4, 2, 1, 3).reshape(ns, SSM_STATE, lw)
    pf = jnp.concatenate([fold_f(cl_re), fold_f(-cl_im)], axis=1)
    per_chain = lambda t: jnp.tile(t.reshape(ns, gs * SSM_STATE // LANES, LANES), (1, 2, 1))
    d_vec = jnp.tile(f(d_skip).reshape(ns, 1, LANES), (1, 1, L))
    return pm.astype(BF16), pe.astype(BF16), pf.astype(BF16), per_chain(pw_re[L]), per_chain(pw_im[L]), d_vec


def _log2(n):
    assert n & (n - 1) == 0, "index fields are split with shifts"
    return n.bit_length() - 1


def _iota2(shape):
    return lax.broadcasted_iota(jnp.int32, shape, 0), lax.broadcasted_iota(jnp.int32, shape, 1)


def _widen(compact, group_major_cols, shape, r_shift, c_shift, sel_rows):
    gmask = SSM_SLAB_GROUPS - 1
    state_bits, wide_state_bits = _log2(SSM_STATE), _log2(SSM_SLAB_GROUPS * SSM_STATE)
    chan_bits, lane_bits = _log2(SSM_GROUP), _log2(LANES)
    if sel_rows:
        r, c = _iota2((shape[0], compact.shape[0]))
        sel = ((r >> wide_state_bits) == (c >> state_bits)) & ((r & (SSM_STATE - 1)) == (c & (SSM_STATE - 1)))
        wide = jnp.dot(sel.astype(BF16), compact, preferred_element_type=F32)
    else:
        r, c = _iota2((compact.shape[1], shape[1]))
        if group_major_cols:
            sel = ((c >> wide_state_bits) == (r >> state_bits)) & ((c & (SSM_STATE - 1)) == (r & (SSM_STATE - 1)))
        else:
            sel = ((c >> lane_bits) == (r >> chan_bits)) & ((c & (SSM_GROUP - 1)) == (r & (SSM_GROUP - 1)))
        wide = jnp.dot(compact, sel.astype(BF16), preferred_element_type=F32)
    r, c = _iota2(shape)
    keep = ((r >> r_shift) & gmask) == ((c >> c_shift) & gmask)
    return jnp.where(keep, wide, 0.0).astype(BF16)


def _s5_kernel(u_ref, pm_ref, pe_ref, pf_ref, ar_ref, ai_ref, d_ref, y_ref, m_ref, e_ref, f_ref, x_ref, sr_ref, si_ref,
               *, n_chunk, pitch):
    L = SSM_CHUNK
    n_blk = SSM_SLAB_GROUPS * SSM_STATE // LANES
    n_re = n_blk * LANES
    lw = L * LANES
    chan_bits, state_bits = _log2(SSM_GROUP), _log2(SSM_STATE)
    m_ref[...] = _widen(pm_ref[...], False, (lw, lw), chan_bits, chan_bits, False)
    e_ref[...] = _widen(pe_ref[...], True, (lw, 2 * n_re), chan_bits, state_bits, False)
    f_ref[...] = _widen(pf_ref[...], False, (2 * n_re, lw), state_bits, chan_bits, True)
    tiles = [(b, c0) for b in range(2) for c0 in range(0, n_chunk, S5_ROW_TILE)]
    for b, c0 in tiles:
        r0 = b * n_chunk + c0
        for t in range(L):
            x_ref[r0:r0 + S5_ROW_TILE, t * LANES:(t + 1) * LANES] = (
                u_ref[pl.ds(r0 * L + t, S5_ROW_TILE, stride=L), :].astype(BF16))
        sl = jnp.dot(x_ref[r0:r0 + S5_ROW_TILE, :], e_ref[...], preferred_element_type=F32)
        for j in range(n_blk):
            base = (b * n_blk + j) * pitch + c0
            sr_ref[base:base + S5_ROW_TILE, :] = sl[:, j * LANES:(j + 1) * LANES]
            si_ref[base:base + S5_ROW_TILE, :] = sl[:, n_re + j * LANES:n_re + (j + 1) * LANES]
    ar, ai = ar_ref[...], ai_ref[...]
    half = LANES

    def scan_step(c, carry):
        s_re, s_im = carry
        rows = pl.ds(c, SUBLANES, stride=pitch)
        x_re, x_im = sr_ref[rows, :], si_ref[rows, :]
        sr_ref[rows, :] = s_re
        si_ref[rows, :] = s_im
        return ar * s_re - ai * s_im + x_re, ar * s_im + ai * s_re + x_im

    zero = jnp.zeros((SUBLANES, half), F32)
    lax.fori_loop(0, n_chunk, scan_step, (zero, zero), unroll=8)

    for b, c0 in tiles:
        r0 = b * n_chunk + c0
        chain = lambda ref, j: ref[(b * n_blk + j) * pitch + c0:(b * n_blk + j) * pitch + c0 + S5_ROW_TILE, :]
        sp = jnp.concatenate([chain(sr_ref, j) for j in range(n_blk)] + [chain(si_ref, j) for j in range(n_blk)],
                             axis=1).astype(BF16)
        xt = x_ref[r0:r0 + S5_ROW_TILE, :]
        y = (jnp.dot(xt, m_ref[...], preferred_element_type=F32)
             + jnp.dot(sp, f_ref[...], preferred_element_type=F32)
             + d_ref[...] * xt.astype(F32))
        for t in range(L):
            y_ref[pl.ds(r0 * L + t, S5_ROW_TILE, stride=L), :] = y[:, t * LANES:(t + 1) * LANES]


def _s5_core(u, tables, bsz, s_len):
    assert bsz == 2, "the scan packs (batch, lane block) into the 8 sublanes of one vreg"
    pm, pe, pf, a_r, a_i, d_vec = tables
    n_tok = bsz * s_len
    n_chunk = s_len // SSM_CHUNK
    pitch = n_chunk + SUBLANES
    lw = SSM_CHUNK * LANES
    n_state = 2 * SSM_SLAB_GROUPS * SSM_STATE
    slab = lambda shape, **kw: pl.BlockSpec((None,) + shape, lambda i: (i,) + (0,) * len(shape), **kw)
    return pl.pallas_call(
        functools.partial(_s5_kernel, n_chunk=n_chunk, pitch=pitch),
        grid=(SSM_SLABS,),
        in_specs=[slab((n_tok, LANES)), slab(pm.shape[1:]), slab(pe.shape[1:]), slab(pf.shape[1:]),
                  slab((SUBLANES, LANES)), slab((SUBLANES, LANES)), slab((1, lw))],
        out_specs=slab((n_tok, LANES)),
        out_shape=jax.ShapeDtypeStruct((SSM_SLABS, n_tok, LANES), F32),
        scratch_shapes=[pltpu.VMEM((lw, lw), BF16), pltpu.VMEM((lw, n_state), BF16), pltpu.VMEM((n_state, lw), BF16),
                        pltpu.VMEM((bsz * n_chunk, lw), BF16),
                        pltpu.VMEM((SUBLANES * pitch, LANES), F32),
                        pltpu.VMEM((SUBLANES * pitch, LANES), F32)],
        compiler_params=_cparams(("parallel",)),
        name="s5_core",
    )(u, pm, pe, pf, a_r, a_i, d_vec)


def _attn_kernel(slope_ref, q_ref, kp_ref, kc_ref, vp_ref, vc_ref, o_ref,
                 q4_ref, k4_ref, v4_ref, k1_ref, v1_ref, m_ref, l_ref, acc_ref):
    slab = pl.program_id(1)
    sb = pl.program_id(2)
    fine = DILATIONS[1]
    nq, nk = ATT_SB // fine, 2 * ATT_SB // fine
    for r in range(fine):
        q4_ref[r * nq:(r + 1) * nq, :] = q_ref[pl.ds(r, nq, stride=fine), :]
        for dst, prev, cur in ((k4_ref, kp_ref, kc_ref), (v4_ref, vp_ref, vc_ref)):
            dst[r * nk:r * nk + nq, :] = prev[pl.ds(r, nq, stride=fine), :]
            dst[r * nk + nq:(r + 1) * nk, :] = cur[pl.ds(r, nq, stride=fine), :]
    for dst, prev, cur in ((k1_ref, kp_ref, kc_ref), (v1_ref, vp_ref, vc_ref)):
        dst[0:ATT_BLK, :] = prev[ATT_SB - ATT_BLK:ATT_SB, :]
        dst[ATT_BLK:ATT_BLK + ATT_SB, :] = cur[...]

    lane = lax.broadcasted_iota(jnp.int32, (ATT_BLK, LANES), 1)
    head0 = lane < ATT_HEAD_DIM
    head0_keys = lax.broadcasted_iota(jnp.int32, (2 * ATT_BLK, LANES), 1) < ATT_HEAD_DIM
    ones = jnp.ones((2 * ATT_BLK, LANES), BF16)
    qi = lax.broadcasted_iota(jnp.int32, (ATT_BLK, 2 * ATT_BLK), 0)
    kj = lax.broadcasted_iota(jnp.int32, (ATT_BLK, 2 * ATT_BLK), 1)
    back = qi + ATT_BLK - kj
    band = (back >= 0) & (back <= ATT_BLK)
    neg_steps = -back.astype(F32)
    slopes = (slope_ref[2 * slab], slope_ref[2 * slab + 1])

    for pat, dil in enumerate(DILATIONS):
        span = ATT_BLK * dil
        bias = [jnp.where(band, (slopes[hh] * float(dil)) * neg_steps, NEG_INF) for hh in range(2)]

        def tile(idx, carry, dil=dil, span=span, pat=pat):
            start = (idx // dil) * span + idx % dil
            seq_ok = jnp.logical_or(sb > 0, idx >= dil)
            valid = (kj >= ATT_BLK) | seq_ok
            if dil == 1:
                start = pl.multiple_of(start, ATT_BLK)
                rows = pl.ds(start, ATT_BLK)
                qt, k_src, v_src, k_rows = q_ref[rows, :], k1_ref, v1_ref, pl.ds(start, 2 * ATT_BLK)
            elif dil == fine:
                first = pl.multiple_of((idx // fine) * ATT_BLK, ATT_BLK)
                rows = pl.ds(start, ATT_BLK, stride=dil)
                qt = q4_ref[pl.ds((idx % fine) * nq + first, ATT_BLK), :]
                k_src, v_src = k4_ref, v4_ref
                k_rows = pl.ds((idx % fine) * nk + (nq - ATT_BLK) + first, 2 * ATT_BLK)
            else:
                rows = pl.ds(start, ATT_BLK, stride=dil)
                qt = q4_ref[pl.ds((idx % fine) * nq + idx // fine, ATT_BLK, stride=fine), :]
                k_src, v_src = k4_ref, v4_ref
                k_rows = pl.ds((idx % fine) * nk + idx // fine, 2 * ATT_BLK, stride=fine)
            kt = k_src[k_rows, :].astype(BF16)
            vt = v_src[k_rows, :].astype(BF16)
            q0 = jnp.where(head0, qt, 0.0)
            scores = [lax.dot_general(qh.astype(BF16), kt, (((1,), (1,)), ((), ())), preferred_element_type=F32)
                      for qh in (q0, qt - q0)]
            return rows, valid, vt, scores

        def softmax(valid, s, hh, bias=bias):
            s = jnp.where(valid, s + bias[hh], NEG_INF)
            m = jnp.max(s, axis=-1, keepdims=True)
            return m, jnp.exp(s - m).astype(BF16)

        def merge(rows, parts, pat=pat):
            (m0, o0), (m1, o1) = parts
            m_t = jnp.where(head0, m0, m1)
            o_t = jnp.where(head0, o0, o1)
            l_t = pltpu.roll(jnp.where(head0, o1, o0), ATT_HEAD_DIM, 1)
            if pat == 0:
                m_ref[rows, :] = m_t
                l_ref[rows, :] = l_t
                acc_ref[rows, :] = o_t
            else:
                m_old = m_ref[rows, :]
                m_new = jnp.maximum(m_old, m_t)
                a = jnp.exp(m_old - m_new)
                b = jnp.exp(m_t - m_new)
                m_ref[rows, :] = m_new
                l_ref[rows, :] = a * l_ref[rows, :] + b * l_t
                acc_ref[rows, :] = a * acc_ref[rows, :] + b * o_t

        def group(gi, carry):
            tiles = [tile(gi * ATT_GROUP + t, None) for t in range(ATT_GROUP)]
            soft = [[softmax(valid, s, hh) for hh, s in enumerate(scores)] for _, valid, _, scores in tiles]
            outs = [[(m, jnp.dot(p, jnp.where(own, vt, ones), preferred_element_type=F32))
                     for (m, p), own in zip(per_head, (head0_keys, ~head0_keys))]
                    for (_, _, vt, _), per_head in zip(tiles, soft)]
            for (rows, _, _, _), parts in zip(tiles, outs):
                merge(rows, parts)
            return carry

        lax.fori_loop(0, ATT_SB // ATT_BLK // ATT_GROUP, group, 0)

    o_ref[...] = (acc_ref[...] / l_ref[...]).astype(o_ref.dtype)


def _dilated_attention(q, k, v, bsz, s_len):
    assert DILATIONS == (DILATIONS[1] ** 2, DILATIONS[1], 1) and ATT_SB == ATT_BLK * DILATIONS[0]
    n_slab = q.shape[0]
    shape4 = (n_slab, bsz, s_len, LANES)
    q, k, v = (t.reshape(shape4) for t in (q, k, v))
    slopes = jnp.asarray([2.0 ** (-8.0 * (h + 1) / ATT_HEADS) for h in range(ATT_HEADS)], F32)
    blk = (None, None, ATT_SB, LANES)
    cur = pl.BlockSpec(blk, lambda b, j, i, s: (j, b, i, 0))
    prev = pl.BlockSpec(blk, lambda b, j, i, s: (j, b, jnp.maximum(i - 1, 0), 0))
    out = pl.pallas_call(
        _attn_kernel,
        grid_spec=pltpu.PrefetchScalarGridSpec(
            num_scalar_prefetch=1,
            grid=(bsz, n_slab, s_len // ATT_SB),
            in_specs=[cur, prev, cur, prev, cur],
            out_specs=cur,
            scratch_shapes=[pltpu.VMEM((ATT_SB, LANES), F32),
                            pltpu.VMEM((2 * ATT_SB, LANES), F32), pltpu.VMEM((2 * ATT_SB, LANES), F32),
                            pltpu.VMEM((ATT_BLK + ATT_SB, LANES), F32), pltpu.VMEM((ATT_BLK + ATT_SB, LANES), F32),
                            pltpu.VMEM((ATT_SB, LANES), F32), pltpu.VMEM((ATT_SB, LANES), F32),
                            pltpu.VMEM((ATT_SB, LANES), F32)]),
        out_shape=jax.ShapeDtypeStruct(shape4, BF16),
        compiler_params=_cparams(("parallel", "parallel", "parallel")),
        name="dilated_attn",
    )(slopes, q, k, k, v, v)
    return out.reshape(n_slab, bsz * s_len, LANES)


def _route_epilogue(h_parts, gain_ref, wr_ref, br_ref, tri_ref, cnt_ref, h_ref, hn_ref, ids_ref, gate_ref, cnt_out_ref):
    sub = h_parts[0].shape[0]
    tm = sub * len(h_parts)
    splits = []
    for g, h in enumerate(h_parts):
        h_ref[g * sub:(g + 1) * sub, :] = h
        hn = _rms(h, gain_ref[...])
        _store_row_tiles(hn_ref, hn, first=g * sub)
        hn_hi = hn.astype(BF16)
        splits.append((hn_hi, (hn - hn_hi.astype(F32)).astype(BF16)))
    logit_parts = []
    for hn_hi, hn_lo in splits:
        hi_part = jnp.dot(hn_hi, wr_ref[...], preferred_element_type=F32)
        logit_parts.append(hi_part[:, :LANES] + (hi_part[:, LANES:]
                                                 + jnp.dot(hn_lo, wr_ref[:, :LANES], preferred_element_type=F32)))
    logits = jnp.concatenate(logit_parts, axis=0) + br_ref[...]
    lane = lax.broadcasted_iota(jnp.int32, (tm, LANES), 1)
    big = jnp.int32(LANES)
    rmax = lambda t: jnp.max(t, axis=-1, keepdims=True)
    rmin = lambda t: jnp.min(t, axis=-1, keepdims=True)
    rsum = lambda t: jnp.sum(t, axis=-1, keepdims=True)
    gmask = lane < N_GROUPS
    gl = jnp.where(gmask, logits, -jnp.inf)
    gmax = rmax(gl)
    ge = jnp.where(gmask, jnp.exp(gl - gmax), 0.0)
    gprob = ge / rsum(ge)
    g_w = rmax(gprob)
    grp = rmin(jnp.where(gmask & (gprob == g_w), lane, big))
    group_of_lane = (lane - N_GROUPS) >> int(math.log2(EXPERTS_PER_GROUP))
    emask = (lane >= N_GROUPS) & (lane < N_GROUPS + N_EXPERTS) & (group_of_lane == grp)
    el = jnp.where(emask, logits, -jnp.inf)
    ee = jnp.where(emask, jnp.exp(el - rmax(el)), 0.0)
    ep = jnp.where(emask, ee / rsum(ee), -1.0)
    p1 = rmax(ep)
    i1 = rmin(jnp.where(ep == p1, lane, big))
    ep2 = jnp.where(lane == i1, -1.0, ep)
    p2 = rmax(ep2)
    i2 = rmin(jnp.where(ep2 == p2, lane, big))
    e1, e2 = i1 - N_GROUPS, i2 - N_GROUPS
    psum = p1 + p2
    gate1, gate2 = g_w * p1 / psum, g_w * p2 / psum
    oh1, oh2 = lane == e1, lane == e2
    member = (oh1 | oh2).astype(BF16)
    before = jnp.dot(tri_ref[...], member, preferred_element_type=F32) + cnt_ref[...]
    r1 = rsum(jnp.where(oh1, before, 0.0)).astype(jnp.int32)
    r2 = rsum(jnp.where(oh2, before, 0.0)).astype(jnp.int32)
    cnt_ref[...] = cnt_ref[...] + jnp.sum(member.astype(F32), axis=0, keepdims=True)
    ids = jnp.where(lane == 0, e1, jnp.where(lane == 1, e2, jnp.where(lane == 2, r1, jnp.where(lane == 3, r2, 0))))
    ids_ref[...] = jnp.transpose(ids)[0:SUBLANES, :]
    gate_ref[...] = jnp.where(lane == 0, gate1, jnp.where(lane == 1, gate2, 0.0))
    cnt_out_ref[...] = jnp.broadcast_to(cnt_ref[...], cnt_out_ref.shape)


def _router_operands(norm_gain, w_rg, b_rg, w_re, b_re):
    pad = LANES - N_GROUPS - N_EXPERTS
    wr = jnp.pad(jnp.concatenate([w_rg, w_re], axis=1).astype(F32), ((0, 0), (0, pad)))
    br = jnp.pad(jnp.concatenate([b_rg, b_re]).astype(F32), (0, pad))[None]
    r = jnp.arange(TOK_TILE)
    tri = (r[None, :] < r[:, None]).astype(BF16)
    wr_hi = wr.astype(BF16)
    wr_lo = (wr - wr_hi.astype(F32)).astype(BF16)
    return norm_gain[None].astype(F32), jnp.concatenate([wr_hi, wr_lo], axis=1), br, tri


def _route_specs(n_tok):
    full = lambda shape: pl.BlockSpec(shape, lambda i: (0,) * len(shape))
    in_specs = [full((1, D_MODEL)), full((D_MODEL, 2 * LANES)), full((1, LANES)), full((TOK_TILE, TOK_TILE))]
    tok = lambda w: pl.BlockSpec((TOK_TILE, w), lambda i: (i, 0))
    out_specs = [tok(D_MODEL), pl.BlockSpec((TOK_TILE * ROW_CHUNKS, LANES), lambda i: (i, 0)),
                 pl.BlockSpec((SUBLANES, TOK_TILE), lambda i: (0, i)), tok(LANES), full((SUBLANES, LANES))]
    out_shape = [jax.ShapeDtypeStruct((n_tok, D_MODEL), F32), jax.ShapeDtypeStruct((n_tok * ROW_CHUNKS, LANES), jnp.uint32),
                 jax.ShapeDtypeStruct((SUBLANES, n_tok), jnp.int32), jax.ShapeDtypeStruct((n_tok, LANES), F32),
                 jax.ShapeDtypeStruct((SUBLANES, LANES), F32)]
    return in_specs, out_specs, out_shape


def _gelu_tanh(x):
    return 0.5 * x * (1.0 + jnp.tanh(math.sqrt(2.0 / math.pi) * (x + 0.044715 * (x * x * x))))


def _outproj_even_kernel(x_ref, y_ref, a_ref, wglu_ref, bglu_ref, wout_ref, gain_ref, wr_ref, br_ref, tri_ref,
                         h_ref, hn_ref, ids_ref, gate_ref, cnt_out_ref, cnt_ref):
    @pl.when(pl.program_id(0) == 0)
    def _():
        cnt_ref[...] = jnp.zeros_like(cnt_ref)

    half = x_ref.shape[0] // ROW_GROUPS
    parts = [slice(g * half, (g + 1) * half) for g in range(ROW_GROUPS)]
    y = [_gelu_tanh(jnp.concatenate([y_ref[j, rows, :] for j in range(SSM_SLABS)], axis=1)) for rows in parts]
    glu = [jnp.dot(t.astype(BF16), wglu_ref[...], preferred_element_type=F32) for t in y]
    h_out = []
    for rows, t, g in zip(parts, y, glu):
        t = t * jax.nn.sigmoid(g + bglu_ref[...])
        mix = jnp.dot(t.astype(BF16), wout_ref[0:SSM_WIDTH, :], preferred_element_type=F32)
        for j in range(ATT_WIDTH // LANES):
            w_rows = slice(SSM_WIDTH + j * LANES, SSM_WIDTH + (j + 1) * LANES)
            mix = mix + jnp.dot(a_ref[j, rows, :], wout_ref[w_rows, :], preferred_element_type=F32)
        h_out.append(x_ref[rows, :] + mix)
    _route_epilogue(h_out, gain_ref, wr_ref, br_ref, tri_ref, cnt_ref,
                    h_ref, hn_ref, ids_ref, gate_ref, cnt_out_ref)


def _outproj_even(x2, y_pre, attn, w_glu, b_glu, w_out, route_ops):
    n_tok = x2.shape[0]
    n_slab = attn.shape[0]
    r_in, r_out, r_shape = _route_specs(n_tok)
    full = lambda shape: pl.BlockSpec(shape, lambda i: (0,) * len(shape))
    return pl.pallas_call(
        _outproj_even_kernel,
        grid=(n_tok // TOK_TILE,),
        in_specs=[pl.BlockSpec((TOK_TILE, D_MODEL), lambda i: (i, 0)),
                  pl.BlockSpec((SSM_SLABS, TOK_TILE, LANES), lambda i: (0, i, 0)),
                  pl.BlockSpec((n_slab, TOK_TILE, LANES), lambda i: (0, i, 0)),
                  full(w_glu.shape), full((1, SSM_WIDTH)), full(w_out.shape)] + r_in,
        out_specs=r_out, out_shape=r_shape,
        scratch_shapes=[pltpu.VMEM((1, LANES), F32)],
        compiler_params=_cparams(("arbitrary",)),
        name="outproj_even",
    )(x2, y_pre, attn, w_glu.astype(BF16), b_glu[None].astype(F32), w_out.astype(BF16), *route_ops)


def _conv_layer_kernel(h_ref, pgate_ref, y0_ref, y1_ref, gmix_ref, win_ref, cw_ref, wout_ref, gain_ref, wr_ref, br_ref,
                       tri_ref, ho_ref, hn_ref, ids_ref, gate_ref, cnt_out_ref, cnt_ref, zc_ref, *, tiles_per_seq):
    i = pl.program_id(0)

    @pl.when(i == 0)
    def _():
        cnt_ref[...] = jnp.zeros_like(cnt_ref)

    @pl.when(i % tiles_per_seq == 0)
    def _():
        zc_ref[0:SUBLANES, :] = jnp.zeros((SUBLANES, D_MODEL), F32)

    tm = h_ref.shape[0]
    c = D_MODEL
    half = tm // ROW_GROUPS
    parts = tuple(g * half for g in range(ROW_GROUPS))
    h_in, hn = {}, {}
    for r0 in parts:
        rows = slice(r0, r0 + half)
        pgate = pgate_ref[rows, :]
        h_in[r0] = (h_ref[rows, :] + pgate[:, 0:1] * _load_row_tiles(y0_ref, first=r0, n=half)
                    + pgate[:, 1:2] * _load_row_tiles(y1_ref, first=r0, n=half))
        hn[r0] = _rms(h_in[r0], gmix_ref[...]).astype(BF16)
    b_gate, zc = {}, {}
    for r0 in parts:
        b_gate[r0] = jnp.dot(hn[r0], win_ref[:, 0:c], preferred_element_type=F32)
        zc[r0] = (jnp.dot(hn[r0], win_ref[:, c:2 * c], preferred_element_type=F32)
                  * jnp.dot(hn[r0], win_ref[:, 2 * c:3 * c], preferred_element_type=F32))
    h_out = []
    for r0 in parts:
        z0 = SUBLANES + r0
        zc_ref[z0:z0 + half, :] = zc[r0]
        conv = cw_ref[CONV_TAPS - 1:CONV_TAPS, :] * zc[r0]
        for back in range(1, CONV_TAPS):
            tap = CONV_TAPS - 1 - back
            conv = conv + cw_ref[tap:tap + 1, :] * zc_ref[z0 - back:z0 - back + half, :]
        mix = jnp.dot((b_gate[r0] * conv).astype(BF16), wout_ref[...], preferred_element_type=F32)
        h_out.append(h_in[r0] + mix)
    zc_ref[0:SUBLANES, :] = zc_ref[tm:tm + SUBLANES, :]
    _route_epilogue(h_out, gain_ref, wr_ref, br_ref, tri_ref, cnt_ref,
                    ho_ref, hn_ref, ids_ref, gate_ref, cnt_out_ref)


def _conv_layer(h1, pgate, y2, gain_mix, w_in, conv_w, w_out, route_ops, s_len):
    n_tok = h1.shape[0]
    r_in, r_out, r_shape = _route_specs(n_tok)
    full = lambda shape: pl.BlockSpec(shape, lambda i: (0,) * len(shape))
    return pl.pallas_call(
        functools.partial(_conv_layer_kernel, tiles_per_seq=s_len // TOK_TILE),
        grid=(n_tok // TOK_TILE,),
        in_specs=_moe_specs(n_tok) + [full((1, D_MODEL)), full(w_in.shape), full(conv_w.shape), full(w_out.shape)] + r_in,
        out_specs=r_out, out_shape=r_shape,
        scratch_shapes=[pltpu.VMEM((1, LANES), F32), pltpu.VMEM((TOK_TILE + SUBLANES, D_MODEL), F32)],
        compiler_params=_cparams(("arbitrary",)),
        name="conv_layer",
    )(h1, pgate, y2, y2, gain_mix[None].astype(F32), w_in.astype(BF16), conv_w.astype(F32), w_out.astype(BF16),
      *route_ops)


def _row_map(experts, ranks, first_row, n_rows):
    n_slot, n_tok = len(experts), experts[0].shape[0]
    n_assign = n_slot * n_tok
    n_src = n_rows + 2 * MOE_BLK
    mesh = plsc.VectorSubcoreMesh(core_axis_name="core", subcore_axis_name="subcore",
                                  num_cores=SC_CORES, num_subcores=SC_SUBCORES)
    tok_vec = pltpu.VMEM((n_tok,), jnp.int32)

    @functools.partial(
        pl.kernel, mesh=mesh, out_type=jax.ShapeDtypeStruct((n_src,), jnp.int32),
        scratch_types=[tok_vec] * (2 * n_slot) + [pltpu.VMEM((N_EXPERTS,), jnp.int32), pltpu.VMEM((n_src,), jnp.int32)],
        compiler_params=pltpu.CompilerParams(needs_layout_passes=False), name="moe_row_map")
    def row_map(*refs):
        ins, src_hbm, scratch = refs[:2 * n_slot + 2], refs[2 * n_slot + 2], refs[2 * n_slot + 3:]
        first_v, src_v = scratch[2 * n_slot], scratch[2 * n_slot + 1]

        @pl.when(jnp.logical_and(lax.axis_index("core") == 0, lax.axis_index("subcore") == 0))
        def _():
            for hbm, vmem in zip(ins, scratch):
                pltpu.sync_copy(hbm, vmem)
            lanes = lax.iota(jnp.int32, SC_LANES)

            for slot in range(n_slot):
                e_v, r_v = scratch[slot], scratch[n_slot + slot]

                @pl.loop(0, n_tok, step=SC_LANES)
                def _(t):
                    row = plsc.load_gather(first_v, [e_v[pl.ds(t, SC_LANES)]]) + r_v[pl.ds(t, SC_LANES)]
                    plsc.store_scatter(src_v, [row + MOE_BLK], slot * n_tok + t + lanes)

            pltpu.sync_copy(src_v, src_hbm)

    dump_rows = n_assign + (jnp.arange(n_src, dtype=jnp.int32) & (MOE_BLK - 1))
    return row_map(*experts, *ranks, first_row, dump_rows)


def _expert_kernel(blk_e_ref, n_used_ref, src_ref, next_e_ref, wslot_ref, hn_ref, wg_ref, wu_ref, wd_ref, y2_ref,
                   xbuf, ybuf, wg_f, wu_f, wd_f, wg_s, wu_s, wd_s, ssem, wsem, *, n_tok, layer):
    b = pl.program_id(0)
    n_used = n_used_ref[0]
    cur = b % 2
    nxt = 1 - cur

    tile = lambda i: pl.ds(i * ROW_CHUNKS, ROW_CHUNKS)

    def scatter(blk, slot, i):
        row = src_ref[(blk + 1) * MOE_BLK + i]
        return pltpu.make_async_copy(ybuf.at[slot, tile(i)], y2_ref.at[row], ssem.at[slot])

    wait_block = lambda slot: pltpu.make_async_copy(ybuf.at[slot], ybuf.at[slot], ssem.at[slot]).wait()

    def weights(expert, slot, act):
        for hbm, buf in ((wg_ref, wg_f), (wu_ref, wu_f), (wd_ref, wd_f)):
            act(pltpu.make_async_copy(hbm.at[layer, expert], buf.at[slot], wsem.at[slot]))

    @pl.when(b == 0)
    def _():
        ybuf[1] = jnp.zeros(ybuf.shape[1:], ybuf.dtype)
        weights(blk_e_ref[0], wslot_ref[0], lambda c: c.start())

    @pl.when(b < n_used)
    def _():
        new_expert = jnp.logical_or(b == 0, blk_e_ref[b] != blk_e_ref[jnp.maximum(b - 1, 0)])

        @pl.when(new_expert)
        def _():
            slot = wslot_ref[b]
            weights(blk_e_ref[b], slot, lambda c: c.wait())
            wg_s[...] = wg_f[slot].astype(BF16)
            wu_s[...] = wu_f[slot].astype(BF16)
            wd_s[...] = wd_f[slot].astype(BF16)

            @pl.when(next_e_ref[b] >= 0)
            def _():
                weights(next_e_ref[b], 1 - slot, lambda c: c.start())

        @pl.when(b >= 1)
        def _():
            wait_block(cur)

        for i in range(MOE_BLK):
            tok = src_ref[(b + 1) * MOE_BLK + i] & (n_tok - 1)
            xbuf[tile(i), :] = hn_ref[pl.ds(pl.multiple_of(tok * ROW_CHUNKS, ROW_CHUNKS), ROW_CHUNKS), :]
            scatter(b - 1, nxt, i).start(priority=i % 2)
        x = _load_row_tiles(xbuf).astype(BF16)
        g = jnp.dot(x, wg_s[...], preferred_element_type=F32)
        u = jnp.dot(x, wu_s[...], preferred_element_type=F32)
        hb = (g * jax.nn.sigmoid(g) * u).astype(BF16)
        _store_row_tiles(ybuf, jnp.dot(hb, wd_s[...], preferred_element_type=F32), (cur,))

    @pl.when(b == n_used)
    def _():
        wait_block(cur)
        for i in range(MOE_BLK):
            scatter(b - 1, nxt, i).start(priority=i % 2)
        wait_block(nxt)


def _expert_ffn(hn, src, blk_e, n_used, next_e, wslot, w_g, w_u, w_d, layer):
    n_tok = hn.shape[0] // ROW_CHUNKS
    assert n_tok & (n_tok - 1) == 0, "dump-row aliasing masks the token index with T - 1"
    n_blocks = (src.shape[0] - 2 * MOE_BLK) // MOE_BLK
    in_hbm = pl.BlockSpec(memory_space=pltpu.HBM)
    return pl.pallas_call(
        functools.partial(_expert_kernel, n_tok=n_tok, layer=layer),
        grid_spec=pltpu.PrefetchScalarGridSpec(
            num_scalar_prefetch=5,
            grid=(n_blocks + 1,),
            in_specs=[pl.BlockSpec(hn.shape, lambda b, *_: (0, 0), pipeline_mode=pl.Buffered(1)),
                      in_hbm, in_hbm, in_hbm],
            out_specs=pl.BlockSpec(memory_space=pltpu.HBM),
            scratch_shapes=[pltpu.VMEM((MOE_BLK * ROW_CHUNKS, LANES), jnp.uint32),
                            pltpu.VMEM((2, MOE_BLK * ROW_CHUNKS, LANES), jnp.uint32),
                            pltpu.VMEM((2, D_MODEL, D_EXPERT), F32), pltpu.VMEM((2, D_MODEL, D_EXPERT), F32),
                            pltpu.VMEM((2, D_EXPERT, D_MODEL), F32),
                            pltpu.VMEM((D_MODEL, D_EXPERT), BF16), pltpu.VMEM((D_MODEL, D_EXPERT), BF16),
                            pltpu.VMEM((D_EXPERT, D_MODEL), BF16),
                            pltpu.SemaphoreType.DMA((2,)), pltpu.SemaphoreType.DMA((2,))]),
        out_shape=jax.ShapeDtypeStruct((2 * n_tok + MOE_BLK, ROW_CHUNKS, LANES), jnp.uint32),
        compiler_params=_cparams(("arbitrary",)),
        name="moe_experts",
    )(blk_e, n_used, src, next_e, wslot, hn, w_g, w_u, w_d)


def _moe_add_kernel(h_ref, gate_ref, y0_ref, y1_ref, o_ref):
    gate = gate_ref[...]
    o_ref[...] = h_ref[...] + gate[:, 0:1] * _load_row_tiles(y0_ref) + gate[:, 1:2] * _load_row_tiles(y1_ref)


def _moe_specs(n_tok):
    slots = n_tok // TOK_TILE
    y2_rows = (TOK_TILE * ROW_CHUNKS, LANES)
    return [pl.BlockSpec((TOK_TILE, D_MODEL), lambda i: (i, 0)), pl.BlockSpec((TOK_TILE, LANES), lambda i: (i, 0)),
            pl.BlockSpec(y2_rows, lambda i: (i, 0)), pl.BlockSpec(y2_rows, lambda i: (slots + i, 0))]


def _moe_add(h, gate, y2):
    n_tok = h.shape[0]
    return pl.pallas_call(
        _moe_add_kernel,
        grid=(n_tok // TOK_TILE,),
        in_specs=_moe_specs(n_tok),
        out_specs=pl.BlockSpec((TOK_TILE, D_MODEL), lambda i: (i, 0)),
        out_shape=jax.ShapeDtypeStruct((n_tok, D_MODEL), F32),
        compiler_params=_cparams(("parallel",)),
        name="moe_add",
    )(h, gate, y2, y2)


def _moe(hn, ids, counts, w_g, w_u, w_d, layer):
    n_tok = hn.shape[0] // ROW_CHUNKS
    n_assign = 2 * n_tok
    n_blocks = n_assign // MOE_BLK + N_EXPERTS
    n_rows = n_blocks * MOE_BLK
    cnt = counts[0, :N_EXPERTS].astype(jnp.int32)
    padded = (cnt + MOE_BLK - 1) // MOE_BLK * MOE_BLK
    pends = jnp.cumsum(padded).astype(jnp.int32)
    pstarts = pends - padded
    blk_start = jnp.arange(n_blocks + 1, dtype=jnp.int32) * MOE_BLK
    blk_e = jnp.minimum(jnp.sum(pends[None, :] <= blk_start[:, None], axis=1), N_EXPERTS - 1).astype(jnp.int32)
    n_used = (pends[-1:] // MOE_BLK).astype(jnp.int32)
    used = (padded > 0)[None, :]
    e_idx = jnp.arange(N_EXPERTS, dtype=jnp.int32)[None, :]
    next_e = jnp.min(jnp.where(used & (e_idx > blk_e[:, None]), e_idx, N_EXPERTS), axis=1)
    next_e = jnp.where(next_e < N_EXPERTS, next_e, -1).astype(jnp.int32)
    ordinal = jnp.sum((used & (e_idx <= blk_e[:, None])).astype(jnp.int32), axis=1) - 1
    src = _row_map((ids[0], ids[1]), (ids[2], ids[3]), pstarts, n_rows)
    y2 = _expert_ffn(hn, src, blk_e, n_used, next_e, (ordinal % 2).astype(jnp.int32), w_g, w_u, w_d, layer)
    return y2.reshape(-1, LANES)


def kernel(x, norm_mix, norm_ffn, w_in_even, ssm_a_re, ssm_a_im, ssm_b_re, ssm_b_im, ssm_c_re, ssm_c_im, ssm_d,
           ssm_log_step, w_glu, b_glu, q_norm, k_norm, w_out_even, w_in_conv, conv_w, w_out_conv, w_router_group,
           b_router_group, w_router_expert, b_router_expert, w_expert_gate, w_expert_up, w_expert_down):
    bsz, s_len, d = x.shape
    x2 = x.reshape(bsz * s_len, d)
    route = lambda layer: _router_operands(norm_ffn[layer], w_router_group[layer], b_router_group[layer],
                                           w_router_expert[layer], b_router_expert[layer])
    experts = lambda layer: (w_expert_gate, w_expert_up, w_expert_down, layer)

    u, q, k, v = _inproj_even(x2, norm_mix[0], w_in_even[0], q_norm[0], k_norm[0])
    tables = _s5_tables(ssm_a_re[0], ssm_a_im[0], ssm_b_re[0], ssm_b_im[0], ssm_c_re[0], ssm_c_im[0], ssm_d[0],
                        ssm_log_step[0])
    y_pre = _s5_core(u, tables, bsz, s_len)
    attn = _dilated_attention(q, k, v, bsz, s_len)
    h, hn, ids, gate, counts = _outproj_even(x2, y_pre, attn, w_glu[0], b_glu[0], w_out_even[0], route(0))
    y2 = _moe(hn, ids, counts, *experts(0))

    h, hn, ids, gate, counts = _conv_layer(h, gate, y2, norm_mix[1], w_in_conv[0], conv_w[0], w_out_conv[0],
                                           route(1), s_len)
    y2 = _moe(hn, ids, counts, *experts(1))
    return _moe_add(h, gate, y2).reshape(bsz, s_len, d)
```
